```python
import math
import jax, jax.numpy as jnp
from jax import lax
import numpy as np

D_MODEL = 1024
BATCH = 4
SEQ = 4096
DEPTH = 1
DEC_BATCH = 128
DEC_SEQ = 1
PAST_LEN = 8192
PAGE_SIZE = 128

HEAD_DIM = 64
N_Q_HEADS = 8
N_KV_HEADS = 2
Q_PER_KV = N_Q_HEADS // N_KV_HEADS
D_ATTN = N_Q_HEADS * HEAD_DIM
D_KV = N_KV_HEADS * HEAD_DIM
WINDOW = 128
ATT_BLOCK = WINDOW
W_BUF = min(WINDOW, PAST_LEN)
ATTN_SCALE = HEAD_DIM ** -0.5
SSM_GROUP = 16
D_SSM = D_MODEL // 2
N_SSM_GROUPS = D_SSM // SSM_GROUP
SSM_STATE = 64
DT_MIN = 0.001
DT_MAX = 0.1
D_IN = D_ATTN + 2 * D_KV + D_SSM
N_EXPERTS = 32
TOP_K = 4
D_FF = D_MODEL
SWIGLU_LIMIT = 7.0
SWIGLU_ALPHA = 1.702
MOE_BLOCK = 128
LN_EPS = 1e-5
DEEPNORM_ALPHA = (2 * DEPTH) ** 0.25
DEEPNORM_BETA = (8 * DEPTH) ** -0.25

kernel_name = "hybrid_swa_s5_moe_decode_step"


def _layer_norm(x, g, b):
    xf = x.astype(jnp.float32)
    mu = jnp.mean(xf, axis=-1, keepdims=True)
    var = jnp.mean(jnp.square(xf - mu), axis=-1, keepdims=True)
    y = (xf - mu) * lax.rsqrt(var + LN_EPS) * g.astype(jnp.float32) + b.astype(jnp.float32)
    return y.astype(x.dtype)


def _sink_probs(scores, mask, sink):
    s = jnp.where(mask, scores, -jnp.inf)
    m = jnp.maximum(jnp.max(s, axis=-1, keepdims=True), sink)
    p = jnp.exp(s - m)
    return p / (jnp.sum(p, axis=-1, keepdims=True) + jnp.exp(sink - m))


def _window_attention_prompt(q, k, v, sinks):
    bsz, seq = q.shape[0], q.shape[1]
    nb = seq // ATT_BLOCK
    qb = q.astype(jnp.float32).reshape(bsz, nb, ATT_BLOCK, N_KV_HEADS, Q_PER_KV, HEAD_DIM)

    def band(t):
        t = t.astype(jnp.float32)
        t_prev = jnp.concatenate([jnp.zeros_like(t[:, :ATT_BLOCK]), t[:, :seq - ATT_BLOCK]], axis=1)
        cur = t.reshape(bsz, nb, ATT_BLOCK, N_KV_HEADS, HEAD_DIM)
        prev = t_prev.reshape(bsz, nb, ATT_BLOCK, N_KV_HEADS, HEAD_DIM)
        return jnp.concatenate([prev, cur], axis=2)

    kb, vb = band(k), band(v)
    scores = jnp.einsum("bnqkgd,bnskd->bnkgqs", qb, kb) * ATTN_SCALE
    q_off = jnp.arange(ATT_BLOCK)[:, None] + ATT_BLOCK
    s_off = jnp.arange(2 * ATT_BLOCK)[None, :]
    rel = q_off - s_off
    kpos = (jnp.arange(nb)[:, None, None] - 1) * ATT_BLOCK + s_off[None]
    mask = (rel >= 0) & (rel <= WINDOW) & (kpos >= 0)
    sink = sinks.astype(jnp.float32).reshape(N_KV_HEADS, Q_PER_KV, 1, 1)
    p = _sink_probs(scores, mask[None, :, None, None], sink)
    o = jnp.einsum("bnkgqs,bnskd->bnqkgd", p, vb)
    return o.reshape(bsz, seq, D_ATTN).astype(q.dtype)


def _window_attention_sample(q, k, v, k_buf, v_buf, sinks):
    bsz, s_new = q.shape[0], q.shape[1]
    kk = jnp.concatenate([k_buf.astype(k.dtype), k], axis=1)
    vv = jnp.concatenate([v_buf.astype(v.dtype), v], axis=1)
    qg = q.astype(jnp.float32).reshape(bsz, s_new, N_KV_HEADS, Q_PER_KV, HEAD_DIM)
    scores = jnp.einsum("bqkgd,bskd->bkgqs", qg, kk.astype(jnp.float32)) * ATTN_SCALE
    q_pos = PAST_LEN + jnp.arange(s_new)
    k_pos = jnp.concatenate([PAST_LEN - W_BUF + jnp.arange(W_BUF), PAST_LEN + jnp.arange(s_new)])
    rel = q_pos[:, None] - k_pos[None, :]
    mask = (rel >= 0) & (rel <= WINDOW)
    sink = sinks.astype(jnp.float32).reshape(N_KV_HEADS, Q_PER_KV, 1, 1)
    p = _sink_probs(scores, mask, sink)
    o = jnp.einsum("bkgqs,bskd->bqkgd", p, vv.astype(jnp.float32))
    return o.reshape(bsz, s_new, D_ATTN).astype(q.dtype), kk[:, -W_BUF:], vv[:, -W_BUF:]


def _s5_params(a_re, a_im, log_dt, b_re, b_im, c_re, c_im):
    f = jnp.float32
    a = lax.complex(a_re.astype(f), a_im.astype(f))
    dt = jnp.exp(log_dt.astype(f))[:, None]
    a_bar = jnp.exp(a * dt)
    b_bar = ((a_bar - 1.0) / a)[..., None] * lax.complex(b_re.astype(f), b_im.astype(f))
    c = lax.complex(c_re.astype(f), c_im.astype(f))
    return a_bar, b_bar, c


def _s5_scan(u, h0, a_bar, b_bar, c, d_skip):
    uf = u.astype(jnp.float32)
    bu = jnp.einsum("blgc,gpc->blgp", uf.astype(jnp.complex64), b_bar)
    bu = bu.at[:, 0].add(a_bar * h0)

    def combine(left, right):
        a_l, b_l = left
        a_r, b_r = right
        return a_l * a_r, a_r * b_l + b_r

    _, h = lax.associative_scan(combine, (jnp.broadcast_to(a_bar, bu.shape), bu), axis=1)
    y = jnp.real(jnp.einsum("blgp,gcp->blgc", h, c)) \
        + d_skip.astype(jnp.float32).reshape(N_SSM_GROUPS, SSM_GROUP) * uf
    return y.astype(u.dtype), h[:, -1]


def _clamped_swiglu(h):
    x_glu = jnp.minimum(h[..., ::2], SWIGLU_LIMIT)
    x_lin = jnp.clip(h[..., 1::2], -SWIGLU_LIMIT, SWIGLU_LIMIT)
    return x_glu * jax.nn.sigmoid(SWIGLU_ALPHA * x_glu) * (x_lin + 1.0)


def _moe_ffn(x, w_router, b_router, w_exp1, b_exp1, w_exp2, b_exp2):
    lead = x.shape[:-1]
    xt = x.reshape(-1, D_MODEL)
    n_tok = xt.shape[0]
    logits = xt.astype(jnp.float32) @ w_router.astype(jnp.float32) + b_router.astype(jnp.float32)
    top_vals, top_idx = lax.top_k(logits, TOP_K)
    gate = jax.nn.softmax(top_vals, axis=-1)
    n_assign = n_tok * TOP_K
    flat_e = top_idx.reshape(-1)
    flat_tok = jnp.repeat(jnp.arange(n_tok, dtype=jnp.int32), TOP_K)
    flat_w = gate.reshape(-1)
    order = jnp.argsort(flat_e)
    e_sorted = flat_e[order]
    counts = jnp.bincount(flat_e, length=N_EXPERTS)
    start = jnp.cumsum(counts) - counts
    padded = ((counts + MOE_BLOCK - 1) // MOE_BLOCK) * MOE_BLOCK
    pad_end = jnp.cumsum(padded)
    pad_start = pad_end - padded
    dest = pad_start[e_sorted] + (jnp.arange(n_assign) - start[e_sorted])
    n_blocks = (n_assign + N_EXPERTS * (MOE_BLOCK - 1) + MOE_BLOCK - 1) // MOE_BLOCK
    n_rows = n_blocks * MOE_BLOCK
    row_tok = jnp.full((n_rows,), n_tok, jnp.int32).at[dest].set(flat_tok[order])
    row_w = jnp.zeros((n_rows,), jnp.float32).at[dest].set(flat_w[order])
    block_start = jnp.arange(n_blocks) * MOE_BLOCK
    block_e = jnp.minimum(jnp.sum(block_start[:, None] >= pad_end[None, :], axis=1), N_EXPERTS - 1)
    x_pad = jnp.concatenate([xt, jnp.zeros((1, D_MODEL), xt.dtype)], axis=0)
    xb = x_pad[row_tok].reshape(n_blocks, MOE_BLOCK, D_MODEL)

    def expert_block(args):
        xblk, e = args
        h = _clamped_swiglu(xblk @ w_exp1[e] + b_exp1[e])
        return h @ w_exp2[e] + b_exp2[e]

    yb = lax.map(expert_block, (xb, block_e))
    y_rows = yb.reshape(n_rows, D_MODEL) * row_w[:, None].astype(yb.dtype)
    y = jax.ops.segment_sum(y_rows, row_tok, num_segments=n_tok + 1)[:n_tok]
    return y.reshape(*lead, D_MODEL).astype(x.dtype)


def _decoder_layer(x_p, x_s, k_buf, v_buf, s_re, s_im,
                   w_in, b_in, attn_sinks, w_attn_out, ssm_a_re, ssm_a_im, ssm_log_dt,
                   ssm_b_re, ssm_b_im, ssm_c_re, ssm_c_im, ssm_d, w_ssm_out,
                   w_gate, b_gate, w_out, ln1_g, ln1_b,
                   w_router, b_router, w_exp1, b_exp1, w_exp2, b_exp2, ln2_g, ln2_b):
    a_bar, b_bar, c_mat = _s5_params(ssm_a_re, ssm_a_im, ssm_log_dt, ssm_b_re, ssm_b_im, ssm_c_re, ssm_c_im)

    def mixer_inputs(x):
        bsz, s = x.shape[0], x.shape[1]
        h = x @ w_in + b_in
        q = h[..., :D_ATTN].reshape(bsz, s, N_Q_HEADS, HEAD_DIM)
        k = h[..., D_ATTN:D_ATTN + D_KV].reshape(bsz, s, N_KV_HEADS, HEAD_DIM)
        v = h[..., D_ATTN + D_KV:D_ATTN + 2 * D_KV].reshape(bsz, s, N_KV_HEADS, HEAD_DIM)
        u = h[..., D_ATTN + 2 * D_KV:].reshape(bsz, s, N_SSM_GROUPS, SSM_GROUP)
        return q, k, v, u

    def merge_and_channel(x, o_attn, y_ssm):
        bsz, s = x.shape[0], x.shape[1]
        branch_a = o_attn @ w_attn_out
        z = jax.nn.gelu(y_ssm.reshape(bsz, s, D_SSM)) @ w_ssm_out
        branch_b = z[..., :D_MODEL] * jax.nn.sigmoid(z[..., D_MODEL:])
        g = jax.nn.sigmoid(x @ w_gate + b_gate)
        mix = (g[..., :D_MODEL] * branch_a + g[..., D_MODEL:] * branch_b) @ w_out
        x1 = _layer_norm(DEEPNORM_ALPHA * x + mix, ln1_g, ln1_b)
        ffn = _moe_ffn(x1, w_router, b_router, w_exp1, b_exp1, w_exp2, b_exp2)
        return _layer_norm(DEEPNORM_ALPHA * x1 + ffn, ln2_g, ln2_b)

    q, k, v, u = mixer_inputs(x_p)
    o_attn = _window_attention_prompt(q, k, v, attn_sinks)
    h0 = jnp.zeros((x_p.shape[0], N_SSM_GROUPS, SSM_STATE), jnp.complex64)
    y_ssm, h_p = _s5_scan(u, h0, a_bar, b_bar, c_mat, ssm_d)
    y_p = merge_and_channel(x_p, o_attn, y_ssm)
    k_p, v_p = k[:, -W_BUF:], v[:, -W_BUF:]

    q, k, v, u = mixer_inputs(x_s)
    o_attn, k_s, v_s = _window_attention_sample(q, k, v, k_buf, v_buf, attn_sinks)
    h0 = lax.complex(s_re.astype(jnp.float32), s_im.astype(jnp.float32))
    y_ssm, h_s = _s5_scan(u, h0, a_bar, b_bar, c_mat, ssm_d)
    y_s = merge_and_channel(x_s, o_attn, y_ssm)
    return (y_p, y_s, k_p, v_p, jnp.real(h_p), jnp.imag(h_p), k_s, v_s, jnp.real(h_s), jnp.imag(h_s))


def setup_inputs(seed: int = 0) -> dict:
    key = jax.random.key(seed)
    ks = iter(jax.random.split(key, 40))
    f = jnp.float32
    L = DEPTH

    def nrm(shape, scale):
        return jax.random.normal(next(ks), shape, f) * scale

    col_scale = jnp.concatenate([jnp.ones((D_ATTN + D_KV,), f), jnp.full((D_KV,), DEEPNORM_BETA, f),
                                 jnp.ones((D_SSM,), f)])
    return {
        "x_prompt": nrm((BATCH, SEQ, D_MODEL), 1.0),
        "x_sample": nrm((DEC_BATCH, DEC_SEQ, D_MODEL), 1.0),
        "cache_k_win": nrm((L, DEC_BATCH, W_BUF, N_KV_HEADS, HEAD_DIM), 1.0),
        "cache_v_win": nrm((L, DEC_BATCH, W_BUF, N_KV_HEADS, HEAD_DIM), 0.7),
        "state_ssm_re": nrm((L, DEC_BATCH, N_SSM_GROUPS, SSM_STATE), 0.1),
        "state_ssm_im": nrm((L, DEC_BATCH, N_SSM_GROUPS, SSM_STATE), 0.1),
        "w_in": nrm((L, D_MODEL, D_IN), D_MODEL ** -0.5) * col_scale,
        "b_in": nrm((L, D_IN), 0.02),
        "attn_sinks": nrm((L, N_Q_HEADS), 0.5),
        "w_attn_out": nrm((L, D_ATTN, D_MODEL), DEEPNORM_BETA * D_ATTN ** -0.5),
        "ssm_a_re": -0.5 + nrm((L, N_SSM_GROUPS, SSM_STATE), 0.01),
        "ssm_a_im": jnp.pi * jnp.arange(SSM_STATE, dtype=f) + nrm((L, N_SSM_GROUPS, SSM_STATE), 0.01),
        "ssm_log_dt": jax.random.uniform(next(ks), (L, N_SSM_GROUPS), f, math.log(DT_MIN), math.log(DT_MAX)),
        "ssm_b_re": nrm((L, N_SSM_GROUPS, SSM_STATE, SSM_GROUP), (2 * SSM_GROUP) ** -0.5),
        "ssm_b_im": nrm((L, N_SSM_GROUPS, SSM_STATE, SSM_GROUP), (2 * SSM_GROUP) ** -0.5),
        "ssm_c_re": nrm((L, N_SSM_GROUPS, SSM_GROUP, SSM_STATE), (2 * SSM_STATE) ** -0.5),
        "ssm_c_im": nrm((L, N_SSM_GROUPS, SSM_GROUP, SSM_STATE), (2 * SSM_STATE) ** -0.5),
        "ssm_d": nrm((L, D_SSM), 1.0),
        "w_ssm_out": nrm((L, D_SSM, 2 * D_MODEL), DEEPNORM_BETA * D_SSM ** -0.5),
        "w_gate": nrm((L, D_MODEL, 2 * D_MODEL), D_MODEL ** -0.5),
        "b_gate": nrm((L, 2 * D_MODEL), 0.02),
        "w_out": nrm((L, D_MODEL, D_MODEL), DEEPNORM_BETA * D_MODEL ** -0.5),
        "ln1_g": 1.0 + nrm((L, D_MODEL), 0.02),
        "ln1_b": nrm((L, D_MODEL), 0.02),
        "w_router": nrm((L, D_MODEL, N_EXPERTS), D_MODEL ** -0.5),
        "b_router": nrm((L, N_EXPERTS), 0.01),
        "w_exp1": nrm((L, N_EXPERTS, D_MODEL, 2 * D_FF), DEEPNORM_BETA * D_MODEL ** -0.5),
        "b_exp1": nrm((L, N_EXPERTS, 2 * D_FF), 0.02),
        "w_exp2": nrm((L, N_EXPERTS, D_FF, D_MODEL), DEEPNORM_BETA * D_FF ** -0.5),
        "b_exp2": nrm((L, N_EXPERTS, D_MODEL), 0.02),
        "ln2_g": 1.0 + nrm((L, D_MODEL), 0.02),
        "ln2_b": nrm((L, D_MODEL), 0.02),
    }


def reference(x_prompt, x_sample, cache_k_win, cache_v_win, state_ssm_re, state_ssm_im,
              w_in, b_in, attn_sinks, w_attn_out, ssm_a_re, ssm_a_im, ssm_log_dt,
              ssm_b_re, ssm_b_im, ssm_c_re, ssm_c_im, ssm_d, w_ssm_out,
              w_gate, b_gate, w_out, ln1_g, ln1_b,
              w_router, b_router, w_exp1, b_exp1, w_exp2, b_exp2, ln2_g, ln2_b):
    weights = (w_in, b_in, attn_sinks, w_attn_out, ssm_a_re, ssm_a_im, ssm_log_dt,
               ssm_b_re, ssm_b_im, ssm_c_re, ssm_c_im, ssm_d, w_ssm_out,
               w_gate, b_gate, w_out, ln1_g, ln1_b,
               w_router, b_router, w_exp1, b_exp1, w_exp2, b_exp2, ln2_g, ln2_b)
    y_prompt, y_sample = x_prompt, x_sample
    per_layer = []
    for layer in range(DEPTH):
        y_prompt, y_sample, *st = _decoder_layer(
            y_prompt, y_sample, cache_k_win[layer], cache_v_win[layer],
            state_ssm_re[layer], state_ssm_im[layer], *[w[layer] for w in weights])
        per_layer.append(st)
    new_k_prompt = jnp.stack([s[0] for s in per_layer])
    new_v_prompt = jnp.stack([s[1] for s in per_layer])
    new_ssm_re_prompt = jnp.stack([s[2] for s in per_layer])
    new_ssm_im_prompt = jnp.stack([s[3] for s in per_layer])
    new_k_sample = jnp.stack([s[4] for s in per_layer])
    new_v_sample = jnp.stack([s[5] for s in per_layer])
    new_ssm_re_sample = jnp.stack([s[6] for s in per_layer])
    new_ssm_im_sample = jnp.stack([s[7] for s in per_layer])
    return (y_prompt, y_sample, new_k_prompt, new_v_prompt, new_ssm_re_prompt, new_ssm_im_prompt,
            new_k_sample, new_v_sample, new_ssm_re_sample, new_ssm_im_sample)
```

```python
import functools
import math

import numpy as np
import jax
import jax.numpy as jnp
from jax import lax
from jax.experimental import pallas as pl
from jax.experimental.pallas import tpu as pltpu

F32 = jnp.float32
BF16 = jnp.bfloat16

D_MODEL = 1024
HEAD_DIM = 64
N_Q_HEADS = 8
N_KV_HEADS = 2
Q_PER_KV = N_Q_HEADS // N_KV_HEADS
D_ATTN = N_Q_HEADS * HEAD_DIM
D_KV = N_KV_HEADS * HEAD_DIM
WINDOW = 128
ATTN_SCALE = HEAD_DIM ** -0.5
SSM_GROUP = 16
D_SSM = D_MODEL // 2
N_SSM_GROUPS = D_SSM // SSM_GROUP
SSM_STATE = 64
D_IN = D_ATTN + 2 * D_KV + D_SSM
N_EXPERTS = 32
TOP_K = 4
D_FF = D_MODEL
SWIGLU_LIMIT = 7.0
SWIGLU_ALPHA = 1.702
LN_EPS = 1e-5
DEPTH = 1
DEEPNORM_ALPHA = (2 * DEPTH) ** 0.25

LANES = 128
SUBLANES = 8
MXU_DIM = 256

S5_CHUNK = MXU_DIM // SSM_GROUP
MOE_ROWS = 256
TOK_TILE = 256
VMEM_LIMIT = 48 * 1024 * 1024


def _cparams(sem, vmem=None):
    return pltpu.CompilerParams(dimension_semantics=sem, vmem_limit_bytes=vmem)


def _proj_kernel(x_ref, w_ref, b_ref, q_ref, k_ref, v_ref, u_ref, *, exact_f32):
    if exact_f32:
        h = jnp.dot(x_ref[...], w_ref[...], preferred_element_type=F32, precision=lax.Precision.HIGHEST)
    else:
        h = jnp.dot(x_ref[...].astype(BF16), w_ref[...], preferred_element_type=F32)
    h = h + b_ref[...]
    q_ref[...] = (h[:, :D_ATTN] * ATTN_SCALE).astype(q_ref.dtype)
    k_ref[...] = h[:, D_ATTN:D_ATTN + D_KV]
    v_ref[...] = h[:, D_ATTN + D_KV:D_ATTN + 2 * D_KV]
    u_ref[...] = h[:, D_ATTN + 2 * D_KV:].astype(u_ref.dtype)


def _proj(x, w, b, *, tile, exact_f32, qu_dtype):
    n = x.shape[0]
    return pl.pallas_call(
        functools.partial(_proj_kernel, exact_f32=exact_f32),
        grid=(n // tile,),
        in_specs=[pl.BlockSpec((tile, D_MODEL), lambda i: (i, 0)),
                  pl.BlockSpec((D_MODEL, D_IN), lambda i: (0, 0)),
                  pl.BlockSpec((1, D_IN), lambda i: (0, 0))],
        out_specs=[pl.BlockSpec((tile, D_ATTN), lambda i: (i, 0)),
                   pl.BlockSpec((tile, D_KV), lambda i: (i, 0)),
                   pl.BlockSpec((tile, D_KV), lambda i: (i, 0)),
                   pl.BlockSpec((tile, D_SSM), lambda i: (i, 0))],
        out_shape=[jax.ShapeDtypeStruct((n, D_ATTN), qu_dtype),
                   jax.ShapeDtypeStruct((n, D_KV), F32),
                   jax.ShapeDtypeStruct((n, D_KV), F32),
                   jax.ShapeDtypeStruct((n, D_SSM), qu_dtype)],
        compiler_params=_cparams(("parallel",)),
        name="proj",
    )(x, w, b)


ATT_Q_TILE = 512


def _attn_prompt_kernel(sink_ref, q_ref, k_ref, v_ref, o_ref):
    i = pl.program_id(1)
    for blk in range(ATT_Q_TILE // WINDOW):
        q0 = i * ATT_Q_TILE + blk * WINDOW
        k0 = pl.multiple_of(jnp.maximum(q0 - WINDOW, 0), WINDOW)
        kk = k_ref[0, pl.ds(k0, 2 * WINDOW), :].astype(BF16)
        vv = v_ref[0, pl.ds(k0, 2 * WINDOW), :].astype(BF16)
        qb = q_ref[0, blk * WINDOW:(blk + 1) * WINDOW, :]
        qpos = q0 + lax.broadcasted_iota(jnp.int32, (WINDOW, 2 * WINDOW), 0)
        kpos = k0 + lax.broadcasted_iota(jnp.int32, (WINDOW, 2 * WINDOW), 1)
        valid = (kpos <= qpos) & (qpos - kpos <= WINDOW)
        for h in range(N_Q_HEADS):
            kv = h // Q_PER_KV
            qh = qb[:, h * HEAD_DIM:(h + 1) * HEAD_DIM]
            kh = kk[:, kv * HEAD_DIM:(kv + 1) * HEAD_DIM]
            vh = vv[:, kv * HEAD_DIM:(kv + 1) * HEAD_DIM]
            s = lax.dot_general(qh, kh, (((1,), (1,)), ((), ())), preferred_element_type=F32)
            s = jnp.where(valid, s, -jnp.inf)
            sink = sink_ref[h]
            m = jnp.maximum(jnp.max(s, axis=-1, keepdims=True), sink)
            p = jnp.exp(s - m)
            denom = jnp.sum(p, axis=-1, keepdims=True) + jnp.exp(sink - m)
            o = jnp.dot(p.astype(BF16), vh, preferred_element_type=F32) / denom
            o_ref[0, blk * WINDOW:(blk + 1) * WINDOW, h * HEAD_DIM:(h + 1) * HEAD_DIM] = o.astype(o_ref.dtype)


def _attn_prompt(sinks, q, k, v):
    bsz, seq = q.shape[0], q.shape[1]
    return pl.pallas_call(
        _attn_prompt_kernel,
        grid=(bsz, seq // ATT_Q_TILE),
        in_specs=[pl.BlockSpec(memory_space=pltpu.SMEM),
                  pl.BlockSpec((1, ATT_Q_TILE, D_ATTN), lambda b, i: (b, i, 0)),
                  pl.BlockSpec((1, seq, D_KV), lambda b, i: (b, 0, 0)),
                  pl.BlockSpec((1, seq, D_KV), lambda b, i: (b, 0, 0))],
        out_specs=pl.BlockSpec((1, ATT_Q_TILE, D_ATTN), lambda b, i: (b, i, 0)),
        out_shape=jax.ShapeDtypeStruct((bsz, seq, D_ATTN), BF16),
        compiler_params=_cparams(("parallel", "parallel")),
        name="attn_prompt",
    )(sinks, q, k, v)


ATT_S_GROUP = 8


def _attn_sample_kernel(sink_ref, q_ref, kn_ref, vn_ref, kb_ref, vb_ref, o_ref):
    g = ATT_S_GROUP
    rows = Q_PER_KV * g
    ncol = g * WINDOW
    kb = kb_ref[...].reshape(ncol, D_KV).astype(BF16)
    vb = vb_ref[...].reshape(ncol, D_KV).astype(BF16)
    rseq = lax.broadcasted_iota(jnp.int32, (rows, ncol), 0) % g
    cseq = lax.broadcasted_iota(jnp.int32, (rows, ncol), 1) // WINDOW
    own = rseq == cseq
    rhead = lax.broadcasted_iota(jnp.int32, (rows, 1), 0) // g
    for kv in range(N_KV_HEADS):
        lo = kv * HEAD_DIM
        qs = jnp.concatenate(
            [q_ref[:, (kv * Q_PER_KV + h) * HEAD_DIM:(kv * Q_PER_KV + h + 1) * HEAD_DIM] for h in range(Q_PER_KV)],
            axis=0)
        kn = jnp.concatenate([kn_ref[:, lo:lo + HEAD_DIM]] * Q_PER_KV, axis=0)
        vn = jnp.concatenate([vn_ref[:, lo:lo + HEAD_DIM]] * Q_PER_KV, axis=0)
        sink = jnp.zeros((rows, 1), F32)
        for h in range(Q_PER_KV):
            sink = jnp.where(rhead == h, sink_ref[kv * Q_PER_KV + h], sink)
        qs = qs.astype(BF16)
        s = lax.dot_general(qs, kb[:, lo:lo + HEAD_DIM], (((1,), (1,)), ((), ())), preferred_element_type=F32)
        s = jnp.where(own, s, -jnp.inf)
        s_new = jnp.sum(qs.astype(F32) * kn.astype(BF16).astype(F32), axis=-1, keepdims=True)
        m = jnp.maximum(jnp.maximum(jnp.max(s, axis=-1, keepdims=True), s_new), sink)
        p = jnp.exp(s - m)
        p_new = jnp.exp(s_new - m)
        denom = jnp.sum(p, axis=-1, keepdims=True) + p_new + jnp.exp(sink - m)
        o = jnp.dot(p.astype(BF16), vb[:, lo:lo + HEAD_DIM], preferred_element_type=F32)
        o = (o + p_new.astype(BF16).astype(F32) * vn.astype(BF16).astype(F32)) / denom
        for h in range(Q_PER_KV):
            c0 = (kv * Q_PER_KV + h) * HEAD_DIM
            o_ref[:, c0:c0 + HEAD_DIM] = o[h * g:(h + 1) * g].astype(o_ref.dtype)


def _attn_sample(sinks, q, k_new, v_new, k_buf, v_buf):
    n = q.shape[0]
    g = ATT_S_GROUP
    return pl.pallas_call(
        _attn_sample_kernel,
        grid=(n // g,),
        in_specs=[pl.BlockSpec(memory_space=pltpu.SMEM),
                  pl.BlockSpec((g, D_ATTN), lambda i: (i, 0)),
                  pl.BlockSpec((g, D_KV), lambda i: (i, 0)),
                  pl.BlockSpec((g, D_KV), lambda i: (i, 0)),
                  pl.BlockSpec((g, WINDOW, D_KV), lambda i: (i, 0, 0)),
                  pl.BlockSpec((g, WINDOW, D_KV), lambda i: (i, 0, 0))],
        out_specs=pl.BlockSpec((g, D_ATTN), lambda i: (i, 0)),
        out_shape=jax.ShapeDtypeStruct((n, D_ATTN), F32),
        compiler_params=_cparams(("parallel",)),
        name="attn_sample",
    )(sinks, q, k_new, v_new, k_buf, v_buf)


def _s5_params(a_re, a_im, log_dt, b_re, b_im, c_re, c_im):
    hp = lax.Precision.HIGHEST
    dt = jnp.exp(log_dt.astype(F32))[:, None]
    are, aim = a_re.astype(F32), a_im.astype(F32)
    tau = jnp.arange(S5_CHUNK + 1, dtype=F32)[None, :, None]
    mag = jnp.exp(tau * (dt * are)[:, None, :])
    ang = tau * (dt * aim)[:, None, :]
    pw_re, pw_im = mag * jnp.cos(ang), mag * jnp.sin(ang)
    ab_re, ab_im = pw_re[:, 1], pw_im[:, 1]
    den = are * are + aim * aim
    f_re = ((ab_re - 1.0) * are + ab_im * aim) / den
    f_im = (ab_im * are - (ab_re - 1.0) * aim) / den
    bre, bim = b_re.astype(F32), b_im.astype(F32)
    bb_re = f_re[..., None] * bre - f_im[..., None] * bim
    bb_im = f_re[..., None] * bim + f_im[..., None] * bre
    cre, cim = c_re.astype(F32), c_im.astype(F32)
    return dict(pw_re=pw_re, pw_im=pw_im, ab_re=ab_re, ab_im=ab_im, bb_re=bb_re, bb_im=bb_im,
                c_re=cre, c_im=cim, hp=hp)


def _s5_chunk_mats(sp, d_skip):
    hp = sp["hp"]
    g, t, c, p = N_SSM_GROUPS, S5_CHUNK, SSM_GROUP, SSM_STATE
    pw_re, pw_im = sp["pw_re"], sp["pw_im"]
    ca_re = sp["c_re"][:, None] * pw_re[:, :, None, :] - sp["c_im"][:, None] * pw_im[:, :, None, :]
    ca_im = sp["c_re"][:, None] * pw_im[:, :, None, :] + sp["c_im"][:, None] * pw_re[:, :, None, :]
    kern = (jnp.einsum("gtcp,gpd->gtcd", ca_re[:, :t], sp["bb_re"], precision=hp)
            - jnp.einsum("gtcp,gpd->gtcd", ca_im[:, :t], sp["bb_im"], precision=hp))
    lag = jnp.arange(t)[None, :] - jnp.arange(t)[:, None]
    toe = kern[:, jnp.clip(lag, 0, t - 1)]
    toe = jnp.where((lag >= 0)[None, :, :, None, None], toe, 0.0)
    m = jnp.transpose(toe, (0, 1, 4, 2, 3)).reshape(g, t * c, t * c)
    skip = jnp.tile(d_skip.astype(F32).reshape(g, 1, c), (1, t, 1)).reshape(g, t * c)
    m = m + skip[:, :, None] * jnp.eye(t * c, dtype=F32)[None]
    rev_re, rev_im = pw_re[:, t - 1::-1][:, :t], pw_im[:, t - 1::-1][:, :t]
    wst_re = rev_re[:, :, None, :] * jnp.swapaxes(sp["bb_re"], 1, 2)[:, None] \
        - rev_im[:, :, None, :] * jnp.swapaxes(sp["bb_im"], 1, 2)[:, None]
    wst_im = rev_re[:, :, None, :] * jnp.swapaxes(sp["bb_im"], 1, 2)[:, None] \
        + rev_im[:, :, None, :] * jnp.swapaxes(sp["bb_re"], 1, 2)[:, None]
    wst_re = wst_re.reshape(g, t * c, p)
    wst_im = wst_im.reshape(g, t * c, p)
    z = jnp.zeros_like(wst_re[0::2])
    wst2 = jnp.concatenate([
        jnp.concatenate([wst_re[0::2], z, wst_im[0::2], z], axis=2),
        jnp.concatenate([z, wst_re[1::2], z, wst_im[1::2]], axis=2)], axis=1)
    wo_re = jnp.transpose(ca_re[:, 1:t + 1], (0, 3, 1, 2)).reshape(g, p, t * c)
    wo_im = -jnp.transpose(ca_im[:, 1:t + 1], (0, 3, 1, 2)).reshape(g, p, t * c)
    zo = jnp.zeros_like(wo_re[0::2])
    wout2 = jnp.concatenate([
        jnp.concatenate([wo_re[0::2], zo], axis=2),
        jnp.concatenate([zo, wo_re[1::2]], axis=2),
        jnp.concatenate([wo_im[0::2], zo], axis=2),
        jnp.concatenate([zo, wo_im[1::2]], axis=2)], axis=1)
    at_re = pw_re[:, t].reshape(1, g * p)
    at_im = pw_im[:, t].reshape(1, g * p)
    return m.astype(BF16), wst2.astype(BF16), wout2.astype(BF16), at_re, at_im


def _s5_state_kernel(u_ref, wst_ref, sre_ref, sim_ref):
    ucat = jnp.concatenate([u_ref[0], u_ref[1]], axis=1)
    s = jnp.dot(ucat, wst_ref[0], preferred_element_type=F32)
    sre_ref[...] = s[:, :LANES]
    sim_ref[...] = s[:, LANES:]


def _s5_scan_kernel(sre_ref, sim_ref, are_ref, aim_ref, hre_ref, him_ref, fre_ref, fim_ref, *, bsz):
    n = sre_ref.shape[0]
    w = sre_ref.shape[1]
    are = jnp.broadcast_to(are_ref[...], (bsz, w))
    aim = jnp.broadcast_to(aim_ref[...], (bsz, w))
    per_tile = SUBLANES // bsz

    def body(i, carry):
        cre, cim = carry
        r0 = pl.multiple_of(i * SUBLANES, SUBLANES)
        xre = sre_ref[pl.ds(r0, SUBLANES), :]
        xim = sim_ref[pl.ds(r0, SUBLANES), :]
        ore, oim = [], []
        for j in range(per_tile):
            ore.append(cre)
            oim.append(cim)
            sr = xre[j * bsz:(j + 1) * bsz]
            si = xim[j * bsz:(j + 1) * bsz]
            cre, cim = are * cre - aim * cim + sr, are * cim + aim * cre + si
        hre_ref[pl.ds(r0, SUBLANES), :] = jnp.concatenate(ore, axis=0)
        him_ref[pl.ds(r0, SUBLANES), :] = jnp.concatenate(oim, axis=0)
        return cre, cim

    zero = jnp.zeros((bsz, w), F32)
    fre, fim = lax.fori_loop(0, n // SUBLANES, body, (zero, zero))
    fre_ref[...] = fre
    fim_ref[...] = fim


def _s5_out_kernel(u_ref, m_ref, hre_ref, him_ref, wout_ref, y_ref):
    hcat = jnp.concatenate([hre_ref[...], him_ref[...]], axis=1).astype(BF16)
    yi = jnp.dot(hcat, wout_ref[0], preferred_element_type=F32)
    width = S5_CHUNK * SSM_GROUP
    for j in range(2):
        y = jnp.dot(u_ref[j], m_ref[j], preferred_element_type=F32) + yi[:, j * width:(j + 1) * width]
        y_ref[j] = y.astype(y_ref.dtype)


def _s5_prompt(u, bsz, seq, mats):
    m, wst2, wout2, at_re, at_im = mats
    g, t, c, p = N_SSM_GROUPS, S5_CHUNK, SSM_GROUP, SSM_STATE
    nchunk = seq // t
    n = nchunk * bsz
    width = t * c
    ug = u.reshape(bsz, nchunk, t, g, c).transpose(3, 1, 0, 2, 4).reshape(g, n, width)
    gp = g // 2
    s_re, s_im = pl.pallas_call(
        _s5_state_kernel,
        grid=(gp,),
        in_specs=[pl.BlockSpec((2, n, width), lambda i: (i, 0, 0)),
                  pl.BlockSpec((1, 2 * width, 2 * LANES), lambda i: (i, 0, 0))],
        out_specs=[pl.BlockSpec((n, LANES), lambda i: (0, i)),
                   pl.BlockSpec((n, LANES), lambda i: (0, i))],
        out_shape=[jax.ShapeDtypeStruct((n, g * p), F32)] * 2,
        compiler_params=_cparams(("parallel",)),
        name="s5_state",
    )(ug, wst2)
    wblk = 4 * LANES
    h_re, h_im, f_re, f_im = pl.pallas_call(
        functools.partial(_s5_scan_kernel, bsz=bsz),
        grid=(g * p // wblk,),
        in_specs=[pl.BlockSpec((n, wblk), lambda i: (0, i)),
                  pl.BlockSpec((n, wblk), lambda i: (0, i)),
                  pl.BlockSpec((1, wblk), lambda i: (0, i)),
                  pl.BlockSpec((1, wblk), lambda i: (0, i))],
        out_specs=[pl.BlockSpec((n, wblk), lambda i: (0, i)),
                   pl.BlockSpec((n, wblk), lambda i: (0, i)),
                   pl.BlockSpec((bsz, wblk), lambda i: (0, i)),
                   pl.BlockSpec((bsz, wblk), lambda i: (0, i))],
        out_shape=[jax.ShapeDtypeStruct((n, g * p), F32)] * 2 + [jax.ShapeDtypeStruct((bsz, g * p), F32)] * 2,
        compiler_params=_cparams(("parallel",)),
        name="s5_scan",
    )(s_re, s_im, at_re, at_im)
    yg = pl.pallas_call(
        _s5_out_kernel,
        grid=(gp,),
        in_specs=[pl.BlockSpec((2, n, width), lambda i: (i, 0, 0)),
                  pl.BlockSpec((2, width, width), lambda i: (i, 0, 0)),
                  pl.BlockSpec((n, LANES), lambda i: (0, i)),
                  pl.BlockSpec((n, LANES), lambda i: (0, i)),
                  pl.BlockSpec((1, 2 * LANES, 2 * width), lambda i: (i, 0, 0))],
        out_specs=pl.BlockSpec((2, n, width), lambda i: (i, 0, 0)),
        out_shape=jax.ShapeDtypeStruct((g, n, width), BF16),
        compiler_params=_cparams(("parallel",)),
        name="s5_out",
    )(ug, m, h_re, h_im, wout2)
    y = yg.reshape(g, nchunk, bsz, t, c).transpose(2, 1, 3, 0, 4).reshape(bsz * seq, g * c)
    return y, f_re, f_im


S5S_GROUPS = LANES // SSM_GROUP


def _s5_sample_mats(sp, d_skip):
    go, gl, c, p = N_SSM_GROUPS // S5S_GROUPS, S5S_GROUPS, SSM_GROUP, SSM_STATE
    eye = jnp.eye(gl, dtype=F32)

    def bdiag_in(b):
        b4 = b.reshape(go, gl, p, c)
        return jnp.einsum("ogpc,gh->ogchp", b4, eye).reshape(go, gl * c, gl * p)

    def bdiag_out(cm):
        c4 = cm.reshape(go, gl, c, p)
        return jnp.einsum("ogcp,gh->ogphc", c4, eye).reshape(go, gl * p, gl * c)

    b8 = jnp.concatenate([bdiag_in(sp["bb_re"]), bdiag_in(sp["bb_im"])], axis=2)
    c8 = jnp.concatenate([bdiag_out(sp["c_re"]), -bdiag_out(sp["c_im"])], axis=1)
    a_re = sp["ab_re"].reshape(1, N_SSM_GROUPS * p)
    a_im = sp["ab_im"].reshape(1, N_SSM_GROUPS * p)
    return b8, c8, a_re, a_im, d_skip.astype(F32).reshape(1, D_SSM)


def _s5_sample_kernel(u_ref, hre_ref, him_ref, b8_ref, c8_ref, are_ref, aim_ref, d_ref,
                      y_ref, ore_ref, oim_ref):
    hp = lax.Precision.HIGHEST
    u = u_ref[...]
    half = S5S_GROUPS * SSM_STATE
    bu = jnp.dot(u, b8_ref[0], preferred_element_type=F32, precision=hp)
    are, aim = are_ref[...], aim_ref[...]
    h0r, h0i = hre_ref[...], him_ref[...]
    hr = are * h0r - aim * h0i + bu[:, :half]
    hi = are * h0i + aim * h0r + bu[:, half:]
    ore_ref[...] = hr
    oim_ref[...] = hi
    y = jnp.dot(jnp.concatenate([hr, hi], axis=1), c8_ref[0], preferred_element_type=F32, precision=hp)
    y_ref[...] = (y + d_ref[...] * u).astype(y_ref.dtype)


def _s5_sample(u, h0_re, h0_im, mats):
    b8, c8, a_re, a_im, d = mats
    n = u.shape[0]
    half = S5S_GROUPS * SSM_STATE
    return pl.pallas_call(
        _s5_sample_kernel,
        grid=(N_SSM_GROUPS // S5S_GROUPS,),
        in_specs=[pl.BlockSpec((n, LANES), lambda i: (0, i)),
                  pl.BlockSpec((n, half), lambda i: (0, i)),
                  pl.BlockSpec((n, half), lambda i: (0, i)),
                  pl.BlockSpec((1, LANES, 2 * half), lambda i: (i, 0, 0)),
                  pl.BlockSpec((1, 2 * half, LANES), lambda i: (i, 0, 0)),
                  pl.BlockSpec((1, half), lambda i: (0, i)),
                  pl.BlockSpec((1, half), lambda i: (0, i)),
                  pl.BlockSpec((1, LANES), lambda i: (0, i))],
        out_specs=[pl.BlockSpec((n, LANES), lambda i: (0, i)),
                   pl.BlockSpec((n, half), lambda i: (0, i)),
                   pl.BlockSpec((n, half), lambda i: (0, i))],
        out_shape=[jax.ShapeDtypeStruct((n, D_SSM), BF16),
                   jax.ShapeDtypeStruct((n, N_SSM_GROUPS * SSM_STATE), F32),
                   jax.ShapeDtypeStruct((n, N_SSM_GROUPS * SSM_STATE), F32)],
        compiler_params=_cparams(("parallel",)),
        name="s5_sample",
    )(u, h0_re, h0_im, b8, c8, a_re, a_im, d)


def _layer_norm(x, g, b):
    mu = jnp.mean(x, axis=-1, keepdims=True)
    xc = x - mu
    var = jnp.mean(xc * xc, axis=-1, keepdims=True)
    return xc * lax.rsqrt(var + LN_EPS) * g + b


def _merge_kernel(x_ref, oa_ref, ys_ref, cnt_in_ref, wao_ref, wso_ref, wg_ref, bg_ref, wo_ref,
                  g1_ref, b1_ref, wr_ref, br_ref,
                  x1_ref, eidx_ref, gate_ref, rank_ref, cnt_out_ref, cnt_sc):
    step = pl.program_id(0)

    @pl.when(step == 0)
    def _():
        cnt_sc[...] = cnt_in_ref[...]

    tm = x_ref.shape[0]
    x = x_ref[...]
    branch_a = jnp.dot(oa_ref[...].astype(BF16), wao_ref[...], preferred_element_type=F32)
    z = jnp.dot(jax.nn.gelu(ys_ref[...].astype(F32)).astype(BF16), wso_ref[...], preferred_element_type=F32)
    branch_b = z[:, :D_MODEL] * jax.nn.sigmoid(z[:, D_MODEL:])
    gates = jax.nn.sigmoid(jnp.dot(x.astype(BF16), wg_ref[...], preferred_element_type=F32) + bg_ref[...])
    mixed = gates[:, :D_MODEL] * branch_a + gates[:, D_MODEL:] * branch_b
    mix = jnp.dot(mixed.astype(BF16), wo_ref[...], preferred_element_type=F32)
    x1 = _layer_norm(DEEPNORM_ALPHA * x + mix, g1_ref[...], b1_ref[...])
    x1_ref[...] = x1

    logits = jnp.dot(x1, wr_ref[...], preferred_element_type=F32, precision=lax.Precision.HIGHEST) + br_ref[...]
    lane = lax.broadcasted_iota(jnp.int32, (tm, N_EXPERTS), 1)
    work = logits
    vals, sels, idxs = [], [], []
    for _ in range(TOP_K):
        mx = jnp.max(work, axis=-1, keepdims=True)
        idx = jnp.min(jnp.where(work == mx, lane, N_EXPERTS), axis=-1, keepdims=True)
        sel = lane == idx
        vals.append(mx)
        idxs.append(idx)
        sels.append(sel)
        work = jnp.where(sel, -jnp.inf, work)
    ex = [jnp.exp(v - vals[0]) for v in vals]
    tot = ex[0] + ex[1] + ex[2] + ex[3]
    gate_ref[...] = jnp.concatenate([e / tot for e in ex], axis=1)
    eidx_ref[...] = jnp.concatenate(idxs, axis=1)

    multi = jnp.zeros((tm, N_EXPERTS), F32)
    for sel in sels:
        multi = multi + sel.astype(F32)
    r = lax.broadcasted_iota(jnp.int32, (tm, tm), 0)
    cc = lax.broadcasted_iota(jnp.int32, (tm, tm), 1)
    tri = (cc < r).astype(BF16)
    before = jnp.dot(tri, multi.astype(BF16), preferred_element_type=F32) + cnt_sc[...]
    ranks = [jnp.sum(jnp.where(sel, before, 0.0), axis=-1, keepdims=True) for sel in sels]
    rank_ref[...] = jnp.concatenate(ranks, axis=1).astype(jnp.int32)
    cnt_sc[...] = cnt_sc[...] + jnp.sum(multi, axis=0, keepdims=True)
    cnt_out_ref[...] = cnt_sc[...]


def _merge(x, o_attn, y_ssm, cnt_in, w, *, tile):
    n = x.shape[0]
    full = lambda shape: pl.BlockSpec(shape, lambda i: (0,) * len(shape))
    return pl.pallas_call(
        _merge_kernel,
        grid=(n // tile,),
        in_specs=[pl.BlockSpec((tile, D_MODEL), lambda i: (i, 0)),
                  pl.BlockSpec((tile, D_ATTN), lambda i: (i, 0)),
                  pl.BlockSpec((tile, D_SSM), lambda i: (i, 0)),
                  full((1, N_EXPERTS)),
                  full((D_ATTN, D_MODEL)), full((D_SSM, 2 * D_MODEL)), full((D_MODEL, 2 * D_MODEL)),
                  full((1, 2 * D_MODEL)), full((D_MODEL, D_MODEL)),
                  full((1, D_MODEL)), full((1, D_MODEL)),
                  full((D_MODEL, N_EXPERTS)), full((1, N_EXPERTS))],
        out_specs=[pl.BlockSpec((tile, D_MODEL), lambda i: (i, 0)),
                   pl.BlockSpec((tile, TOP_K), lambda i: (i, 0)),
                   pl.BlockSpec((tile, TOP_K), lambda i: (i, 0)),
                   pl.BlockSpec((tile, TOP_K), lambda i: (i, 0)),
                   full((1, N_EXPERTS))],
        out_shape=[jax.ShapeDtypeStruct((n, D_MODEL), F32),
                   jax.ShapeDtypeStruct((n, TOP_K), jnp.int32),
                   jax.ShapeDtypeStruct((n, TOP_K), F32),
                   jax.ShapeDtypeStruct((n, TOP_K), jnp.int32),
                   jax.ShapeDtypeStruct((1, N_EXPERTS), F32)],
        scratch_shapes=[pltpu.VMEM((1, N_EXPERTS), F32)],
        compiler_params=_cparams(("arbitrary",), VMEM_LIMIT),
        name="merge",
    )(x, o_attn, y_ssm, cnt_in, w["wao"], w["wso"], w["wg"], w["bg"], w["wo"], w["g1"], w["b1"], w["wr"], w["br"])


def _dispatch_kernel(pstart_ref, eidx_ref, rank_ref, x_ref, xs_in_ref, xs_ref, sem):
    del xs_in_ref
    tm = x_ref.shape[0]

    def row_copy(t, k):
        j = t * TOP_K + k
        dest = pstart_ref[eidx_ref[j]] + rank_ref[j]
        return pltpu.make_async_copy(x_ref.at[pl.ds(t, 1)], xs_ref.at[pl.ds(dest, 1)], sem)

    def issue(t, c):
        for k in range(TOP_K):
            row_copy(t, k).start()
        return c

    lax.fori_loop(0, tm, issue, 0)
    for _ in range(TOP_K):
        pltpu.make_async_copy(x_ref, xs_ref.at[pl.ds(0, tm)], sem).wait()


def _dispatch(pstart, eidx_flat, rank_flat, x1, xs, *, tile):
    n = x1.shape[0]
    nrows = xs.shape[0]
    return pl.pallas_call(
        _dispatch_kernel,
        grid_spec=pltpu.PrefetchScalarGridSpec(
            num_scalar_prefetch=1,
            grid=(n // tile,),
            in_specs=[pl.BlockSpec((tile * TOP_K,), lambda i, ps: (i,), memory_space=pltpu.SMEM),
                      pl.BlockSpec((tile * TOP_K,), lambda i, ps: (i,), memory_space=pltpu.SMEM),
                      pl.BlockSpec((tile, D_MODEL), lambda i, ps: (i, 0)),
                      pl.BlockSpec(memory_space=pl.ANY)],
            out_specs=pl.BlockSpec(memory_space=pl.ANY),
            scratch_shapes=[pltpu.SemaphoreType.DMA(())]),
        out_shape=jax.ShapeDtypeStruct((nrows, D_MODEL), F32),
        input_output_aliases={4: 0},
        compiler_params=_cparams(("arbitrary",)),
        name="dispatch",
    )(pstart, eidx_flat, rank_flat, x1, xs)


def _deinterleave_matrix():
    pm = np.zeros((MXU_DIM, MXU_DIM), np.float32)
    half = MXU_DIM // 2
    for c in range(half):
        pm[2 * c, c] = 1.0
        pm[2 * c + 1, half + c] = 1.0
    return pm


def _expert_kernel(be_ref, nu_ref, xs_ref, w1_ref, b1_ref, w2_ref, b2_ref, pm_ref, y_ref, w1p_sc, w2b_sc):
    i = pl.program_id(0)
    e = be_ref[i]
    prev = be_ref[jnp.maximum(i - 1, 0)]
    nblk = 2 * D_FF // MXU_DIM

    @pl.when((i == 0) | (e != prev))
    def _():
        for cb in range(nblk):
            blk = w1_ref[0, :, cb * MXU_DIM:(cb + 1) * MXU_DIM].astype(BF16)
            w1p_sc[:, cb * MXU_DIM:(cb + 1) * MXU_DIM] = jnp.dot(
                blk, pm_ref[...], preferred_element_type=F32).astype(BF16)
        w2b_sc[...] = w2_ref[0].astype(BF16)

    @pl.when(i < nu_ref[0])
    def _():
        x = xs_ref[...].astype(BF16)
        h = jnp.dot(x, w1p_sc[...], preferred_element_type=F32) + b1_ref[0]
        half = MXU_DIM // 2
        acts = []
        for cb in range(nblk):
            x_glu = jnp.minimum(h[:, cb * MXU_DIM:cb * MXU_DIM + half], SWIGLU_LIMIT)
            x_lin = jnp.clip(h[:, cb * MXU_DIM + half:(cb + 1) * MXU_DIM], -SWIGLU_LIMIT, SWIGLU_LIMIT)
            acts.append((x_glu * jax.nn.sigmoid(SWIGLU_ALPHA * x_glu) * (x_lin + 1.0)).astype(BF16))
        act = jnp.concatenate(acts, axis=1)
        y_ref[...] = jnp.dot(act, w2b_sc[...], preferred_element_type=F32) + b2_ref[0]

    @pl.when(i >= nu_ref[0])
    def _():
        y_ref[...] = jnp.zeros_like(y_ref)


def _experts(block_e, n_used, xs, w1, b1p, w2, b2, pm):
    nrows = xs.shape[0]
    nb = nrows // MOE_ROWS
    return pl.pallas_call(
        _expert_kernel,
        grid_spec=pltpu.PrefetchScalarGridSpec(
            num_scalar_prefetch=2,
            grid=(nb,),
            in_specs=[pl.BlockSpec((MOE_ROWS, D_MODEL), lambda i, be, nu: (jnp.minimum(i, nu[0] - 1), 0)),
                      pl.BlockSpec((1, D_MODEL, 2 * D_FF), lambda i, be, nu: (be[i], 0, 0)),
                      pl.BlockSpec((1, 1, 2 * D_FF), lambda i, be, nu: (be[i], 0, 0)),
                      pl.BlockSpec((1, D_FF, D_MODEL), lambda i, be, nu: (be[i], 0, 0)),
                      pl.BlockSpec((1, 1, D_MODEL), lambda i, be, nu: (be[i], 0, 0)),
                      pl.BlockSpec((MXU_DIM, MXU_DIM), lambda i, be, nu: (0, 0))],
            out_specs=pl.BlockSpec((MOE_ROWS, D_MODEL), lambda i, be, nu: (i, 0)),
            scratch_shapes=[pltpu.VMEM((D_MODEL, 2 * D_FF), BF16), pltpu.VMEM((D_FF, D_MODEL), BF16)]),
        out_shape=jax.ShapeDtypeStruct((nrows, D_MODEL), F32),
        compiler_params=_cparams(("arbitrary",), VMEM_LIMIT),
        name="experts",
    )(block_e, n_used, xs, w1, b1p, w2, b2, pm)


def _combine_kernel(pstart_ref, eidx_ref, rank_ref, x1_ref, gate_ref, g2_ref, b2_ref, ys_ref, y_ref, rows, sem):
    tm = x1_ref.shape[0]

    def row_copy(t, k):
        j = t * TOP_K + k
        src = pstart_ref[eidx_ref[j]] + rank_ref[j]
        return pltpu.make_async_copy(ys_ref.at[pl.ds(src, 1)], rows.at[k, pl.ds(t, 1)], sem)

    def issue(t, c):
        for k in range(TOP_K):
            row_copy(t, k).start()
        return c

    lax.fori_loop(0, tm, issue, 0)
    for k in range(TOP_K):
        pltpu.make_async_copy(ys_ref.at[pl.ds(0, tm)], rows.at[k], sem).wait()

    gate = gate_ref[...]
    ffn = gate[:, 0:1] * rows[0]
    for k in range(1, TOP_K):
        ffn = ffn + gate[:, k:k + 1] * rows[k]
    y_ref[...] = _layer_norm(DEEPNORM_ALPHA * x1_ref[...] + ffn, g2_ref[...], b2_ref[...])


def _combine(pstart, eidx_flat, rank_flat, x1, gate, g2, b2, ys, *, tile):
    n = x1.shape[0]
    return pl.pallas_call(
        _combine_kernel,
        grid_spec=pltpu.PrefetchScalarGridSpec(
            num_scalar_prefetch=1,
            grid=(n // tile,),
            in_specs=[pl.BlockSpec((tile * TOP_K,), lambda i, ps: (i,), memory_space=pltpu.SMEM),
                      pl.BlockSpec((tile * TOP_K,), lambda i, ps: (i,), memory_space=pltpu.SMEM),
                      pl.BlockSpec((tile, D_MODEL), lambda i, ps: (i, 0)),
                      pl.BlockSpec((tile, TOP_K), lambda i, ps: (i, 0)),
                      pl.BlockSpec((1, D_MODEL), lambda i, ps: (0, 0)),
                      pl.BlockSpec((1, D_MODEL), lambda i, ps: (0, 0)),
                      pl.BlockSpec(memory_space=pl.ANY)],
            out_specs=pl.BlockSpec((tile, D_MODEL), lambda i, ps: (i, 0)),
            scratch_shapes=[pltpu.VMEM((TOP_K, tile, D_MODEL), F32), pltpu.SemaphoreType.DMA(())]),
        out_shape=jax.ShapeDtypeStruct((n, D_MODEL), F32),
        compiler_params=_cparams(("arbitrary",)),
        name="combine",
    )(pstart, eidx_flat, rank_flat, x1, gate, g2, b2, ys)


def kernel(x_prompt, x_sample, cache_k_win, cache_v_win, state_ssm_re, state_ssm_im, w_in, b_in, attn_sinks,
           w_attn_out, ssm_a_re, ssm_a_im, ssm_log_dt, ssm_b_re, ssm_b_im, ssm_c_re, ssm_c_im, ssm_d, w_ssm_out,
           w_gate, b_gate, w_out, ln1_g, ln1_b, w_router, b_router, w_exp1, b_exp1, w_exp2, b_exp2, ln2_g, ln2_b):
    assert w_in.shape[0] == DEPTH == 1
    bsz, seq, _ = x_prompt.shape
    nsamp = x_sample.shape[0]
    assert x_sample.shape[1] == 1
    n_p = bsz * seq
    n_tok = n_p + nsamp

    xp = x_prompt.reshape(n_p, D_MODEL)
    xsm = x_sample.reshape(nsamp, D_MODEL)
    b_in2 = b_in[0].reshape(1, D_IN)
    sinks = attn_sinks[0].astype(F32)

    q_p, k_p, v_p, u_p = _proj(xp, w_in[0].astype(BF16), b_in2, tile=512, exact_f32=False, qu_dtype=BF16)
    q_s, k_s, v_s, u_s = _proj(xsm, w_in[0], b_in2, tile=nsamp, exact_f32=True, qu_dtype=F32)

    o_p = _attn_prompt(sinks, q_p.reshape(bsz, seq, D_ATTN), k_p.reshape(bsz, seq, D_KV),
                       v_p.reshape(bsz, seq, D_KV)).reshape(n_p, D_ATTN)
    k_buf = cache_k_win[0].reshape(nsamp, WINDOW, D_KV)
    v_buf = cache_v_win[0].reshape(nsamp, WINDOW, D_KV)
    o_s = _attn_sample(sinks, q_s, k_s, v_s, k_buf, v_buf)

    sp = _s5_params(ssm_a_re[0], ssm_a_im[0], ssm_log_dt[0], ssm_b_re[0], ssm_b_im[0], ssm_c_re[0], ssm_c_im[0])
    y_p, hp_re, hp_im = _s5_prompt(u_p, bsz, seq, _s5_chunk_mats(sp, ssm_d[0]))
    y_s, hs_re, hs_im = _s5_sample(u_s, state_ssm_re[0].reshape(nsamp, -1), state_ssm_im[0].reshape(nsamp, -1),
                                   _s5_sample_mats(sp, ssm_d[0]))

    wm = dict(wao=w_attn_out[0].astype(BF16), wso=w_ssm_out[0].astype(BF16), wg=w_gate[0].astype(BF16),
              bg=b_gate[0].reshape(1, -1), wo=w_out[0].astype(BF16), g1=ln1_g[0].reshape(1, -1),
              b1=ln1_b[0].reshape(1, -1), wr=w_router[0], br=b_router[0].reshape(1, -1))
    cnt0 = jnp.zeros((1, N_EXPERTS), F32)
    x1_p, e_p, g_p, r_p, cnt1 = _merge(xp, o_p, y_p, cnt0, wm, tile=TOK_TILE)
    x1_s, e_s, g_s, r_s, cnt2 = _merge(xsm, o_s, y_s, cnt1, wm, tile=nsamp)

    counts = cnt2[0].astype(jnp.int32)
    padded = ((counts + MOE_ROWS - 1) // MOE_ROWS) * MOE_ROWS
    pad_end = jnp.cumsum(padded)
    pad_start = (pad_end - padded).astype(jnp.int32)
    nb_max = (n_tok * TOP_K + N_EXPERTS * (MOE_ROWS - 1) + MOE_ROWS - 1) // MOE_ROWS
    n_used = (pad_end[-1] // MOE_ROWS).astype(jnp.int32)
    blk_start = jnp.arange(nb_max, dtype=jnp.int32) * MOE_ROWS
    blk_e = jnp.minimum(jnp.sum(blk_start[:, None] >= pad_end[None, :], axis=1), N_EXPERTS - 1).astype(jnp.int32)
    blk_e = jnp.where(jnp.arange(nb_max) < n_used, blk_e, blk_e[jnp.maximum(n_used - 1, 0)])

    xs = jnp.zeros((nb_max * MOE_ROWS, D_MODEL), F32)
    xs = _dispatch(pad_start, e_p.reshape(-1), r_p.reshape(-1), x1_p, xs, tile=TOK_TILE)
    xs = _dispatch(pad_start, e_s.reshape(-1), r_s.reshape(-1), x1_s, xs, tile=nsamp)

    b1p = b_exp1[0].reshape(N_EXPERTS, 2 * D_FF // MXU_DIM, MXU_DIM // 2, 2)
    b1p = jnp.swapaxes(b1p, 2, 3).reshape(N_EXPERTS, 1, 2 * D_FF)
    ys = _experts(blk_e, n_used.reshape(1), xs, w_exp1[0], b1p, w_exp2[0], b_exp2[0].reshape(N_EXPERTS, 1, D_MODEL),
                  jnp.asarray(_deinterleave_matrix(), BF16))

    g2, b2 = ln2_g[0].reshape(1, -1), ln2_b[0].reshape(1, -1)
    y_prompt = _combine(pad_start, e_p.reshape(-1), r_p.reshape(-1), x1_p, g_p, g2, b2, ys, tile=TOK_TILE)
    y_sample = _combine(pad_start, e_s.reshape(-1), r_s.reshape(-1), x1_s, g_s, g2, b2, ys, tile=nsamp)

    k_p4 = k_p.reshape(bsz, seq, N_KV_HEADS, HEAD_DIM)[:, -WINDOW:]
    v_p4 = v_p.reshape(bsz, seq, N_KV_HEADS, HEAD_DIM)[:, -WINDOW:]
    k_s4 = jnp.concatenate([cache_k_win[0][:, 1:], k_s.reshape(nsamp, 1, N_KV_HEADS, HEAD_DIM)], axis=1)
    v_s4 = jnp.concatenate([cache_v_win[0][:, 1:], v_s.reshape(nsamp, 1, N_KV_HEADS, HEAD_DIM)], axis=1)
    st = lambda a, n: a.reshape(1, n, N_SSM_GROUPS, SSM_STATE)
    return (y_prompt.reshape(bsz, seq, D_MODEL), y_sample.reshape(nsamp, 1, D_MODEL),
            k_p4[None], v_p4[None], st(hp_re, bsz), st(hp_im, bsz),
            k_s4[None], v_s4[None], st(hs_re, nsamp), st(hs_im, nsamp))
```

```python
import functools
import math

import numpy as np
import jax
import jax.numpy as jnp
from jax import lax
from jax.experimental import pallas as pl
from jax.experimental.pallas import tpu as pltpu

F32 = jnp.float32
BF16 = jnp.bfloat16

D_MODEL = 1024
HEAD_DIM = 64
N_Q_HEADS = 8
N_KV_HEADS = 2
Q_PER_KV = N_Q_HEADS // N_KV_HEADS
D_ATTN = N_Q_HEADS * HEAD_DIM
D_KV = N_KV_HEADS * HEAD_DIM
WINDOW = 128
ATTN_SCALE = HEAD_DIM ** -0.5
SSM_GROUP = 16
D_SSM = D_MODEL // 2
N_SSM_GROUPS = D_SSM // SSM_GROUP
SSM_STATE = 64
D_IN = D_ATTN + 2 * D_KV + D_SSM
N_EXPERTS = 32
TOP_K = 4
D_FF = D_MODEL
SWIGLU_LIMIT = 7.0
SWIGLU_ALPHA = 1.702
LN_EPS = 1e-5
DEPTH = 1
DEEPNORM_ALPHA = (2 * DEPTH) ** 0.25

LANES = 128
SUBLANES = 8
MXU_DIM = 256

S5_CHUNK = MXU_DIM // SSM_GROUP
S5_LANE_GROUPS = LANES // SSM_GROUP
MOE_ROWS = 256
TOK_TILE = 256
VMEM_LIMIT = 48 * 1024 * 1024


def _cparams(sem, vmem=None):
    return pltpu.CompilerParams(dimension_semantics=sem, vmem_limit_bytes=vmem)


def _proj_kernel(x_ref, w_ref, b_ref, q_ref, k_ref, v_ref, u_ref, *, exact_f32):
    if exact_f32:
        h = jnp.dot(x_ref[...], w_ref[...], preferred_element_type=F32, precision=lax.Precision.HIGHEST)
    else:
        h = jnp.dot(x_ref[...].astype(BF16), w_ref[...], preferred_element_type=F32)
    h = h + b_ref[...]
    q_ref[...] = (h[:, :D_ATTN] * ATTN_SCALE).astype(q_ref.dtype)
    k_ref[...] = h[:, D_ATTN:D_ATTN + D_KV]
    v_ref[...] = h[:, D_ATTN + D_KV:D_ATTN + 2 * D_KV]
    u_ref[...] = h[:, D_ATTN + 2 * D_KV:].astype(u_ref.dtype)


def _proj(x, w, b, *, tile, exact_f32, q_dtype):
    n = x.shape[0]
    return pl.pallas_call(
        functools.partial(_proj_kernel, exact_f32=exact_f32),
        grid=(n // tile,),
        in_specs=[pl.BlockSpec((tile, D_MODEL), lambda i: (i, 0)),
                  pl.BlockSpec((D_MODEL, D_IN), lambda i: (0, 0)),
                  pl.BlockSpec((1, D_IN), lambda i: (0, 0))],
        out_specs=[pl.BlockSpec((tile, D_ATTN), lambda i: (i, 0)),
                   pl.BlockSpec((tile, D_KV), lambda i: (i, 0)),
                   pl.BlockSpec((tile, D_KV), lambda i: (i, 0)),
                   pl.BlockSpec((tile, D_SSM), lambda i: (i, 0))],
        out_shape=[jax.ShapeDtypeStruct((n, D_ATTN), q_dtype),
                   jax.ShapeDtypeStruct((n, D_KV), F32),
                   jax.ShapeDtypeStruct((n, D_KV), F32),
                   jax.ShapeDtypeStruct((n, D_SSM), F32)],
        compiler_params=_cparams(("parallel",)),
        name="proj",
    )(x, w, b)


ATT_Q_TILE = 512


def _attn_prompt_kernel(sink_ref, q_ref, k_ref, v_ref, o_ref):
    i = pl.program_id(1)
    for blk in range(ATT_Q_TILE // WINDOW):
        q0 = i * ATT_Q_TILE + blk * WINDOW
        k0 = pl.multiple_of(jnp.maximum(q0 - WINDOW, 0), WINDOW)
        kk = k_ref[0, pl.ds(k0, 2 * WINDOW), :].astype(BF16)
        vv = v_ref[0, pl.ds(k0, 2 * WINDOW), :].astype(BF16)
        qb = q_ref[0, blk * WINDOW:(blk + 1) * WINDOW, :]
        qpos = q0 + lax.broadcasted_iota(jnp.int32, (WINDOW, 2 * WINDOW), 0)
        kpos = k0 + lax.broadcasted_iota(jnp.int32, (WINDOW, 2 * WINDOW), 1)
        valid = (kpos <= qpos) & (qpos - kpos <= WINDOW)
        for h in range(N_Q_HEADS):
            kv = h // Q_PER_KV
            qh = qb[:, h * HEAD_DIM:(h + 1) * HEAD_DIM]
            kh = kk[:, kv * HEAD_DIM:(kv + 1) * HEAD_DIM]
            vh = vv[:, kv * HEAD_DIM:(kv + 1) * HEAD_DIM]
            s = lax.dot_general(qh, kh, (((1,), (1,)), ((), ())), preferred_element_type=F32)
            s = jnp.where(valid, s, -jnp.inf)
            sink = sink_ref[h]
            m = jnp.maximum(jnp.max(s, axis=-1, keepdims=True), sink)
            p = jnp.exp(s - m)
            denom = jnp.sum(p, axis=-1, keepdims=True) + jnp.exp(sink - m)
            o = jnp.dot(p.astype(BF16), vh, preferred_element_type=F32) / denom
            o_ref[0, blk * WINDOW:(blk + 1) * WINDOW, h * HEAD_DIM:(h + 1) * HEAD_DIM] = o.astype(o_ref.dtype)


def _attn_prompt(sinks, q, k, v):
    bsz, seq = q.shape[0], q.shape[1]
    return pl.pallas_call(
        _attn_prompt_kernel,
        grid=(bsz, seq // ATT_Q_TILE),
        in_specs=[pl.BlockSpec(memory_space=pltpu.SMEM),
                  pl.BlockSpec((1, ATT_Q_TILE, D_ATTN), lambda b, i: (b, i, 0)),
                  pl.BlockSpec((1, seq, D_KV), lambda b, i: (b, 0, 0)),
                  pl.BlockSpec((1, seq, D_KV), lambda b, i: (b, 0, 0))],
        out_specs=pl.BlockSpec((1, ATT_Q_TILE, D_ATTN), lambda b, i: (b, i, 0)),
        out_shape=jax.ShapeDtypeStruct((bsz, seq, D_ATTN), BF16),
        compiler_params=_cparams(("parallel", "parallel")),
        name="attn_prompt",
    )(sinks, q, k, v)


ATT_S_GROUP = 8


def _attn_sample_kernel(sink_ref, q_ref, kn_ref, vn_ref, kb_ref, vb_ref, o_ref):
    g = ATT_S_GROUP
    rows = Q_PER_KV * g
    ncol = g * WINDOW
    kb = kb_ref[...].reshape(ncol, D_KV).astype(BF16)
    vb = vb_ref[...].reshape(ncol, D_KV).astype(BF16)
    rseq = lax.broadcasted_iota(jnp.int32, (rows, ncol), 0) % g
    cseq = lax.broadcasted_iota(jnp.int32, (rows, ncol), 1) // WINDOW
    own = rseq == cseq
    rhead = lax.broadcasted_iota(jnp.int32, (rows, 1), 0) // g
    for kv in range(N_KV_HEADS):
        lo = kv * HEAD_DIM
        qs = jnp.concatenate(
            [q_ref[:, (kv * Q_PER_KV + h) * HEAD_DIM:(kv * Q_PER_KV + h + 1) * HEAD_DIM] for h in range(Q_PER_KV)],
            axis=0)
        kn = jnp.concatenate([kn_ref[:, lo:lo + HEAD_DIM]] * Q_PER_KV, axis=0)
        vn = jnp.concatenate([vn_ref[:, lo:lo + HEAD_DIM]] * Q_PER_KV, axis=0)
        sink = jnp.zeros((rows, 1), F32)
        for h in range(Q_PER_KV):
            sink = jnp.where(rhead == h, sink_ref[kv * Q_PER_KV + h], sink)
        qs = qs.astype(BF16)
        s = lax.dot_general(qs, kb[:, lo:lo + HEAD_DIM], (((1,), (1,)), ((), ())), preferred_element_type=F32)
        s = jnp.where(own, s, -jnp.inf)
        s_new = jnp.sum(qs.astype(F32) * kn.astype(BF16).astype(F32), axis=-1, keepdims=True)
        m = jnp.maximum(jnp.maximum(jnp.max(s, axis=-1, keepdims=True), s_new), sink)
        p = jnp.exp(s - m)
        p_new = jnp.exp(s_new - m)
        denom = jnp.sum(p, axis=-1, keepdims=True) + p_new + jnp.exp(sink - m)
        o = jnp.dot(p.astype(BF16), vb[:, lo:lo + HEAD_DIM], preferred_element_type=F32)
        o = (o + p_new.astype(BF16).astype(F32) * vn.astype(BF16).astype(F32)) / denom
        for h in range(Q_PER_KV):
            c0 = (kv * Q_PER_KV + h) * HEAD_DIM
            o_ref[:, c0:c0 + HEAD_DIM] = o[h * g:(h + 1) * g].astype(o_ref.dtype)


def _attn_sample(sinks, q, k_new, v_new, k_buf, v_buf):
    n = q.shape[0]
    g = ATT_S_GROUP
    return pl.pallas_call(
        _attn_sample_kernel,
        grid=(n // g,),
        in_specs=[pl.BlockSpec(memory_space=pltpu.SMEM),
                  pl.BlockSpec((g, D_ATTN), lambda i: (i, 0)),
                  pl.BlockSpec((g, D_KV), lambda i: (i, 0)),
                  pl.BlockSpec((g, D_KV), lambda i: (i, 0)),
                  pl.BlockSpec((g, WINDOW, D_KV), lambda i: (i, 0, 0)),
                  pl.BlockSpec((g, WINDOW, D_KV), lambda i: (i, 0, 0))],
        out_specs=pl.BlockSpec((g, D_ATTN), lambda i: (i, 0)),
        out_shape=jax.ShapeDtypeStruct((n, D_ATTN), F32),
        compiler_params=_cparams(("parallel",)),
        name="attn_sample",
    )(sinks, q, k_new, v_new, k_buf, v_buf)


def _s5_params(a_re, a_im, log_dt, b_re, b_im, c_re, c_im):
    hp = lax.Precision.HIGHEST
    dt = jnp.exp(log_dt.astype(F32))[:, None]
    are, aim = a_re.astype(F32), a_im.astype(F32)
    tau = jnp.arange(S5_CHUNK + 1, dtype=F32)[None, :, None]
    mag = jnp.exp(tau * (dt * are)[:, None, :])
    ang = tau * (dt * aim)[:, None, :]
    pw_re, pw_im = mag * jnp.cos(ang), mag * jnp.sin(ang)
    ab_re, ab_im = pw_re[:, 1], pw_im[:, 1]
    den = are * are + aim * aim
    f_re = ((ab_re - 1.0) * are + ab_im * aim) / den
    f_im = (ab_im * are - (ab_re - 1.0) * aim) / den
    bre, bim = b_re.astype(F32), b_im.astype(F32)
    bb_re = f_re[..., None] * bre - f_im[..., None] * bim
    bb_im = f_re[..., None] * bim + f_im[..., None] * bre
    cre, cim = c_re.astype(F32), c_im.astype(F32)
    return dict(pw_re=pw_re, pw_im=pw_im, ab_re=ab_re, ab_im=ab_im, bb_re=bb_re, bb_im=bb_im,
                c_re=cre, c_im=cim, hp=hp)


def _s5_chunk_mats(sp, d_skip):
    hp = sp["hp"]
    g, t, c, p = N_SSM_GROUPS, S5_CHUNK, SSM_GROUP, SSM_STATE
    pw_re, pw_im = sp["pw_re"], sp["pw_im"]
    ca_re = sp["c_re"][:, None] * pw_re[:, :, None, :] - sp["c_im"][:, None] * pw_im[:, :, None, :]
    ca_im = sp["c_re"][:, None] * pw_im[:, :, None, :] + sp["c_im"][:, None] * pw_re[:, :, None, :]
    kern = (jnp.einsum("gtcp,gpd->gtcd", ca_re[:, :t], sp["bb_re"], precision=hp)
            - jnp.einsum("gtcp,gpd->gtcd", ca_im[:, :t], sp["bb_im"], precision=hp))
    kc = jnp.swapaxes(kern, 2, 3)
    kc = kc.at[:, 0].add(d_skip.astype(F32).reshape(g, 1, c) * jnp.eye(c, dtype=F32)[None])
    rev_re, rev_im = pw_re[:, t - 1::-1][:, :t], pw_im[:, t - 1::-1][:, :t]
    wst_re = rev_re[:, :, None, :] * jnp.swapaxes(sp["bb_re"], 1, 2)[:, None] \
        - rev_im[:, :, None, :] * jnp.swapaxes(sp["bb_im"], 1, 2)[:, None]
    wst_im = rev_re[:, :, None, :] * jnp.swapaxes(sp["bb_im"], 1, 2)[:, None] \
        + rev_im[:, :, None, :] * jnp.swapaxes(sp["bb_re"], 1, 2)[:, None]
    wo_re = jnp.transpose(ca_re[:, 1:t + 1], (0, 3, 1, 2))
    wo_im = -jnp.transpose(ca_im[:, 1:t + 1], (0, 3, 1, 2))
    nv, gl = g // S5_LANE_GROUPS, S5_LANE_GROUPS
    kc, wst_re, wst_im, wo_re, wo_im = lax.optimization_barrier((kc, wst_re, wst_im, wo_re, wo_im))
    kc5 =jnp.transpose(kc.reshape(nv, gl, t, c, c), (0, 2, 1, 3, 4))
    ws6 = jnp.transpose(jnp.stack([wst_re, wst_im], axis=3).reshape(nv, gl, t, c, 2, p),
                        (0, 2, 1, 3, 4, 5))
    wo6 = jnp.transpose(jnp.stack([wo_re, wo_im], axis=0).reshape(2, nv, gl, p, t, c),
                        (1, 0, 2, 3, 4, 5))
    kc5, ws6, wo6 = lax.optimization_barrier((kc5.astype(BF16), ws6.astype(BF16), wo6.astype(BF16)))
    gid = jnp.arange(gl)
    zero = jnp.zeros((), BF16)
    same5 = (gid[:, None, None, None] == gid[None, None, :, None])
    bd = jnp.where(same5[None, None], kc5[:, :, :, :, None, :], zero).reshape(nv, t, LANES, LANES)
    same_s = (gid[:, None, None, None, None] == gid[None, None, None, :, None])
    wst_v = jnp.where(same_s[None, None], ws6[:, :, :, :, :, None, :], zero).reshape(nv, t * LANES, 2 * gl * p)
    same_o = (gid[:, None, None, None, None] == gid[None, None, None, :, None])
    wout_v = jnp.where(same_o[None, None], wo6[:, :, :, :, :, None, :], zero).reshape(nv, 2 * gl * p, t * LANES)
    at_re = pw_re[:, t].reshape(1, g * p)
    at_im = pw_im[:, t].reshape(1, g * p)
    return bd, wst_v, wout_v, at_re, at_im


def _s5_chunk_rows(u_ref, nchunk):
    return jnp.concatenate(
        [u_ref[pl.ds(s, nchunk, stride=S5_CHUNK), :] for s in range(S5_CHUNK)], axis=1).astype(BF16)


S5_SLABS = S5_LANE_GROUPS * SSM_STATE // LANES


def _s5_state_kernel(u_ref, wst_ref, sre_ref, sim_ref):
    nchunk = sre_ref.shape[1]
    s = jnp.dot(_s5_chunk_rows(u_ref, nchunk), wst_ref[0], preferred_element_type=F32)
    for k in range(S5_SLABS):
        sre_ref[k] = s[:, k * LANES:(k + 1) * LANES]
        sim_ref[k] = s[:, (S5_SLABS + k) * LANES:(S5_SLABS + k + 1) * LANES]


def _s5_scan_kernel(sre_ref, sim_ref, are_ref, aim_ref, hre_ref, him_ref, fre_ref, fim_ref, *, bsz):
    nchunk = sre_ref.shape[1] // bsz
    are = [jnp.broadcast_to(are_ref[:, k * LANES:(k + 1) * LANES], (bsz, LANES)) for k in range(S5_SLABS)]
    aim = [jnp.broadcast_to(aim_ref[:, k * LANES:(k + 1) * LANES], (bsz, LANES)) for k in range(S5_SLABS)]

    def body(j, carry):
        rows = pl.ds(j, bsz, stride=nchunk)
        out = []
        for k in range(S5_SLABS):
            cre, cim = carry[2 * k], carry[2 * k + 1]
            hre_ref[k, rows, :] = cre
            him_ref[k, rows, :] = cim
            sr = sre_ref[k, rows, :]
            si = sim_ref[k, rows, :]
            out += [are[k] * cre - aim[k] * cim + sr, are[k] * cim + aim[k] * cre + si]
        return tuple(out)

    zero = jnp.zeros((bsz, LANES), F32)
    fin = lax.fori_loop(0, nchunk, body, (zero,) * (2 * S5_SLABS))
    fre_ref[...] = jnp.concatenate(fin[0::2], axis=1)
    fim_ref[...] = jnp.concatenate(fin[1::2], axis=1)


def _s5_out_kernel(u_ref, bd_ref, hre_ref, him_ref, wout_ref, y_ref, m_sc):
    nchunk = hre_ref.shape[1]

    @pl.when(pl.program_id(1) == 0)
    def _():
        for s in range(S5_CHUNK):
            for t in range(S5_CHUNK):
                blk = bd_ref[0, t - s] if t >= s else jnp.zeros((LANES, LANES), BF16)
                m_sc[s * LANES:(s + 1) * LANES, t * LANES:(t + 1) * LANES] = blk

    hcat = jnp.concatenate([hre_ref[k] for k in range(S5_SLABS)] + [him_ref[k] for k in range(S5_SLABS)],
                           axis=1).astype(BF16)
    y = jnp.dot(_s5_chunk_rows(u_ref, nchunk), m_sc[...], preferred_element_type=F32)
    y = y + jnp.dot(hcat, wout_ref[0], preferred_element_type=F32)
    for s in range(S5_CHUNK):
        y_ref[pl.ds(s, nchunk, stride=S5_CHUNK), :] = y[:, s * LANES:(s + 1) * LANES]


def _s5_prompt(u, bsz, seq, mats):
    bd, wst_v, wout_v, at_re, at_im = mats
    g, t, p = N_SSM_GROUPS, S5_CHUNK, SSM_STATE
    nchunk = seq // t
    n = nchunk * bsz
    nv = g // S5_LANE_GROUPS
    half = S5_LANE_GROUPS * p
    s_re, s_im = pl.pallas_call(
        _s5_state_kernel,
        grid=(nv, bsz),
        in_specs=[pl.BlockSpec((seq, LANES), lambda v, b: (b, v)),
                  pl.BlockSpec((1, t * LANES, 2 * half), lambda v, b: (v, 0, 0))],
        out_specs=[pl.BlockSpec((S5_SLABS, nchunk, LANES), lambda v, b: (v, b, 0)),
                   pl.BlockSpec((S5_SLABS, nchunk, LANES), lambda v, b: (v, b, 0))],
        out_shape=[jax.ShapeDtypeStruct((nv * S5_SLABS, n, LANES), F32)] * 2,
        compiler_params=_cparams(("parallel", "parallel"), VMEM_LIMIT),
        name="s5_state",
    )(u, wst_v)
    h_re, h_im, f_re, f_im = pl.pallas_call(
        functools.partial(_s5_scan_kernel, bsz=bsz),
        grid=(nv,),
        in_specs=[pl.BlockSpec((S5_SLABS, n, LANES), lambda i: (i, 0, 0)),
                  pl.BlockSpec((S5_SLABS, n, LANES), lambda i: (i, 0, 0)),
                  pl.BlockSpec((1, half), lambda i: (0, i)),
                  pl.BlockSpec((1, half), lambda i: (0, i))],
        out_specs=[pl.BlockSpec((S5_SLABS, n, LANES), lambda i: (i, 0, 0)),
                   pl.BlockSpec((S5_SLABS, n, LANES), lambda i: (i, 0, 0)),
                   pl.BlockSpec((bsz, half), lambda i: (0, i)),
                   pl.BlockSpec((bsz, half), lambda i: (0, i))],
        out_shape=[jax.ShapeDtypeStruct((nv * S5_SLABS, n, LANES), F32)] * 2
        + [jax.ShapeDtypeStruct((bsz, g * p), F32)] * 2,
        compiler_params=_cparams(("parallel",)),
        name="s5_scan",
    )(s_re, s_im, at_re, at_im)
    y = pl.pallas_call(
        _s5_out_kernel,
        grid=(nv, bsz),
        in_specs=[pl.BlockSpec((seq, LANES), lambda v, b: (b, v)),
                  pl.BlockSpec((1, t, LANES, LANES), lambda v, b: (v, 0, 0, 0)),
                  pl.BlockSpec((S5_SLABS, nchunk, LANES), lambda v, b: (v, b, 0)),
                  pl.BlockSpec((S5_SLABS, nchunk, LANES), lambda v, b: (v, b, 0)),
                  pl.BlockSpec((1, 2 * half, t * LANES), lambda v, b: (v, 0, 0))],
        out_specs=pl.BlockSpec((seq, LANES), lambda v, b: (b, v)),
        out_shape=jax.ShapeDtypeStruct((bsz * seq, D_SSM), F32),
        scratch_shapes=[pltpu.VMEM((t * LANES, t * LANES), BF16)],
        compiler_params=_cparams(("parallel", "arbitrary"), VMEM_LIMIT),
        name="s5_out",
    )(u, bd, h_re, h_im, wout_v)
    return y, f_re, f_im


S5S_GROUPS = LANES // SSM_GROUP


def _s5_sample_mats(sp, d_skip):
    go, gl, c, p = N_SSM_GROUPS // S5S_GROUPS, S5S_GROUPS, SSM_GROUP, SSM_STATE
    eye = jnp.eye(gl, dtype=F32)

    def bdiag_in(b):
        b4 = b.reshape(go, gl, p, c)
        return jnp.einsum("ogpc,gh->ogchp", b4, eye).reshape(go, gl * c, gl * p)

    def bdiag_out(cm):
        c4 = cm.reshape(go, gl, c, p)
        return jnp.einsum("ogcp,gh->ogphc", c4, eye).reshape(go, gl * p, gl * c)

    b8 = jnp.concatenate([bdiag_in(sp["bb_re"]), bdiag_in(sp["bb_im"])], axis=2)
    c8 = jnp.concatenate([bdiag_out(sp["c_re"]), -bdiag_out(sp["c_im"])], axis=1)
    a_re = sp["ab_re"].reshape(1, N_SSM_GROUPS * p)
    a_im = sp["ab_im"].reshape(1, N_SSM_GROUPS * p)
    return b8, c8, a_re, a_im, d_skip.astype(F32).reshape(1, D_SSM)


def _s5_sample_kernel(u_ref, hre_ref, him_ref, b8_ref, c8_ref, are_ref, aim_ref, d_ref,
                      y_ref, ore_ref, oim_ref):
    hp = lax.Precision.HIGHEST
    u = u_ref[...]
    half = S5S_GROUPS * SSM_STATE
    bu = jnp.dot(u, b8_ref[0], preferred_element_type=F32, precision=hp)
    are, aim = are_ref[...], aim_ref[...]
    h0r, h0i = hre_ref[...], him_ref[...]
    hr = are * h0r - aim * h0i + bu[:, :half]
    hi = are * h0i + aim * h0r + bu[:, half:]
    ore_ref[...] = hr
    oim_ref[...] = hi
    y = jnp.dot(jnp.concatenate([hr, hi], axis=1), c8_ref[0], preferred_element_type=F32, precision=hp)
    y_ref[...] = (y + d_ref[...] * u).astype(y_ref.dtype)


def _s5_sample(u, h0_re, h0_im, mats):
    b8, c8, a_re, a_im, d = mats
    n = u.shape[0]
    half = S5S_GROUPS * SSM_STATE
    return pl.pallas_call(
        _s5_sample_kernel,
        grid=(N_SSM_GROUPS // S5S_GROUPS,),
        in_specs=[pl.BlockSpec((n, LANES), lambda i: (0, i)),
                  pl.BlockSpec((n, half), lambda i: (0, i)),
                  pl.BlockSpec((n, half), lambda i: (0, i)),
                  pl.BlockSpec((1, LANES, 2 * half), lambda i: (i, 0, 0)),
                  pl.BlockSpec((1, 2 * half, LANES), lambda i: (i, 0, 0)),
                  pl.BlockSpec((1, half), lambda i: (0, i)),
                  pl.BlockSpec((1, half), lambda i: (0, i)),
                  pl.BlockSpec((1, LANES), lambda i: (0, i))],
        out_specs=[pl.BlockSpec((n, LANES), lambda i: (0, i)),
                   pl.BlockSpec((n, half), lambda i: (0, i)),
                   pl.BlockSpec((n, half), lambda i: (0, i))],
        out_shape=[jax.ShapeDtypeStruct((n, D_SSM), BF16),
                   jax.ShapeDtypeStruct((n, N_SSM_GROUPS * SSM_STATE), F32),
                   jax.ShapeDtypeStruct((n, N_SSM_GROUPS * SSM_STATE), F32)],
        compiler_params=_cparams(("parallel",)),
        name="s5_sample",
    )(u, h0_re, h0_im, b8, c8, a_re, a_im, d)


def _layer_norm(x, g, b):
    mu = jnp.mean(x, axis=-1, keepdims=True)
    xc = x - mu
    var = jnp.mean(xc * xc, axis=-1, keepdims=True)
    return xc * lax.rsqrt(var + LN_EPS) * g + b


def _merge_kernel(x_ref, oa_ref, ys_ref, cnt_in_ref, wao_ref, wso_ref, wg_ref, bg_ref, wo_ref,
                  g1_ref, b1_ref, wr_ref, br_ref,
                  x1_ref, eidx_ref, gate_ref, rank_ref, cnt_out_ref, cnt_sc):
    step = pl.program_id(0)

    @pl.when(step == 0)
    def _():
        cnt_sc[...] = cnt_in_ref[...]

    tm = x_ref.shape[0]
    x = x_ref[...]
    branch_a = jnp.dot(oa_ref[...].astype(BF16), wao_ref[...], preferred_element_type=F32)
    z = jnp.dot(jax.nn.gelu(ys_ref[...].astype(F32)).astype(BF16), wso_ref[...], preferred_element_type=F32)
    branch_b = z[:, :D_MODEL] * jax.nn.sigmoid(z[:, D_MODEL:])
    gates = jax.nn.sigmoid(jnp.dot(x.astype(BF16), wg_ref[...], preferred_element_type=F32) + bg_ref[...])
    mixed = gates[:, :D_MODEL] * branch_a + gates[:, D_MODEL:] * branch_b
    mix = jnp.dot(mixed.astype(BF16), wo_ref[...], preferred_element_type=F32)
    x1 = _layer_norm(DEEPNORM_ALPHA * x + mix, g1_ref[...], b1_ref[...])
    x1_ref[...] = x1

    logits = jnp.dot(x1, wr_ref[...], preferred_element_type=F32, precision=lax.Precision.HIGHEST) + br_ref[...]
    lane = lax.broadcasted_iota(jnp.int32, (tm, N_EXPERTS), 1)
    work = logits
    vals, sels, idxs = [], [], []
    for _ in range(TOP_K):
        mx = jnp.max(work, axis=-1, keepdims=True)
        idx = jnp.min(jnp.where(work == mx, lane, N_EXPERTS), axis=-1, keepdims=True)
        sel = lane == idx
        vals.append(mx)
        idxs.append(idx)
        sels.append(sel)
        work = jnp.where(sel, -jnp.inf, work)
    ex = [jnp.exp(v - vals[0]) for v in vals]
    tot = ex[0] + ex[1] + ex[2] + ex[3]
    gate_ref[...] = jnp.concatenate([e / tot for e in ex], axis=1)
    eidx_ref[...] = jnp.concatenate(idxs, axis=1)

    multi = jnp.zeros((tm, N_EXPERTS), F32)
    for sel in sels:
        multi = multi + sel.astype(F32)
    r = lax.broadcasted_iota(jnp.int32, (tm, tm), 0)
    cc = lax.broadcasted_iota(jnp.int32, (tm, tm), 1)
    tri = (cc < r).astype(BF16)
    before = jnp.dot(tri, multi.astype(BF16), preferred_element_type=F32) + cnt_sc[...]
    ranks = [jnp.sum(jnp.where(sel, before, 0.0), axis=-1, keepdims=True) for sel in sels]
    rank_ref[...] = jnp.concatenate(ranks, axis=1).astype(jnp.int32)
    cnt_sc[...] = cnt_sc[...] + jnp.sum(multi, axis=0, keepdims=True)
    cnt_out_ref[...] = cnt_sc[...]


def _merge(x, o_attn, y_ssm, cnt_in, w, *, tile):
    n = x.shape[0]
    full = lambda shape: pl.BlockSpec(shape, lambda i: (0,) * len(shape))
    return pl.pallas_call(
        _merge_kernel,
        grid=(n // tile,),
        in_specs=[pl.BlockSpec((tile, D_MODEL), lambda i: (i, 0)),
                  pl.BlockSpec((tile, D_ATTN), lambda i: (i, 0)),
                  pl.BlockSpec((tile, D_SSM), lambda i: (i, 0)),
                  full((1, N_EXPERTS)),
                  full((D_ATTN, D_MODEL)), full((D_SSM, 2 * D_MODEL)), full((D_MODEL, 2 * D_MODEL)),
                  full((1, 2 * D_MODEL)), full((D_MODEL, D_MODEL)),
                  full((1, D_MODEL)), full((1, D_MODEL)),
                  full((D_MODEL, N_EXPERTS)), full((1, N_EXPERTS))],
        out_specs=[pl.BlockSpec((tile, D_MODEL), lambda i: (i, 0)),
                   pl.BlockSpec((tile, TOP_K), lambda i: (i, 0)),
                   pl.BlockSpec((tile, TOP_K), lambda i: (i, 0)),
                   pl.BlockSpec((tile, TOP_K), lambda i: (i, 0)),
                   full((1, N_EXPERTS))],
        out_shape=[jax.ShapeDtypeStruct((n, D_MODEL), F32),
                   jax.ShapeDtypeStruct((n, TOP_K), jnp.int32),
                   jax.ShapeDtypeStruct((n, TOP_K), F32),
                   jax.ShapeDtypeStruct((n, TOP_K), jnp.int32),
                   jax.ShapeDtypeStruct((1, N_EXPERTS), F32)],
        scratch_shapes=[pltpu.VMEM((1, N_EXPERTS), F32)],
        compiler_params=_cparams(("arbitrary",), VMEM_LIMIT),
        name="merge",
    )(x, o_attn, y_ssm, cnt_in, w["wao"], w["wso"], w["wg"], w["bg"], w["wo"], w["g1"], w["b1"], w["wr"], w["br"])


def _dispatch_kernel(pstart_ref, eidx_ref, rank_ref, x_ref, xs_in_ref, xs_ref, sem):
    del xs_in_ref
    tm = x_ref.shape[0]

    def row_copy(t, k):
        j = t * TOP_K + k
        dest = pstart_ref[eidx_ref[j]] + rank_ref[j]
        return pltpu.make_async_copy(x_ref.at[pl.ds(t, 1)], xs_ref.at[pl.ds(dest, 1)], sem)

    def issue(t, c):
        for k in range(TOP_K):
            row_copy(t, k).start()
        return c

    lax.fori_loop(0, tm, issue, 0)
    for _ in range(TOP_K):
        pltpu.make_async_copy(x_ref, xs_ref.at[pl.ds(0, tm)], sem).wait()


def _dispatch(pstart, eidx_flat, rank_flat, x1, xs, *, tile):
    n = x1.shape[0]
    nrows = xs.shape[0]
    return pl.pallas_call(
        _dispatch_kernel,
        grid_spec=pltpu.PrefetchScalarGridSpec(
            num_scalar_prefetch=1,
            grid=(n // tile,),
            in_specs=[pl.BlockSpec((tile * TOP_K,), lambda i, ps: (i,), memory_space=pltpu.SMEM),
                      pl.BlockSpec((tile * TOP_K,), lambda i, ps: (i,), memory_space=pltpu.SMEM),
                      pl.BlockSpec((tile, D_MODEL), lambda i, ps: (i, 0)),
                      pl.BlockSpec(memory_space=pl.ANY)],
            out_specs=pl.BlockSpec(memory_space=pl.ANY),
            scratch_shapes=[pltpu.SemaphoreType.DMA(())]),
        out_shape=jax.ShapeDtypeStruct((nrows, D_MODEL), F32),
        input_output_aliases={4: 0},
        compiler_params=_cparams(("arbitrary",)),
        name="dispatch",
    )(pstart, eidx_flat, rank_flat, x1, xs)


def _deinterleave_matrix():
    pm = np.zeros((MXU_DIM, MXU_DIM), np.float32)
    half = MXU_DIM // 2
    for c in range(half):
        pm[2 * c, c] = 1.0
        pm[2 * c + 1, half + c] = 1.0
    return pm


def _expert_kernel(be_ref, nu_ref, ord_ref, nxt_ref, xs_ref, w1_hbm, b1_ref, w2_hbm, b2_ref, pm_ref, y_ref,
                   w1f_sc, w2f_sc, w1p_sc, w2b_sc, sem):
    i = pl.program_id(0)
    e = be_ref[i]
    prev = be_ref[jnp.maximum(i - 1, 0)]
    nblk = 2 * D_FF // MXU_DIM

    def weight_copies(expert, slot):
        return (pltpu.make_async_copy(w1_hbm.at[expert], w1f_sc.at[slot], sem.at[0, slot]),
                pltpu.make_async_copy(w2_hbm.at[expert], w2f_sc.at[slot], sem.at[1, slot]))

    @pl.when(i == 0)
    def _():
        for cp in weight_copies(e, 0):
            cp.start()

    @pl.when((i == 0) | (e != prev))
    def _():
        slot = ord_ref[i] % 2
        for cp in weight_copies(e, slot):
            cp.wait()
        nxt = nxt_ref[i]

        @pl.when(nxt >= 0)
        def _():
            for cp in weight_copies(nxt, 1 - slot):
                cp.start()

        for cb in range(nblk):
            blk = w1f_sc[slot, :, cb * MXU_DIM:(cb + 1) * MXU_DIM].astype(BF16)
            w1p_sc[:, cb * MXU_DIM:(cb + 1) * MXU_DIM] = jnp.dot(
                blk, pm_ref[...], preferred_element_type=F32).astype(BF16)
        w2b_sc[...] = w2f_sc[slot].astype(BF16)

    @pl.when(i < nu_ref[0])
    def _():
        x = xs_ref[...].astype(BF16)
        h = jnp.dot(x, w1p_sc[...], preferred_element_type=F32) + b1_ref[0]
        half = MXU_DIM // 2
        acts = []
        for cb in range(nblk):
            x_glu = jnp.minimum(h[:, cb * MXU_DIM:cb * MXU_DIM + half], SWIGLU_LIMIT)
            x_lin = jnp.clip(h[:, cb * MXU_DIM + half:(cb + 1) * MXU_DIM], -SWIGLU_LIMIT, SWIGLU_LIMIT)
            acts.append((x_glu * jax.nn.sigmoid(SWIGLU_ALPHA * x_glu) * (x_lin + 1.0)).astype(BF16))
        act = jnp.concatenate(acts, axis=1)
        y_ref[...] = jnp.dot(act, w2b_sc[...], preferred_element_type=F32) + b2_ref[0]

    @pl.when(i >= nu_ref[0])
    def _():
        y_ref[...] = jnp.zeros_like(y_ref)


def _experts(block_e, n_used, run_ord, run_next, xs, w1, b1p, w2, b2, pm):
    nrows = xs.shape[0]
    nb = nrows // MOE_ROWS
    return pl.pallas_call(
        _expert_kernel,
        grid_spec=pltpu.PrefetchScalarGridSpec(
            num_scalar_prefetch=4,
            grid=(nb,),
            in_specs=[pl.BlockSpec((MOE_ROWS, D_MODEL), lambda i, be, nu, ro, rn: (jnp.minimum(i, nu[0] - 1), 0)),
                      pl.BlockSpec(memory_space=pl.ANY),
                      pl.BlockSpec((1, 1, 2 * D_FF), lambda i, be, nu, ro, rn: (be[i], 0, 0)),
                      pl.BlockSpec(memory_space=pl.ANY),
                      pl.BlockSpec((1, 1, D_MODEL), lambda i, be, nu, ro, rn: (be[i], 0, 0)),
                      pl.BlockSpec((MXU_DIM, MXU_DIM), lambda i, be, nu, ro, rn: (0, 0))],
            out_specs=pl.BlockSpec((MOE_ROWS, D_MODEL), lambda i, be, nu, ro, rn: (i, 0)),
            scratch_shapes=[pltpu.VMEM((2, D_MODEL, 2 * D_FF), F32), pltpu.VMEM((2, D_FF, D_MODEL), F32),
                            pltpu.VMEM((D_MODEL, 2 * D_FF), BF16), pltpu.VMEM((D_FF, D_MODEL), BF16),
                            pltpu.SemaphoreType.DMA((2, 2))]),
        out_shape=jax.ShapeDtypeStruct((nrows, D_MODEL), F32),
        compiler_params=_cparams(("arbitrary",), VMEM_LIMIT),
        name="experts",
    )(block_e, n_used, run_ord, run_next, xs, w1, b1p, w2, b2, pm)


def _combine_kernel(pstart_ref, eidx_ref, rank_ref, x1_ref, gate_ref, g2_ref, b2_ref, ys_ref, y_ref, rows, sem):
    tm = x1_ref.shape[0]

    def row_copy(t, k):
        j = t * TOP_K + k
        src = pstart_ref[eidx_ref[j]] + rank_ref[j]
        return pltpu.make_async_copy(ys_ref.at[pl.ds(src, 1)], rows.at[k, pl.ds(t, 1)], sem)

    def issue(t, c):
        for k in range(TOP_K):
            row_copy(t, k).start()
        return c

    lax.fori_loop(0, tm, issue, 0)
    for k in range(TOP_K):
        pltpu.make_async_copy(ys_ref.at[pl.ds(0, tm)], rows.at[k], sem).wait()

    gate = gate_ref[...]
    ffn = gate[:, 0:1] * rows[0]
    for k in range(1, TOP_K):
        ffn = ffn + gate[:, k:k + 1] * rows[k]
    y_ref[...] = _layer_norm(DEEPNORM_ALPHA * x1_ref[...] + ffn, g2_ref[...], b2_ref[...])


def _combine(pstart, eidx_flat, rank_flat, x1, gate, g2, b2, ys, *, tile):
    n = x1.shape[0]
    return pl.pallas_call(
        _combine_kernel,
        grid_spec=pltpu.PrefetchScalarGridSpec(
            num_scalar_prefetch=1,
            grid=(n // tile,),
            in_specs=[pl.BlockSpec((tile * TOP_K,), lambda i, ps: (i,), memory_space=pltpu.SMEM),
                      pl.BlockSpec((tile * TOP_K,), lambda i, ps: (i,), memory_space=pltpu.SMEM),
                      pl.BlockSpec((tile, D_MODEL), lambda i, ps: (i, 0)),
                      pl.BlockSpec((tile, TOP_K), lambda i, ps: (i, 0)),
                      pl.BlockSpec((1, D_MODEL), lambda i, ps: (0, 0)),
                      pl.BlockSpec((1, D_MODEL), lambda i, ps: (0, 0)),
                      pl.BlockSpec(memory_space=pl.ANY)],
            out_specs=pl.BlockSpec((tile, D_MODEL), lambda i, ps: (i, 0)),
            scratch_shapes=[pltpu.VMEM((TOP_K, tile, D_MODEL), F32), pltpu.SemaphoreType.DMA(())]),
        out_shape=jax.ShapeDtypeStruct((n, D_MODEL), F32),
        compiler_params=_cparams(("arbitrary",)),
        name="combine",
    )(pstart, eidx_flat, rank_flat, x1, gate, g2, b2, ys)


def kernel(x_prompt, x_sample, cache_k_win, cache_v_win, state_ssm_re, state_ssm_im, w_in, b_in, attn_sinks,
           w_attn_out, ssm_a_re, ssm_a_im, ssm_log_dt, ssm_b_re, ssm_b_im, ssm_c_re, ssm_c_im, ssm_d, w_ssm_out,
           w_gate, b_gate, w_out, ln1_g, ln1_b, w_router, b_router, w_exp1, b_exp1, w_exp2, b_exp2, ln2_g, ln2_b):
    assert w_in.shape[0] == DEPTH == 1
    bsz, seq, _ = x_prompt.shape
    nsamp = x_sample.shape[0]
    assert x_sample.shape[1] == 1
    n_p = bsz * seq
    n_tok = n_p + nsamp

    xp = x_prompt.reshape(n_p, D_MODEL)
    xsm = x_sample.reshape(nsamp, D_MODEL)
    b_in2 = b_in[0].reshape(1, D_IN)
    sinks = attn_sinks[0].astype(F32)

    q_p, k_p, v_p, u_p = _proj(xp, w_in[0].astype(BF16), b_in2, tile=512, exact_f32=False, q_dtype=BF16)
    q_s, k_s, v_s, u_s = _proj(xsm, w_in[0], b_in2, tile=nsamp, exact_f32=True, q_dtype=F32)

    o_p = _attn_prompt(sinks, q_p.reshape(bsz, seq, D_ATTN), k_p.reshape(bsz, seq, D_KV),
                       v_p.reshape(bsz, seq, D_KV)).reshape(n_p, D_ATTN)
    k_buf = cache_k_win[0].reshape(nsamp, WINDOW, D_KV)
    v_buf = cache_v_win[0].reshape(nsamp, WINDOW, D_KV)
    o_s = _attn_sample(sinks, q_s, k_s, v_s, k_buf, v_buf)

    sp = _s5_params(ssm_a_re[0], ssm_a_im[0], ssm_log_dt[0], ssm_b_re[0], ssm_b_im[0], ssm_c_re[0], ssm_c_im[0])
    y_p, hp_re, hp_im = _s5_prompt(u_p, bsz, seq, _s5_chunk_mats(sp, ssm_d[0]))
    y_s, hs_re, hs_im = _s5_sample(u_s, state_ssm_re[0].reshape(nsamp, -1), state_ssm_im[0].reshape(nsamp, -1),
                                   _s5_sample_mats(sp, ssm_d[0]))

    wm = dict(wao=w_attn_out[0].astype(BF16), wso=w_ssm_out[0].astype(BF16), wg=w_gate[0].astype(BF16),
              bg=b_gate[0].reshape(1, -1), wo=w_out[0].astype(BF16), g1=ln1_g[0].reshape(1, -1),
              b1=ln1_b[0].reshape(1, -1), wr=w_router[0], br=b_router[0].reshape(1, -1))
    cnt0 = jnp.zeros((1, N_EXPERTS), F32)
    x1_p, e_p, g_p, r_p, cnt1 = _merge(xp, o_p, y_p, cnt0, wm, tile=TOK_TILE)
    x1_s, e_s, g_s, r_s, cnt2 = _merge(xsm, o_s, y_s, cnt1, wm, tile=nsamp)

    counts = cnt2[0].astype(jnp.int32)
    padded = ((counts + MOE_ROWS - 1) // MOE_ROWS) * MOE_ROWS
    pad_end = jnp.cumsum(padded)
    pad_start = (pad_end - padded).astype(jnp.int32)
    nb_max = (n_tok * TOP_K + N_EXPERTS * (MOE_ROWS - 1) + MOE_ROWS - 1) // MOE_ROWS
    n_used = (pad_end[-1] // MOE_ROWS).astype(jnp.int32)
    blk_start = jnp.arange(nb_max, dtype=jnp.int32) * MOE_ROWS
    blk_e = jnp.minimum(jnp.sum(blk_start[:, None] >= pad_end[None, :], axis=1), N_EXPERTS - 1).astype(jnp.int32)
    blk_e = jnp.where(jnp.arange(nb_max) < n_used, blk_e, blk_e[jnp.maximum(n_used - 1, 0)])
    new_run = jnp.concatenate([jnp.ones((1,), jnp.int32), (blk_e[1:] != blk_e[:-1]).astype(jnp.int32)])
    run_ord = (jnp.cumsum(new_run) - 1).astype(jnp.int32)
    ids = jnp.arange(N_EXPERTS, dtype=jnp.int32)
    later = (ids[None, :] > ids[:, None]) & (padded > 0)[None, :]
    next_e = jnp.min(jnp.where(later, ids[None, :], N_EXPERTS), axis=1)
    next_e = jnp.where(next_e < N_EXPERTS, next_e, -1).astype(jnp.int32)
    run_next = next_e[blk_e]

    xs = jnp.zeros((nb_max * MOE_ROWS, D_MODEL), F32)
    xs = _dispatch(pad_start, e_p.reshape(-1), r_p.reshape(-1), x1_p, xs, tile=TOK_TILE)
    xs = _dispatch(pad_start, e_s.reshape(-1), r_s.reshape(-1), x1_s, xs, tile=nsamp)

    b1p = b_exp1[0].reshape(N_EXPERTS, 2 * D_FF // MXU_DIM, MXU_DIM // 2, 2)
    b1p = jnp.swapaxes(b1p, 2, 3).reshape(N_EXPERTS, 1, 2 * D_FF)
    ys = _experts(blk_e, n_used.reshape(1), run_ord, run_next, xs, w_exp1[0], b1p, w_exp2[0],
                  b_exp2[0].reshape(N_EXPERTS, 1, D_MODEL),
                  jnp.asarray(_deinterleave_matrix(), BF16))

    g2, b2 = ln2_g[0].reshape(1, -1), ln2_b[0].reshape(1, -1)
    y_prompt = _combine(pad_start, e_p.reshape(-1), r_p.reshape(-1), x1_p, g_p, g2, b2, ys, tile=TOK_TILE)
    y_sample = _combine(pad_start, e_s.reshape(-1), r_s.reshape(-1), x1_s, g_s, g2, b2, ys, tile=nsamp)

    k_p4 = k_p.reshape(bsz, seq, N_KV_HEADS, HEAD_DIM)[:, -WINDOW:]
    v_p4 = v_p.reshape(bsz, seq, N_KV_HEADS, HEAD_DIM)[:, -WINDOW:]
    k_s4 = jnp.concatenate([cache_k_win[0][:, 1:], k_s.reshape(nsamp, 1, N_KV_HEADS, HEAD_DIM)], axis=1)
    v_s4 = jnp.concatenate([cache_v_win[0][:, 1:], v_s.reshape(nsamp, 1, N_KV_HEADS, HEAD_DIM)], axis=1)
    st = lambda a, n: a.reshape(1, n, N_SSM_GROUPS, SSM_STATE)
    return (y_prompt.reshape(bsz, seq, D_MODEL), y_sample.reshape(nsamp, 1, D_MODEL),
            k_p4[None], v_p4[None], st(hp_re, bsz), st(hp_im, bsz),
            k_s4[None], v_s4[None], st(hs_re, nsamp), st(hs_im, nsamp))
```

```python
import functools
import math

import numpy as np
import jax
import jax.numpy as jnp
from jax import lax
from jax.experimental import pallas as pl
from jax.experimental.pallas import tpu as pltpu

F32 = jnp.float32
BF16 = jnp.bfloat16

D_MODEL = 1024
HEAD_DIM = 64
N_Q_HEADS = 8
N_KV_HEADS = 2
Q_PER_KV = N_Q_HEADS // N_KV_HEADS
D_ATTN = N_Q_HEADS * HEAD_DIM
D_KV = N_KV_HEADS * HEAD_DIM
WINDOW = 128
ATTN_SCALE = HEAD_DIM ** -0.5
SSM_GROUP = 16
D_SSM = D_MODEL // 2
N_SSM_GROUPS = D_SSM // SSM_GROUP
SSM_STATE = 64
D_IN = D_ATTN + 2 * D_KV + D_SSM
N_EXPERTS = 32
TOP_K = 4
D_FF = D_MODEL
SWIGLU_LIMIT = 7.0
SWIGLU_ALPHA = 1.702
LN_EPS = 1e-5
DEPTH = 1
DEEPNORM_ALPHA = (2 * DEPTH) ** 0.25

LANES = 128
SUBLANES = 8
MXU_DIM = 256

S5_CHUNK = MXU_DIM // SSM_GROUP
S5_LANE_GROUPS = LANES // SSM_GROUP
MOE_ROWS = 256
TOK_TILE = 256
VMEM_LIMIT = 48 * 1024 * 1024


def _cparams(sem, vmem=None):
    return pltpu.CompilerParams(dimension_semantics=sem, vmem_limit_bytes=vmem)


def _proj_kernel(x_ref, w_ref, b_ref, q_ref, k_ref, v_ref, u_ref, *, exact_f32):
    if exact_f32:
        h = jnp.dot(x_ref[...], w_ref[...], preferred_element_type=F32, precision=lax.Precision.HIGHEST)
    else:
        h = jnp.dot(x_ref[...].astype(BF16), w_ref[...], preferred_element_type=F32)
    h = h + b_ref[...]
    q_ref[...] = (h[:, :D_ATTN] * ATTN_SCALE).astype(q_ref.dtype)
    k_ref[...] = h[:, D_ATTN:D_ATTN + D_KV]
    v_ref[...] = h[:, D_ATTN + D_KV:D_ATTN + 2 * D_KV]
    u_ref[...] = h[:, D_ATTN + 2 * D_KV:].astype(u_ref.dtype)


def _proj(x, w, b, *, tile, exact_f32, q_dtype):
    n = x.shape[0]
    return pl.pallas_call(
        functools.partial(_proj_kernel, exact_f32=exact_f32),
        grid=(n // tile,),
        in_specs=[pl.BlockSpec((tile, D_MODEL), lambda i: (i, 0)),
                  pl.BlockSpec((D_MODEL, D_IN), lambda i: (0, 0)),
                  pl.BlockSpec((1, D_IN), lambda i: (0, 0))],
        out_specs=[pl.BlockSpec((tile, D_ATTN), lambda i: (i, 0)),
                   pl.BlockSpec((tile, D_KV), lambda i: (i, 0)),
                   pl.BlockSpec((tile, D_KV), lambda i: (i, 0)),
                   pl.BlockSpec((tile, D_SSM), lambda i: (i, 0))],
        out_shape=[jax.ShapeDtypeStruct((n, D_ATTN), q_dtype),
                   jax.ShapeDtypeStruct((n, D_KV), F32),
                   jax.ShapeDtypeStruct((n, D_KV), F32),
                   jax.ShapeDtypeStruct((n, D_SSM), F32)],
        compiler_params=_cparams(("parallel",)),
        name="proj",
    )(x, w, b)


ATT_Q_TILE = 512


def _attn_prompt_kernel(sink_ref, q_ref, k_ref, v_ref, o_ref):
    i = pl.program_id(1)
    for blk in range(ATT_Q_TILE // WINDOW):
        q0 = i * ATT_Q_TILE + blk * WINDOW
        k0 = pl.multiple_of(jnp.maximum(q0 - WINDOW, 0), WINDOW)
        kk = k_ref[0, pl.ds(k0, 2 * WINDOW), :].astype(BF16)
        vv = v_ref[0, pl.ds(k0, 2 * WINDOW), :].astype(BF16)
        qb = q_ref[0, blk * WINDOW:(blk + 1) * WINDOW, :]
        qpos = q0 + lax.broadcasted_iota(jnp.int32, (WINDOW, 2 * WINDOW), 0)
        kpos = k0 + lax.broadcasted_iota(jnp.int32, (WINDOW, 2 * WINDOW), 1)
        valid = (kpos <= qpos) & (qpos - kpos <= WINDOW)
        for h in range(N_Q_HEADS):
            kv = h // Q_PER_KV
            qh = qb[:, h * HEAD_DIM:(h + 1) * HEAD_DIM]
            kh = kk[:, kv * HEAD_DIM:(kv + 1) * HEAD_DIM]
            vh = vv[:, kv * HEAD_DIM:(kv + 1) * HEAD_DIM]
            s = lax.dot_general(qh, kh, (((1,), (1,)), ((), ())), preferred_element_type=F32)
            s = jnp.where(valid, s, -jnp.inf)
            sink = sink_ref[h]
            m = jnp.maximum(jnp.max(s, axis=-1, keepdims=True), sink)
            p = jnp.exp(s - m)
            denom = jnp.sum(p, axis=-1, keepdims=True) + jnp.exp(sink - m)
            o = jnp.dot(p.astype(BF16), vh, preferred_element_type=F32) / denom
            o_ref[0, blk * WINDOW:(blk + 1) * WINDOW, h * HEAD_DIM:(h + 1) * HEAD_DIM] = o.astype(o_ref.dtype)


def _attn_prompt(sinks, q, k, v):
    bsz, seq = q.shape[0], q.shape[1]
    return pl.pallas_call(
        _attn_prompt_kernel,
        grid=(bsz, seq // ATT_Q_TILE),
        in_specs=[pl.BlockSpec(memory_space=pltpu.SMEM),
                  pl.BlockSpec((1, ATT_Q_TILE, D_ATTN), lambda b, i: (b, i, 0)),
                  pl.BlockSpec((1, seq, D_KV), lambda b, i: (b, 0, 0)),
                  pl.BlockSpec((1, seq, D_KV), lambda b, i: (b, 0, 0))],
        out_specs=pl.BlockSpec((1, ATT_Q_TILE, D_ATTN), lambda b, i: (b, i, 0)),
        out_shape=jax.ShapeDtypeStruct((bsz, seq, D_ATTN), BF16),
        compiler_params=_cparams(("parallel", "parallel")),
        name="attn_prompt",
    )(sinks, q, k, v)


ATT_S_GROUP = 8


def _attn_sample_kernel(sink_ref, q_ref, kn_ref, vn_ref, kb_ref, vb_ref, o_ref):
    g = ATT_S_GROUP
    rows = Q_PER_KV * g
    ncol = g * WINDOW
    kb = kb_ref[...].reshape(ncol, D_KV).astype(BF16)
    vb = vb_ref[...].reshape(ncol, D_KV).astype(BF16)
    rseq = lax.broadcasted_iota(jnp.int32, (rows, ncol), 0) % g
    cseq = lax.broadcasted_iota(jnp.int32, (rows, ncol), 1) // WINDOW
    own = rseq == cseq
    rhead = lax.broadcasted_iota(jnp.int32, (rows, 1), 0) // g
    for kv in range(N_KV_HEADS):
        lo = kv * HEAD_DIM
        qs = jnp.concatenate(
            [q_ref[:, (kv * Q_PER_KV + h) * HEAD_DIM:(kv * Q_PER_KV + h + 1) * HEAD_DIM] for h in range(Q_PER_KV)],
            axis=0)
        kn = jnp.concatenate([kn_ref[:, lo:lo + HEAD_DIM]] * Q_PER_KV, axis=0)
        vn = jnp.concatenate([vn_ref[:, lo:lo + HEAD_DIM]] * Q_PER_KV, axis=0)
        sink = jnp.zeros((rows, 1), F32)
        for h in range(Q_PER_KV):
            sink = jnp.where(rhead == h, sink_ref[kv * Q_PER_KV + h], sink)
        qs = qs.astype(BF16)
        s = lax.dot_general(qs, kb[:, lo:lo + HEAD_DIM], (((1,), (1,)), ((), ())), preferred_element_type=F32)
        s = jnp.where(own, s, -jnp.inf)
        s_new = jnp.sum(qs.astype(F32) * kn.astype(BF16).astype(F32), axis=-1, keepdims=True)
        m = jnp.maximum(jnp.maximum(jnp.max(s, axis=-1, keepdims=True), s_new), sink)
        p = jnp.exp(s - m)
        p_new = jnp.exp(s_new - m)
        denom = jnp.sum(p, axis=-1, keepdims=True) + p_new + jnp.exp(sink - m)
        o = jnp.dot(p.astype(BF16), vb[:, lo:lo + HEAD_DIM], preferred_element_type=F32)
        o = (o + p_new.astype(BF16).astype(F32) * vn.astype(BF16).astype(F32)) / denom
        for h in range(Q_PER_KV):
            c0 = (kv * Q_PER_KV + h) * HEAD_DIM
            o_ref[:, c0:c0 + HEAD_DIM] = o[h * g:(h + 1) * g].astype(o_ref.dtype)


def _attn_sample(sinks, q, k_new, v_new, k_buf, v_buf):
    n = q.shape[0]
    g = ATT_S_GROUP
    return pl.pallas_call(
        _attn_sample_kernel,
        grid=(n // g,),
        in_specs=[pl.BlockSpec(memory_space=pltpu.SMEM),
                  pl.BlockSpec((g, D_ATTN), lambda i: (i, 0)),
                  pl.BlockSpec((g, D_KV), lambda i: (i, 0)),
                  pl.BlockSpec((g, D_KV), lambda i: (i, 0)),
                  pl.BlockSpec((g, WINDOW, D_KV), lambda i: (i, 0, 0)),
                  pl.BlockSpec((g, WINDOW, D_KV), lambda i: (i, 0, 0))],
        out_specs=pl.BlockSpec((g, D_ATTN), lambda i: (i, 0)),
        out_shape=jax.ShapeDtypeStruct((n, D_ATTN), F32),
        compiler_params=_cparams(("parallel",)),
        name="attn_sample",
    )(sinks, q, k_new, v_new, k_buf, v_buf)


def _s5_params(a_re, a_im, log_dt, b_re, b_im, c_re, c_im):
    hp = lax.Precision.HIGHEST
    dt = jnp.exp(log_dt.astype(F32))[:, None]
    are, aim = a_re.astype(F32), a_im.astype(F32)
    tau = jnp.arange(S5_CHUNK + 1, dtype=F32)[None, :, None]
    mag = jnp.exp(tau * (dt * are)[:, None, :])
    ang = tau * (dt * aim)[:, None, :]
    pw_re, pw_im = mag * jnp.cos(ang), mag * jnp.sin(ang)
    ab_re, ab_im = pw_re[:, 1], pw_im[:, 1]
    den = are * are + aim * aim
    f_re = ((ab_re - 1.0) * are + ab_im * aim) / den
    f_im = (ab_im * are - (ab_re - 1.0) * aim) / den
    bre, bim = b_re.astype(F32), b_im.astype(F32)
    bb_re = f_re[..., None] * bre - f_im[..., None] * bim
    bb_im = f_re[..., None] * bim + f_im[..., None] * bre
    cre, cim = c_re.astype(F32), c_im.astype(F32)
    return dict(pw_re=pw_re, pw_im=pw_im, ab_re=ab_re, ab_im=ab_im, bb_re=bb_re, bb_im=bb_im,
                c_re=cre, c_im=cim, hp=hp)


def _s5_chunk_mats(sp, d_skip):
    hp = sp["hp"]
    g, t, c, p = N_SSM_GROUPS, S5_CHUNK, SSM_GROUP, SSM_STATE
    pw_re, pw_im = sp["pw_re"], sp["pw_im"]
    ca_re = sp["c_re"][:, None] * pw_re[:, :, None, :] - sp["c_im"][:, None] * pw_im[:, :, None, :]
    ca_im = sp["c_re"][:, None] * pw_im[:, :, None, :] + sp["c_im"][:, None] * pw_re[:, :, None, :]
    kern = (jnp.einsum("gtcp,gpd->gtcd", ca_re[:, :t], sp["bb_re"], precision=hp)
            - jnp.einsum("gtcp,gpd->gtcd", ca_im[:, :t], sp["bb_im"], precision=hp))
    kc = jnp.swapaxes(kern, 2, 3)
    kc = kc.at[:, 0].add(d_skip.astype(F32).reshape(g, 1, c) * jnp.eye(c, dtype=F32)[None])
    rev_re, rev_im = pw_re[:, t - 1::-1][:, :t], pw_im[:, t - 1::-1][:, :t]
    wst_re = rev_re[:, :, None, :] * jnp.swapaxes(sp["bb_re"], 1, 2)[:, None] \
        - rev_im[:, :, None, :] * jnp.swapaxes(sp["bb_im"], 1, 2)[:, None]
    wst_im = rev_re[:, :, None, :] * jnp.swapaxes(sp["bb_im"], 1, 2)[:, None] \
        + rev_im[:, :, None, :] * jnp.swapaxes(sp["bb_re"], 1, 2)[:, None]
    wo_re = jnp.transpose(ca_re[:, 1:t + 1], (0, 3, 1, 2))
    wo_im = -jnp.transpose(ca_im[:, 1:t + 1], (0, 3, 1, 2))
    nv, gl = g // S5_LANE_GROUPS, S5_LANE_GROUPS
    kc, wst_re, wst_im, wo_re, wo_im = lax.optimization_barrier((kc, wst_re, wst_im, wo_re, wo_im))
    kc5 =jnp.transpose(kc.reshape(nv, gl, t, c, c), (0, 2, 1, 3, 4))
    ws6 = jnp.transpose(jnp.stack([wst_re, wst_im], axis=3).reshape(nv, gl, t, c, 2, p),
                        (0, 2, 1, 3, 4, 5))
    wo6 = jnp.transpose(jnp.stack([wo_re, wo_im], axis=0).reshape(2, nv, gl, p, t, c),
                        (1, 0, 2, 3, 4, 5))
    kc5, ws6, wo6 = lax.optimization_barrier((kc5.astype(BF16), ws6.astype(BF16), wo6.astype(BF16)))
    gid = jnp.arange(gl)
    zero = jnp.zeros((), BF16)
    same5 = (gid[:, None, None, None] == gid[None, None, :, None])
    bd = jnp.where(same5[None, None], kc5[:, :, :, :, None, :], zero).reshape(nv, t, LANES, LANES)
    same_s = (gid[:, None, None, None, None] == gid[None, None, None, :, None])
    wst_v = jnp.where(same_s[None, None], ws6[:, :, :, :, :, None, :], zero).reshape(nv, t * LANES, 2 * gl * p)
    same_o = (gid[:, None, None, None, None] == gid[None, None, None, :, None])
    wout_v = jnp.where(same_o[None, None], wo6[:, :, :, :, :, None, :], zero).reshape(nv, 2 * gl * p, t * LANES)
    at_re = pw_re[:, t].reshape(1, g * p)
    at_im = pw_im[:, t].reshape(1, g * p)
    return bd, wst_v, wout_v, at_re, at_im


def _s5_chunk_rows(u_ref, nchunk):
    return jnp.concatenate(
        [u_ref[pl.ds(s, nchunk, stride=S5_CHUNK), :] for s in range(S5_CHUNK)], axis=1).astype(BF16)


S5_SLABS = S5_LANE_GROUPS * SSM_STATE // LANES


def _s5_state_kernel(u_ref, wst_ref, sre_ref, sim_ref):
    nchunk = sre_ref.shape[1]
    s = jnp.dot(_s5_chunk_rows(u_ref, nchunk), wst_ref[0], preferred_element_type=F32)
    for k in range(S5_SLABS):
        sre_ref[k] = s[:, k * LANES:(k + 1) * LANES]
        sim_ref[k] = s[:, (S5_SLABS + k) * LANES:(S5_SLABS + k + 1) * LANES]


def _s5_scan_kernel(sre_ref, sim_ref, are_ref, aim_ref, hre_ref, him_ref, fre_ref, fim_ref, *, bsz):
    nchunk = sre_ref.shape[1] // bsz
    are = [jnp.broadcast_to(are_ref[:, k * LANES:(k + 1) * LANES], (bsz, LANES)) for k in range(S5_SLABS)]
    aim = [jnp.broadcast_to(aim_ref[:, k * LANES:(k + 1) * LANES], (bsz, LANES)) for k in range(S5_SLABS)]

    def body(j, carry):
        rows = pl.ds(j, bsz, stride=nchunk)
        out = []
        for k in range(S5_SLABS):
            cre, cim = carry[2 * k], carry[2 * k + 1]
            hre_ref[k, rows, :] = cre
            him_ref[k, rows, :] = cim
            sr = sre_ref[k, rows, :]
            si = sim_ref[k, rows, :]
            out += [are[k] * cre - aim[k] * cim + sr, are[k] * cim + aim[k] * cre + si]
        return tuple(out)

    zero = jnp.zeros((bsz, LANES), F32)
    fin = lax.fori_loop(0, nchunk, body, (zero,) * (2 * S5_SLABS))
    fre_ref[...] = jnp.concatenate(fin[0::2], axis=1)
    fim_ref[...] = jnp.concatenate(fin[1::2], axis=1)


def _s5_out_kernel(u_ref, bd_ref, hre_ref, him_ref, wout_ref, y_ref, m_sc):
    nchunk = hre_ref.shape[1]

    @pl.when(pl.program_id(1) == 0)
    def _():
        for s in range(S5_CHUNK):
            for t in range(S5_CHUNK):
                blk = bd_ref[0, t - s] if t >= s else jnp.zeros((LANES, LANES), BF16)
                m_sc[s * LANES:(s + 1) * LANES, t * LANES:(t + 1) * LANES] = blk

    hcat = jnp.concatenate([hre_ref[k] for k in range(S5_SLABS)] + [him_ref[k] for k in range(S5_SLABS)],
                           axis=1).astype(BF16)
    y = jnp.dot(_s5_chunk_rows(u_ref, nchunk), m_sc[...], preferred_element_type=F32)
    y = y + jnp.dot(hcat, wout_ref[0], preferred_element_type=F32)
    for s in range(S5_CHUNK):
        y_ref[pl.ds(s, nchunk, stride=S5_CHUNK), :] = y[:, s * LANES:(s + 1) * LANES]


def _s5_prompt(u, bsz, seq, mats):
    bd, wst_v, wout_v, at_re, at_im = mats
    g, t, p = N_SSM_GROUPS, S5_CHUNK, SSM_STATE
    nchunk = seq // t
    n = nchunk * bsz
    nv = g // S5_LANE_GROUPS
    half = S5_LANE_GROUPS * p
    s_re, s_im = pl.pallas_call(
        _s5_state_kernel,
        grid=(nv, bsz),
        in_specs=[pl.BlockSpec((seq, LANES), lambda v, b: (b, v)),
                  pl.BlockSpec((1, t * LANES, 2 * half), lambda v, b: (v, 0, 0))],
        out_specs=[pl.BlockSpec((S5_SLABS, nchunk, LANES), lambda v, b: (v, b, 0)),
                   pl.BlockSpec((S5_SLABS, nchunk, LANES), lambda v, b: (v, b, 0))],
        out_shape=[jax.ShapeDtypeStruct((nv * S5_SLABS, n, LANES), F32)] * 2,
        compiler_params=_cparams(("parallel", "parallel"), VMEM_LIMIT),
        name="s5_state",
    )(u, wst_v)
    h_re, h_im, f_re, f_im = pl.pallas_call(
        functools.partial(_s5_scan_kernel, bsz=bsz),
        grid=(nv,),
        in_specs=[pl.BlockSpec((S5_SLABS, n, LANES), lambda i: (i, 0, 0)),
                  pl.BlockSpec((S5_SLABS, n, LANES), lambda i: (i, 0, 0)),
                  pl.BlockSpec((1, half), lambda i: (0, i)),
                  pl.BlockSpec((1, half), lambda i: (0, i))],
        out_specs=[pl.BlockSpec((S5_SLABS, n, LANES), lambda i: (i, 0, 0)),
                   pl.BlockSpec((S5_SLABS, n, LANES), lambda i: (i, 0, 0)),
                   pl.BlockSpec((bsz, half), lambda i: (0, i)),
                   pl.BlockSpec((bsz, half), lambda i: (0, i))],
        out_shape=[jax.ShapeDtypeStruct((nv * S5_SLABS, n, LANES), F32)] * 2
        + [jax.ShapeDtypeStruct((bsz, g * p), F32)] * 2,
        compiler_params=_cparams(("parallel",)),
        name="s5_scan",
    )(s_re, s_im, at_re, at_im)
    y = pl.pallas_call(
        _s5_out_kernel,
        grid=(nv, bsz),
        in_specs=[pl.BlockSpec((seq, LANES), lambda v, b: (b, v)),
                  pl.BlockSpec((1, t, LANES, LANES), lambda v, b: (v, 0, 0, 0)),
                  pl.BlockSpec((S5_SLABS, nchunk, LANES), lambda v, b: (v, b, 0)),
                  pl.BlockSpec((S5_SLABS, nchunk, LANES), lambda v, b: (v, b, 0)),
                  pl.BlockSpec((1, 2 * half, t * LANES), lambda v, b: (v, 0, 0))],
        out_specs=pl.BlockSpec((seq, LANES), lambda v, b: (b, v)),
        out_shape=jax.ShapeDtypeStruct((bsz * seq, D_SSM), F32),
        scratch_shapes=[pltpu.VMEM((t * LANES, t * LANES), BF16)],
        compiler_params=_cparams(("parallel", "arbitrary"), VMEM_LIMIT),
        name="s5_out",
    )(u, bd, h_re, h_im, wout_v)
    return y, f_re, f_im


S5S_GROUPS = LANES // SSM_GROUP


def _s5_sample_mats(sp, d_skip):
    go, gl, c, p = N_SSM_GROUPS // S5S_GROUPS, S5S_GROUPS, SSM_GROUP, SSM_STATE
    eye = jnp.eye(gl, dtype=F32)

    def bdiag_in(b):
        b4 = b.reshape(go, gl, p, c)
        return jnp.einsum("ogpc,gh->ogchp", b4, eye).reshape(go, gl * c, gl * p)

    def bdiag_out(cm):
        c4 = cm.reshape(go, gl, c, p)
        return jnp.einsum("ogcp,gh->ogphc", c4, eye).reshape(go, gl * p, gl * c)

    b8 = jnp.concatenate([bdiag_in(sp["bb_re"]), bdiag_in(sp["bb_im"])], axis=2)
    c8 = jnp.concatenate([bdiag_out(sp["c_re"]), -bdiag_out(sp["c_im"])], axis=1)
    a_re = sp["ab_re"].reshape(1, N_SSM_GROUPS * p)
    a_im = sp["ab_im"].reshape(1, N_SSM_GROUPS * p)
    return b8, c8, a_re, a_im, d_skip.astype(F32).reshape(1, D_SSM)


def _s5_sample_kernel(u_ref, hre_ref, him_ref, b8_ref, c8_ref, are_ref, aim_ref, d_ref,
                      y_ref, ore_ref, oim_ref):
    hp = lax.Precision.HIGHEST
    u = u_ref[...]
    half = S5S_GROUPS * SSM_STATE
    bu = jnp.dot(u, b8_ref[0], preferred_element_type=F32, precision=hp)
    are, aim = are_ref[...], aim_ref[...]
    h0r, h0i = hre_ref[...], him_ref[...]
    hr = are * h0r - aim * h0i + bu[:, :half]
    hi = are * h0i + aim * h0r + bu[:, half:]
    ore_ref[...] = hr
    oim_ref[...] = hi
    y = jnp.dot(jnp.concatenate([hr, hi], axis=1), c8_ref[0], preferred_element_type=F32, precision=hp)
    y_ref[...] = (y + d_ref[...] * u).astype(y_ref.dtype)


def _s5_sample(u, h0_re, h0_im, mats):
    b8, c8, a_re, a_im, d = mats
    n = u.shape[0]
    half = S5S_GROUPS * SSM_STATE
    return pl.pallas_call(
        _s5_sample_kernel,
        grid=(N_SSM_GROUPS // S5S_GROUPS,),
        in_specs=[pl.BlockSpec((n, LANES), lambda i: (0, i)),
                  pl.BlockSpec((n, half), lambda i: (0, i)),
                  pl.BlockSpec((n, half), lambda i: (0, i)),
                  pl.BlockSpec((1, LANES, 2 * half), lambda i: (i, 0, 0)),
                  pl.BlockSpec((1, 2 * half, LANES), lambda i: (i, 0, 0)),
                  pl.BlockSpec((1, half), lambda i: (0, i)),
                  pl.BlockSpec((1, half), lambda i: (0, i)),
                  pl.BlockSpec((1, LANES), lambda i: (0, i))],
        out_specs=[pl.BlockSpec((n, LANES), lambda i: (0, i)),
                   pl.BlockSpec((n, half), lambda i: (0, i)),
                   pl.BlockSpec((n, half), lambda i: (0, i))],
        out_shape=[jax.ShapeDtypeStruct((n, D_SSM), BF16),
                   jax.ShapeDtypeStruct((n, N_SSM_GROUPS * SSM_STATE), F32),
                   jax.ShapeDtypeStruct((n, N_SSM_GROUPS * SSM_STATE), F32)],
        compiler_params=_cparams(("parallel",)),
        name="s5_sample",
    )(u, h0_re, h0_im, b8, c8, a_re, a_im, d)


def _layer_norm(x, g, b):
    mu = jnp.mean(x, axis=-1, keepdims=True)
    xc = x - mu
    var = jnp.mean(xc * xc, axis=-1, keepdims=True)
    return xc * lax.rsqrt(var + LN_EPS) * g + b


RUN_ROWS = SUBLANES
TAB_ROWS = 3


def _merge_kernel(x_ref, oa_ref, ys_ref, carry_in_ref, wao_ref, wso_ref, wg_ref, bg_ref, wo_ref,
                  g1_ref, b1_ref, wrt_ref, brt_ref,
                  x1_ref, lpos_ref, cols_ref, tab_ref, carry_out_ref, carry_sc):
    step = pl.program_id(0)

    @pl.when(step == 0)
    def _():
        carry_sc[...] = carry_in_ref[...]

    tm = x_ref.shape[0]
    x = x_ref[...]
    branch_a = jnp.dot(oa_ref[...].astype(BF16), wao_ref[...], preferred_element_type=F32)
    z = jnp.dot(jax.nn.gelu(ys_ref[...].astype(F32)).astype(BF16), wso_ref[...], preferred_element_type=F32)
    branch_b = z[:, :D_MODEL] * jax.nn.sigmoid(z[:, D_MODEL:])
    gates = jax.nn.sigmoid(jnp.dot(x.astype(BF16), wg_ref[...], preferred_element_type=F32) + bg_ref[...])
    mixed = gates[:, :D_MODEL] * branch_a + gates[:, D_MODEL:] * branch_b
    mix = jnp.dot(mixed.astype(BF16), wo_ref[...], preferred_element_type=F32)
    x1 = _layer_norm(DEEPNORM_ALPHA * x + mix, g1_ref[...], b1_ref[...])
    x1_ref[...] = x1

    hp = lax.Precision.HIGHEST
    nt_dims = (((1,), (1,)), ((), ()))
    logits = lax.dot_general(wrt_ref[...], x1, nt_dims, preferred_element_type=F32, precision=hp) + brt_ref[...]
    sub = lax.broadcasted_iota(jnp.int32, (N_EXPERTS, tm), 0)
    work = logits
    vals, sels = [], []
    for _ in range(TOP_K):
        mx = jnp.max(work, axis=0, keepdims=True)
        idx = jnp.min(jnp.where(work == mx, sub, N_EXPERTS), axis=0, keepdims=True)
        sel = sub == idx
        vals.append(mx)
        sels.append(sel)
        work = jnp.where(sel, -jnp.inf, work)
    ex = [jnp.exp(v - vals[0]) for v in vals]
    tot = ex[0] + ex[1] + ex[2] + ex[3]
    gate_rows = jnp.concatenate([e / tot for e in ex], axis=0)

    multi = jnp.zeros((N_EXPERTS, tm), F32)
    for sel in sels:
        multi = multi + jnp.where(sel, 1.0, 0.0)
    multi_b = multi.astype(BF16)
    r = lax.broadcasted_iota(jnp.int32, (tm, tm), 0)
    c = lax.broadcasted_iota(jnp.int32, (tm, tm), 1)
    earlier = jnp.dot(multi_b, jnp.where(r < c, 1.0, 0.0).astype(BF16), preferred_element_type=F32)
    cnt_col = jnp.sum(multi, axis=1, keepdims=True)
    nb_col = jnp.floor((cnt_col + (RUN_ROWS - 1.0)) * (1.0 / RUN_ROWS))
    er = lax.broadcasted_iota(jnp.int32, (N_EXPERTS, N_EXPERTS), 0)
    ec = lax.broadcasted_iota(jnp.int32, (N_EXPERTS, N_EXPERTS), 1)
    loff_col = jnp.dot(jnp.where(ec < er, 1.0, 0.0).astype(BF16),
                       jnp.broadcast_to(nb_col, (N_EXPERTS, tm)).astype(BF16), preferred_element_type=F32)
    base = RUN_ROWS * loff_col + earlier
    lpos = jnp.concatenate([jnp.sum(jnp.where(sel, base, 0.0), axis=0, keepdims=True) for sel in sels], axis=0)
    lpos_ref[...] = lpos.astype(jnp.int32)
    rows = jnp.concatenate([lpos, gate_rows], axis=0)
    cols_ref[...] = lax.dot_general(jnp.where(r == c, 1.0, 0.0), rows, nt_dims,
                                    preferred_element_type=F32, precision=hp)

    cnt_row = lax.dot_general(jnp.ones((SUBLANES, tm), BF16), multi_b, nt_dims, preferred_element_type=F32)
    nb_row = jnp.floor((cnt_row + (RUN_ROWS - 1.0)) * (1.0 / RUN_ROWS))
    loff_row = jnp.dot(nb_row.astype(BF16), jnp.where(er < ec, 1.0, 0.0).astype(BF16), preferred_element_type=F32)
    lane_pad = jnp.zeros((SUBLANES, LANES - N_EXPERTS), F32)
    nb_p = jnp.concatenate([nb_row, lane_pad], axis=1)
    loff_p = jnp.concatenate([loff_row, lane_pad], axis=1)
    goff_p = carry_sc[...]
    rid = lax.broadcasted_iota(jnp.int32, (SUBLANES, LANES), 0)
    tab = jnp.where(rid == 0, nb_p, jnp.where(rid == 1, loff_p, jnp.where(rid == 2, goff_p, 0.0)))
    tab_ref[0] = tab.astype(jnp.int32)
    carry_sc[...] = goff_p + nb_p
    carry_out_ref[...] = carry_sc[...]


def _merge(x, o_attn, y_ssm, carry_in, w, *, tile):
    n = x.shape[0]
    nt = n // tile
    full = lambda shape: pl.BlockSpec(shape, lambda i: (0,) * len(shape))
    return pl.pallas_call(
        _merge_kernel,
        grid=(nt,),
        in_specs=[pl.BlockSpec((tile, D_MODEL), lambda i: (i, 0)),
                  pl.BlockSpec((tile, D_ATTN), lambda i: (i, 0)),
                  pl.BlockSpec((tile, D_SSM), lambda i: (i, 0)),
                  full((SUBLANES, LANES)),
                  full((D_ATTN, D_MODEL)), full((D_SSM, 2 * D_MODEL)), full((D_MODEL, 2 * D_MODEL)),
                  full((1, 2 * D_MODEL)), full((D_MODEL, D_MODEL)),
                  full((1, D_MODEL)), full((1, D_MODEL)),
                  full((N_EXPERTS, D_MODEL)), full((N_EXPERTS, 1))],
        out_specs=[pl.BlockSpec((tile, D_MODEL), lambda i: (i, 0)),
                   pl.BlockSpec((TOP_K, tile), lambda i: (0, i)),
                   pl.BlockSpec((tile, 2 * TOP_K), lambda i: (i, 0)),
                   pl.BlockSpec((1, SUBLANES, LANES), lambda i: (i, 0, 0)),
                   full((SUBLANES, LANES))],
        out_shape=[jax.ShapeDtypeStruct((n, D_MODEL), F32),
                   jax.ShapeDtypeStruct((TOP_K, n), jnp.int32),
                   jax.ShapeDtypeStruct((n, 2 * TOP_K), F32),
                   jax.ShapeDtypeStruct((nt, SUBLANES, LANES), jnp.int32),
                   jax.ShapeDtypeStruct((SUBLANES, LANES), F32)],
        scratch_shapes=[pltpu.VMEM((SUBLANES, LANES), F32)],
        compiler_params=_cparams(("arbitrary",), VMEM_LIMIT),
        name="merge",
    )(x, o_attn, y_ssm, carry_in, w["wao"], w["wso"], w["wg"], w["bg"], w["wo"], w["g1"], w["b1"],
      w["wrt"], w["brt"])


def _tab(tab_ref, tile, row, e):
    return tab_ref[(tile * TAB_ROWS + row) * N_EXPERTS + e]


def _for_each_run_unit(tab_ref, tile, fn):
    def per_expert(e, carry):
        loff = _tab(tab_ref, tile, 1, e)
        goff = _tab(tab_ref, tile, 2, e)

        def per_unit(j, c2):
            fn(pl.multiple_of(RUN_ROWS * (loff + j), RUN_ROWS), RUN_ROWS * (goff + j), e)
            return c2

        lax.fori_loop(0, _tab(tab_ref, tile, 0, e), per_unit, 0)
        return carry

    lax.fori_loop(0, N_EXPERTS, per_expert, 0)


def _dispatch_kernel(tab_ref, seg_ref, tot_ref, tail_ref, lpos_p_ref, xp_ref, lpos_s_ref, xs_in_ref, xs_ref,
                     loc_sc, zero_sc, sem, zsem):
    i = pl.program_id(0)
    last = pl.num_programs(0) - 1
    tile = i
    slot = i % 2
    loc = loc_sc.shape[1]

    @pl.when(i == 0)
    def _():
        zero_sc[...] = jnp.zeros_like(zero_sc)

        def tail_copy(e, j):
            row = pl.multiple_of(RUN_ROWS * (tail_ref[e] + j), RUN_ROWS)
            return pltpu.make_async_copy(zero_sc.at[pl.ds(0, RUN_ROWS)], xs_ref.at[pl.ds(row, RUN_ROWS)], zsem)

        def per_expert(e, carry):
            n = tail_ref[N_EXPERTS + e]
            lax.fori_loop(0, n, lambda j, c2: (tail_copy(e, j).start(), c2)[1], 0)
            lax.fori_loop(0, n, lambda j, c2: (tail_copy(e, j).wait(), c2)[1], 0)
            return carry

        lax.fori_loop(0, N_EXPERTS, per_expert, 0)

        def block_copy(b):
            row = pl.multiple_of(b * MOE_ROWS, MOE_ROWS)
            return pltpu.make_async_copy(zero_sc, xs_ref.at[pl.ds(row, MOE_ROWS)], zsem)

        first_unused, n_blocks = tail_ref[2 * N_EXPERTS], xs_ref.shape[0] // MOE_ROWS
        lax.fori_loop(first_unused, n_blocks, lambda b, c2: (block_copy(b).start(), c2)[1], 0)
        lax.fori_loop(first_unused, n_blocks, lambda b, c2: (block_copy(b).wait(), c2)[1], 0)

    def sort_tile(lpos_ref, x_ref):
        tm = x_ref.shape[0]
        rows = lax.broadcasted_iota(jnp.int32, (loc, tm), 0)
        lp = lpos_ref[...]
        onehot = jnp.zeros((loc, tm), F32)
        for k in range(TOP_K):
            onehot = jnp.where(rows == lp[k:k + 1], 1.0, onehot)
        loc_sc[slot] = jnp.dot(onehot.astype(BF16), x_ref[...].astype(BF16), preferred_element_type=F32)

    @pl.when(i < last)
    def _():
        sort_tile(lpos_p_ref, xp_ref)

    @pl.when(i == last)
    def _():
        sort_tile(lpos_s_ref, xs_in_ref)

    def unit_copy(sl, lrow, grow, e):
        dst = pl.multiple_of(seg_ref[e] + grow, RUN_ROWS)
        return pltpu.make_async_copy(loc_sc.at[sl, pl.ds(lrow, RUN_ROWS)], xs_ref.at[pl.ds(dst, RUN_ROWS)], sem.at[sl])

    _for_each_run_unit(tab_ref, tile, lambda lrow, grow, e: unit_copy(slot, lrow, grow, e).start())

    def drain(tl, sl):
        lax.fori_loop(0, tot_ref[tl], lambda j, c2: (unit_copy(sl, 0, 0, 0).wait(), c2)[1], 0)

    @pl.when(i > 0)
    def _():
        drain(tile - 1, 1 - slot)

    @pl.when(i == last)
    def _():
        drain(tile, slot)


def _dispatch(tab, seg_start, tot, tails, lpos_p, x1_p, lpos_s, x1_s, *, tile, nrows):
    nt_p = x1_p.shape[0] // tile
    ns = x1_s.shape[0]
    loc = tile * TOP_K + N_EXPERTS * RUN_ROWS
    prompt_blk = lambda i, *_: jnp.minimum(i, nt_p - 1)
    return pl.pallas_call(
        _dispatch_kernel,
        grid_spec=pltpu.PrefetchScalarGridSpec(
            num_scalar_prefetch=4,
            grid=(nt_p + 1,),
            in_specs=[pl.BlockSpec((TOP_K, tile), lambda i, *_: (0, prompt_blk(i))),
                      pl.BlockSpec((tile, D_MODEL), lambda i, *_: (prompt_blk(i), 0)),
                      pl.BlockSpec((TOP_K, ns), lambda i, *_: (0, 0)),
                      pl.BlockSpec((ns, D_MODEL), lambda i, *_: (0, 0))],
            out_specs=pl.BlockSpec(memory_space=pl.ANY),
            scratch_shapes=[pltpu.VMEM((2, loc, D_MODEL), F32), pltpu.VMEM((MOE_ROWS, D_MODEL), F32),
                            pltpu.SemaphoreType.DMA((2,)), pltpu.SemaphoreType.DMA(())]),
        out_shape=jax.ShapeDtypeStruct((nrows, D_MODEL), F32),
        compiler_params=_cparams(("arbitrary",), VMEM_LIMIT),
        name="dispatch",
    )(tab, seg_start, tot, tails, lpos_p, x1_p, lpos_s, x1_s)


def _deinterleave_matrix():
    pm = np.zeros((MXU_DIM, MXU_DIM), np.float32)
    half = MXU_DIM // 2
    for c in range(half):
        pm[2 * c, c] = 1.0
        pm[2 * c + 1, half + c] = 1.0
    return pm


def _expert_kernel(be_ref, nu_ref, ord_ref, nxt_ref, xs_ref, w1_hbm, b1_ref, w2_hbm, b2_ref, pm_ref, y_ref,
                   w1f_sc, w2f_sc, w1p_sc, w2b_sc, sem):
    i = pl.program_id(0)
    e = be_ref[i]
    prev = be_ref[jnp.maximum(i - 1, 0)]
    nblk = 2 * D_FF // MXU_DIM

    def weight_copies(expert, slot):
        return (pltpu.make_async_copy(w1_hbm.at[expert], w1f_sc.at[slot], sem.at[0, slot]),
                pltpu.make_async_copy(w2_hbm.at[expert], w2f_sc.at[slot], sem.at[1, slot]))

    @pl.when(i == 0)
    def _():
        for cp in weight_copies(e, 0):
            cp.start()

    @pl.when((i == 0) | (e != prev))
    def _():
        slot = ord_ref[i] % 2
        for cp in weight_copies(e, slot):
            cp.wait()
        nxt = nxt_ref[i]

        @pl.when(nxt >= 0)
        def _():
            for cp in weight_copies(nxt, 1 - slot):
                cp.start()

        for cb in range(nblk):
            blk = w1f_sc[slot, :, cb * MXU_DIM:(cb + 1) * MXU_DIM].astype(BF16)
            w1p_sc[:, cb * MXU_DIM:(cb + 1) * MXU_DIM] = jnp.dot(
                blk, pm_ref[...], preferred_element_type=F32).astype(BF16)
        w2b_sc[...] = w2f_sc[slot].astype(BF16)

    @pl.when(i < nu_ref[0])
    def _():
        x = xs_ref[...].astype(BF16)
        h = jnp.dot(x, w1p_sc[...], preferred_element_type=F32) + b1_ref[0]
        half = MXU_DIM // 2
        acts = []
        for cb in range(nblk):
            x_glu = jnp.minimum(h[:, cb * MXU_DIM:cb * MXU_DIM + half], SWIGLU_LIMIT)
            x_lin = jnp.clip(h[:, cb * MXU_DIM + half:(cb + 1) * MXU_DIM], -SWIGLU_LIMIT, SWIGLU_LIMIT)
            acts.append((x_glu * jax.nn.sigmoid(SWIGLU_ALPHA * x_glu) * (x_lin + 1.0)).astype(BF16))
        act = jnp.concatenate(acts, axis=1)
        y_ref[...] = jnp.dot(act, w2b_sc[...], preferred_element_type=F32) + b2_ref[0]

    @pl.when(i >= nu_ref[0])
    def _():
        y_ref[...] = jnp.zeros_like(y_ref)


def _experts(block_e, n_used, run_ord, run_next, xs, w1, b1p, w2, b2, pm):
    nrows = xs.shape[0]
    nb = nrows // MOE_ROWS
    return pl.pallas_call(
        _expert_kernel,
        grid_spec=pltpu.PrefetchScalarGridSpec(
            num_scalar_prefetch=4,
            grid=(nb,),
            in_specs=[pl.BlockSpec((MOE_ROWS, D_MODEL), lambda i, be, nu, ro, rn: (jnp.minimum(i, nu[0] - 1), 0)),
                      pl.BlockSpec(memory_space=pl.ANY),
                      pl.BlockSpec((1, 1, 2 * D_FF), lambda i, be, nu, ro, rn: (be[i], 0, 0)),
                      pl.BlockSpec(memory_space=pl.ANY),
                      pl.BlockSpec((1, 1, D_MODEL), lambda i, be, nu, ro, rn: (be[i], 0, 0)),
                      pl.BlockSpec((MXU_DIM, MXU_DIM), lambda i, be, nu, ro, rn: (0, 0))],
            out_specs=pl.BlockSpec((MOE_ROWS, D_MODEL), lambda i, be, nu, ro, rn: (i, 0)),
            scratch_shapes=[pltpu.VMEM((2, D_MODEL, 2 * D_FF), F32), pltpu.VMEM((2, D_FF, D_MODEL), F32),
                            pltpu.VMEM((D_MODEL, 2 * D_FF), BF16), pltpu.VMEM((D_FF, D_MODEL), BF16),
                            pltpu.SemaphoreType.DMA((2, 2))]),
        out_shape=jax.ShapeDtypeStruct((nrows, D_MODEL), F32),
        compiler_params=_cparams(("arbitrary",), VMEM_LIMIT),
        name="experts",
    )(block_e, n_used, run_ord, run_next, xs, w1, b1p, w2, b2, pm)


def _combine_kernel(tab_ref, seg_ref, tot_ref, cols_ref, x1_ref, g2_ref, b2_ref, ys_ref, y_ref, loc_sc, sem,
                    *, tile_base):
    i = pl.program_id(0)
    last = pl.num_programs(0) - 1
    tile = i + tile_base
    slot = i % 2
    loc, tm = loc_sc.shape[1], x1_ref.shape[0]

    def unit_copy(sl, lrow, grow, e):
        src = pl.multiple_of(seg_ref[e] + grow, RUN_ROWS)
        return pltpu.make_async_copy(ys_ref.at[pl.ds(src, RUN_ROWS)], loc_sc.at[sl, pl.ds(lrow, RUN_ROWS)], sem.at[sl])

    def gather(tl, sl):
        _for_each_run_unit(tab_ref, tl, lambda lrow, grow, e: unit_copy(sl, lrow, grow, e).start())

    @pl.when(i == 0)
    def _():
        loc_sc[...] = jnp.zeros_like(loc_sc)
        gather(tile, slot)

    @pl.when(i < last)
    def _():
        gather(tile + 1, 1 - slot)

    lax.fori_loop(0, tot_ref[tile], lambda j, c2: (unit_copy(slot, 0, 0, 0).wait(), c2)[1], 0)

    cols = cols_ref[...]
    lane = lax.broadcasted_iota(jnp.int32, (tm, loc), 1)
    weights = jnp.zeros((tm, loc), F32)
    for k in range(TOP_K):
        weights = jnp.where(lane == cols[:, k:k + 1].astype(jnp.int32), cols[:, TOP_K + k:TOP_K + k + 1], weights)
    ffn = jnp.dot(weights.astype(BF16), loc_sc[slot].astype(BF16), preferred_element_type=F32)
    y_ref[...] = _layer_norm(DEEPNORM_ALPHA * x1_ref[...] + ffn, g2_ref[...], b2_ref[...])


def _combine(tab, seg_start, tot, cols, x1, g2, b2, ys, *, tile, tile_base):
    n = x1.shape[0]
    loc = tile * TOP_K + N_EXPERTS * RUN_ROWS
    return pl.pallas_call(
        functools.partial(_combine_kernel, tile_base=tile_base),
        grid_spec=pltpu.PrefetchScalarGridSpec(
            num_scalar_prefetch=3,
            grid=(n // tile,),
            in_specs=[pl.BlockSpec((tile, 2 * TOP_K), lambda i, *_: (i, 0)),
                      pl.BlockSpec((tile, D_MODEL), lambda i, *_: (i, 0)),
                      pl.BlockSpec((1, D_MODEL), lambda i, *_: (0, 0)),
                      pl.BlockSpec((1, D_MODEL), lambda i, *_: (0, 0)),
                      pl.BlockSpec(memory_space=pl.ANY)],
            out_specs=pl.BlockSpec((tile, D_MODEL), lambda i, *_: (i, 0)),
            scratch_shapes=[pltpu.VMEM((2, loc, D_MODEL), F32), pltpu.SemaphoreType.DMA((2,))]),
        out_shape=jax.ShapeDtypeStruct((n, D_MODEL), F32),
        compiler_params=_cparams(("arbitrary",), VMEM_LIMIT),
        name="combine",
    )(tab, seg_start, tot, cols, x1, g2, b2, ys)


def kernel(x_prompt, x_sample, cache_k_win, cache_v_win, state_ssm_re, state_ssm_im, w_in, b_in, attn_sinks,
           w_attn_out, ssm_a_re, ssm_a_im, ssm_log_dt, ssm_b_re, ssm_b_im, ssm_c_re, ssm_c_im, ssm_d, w_ssm_out,
           w_gate, b_gate, w_out, ln1_g, ln1_b, w_router, b_router, w_exp1, b_exp1, w_exp2, b_exp2, ln2_g, ln2_b):
    assert w_in.shape[0] == DEPTH == 1
    bsz, seq, _ = x_prompt.shape
    nsamp = x_sample.shape[0]
    assert x_sample.shape[1] == 1
    n_p = bsz * seq
    n_tok = n_p + nsamp

    xp = x_prompt.reshape(n_p, D_MODEL)
    xsm = x_sample.reshape(nsamp, D_MODEL)
    b_in2 = b_in[0].reshape(1, D_IN)
    sinks = attn_sinks[0].astype(F32)

    q_p, k_p, v_p, u_p = _proj(xp, w_in[0].astype(BF16), b_in2, tile=512, exact_f32=False, q_dtype=BF16)
    q_s, k_s, v_s, u_s = _proj(xsm, w_in[0], b_in2, tile=nsamp, exact_f32=True, q_dtype=F32)

    o_p = _attn_prompt(sinks, q_p.reshape(bsz, seq, D_ATTN), k_p.reshape(bsz, seq, D_KV),
                       v_p.reshape(bsz, seq, D_KV)).reshape(n_p, D_ATTN)
    k_buf = cache_k_win[0].reshape(nsamp, WINDOW, D_KV)
    v_buf = cache_v_win[0].reshape(nsamp, WINDOW, D_KV)
    o_s = _attn_sample(sinks, q_s, k_s, v_s, k_buf, v_buf)

    sp = _s5_params(ssm_a_re[0], ssm_a_im[0], ssm_log_dt[0], ssm_b_re[0], ssm_b_im[0], ssm_c_re[0], ssm_c_im[0])
    y_p, hp_re, hp_im = _s5_prompt(u_p, bsz, seq, _s5_chunk_mats(sp, ssm_d[0]))
    y_s, hs_re, hs_im = _s5_sample(u_s, state_ssm_re[0].reshape(nsamp, -1), state_ssm_im[0].reshape(nsamp, -1),
                                   _s5_sample_mats(sp, ssm_d[0]))

    wm = dict(wao=w_attn_out[0].astype(BF16), wso=w_ssm_out[0].astype(BF16), wg=w_gate[0].astype(BF16),
              bg=b_gate[0].reshape(1, -1), wo=w_out[0].astype(BF16), g1=ln1_g[0].reshape(1, -1),
              b1=ln1_b[0].reshape(1, -1), wrt=w_router[0].T, brt=b_router[0].reshape(-1, 1))
    carry0 = jnp.zeros((SUBLANES, LANES), F32)
    x1_p, lpos_p, cols_p, tab_p, carry1 = _merge(xp, o_p, y_p, carry0, wm, tile=TOK_TILE)
    x1_s, lpos_s, cols_s, tab_s, carry2 = _merge(xsm, o_s, y_s, carry1, wm, tile=nsamp)

    nt_p = n_p // TOK_TILE
    tab = jnp.concatenate([tab_p[:, :TAB_ROWS, :N_EXPERTS], tab_s[:, :TAB_ROWS, :N_EXPERTS]], axis=0)
    tot = jnp.sum(tab[:, 0, :], axis=1).astype(jnp.int32)
    tab = tab.reshape(-1)
    seg_rows = carry2[0, :N_EXPERTS].astype(jnp.int32) * RUN_ROWS
    padded = ((seg_rows + MOE_ROWS - 1) // MOE_ROWS) * MOE_ROWS
    pad_end = jnp.cumsum(padded)
    pad_start = (pad_end - padded).astype(jnp.int32)
    tails = jnp.concatenate([(pad_start + seg_rows) // RUN_ROWS, (padded - seg_rows) // RUN_ROWS]).astype(jnp.int32)
    n_runs = (nt_p + 1) * N_EXPERTS
    nb_max = (n_tok * TOP_K + n_runs * (RUN_ROWS - 1) + N_EXPERTS * (MOE_ROWS - 1) + MOE_ROWS - 1) // MOE_ROWS
    n_used = (pad_end[-1] // MOE_ROWS).astype(jnp.int32)
    tails = jnp.concatenate([tails, n_used.reshape(1)])
    blk_start = jnp.arange(nb_max, dtype=jnp.int32) * MOE_ROWS
    blk_e = jnp.minimum(jnp.sum(blk_start[:, None] >= pad_end[None, :], axis=1), N_EXPERTS - 1).astype(jnp.int32)
    blk_e = jnp.where(jnp.arange(nb_max) < n_used, blk_e, blk_e[jnp.maximum(n_used - 1, 0)])
    new_run = jnp.concatenate([jnp.ones((1,), jnp.int32), (blk_e[1:] != blk_e[:-1]).astype(jnp.int32)])
    run_ord = (jnp.cumsum(new_run) - 1).astype(jnp.int32)
    ids = jnp.arange(N_EXPERTS, dtype=jnp.int32)
    later = (ids[None, :] > ids[:, None]) & (padded > 0)[None, :]
    next_e = jnp.min(jnp.where(later, ids[None, :], N_EXPERTS), axis=1)
    next_e = jnp.where(next_e < N_EXPERTS, next_e, -1).astype(jnp.int32)
    run_next = next_e[blk_e]

    nrows = nb_max * MOE_ROWS
    xs = _dispatch(tab, pad_start, tot, tails, lpos_p, x1_p, lpos_s, x1_s, tile=TOK_TILE, nrows=nrows)

    b1p = b_exp1[0].reshape(N_EXPERTS, 2 * D_FF // MXU_DIM, MXU_DIM // 2, 2)
    b1p = jnp.swapaxes(b1p, 2, 3).reshape(N_EXPERTS, 1, 2 * D_FF)
    ys = _experts(blk_e, n_used.reshape(1), run_ord, run_next, xs, w_exp1[0], b1p, w_exp2[0],
                  b_exp2[0].reshape(N_EXPERTS, 1, D_MODEL),
                  jnp.asarray(_deinterleave_matrix(), BF16))

    g2, b2 = ln2_g[0].reshape(1, -1), ln2_b[0].reshape(1, -1)
    y_prompt = _combine(tab, pad_start, tot, cols_p, x1_p, g2, b2, ys, tile=TOK_TILE, tile_base=0)
    y_sample = _combine(tab, pad_start, tot, cols_s, x1_s, g2, b2, ys, tile=nsamp, tile_base=nt_p)

    k_p4 = k_p.reshape(bsz, seq, N_KV_HEADS, HEAD_DIM)[:, -WINDOW:]
    v_p4 = v_p.reshape(bsz, seq, N_KV_HEADS, HEAD_DIM)[:, -WINDOW:]
    k_s4 = jnp.concatenate([cache_k_win[0][:, 1:], k_s.reshape(nsamp, 1, N_KV_HEADS, HEAD_DIM)], axis=1)
    v_s4 = jnp.concatenate([cache_v_win[0][:, 1:], v_s.reshape(nsamp, 1, N_KV_HEADS, HEAD_DIM)], axis=1)
    st = lambda a, n: a.reshape(1, n, N_SSM_GROUPS, SSM_STATE)
    return (y_prompt.reshape(bsz, seq, D_MODEL), y_sample.reshape(nsamp, 1, D_MODEL),
            k_p4[None], v_p4[None], st(hp_re, bsz), st(hp_im, bsz),
            k_s4[None], v_s4[None], st(hs_re, nsamp), st(hs_im, nsamp))
```

```python
import functools
import math

import numpy as np
import jax
import jax.numpy as jnp
from jax import lax
from jax.experimental import pallas as pl
from jax.experimental.pallas import tpu as pltpu

F32 = jnp.float32
BF16 = jnp.bfloat16

D_MODEL = 1024
HEAD_DIM = 64
N_Q_HEADS = 8
N_KV_HEADS = 2
Q_PER_KV = N_Q_HEADS // N_KV_HEADS
D_ATTN = N_Q_HEADS * HEAD_DIM
D_KV = N_KV_HEADS * HEAD_DIM
WINDOW = 128
ATTN_SCALE = HEAD_DIM ** -0.5
SSM_GROUP = 16
D_SSM = D_MODEL // 2
N_SSM_GROUPS = D_SSM // SSM_GROUP
SSM_STATE = 64
D_IN = D_ATTN + 2 * D_KV + D_SSM
N_EXPERTS = 32
TOP_K = 4
D_FF = D_MODEL
SWIGLU_LIMIT = 7.0
SWIGLU_ALPHA = 1.702
LN_EPS = 1e-5
DEPTH = 1
DEEPNORM_ALPHA = (2 * DEPTH) ** 0.25

LANES = 128
SUBLANES = 8
MXU_DIM = 256

S5_CHUNK = MXU_DIM // SSM_GROUP
S5_LANE_GROUPS = LANES // SSM_GROUP
MOE_ROWS = 256
TOK_TILE = 256
VMEM_LIMIT = 48 * 1024 * 1024


def _cparams(sem, vmem=None):
    return pltpu.CompilerParams(dimension_semantics=sem, vmem_limit_bytes=vmem)


def _proj_kernel(x_ref, w_ref, b_ref, q_ref, k_ref, v_ref, u_ref, *, exact_f32):
    if exact_f32:
        h = jnp.dot(x_ref[...], w_ref[...], preferred_element_type=F32, precision=lax.Precision.HIGHEST)
    else:
        h = jnp.dot(x_ref[...].astype(BF16), w_ref[...], preferred_element_type=F32)
    h = h + b_ref[...]
    q_ref[...] = (h[:, :D_ATTN] * ATTN_SCALE).astype(q_ref.dtype)
    k_ref[...] = h[:, D_ATTN:D_ATTN + D_KV]
    v_ref[...] = h[:, D_ATTN + D_KV:D_ATTN + 2 * D_KV]
    u_ref[...] = h[:, D_ATTN + 2 * D_KV:].astype(u_ref.dtype)


def _proj(x, w, b, *, tile, exact_f32, q_dtype):
    n = x.shape[0]
    return pl.pallas_call(
        functools.partial(_proj_kernel, exact_f32=exact_f32),
        grid=(n // tile,),
        in_specs=[pl.BlockSpec((tile, D_MODEL), lambda i: (i, 0)),
                  pl.BlockSpec((D_MODEL, D_IN), lambda i: (0, 0)),
                  pl.BlockSpec((1, D_IN), lambda i: (0, 0))],
        out_specs=[pl.BlockSpec((tile, D_ATTN), lambda i: (i, 0)),
                   pl.BlockSpec((tile, D_KV), lambda i: (i, 0)),
                   pl.BlockSpec((tile, D_KV), lambda i: (i, 0)),
                   pl.BlockSpec((tile, D_SSM), lambda i: (i, 0))],
        out_shape=[jax.ShapeDtypeStruct((n, D_ATTN), q_dtype),
                   jax.ShapeDtypeStruct((n, D_KV), F32),
                   jax.ShapeDtypeStruct((n, D_KV), F32),
                   jax.ShapeDtypeStruct((n, D_SSM), F32)],
        compiler_params=_cparams(("parallel",)),
        name="proj",
    )(x, w, b)


ATT_Q_TILE = 512


def _attn_prompt_kernel(sink_ref, q_ref, k_ref, v_ref, o_ref):
    i = pl.program_id(1)
    for blk in range(ATT_Q_TILE // WINDOW):
        q0 = i * ATT_Q_TILE + blk * WINDOW
        k0 = pl.multiple_of(jnp.maximum(q0 - WINDOW, 0), WINDOW)
        kk = k_ref[0, pl.ds(k0, 2 * WINDOW), :].astype(BF16)
        vv = v_ref[0, pl.ds(k0, 2 * WINDOW), :].astype(BF16)
        qb = q_ref[0, blk * WINDOW:(blk + 1) * WINDOW, :]
        qpos = q0 + lax.broadcasted_iota(jnp.int32, (WINDOW, 2 * WINDOW), 0)
        kpos = k0 + lax.broadcasted_iota(jnp.int32, (WINDOW, 2 * WINDOW), 1)
        valid = (kpos <= qpos) & (qpos - kpos <= WINDOW)
        for h in range(N_Q_HEADS):
            kv = h // Q_PER_KV
            qh = qb[:, h * HEAD_DIM:(h + 1) * HEAD_DIM]
            kh = kk[:, kv * HEAD_DIM:(kv + 1) * HEAD_DIM]
            vh = vv[:, kv * HEAD_DIM:(kv + 1) * HEAD_DIM]
            s = lax.dot_general(qh, kh, (((1,), (1,)), ((), ())), preferred_element_type=F32)
            s = jnp.where(valid, s, -jnp.inf)
            sink = sink_ref[h]
            m = jnp.maximum(jnp.max(s, axis=-1, keepdims=True), sink)
            p = jnp.exp(s - m)
            denom = jnp.sum(p, axis=-1, keepdims=True) + jnp.exp(sink - m)
            o = jnp.dot(p.astype(BF16), vh, preferred_element_type=F32) / denom
            o_ref[0, blk * WINDOW:(blk + 1) * WINDOW, h * HEAD_DIM:(h + 1) * HEAD_DIM] = o.astype(o_ref.dtype)


def _attn_prompt(sinks, q, k, v):
    bsz, seq = q.shape[0], q.shape[1]
    return pl.pallas_call(
        _attn_prompt_kernel,
        grid=(bsz, seq // ATT_Q_TILE),
        in_specs=[pl.BlockSpec(memory_space=pltpu.SMEM),
                  pl.BlockSpec((1, ATT_Q_TILE, D_ATTN), lambda b, i: (b, i, 0)),
                  pl.BlockSpec((1, seq, D_KV), lambda b, i: (b, 0, 0)),
                  pl.BlockSpec((1, seq, D_KV), lambda b, i: (b, 0, 0))],
        out_specs=pl.BlockSpec((1, ATT_Q_TILE, D_ATTN), lambda b, i: (b, i, 0)),
        out_shape=jax.ShapeDtypeStruct((bsz, seq, D_ATTN), BF16),
        compiler_params=_cparams(("parallel", "parallel")),
        name="attn_prompt",
    )(sinks, q, k, v)


ATT_S_GROUP = 8


def _attn_sample_kernel(sink_ref, q_ref, kn_ref, vn_ref, kb_ref, vb_ref, o_ref):
    g = ATT_S_GROUP
    rows = Q_PER_KV * g
    ncol = g * WINDOW
    kb = kb_ref[...].reshape(ncol, D_KV).astype(BF16)
    vb = vb_ref[...].reshape(ncol, D_KV).astype(BF16)
    rseq = lax.broadcasted_iota(jnp.int32, (rows, ncol), 0) % g
    cseq = lax.broadcasted_iota(jnp.int32, (rows, ncol), 1) // WINDOW
    own = rseq == cseq
    rhead = lax.broadcasted_iota(jnp.int32, (rows, 1), 0) // g
    for kv in range(N_KV_HEADS):
        lo = kv * HEAD_DIM
        qs = jnp.concatenate(
            [q_ref[:, (kv * Q_PER_KV + h) * HEAD_DIM:(kv * Q_PER_KV + h + 1) * HEAD_DIM] for h in range(Q_PER_KV)],
            axis=0)
        kn = jnp.concatenate([kn_ref[:, lo:lo + HEAD_DIM]] * Q_PER_KV, axis=0)
        vn = jnp.concatenate([vn_ref[:, lo:lo + HEAD_DIM]] * Q_PER_KV, axis=0)
        sink = jnp.zeros((rows, 1), F32)
        for h in range(Q_PER_KV):
            sink = jnp.where(rhead == h, sink_ref[kv * Q_PER_KV + h], sink)
        qs = qs.astype(BF16)
        s = lax.dot_general(qs, kb[:, lo:lo + HEAD_DIM], (((1,), (1,)), ((), ())), preferred_element_type=F32)
        s = jnp.where(own, s, -jnp.inf)
        s_new = jnp.sum(qs.astype(F32) * kn.astype(BF16).astype(F32), axis=-1, keepdims=True)
        m = jnp.maximum(jnp.maximum(jnp.max(s, axis=-1, keepdims=True), s_new), sink)
        p = jnp.exp(s - m)
        p_new = jnp.exp(s_new - m)
        denom = jnp.sum(p, axis=-1, keepdims=True) + p_new + jnp.exp(sink - m)
        o = jnp.dot(p.astype(BF16), vb[:, lo:lo + HEAD_DIM], preferred_element_type=F32)
        o = (o + p_new.astype(BF16).astype(F32) * vn.astype(BF16).astype(F32)) / denom
        for h in range(Q_PER_KV):
            c0 = (kv * Q_PER_KV + h) * HEAD_DIM
            o_ref[:, c0:c0 + HEAD_DIM] = o[h * g:(h + 1) * g].astype(o_ref.dtype)


def _attn_sample(sinks, q, k_new, v_new, k_buf, v_buf):
    n = q.shape[0]
    g = ATT_S_GROUP
    return pl.pallas_call(
        _attn_sample_kernel,
        grid=(n // g,),
        in_specs=[pl.BlockSpec(memory_space=pltpu.SMEM),
                  pl.BlockSpec((g, D_ATTN), lambda i: (i, 0)),
                  pl.BlockSpec((g, D_KV), lambda i: (i, 0)),
                  pl.BlockSpec((g, D_KV), lambda i: (i, 0)),
                  pl.BlockSpec((g, WINDOW, D_KV), lambda i: (i, 0, 0)),
                  pl.BlockSpec((g, WINDOW, D_KV), lambda i: (i, 0, 0))],
        out_specs=pl.BlockSpec((g, D_ATTN), lambda i: (i, 0)),
        out_shape=jax.ShapeDtypeStruct((n, D_ATTN), F32),
        compiler_params=_cparams(("parallel",)),
        name="attn_sample",
    )(sinks, q, k_new, v_new, k_buf, v_buf)


def _s5_params(a_re, a_im, log_dt, b_re, b_im, c_re, c_im):
    hp = lax.Precision.HIGHEST
    dt = jnp.exp(log_dt.astype(F32))[:, None]
    are, aim = a_re.astype(F32), a_im.astype(F32)
    tau = jnp.arange(S5_CHUNK + 1, dtype=F32)[None, :, None]
    mag = jnp.exp(tau * (dt * are)[:, None, :])
    ang = tau * (dt * aim)[:, None, :]
    pw_re, pw_im = mag * jnp.cos(ang), mag * jnp.sin(ang)
    ab_re, ab_im = pw_re[:, 1], pw_im[:, 1]
    den = are * are + aim * aim
    f_re = ((ab_re - 1.0) * are + ab_im * aim) / den
    f_im = (ab_im * are - (ab_re - 1.0) * aim) / den
    bre, bim = b_re.astype(F32), b_im.astype(F32)
    bb_re = f_re[..., None] * bre - f_im[..., None] * bim
    bb_im = f_re[..., None] * bim + f_im[..., None] * bre
    cre, cim = c_re.astype(F32), c_im.astype(F32)
    return dict(pw_re=pw_re, pw_im=pw_im, ab_re=ab_re, ab_im=ab_im, bb_re=bb_re, bb_im=bb_im,
                c_re=cre, c_im=cim, hp=hp)


def _s5_chunk_mats(sp, d_skip):
    hp = sp["hp"]
    g, t, c, p = N_SSM_GROUPS, S5_CHUNK, SSM_GROUP, SSM_STATE
    pw_re, pw_im = sp["pw_re"], sp["pw_im"]
    ca_re = sp["c_re"][:, None] * pw_re[:, :, None, :] - sp["c_im"][:, None] * pw_im[:, :, None, :]
    ca_im = sp["c_re"][:, None] * pw_im[:, :, None, :] + sp["c_im"][:, None] * pw_re[:, :, None, :]
    kern = (jnp.einsum("gtcp,gpd->gtcd", ca_re[:, :t], sp["bb_re"], precision=hp)
            - jnp.einsum("gtcp,gpd->gtcd", ca_im[:, :t], sp["bb_im"], precision=hp))
    kc = jnp.swapaxes(kern, 2, 3)
    kc = kc.at[:, 0].add(d_skip.astype(F32).reshape(g, 1, c) * jnp.eye(c, dtype=F32)[None])
    rev_re, rev_im = pw_re[:, t - 1::-1][:, :t], pw_im[:, t - 1::-1][:, :t]
    wst_re = rev_re[:, :, None, :] * jnp.swapaxes(sp["bb_re"], 1, 2)[:, None] \
        - rev_im[:, :, None, :] * jnp.swapaxes(sp["bb_im"], 1, 2)[:, None]
    wst_im = rev_re[:, :, None, :] * jnp.swapaxes(sp["bb_im"], 1, 2)[:, None] \
        + rev_im[:, :, None, :] * jnp.swapaxes(sp["bb_re"], 1, 2)[:, None]
    wo_re = jnp.transpose(ca_re[:, 1:t + 1], (0, 3, 1, 2))
    wo_im = -jnp.transpose(ca_im[:, 1:t + 1], (0, 3, 1, 2))
    nv, gl = g // S5_LANE_GROUPS, S5_LANE_GROUPS
    kc, wst_re, wst_im, wo_re, wo_im = lax.optimization_barrier((kc, wst_re, wst_im, wo_re, wo_im))
    kc5 =jnp.transpose(kc.reshape(nv, gl, t, c, c), (0, 2, 1, 3, 4))
    ws6 = jnp.transpose(jnp.stack([wst_re, wst_im], axis=3).reshape(nv, gl, t, c, 2, p),
                        (0, 2, 1, 3, 4, 5))
    wo6 = jnp.transpose(jnp.stack([wo_re, wo_im], axis=0).reshape(2, nv, gl, p, t, c),
                        (1, 0, 2, 3, 4, 5))
    kc5, ws6, wo6 = lax.optimization_barrier((kc5.astype(BF16), ws6.astype(BF16), wo6.astype(BF16)))
    gid = jnp.arange(gl)
    zero = jnp.zeros((), BF16)
    same5 = (gid[:, None, None, None] == gid[None, None, :, None])
    bd = jnp.where(same5[None, None], kc5[:, :, :, :, None, :], zero).reshape(nv, t, LANES, LANES)
    spread_s = np.zeros((2 * p, 2 * gl * p), np.float32)
    spread_o = np.zeros((t * c, t * LANES), np.float32)
    for h in range(gl):
        for ri in range(2):
            spread_s[ri * p + np.arange(p), ri * gl * p + h * p + np.arange(p)] = 1.0
        for tt in range(t):
            spread_o[tt * c + np.arange(c), tt * LANES + h * c + np.arange(c)] = 1.0
    row_g = (jnp.arange(t * LANES) // c) % gl
    blk_g = (jnp.arange(2 * gl * p) // p) % gl
    wst_v = jnp.einsum("vrk,kc->vrc", ws6.reshape(nv, t * LANES, 2 * p), jnp.asarray(spread_s, BF16),
                       preferred_element_type=F32)
    wst_v = jnp.where((row_g[:, None] == blk_g[None, :])[None], wst_v, 0.0).astype(BF16)
    wout_v = jnp.einsum("vrk,kc->vrc", wo6.reshape(nv, 2 * gl * p, t * c), jnp.asarray(spread_o, BF16),
                        preferred_element_type=F32)
    wout_v = jnp.where((blk_g[:, None] == row_g[None, :])[None], wout_v, 0.0).astype(BF16)
    at_re = pw_re[:, t].reshape(1, g * p)
    at_im = pw_im[:, t].reshape(1, g * p)
    return bd, wst_v, wout_v, at_re, at_im


def _s5_chunk_rows(u_ref, nchunk):
    return jnp.concatenate(
        [u_ref[pl.ds(s, nchunk, stride=S5_CHUNK), :] for s in range(S5_CHUNK)], axis=1).astype(BF16)


S5_SLABS = S5_LANE_GROUPS * SSM_STATE // LANES


def _s5_state_kernel(u_ref, wst_ref, sre_ref, sim_ref):
    nchunk = sre_ref.shape[1]
    s = jnp.dot(_s5_chunk_rows(u_ref, nchunk), wst_ref[0], preferred_element_type=F32)
    for k in range(S5_SLABS):
        sre_ref[k] = s[:, k * LANES:(k + 1) * LANES]
        sim_ref[k] = s[:, (S5_SLABS + k) * LANES:(S5_SLABS + k + 1) * LANES]


def _s5_scan_kernel(sre_ref, sim_ref, are_ref, aim_ref, hre_ref, him_ref, fre_ref, fim_ref, *, bsz):
    nchunk = sre_ref.shape[1] // bsz
    are = [jnp.broadcast_to(are_ref[:, k * LANES:(k + 1) * LANES], (bsz, LANES)) for k in range(S5_SLABS)]
    aim = [jnp.broadcast_to(aim_ref[:, k * LANES:(k + 1) * LANES], (bsz, LANES)) for k in range(S5_SLABS)]

    def body(j, carry):
        rows = pl.ds(j, bsz, stride=nchunk)
        out = []
        for k in range(S5_SLABS):
            cre, cim = carry[2 * k], carry[2 * k + 1]
            hre_ref[k, rows, :] = cre
            him_ref[k, rows, :] = cim
            sr = sre_ref[k, rows, :]
            si = sim_ref[k, rows, :]
            out += [are[k] * cre - aim[k] * cim + sr, are[k] * cim + aim[k] * cre + si]
        return tuple(out)

    zero = jnp.zeros((bsz, LANES), F32)
    fin = lax.fori_loop(0, nchunk, body, (zero,) * (2 * S5_SLABS))
    fre_ref[...] = jnp.concatenate(fin[0::2], axis=1)
    fim_ref[...] = jnp.concatenate(fin[1::2], axis=1)


def _s5_out_kernel(u_ref, bd_ref, hre_ref, him_ref, wout_ref, y_ref, m_sc):
    nchunk = hre_ref.shape[1]

    @pl.when(pl.program_id(1) == 0)
    def _():
        for s in range(S5_CHUNK):
            for t in range(S5_CHUNK):
                blk = bd_ref[0, t - s] if t >= s else jnp.zeros((LANES, LANES), BF16)
                m_sc[s * LANES:(s + 1) * LANES, t * LANES:(t + 1) * LANES] = blk

    hcat = jnp.concatenate([hre_ref[k] for k in range(S5_SLABS)] + [him_ref[k] for k in range(S5_SLABS)],
                           axis=1).astype(BF16)
    y = jnp.dot(_s5_chunk_rows(u_ref, nchunk), m_sc[...], preferred_element_type=F32)
    y = y + jnp.dot(hcat, wout_ref[0], preferred_element_type=F32)
    for s in range(S5_CHUNK):
        y_ref[pl.ds(s, nchunk, stride=S5_CHUNK), :] = y[:, s * LANES:(s + 1) * LANES]


def _s5_prompt(u, bsz, seq, mats):
    bd, wst_v, wout_v, at_re, at_im = mats
    g, t, p = N_SSM_GROUPS, S5_CHUNK, SSM_STATE
    nchunk = seq // t
    n = nchunk * bsz
    nv = g // S5_LANE_GROUPS
    half = S5_LANE_GROUPS * p
    s_re, s_im = pl.pallas_call(
        _s5_state_kernel,
        grid=(nv, bsz),
        in_specs=[pl.BlockSpec((seq, LANES), lambda v, b: (b, v)),
                  pl.BlockSpec((1, t * LANES, 2 * half), lambda v, b: (v, 0, 0))],
        out_specs=[pl.BlockSpec((S5_SLABS, nchunk, LANES), lambda v, b: (v, b, 0)),
                   pl.BlockSpec((S5_SLABS, nchunk, LANES), lambda v, b: (v, b, 0))],
        out_shape=[jax.ShapeDtypeStruct((nv * S5_SLABS, n, LANES), F32)] * 2,
        compiler_params=_cparams(("parallel", "parallel"), VMEM_LIMIT),
        name="s5_state",
    )(u, wst_v)
    h_re, h_im, f_re, f_im = pl.pallas_call(
        functools.partial(_s5_scan_kernel, bsz=bsz),
        grid=(nv,),
        in_specs=[pl.BlockSpec((S5_SLABS, n, LANES), lambda i: (i, 0, 0)),
                  pl.BlockSpec((S5_SLABS, n, LANES), lambda i: (i, 0, 0)),
                  pl.BlockSpec((1, half), lambda i: (0, i)),
                  pl.BlockSpec((1, half), lambda i: (0, i))],
        out_specs=[pl.BlockSpec((S5_SLABS, n, LANES), lambda i: (i, 0, 0)),
                   pl.BlockSpec((S5_SLABS, n, LANES), lambda i: (i, 0, 0)),
                   pl.BlockSpec((bsz, half), lambda i: (0, i)),
                   pl.BlockSpec((bsz, half), lambda i: (0, i))],
        out_shape=[jax.ShapeDtypeStruct((nv * S5_SLABS, n, LANES), F32)] * 2
        + [jax.ShapeDtypeStruct((bsz, g * p), F32)] * 2,
        compiler_params=_cparams(("parallel",)),
        name="s5_scan",
    )(s_re, s_im, at_re, at_im)
    y = pl.pallas_call(
        _s5_out_kernel,
        grid=(nv, bsz),
        in_specs=[pl.BlockSpec((seq, LANES), lambda v, b: (b, v)),
                  pl.BlockSpec((1, t, LANES, LANES), lambda v, b: (v, 0, 0, 0)),
                  pl.BlockSpec((S5_SLABS, nchunk, LANES), lambda v, b: (v, b, 0)),
                  pl.BlockSpec((S5_SLABS, nchunk, LANES), lambda v, b: (v, b, 0)),
                  pl.BlockSpec((1, 2 * half, t * LANES), lambda v, b: (v, 0, 0))],
        out_specs=pl.BlockSpec((seq, LANES), lambda v, b: (b, v)),
        out_shape=jax.ShapeDtypeStruct((bsz * seq, D_SSM), F32),
        scratch_shapes=[pltpu.VMEM((t * LANES, t * LANES), BF16)],
        compiler_params=_cparams(("parallel", "arbitrary"), VMEM_LIMIT),
        name="s5_out",
    )(u, bd, h_re, h_im, wout_v)
    return y, f_re, f_im


S5S_GROUPS = LANES // SSM_GROUP


def _s5_sample_mats(sp, d_skip):
    go, gl, c, p = N_SSM_GROUPS // S5S_GROUPS, S5S_GROUPS, SSM_GROUP, SSM_STATE
    eye = jnp.eye(gl, dtype=F32)

    def bdiag_in(b):
        b4 = b.reshape(go, gl, p, c)
        return jnp.einsum("ogpc,gh->ogchp", b4, eye).reshape(go, gl * c, gl * p)

    def bdiag_out(cm):
        c4 = cm.reshape(go, gl, c, p)
        return jnp.einsum("ogcp,gh->ogphc", c4, eye).reshape(go, gl * p, gl * c)

    b8 = jnp.concatenate([bdiag_in(sp["bb_re"]), bdiag_in(sp["bb_im"])], axis=2)
    c8 = jnp.concatenate([bdiag_out(sp["c_re"]), -bdiag_out(sp["c_im"])], axis=1)
    a_re = sp["ab_re"].reshape(1, N_SSM_GROUPS * p)
    a_im = sp["ab_im"].reshape(1, N_SSM_GROUPS * p)
    return b8, c8, a_re, a_im, d_skip.astype(F32).reshape(1, D_SSM)


def _s5_sample_kernel(u_ref, hre_ref, him_ref, b8_ref, c8_ref, are_ref, aim_ref, d_ref,
                      y_ref, ore_ref, oim_ref):
    hp = lax.Precision.HIGHEST
    u = u_ref[...]
    half = S5S_GROUPS * SSM_STATE
    bu = jnp.dot(u, b8_ref[0], preferred_element_type=F32, precision=hp)
    are, aim = are_ref[...], aim_ref[...]
    h0r, h0i = hre_ref[...], him_ref[...]
    hr = are * h0r - aim * h0i + bu[:, :half]
    hi = are * h0i + aim * h0r + bu[:, half:]
    ore_ref[...] = hr
    oim_ref[...] = hi
    y = jnp.dot(jnp.concatenate([hr, hi], axis=1), c8_ref[0], preferred_element_type=F32, precision=hp)
    y_ref[...] = (y + d_ref[...] * u).astype(y_ref.dtype)


def _s5_sample(u, h0_re, h0_im, mats):
    b8, c8, a_re, a_im, d = mats
    n = u.shape[0]
    half = S5S_GROUPS * SSM_STATE
    return pl.pallas_call(
        _s5_sample_kernel,
        grid=(N_SSM_GROUPS // S5S_GROUPS,),
        in_specs=[pl.BlockSpec((n, LANES), lambda i: (0, i)),
                  pl.BlockSpec((n, half), lambda i: (0, i)),
                  pl.BlockSpec((n, half), lambda i: (0, i)),
                  pl.BlockSpec((1, LANES, 2 * half), lambda i: (i, 0, 0)),
                  pl.BlockSpec((1, 2 * half, LANES), lambda i: (i, 0, 0)),
                  pl.BlockSpec((1, half), lambda i: (0, i)),
                  pl.BlockSpec((1, half), lambda i: (0, i)),
                  pl.BlockSpec((1, LANES), lambda i: (0, i))],
        out_specs=[pl.BlockSpec((n, LANES), lambda i: (0, i)),
                   pl.BlockSpec((n, half), lambda i: (0, i)),
                   pl.BlockSpec((n, half), lambda i: (0, i))],
        out_shape=[jax.ShapeDtypeStruct((n, D_SSM), BF16),
                   jax.ShapeDtypeStruct((n, N_SSM_GROUPS * SSM_STATE), F32),
                   jax.ShapeDtypeStruct((n, N_SSM_GROUPS * SSM_STATE), F32)],
        compiler_params=_cparams(("parallel",)),
        name="s5_sample",
    )(u, h0_re, h0_im, b8, c8, a_re, a_im, d)


def _layer_norm(x, g, b):
    mu = jnp.mean(x, axis=-1, keepdims=True)
    xc = x - mu
    var = jnp.mean(xc * xc, axis=-1, keepdims=True)
    return xc * lax.rsqrt(var + LN_EPS) * g + b


RUN_ROWS = SUBLANES
TAB_ROWS = 3


def _merge_kernel(x_ref, oa_ref, ys_ref, carry_in_ref, wao_ref, wso_ref, wg_ref, bg_ref, wo_ref,
                  g1_ref, b1_ref, wrt_ref, brt_ref,
                  x1_ref, lpos_ref, cols_ref, tab_ref, carry_out_ref, carry_sc):
    step = pl.program_id(0)

    @pl.when(step == 0)
    def _():
        carry_sc[...] = carry_in_ref[...]

    tm = x_ref.shape[0]
    x = x_ref[...]
    branch_a = jnp.dot(oa_ref[...].astype(BF16), wao_ref[...], preferred_element_type=F32)
    z = jnp.dot(jax.nn.gelu(ys_ref[...].astype(F32)).astype(BF16), wso_ref[...], preferred_element_type=F32)
    branch_b = z[:, :D_MODEL] * jax.nn.sigmoid(z[:, D_MODEL:])
    gates = jax.nn.sigmoid(jnp.dot(x.astype(BF16), wg_ref[...], preferred_element_type=F32) + bg_ref[...])
    mixed = gates[:, :D_MODEL] * branch_a + gates[:, D_MODEL:] * branch_b
    mix = jnp.dot(mixed.astype(BF16), wo_ref[...], preferred_element_type=F32)
    x1 = _layer_norm(DEEPNORM_ALPHA * x + mix, g1_ref[...], b1_ref[...])
    x1_ref[...] = x1

    def split2(v):
        hi = v.astype(BF16)
        return hi, (v - hi.astype(F32)).astype(BF16)

    def dot_nt(a, b):
        return lax.dot_general(a, b, (((1,), (1,)), ((), ())), preferred_element_type=F32)

    w_hi, w_lo = split2(wrt_ref[...])
    x_hi, x_lo = split2(x1)
    logits = dot_nt(w_hi, x_hi) + dot_nt(w_hi, x_lo) + dot_nt(w_lo, x_hi) + brt_ref[...]
    sub = lax.broadcasted_iota(jnp.int32, (N_EXPERTS, tm), 0)
    work = logits
    vals, sels = [], []
    for _ in range(TOP_K):
        mx = jnp.max(work, axis=0, keepdims=True)
        idx = jnp.min(jnp.where(work == mx, sub, N_EXPERTS), axis=0, keepdims=True)
        sel = sub == idx
        vals.append(mx)
        sels.append(sel)
        work = jnp.where(sel, -jnp.inf, work)
    ex = [jnp.exp(v - vals[0]) for v in vals]
    tot = ex[0] + ex[1] + ex[2] + ex[3]
    gate_rows = jnp.concatenate([e / tot for e in ex], axis=0)

    multi = jnp.zeros((N_EXPERTS, tm), F32)
    for sel in sels:
        multi = multi + jnp.where(sel, 1.0, 0.0)
    multi_b = multi.astype(BF16)
    r = lax.broadcasted_iota(jnp.int32, (tm, tm), 0)
    c = lax.broadcasted_iota(jnp.int32, (tm, tm), 1)
    earlier = jnp.dot(multi_b, jnp.where(r < c, 1.0, 0.0).astype(BF16), preferred_element_type=F32)
    cnt_col = jnp.sum(multi, axis=1, keepdims=True)
    nb_col = jnp.floor((cnt_col + (RUN_ROWS - 1.0)) * (1.0 / RUN_ROWS))
    er = lax.broadcasted_iota(jnp.int32, (N_EXPERTS, N_EXPERTS), 0)
    ec = lax.broadcasted_iota(jnp.int32, (N_EXPERTS, N_EXPERTS), 1)
    loff_col = jnp.dot(jnp.where(ec < er, 1.0, 0.0).astype(BF16),
                       jnp.broadcast_to(nb_col, (N_EXPERTS, tm)).astype(BF16), preferred_element_type=F32)
    base = RUN_ROWS * loff_col + earlier
    lpos = jnp.concatenate([jnp.sum(jnp.where(sel, base, 0.0), axis=0, keepdims=True) for sel in sels], axis=0)
    lpos_ref[...] = lpos.astype(jnp.int32)
    rows_hi, rows_lo = split2(jnp.concatenate([lpos, gate_rows], axis=0))
    eye = jnp.where(r == c, 1.0, 0.0).astype(BF16)
    cols_ref[...] = dot_nt(eye, rows_hi) + dot_nt(eye, rows_lo)

    cnt_row = dot_nt(jnp.ones((SUBLANES, tm), BF16), multi_b)
    nb_row = jnp.floor((cnt_row + (RUN_ROWS - 1.0)) * (1.0 / RUN_ROWS))
    loff_row = jnp.dot(nb_row.astype(BF16), jnp.where(er < ec, 1.0, 0.0).astype(BF16), preferred_element_type=F32)
    lane_pad = jnp.zeros((SUBLANES, LANES - N_EXPERTS), F32)
    nb_p = jnp.concatenate([nb_row, lane_pad], axis=1)
    loff_p = jnp.concatenate([loff_row, lane_pad], axis=1)
    goff_p = carry_sc[...]
    rid = lax.broadcasted_iota(jnp.int32, (SUBLANES, LANES), 0)
    tab = jnp.where(rid == 0, nb_p, jnp.where(rid == 1, loff_p, jnp.where(rid == 2, goff_p, 0.0)))
    tab_ref[0] = tab.astype(jnp.int32)
    carry_sc[...] = goff_p + nb_p
    carry_out_ref[...] = carry_sc[...]


def _merge(x, o_attn, y_ssm, carry_in, w, *, tile):
    n = x.shape[0]
    nt = n // tile
    full = lambda shape: pl.BlockSpec(shape, lambda i: (0,) * len(shape))
    return pl.pallas_call(
        _merge_kernel,
        grid=(nt,),
        in_specs=[pl.BlockSpec((tile, D_MODEL), lambda i: (i, 0)),
                  pl.BlockSpec((tile, D_ATTN), lambda i: (i, 0)),
                  pl.BlockSpec((tile, D_SSM), lambda i: (i, 0)),
                  full((SUBLANES, LANES)),
                  full((D_ATTN, D_MODEL)), full((D_SSM, 2 * D_MODEL)), full((D_MODEL, 2 * D_MODEL)),
                  full((1, 2 * D_MODEL)), full((D_MODEL, D_MODEL)),
                  full((1, D_MODEL)), full((1, D_MODEL)),
                  full((N_EXPERTS, D_MODEL)), full((N_EXPERTS, 1))],
        out_specs=[pl.BlockSpec((tile, D_MODEL), lambda i: (i, 0)),
                   pl.BlockSpec((TOP_K, tile), lambda i: (0, i)),
                   pl.BlockSpec((tile, 2 * TOP_K), lambda i: (i, 0)),
                   pl.BlockSpec((1, SUBLANES, LANES), lambda i: (i, 0, 0)),
                   full((SUBLANES, LANES))],
        out_shape=[jax.ShapeDtypeStruct((n, D_MODEL), F32),
                   jax.ShapeDtypeStruct((TOP_K, n), jnp.int32),
                   jax.ShapeDtypeStruct((n, 2 * TOP_K), F32),
                   jax.ShapeDtypeStruct((nt, SUBLANES, LANES), jnp.int32),
                   jax.ShapeDtypeStruct((SUBLANES, LANES), F32)],
        scratch_shapes=[pltpu.VMEM((SUBLANES, LANES), F32)],
        compiler_params=_cparams(("arbitrary",), VMEM_LIMIT),
        name="merge",
    )(x, o_attn, y_ssm, carry_in, w["wao"], w["wso"], w["wg"], w["bg"], w["wo"], w["g1"], w["b1"],
      w["wrt"], w["brt"])


def _tab(tab_ref, tile, row, e):
    return tab_ref[(tile * TAB_ROWS + row) * N_EXPERTS + e]


BIG_PIECE = 4 * RUN_ROWS
MAX_UNITS_LOG2 = 8


def _for_each_run_piece(tab_ref, tile, fn):
    def per_expert(e, carry):
        loff = RUN_ROWS * _tab(tab_ref, tile, 1, e)
        goff = RUN_ROWS * _tab(tab_ref, tile, 2, e)
        units = _tab(tab_ref, tile, 0, e)
        n_big = lax.shift_right_logical(units, 2)

        def big(j, c2):
            fn(pl.multiple_of(loff + j * BIG_PIECE, RUN_ROWS), goff + j * BIG_PIECE, e, BIG_PIECE)
            return c2

        lax.fori_loop(0, n_big, big, 0)
        done = n_big * BIG_PIECE

        def small(j, c2):
            fn(pl.multiple_of(loff + done + j * RUN_ROWS, RUN_ROWS), goff + done + j * RUN_ROWS, e, RUN_ROWS)
            return c2

        lax.fori_loop(0, units & 3, small, 0)
        return carry

    lax.fori_loop(0, N_EXPERTS, per_expert, 0)


def _drain_units(units, wait_copy, buffer_rows):
    assert buffer_rows < (RUN_ROWS << MAX_UNITS_LOG2)
    for b in range(MAX_UNITS_LOG2):
        if (RUN_ROWS << b) > buffer_rows:
            break

        @pl.when((lax.shift_right_logical(units, b) & 1) == 1)
        def _():
            wait_copy(RUN_ROWS << b).wait()


def _dispatch_kernel(tab_ref, seg_ref, tot_ref, tail_ref, lpos_p_ref, xp_ref, lpos_s_ref, xs_in_ref, xs_ref,
                     loc_sc, zero_sc, sem, zsem):
    i = pl.program_id(0)
    last = pl.num_programs(0) - 1
    tile = i
    slot = i % 2
    loc = loc_sc.shape[1]

    @pl.when(i == 0)
    def _():
        zero_sc[...] = jnp.zeros_like(zero_sc)

        def tail_copy(e, j):
            row = pl.multiple_of(RUN_ROWS * (tail_ref[e] + j), RUN_ROWS)
            return pltpu.make_async_copy(zero_sc.at[pl.ds(0, RUN_ROWS)], xs_ref.at[pl.ds(row, RUN_ROWS)], zsem)

        def per_expert(e, carry):
            n = tail_ref[N_EXPERTS + e]
            lax.fori_loop(0, n, lambda j, c2: (tail_copy(e, j).start(), c2)[1], 0)
            lax.fori_loop(0, n, lambda j, c2: (tail_copy(e, j).wait(), c2)[1], 0)
            return carry

        lax.fori_loop(0, N_EXPERTS, per_expert, 0)

        def block_copy(b):
            row = pl.multiple_of(b * MOE_ROWS, MOE_ROWS)
            return pltpu.make_async_copy(zero_sc, xs_ref.at[pl.ds(row, MOE_ROWS)], zsem)

        first_unused, n_blocks = tail_ref[2 * N_EXPERTS], xs_ref.shape[0] // MOE_ROWS
        lax.fori_loop(first_unused, n_blocks, lambda b, c2: (block_copy(b).start(), c2)[1], 0)
        lax.fori_loop(first_unused, n_blocks, lambda b, c2: (block_copy(b).wait(), c2)[1], 0)

    def sort_tile(lpos_ref, x_ref):
        tm = x_ref.shape[0]
        rows = lax.broadcasted_iota(jnp.int32, (loc, tm), 0)
        lp = lpos_ref[...]
        onehot = jnp.zeros((loc, tm), F32)
        for k in range(TOP_K):
            onehot = jnp.where(rows == lp[k:k + 1], 1.0, onehot)
        loc_sc[slot] = jnp.dot(onehot.astype(BF16), x_ref[...].astype(BF16), preferred_element_type=F32)

    @pl.when(i < last)
    def _():
        sort_tile(lpos_p_ref, xp_ref)

    @pl.when(i == last)
    def _():
        sort_tile(lpos_s_ref, xs_in_ref)

    def piece_copy(sl, lrow, grow, e, n):
        dst = pl.multiple_of(seg_ref[e] + grow, RUN_ROWS)
        return pltpu.make_async_copy(loc_sc.at[sl, pl.ds(lrow, n)], xs_ref.at[pl.ds(dst, n)], sem.at[sl])

    _for_each_run_piece(tab_ref, tile, lambda lrow, grow, e, n: piece_copy(slot, lrow, grow, e, n).start())

    def drain(tl, sl):
        _drain_units(tot_ref[tl], lambda n: piece_copy(sl, 0, 0, 0, n), loc)

    @pl.when(i > 0)
    def _():
        drain(tile - 1, 1 - slot)

    @pl.when(i == last)
    def _():
        drain(tile, slot)


def _dispatch(tab, seg_start, tot, tails, lpos_p, x1_p, lpos_s, x1_s, *, tile, nrows):
    nt_p = x1_p.shape[0] // tile
    ns = x1_s.shape[0]
    loc = tile * TOP_K + N_EXPERTS * RUN_ROWS
    prompt_blk = lambda i, *_: jnp.minimum(i, nt_p - 1)
    return pl.pallas_call(
        _dispatch_kernel,
        grid_spec=pltpu.PrefetchScalarGridSpec(
            num_scalar_prefetch=4,
            grid=(nt_p + 1,),
            in_specs=[pl.BlockSpec((TOP_K, tile), lambda i, *_: (0, prompt_blk(i))),
                      pl.BlockSpec((tile, D_MODEL), lambda i, *_: (prompt_blk(i), 0)),
                      pl.BlockSpec((TOP_K, ns), lambda i, *_: (0, 0)),
                      pl.BlockSpec((ns, D_MODEL), lambda i, *_: (0, 0))],
            out_specs=pl.BlockSpec(memory_space=pl.ANY),
            scratch_shapes=[pltpu.VMEM((2, loc, D_MODEL), F32), pltpu.VMEM((MOE_ROWS, D_MODEL), F32),
                            pltpu.SemaphoreType.DMA((2,)), pltpu.SemaphoreType.DMA(())]),
        out_shape=jax.ShapeDtypeStruct((nrows, D_MODEL), F32),
        compiler_params=_cparams(("arbitrary",), VMEM_LIMIT),
        name="dispatch",
    )(tab, seg_start, tot, tails, lpos_p, x1_p, lpos_s, x1_s)


def _deinterleave_matrix():
    pm = np.zeros((MXU_DIM, MXU_DIM), np.float32)
    half = MXU_DIM // 2
    for c in range(half):
        pm[2 * c, c] = 1.0
        pm[2 * c + 1, half + c] = 1.0
    return pm


def _expert_kernel(be_ref, nu_ref, ord_ref, nxt_ref, xs_ref, w1_hbm, b1_ref, w2_hbm, b2_ref, pm_ref, y_ref,
                   w1f_sc, w2f_sc, w1p_sc, w2b_sc, sem):
    i = pl.program_id(0)
    e = be_ref[i]
    prev = be_ref[jnp.maximum(i - 1, 0)]
    nblk = 2 * D_FF // MXU_DIM

    def weight_copies(expert, slot):
        return (pltpu.make_async_copy(w1_hbm.at[expert], w1f_sc.at[slot], sem.at[0, slot]),
                pltpu.make_async_copy(w2_hbm.at[expert], w2f_sc.at[slot], sem.at[1, slot]))

    @pl.when(i == 0)
    def _():
        for cp in weight_copies(e, 0):
            cp.start()

    @pl.when((i == 0) | (e != prev))
    def _():
        slot = ord_ref[i] % 2
        for cp in weight_copies(e, slot):
            cp.wait()
        nxt = nxt_ref[i]

        @pl.when(nxt >= 0)
        def _():
            for cp in weight_copies(nxt, 1 - slot):
                cp.start()

        for cb in range(nblk):
            blk = w1f_sc[slot, :, cb * MXU_DIM:(cb + 1) * MXU_DIM].astype(BF16)
            w1p_sc[:, cb * MXU_DIM:(cb + 1) * MXU_DIM] = jnp.dot(
                blk, pm_ref[...], preferred_element_type=F32).astype(BF16)
        w2b_sc[...] = w2f_sc[slot].astype(BF16)

    @pl.when(i < nu_ref[0])
    def _():
        x = xs_ref[...].astype(BF16)
        h = jnp.dot(x, w1p_sc[...], preferred_element_type=F32) + b1_ref[0]
        half = MXU_DIM // 2
        acts = []
        for cb in range(nblk):
            x_glu = jnp.minimum(h[:, cb * MXU_DIM:cb * MXU_DIM + half], SWIGLU_LIMIT)
            x_lin = jnp.clip(h[:, cb * MXU_DIM + half:(cb + 1) * MXU_DIM], -SWIGLU_LIMIT, SWIGLU_LIMIT)
            acts.append((x_glu * jax.nn.sigmoid(SWIGLU_ALPHA * x_glu) * (x_lin + 1.0)).astype(BF16))
        act = jnp.concatenate(acts, axis=1)
        y_ref[...] = jnp.dot(act, w2b_sc[...], preferred_element_type=F32) + b2_ref[0]

    @pl.when(i >= nu_ref[0])
    def _():
        y_ref[...] = jnp.zeros_like(y_ref)


def _experts(block_e, n_used, run_ord, run_next, xs, w1, b1p, w2, b2, pm):
    nrows = xs.shape[0]
    nb = nrows // MOE_ROWS
    return pl.pallas_call(
        _expert_kernel,
        grid_spec=pltpu.PrefetchScalarGridSpec(
            num_scalar_prefetch=4,
            grid=(nb,),
            in_specs=[pl.BlockSpec((MOE_ROWS, D_MODEL), lambda i, be, nu, ro, rn: (jnp.minimum(i, nu[0] - 1), 0)),
                      pl.BlockSpec(memory_space=pl.ANY),
                      pl.BlockSpec((1, 1, 2 * D_FF), lambda i, be, nu, ro, rn: (be[i], 0, 0)),
                      pl.BlockSpec(memory_space=pl.ANY),
                      pl.BlockSpec((1, 1, D_MODEL), lambda i, be, nu, ro, rn: (be[i], 0, 0)),
                      pl.BlockSpec((MXU_DIM, MXU_DIM), lambda i, be, nu, ro, rn: (0, 0))],
            out_specs=pl.BlockSpec((MOE_ROWS, D_MODEL), lambda i, be, nu, ro, rn: (i, 0)),
            scratch_shapes=[pltpu.VMEM((2, D_MODEL, 2 * D_FF), F32), pltpu.VMEM((2, D_FF, D_MODEL), F32),
                            pltpu.VMEM((D_MODEL, 2 * D_FF), BF16), pltpu.VMEM((D_FF, D_MODEL), BF16),
                            pltpu.SemaphoreType.DMA((2, 2))]),
        out_shape=jax.ShapeDtypeStruct((nrows, D_MODEL), F32),
        compiler_params=_cparams(("arbitrary",), VMEM_LIMIT),
        name="experts",
    )(block_e, n_used, run_ord, run_next, xs, w1, b1p, w2, b2, pm)


def _combine_kernel(tab_ref, seg_ref, tot_ref, cols_ref, x1_ref, g2_ref, b2_ref, ys_ref, y_ref, loc_sc, sem,
                    *, tile_base):
    i = pl.program_id(0)
    last = pl.num_programs(0) - 1
    tile = i + tile_base
    slot = i % 2
    loc, tm = loc_sc.shape[1], x1_ref.shape[0]

    def piece_copy(sl, lrow, grow, e, n):
        src = pl.multiple_of(seg_ref[e] + grow, RUN_ROWS)
        return pltpu.make_async_copy(ys_ref.at[pl.ds(src, n)], loc_sc.at[sl, pl.ds(lrow, n)], sem.at[sl])

    def gather(tl, sl):
        _for_each_run_piece(tab_ref, tl, lambda lrow, grow, e, n: piece_copy(sl, lrow, grow, e, n).start())

    @pl.when(i == 0)
    def _():
        loc_sc[...] = jnp.zeros_like(loc_sc)
        gather(tile, slot)

    @pl.when(i < last)
    def _():
        gather(tile + 1, 1 - slot)

    _drain_units(tot_ref[tile], lambda n: piece_copy(slot, 0, 0, 0, n), loc)

    cols = cols_ref[...]
    lane = lax.broadcasted_iota(jnp.int32, (tm, loc), 1)
    weights = jnp.zeros((tm, loc), F32)
    for k in range(TOP_K):
        weights = jnp.where(lane == cols[:, k:k + 1].astype(jnp.int32), cols[:, TOP_K + k:TOP_K + k + 1], weights)
    ffn = jnp.dot(weights.astype(BF16), loc_sc[slot].astype(BF16), preferred_element_type=F32)
    y_ref[...] = _layer_norm(DEEPNORM_ALPHA * x1_ref[...] + ffn, g2_ref[...], b2_ref[...])


def _combine(tab, seg_start, tot, cols, x1, g2, b2, ys, *, tile, tile_base):
    n = x1.shape[0]
    loc = tile * TOP_K + N_EXPERTS * RUN_ROWS
    return pl.pallas_call(
        functools.partial(_combine_kernel, tile_base=tile_base),
        grid_spec=pltpu.PrefetchScalarGridSpec(
            num_scalar_prefetch=3,
            grid=(n // tile,),
            in_specs=[pl.BlockSpec((tile, 2 * TOP_K), lambda i, *_: (i, 0)),
                      pl.BlockSpec((tile, D_MODEL), lambda i, *_: (i, 0)),
                      pl.BlockSpec((1, D_MODEL), lambda i, *_: (0, 0)),
                      pl.BlockSpec((1, D_MODEL), lambda i, *_: (0, 0)),
                      pl.BlockSpec(memory_space=pl.ANY)],
            out_specs=pl.BlockSpec((tile, D_MODEL), lambda i, *_: (i, 0)),
            scratch_shapes=[pltpu.VMEM((2, loc, D_MODEL), F32), pltpu.SemaphoreType.DMA((2,))]),
        out_shape=jax.ShapeDtypeStruct((n, D_MODEL), F32),
        compiler_params=_cparams(("arbitrary",), VMEM_LIMIT),
        name="combine",
    )(tab, seg_start, tot, cols, x1, g2, b2, ys)


def kernel(x_prompt, x_sample, cache_k_win, cache_v_win, state_ssm_re, state_ssm_im, w_in, b_in, attn_sinks,
           w_attn_out, ssm_a_re, ssm_a_im, ssm_log_dt, ssm_b_re, ssm_b_im, ssm_c_re, ssm_c_im, ssm_d, w_ssm_out,
           w_gate, b_gate, w_out, ln1_g, ln1_b, w_router, b_router, w_exp1, b_exp1, w_exp2, b_exp2, ln2_g, ln2_b):
    assert w_in.shape[0] == DEPTH == 1
    bsz, seq, _ = x_prompt.shape
    nsamp = x_sample.shape[0]
    assert x_sample.shape[1] == 1
    n_p = bsz * seq
    n_tok = n_p + nsamp

    xp = x_prompt.reshape(n_p, D_MODEL)
    xsm = x_sample.reshape(nsamp, D_MODEL)
    b_in2 = b_in[0].reshape(1, D_IN)
    sinks = attn_sinks[0].astype(F32)

    q_p, k_p, v_p, u_p = _proj(xp, w_in[0].astype(BF16), b_in2, tile=512, exact_f32=False, q_dtype=BF16)
    q_s, k_s, v_s, u_s = _proj(xsm, w_in[0], b_in2, tile=nsamp, exact_f32=True, q_dtype=F32)

    o_p = _attn_prompt(sinks, q_p.reshape(bsz, seq, D_ATTN), k_p.reshape(bsz, seq, D_KV),
                       v_p.reshape(bsz, seq, D_KV)).reshape(n_p, D_ATTN)
    k_buf = cache_k_win[0].reshape(nsamp, WINDOW, D_KV)
    v_buf = cache_v_win[0].reshape(nsamp, WINDOW, D_KV)
    o_s = _attn_sample(sinks, q_s, k_s, v_s, k_buf, v_buf)

    sp = _s5_params(ssm_a_re[0], ssm_a_im[0], ssm_log_dt[0], ssm_b_re[0], ssm_b_im[0], ssm_c_re[0], ssm_c_im[0])
    y_p, hp_re, hp_im = _s5_prompt(u_p, bsz, seq, _s5_chunk_mats(sp, ssm_d[0]))
    y_s, hs_re, hs_im = _s5_sample(u_s, state_ssm_re[0].reshape(nsamp, -1), state_ssm_im[0].reshape(nsamp, -1),
                                   _s5_sample_mats(sp, ssm_d[0]))

    wm = dict(wao=w_attn_out[0].astype(BF16), wso=w_ssm_out[0].astype(BF16), wg=w_gate[0].astype(BF16),
              bg=b_gate[0].reshape(1, -1), wo=w_out[0].astype(BF16), g1=ln1_g[0].reshape(1, -1),
              b1=ln1_b[0].reshape(1, -1), wrt=w_router[0].T, brt=b_router[0].reshape(-1, 1))
    carry0 = jnp.zeros((SUBLANES, LANES), F32)
    x1_p, lpos_p, cols_p, tab_p, carry1 = _merge(xp, o_p, y_p, carry0, wm, tile=TOK_TILE)
    x1_s, lpos_s, cols_s, tab_s, carry2 = _merge(xsm, o_s, y_s, carry1, wm, tile=nsamp)

    nt_p = n_p // TOK_TILE
    tab = jnp.concatenate([tab_p[:, :TAB_ROWS, :N_EXPERTS], tab_s[:, :TAB_ROWS, :N_EXPERTS]], axis=0)
    tot = jnp.sum(tab[:, 0, :], axis=1).astype(jnp.int32)
    tab = tab.reshape(-1)
    seg_rows = carry2[0, :N_EXPERTS].astype(jnp.int32) * RUN_ROWS
    padded = ((seg_rows + MOE_ROWS - 1) // MOE_ROWS) * MOE_ROWS
    pad_end = jnp.cumsum(padded)
    pad_start = (pad_end - padded).astype(jnp.int32)
    tails = jnp.concatenate([(pad_start + seg_rows) // RUN_ROWS, (padded - seg_rows) // RUN_ROWS]).astype(jnp.int32)
    n_runs = (nt_p + 1) * N_EXPERTS
    nb_max = (n_tok * TOP_K + n_runs * (RUN_ROWS - 1) + N_EXPERTS * (MOE_ROWS - 1) + MOE_ROWS - 1) // MOE_ROWS
    n_used = (pad_end[-1] // MOE_ROWS).astype(jnp.int32)
    tails = jnp.concatenate([tails, n_used.reshape(1)])
    blk_start = jnp.arange(nb_max, dtype=jnp.int32) * MOE_ROWS
    blk_e = jnp.minimum(jnp.sum(blk_start[:, None] >= pad_end[None, :], axis=1), N_EXPERTS - 1).astype(jnp.int32)
    blk_e = jnp.where(jnp.arange(nb_max) < n_used, blk_e, blk_e[jnp.maximum(n_used - 1, 0)])
    new_run = jnp.concatenate([jnp.ones((1,), jnp.int32), (blk_e[1:] != blk_e[:-1]).astype(jnp.int32)])
    run_ord = (jnp.cumsum(new_run) - 1).astype(jnp.int32)
    ids = jnp.arange(N_EXPERTS, dtype=jnp.int32)
    later = (ids[None, :] > ids[:, None]) & (padded > 0)[None, :]
    next_e = jnp.min(jnp.where(later, ids[None, :], N_EXPERTS), axis=1)
    next_e = jnp.where(next_e < N_EXPERTS, next_e, -1).astype(jnp.int32)
    run_next = next_e[blk_e]

    nrows = nb_max * MOE_ROWS
    xs = _dispatch(tab, pad_start, tot, tails, lpos_p, x1_p, lpos_s, x1_s, tile=TOK_TILE, nrows=nrows)

    b1p = b_exp1[0].reshape(N_EXPERTS, 2 * D_FF // MXU_DIM, MXU_DIM // 2, 2)
    b1p = jnp.swapaxes(b1p, 2, 3).reshape(N_EXPERTS, 1, 2 * D_FF)
    ys = _experts(blk_e, n_used.reshape(1), run_ord, run_next, xs, w_exp1[0], b1p, w_exp2[0],
                  b_exp2[0].reshape(N_EXPERTS, 1, D_MODEL),
                  jnp.asarray(_deinterleave_matrix(), BF16))

    g2, b2 = ln2_g[0].reshape(1, -1), ln2_b[0].reshape(1, -1)
    y_prompt = _combine(tab, pad_start, tot, cols_p, x1_p, g2, b2, ys, tile=TOK_TILE, tile_base=0)
    y_sample = _combine(tab, pad_start, tot, cols_s, x1_s, g2, b2, ys, tile=nsamp, tile_base=nt_p)

    k_p4 = k_p.reshape(bsz, seq, N_KV_HEADS, HEAD_DIM)[:, -WINDOW:]
    v_p4 = v_p.reshape(bsz, seq, N_KV_HEADS, HEAD_DIM)[:, -WINDOW:]
    k_s4 = jnp.concatenate([cache_k_win[0][:, 1:], k_s.reshape(nsamp, 1, N_KV_HEADS, HEAD_DIM)], axis=1)
    v_s4 = jnp.concatenate([cache_v_win[0][:, 1:], v_s.reshape(nsamp, 1, N_KV_HEADS, HEAD_DIM)], axis=1)
    st = lambda a, n: a.reshape(1, n, N_SSM_GROUPS, SSM_STATE)
    return (y_prompt.reshape(bsz, seq, D_MODEL), y_sample.reshape(nsamp, 1, D_MODEL),
            k_p4[None], v_p4[None], st(hp_re, bsz), st(hp_im, bsz),
            k_s4[None], v_s4[None], st(hs_re, nsamp), st(hs_im, nsamp))
```

```python
import functools
import math

import numpy as np
import jax
import jax.numpy as jnp
from jax import lax
from jax.experimental import pallas as pl
from jax.experimental.pallas import tpu as pltpu

F32 = jnp.float32
BF16 = jnp.bfloat16

D_MODEL = 1024
HEAD_DIM = 64
N_Q_HEADS = 8
N_KV_HEADS = 2
Q_PER_KV = N_Q_HEADS // N_KV_HEADS
D_ATTN = N_Q_HEADS * HEAD_DIM
D_KV = N_KV_HEADS * HEAD_DIM
WINDOW = 128
ATTN_SCALE = HEAD_DIM ** -0.5
SSM_GROUP = 16
D_SSM = D_MODEL // 2
N_SSM_GROUPS = D_SSM // SSM_GROUP
SSM_STATE = 64
D_IN = D_ATTN + 2 * D_KV + D_SSM
N_EXPERTS = 32
TOP_K = 4
D_FF = D_MODEL
SWIGLU_LIMIT = 7.0
SWIGLU_ALPHA = 1.702
LN_EPS = 1e-5
DEPTH = 1
DEEPNORM_ALPHA = (2 * DEPTH) ** 0.25

LANES = 128
SUBLANES = 8
MXU_DIM = 256

S5_CHUNK = MXU_DIM // SSM_GROUP
S5_LANE_GROUPS = LANES // SSM_GROUP
MOE_ROWS = 256
MOE_STEP_BLOCKS = 2
TOK_TILE = 256
MERGE_TILE = 512
VMEM_LIMIT = 48 * 1024 * 1024
VMEM_LIMIT_MERGE = 56 * 1024 * 1024


def _cparams(sem, vmem=None):
    return pltpu.CompilerParams(dimension_semantics=sem, vmem_limit_bytes=vmem)


def _proj_kernel(x_ref, w_ref, b_ref, q_ref, k_ref, v_ref, u_ref, *, exact_f32):
    if exact_f32:
        h = jnp.dot(x_ref[...], w_ref[...], preferred_element_type=F32, precision=lax.Precision.HIGHEST)
    else:
        h = jnp.dot(x_ref[...].astype(BF16), w_ref[...], preferred_element_type=F32)
    h = h + b_ref[...]
    q_ref[...] = (h[:, :D_ATTN] * ATTN_SCALE).astype(q_ref.dtype)
    k_ref[...] = h[:, D_ATTN:D_ATTN + D_KV]
    v_ref[...] = h[:, D_ATTN + D_KV:D_ATTN + 2 * D_KV]
    u_ref[...] = h[:, D_ATTN + 2 * D_KV:].astype(u_ref.dtype)


def _proj(x, w, b, *, tile, exact_f32, q_dtype):
    n = x.shape[0]
    return pl.pallas_call(
        functools.partial(_proj_kernel, exact_f32=exact_f32),
        grid=(n // tile,),
        in_specs=[pl.BlockSpec((tile, D_MODEL), lambda i: (i, 0)),
                  pl.BlockSpec((D_MODEL, D_IN), lambda i: (0, 0)),
                  pl.BlockSpec((1, D_IN), lambda i: (0, 0))],
        out_specs=[pl.BlockSpec((tile, D_ATTN), lambda i: (i, 0)),
                   pl.BlockSpec((tile, D_KV), lambda i: (i, 0)),
                   pl.BlockSpec((tile, D_KV), lambda i: (i, 0)),
                   pl.BlockSpec((tile, D_SSM), lambda i: (i, 0))],
        out_shape=[jax.ShapeDtypeStruct((n, D_ATTN), q_dtype),
                   jax.ShapeDtypeStruct((n, D_KV), F32),
                   jax.ShapeDtypeStruct((n, D_KV), F32),
                   jax.ShapeDtypeStruct((n, D_SSM), F32)],
        compiler_params=_cparams(("parallel",)),
        name="proj",
    )(x, w, b)


ATT_Q_TILE = 512


def _attn_prompt_kernel(sink_ref, q_ref, k_ref, v_ref, o_ref):
    i = pl.program_id(1)
    for blk in range(ATT_Q_TILE // WINDOW):
        q0 = i * ATT_Q_TILE + blk * WINDOW
        k0 = pl.multiple_of(jnp.maximum(q0 - WINDOW, 0), WINDOW)
        kk = k_ref[0, pl.ds(k0, 2 * WINDOW), :].astype(BF16)
        vv = v_ref[0, pl.ds(k0, 2 * WINDOW), :].astype(BF16)
        qb = q_ref[0, blk * WINDOW:(blk + 1) * WINDOW, :]
        qpos = q0 + lax.broadcasted_iota(jnp.int32, (WINDOW, 2 * WINDOW), 0)
        kpos = k0 + lax.broadcasted_iota(jnp.int32, (WINDOW, 2 * WINDOW), 1)
        valid = (kpos <= qpos) & (qpos - kpos <= WINDOW)
        for h in range(N_Q_HEADS):
            kv = h // Q_PER_KV
            qh = qb[:, h * HEAD_DIM:(h + 1) * HEAD_DIM]
            kh = kk[:, kv * HEAD_DIM:(kv + 1) * HEAD_DIM]
            vh = vv[:, kv * HEAD_DIM:(kv + 1) * HEAD_DIM]
            s = lax.dot_general(qh, kh, (((1,), (1,)), ((), ())), preferred_element_type=F32)
            s = jnp.where(valid, s, -jnp.inf)
            sink = sink_ref[h]
            m = jnp.maximum(jnp.max(s, axis=-1, keepdims=True), sink)
            p = jnp.exp(s - m)
            denom = jnp.sum(p, axis=-1, keepdims=True) + jnp.exp(sink - m)
            o = jnp.dot(p.astype(BF16), vh, preferred_element_type=F32) / denom
            o_ref[0, blk * WINDOW:(blk + 1) * WINDOW, h * HEAD_DIM:(h + 1) * HEAD_DIM] = o.astype(o_ref.dtype)


def _attn_prompt(sinks, q, k, v):
    bsz, seq = q.shape[0], q.shape[1]
    return pl.pallas_call(
        _attn_prompt_kernel,
        grid=(bsz, seq // ATT_Q_TILE),
        in_specs=[pl.BlockSpec(memory_space=pltpu.SMEM),
                  pl.BlockSpec((1, ATT_Q_TILE, D_ATTN), lambda b, i: (b, i, 0)),
                  pl.BlockSpec((1, seq, D_KV), lambda b, i: (b, 0, 0)),
                  pl.BlockSpec((1, seq, D_KV), lambda b, i: (b, 0, 0))],
        out_specs=pl.BlockSpec((1, ATT_Q_TILE, D_ATTN), lambda b, i: (b, i, 0)),
        out_shape=jax.ShapeDtypeStruct((bsz, seq, D_ATTN), BF16),
        compiler_params=_cparams(("parallel", "parallel")),
        name="attn_prompt",
    )(sinks, q, k, v)


ATT_S_GROUP = 8


def _attn_sample_kernel(sink_ref, q_ref, kn_ref, vn_ref, kb_ref, vb_ref, o_ref):
    g = ATT_S_GROUP
    rows = Q_PER_KV * g
    ncol = g * WINDOW
    kb = kb_ref[...].reshape(ncol, D_KV).astype(BF16)
    vb = vb_ref[...].reshape(ncol, D_KV).astype(BF16)
    rseq = lax.broadcasted_iota(jnp.int32, (rows, ncol), 0) % g
    cseq = lax.broadcasted_iota(jnp.int32, (rows, ncol), 1) // WINDOW
    own = rseq == cseq
    rhead = lax.broadcasted_iota(jnp.int32, (rows, 1), 0) // g
    for kv in range(N_KV_HEADS):
        lo = kv * HEAD_DIM
        qs = jnp.concatenate(
            [q_ref[:, (kv * Q_PER_KV + h) * HEAD_DIM:(kv * Q_PER_KV + h + 1) * HEAD_DIM] for h in range(Q_PER_KV)],
            axis=0)
        kn = jnp.concatenate([kn_ref[:, lo:lo + HEAD_DIM]] * Q_PER_KV, axis=0)
        vn = jnp.concatenate([vn_ref[:, lo:lo + HEAD_DIM]] * Q_PER_KV, axis=0)
        sink = jnp.zeros((rows, 1), F32)
        for h in range(Q_PER_KV):
            sink = jnp.where(rhead == h, sink_ref[kv * Q_PER_KV + h], sink)
        qs = qs.astype(BF16)
        s = lax.dot_general(qs, kb[:, lo:lo + HEAD_DIM], (((1,), (1,)), ((), ())), preferred_element_type=F32)
        s = jnp.where(own, s, -jnp.inf)
        s_new = jnp.sum(qs.astype(F32) * kn.astype(BF16).astype(F32), axis=-1, keepdims=True)
        m = jnp.maximum(jnp.maximum(jnp.max(s, axis=-1, keepdims=True), s_new), sink)
        p = jnp.exp(s - m)
        p_new = jnp.exp(s_new - m)
        denom = jnp.sum(p, axis=-1, keepdims=True) + p_new + jnp.exp(sink - m)
        o = jnp.dot(p.astype(BF16), vb[:, lo:lo + HEAD_DIM], preferred_element_type=F32)
        o = (o + p_new.astype(BF16).astype(F32) * vn.astype(BF16).astype(F32)) / denom
        for h in range(Q_PER_KV):
            c0 = (kv * Q_PER_KV + h) * HEAD_DIM
            o_ref[:, c0:c0 + HEAD_DIM] = o[h * g:(h + 1) * g].astype(o_ref.dtype)


def _attn_sample(sinks, q, k_new, v_new, k_buf, v_buf):
    n = q.shape[0]
    g = ATT_S_GROUP
    return pl.pallas_call(
        _attn_sample_kernel,
        grid=(n // g,),
        in_specs=[pl.BlockSpec(memory_space=pltpu.SMEM),
                  pl.BlockSpec((g, D_ATTN), lambda i: (i, 0)),
                  pl.BlockSpec((g, D_KV), lambda i: (i, 0)),
                  pl.BlockSpec((g, D_KV), lambda i: (i, 0)),
                  pl.BlockSpec((g, WINDOW, D_KV), lambda i: (i, 0, 0)),
                  pl.BlockSpec((g, WINDOW, D_KV), lambda i: (i, 0, 0))],
        out_specs=pl.BlockSpec((g, D_ATTN), lambda i: (i, 0)),
        out_shape=jax.ShapeDtypeStruct((n, D_ATTN), F32),
        compiler_params=_cparams(("parallel",)),
        name="attn_sample",
    )(sinks, q, k_new, v_new, k_buf, v_buf)


def _s5_params(a_re, a_im, log_dt, b_re, b_im, c_re, c_im):
    hp = lax.Precision.HIGHEST
    dt = jnp.exp(log_dt.astype(F32))[:, None]
    are, aim = a_re.astype(F32), a_im.astype(F32)
    tau = jnp.arange(S5_CHUNK + 1, dtype=F32)[None, :, None]
    mag = jnp.exp(tau * (dt * are)[:, None, :])
    ang = tau * (dt * aim)[:, None, :]
    pw_re, pw_im = mag * jnp.cos(ang), mag * jnp.sin(ang)
    ab_re, ab_im = pw_re[:, 1], pw_im[:, 1]
    den = are * are + aim * aim
    f_re = ((ab_re - 1.0) * are + ab_im * aim) / den
    f_im = (ab_im * are - (ab_re - 1.0) * aim) / den
    bre, bim = b_re.astype(F32), b_im.astype(F32)
    bb_re = f_re[..., None] * bre - f_im[..., None] * bim
    bb_im = f_re[..., None] * bim + f_im[..., None] * bre
    cre, cim = c_re.astype(F32), c_im.astype(F32)
    return dict(pw_re=pw_re, pw_im=pw_im, ab_re=ab_re, ab_im=ab_im, bb_re=bb_re, bb_im=bb_im,
                c_re=cre, c_im=cim, hp=hp)


def _s5_chunk_mats(sp, d_skip):
    hp = sp["hp"]
    g, t, c, p = N_SSM_GROUPS, S5_CHUNK, SSM_GROUP, SSM_STATE
    pw_re, pw_im = sp["pw_re"], sp["pw_im"]
    ca_re = sp["c_re"][:, None] * pw_re[:, :, None, :] - sp["c_im"][:, None] * pw_im[:, :, None, :]
    ca_im = sp["c_re"][:, None] * pw_im[:, :, None, :] + sp["c_im"][:, None] * pw_re[:, :, None, :]
    kern = (jnp.einsum("gtcp,gpd->gtcd", ca_re[:, :t], sp["bb_re"], precision=hp)
            - jnp.einsum("gtcp,gpd->gtcd", ca_im[:, :t], sp["bb_im"], precision=hp))
    kc = jnp.swapaxes(kern, 2, 3)
    kc = kc.at[:, 0].add(d_skip.astype(F32).reshape(g, 1, c) * jnp.eye(c, dtype=F32)[None])
    rev_re, rev_im = pw_re[:, t - 1::-1][:, :t], pw_im[:, t - 1::-1][:, :t]
    wst_re = rev_re[:, :, None, :] * jnp.swapaxes(sp["bb_re"], 1, 2)[:, None] \
        - rev_im[:, :, None, :] * jnp.swapaxes(sp["bb_im"], 1, 2)[:, None]
    wst_im = rev_re[:, :, None, :] * jnp.swapaxes(sp["bb_im"], 1, 2)[:, None] \
        + rev_im[:, :, None, :] * jnp.swapaxes(sp["bb_re"], 1, 2)[:, None]
    wo_re = jnp.transpose(ca_re[:, 1:t + 1], (0, 3, 1, 2))
    wo_im = -jnp.transpose(ca_im[:, 1:t + 1], (0, 3, 1, 2))
    nv, gl = g // S5_LANE_GROUPS, S5_LANE_GROUPS
    kc, wst_re, wst_im, wo_re, wo_im = lax.optimization_barrier((kc, wst_re, wst_im, wo_re, wo_im))
    kc5 =jnp.transpose(kc.reshape(nv, gl, t, c, c), (0, 2, 1, 3, 4))
    ws6 = jnp.transpose(jnp.stack([wst_re, wst_im], axis=3).reshape(nv, gl, t, c, 2, p),
                        (0, 2, 1, 3, 4, 5))
    wo6 = jnp.transpose(jnp.stack([wo_re, wo_im], axis=0).reshape(2, nv, gl, p, t, c),
                        (1, 0, 2, 3, 4, 5))
    kc5, ws6, wo6 = lax.optimization_barrier((kc5.astype(BF16), ws6.astype(BF16), wo6.astype(BF16)))
    gid = jnp.arange(gl)
    zero = jnp.zeros((), BF16)
    same5 = (gid[:, None, None, None] == gid[None, None, :, None])
    bd = jnp.where(same5[None, None], kc5[:, :, :, :, None, :], zero).reshape(nv, t, LANES, LANES)
    spread_s = np.zeros((2 * p, 2 * gl * p), np.float32)
    spread_o = np.zeros((t * c, t * LANES), np.float32)
    for h in range(gl):
        for ri in range(2):
            spread_s[ri * p + np.arange(p), ri * gl * p + h * p + np.arange(p)] = 1.0
        for tt in range(t):
            spread_o[tt * c + np.arange(c), tt * LANES + h * c + np.arange(c)] = 1.0
    row_g = (jnp.arange(t * LANES) // c) % gl
    blk_g = (jnp.arange(2 * gl * p) // p) % gl
    wst_v = jnp.einsum("vrk,kc->vrc", ws6.reshape(nv, t * LANES, 2 * p), jnp.asarray(spread_s, BF16),
                       preferred_element_type=F32)
    wst_v = jnp.where((row_g[:, None] == blk_g[None, :])[None], wst_v, 0.0).astype(BF16)
    wout_v = jnp.einsum("vrk,kc->vrc", wo6.reshape(nv, 2 * gl * p, t * c), jnp.asarray(spread_o, BF16),
                        preferred_element_type=F32)
    wout_v = jnp.where((blk_g[:, None] == row_g[None, :])[None], wout_v, 0.0).astype(BF16)
    at_re = pw_re[:, t].reshape(1, g * p)
    at_im = pw_im[:, t].reshape(1, g * p)
    return bd, wst_v, wout_v, at_re, at_im


def _s5_chunk_rows(u_ref, nchunk):
    return jnp.concatenate(
        [u_ref[pl.ds(s, nchunk, stride=S5_CHUNK), :] for s in range(S5_CHUNK)], axis=1).astype(BF16)


S5_SLABS = S5_LANE_GROUPS * SSM_STATE // LANES


def _s5_state_kernel(u_ref, wst_ref, sre_ref, sim_ref):
    nchunk = sre_ref.shape[1]
    s = jnp.dot(_s5_chunk_rows(u_ref, nchunk), wst_ref[0], preferred_element_type=F32)
    for k in range(S5_SLABS):
        sre_ref[k] = s[:, k * LANES:(k + 1) * LANES]
        sim_ref[k] = s[:, (S5_SLABS + k) * LANES:(S5_SLABS + k + 1) * LANES]


def _s5_scan_kernel(sre_ref, sim_ref, are_ref, aim_ref, hre_ref, him_ref, fre_ref, fim_ref, *, bsz):
    nchunk = sre_ref.shape[1] // bsz
    are = [jnp.broadcast_to(are_ref[:, k * LANES:(k + 1) * LANES], (bsz, LANES)) for k in range(S5_SLABS)]
    aim = [jnp.broadcast_to(aim_ref[:, k * LANES:(k + 1) * LANES], (bsz, LANES)) for k in range(S5_SLABS)]

    def body(j, carry):
        rows = pl.ds(j, bsz, stride=nchunk)
        out = []
        for k in range(S5_SLABS):
            cre, cim = carry[2 * k], carry[2 * k + 1]
            hre_ref[k, rows, :] = cre
            him_ref[k, rows, :] = cim
            sr = sre_ref[k, rows, :]
            si = sim_ref[k, rows, :]
            out += [are[k] * cre - aim[k] * cim + sr, are[k] * cim + aim[k] * cre + si]
        return tuple(out)

    zero = jnp.zeros((bsz, LANES), F32)
    fin = lax.fori_loop(0, nchunk, body, (zero,) * (2 * S5_SLABS))
    fre_ref[...] = jnp.concatenate(fin[0::2], axis=1)
    fim_ref[...] = jnp.concatenate(fin[1::2], axis=1)


def _s5_out_kernel(u_ref, bd_ref, hre_ref, him_ref, wout_ref, y_ref, m_sc):
    nchunk = hre_ref.shape[1]

    @pl.when(pl.program_id(1) == 0)
    def _():
        for s in range(S5_CHUNK):
            for t in range(S5_CHUNK):
                blk = bd_ref[0, t - s] if t >= s else jnp.zeros((LANES, LANES), BF16)
                m_sc[s * LANES:(s + 1) * LANES, t * LANES:(t + 1) * LANES] = blk

    hcat = jnp.concatenate([hre_ref[k] for k in range(S5_SLABS)] + [him_ref[k] for k in range(S5_SLABS)],
                           axis=1).astype(BF16)
    y = jnp.dot(_s5_chunk_rows(u_ref, nchunk), m_sc[...], preferred_element_type=F32)
    y = y + jnp.dot(hcat, wout_ref[0], preferred_element_type=F32)
    for s in range(S5_CHUNK):
        y_ref[pl.ds(s, nchunk, stride=S5_CHUNK), :] = y[:, s * LANES:(s + 1) * LANES]


def _s5_prompt(u, bsz, seq, mats):
    bd, wst_v, wout_v, at_re, at_im = mats
    g, t, p = N_SSM_GROUPS, S5_CHUNK, SSM_STATE
    nchunk = seq // t
    n = nchunk * bsz
    nv = g // S5_LANE_GROUPS
    half = S5_LANE_GROUPS * p
    s_re, s_im = pl.pallas_call(
        _s5_state_kernel,
        grid=(nv, bsz),
        in_specs=[pl.BlockSpec((seq, LANES), lambda v, b: (b, v)),
                  pl.BlockSpec((1, t * LANES, 2 * half), lambda v, b: (v, 0, 0))],
        out_specs=[pl.BlockSpec((S5_SLABS, nchunk, LANES), lambda v, b: (v, b, 0)),
                   pl.BlockSpec((S5_SLABS, nchunk, LANES), lambda v, b: (v, b, 0))],
        out_shape=[jax.ShapeDtypeStruct((nv * S5_SLABS, n, LANES), F32)] * 2,
        compiler_params=_cparams(("parallel", "parallel"), VMEM_LIMIT),
        name="s5_state",
    )(u, wst_v)
    h_re, h_im, f_re, f_im = pl.pallas_call(
        functools.partial(_s5_scan_kernel, bsz=bsz),
        grid=(nv,),
        in_specs=[pl.BlockSpec((S5_SLABS, n, LANES), lambda i: (i, 0, 0)),
                  pl.BlockSpec((S5_SLABS, n, LANES), lambda i: (i, 0, 0)),
                  pl.BlockSpec((1, half), lambda i: (0, i)),
                  pl.BlockSpec((1, half), lambda i: (0, i))],
        out_specs=[pl.BlockSpec((S5_SLABS, n, LANES), lambda i: (i, 0, 0)),
                   pl.BlockSpec((S5_SLABS, n, LANES), lambda i: (i, 0, 0)),
                   pl.BlockSpec((bsz, half), lambda i: (0, i)),
                   pl.BlockSpec((bsz, half), lambda i: (0, i))],
        out_shape=[jax.ShapeDtypeStruct((nv * S5_SLABS, n, LANES), F32)] * 2
        + [jax.ShapeDtypeStruct((bsz, g * p), F32)] * 2,
        compiler_params=_cparams(("parallel",)),
        name="s5_scan",
    )(s_re, s_im, at_re, at_im)
    y = pl.pallas_call(
        _s5_out_kernel,
        grid=(nv, bsz),
        in_specs=[pl.BlockSpec((seq, LANES), lambda v, b: (b, v)),
                  pl.BlockSpec((1, t, LANES, LANES), lambda v, b: (v, 0, 0, 0)),
                  pl.BlockSpec((S5_SLABS, nchunk, LANES), lambda v, b: (v, b, 0)),
                  pl.BlockSpec((S5_SLABS, nchunk, LANES), lambda v, b: (v, b, 0)),
                  pl.BlockSpec((1, 2 * half, t * LANES), lambda v, b: (v, 0, 0))],
        out_specs=pl.BlockSpec((seq, LANES), lambda v, b: (b, v)),
        out_shape=jax.ShapeDtypeStruct((bsz * seq, D_SSM), F32),
        scratch_shapes=[pltpu.VMEM((t * LANES, t * LANES), BF16)],
        compiler_params=_cparams(("parallel", "arbitrary"), VMEM_LIMIT),
        name="s5_out",
    )(u, bd, h_re, h_im, wout_v)
    return y, f_re, f_im


S5S_GROUPS = LANES // SSM_GROUP


def _s5_sample_mats(sp, d_skip):
    go, gl, c, p = N_SSM_GROUPS // S5S_GROUPS, S5S_GROUPS, SSM_GROUP, SSM_STATE
    eye = jnp.eye(gl, dtype=F32)

    def bdiag_in(b):
        b4 = b.reshape(go, gl, p, c)
        return jnp.einsum("ogpc,gh->ogchp", b4, eye).reshape(go, gl * c, gl * p)

    def bdiag_out(cm):
        c4 = cm.reshape(go, gl, c, p)
        return jnp.einsum("ogcp,gh->ogphc", c4, eye).reshape(go, gl * p, gl * c)

    b8 = jnp.concatenate([bdiag_in(sp["bb_re"]), bdiag_in(sp["bb_im"])], axis=2)
    c8 = jnp.concatenate([bdiag_out(sp["c_re"]), -bdiag_out(sp["c_im"])], axis=1)
    a_re = sp["ab_re"].reshape(1, N_SSM_GROUPS * p)
    a_im = sp["ab_im"].reshape(1, N_SSM_GROUPS * p)
    return b8, c8, a_re, a_im, d_skip.astype(F32).reshape(1, D_SSM)


def _s5_sample_kernel(u_ref, hre_ref, him_ref, b8_ref, c8_ref, are_ref, aim_ref, d_ref,
                      y_ref, ore_ref, oim_ref):
    hp = lax.Precision.HIGHEST
    u = u_ref[...]
    half = S5S_GROUPS * SSM_STATE
    bu = jnp.dot(u, b8_ref[0], preferred_element_type=F32, precision=hp)
    are, aim = are_ref[...], aim_ref[...]
    h0r, h0i = hre_ref[...], him_ref[...]
    hr = are * h0r - aim * h0i + bu[:, :half]
    hi = are * h0i + aim * h0r + bu[:, half:]
    ore_ref[...] = hr
    oim_ref[...] = hi
    y = jnp.dot(jnp.concatenate([hr, hi], axis=1), c8_ref[0], preferred_element_type=F32, precision=hp)
    y_ref[...] = (y + d_ref[...] * u).astype(y_ref.dtype)


def _s5_sample(u, h0_re, h0_im, mats):
    b8, c8, a_re, a_im, d = mats
    n = u.shape[0]
    half = S5S_GROUPS * SSM_STATE
    return pl.pallas_call(
        _s5_sample_kernel,
        grid=(N_SSM_GROUPS // S5S_GROUPS,),
        in_specs=[pl.BlockSpec((n, LANES), lambda i: (0, i)),
                  pl.BlockSpec((n, half), lambda i: (0, i)),
                  pl.BlockSpec((n, half), lambda i: (0, i)),
                  pl.BlockSpec((1, LANES, 2 * half), lambda i: (i, 0, 0)),
                  pl.BlockSpec((1, 2 * half, LANES), lambda i: (i, 0, 0)),
                  pl.BlockSpec((1, half), lambda i: (0, i)),
                  pl.BlockSpec((1, half), lambda i: (0, i)),
                  pl.BlockSpec((1, LANES), lambda i: (0, i))],
        out_specs=[pl.BlockSpec((n, LANES), lambda i: (0, i)),
                   pl.BlockSpec((n, half), lambda i: (0, i)),
                   pl.BlockSpec((n, half), lambda i: (0, i))],
        out_shape=[jax.ShapeDtypeStruct((n, D_SSM), BF16),
                   jax.ShapeDtypeStruct((n, N_SSM_GROUPS * SSM_STATE), F32),
                   jax.ShapeDtypeStruct((n, N_SSM_GROUPS * SSM_STATE), F32)],
        compiler_params=_cparams(("parallel",)),
        name="s5_sample",
    )(u, h0_re, h0_im, b8, c8, a_re, a_im, d)


def _layer_norm(x, g, b):
    mu = jnp.mean(x, axis=-1, keepdims=True)
    xc = x - mu
    var = jnp.mean(xc * xc, axis=-1, keepdims=True)
    return xc * lax.rsqrt(var + LN_EPS) * g + b


RUN_ROWS = SUBLANES
TAB_ROWS = 3


def _merge_kernel(x_ref, oa_ref, ys_ref, carry_in_ref, wao_ref, wso_ref, wg_ref, bg_ref, wo_ref,
                  g1_ref, b1_ref, wrt_ref, brt_ref,
                  x1_ref, lpos_ref, cols_ref, tab_ref, carry_out_ref, carry_sc):
    step = pl.program_id(0)

    @pl.when(step == 0)
    def _():
        carry_sc[...] = carry_in_ref[...]

    tm = x_ref.shape[0]
    x = x_ref[...]
    branch_a = jnp.dot(oa_ref[...].astype(BF16), wao_ref[...], preferred_element_type=F32)
    z = jnp.dot(jax.nn.gelu(ys_ref[...].astype(F32)).astype(BF16), wso_ref[...], preferred_element_type=F32)
    branch_b = z[:, :D_MODEL] * jax.nn.sigmoid(z[:, D_MODEL:])
    gates = jax.nn.sigmoid(jnp.dot(x.astype(BF16), wg_ref[...], preferred_element_type=F32) + bg_ref[...])
    mixed = gates[:, :D_MODEL] * branch_a + gates[:, D_MODEL:] * branch_b
    mix = jnp.dot(mixed.astype(BF16), wo_ref[...], preferred_element_type=F32)
    x1 = _layer_norm(DEEPNORM_ALPHA * x + mix, g1_ref[...], b1_ref[...])
    x1_ref[...] = x1

    def split2(v):
        hi = v.astype(BF16)
        return hi, (v - hi.astype(F32)).astype(BF16)

    def dot_nt(a, b):
        return lax.dot_general(a, b, (((1,), (1,)), ((), ())), preferred_element_type=F32)

    w_hi, w_lo = split2(wrt_ref[...])
    rt = tm // tab_ref.shape[0]
    sub = lax.broadcasted_iota(jnp.int32, (N_EXPERTS, rt), 0)
    r = lax.broadcasted_iota(jnp.int32, (rt, rt), 0)
    c = lax.broadcasted_iota(jnp.int32, (rt, rt), 1)
    er = lax.broadcasted_iota(jnp.int32, (N_EXPERTS, N_EXPERTS), 0)
    ec = lax.broadcasted_iota(jnp.int32, (N_EXPERTS, N_EXPERTS), 1)
    rid = lax.broadcasted_iota(jnp.int32, (SUBLANES, LANES), 0)
    lane_pad = jnp.zeros((SUBLANES, LANES - N_EXPERTS), F32)
    for h in range(tab_ref.shape[0]):
        x_hi, x_lo = split2(x1[h * rt:(h + 1) * rt])
        logits = dot_nt(w_hi, x_hi) + dot_nt(w_hi, x_lo) + dot_nt(w_lo, x_hi) + brt_ref[...]
        work = logits
        vals, sels = [], []
        for _ in range(TOP_K):
            mx = jnp.max(work, axis=0, keepdims=True)
            idx = jnp.min(jnp.where(work == mx, sub, N_EXPERTS), axis=0, keepdims=True)
            sel = sub == idx
            vals.append(mx)
            sels.append(sel)
            work = jnp.where(sel, -jnp.inf, work)
        ex = [jnp.exp(v - vals[0]) for v in vals]
        tot = ex[0] + ex[1] + ex[2] + ex[3]
        gate_rows = jnp.concatenate([e / tot for e in ex], axis=0)

        multi = jnp.zeros((N_EXPERTS, rt), F32)
        for sel in sels:
            multi = multi + jnp.where(sel, 1.0, 0.0)
        multi_b = multi.astype(BF16)
        earlier = jnp.dot(multi_b, jnp.where(r < c, 1.0, 0.0).astype(BF16), preferred_element_type=F32)
        cnt_col = jnp.sum(multi, axis=1, keepdims=True)
        nb_col = jnp.floor((cnt_col + (RUN_ROWS - 1.0)) * (1.0 / RUN_ROWS))
        loff_col = jnp.dot(jnp.where(ec < er, 1.0, 0.0).astype(BF16),
                           jnp.broadcast_to(nb_col, (N_EXPERTS, rt)).astype(BF16), preferred_element_type=F32)
        base = RUN_ROWS * loff_col + earlier
        lpos = jnp.concatenate([jnp.sum(jnp.where(sel, base, 0.0), axis=0, keepdims=True) for sel in sels],
                               axis=0)
        lpos_ref[:, h * rt:(h + 1) * rt] = lpos.astype(jnp.int32)
        rows_hi, rows_lo = split2(jnp.concatenate([lpos, gate_rows], axis=0))
        eye = jnp.where(r == c, 1.0, 0.0).astype(BF16)
        cols_ref[h * rt:(h + 1) * rt, :] = dot_nt(eye, rows_hi) + dot_nt(eye, rows_lo)

        cnt_row = dot_nt(jnp.ones((SUBLANES, rt), BF16), multi_b)
        nb_row = jnp.floor((cnt_row + (RUN_ROWS - 1.0)) * (1.0 / RUN_ROWS))
        loff_row = jnp.dot(nb_row.astype(BF16), jnp.where(er < ec, 1.0, 0.0).astype(BF16),
                           preferred_element_type=F32)
        nb_p = jnp.concatenate([nb_row, lane_pad], axis=1)
        loff_p = jnp.concatenate([loff_row, lane_pad], axis=1)
        goff_p = carry_sc[...]
        tab = jnp.where(rid == 0, nb_p, jnp.where(rid == 1, loff_p, jnp.where(rid == 2, goff_p, 0.0)))
        tab_ref[h] = tab.astype(jnp.int32)
        carry_sc[...] = goff_p + nb_p
    carry_out_ref[...] = carry_sc[...]


def _merge(x, o_attn, y_ssm, carry_in, w, *, tile, route_tile):
    n = x.shape[0]
    nt = n // tile
    per_step = tile // route_tile
    full = lambda shape: pl.BlockSpec(shape, lambda i: (0,) * len(shape))
    return pl.pallas_call(
        _merge_kernel,
        grid=(nt,),
        in_specs=[pl.BlockSpec((tile, D_MODEL), lambda i: (i, 0)),
                  pl.BlockSpec((tile, D_ATTN), lambda i: (i, 0)),
                  pl.BlockSpec((tile, D_SSM), lambda i: (i, 0)),
                  full((SUBLANES, LANES)),
                  full((D_ATTN, D_MODEL)), full((D_SSM, 2 * D_MODEL)), full((D_MODEL, 2 * D_MODEL)),
                  full((1, 2 * D_MODEL)), full((D_MODEL, D_MODEL)),
                  full((1, D_MODEL)), full((1, D_MODEL)),
                  full((N_EXPERTS, D_MODEL)), full((N_EXPERTS, 1))],
        out_specs=[pl.BlockSpec((tile, D_MODEL), lambda i: (i, 0)),
                   pl.BlockSpec((TOP_K, tile), lambda i: (0, i)),
                   pl.BlockSpec((tile, 2 * TOP_K), lambda i: (i, 0)),
                   pl.BlockSpec((per_step, SUBLANES, LANES), lambda i: (i, 0, 0)),
                   full((SUBLANES, LANES))],
        out_shape=[jax.ShapeDtypeStruct((n, D_MODEL), F32),
                   jax.ShapeDtypeStruct((TOP_K, n), jnp.int32),
                   jax.ShapeDtypeStruct((n, 2 * TOP_K), F32),
                   jax.ShapeDtypeStruct((nt * per_step, SUBLANES, LANES), jnp.int32),
                   jax.ShapeDtypeStruct((SUBLANES, LANES), F32)],
        scratch_shapes=[pltpu.VMEM((SUBLANES, LANES), F32)],
        compiler_params=_cparams(("arbitrary",), VMEM_LIMIT_MERGE),
        name="merge",
    )(x, o_attn, y_ssm, carry_in, w["wao"], w["wso"], w["wg"], w["bg"], w["wo"], w["g1"], w["b1"],
      w["wrt"], w["brt"])


def _tab(tab_ref, tile, row, e):
    return tab_ref[(tile * TAB_ROWS + row) * N_EXPERTS + e]


BIG_PIECE = 4 * RUN_ROWS
MAX_UNITS_LOG2 = 8


def _for_each_run_piece(tab_ref, tile, fn):
    def per_expert(e, carry):
        loff = RUN_ROWS * _tab(tab_ref, tile, 1, e)
        goff = RUN_ROWS * _tab(tab_ref, tile, 2, e)
        units = _tab(tab_ref, tile, 0, e)
        n_big = lax.shift_right_logical(units, 2)

        def big(j, c2):
            fn(pl.multiple_of(loff + j * BIG_PIECE, RUN_ROWS), goff + j * BIG_PIECE, e, BIG_PIECE)
            return c2

        lax.fori_loop(0, n_big, big, 0)
        done = n_big * BIG_PIECE

        def small(j, c2):
            fn(pl.multiple_of(loff + done + j * RUN_ROWS, RUN_ROWS), goff + done + j * RUN_ROWS, e, RUN_ROWS)
            return c2

        lax.fori_loop(0, units & 3, small, 0)
        return carry

    lax.fori_loop(0, N_EXPERTS, per_expert, 0)


def _drain_units(units, wait_copy, buffer_rows):
    assert buffer_rows < (RUN_ROWS << MAX_UNITS_LOG2)
    for b in range(MAX_UNITS_LOG2):
        if (RUN_ROWS << b) > buffer_rows:
            break

        @pl.when((lax.shift_right_logical(units, b) & 1) == 1)
        def _():
            wait_copy(RUN_ROWS << b).wait()


def _dispatch_kernel(tab_ref, seg_ref, tot_ref, tail_ref, lpos_p_ref, xp_ref, lpos_s_ref, xs_in_ref, xs_ref,
                     loc_sc, zero_sc, sem, zsem):
    i = pl.program_id(0)
    last = pl.num_programs(0) - 1
    tile = i
    slot = i % 2
    loc = loc_sc.shape[1]

    @pl.when(i == 0)
    def _():
        zero_sc[...] = jnp.zeros_like(zero_sc)

        def tail_copy(e, j):
            row = pl.multiple_of(RUN_ROWS * (tail_ref[e] + j), RUN_ROWS)
            return pltpu.make_async_copy(zero_sc.at[pl.ds(0, RUN_ROWS)], xs_ref.at[pl.ds(row, RUN_ROWS)], zsem)

        def per_expert(e, carry):
            n = tail_ref[N_EXPERTS + e]
            lax.fori_loop(0, n, lambda j, c2: (tail_copy(e, j).start(), c2)[1], 0)
            lax.fori_loop(0, n, lambda j, c2: (tail_copy(e, j).wait(), c2)[1], 0)
            return carry

        lax.fori_loop(0, N_EXPERTS, per_expert, 0)

        def block_copy(b):
            row = pl.multiple_of(b * MOE_ROWS, MOE_ROWS)
            return pltpu.make_async_copy(zero_sc, xs_ref.at[pl.ds(row, MOE_ROWS)], zsem)

        first_unused, n_blocks = tail_ref[2 * N_EXPERTS], xs_ref.shape[0] // MOE_ROWS
        lax.fori_loop(first_unused, n_blocks, lambda b, c2: (block_copy(b).start(), c2)[1], 0)
        lax.fori_loop(first_unused, n_blocks, lambda b, c2: (block_copy(b).wait(), c2)[1], 0)

    def sort_tile(lpos_ref, x_ref):
        tm = x_ref.shape[0]
        rows = lax.broadcasted_iota(jnp.int32, (loc, tm), 0)
        lp = lpos_ref[...]
        onehot = jnp.zeros((loc, tm), F32)
        for k in range(TOP_K):
            onehot = jnp.where(rows == lp[k:k + 1], 1.0, onehot)
        loc_sc[slot] = jnp.dot(onehot.astype(BF16), x_ref[...].astype(BF16), preferred_element_type=F32)

    @pl.when(i < last)
    def _():
        sort_tile(lpos_p_ref, xp_ref)

    @pl.when(i == last)
    def _():
        sort_tile(lpos_s_ref, xs_in_ref)

    def piece_copy(sl, lrow, grow, e, n):
        dst = pl.multiple_of(seg_ref[e] + grow, RUN_ROWS)
        return pltpu.make_async_copy(loc_sc.at[sl, pl.ds(lrow, n)], xs_ref.at[pl.ds(dst, n)], sem.at[sl])

    _for_each_run_piece(tab_ref, tile, lambda lrow, grow, e, n: piece_copy(slot, lrow, grow, e, n).start())

    def drain(tl, sl):
        _drain_units(tot_ref[tl], lambda n: piece_copy(sl, 0, 0, 0, n), loc)

    @pl.when(i > 0)
    def _():
        drain(tile - 1, 1 - slot)

    @pl.when(i == last)
    def _():
        drain(tile, slot)


def _dispatch(tab, seg_start, tot, tails, lpos_p, x1_p, lpos_s, x1_s, *, tile, nrows):
    nt_p = x1_p.shape[0] // tile
    ns = x1_s.shape[0]
    loc = tile * TOP_K + N_EXPERTS * RUN_ROWS
    prompt_blk = lambda i, *_: jnp.minimum(i, nt_p - 1)
    return pl.pallas_call(
        _dispatch_kernel,
        grid_spec=pltpu.PrefetchScalarGridSpec(
            num_scalar_prefetch=4,
            grid=(nt_p + 1,),
            in_specs=[pl.BlockSpec((TOP_K, tile), lambda i, *_: (0, prompt_blk(i))),
                      pl.BlockSpec((tile, D_MODEL), lambda i, *_: (prompt_blk(i), 0)),
                      pl.BlockSpec((TOP_K, ns), lambda i, *_: (0, 0)),
                      pl.BlockSpec((ns, D_MODEL), lambda i, *_: (0, 0))],
            out_specs=pl.BlockSpec(memory_space=pl.ANY),
            scratch_shapes=[pltpu.VMEM((2, loc, D_MODEL), F32), pltpu.VMEM((MOE_ROWS, D_MODEL), F32),
                            pltpu.SemaphoreType.DMA((2,)), pltpu.SemaphoreType.DMA(())]),
        out_shape=jax.ShapeDtypeStruct((nrows, D_MODEL), F32),
        compiler_params=_cparams(("arbitrary",), VMEM_LIMIT),
        name="dispatch",
    )(tab, seg_start, tot, tails, lpos_p, x1_p, lpos_s, x1_s)


def _deinterleave_matrix():
    pm = np.zeros((MXU_DIM, MXU_DIM), np.float32)
    half = MXU_DIM // 2
    for c in range(half):
        pm[2 * c, c] = 1.0
        pm[2 * c + 1, half + c] = 1.0
    return pm


def _expert_kernel(be_ref, nu_ref, nv_ref, ord_ref, nxt_ref, xs_ref, w1_hbm, b1_ref, w2_hbm, b2_ref, pm_ref, y_ref,
                   w1f_sc, w2f_sc, w1p_sc, w2b_sc, sem):
    del nu_ref
    i = pl.program_id(0)
    e = be_ref[i]
    prev = be_ref[jnp.maximum(i - 1, 0)]
    nblk = 2 * D_FF // MXU_DIM

    def weight_copies(expert, slot):
        return (pltpu.make_async_copy(w1_hbm.at[expert], w1f_sc.at[slot], sem.at[0, slot]),
                pltpu.make_async_copy(w2_hbm.at[expert], w2f_sc.at[slot], sem.at[1, slot]))

    @pl.when(i == 0)
    def _():
        for cp in weight_copies(e, 0):
            cp.start()

    @pl.when((i == 0) | (e != prev))
    def _():
        slot = ord_ref[i] % 2
        for cp in weight_copies(e, slot):
            cp.wait()
        nxt = nxt_ref[i]

        @pl.when(nxt >= 0)
        def _():
            for cp in weight_copies(nxt, 1 - slot):
                cp.start()

        for cb in range(nblk):
            blk = w1f_sc[slot, :, cb * MXU_DIM:(cb + 1) * MXU_DIM].astype(BF16)
            w1p_sc[:, cb * MXU_DIM:(cb + 1) * MXU_DIM] = jnp.dot(
                blk, pm_ref[...], preferred_element_type=F32).astype(BF16)
        w2b_sc[...] = w2f_sc[slot].astype(BF16)

    for blk in range(MOE_STEP_BLOCKS):
        rows = slice(blk * MOE_ROWS, (blk + 1) * MOE_ROWS)

        @pl.when(blk < nv_ref[i])
        def _():
            x = xs_ref[rows, :].astype(BF16)
            h = jnp.dot(x, w1p_sc[...], preferred_element_type=F32) + b1_ref[0]
            half = MXU_DIM // 2
            acts = []
            for cb in range(nblk):
                x_glu = jnp.minimum(h[:, cb * MXU_DIM:cb * MXU_DIM + half], SWIGLU_LIMIT)
                x_lin = jnp.clip(h[:, cb * MXU_DIM + half:(cb + 1) * MXU_DIM], -SWIGLU_LIMIT, SWIGLU_LIMIT)
                acts.append((x_glu * jax.nn.sigmoid(SWIGLU_ALPHA * x_glu) * (x_lin + 1.0)).astype(BF16))
            act = jnp.concatenate(acts, axis=1)
            y_ref[rows, :] = jnp.dot(act, w2b_sc[...], preferred_element_type=F32) + b2_ref[0]

        @pl.when(blk >= nv_ref[i])
        def _():
            y_ref[rows, :] = jnp.zeros((MOE_ROWS, D_MODEL), F32)


def _experts(block_e, n_used, n_valid, run_ord, run_next, xs, w1, b1p, w2, b2, pm):
    nrows = xs.shape[0]
    step_rows = MOE_STEP_BLOCKS * MOE_ROWS
    nb = nrows // step_rows
    return pl.pallas_call(
        _expert_kernel,
        grid_spec=pltpu.PrefetchScalarGridSpec(
            num_scalar_prefetch=5,
            grid=(nb,),
            in_specs=[pl.BlockSpec((step_rows, D_MODEL), lambda i, be, nu, *_: (jnp.minimum(i, nu[0] - 1), 0)),
                      pl.BlockSpec(memory_space=pl.ANY),
                      pl.BlockSpec((1, 1, 2 * D_FF), lambda i, be, *_: (be[i], 0, 0)),
                      pl.BlockSpec(memory_space=pl.ANY),
                      pl.BlockSpec((1, 1, D_MODEL), lambda i, be, *_: (be[i], 0, 0)),
                      pl.BlockSpec((MXU_DIM, MXU_DIM), lambda i, *_: (0, 0))],
            out_specs=pl.BlockSpec((step_rows, D_MODEL), lambda i, *_: (i, 0)),
            scratch_shapes=[pltpu.VMEM((2, D_MODEL, 2 * D_FF), F32), pltpu.VMEM((2, D_FF, D_MODEL), F32),
                            pltpu.VMEM((D_MODEL, 2 * D_FF), BF16), pltpu.VMEM((D_FF, D_MODEL), BF16),
                            pltpu.SemaphoreType.DMA((2, 2))]),
        out_shape=jax.ShapeDtypeStruct((nrows, D_MODEL), F32),
        compiler_params=_cparams(("arbitrary",), VMEM_LIMIT_MERGE),
        name="experts",
    )(block_e, n_used, n_valid, run_ord, run_next, xs, w1, b1p, w2, b2, pm)


def _combine_kernel(tab_ref, seg_ref, tot_ref, cols_ref, x1_ref, g2_ref, b2_ref, ys_ref, y_ref, loc_sc, sem,
                    *, tile_base):
    i = pl.program_id(0)
    last = pl.num_programs(0) - 1
    tile = i + tile_base
    slot = i % 2
    loc, tm = loc_sc.shape[1], x1_ref.shape[0]

    def piece_copy(sl, lrow, grow, e, n):
        src = pl.multiple_of(seg_ref[e] + grow, RUN_ROWS)
        return pltpu.make_async_copy(ys_ref.at[pl.ds(src, n)], loc_sc.at[sl, pl.ds(lrow, n)], sem.at[sl])

    def gather(tl, sl):
        _for_each_run_piece(tab_ref, tl, lambda lrow, grow, e, n: piece_copy(sl, lrow, grow, e, n).start())

    @pl.when(i == 0)
    def _():
        loc_sc[...] = jnp.zeros_like(loc_sc)
        gather(tile, slot)

    @pl.when(i < last)
    def _():
        gather(tile + 1, 1 - slot)

    _drain_units(tot_ref[tile], lambda n: piece_copy(slot, 0, 0, 0, n), loc)

    cols = cols_ref[...]
    lane = lax.broadcasted_iota(jnp.int32, (tm, loc), 1)
    weights = jnp.zeros((tm, loc), F32)
    for k in range(TOP_K):
        weights = jnp.where(lane == cols[:, k:k + 1].astype(jnp.int32), cols[:, TOP_K + k:TOP_K + k + 1], weights)
    ffn = jnp.dot(weights.astype(BF16), loc_sc[slot].astype(BF16), preferred_element_type=F32)
    y_ref[...] = _layer_norm(DEEPNORM_ALPHA * x1_ref[...] + ffn, g2_ref[...], b2_ref[...])


def _combine(tab, seg_start, tot, cols, x1, g2, b2, ys, *, tile, tile_base):
    n = x1.shape[0]
    loc = tile * TOP_K + N_EXPERTS * RUN_ROWS
    return pl.pallas_call(
        functools.partial(_combine_kernel, tile_base=tile_base),
        grid_spec=pltpu.PrefetchScalarGridSpec(
            num_scalar_prefetch=3,
            grid=(n // tile,),
            in_specs=[pl.BlockSpec((tile, 2 * TOP_K), lambda i, *_: (i, 0)),
                      pl.BlockSpec((tile, D_MODEL), lambda i, *_: (i, 0)),
                      pl.BlockSpec((1, D_MODEL), lambda i, *_: (0, 0)),
                      pl.BlockSpec((1, D_MODEL), lambda i, *_: (0, 0)),
                      pl.BlockSpec(memory_space=pl.ANY)],
            out_specs=pl.BlockSpec((tile, D_MODEL), lambda i, *_: (i, 0)),
            scratch_shapes=[pltpu.VMEM((2, loc, D_MODEL), F32), pltpu.SemaphoreType.DMA((2,))]),
        out_shape=jax.ShapeDtypeStruct((n, D_MODEL), F32),
        compiler_params=_cparams(("arbitrary",), VMEM_LIMIT),
        name="combine",
    )(tab, seg_start, tot, cols, x1, g2, b2, ys)


def kernel(x_prompt, x_sample, cache_k_win, cache_v_win, state_ssm_re, state_ssm_im, w_in, b_in, attn_sinks,
           w_attn_out, ssm_a_re, ssm_a_im, ssm_log_dt, ssm_b_re, ssm_b_im, ssm_c_re, ssm_c_im, ssm_d, w_ssm_out,
           w_gate, b_gate, w_out, ln1_g, ln1_b, w_router, b_router, w_exp1, b_exp1, w_exp2, b_exp2, ln2_g, ln2_b):
    assert w_in.shape[0] == DEPTH == 1
    bsz, seq, _ = x_prompt.shape
    nsamp = x_sample.shape[0]
    assert x_sample.shape[1] == 1
    n_p = bsz * seq
    n_tok = n_p + nsamp

    xp = x_prompt.reshape(n_p, D_MODEL)
    xsm = x_sample.reshape(nsamp, D_MODEL)
    b_in2 = b_in[0].reshape(1, D_IN)
    sinks = attn_sinks[0].astype(F32)

    q_p, k_p, v_p, u_p = _proj(xp, w_in[0].astype(BF16), b_in2, tile=512, exact_f32=False, q_dtype=BF16)
    q_s, k_s, v_s, u_s = _proj(xsm, w_in[0], b_in2, tile=nsamp, exact_f32=True, q_dtype=F32)

    o_p = _attn_prompt(sinks, q_p.reshape(bsz, seq, D_ATTN), k_p.reshape(bsz, seq, D_KV),
                       v_p.reshape(bsz, seq, D_KV)).reshape(n_p, D_ATTN)
    k_buf = cache_k_win[0].reshape(nsamp, WINDOW, D_KV)
    v_buf = cache_v_win[0].reshape(nsamp, WINDOW, D_KV)
    o_s = _attn_sample(sinks, q_s, k_s, v_s, k_buf, v_buf)

    sp = _s5_params(ssm_a_re[0], ssm_a_im[0], ssm_log_dt[0], ssm_b_re[0], ssm_b_im[0], ssm_c_re[0], ssm_c_im[0])
    y_p, hp_re, hp_im = _s5_prompt(u_p, bsz, seq, _s5_chunk_mats(sp, ssm_d[0]))
    y_s, hs_re, hs_im = _s5_sample(u_s, state_ssm_re[0].reshape(nsamp, -1), state_ssm_im[0].reshape(nsamp, -1),
                                   _s5_sample_mats(sp, ssm_d[0]))

    wm = dict(wao=w_attn_out[0].astype(BF16), wso=w_ssm_out[0].astype(BF16), wg=w_gate[0].astype(BF16),
              bg=b_gate[0].reshape(1, -1), wo=w_out[0].astype(BF16), g1=ln1_g[0].reshape(1, -1),
              b1=ln1_b[0].reshape(1, -1), wrt=w_router[0].T, brt=b_router[0].reshape(-1, 1))
    carry0 = jnp.zeros((SUBLANES, LANES), F32)
    x1_p, lpos_p, cols_p, tab_p, carry1 = _merge(xp, o_p, y_p, carry0, wm, tile=MERGE_TILE, route_tile=TOK_TILE)
    x1_s, lpos_s, cols_s, tab_s, carry2 = _merge(xsm, o_s, y_s, carry1, wm, tile=nsamp, route_tile=nsamp)

    nt_p = n_p // TOK_TILE
    tab = jnp.concatenate([tab_p[:, :TAB_ROWS, :N_EXPERTS], tab_s[:, :TAB_ROWS, :N_EXPERTS]], axis=0)
    tot = jnp.sum(tab[:, 0, :], axis=1).astype(jnp.int32)
    tab = tab.reshape(-1)
    seg_rows = carry2[0, :N_EXPERTS].astype(jnp.int32) * RUN_ROWS
    step_rows = MOE_STEP_BLOCKS * MOE_ROWS
    padded = ((seg_rows + step_rows - 1) // step_rows) * step_rows
    pad_end = jnp.cumsum(padded)
    pad_start = (pad_end - padded).astype(jnp.int32)
    seg_end = pad_start + seg_rows
    n_runs = (nt_p + 1) * N_EXPERTS
    nb_max = (n_tok * TOP_K + n_runs * (RUN_ROWS - 1) + N_EXPERTS * (step_rows - 1) + step_rows - 1) // step_rows
    n_used = (pad_end[-1] // step_rows).astype(jnp.int32)
    tails = jnp.concatenate([seg_end // RUN_ROWS, (padded - seg_rows) // RUN_ROWS,
                             (pad_end[-1:] // MOE_ROWS)]).astype(jnp.int32)
    blk_start = jnp.arange(nb_max, dtype=jnp.int32) * step_rows
    blk_e = jnp.minimum(jnp.sum(blk_start[:, None] >= pad_end[None, :], axis=1), N_EXPERTS - 1).astype(jnp.int32)
    n_valid = jnp.clip((seg_end[blk_e] - blk_start + MOE_ROWS - 1) // MOE_ROWS, 0, MOE_STEP_BLOCKS)
    n_valid = jnp.where(jnp.arange(nb_max) < n_used, n_valid, 0).astype(jnp.int32)
    blk_e = jnp.where(jnp.arange(nb_max) < n_used, blk_e, blk_e[jnp.maximum(n_used - 1, 0)])
    new_run = jnp.concatenate([jnp.ones((1,), jnp.int32), (blk_e[1:] != blk_e[:-1]).astype(jnp.int32)])
    run_ord = (jnp.cumsum(new_run) - 1).astype(jnp.int32)
    ids = jnp.arange(N_EXPERTS, dtype=jnp.int32)
    later = (ids[None, :] > ids[:, None]) & (padded > 0)[None, :]
    next_e = jnp.min(jnp.where(later, ids[None, :], N_EXPERTS), axis=1)
    next_e = jnp.where(next_e < N_EXPERTS, next_e, -1).astype(jnp.int32)
    run_next = next_e[blk_e]

    nrows = nb_max * step_rows
    xs = _dispatch(tab, pad_start, tot, tails, lpos_p, x1_p, lpos_s, x1_s, tile=TOK_TILE, nrows=nrows)

    b1p = b_exp1[0].reshape(N_EXPERTS, 2 * D_FF // MXU_DIM, MXU_DIM // 2, 2)
    b1p = jnp.swapaxes(b1p, 2, 3).reshape(N_EXPERTS, 1, 2 * D_FF)
    ys = _experts(blk_e, n_used.reshape(1), n_valid, run_ord, run_next, xs, w_exp1[0], b1p, w_exp2[0],
                  b_exp2[0].reshape(N_EXPERTS, 1, D_MODEL),
                  jnp.asarray(_deinterleave_matrix(), BF16))

    g2, b2 = ln2_g[0].reshape(1, -1), ln2_b[0].reshape(1, -1)
    y_prompt = _combine(tab, pad_start, tot, cols_p, x1_p, g2, b2, ys, tile=TOK_TILE, tile_base=0)
    y_sample = _combine(tab, pad_start, tot, cols_s, x1_s, g2, b2, ys, tile=nsamp, tile_base=nt_p)

    k_p4 = k_p.reshape(bsz, seq, D_KV)[:, -WINDOW:].reshape(bsz, WINDOW, N_KV_HEADS, HEAD_DIM)
    v_p4 = v_p.reshape(bsz, seq, D_KV)[:, -WINDOW:].reshape(bsz, WINDOW, N_KV_HEADS, HEAD_DIM)
    k_s4 = jnp.concatenate([cache_k_win[0][:, 1:], k_s.reshape(nsamp, 1, N_KV_HEADS, HEAD_DIM)], axis=1)
    v_s4 = jnp.concatenate([cache_v_win[0][:, 1:], v_s.reshape(nsamp, 1, N_KV_HEADS, HEAD_DIM)], axis=1)
    st = lambda a, n: a.reshape(1, n, N_SSM_GROUPS, SSM_STATE)
    return (y_prompt.reshape(bsz, seq, D_MODEL), y_sample.reshape(nsamp, 1, D_MODEL),
            k_p4[None], v_p4[None], st(hp_re, bsz), st(hp_im, bsz),
            k_s4[None], v_s4[None], st(hs_re, nsamp), st(hs_im, nsamp))
```

```python
import functools
import math

import numpy as np
import jax
import jax.numpy as jnp
from jax import lax
from jax.experimental import pallas as pl
from jax.experimental.pallas import tpu as pltpu

F32 = jnp.float32
BF16 = jnp.bfloat16

D_MODEL = 1024
HEAD_DIM = 64
N_Q_HEADS = 8
N_KV_HEADS = 2
Q_PER_KV = N_Q_HEADS // N_KV_HEADS
D_ATTN = N_Q_HEADS * HEAD_DIM
D_KV = N_KV_HEADS * HEAD_DIM
WINDOW = 128
ATTN_SCALE = HEAD_DIM ** -0.5
SSM_GROUP = 16
D_SSM = D_MODEL // 2
N_SSM_GROUPS = D_SSM // SSM_GROUP
SSM_STATE = 64
D_IN = D_ATTN + 2 * D_KV + D_SSM
N_EXPERTS = 32
TOP_K = 4
D_FF = D_MODEL
SWIGLU_LIMIT = 7.0
SWIGLU_ALPHA = 1.702
LN_EPS = 1e-5
DEPTH = 1
DEEPNORM_ALPHA = (2 * DEPTH) ** 0.25

LANES = 128
SUBLANES = 8
MXU_DIM = 256

S5_CHUNK = MXU_DIM // SSM_GROUP
S5_LANE_GROUPS = LANES // SSM_GROUP
MOE_ROWS = 256
MOE_STEP_BLOCKS = 2
TOK_TILE = 256
MERGE_TILE = 512
VMEM_LIMIT = 48 * 1024 * 1024
VMEM_LIMIT_MERGE = 56 * 1024 * 1024


def _cparams(sem, vmem=None):
    return pltpu.CompilerParams(dimension_semantics=sem, vmem_limit_bytes=vmem)


def _proj_kernel(x_ref, w_ref, b_ref, q_ref, k_ref, v_ref, u_ref, *, exact_f32):
    if exact_f32:
        h = jnp.dot(x_ref[...], w_ref[...], preferred_element_type=F32, precision=lax.Precision.HIGHEST)
    else:
        h = jnp.dot(x_ref[...].astype(BF16), w_ref[...], preferred_element_type=F32)
    h = h + b_ref[...]
    q_ref[...] = (h[:, :D_ATTN] * ATTN_SCALE).astype(q_ref.dtype)
    k_ref[...] = h[:, D_ATTN:D_ATTN + D_KV]
    v_ref[...] = h[:, D_ATTN + D_KV:D_ATTN + 2 * D_KV]
    u_ref[...] = h[:, D_ATTN + 2 * D_KV:].astype(u_ref.dtype)


def _proj(x, w, b, *, tile, exact_f32, q_dtype):
    n = x.shape[0]
    return pl.pallas_call(
        functools.partial(_proj_kernel, exact_f32=exact_f32),
        grid=(n // tile,),
        in_specs=[pl.BlockSpec((tile, D_MODEL), lambda i: (i, 0)),
                  pl.BlockSpec((D_MODEL, D_IN), lambda i: (0, 0)),
                  pl.BlockSpec((1, D_IN), lambda i: (0, 0))],
        out_specs=[pl.BlockSpec((tile, D_ATTN), lambda i: (i, 0)),
                   pl.BlockSpec((tile, D_KV), lambda i: (i, 0)),
                   pl.BlockSpec((tile, D_KV), lambda i: (i, 0)),
                   pl.BlockSpec((tile, D_SSM), lambda i: (i, 0))],
        out_shape=[jax.ShapeDtypeStruct((n, D_ATTN), q_dtype),
                   jax.ShapeDtypeStruct((n, D_KV), F32),
                   jax.ShapeDtypeStruct((n, D_KV), F32),
                   jax.ShapeDtypeStruct((n, D_SSM), F32)],
        compiler_params=_cparams(("parallel",)),
        name="proj",
    )(x, w, b)


ATT_Q_TILE = 512


def _attn_prompt_kernel(sink_ref, q_ref, k_ref, v_ref, o_ref):
    i = pl.program_id(1)
    nk, nq = 2 * WINDOW, 2 * WINDOW
    lo = lax.broadcasted_iota(jnp.int32, (nk, LANES), 1) < HEAD_DIM
    top = lax.broadcasted_iota(jnp.int32, (nq, 1), 0) < WINDOW
    for blk in range(ATT_Q_TILE // WINDOW):
        q0 = i * ATT_Q_TILE + blk * WINDOW
        k0 = pl.multiple_of(jnp.maximum(q0 - WINDOW, 0), WINDOW)
        kk = k_ref[0, pl.ds(k0, nk), :]
        vv = v_ref[0, pl.ds(k0, nk), :]
        kk_sw = pltpu.roll(kk, HEAD_DIM, axis=1)
        vv_sw = pltpu.roll(vv, HEAD_DIM, axis=1)
        k_var = [[jnp.where(lo, kk, 0.0).astype(BF16), jnp.where(lo, 0.0, kk_sw).astype(BF16)],
                 [jnp.where(lo, kk_sw, 0.0).astype(BF16), jnp.where(lo, 0.0, kk).astype(BF16)]]
        v_var = [[jnp.where(lo, vv, 1.0).astype(BF16), jnp.where(lo, 1.0, vv_sw).astype(BF16)],
                 [jnp.where(lo, vv_sw, 1.0).astype(BF16), jnp.where(lo, 1.0, vv).astype(BF16)]]
        qpos = q0 + lax.broadcasted_iota(jnp.int32, (nq, nk), 0) % WINDOW
        kpos = k0 + lax.broadcasted_iota(jnp.int32, (nq, nk), 1)
        valid = (kpos <= qpos) & (qpos - kpos <= WINDOW)
        rows = slice(blk * WINDOW, (blk + 1) * WINDOW)
        for kv in range(N_KV_HEADS):
            pairs = (2 * kv, 2 * kv + 1)
            qs = jnp.concatenate([q_ref[0, rows, pr * LANES:(pr + 1) * LANES] for pr in pairs], axis=0)
            outs = []
            for parity in range(2):
                sink = jnp.where(top, sink_ref[2 * pairs[0] + parity], sink_ref[2 * pairs[1] + parity])
                s = lax.dot_general(qs, k_var[kv][parity], (((1,), (1,)), ((), ())), preferred_element_type=F32)
                s = jnp.where(valid, s, -jnp.inf)
                m = jnp.maximum(jnp.max(s, axis=-1, keepdims=True), sink)
                p = jnp.exp(s - m).astype(BF16)
                acc = jnp.dot(p, v_var[kv][parity], preferred_element_type=F32)
                outs.append(acc / (pltpu.roll(acc, HEAD_DIM, axis=1) + jnp.exp(sink - m)))
            o = jnp.where(lo, outs[0], outs[1]).astype(o_ref.dtype)
            for j, pr in enumerate(pairs):
                o_ref[0, rows, pr * LANES:(pr + 1) * LANES] = o[j * WINDOW:(j + 1) * WINDOW]


def _attn_prompt(sinks, q, k, v):
    bsz, seq = q.shape[0], q.shape[1]
    return pl.pallas_call(
        _attn_prompt_kernel,
        grid=(bsz, seq // ATT_Q_TILE),
        in_specs=[pl.BlockSpec(memory_space=pltpu.SMEM),
                  pl.BlockSpec((1, ATT_Q_TILE, D_ATTN), lambda b, i: (b, i, 0)),
                  pl.BlockSpec((1, seq, D_KV), lambda b, i: (b, 0, 0)),
                  pl.BlockSpec((1, seq, D_KV), lambda b, i: (b, 0, 0))],
        out_specs=pl.BlockSpec((1, ATT_Q_TILE, D_ATTN), lambda b, i: (b, i, 0)),
        out_shape=jax.ShapeDtypeStruct((bsz, seq, D_ATTN), BF16),
        compiler_params=_cparams(("parallel", "parallel")),
        name="attn_prompt",
    )(sinks, q, k, v)


ATT_S_GROUP = 8


def _attn_sample_kernel(sink_ref, q_ref, kn_ref, vn_ref, kb_ref, vb_ref, o_ref):
    g = ATT_S_GROUP
    rows = Q_PER_KV * g
    ncol = g * WINDOW
    kb = kb_ref[...].reshape(ncol, D_KV).astype(BF16)
    vb = vb_ref[...].reshape(ncol, D_KV).astype(BF16)
    rseq = lax.broadcasted_iota(jnp.int32, (rows, ncol), 0) % g
    cseq = lax.broadcasted_iota(jnp.int32, (rows, ncol), 1) // WINDOW
    own = rseq == cseq
    rhead = lax.broadcasted_iota(jnp.int32, (rows, 1), 0) // g
    for kv in range(N_KV_HEADS):
        lo = kv * HEAD_DIM
        qs = jnp.concatenate(
            [q_ref[:, (kv * Q_PER_KV + h) * HEAD_DIM:(kv * Q_PER_KV + h + 1) * HEAD_DIM] for h in range(Q_PER_KV)],
            axis=0)
        kn = jnp.concatenate([kn_ref[:, lo:lo + HEAD_DIM]] * Q_PER_KV, axis=0)
        vn = jnp.concatenate([vn_ref[:, lo:lo + HEAD_DIM]] * Q_PER_KV, axis=0)
        sink = jnp.zeros((rows, 1), F32)
        for h in range(Q_PER_KV):
            sink = jnp.where(rhead == h, sink_ref[kv * Q_PER_KV + h], sink)
        qs = qs.astype(BF16)
        s = lax.dot_general(qs, kb[:, lo:lo + HEAD_DIM], (((1,), (1,)), ((), ())), preferred_element_type=F32)
        s = jnp.where(own, s, -jnp.inf)
        s_new = jnp.sum(qs.astype(F32) * kn.astype(BF16).astype(F32), axis=-1, keepdims=True)
        m = jnp.maximum(jnp.maximum(jnp.max(s, axis=-1, keepdims=True), s_new), sink)
        p = jnp.exp(s - m)
        p_new = jnp.exp(s_new - m)
        denom = jnp.sum(p, axis=-1, keepdims=True) + p_new + jnp.exp(sink - m)
        o = jnp.dot(p.astype(BF16), vb[:, lo:lo + HEAD_DIM], preferred_element_type=F32)
        o = (o + p_new.astype(BF16).astype(F32) * vn.astype(BF16).astype(F32)) / denom
        for h in range(Q_PER_KV):
            c0 = (kv * Q_PER_KV + h) * HEAD_DIM
            o_ref[:, c0:c0 + HEAD_DIM] = o[h * g:(h + 1) * g].astype(o_ref.dtype)


def _attn_sample(sinks, q, k_new, v_new, k_buf, v_buf):
    n = q.shape[0]
    g = ATT_S_GROUP
    return pl.pallas_call(
        _attn_sample_kernel,
        grid=(n // g,),
        in_specs=[pl.BlockSpec(memory_space=pltpu.SMEM),
                  pl.BlockSpec((g, D_ATTN), lambda i: (i, 0)),
                  pl.BlockSpec((g, D_KV), lambda i: (i, 0)),
                  pl.BlockSpec((g, D_KV), lambda i: (i, 0)),
                  pl.BlockSpec((g, WINDOW, D_KV), lambda i: (i, 0, 0)),
                  pl.BlockSpec((g, WINDOW, D_KV), lambda i: (i, 0, 0))],
        out_specs=pl.BlockSpec((g, D_ATTN), lambda i: (i, 0)),
        out_shape=jax.ShapeDtypeStruct((n, D_ATTN), F32),
        compiler_params=_cparams(("parallel",)),
        name="attn_sample",
    )(sinks, q, k_new, v_new, k_buf, v_buf)


def _s5_params(a_re, a_im, log_dt, b_re, b_im, c_re, c_im):
    hp = lax.Precision.HIGHEST
    dt = jnp.exp(log_dt.astype(F32))[:, None]
    are, aim = a_re.astype(F32), a_im.astype(F32)
    tau = jnp.arange(S5_CHUNK + 1, dtype=F32)[None, :, None]
    mag = jnp.exp(tau * (dt * are)[:, None, :])
    ang = tau * (dt * aim)[:, None, :]
    pw_re, pw_im = mag * jnp.cos(ang), mag * jnp.sin(ang)
    ab_re, ab_im = pw_re[:, 1], pw_im[:, 1]
    den = are * are + aim * aim
    f_re = ((ab_re - 1.0) * are + ab_im * aim) / den
    f_im = (ab_im * are - (ab_re - 1.0) * aim) / den
    bre, bim = b_re.astype(F32), b_im.astype(F32)
    bb_re = f_re[..., None] * bre - f_im[..., None] * bim
    bb_im = f_re[..., None] * bim + f_im[..., None] * bre
    cre, cim = c_re.astype(F32), c_im.astype(F32)
    return dict(pw_re=pw_re, pw_im=pw_im, ab_re=ab_re, ab_im=ab_im, bb_re=bb_re, bb_im=bb_im,
                c_re=cre, c_im=cim, hp=hp)


def _s5_chunk_mats(sp, d_skip):
    hp = sp["hp"]
    g, t, c, p = N_SSM_GROUPS, S5_CHUNK, SSM_GROUP, SSM_STATE
    pw_re, pw_im = sp["pw_re"], sp["pw_im"]
    ca_re = sp["c_re"][:, None] * pw_re[:, :, None, :] - sp["c_im"][:, None] * pw_im[:, :, None, :]
    ca_im = sp["c_re"][:, None] * pw_im[:, :, None, :] + sp["c_im"][:, None] * pw_re[:, :, None, :]
    kern = (jnp.einsum("gtcp,gpd->gtcd", ca_re[:, :t], sp["bb_re"], precision=hp)
            - jnp.einsum("gtcp,gpd->gtcd", ca_im[:, :t], sp["bb_im"], precision=hp))
    kc = jnp.swapaxes(kern, 2, 3)
    kc = kc.at[:, 0].add(d_skip.astype(F32).reshape(g, 1, c) * jnp.eye(c, dtype=F32)[None])
    rev_re, rev_im = pw_re[:, t - 1::-1][:, :t], pw_im[:, t - 1::-1][:, :t]
    wst_re = rev_re[:, :, None, :] * jnp.swapaxes(sp["bb_re"], 1, 2)[:, None] \
        - rev_im[:, :, None, :] * jnp.swapaxes(sp["bb_im"], 1, 2)[:, None]
    wst_im = rev_re[:, :, None, :] * jnp.swapaxes(sp["bb_im"], 1, 2)[:, None] \
        + rev_im[:, :, None, :] * jnp.swapaxes(sp["bb_re"], 1, 2)[:, None]
    wo_re = jnp.transpose(ca_re[:, 1:t + 1], (0, 3, 1, 2))
    wo_im = -jnp.transpose(ca_im[:, 1:t + 1], (0, 3, 1, 2))
    nv, gl = g // S5_LANE_GROUPS, S5_LANE_GROUPS
    kc, wst_re, wst_im, wo_re, wo_im = lax.optimization_barrier((kc, wst_re, wst_im, wo_re, wo_im))
    kc5 =jnp.transpose(kc.reshape(nv, gl, t, c, c), (0, 2, 1, 3, 4))
    ws6 = jnp.transpose(jnp.stack([wst_re, wst_im], axis=3).reshape(nv, gl, t, c, 2, p),
                        (0, 2, 1, 3, 4, 5))
    wo6 = jnp.transpose(jnp.stack([wo_re, wo_im], axis=0).reshape(2, nv, gl, p, t, c),
                        (1, 0, 2, 3, 4, 5))
    kc5, ws6, wo6 = lax.optimization_barrier((kc5.astype(BF16), ws6.astype(BF16), wo6.astype(BF16)))
    gid = jnp.arange(gl)
    zero = jnp.zeros((), BF16)
    same5 = (gid[:, None, None, None] == gid[None, None, :, None])
    bd = jnp.where(same5[None, None], kc5[:, :, :, :, None, :], zero).reshape(nv, t, LANES, LANES)
    spread_s = np.zeros((2 * p, 2 * gl * p), np.float32)
    spread_o = np.zeros((t * c, t * LANES), np.float32)
    for h in range(gl):
        for ri in range(2):
            spread_s[ri * p + np.arange(p), ri * gl * p + h * p + np.arange(p)] = 1.0
        for tt in range(t):
            spread_o[tt * c + np.arange(c), tt * LANES + h * c + np.arange(c)] = 1.0
    row_g = (jnp.arange(t * LANES) // c) % gl
    blk_g = (jnp.arange(2 * gl * p) // p) % gl
    wst_v = jnp.einsum("vrk,kc->vrc", ws6.reshape(nv, t * LANES, 2 * p), jnp.asarray(spread_s, BF16),
                       preferred_element_type=F32)
    wst_v = jnp.where((row_g[:, None] == blk_g[None, :])[None], wst_v, 0.0).astype(BF16)
    wout_v = jnp.einsum("vrk,kc->vrc", wo6.reshape(nv, 2 * gl * p, t * c), jnp.asarray(spread_o, BF16),
                        preferred_element_type=F32)
    wout_v = jnp.where((blk_g[:, None] == row_g[None, :])[None], wout_v, 0.0).astype(BF16)
    at_re = pw_re[:, t].reshape(1, g * p)
    at_im = pw_im[:, t].reshape(1, g * p)
    return bd, wst_v, wout_v, at_re, at_im


def _s5_chunk_rows(u_ref, nchunk):
    return jnp.concatenate(
        [u_ref[pl.ds(s, nchunk, stride=S5_CHUNK), :] for s in range(S5_CHUNK)], axis=1).astype(BF16)


S5_SLABS = S5_LANE_GROUPS * SSM_STATE // LANES


def _s5_state_kernel(u_ref, wst_ref, sre_ref, sim_ref):
    nchunk = sre_ref.shape[1]
    s = jnp.dot(_s5_chunk_rows(u_ref, nchunk), wst_ref[0], preferred_element_type=F32)
    for k in range(S5_SLABS):
        sre_ref[k] = s[:, k * LANES:(k + 1) * LANES]
        sim_ref[k] = s[:, (S5_SLABS + k) * LANES:(S5_SLABS + k + 1) * LANES]


def _s5_scan_kernel(sre_ref, sim_ref, are_ref, aim_ref, hre_ref, him_ref, fre_ref, fim_ref, *, bsz):
    nchunk = sre_ref.shape[1] // bsz
    are = [jnp.broadcast_to(are_ref[:, k * LANES:(k + 1) * LANES], (bsz, LANES)) for k in range(S5_SLABS)]
    aim = [jnp.broadcast_to(aim_ref[:, k * LANES:(k + 1) * LANES], (bsz, LANES)) for k in range(S5_SLABS)]

    def body(j, carry):
        rows = pl.ds(j, bsz, stride=nchunk)
        out = []
        for k in range(S5_SLABS):
            cre, cim = carry[2 * k], carry[2 * k + 1]
            hre_ref[k, rows, :] = cre
            him_ref[k, rows, :] = cim
            sr = sre_ref[k, rows, :]
            si = sim_ref[k, rows, :]
            out += [are[k] * cre - aim[k] * cim + sr, are[k] * cim + aim[k] * cre + si]
        return tuple(out)

    zero = jnp.zeros((bsz, LANES), F32)
    fin = lax.fori_loop(0, nchunk, body, (zero,) * (2 * S5_SLABS))
    fre_ref[...] = jnp.concatenate(fin[0::2], axis=1)
    fim_ref[...] = jnp.concatenate(fin[1::2], axis=1)


def _s5_out_kernel(u_ref, bd_ref, hre_ref, him_ref, wout_ref, y_ref, m_sc):
    nchunk = hre_ref.shape[1]

    @pl.when(pl.program_id(1) == 0)
    def _():
        for s in range(S5_CHUNK):
            for t in range(S5_CHUNK):
                blk = bd_ref[0, t - s] if t >= s else jnp.zeros((LANES, LANES), BF16)
                m_sc[s * LANES:(s + 1) * LANES, t * LANES:(t + 1) * LANES] = blk

    hcat = jnp.concatenate([hre_ref[k] for k in range(S5_SLABS)] + [him_ref[k] for k in range(S5_SLABS)],
                           axis=1).astype(BF16)
    y = jnp.dot(_s5_chunk_rows(u_ref, nchunk), m_sc[...], preferred_element_type=F32)
    y = y + jnp.dot(hcat, wout_ref[0], preferred_element_type=F32)
    for s in range(S5_CHUNK):
        y_ref[pl.ds(s, nchunk, stride=S5_CHUNK), :] = y[:, s * LANES:(s + 1) * LANES]


def _s5_prompt(u, bsz, seq, mats):
    bd, wst_v, wout_v, at_re, at_im = mats
    g, t, p = N_SSM_GROUPS, S5_CHUNK, SSM_STATE
    nchunk = seq // t
    n = nchunk * bsz
    nv = g // S5_LANE_GROUPS
    half = S5_LANE_GROUPS * p
    s_re, s_im = pl.pallas_call(
        _s5_state_kernel,
        grid=(nv, bsz),
        in_specs=[pl.BlockSpec((seq, LANES), lambda v, b: (b, v)),
                  pl.BlockSpec((1, t * LANES, 2 * half), lambda v, b: (v, 0, 0))],
        out_specs=[pl.BlockSpec((S5_SLABS, nchunk, LANES), lambda v, b: (v, b, 0)),
                   pl.BlockSpec((S5_SLABS, nchunk, LANES), lambda v, b: (v, b, 0))],
        out_shape=[jax.ShapeDtypeStruct((nv * S5_SLABS, n, LANES), F32)] * 2,
        compiler_params=_cparams(("parallel", "parallel"), VMEM_LIMIT),
        name="s5_state",
    )(u, wst_v)
    h_re, h_im, f_re, f_im = pl.pallas_call(
        functools.partial(_s5_scan_kernel, bsz=bsz),
        grid=(nv,),
        in_specs=[pl.BlockSpec((S5_SLABS, n, LANES), lambda i: (i, 0, 0)),
                  pl.BlockSpec((S5_SLABS, n, LANES), lambda i: (i, 0, 0)),
                  pl.BlockSpec((1, half), lambda i: (0, i)),
                  pl.BlockSpec((1, half), lambda i: (0, i))],
        out_specs=[pl.BlockSpec((S5_SLABS, n, LANES), lambda i: (i, 0, 0)),
                   pl.BlockSpec((S5_SLABS, n, LANES), lambda i: (i, 0, 0)),
                   pl.BlockSpec((bsz, half), lambda i: (0, i)),
                   pl.BlockSpec((bsz, half), lambda i: (0, i))],
        out_shape=[jax.ShapeDtypeStruct((nv * S5_SLABS, n, LANES), F32)] * 2
        + [jax.ShapeDtypeStruct((bsz, g * p), F32)] * 2,
        compiler_params=_cparams(("parallel",)),
        name="s5_scan",
    )(s_re, s_im, at_re, at_im)
    y = pl.pallas_call(
        _s5_out_kernel,
        grid=(nv, bsz),
        in_specs=[pl.BlockSpec((seq, LANES), lambda v, b: (b, v)),
                  pl.BlockSpec((1, t, LANES, LANES), lambda v, b: (v, 0, 0, 0)),
                  pl.BlockSpec((S5_SLABS, nchunk, LANES), lambda v, b: (v, b, 0)),
                  pl.BlockSpec((S5_SLABS, nchunk, LANES), lambda v, b: (v, b, 0)),
                  pl.BlockSpec((1, 2 * half, t * LANES), lambda v, b: (v, 0, 0))],
        out_specs=pl.BlockSpec((seq, LANES), lambda v, b: (b, v)),
        out_shape=jax.ShapeDtypeStruct((bsz * seq, D_SSM), F32),
        scratch_shapes=[pltpu.VMEM((t * LANES, t * LANES), BF16)],
        compiler_params=_cparams(("parallel", "arbitrary"), VMEM_LIMIT),
        name="s5_out",
    )(u, bd, h_re, h_im, wout_v)
    return y, f_re, f_im


S5S_GROUPS = LANES // SSM_GROUP


def _s5_sample_mats(sp, d_skip):
    go, gl, c, p = N_SSM_GROUPS // S5S_GROUPS, S5S_GROUPS, SSM_GROUP, SSM_STATE
    eye = jnp.eye(gl, dtype=F32)

    def bdiag_in(b):
        b4 = b.reshape(go, gl, p, c)
        return jnp.einsum("ogpc,gh->ogchp", b4, eye).reshape(go, gl * c, gl * p)

    def bdiag_out(cm):
        c4 = cm.reshape(go, gl, c, p)
        return jnp.einsum("ogcp,gh->ogphc", c4, eye).reshape(go, gl * p, gl * c)

    b8 = jnp.concatenate([bdiag_in(sp["bb_re"]), bdiag_in(sp["bb_im"])], axis=2)
    c8 = jnp.concatenate([bdiag_out(sp["c_re"]), -bdiag_out(sp["c_im"])], axis=1)
    a_re = sp["ab_re"].reshape(1, N_SSM_GROUPS * p)
    a_im = sp["ab_im"].reshape(1, N_SSM_GROUPS * p)
    return b8, c8, a_re, a_im, d_skip.astype(F32).reshape(1, D_SSM)


def _s5_sample_kernel(u_ref, hre_ref, him_ref, b8_ref, c8_ref, are_ref, aim_ref, d_ref,
                      y_ref, ore_ref, oim_ref):
    hp = lax.Precision.HIGHEST
    u = u_ref[...]
    half = S5S_GROUPS * SSM_STATE
    bu = jnp.dot(u, b8_ref[0], preferred_element_type=F32, precision=hp)
    are, aim = are_ref[...], aim_ref[...]
    h0r, h0i = hre_ref[...], him_ref[...]
    hr = are * h0r - aim * h0i + bu[:, :half]
    hi = are * h0i + aim * h0r + bu[:, half:]
    ore_ref[...] = hr
    oim_ref[...] = hi
    y = jnp.dot(jnp.concatenate([hr, hi], axis=1), c8_ref[0], preferred_element_type=F32, precision=hp)
    y_ref[...] = (y + d_ref[...] * u).astype(y_ref.dtype)


def _s5_sample(u, h0_re, h0_im, mats):
    b8, c8, a_re, a_im, d = mats
    n = u.shape[0]
    half = S5S_GROUPS * SSM_STATE
    return pl.pallas_call(
        _s5_sample_kernel,
        grid=(N_SSM_GROUPS // S5S_GROUPS,),
        in_specs=[pl.BlockSpec((n, LANES), lambda i: (0, i)),
                  pl.BlockSpec((n, half), lambda i: (0, i)),
                  pl.BlockSpec((n, half), lambda i: (0, i)),
                  pl.BlockSpec((1, LANES, 2 * half), lambda i: (i, 0, 0)),
                  pl.BlockSpec((1, 2 * half, LANES), lambda i: (i, 0, 0)),
                  pl.BlockSpec((1, half), lambda i: (0, i)),
                  pl.BlockSpec((1, half), lambda i: (0, i)),
                  pl.BlockSpec((1, LANES), lambda i: (0, i))],
        out_specs=[pl.BlockSpec((n, LANES), lambda i: (0, i)),
                   pl.BlockSpec((n, half), lambda i: (0, i)),
                   pl.BlockSpec((n, half), lambda i: (0, i))],
        out_shape=[jax.ShapeDtypeStruct((n, D_SSM), BF16),
                   jax.ShapeDtypeStruct((n, N_SSM_GROUPS * SSM_STATE), F32),
                   jax.ShapeDtypeStruct((n, N_SSM_GROUPS * SSM_STATE), F32)],
        compiler_params=_cparams(("parallel",)),
        name="s5_sample",
    )(u, h0_re, h0_im, b8, c8, a_re, a_im, d)


def _layer_norm(x, g, b):
    mu = jnp.mean(x, axis=-1, keepdims=True)
    xc = x - mu
    var = jnp.mean(xc * xc, axis=-1, keepdims=True)
    return xc * lax.rsqrt(var + LN_EPS) * g + b


RUN_ROWS = SUBLANES
TAB_ROWS = 3


def _merge_kernel(x_ref, oa_ref, ys_ref, carry_in_ref, wao_ref, wso_ref, wg_ref, bg_ref, wo_ref,
                  g1_ref, b1_ref, wrt_ref, brt_ref,
                  x1_ref, lpos_ref, cols_ref, tab_ref, carry_out_ref, carry_sc):
    step = pl.program_id(0)

    @pl.when(step == 0)
    def _():
        carry_sc[...] = carry_in_ref[...]

    tm = x_ref.shape[0]
    x = x_ref[...]
    branch_a = jnp.dot(oa_ref[...].astype(BF16), wao_ref[...], preferred_element_type=F32)
    z = jnp.dot(jax.nn.gelu(ys_ref[...].astype(F32)).astype(BF16), wso_ref[...], preferred_element_type=F32)
    branch_b = z[:, :D_MODEL] * jax.nn.sigmoid(z[:, D_MODEL:])
    gates = jax.nn.sigmoid(jnp.dot(x.astype(BF16), wg_ref[...], preferred_element_type=F32) + bg_ref[...])
    mixed = gates[:, :D_MODEL] * branch_a + gates[:, D_MODEL:] * branch_b
    mix = jnp.dot(mixed.astype(BF16), wo_ref[...], preferred_element_type=F32)
    x1 = _layer_norm(DEEPNORM_ALPHA * x + mix, g1_ref[...], b1_ref[...])
    x1_ref[...] = x1

    def split2(v):
        hi = v.astype(BF16)
        return hi, (v - hi.astype(F32)).astype(BF16)

    def dot_nt(a, b):
        return lax.dot_general(a, b, (((1,), (1,)), ((), ())), preferred_element_type=F32)

    w_hi, w_lo = split2(wrt_ref[...])
    rt = tm // tab_ref.shape[0]
    sub = lax.broadcasted_iota(jnp.int32, (N_EXPERTS, rt), 0)
    r = lax.broadcasted_iota(jnp.int32, (rt, rt), 0)
    c = lax.broadcasted_iota(jnp.int32, (rt, rt), 1)
    er = lax.broadcasted_iota(jnp.int32, (N_EXPERTS, N_EXPERTS), 0)
    ec = lax.broadcasted_iota(jnp.int32, (N_EXPERTS, N_EXPERTS), 1)
    rid = lax.broadcasted_iota(jnp.int32, (SUBLANES, LANES), 0)
    lane_pad = jnp.zeros((SUBLANES, LANES - N_EXPERTS), F32)
    for h in range(tab_ref.shape[0]):
        x_hi, x_lo = split2(x1[h * rt:(h + 1) * rt])
        logits = dot_nt(w_hi, x_hi) + dot_nt(w_hi, x_lo) + dot_nt(w_lo, x_hi) + brt_ref[...]
        work = logits
        vals, sels = [], []
        for _ in range(TOP_K):
            mx = jnp.max(work, axis=0, keepdims=True)
            idx = jnp.min(jnp.where(work == mx, sub, N_EXPERTS), axis=0, keepdims=True)
            sel = sub == idx
            vals.append(mx)
            sels.append(sel)
            work = jnp.where(sel, -jnp.inf, work)
        ex = [jnp.exp(v - vals[0]) for v in vals]
        tot = ex[0] + ex[1] + ex[2] + ex[3]
        gate_rows = jnp.concatenate([e / tot for e in ex], axis=0)

        multi = jnp.zeros((N_EXPERTS, rt), F32)
        for sel in sels:
            multi = multi + jnp.where(sel, 1.0, 0.0)
        multi_b = multi.astype(BF16)
        earlier = jnp.dot(multi_b, jnp.where(r < c, 1.0, 0.0).astype(BF16), preferred_element_type=F32)
        cnt_col = jnp.sum(multi, axis=1, keepdims=True)
        nb_col = jnp.floor((cnt_col + (RUN_ROWS - 1.0)) * (1.0 / RUN_ROWS))
        loff_col = jnp.dot(jnp.where(ec < er, 1.0, 0.0).astype(BF16),
                           jnp.broadcast_to(nb_col, (N_EXPERTS, rt)).astype(BF16), preferred_element_type=F32)
        base = RUN_ROWS * loff_col + earlier
        lpos = jnp.concatenate([jnp.sum(jnp.where(sel, base, 0.0), axis=0, keepdims=True) for sel in sels],
                               axis=0)
        lpos_ref[:, h * rt:(h + 1) * rt] = lpos.astype(jnp.int32)
        rows_hi, rows_lo = split2(jnp.concatenate([lpos, gate_rows], axis=0))
        eye = jnp.where(r == c, 1.0, 0.0).astype(BF16)
        cols_ref[h * rt:(h + 1) * rt, :] = dot_nt(eye, rows_hi) + dot_nt(eye, rows_lo)

        cnt_row = dot_nt(jnp.ones((SUBLANES, rt), BF16), multi_b)
        nb_row = jnp.floor((cnt_row + (RUN_ROWS - 1.0)) * (1.0 / RUN_ROWS))
        loff_row = jnp.dot(nb_row.astype(BF16), jnp.where(er < ec, 1.0, 0.0).astype(BF16),
                           preferred_element_type=F32)
        nb_p = jnp.concatenate([nb_row, lane_pad], axis=1)
        loff_p = jnp.concatenate([loff_row, lane_pad], axis=1)
        goff_p = carry_sc[...]
        tab = jnp.where(rid == 0, nb_p, jnp.where(rid == 1, loff_p, jnp.where(rid == 2, goff_p, 0.0)))
        tab_ref[h] = tab.astype(jnp.int32)
        carry_sc[...] = goff_p + nb_p
    carry_out_ref[...] = carry_sc[...]


def _merge(x, o_attn, y_ssm, carry_in, w, *, tile, route_tile):
    n = x.shape[0]
    nt = n // tile
    per_step = tile // route_tile
    full = lambda shape: pl.BlockSpec(shape, lambda i: (0,) * len(shape))
    return pl.pallas_call(
        _merge_kernel,
        grid=(nt,),
        in_specs=[pl.BlockSpec((tile, D_MODEL), lambda i: (i, 0)),
                  pl.BlockSpec((tile, D_ATTN), lambda i: (i, 0)),
                  pl.BlockSpec((tile, D_SSM), lambda i: (i, 0)),
                  full((SUBLANES, LANES)),
                  full((D_ATTN, D_MODEL)), full((D_SSM, 2 * D_MODEL)), full((D_MODEL, 2 * D_MODEL)),
                  full((1, 2 * D_MODEL)), full((D_MODEL, D_MODEL)),
                  full((1, D_MODEL)), full((1, D_MODEL)),
                  full((N_EXPERTS, D_MODEL)), full((N_EXPERTS, 1))],
        out_specs=[pl.BlockSpec((tile, D_MODEL), lambda i: (i, 0)),
                   pl.BlockSpec((TOP_K, tile), lambda i: (0, i)),
                   pl.BlockSpec((tile, 2 * TOP_K), lambda i: (i, 0)),
                   pl.BlockSpec((per_step, SUBLANES, LANES), lambda i: (i, 0, 0)),
                   full((SUBLANES, LANES))],
        out_shape=[jax.ShapeDtypeStruct((n, D_MODEL), F32),
                   jax.ShapeDtypeStruct((TOP_K, n), jnp.int32),
                   jax.ShapeDtypeStruct((n, 2 * TOP_K), F32),
                   jax.ShapeDtypeStruct((nt * per_step, SUBLANES, LANES), jnp.int32),
                   jax.ShapeDtypeStruct((SUBLANES, LANES), F32)],
        scratch_shapes=[pltpu.VMEM((SUBLANES, LANES), F32)],
        compiler_params=_cparams(("arbitrary",), VMEM_LIMIT_MERGE),
        name="merge",
    )(x, o_attn, y_ssm, carry_in, w["wao"], w["wso"], w["wg"], w["bg"], w["wo"], w["g1"], w["b1"],
      w["wrt"], w["brt"])


def _tab(tab_ref, tile, row, e):
    return tab_ref[(tile * TAB_ROWS + row) * N_EXPERTS + e]


BIG_PIECE = 4 * RUN_ROWS
MAX_UNITS_LOG2 = 8


def _for_each_run_piece(tab_ref, tile, fn):
    def per_expert(e, carry):
        loff = RUN_ROWS * _tab(tab_ref, tile, 1, e)
        goff = RUN_ROWS * _tab(tab_ref, tile, 2, e)
        units = _tab(tab_ref, tile, 0, e)
        n_big = lax.shift_right_logical(units, 2)

        def big(j, c2):
            fn(pl.multiple_of(loff + j * BIG_PIECE, RUN_ROWS), goff + j * BIG_PIECE, e, BIG_PIECE)
            return c2

        lax.fori_loop(0, n_big, big, 0)
        done = n_big * BIG_PIECE

        def small(j, c2):
            fn(pl.multiple_of(loff + done + j * RUN_ROWS, RUN_ROWS), goff + done + j * RUN_ROWS, e, RUN_ROWS)
            return c2

        lax.fori_loop(0, units & 3, small, 0)
        return carry

    lax.fori_loop(0, N_EXPERTS, per_expert, 0)


def _drain_units(units, wait_copy, buffer_rows):
    assert buffer_rows < (RUN_ROWS << MAX_UNITS_LOG2)
    for b in range(MAX_UNITS_LOG2):
        if (RUN_ROWS << b) > buffer_rows:
            break

        @pl.when((lax.shift_right_logical(units, b) & 1) == 1)
        def _():
            wait_copy(RUN_ROWS << b).wait()


def _dispatch_kernel(tab_ref, seg_ref, tot_ref, tail_ref, lpos_p_ref, xp_ref, lpos_s_ref, xs_in_ref, xs_ref,
                     loc_sc, zero_sc, sem, zsem):
    i = pl.program_id(0)
    last = pl.num_programs(0) - 1
    tile = i
    slot = i % 2
    loc = loc_sc.shape[1]

    @pl.when(i == 0)
    def _():
        zero_sc[...] = jnp.zeros_like(zero_sc)

        def tail_copy(e, j):
            row = pl.multiple_of(RUN_ROWS * (tail_ref[e] + j), RUN_ROWS)
            return pltpu.make_async_copy(zero_sc.at[pl.ds(0, RUN_ROWS)], xs_ref.at[pl.ds(row, RUN_ROWS)], zsem)

        def per_expert(e, carry):
            n = tail_ref[N_EXPERTS + e]
            lax.fori_loop(0, n, lambda j, c2: (tail_copy(e, j).start(), c2)[1], 0)
            lax.fori_loop(0, n, lambda j, c2: (tail_copy(e, j).wait(), c2)[1], 0)
            return carry

        lax.fori_loop(0, N_EXPERTS, per_expert, 0)

        def block_copy(b):
            row = pl.multiple_of(b * MOE_ROWS, MOE_ROWS)
            return pltpu.make_async_copy(zero_sc, xs_ref.at[pl.ds(row, MOE_ROWS)], zsem)

        first_unused, n_blocks = tail_ref[2 * N_EXPERTS], xs_ref.shape[0] // MOE_ROWS
        lax.fori_loop(first_unused, n_blocks, lambda b, c2: (block_copy(b).start(), c2)[1], 0)
        lax.fori_loop(first_unused, n_blocks, lambda b, c2: (block_copy(b).wait(), c2)[1], 0)

    def sort_tile(lpos_ref, x_ref):
        tm = x_ref.shape[0]
        rows = lax.broadcasted_iota(jnp.int32, (loc, tm), 0)
        lp = lpos_ref[...]
        onehot = jnp.zeros((loc, tm), F32)
        for k in range(TOP_K):
            onehot = jnp.where(rows == lp[k:k + 1], 1.0, onehot)
        loc_sc[slot] = jnp.dot(onehot.astype(BF16), x_ref[...].astype(BF16), preferred_element_type=F32)

    @pl.when(i < last)
    def _():
        sort_tile(lpos_p_ref, xp_ref)

    @pl.when(i == last)
    def _():
        sort_tile(lpos_s_ref, xs_in_ref)

    def piece_copy(sl, lrow, grow, e, n):
        dst = pl.multiple_of(seg_ref[e] + grow, RUN_ROWS)
        return pltpu.make_async_copy(loc_sc.at[sl, pl.ds(lrow, n)], xs_ref.at[pl.ds(dst, n)], sem.at[sl])

    _for_each_run_piece(tab_ref, tile, lambda lrow, grow, e, n: piece_copy(slot, lrow, grow, e, n).start())

    def drain(tl, sl):
        _drain_units(tot_ref[tl], lambda n: piece_copy(sl, 0, 0, 0, n), loc)

    @pl.when(i > 0)
    def _():
        drain(tile - 1, 1 - slot)

    @pl.when(i == last)
    def _():
        drain(tile, slot)


def _dispatch(tab, seg_start, tot, tails, lpos_p, x1_p, lpos_s, x1_s, *, tile, nrows):
    nt_p = x1_p.shape[0] // tile
    ns = x1_s.shape[0]
    loc = tile * TOP_K + N_EXPERTS * RUN_ROWS
    prompt_blk = lambda i, *_: jnp.minimum(i, nt_p - 1)
    return pl.pallas_call(
        _dispatch_kernel,
        grid_spec=pltpu.PrefetchScalarGridSpec(
            num_scalar_prefetch=4,
            grid=(nt_p + 1,),
            in_specs=[pl.BlockSpec((TOP_K, tile), lambda i, *_: (0, prompt_blk(i))),
                      pl.BlockSpec((tile, D_MODEL), lambda i, *_: (prompt_blk(i), 0)),
                      pl.BlockSpec((TOP_K, ns), lambda i, *_: (0, 0)),
                      pl.BlockSpec((ns, D_MODEL), lambda i, *_: (0, 0))],
            out_specs=pl.BlockSpec(memory_space=pl.ANY),
            scratch_shapes=[pltpu.VMEM((2, loc, D_MODEL), F32), pltpu.VMEM((MOE_ROWS, D_MODEL), F32),
                            pltpu.SemaphoreType.DMA((2,)), pltpu.SemaphoreType.DMA(())]),
        out_shape=jax.ShapeDtypeStruct((nrows, D_MODEL), F32),
        compiler_params=_cparams(("arbitrary",), VMEM_LIMIT),
        name="dispatch",
    )(tab, seg_start, tot, tails, lpos_p, x1_p, lpos_s, x1_s)


def _deinterleave_matrix():
    pm = np.zeros((MXU_DIM, MXU_DIM), np.float32)
    half = MXU_DIM // 2
    for c in range(half):
        pm[2 * c, c] = 1.0
        pm[2 * c + 1, half + c] = 1.0
    return pm


def _expert_kernel(be_ref, nu_ref, nv_ref, ord_ref, nxt_ref, xs_ref, w1_hbm, b1_ref, w2_hbm, b2_ref, pm_ref, y_ref,
                   w1f_sc, w2f_sc, w1p_sc, w2b_sc, sem):
    del nu_ref
    i = pl.program_id(0)
    e = be_ref[i]
    prev = be_ref[jnp.maximum(i - 1, 0)]
    nblk = 2 * D_FF // MXU_DIM

    def weight_copies(expert, slot):
        return (pltpu.make_async_copy(w1_hbm.at[expert], w1f_sc.at[slot], sem.at[0, slot]),
                pltpu.make_async_copy(w2_hbm.at[expert], w2f_sc.at[slot], sem.at[1, slot]))

    @pl.when(i == 0)
    def _():
        for cp in weight_copies(e, 0):
            cp.start()

    @pl.when((i == 0) | (e != prev))
    def _():
        slot = ord_ref[i] % 2
        for cp in weight_copies(e, slot):
            cp.wait()
        nxt = nxt_ref[i]

        @pl.when(nxt >= 0)
        def _():
            for cp in weight_copies(nxt, 1 - slot):
                cp.start()

        for cb in range(nblk):
            blk = w1f_sc[slot, :, cb * MXU_DIM:(cb + 1) * MXU_DIM].astype(BF16)
            w1p_sc[:, cb * MXU_DIM:(cb + 1) * MXU_DIM] = jnp.dot(
                blk, pm_ref[...], preferred_element_type=F32).astype(BF16)
        w2b_sc[...] = w2f_sc[slot].astype(BF16)

    for blk in range(MOE_STEP_BLOCKS):
        rows = slice(blk * MOE_ROWS, (blk + 1) * MOE_ROWS)

        @pl.when(blk < nv_ref[i])
        def _():
            x = xs_ref[rows, :].astype(BF16)
            h = jnp.dot(x, w1p_sc[...], preferred_element_type=F32) + b1_ref[0]
            half = MXU_DIM // 2
            acts = []
            for cb in range(nblk):
                x_glu = jnp.minimum(h[:, cb * MXU_DIM:cb * MXU_DIM + half], SWIGLU_LIMIT)
                x_lin = jnp.clip(h[:, cb * MXU_DIM + half:(cb + 1) * MXU_DIM], -SWIGLU_LIMIT, SWIGLU_LIMIT)
                acts.append((x_glu * jax.nn.sigmoid(SWIGLU_ALPHA * x_glu) * (x_lin + 1.0)).astype(BF16))
            act = jnp.concatenate(acts, axis=1)
            y_ref[rows, :] = jnp.dot(act, w2b_sc[...], preferred_element_type=F32) + b2_ref[0]

        @pl.when(blk >= nv_ref[i])
        def _():
            y_ref[rows, :] = jnp.zeros((MOE_ROWS, D_MODEL), F32)


def _experts(block_e, n_used, n_valid, run_ord, run_next, xs, w1, b1p, w2, b2, pm):
    nrows = xs.shape[0]
    step_rows = MOE_STEP_BLOCKS * MOE_ROWS
    nb = nrows // step_rows
    return pl.pallas_call(
        _expert_kernel,
        grid_spec=pltpu.PrefetchScalarGridSpec(
            num_scalar_prefetch=5,
            grid=(nb,),
            in_specs=[pl.BlockSpec((step_rows, D_MODEL), lambda i, be, nu, *_: (jnp.minimum(i, nu[0] - 1), 0)),
                      pl.BlockSpec(memory_space=pl.ANY),
                      pl.BlockSpec((1, 1, 2 * D_FF), lambda i, be, *_: (be[i], 0, 0)),
                      pl.BlockSpec(memory_space=pl.ANY),
                      pl.BlockSpec((1, 1, D_MODEL), lambda i, be, *_: (be[i], 0, 0)),
                      pl.BlockSpec((MXU_DIM, MXU_DIM), lambda i, *_: (0, 0))],
            out_specs=pl.BlockSpec((step_rows, D_MODEL), lambda i, *_: (i, 0)),
            scratch_shapes=[pltpu.VMEM((2, D_MODEL, 2 * D_FF), F32), pltpu.VMEM((2, D_FF, D_MODEL), F32),
                            pltpu.VMEM((D_MODEL, 2 * D_FF), BF16), pltpu.VMEM((D_FF, D_MODEL), BF16),
                            pltpu.SemaphoreType.DMA((2, 2))]),
        out_shape=jax.ShapeDtypeStruct((nrows, D_MODEL), F32),
        compiler_params=_cparams(("arbitrary",), VMEM_LIMIT_MERGE),
        name="experts",
    )(block_e, n_used, n_valid, run_ord, run_next, xs, w1, b1p, w2, b2, pm)


def _combine_kernel(tab_ref, seg_ref, tot_ref, cols_ref, x1_ref, g2_ref, b2_ref, ys_ref, y_ref, loc_sc, sem,
                    *, tile_base):
    i = pl.program_id(0)
    last = pl.num_programs(0) - 1
    tile = i + tile_base
    slot = i % 2
    loc, tm = loc_sc.shape[1], x1_ref.shape[0]

    def piece_copy(sl, lrow, grow, e, n):
        src = pl.multiple_of(seg_ref[e] + grow, RUN_ROWS)
        return pltpu.make_async_copy(ys_ref.at[pl.ds(src, n)], loc_sc.at[sl, pl.ds(lrow, n)], sem.at[sl])

    def gather(tl, sl):
        _for_each_run_piece(tab_ref, tl, lambda lrow, grow, e, n: piece_copy(sl, lrow, grow, e, n).start())

    @pl.when(i == 0)
    def _():
        loc_sc[...] = jnp.zeros_like(loc_sc)
        gather(tile, slot)

    @pl.when(i < last)
    def _():
        gather(tile + 1, 1 - slot)

    _drain_units(tot_ref[tile], lambda n: piece_copy(slot, 0, 0, 0, n), loc)

    cols = cols_ref[...]
    lane = lax.broadcasted_iota(jnp.int32, (tm, loc), 1)
    weights = jnp.zeros((tm, loc), F32)
    for k in range(TOP_K):
        weights = jnp.where(lane == cols[:, k:k + 1].astype(jnp.int32), cols[:, TOP_K + k:TOP_K + k + 1], weights)
    ffn = jnp.dot(weights.astype(BF16), loc_sc[slot].astype(BF16), preferred_element_type=F32)
    y_ref[...] = _layer_norm(DEEPNORM_ALPHA * x1_ref[...] + ffn, g2_ref[...], b2_ref[...])


def _combine(tab, seg_start, tot, cols, x1, g2, b2, ys, *, tile, tile_base):
    n = x1.shape[0]
    loc = tile * TOP_K + N_EXPERTS * RUN_ROWS
    return pl.pallas_call(
        functools.partial(_combine_kernel, tile_base=tile_base),
        grid_spec=pltpu.PrefetchScalarGridSpec(
            num_scalar_prefetch=3,
            grid=(n // tile,),
            in_specs=[pl.BlockSpec((tile, 2 * TOP_K), lambda i, *_: (i, 0)),
                      pl.BlockSpec((tile, D_MODEL), lambda i, *_: (i, 0)),
                      pl.BlockSpec((1, D_MODEL), lambda i, *_: (0, 0)),
                      pl.BlockSpec((1, D_MODEL), lambda i, *_: (0, 0)),
                      pl.BlockSpec(memory_space=pl.ANY)],
            out_specs=pl.BlockSpec((tile, D_MODEL), lambda i, *_: (i, 0)),
            scratch_shapes=[pltpu.VMEM((2, loc, D_MODEL), F32), pltpu.SemaphoreType.DMA((2,))]),
        out_shape=jax.ShapeDtypeStruct((n, D_MODEL), F32),
        compiler_params=_cparams(("arbitrary",), VMEM_LIMIT),
        name="combine",
    )(tab, seg_start, tot, cols, x1, g2, b2, ys)


def kernel(x_prompt, x_sample, cache_k_win, cache_v_win, state_ssm_re, state_ssm_im, w_in, b_in, attn_sinks,
           w_attn_out, ssm_a_re, ssm_a_im, ssm_log_dt, ssm_b_re, ssm_b_im, ssm_c_re, ssm_c_im, ssm_d, w_ssm_out,
           w_gate, b_gate, w_out, ln1_g, ln1_b, w_router, b_router, w_exp1, b_exp1, w_exp2, b_exp2, ln2_g, ln2_b):
    assert w_in.shape[0] == DEPTH == 1
    bsz, seq, _ = x_prompt.shape
    nsamp = x_sample.shape[0]
    assert x_sample.shape[1] == 1
    n_p = bsz * seq
    n_tok = n_p + nsamp

    xp = x_prompt.reshape(n_p, D_MODEL)
    xsm = x_sample.reshape(nsamp, D_MODEL)
    b_in2 = b_in[0].reshape(1, D_IN)
    sinks = attn_sinks[0].astype(F32)

    q_p, k_p, v_p, u_p = _proj(xp, w_in[0].astype(BF16), b_in2, tile=512, exact_f32=False, q_dtype=BF16)
    q_s, k_s, v_s, u_s = _proj(xsm, w_in[0], b_in2, tile=nsamp, exact_f32=True, q_dtype=F32)

    o_p = _attn_prompt(sinks, q_p.reshape(bsz, seq, D_ATTN), k_p.reshape(bsz, seq, D_KV),
                       v_p.reshape(bsz, seq, D_KV)).reshape(n_p, D_ATTN)
    k_buf = cache_k_win[0].reshape(nsamp, WINDOW, D_KV)
    v_buf = cache_v_win[0].reshape(nsamp, WINDOW, D_KV)
    o_s = _attn_sample(sinks, q_s, k_s, v_s, k_buf, v_buf)

    sp = _s5_params(ssm_a_re[0], ssm_a_im[0], ssm_log_dt[0], ssm_b_re[0], ssm_b_im[0], ssm_c_re[0], ssm_c_im[0])
    y_p, hp_re, hp_im = _s5_prompt(u_p, bsz, seq, _s5_chunk_mats(sp, ssm_d[0]))
    y_s, hs_re, hs_im = _s5_sample(u_s, state_ssm_re[0].reshape(nsamp, -1), state_ssm_im[0].reshape(nsamp, -1),
                                   _s5_sample_mats(sp, ssm_d[0]))

    wm = dict(wao=w_attn_out[0].astype(BF16), wso=w_ssm_out[0].astype(BF16), wg=w_gate[0].astype(BF16),
              bg=b_gate[0].reshape(1, -1), wo=w_out[0].astype(BF16), g1=ln1_g[0].reshape(1, -1),
              b1=ln1_b[0].reshape(1, -1), wrt=w_router[0].T, brt=b_router[0].reshape(-1, 1))
    carry0 = jnp.zeros((SUBLANES, LANES), F32)
    x1_p, lpos_p, cols_p, tab_p, carry1 = _merge(xp, o_p, y_p, carry0, wm, tile=MERGE_TILE, route_tile=TOK_TILE)
    x1_s, lpos_s, cols_s, tab_s, carry2 = _merge(xsm, o_s, y_s, carry1, wm, tile=nsamp, route_tile=nsamp)

    nt_p = n_p // TOK_TILE
    tab = jnp.concatenate([tab_p[:, :TAB_ROWS, :N_EXPERTS], tab_s[:, :TAB_ROWS, :N_EXPERTS]], axis=0)
    tot = jnp.sum(tab[:, 0, :], axis=1).astype(jnp.int32)
    tab = tab.reshape(-1)
    seg_rows = carry2[0, :N_EXPERTS].astype(jnp.int32) * RUN_ROWS
    step_rows = MOE_STEP_BLOCKS * MOE_ROWS
    padded = ((seg_rows + step_rows - 1) // step_rows) * step_rows
    pad_end = jnp.cumsum(padded)
    pad_start = (pad_end - padded).astype(jnp.int32)
    seg_end = pad_start + seg_rows
    n_runs = (nt_p + 1) * N_EXPERTS
    nb_max = (n_tok * TOP_K + n_runs * (RUN_ROWS - 1) + N_EXPERTS * (step_rows - 1) + step_rows - 1) // step_rows
    n_used = (pad_end[-1] // step_rows).astype(jnp.int32)
    tails = jnp.concatenate([seg_end // RUN_ROWS, (padded - seg_rows) // RUN_ROWS,
                             (pad_end[-1:] // MOE_ROWS)]).astype(jnp.int32)
    blk_start = jnp.arange(nb_max, dtype=jnp.int32) * step_rows
    blk_e = jnp.minimum(jnp.sum(blk_start[:, None] >= pad_end[None, :], axis=1), N_EXPERTS - 1).astype(jnp.int32)
    used = jnp.arange(nb_max) < n_used
    blk_e = jnp.where(used, blk_e, jnp.max(jnp.where(used, blk_e, 0)))
    ids = jnp.arange(N_EXPERTS, dtype=jnp.int32)
    of_blk = blk_e[:, None] == ids[None, :]
    n_valid = jnp.clip((jnp.sum(jnp.where(of_blk, seg_end[None, :], 0), axis=1) - blk_start + MOE_ROWS - 1)
                       // MOE_ROWS, 0, MOE_STEP_BLOCKS)
    n_valid = jnp.where(used, n_valid, 0).astype(jnp.int32)
    new_run = jnp.concatenate([jnp.ones((1,), jnp.int32), (blk_e[1:] != blk_e[:-1]).astype(jnp.int32)])
    run_ord = (jnp.cumsum(new_run) - 1).astype(jnp.int32)
    later = (ids[None, :] > ids[:, None]) & (padded > 0)[None, :]
    next_e = jnp.min(jnp.where(later, ids[None, :], N_EXPERTS), axis=1)
    next_e = jnp.where(next_e < N_EXPERTS, next_e, -1).astype(jnp.int32)
    run_next = jnp.sum(jnp.where(of_blk, next_e[None, :], 0), axis=1).astype(jnp.int32)

    nrows = nb_max * step_rows
    xs = _dispatch(tab, pad_start, tot, tails, lpos_p, x1_p, lpos_s, x1_s, tile=TOK_TILE, nrows=nrows)

    b1p = b_exp1[0].reshape(N_EXPERTS, 2 * D_FF // MXU_DIM, MXU_DIM // 2, 2)
    b1p = jnp.swapaxes(b1p, 2, 3).reshape(N_EXPERTS, 1, 2 * D_FF)
    ys = _experts(blk_e, n_used.reshape(1), n_valid, run_ord, run_next, xs, w_exp1[0], b1p, w_exp2[0],
                  b_exp2[0].reshape(N_EXPERTS, 1, D_MODEL),
                  jnp.asarray(_deinterleave_matrix(), BF16))

    g2, b2 = ln2_g[0].reshape(1, -1), ln2_b[0].reshape(1, -1)
    y_prompt = _combine(tab, pad_start, tot, cols_p, x1_p, g2, b2, ys, tile=TOK_TILE, tile_base=0)
    y_sample = _combine(tab, pad_start, tot, cols_s, x1_s, g2, b2, ys, tile=nsamp, tile_base=nt_p)

    k_p4 = k_p.reshape(bsz, seq, D_KV)[:, -WINDOW:].reshape(bsz, WINDOW, N_KV_HEADS, HEAD_DIM)
    v_p4 = v_p.reshape(bsz, seq, D_KV)[:, -WINDOW:].reshape(bsz, WINDOW, N_KV_HEADS, HEAD_DIM)
    k_s4 = jnp.concatenate([cache_k_win[0][:, 1:], k_s.reshape(nsamp, 1, N_KV_HEADS, HEAD_DIM)], axis=1)
    v_s4 = jnp.concatenate([cache_v_win[0][:, 1:], v_s.reshape(nsamp, 1, N_KV_HEADS, HEAD_DIM)], axis=1)
    st = lambda a, n: a.reshape(1, n, N_SSM_GROUPS, SSM_STATE)
    return (y_prompt.reshape(bsz, seq, D_MODEL), y_sample.reshape(nsamp, 1, D_MODEL),
            k_p4[None], v_p4[None], st(hp_re, bsz), st(hp_im, bsz),
            k_s4[None], v_s4[None], st(hs_re, nsamp), st(hs_im, nsamp))
```

```python
import functools
import math

import numpy as np
import jax
import jax.numpy as jnp
from jax import lax
from jax.experimental import pallas as pl
from jax.experimental.pallas import tpu as pltpu

F32 = jnp.float32
BF16 = jnp.bfloat16

D_MODEL = 1024
HEAD_DIM = 64
N_Q_HEADS = 8
N_KV_HEADS = 2
Q_PER_KV = N_Q_HEADS // N_KV_HEADS
D_ATTN = N_Q_HEADS * HEAD_DIM
D_KV = N_KV_HEADS * HEAD_DIM
WINDOW = 128
ATTN_SCALE = HEAD_DIM ** -0.5
SSM_GROUP = 16
D_SSM = D_MODEL // 2
N_SSM_GROUPS = D_SSM // SSM_GROUP
SSM_STATE = 64
D_IN = D_ATTN + 2 * D_KV + D_SSM
N_EXPERTS = 32
TOP_K = 4
D_FF = D_MODEL
SWIGLU_LIMIT = 7.0
SWIGLU_ALPHA = 1.702
LN_EPS = 1e-5
DEPTH = 1
DEEPNORM_ALPHA = (2 * DEPTH) ** 0.25

LANES = 128
SUBLANES = 8
MXU_DIM = 256

S5_CHUNK = MXU_DIM // SSM_GROUP
S5_LANE_GROUPS = LANES // SSM_GROUP
MOE_ROWS = 256
MOE_STEP_BLOCKS = 2
TOK_TILE = 256
MERGE_TILE = 512
VMEM_LIMIT = 48 * 1024 * 1024
VMEM_LIMIT_MERGE = 56 * 1024 * 1024


def _cparams(sem, vmem=None):
    return pltpu.CompilerParams(dimension_semantics=sem, vmem_limit_bytes=vmem)


def _proj_kernel(x_ref, w_ref, b_ref, q_ref, k_ref, v_ref, u_ref, *, exact_f32):
    if exact_f32:
        h = jnp.dot(x_ref[...], w_ref[...], preferred_element_type=F32, precision=lax.Precision.HIGHEST)
    else:
        h = jnp.dot(x_ref[...].astype(BF16), w_ref[...], preferred_element_type=F32)
    h = h + b_ref[...]
    q_ref[...] = (h[:, :D_ATTN] * ATTN_SCALE).astype(q_ref.dtype)
    k_ref[...] = h[:, D_ATTN:D_ATTN + D_KV]
    v_ref[...] = h[:, D_ATTN + D_KV:D_ATTN + 2 * D_KV]
    u_ref[...] = h[:, D_ATTN + 2 * D_KV:].astype(u_ref.dtype)


def _proj(x, w, b, *, tile, exact_f32, q_dtype):
    n = x.shape[0]
    return pl.pallas_call(
        functools.partial(_proj_kernel, exact_f32=exact_f32),
        grid=(n // tile,),
        in_specs=[pl.BlockSpec((tile, D_MODEL), lambda i: (i, 0)),
                  pl.BlockSpec((D_MODEL, D_IN), lambda i: (0, 0)),
                  pl.BlockSpec((1, D_IN), lambda i: (0, 0))],
        out_specs=[pl.BlockSpec((tile, D_ATTN), lambda i: (i, 0)),
                   pl.BlockSpec((tile, D_KV), lambda i: (i, 0)),
                   pl.BlockSpec((tile, D_KV), lambda i: (i, 0)),
                   pl.BlockSpec((tile, D_SSM), lambda i: (i, 0))],
        out_shape=[jax.ShapeDtypeStruct((n, D_ATTN), q_dtype),
                   jax.ShapeDtypeStruct((n, D_KV), F32),
                   jax.ShapeDtypeStruct((n, D_KV), F32),
                   jax.ShapeDtypeStruct((n, D_SSM), F32)],
        compiler_params=_cparams(("parallel",)),
        name="proj",
    )(x, w, b)


ATT_Q_TILE = 512


def _attn_prompt_kernel(sink_ref, q_ref, k_ref, v_ref, o_ref):
    i = pl.program_id(1)
    nk, nq = 2 * WINDOW, 2 * WINDOW
    lo = lax.broadcasted_iota(jnp.int32, (nk, LANES), 1) < HEAD_DIM
    top = lax.broadcasted_iota(jnp.int32, (nq, 1), 0) < WINDOW
    for blk in range(ATT_Q_TILE // WINDOW):
        q0 = i * ATT_Q_TILE + blk * WINDOW
        k0 = pl.multiple_of(jnp.maximum(q0 - WINDOW, 0), WINDOW)
        kk = k_ref[0, pl.ds(k0, nk), :]
        vv = v_ref[0, pl.ds(k0, nk), :]
        kk_sw = pltpu.roll(kk, HEAD_DIM, axis=1)
        vv_sw = pltpu.roll(vv, HEAD_DIM, axis=1)
        k_var = [[jnp.where(lo, kk, 0.0).astype(BF16), jnp.where(lo, 0.0, kk_sw).astype(BF16)],
                 [jnp.where(lo, kk_sw, 0.0).astype(BF16), jnp.where(lo, 0.0, kk).astype(BF16)]]
        v_var = [[jnp.where(lo, vv, 1.0).astype(BF16), jnp.where(lo, 1.0, vv_sw).astype(BF16)],
                 [jnp.where(lo, vv_sw, 1.0).astype(BF16), jnp.where(lo, 1.0, vv).astype(BF16)]]
        qpos = q0 + lax.broadcasted_iota(jnp.int32, (nq, nk), 0) % WINDOW
        kpos = k0 + lax.broadcasted_iota(jnp.int32, (nq, nk), 1)
        valid = (kpos <= qpos) & (qpos - kpos <= WINDOW)
        rows = slice(blk * WINDOW, (blk + 1) * WINDOW)
        for kv in range(N_KV_HEADS):
            pairs = (2 * kv, 2 * kv + 1)
            qs = jnp.concatenate([q_ref[0, rows, pr * LANES:(pr + 1) * LANES] for pr in pairs], axis=0)
            outs = []
            for parity in range(2):
                sink = jnp.where(top, sink_ref[2 * pairs[0] + parity], sink_ref[2 * pairs[1] + parity])
                s = lax.dot_general(qs, k_var[kv][parity], (((1,), (1,)), ((), ())), preferred_element_type=F32)
                s = jnp.where(valid, s, -jnp.inf)
                m = jnp.maximum(jnp.max(s, axis=-1, keepdims=True), sink)
                p = jnp.exp(s - m).astype(BF16)
                acc = jnp.dot(p, v_var[kv][parity], preferred_element_type=F32)
                outs.append(acc / (pltpu.roll(acc, HEAD_DIM, axis=1) + jnp.exp(sink - m)))
            o = jnp.where(lo, outs[0], outs[1]).astype(o_ref.dtype)
            for j, pr in enumerate(pairs):
                o_ref[0, rows, pr * LANES:(pr + 1) * LANES] = o[j * WINDOW:(j + 1) * WINDOW]


def _attn_prompt(sinks, q, k, v):
    bsz, seq = q.shape[0], q.shape[1]
    return pl.pallas_call(
        _attn_prompt_kernel,
        grid=(bsz, seq // ATT_Q_TILE),
        in_specs=[pl.BlockSpec(memory_space=pltpu.SMEM),
                  pl.BlockSpec((1, ATT_Q_TILE, D_ATTN), lambda b, i: (b, i, 0)),
                  pl.BlockSpec((1, seq, D_KV), lambda b, i: (b, 0, 0)),
                  pl.BlockSpec((1, seq, D_KV), lambda b, i: (b, 0, 0))],
        out_specs=pl.BlockSpec((1, ATT_Q_TILE, D_ATTN), lambda b, i: (b, i, 0)),
        out_shape=jax.ShapeDtypeStruct((bsz, seq, D_ATTN), BF16),
        compiler_params=_cparams(("parallel", "parallel")),
        name="attn_prompt",
    )(sinks, q, k, v)


ATT_S_GROUP = 8


def _attn_sample_kernel(sink_ref, q_ref, kn_ref, vn_ref, kb_ref, vb_ref, o_ref):
    g = ATT_S_GROUP
    rows = Q_PER_KV * g
    ncol = g * WINDOW
    kb = kb_ref[...].reshape(ncol, D_KV).astype(BF16)
    vb = vb_ref[...].reshape(ncol, D_KV).astype(BF16)
    rseq = lax.broadcasted_iota(jnp.int32, (rows, ncol), 0) % g
    cseq = lax.broadcasted_iota(jnp.int32, (rows, ncol), 1) // WINDOW
    own = rseq == cseq
    rhead = lax.broadcasted_iota(jnp.int32, (rows, 1), 0) // g
    for kv in range(N_KV_HEADS):
        lo = kv * HEAD_DIM
        qs = jnp.concatenate(
            [q_ref[:, (kv * Q_PER_KV + h) * HEAD_DIM:(kv * Q_PER_KV + h + 1) * HEAD_DIM] for h in range(Q_PER_KV)],
            axis=0)
        kn = jnp.concatenate([kn_ref[:, lo:lo + HEAD_DIM]] * Q_PER_KV, axis=0)
        vn = jnp.concatenate([vn_ref[:, lo:lo + HEAD_DIM]] * Q_PER_KV, axis=0)
        sink = jnp.zeros((rows, 1), F32)
        for h in range(Q_PER_KV):
            sink = jnp.where(rhead == h, sink_ref[kv * Q_PER_KV + h], sink)
        qs = qs.astype(BF16)
        s = lax.dot_general(qs, kb[:, lo:lo + HEAD_DIM], (((1,), (1,)), ((), ())), preferred_element_type=F32)
        s = jnp.where(own, s, -jnp.inf)
        s_new = jnp.sum(qs.astype(F32) * kn.astype(BF16).astype(F32), axis=-1, keepdims=True)
        m = jnp.maximum(jnp.maximum(jnp.max(s, axis=-1, keepdims=True), s_new), sink)
        p = jnp.exp(s - m)
        p_new = jnp.exp(s_new - m)
        denom = jnp.sum(p, axis=-1, keepdims=True) + p_new + jnp.exp(sink - m)
        o = jnp.dot(p.astype(BF16), vb[:, lo:lo + HEAD_DIM], preferred_element_type=F32)
        o = (o + p_new.astype(BF16).astype(F32) * vn.astype(BF16).astype(F32)) / denom
        for h in range(Q_PER_KV):
            c0 = (kv * Q_PER_KV + h) * HEAD_DIM
            o_ref[:, c0:c0 + HEAD_DIM] = o[h * g:(h + 1) * g].astype(o_ref.dtype)


def _attn_sample(sinks, q, k_new, v_new, k_buf, v_buf):
    n = q.shape[0]
    g = ATT_S_GROUP
    return pl.pallas_call(
        _attn_sample_kernel,
        grid=(n // g,),
        in_specs=[pl.BlockSpec(memory_space=pltpu.SMEM),
                  pl.BlockSpec((g, D_ATTN), lambda i: (i, 0)),
                  pl.BlockSpec((g, D_KV), lambda i: (i, 0)),
                  pl.BlockSpec((g, D_KV), lambda i: (i, 0)),
                  pl.BlockSpec((g, WINDOW, D_KV), lambda i: (i, 0, 0)),
                  pl.BlockSpec((g, WINDOW, D_KV), lambda i: (i, 0, 0))],
        out_specs=pl.BlockSpec((g, D_ATTN), lambda i: (i, 0)),
        out_shape=jax.ShapeDtypeStruct((n, D_ATTN), F32),
        compiler_params=_cparams(("parallel",)),
        name="attn_sample",
    )(sinks, q, k_new, v_new, k_buf, v_buf)


def _s5_params(a_re, a_im, log_dt, b_re, b_im, c_re, c_im):
    hp = lax.Precision.HIGHEST
    dt = jnp.exp(log_dt.astype(F32))[:, None]
    are, aim = a_re.astype(F32), a_im.astype(F32)
    tau = jnp.arange(S5_CHUNK + 1, dtype=F32)[None, :, None]
    mag = jnp.exp(tau * (dt * are)[:, None, :])
    ang = tau * (dt * aim)[:, None, :]
    pw_re, pw_im = mag * jnp.cos(ang), mag * jnp.sin(ang)
    ab_re, ab_im = pw_re[:, 1], pw_im[:, 1]
    den = are * are + aim * aim
    f_re = ((ab_re - 1.0) * are + ab_im * aim) / den
    f_im = (ab_im * are - (ab_re - 1.0) * aim) / den
    bre, bim = b_re.astype(F32), b_im.astype(F32)
    bb_re = f_re[..., None] * bre - f_im[..., None] * bim
    bb_im = f_re[..., None] * bim + f_im[..., None] * bre
    cre, cim = c_re.astype(F32), c_im.astype(F32)
    return dict(pw_re=pw_re, pw_im=pw_im, ab_re=ab_re, ab_im=ab_im, bb_re=bb_re, bb_im=bb_im,
                c_re=cre, c_im=cim, hp=hp)


def _s5_chunk_mats(sp, d_skip):
    hp = sp["hp"]
    g, t, c, p = N_SSM_GROUPS, S5_CHUNK, SSM_GROUP, SSM_STATE
    pw_re, pw_im = sp["pw_re"], sp["pw_im"]
    ca_re = sp["c_re"][:, None] * pw_re[:, :, None, :] - sp["c_im"][:, None] * pw_im[:, :, None, :]
    ca_im = sp["c_re"][:, None] * pw_im[:, :, None, :] + sp["c_im"][:, None] * pw_re[:, :, None, :]
    kern = (jnp.einsum("gtcp,gpd->gtcd", ca_re[:, :t], sp["bb_re"], precision=hp)
            - jnp.einsum("gtcp,gpd->gtcd", ca_im[:, :t], sp["bb_im"], precision=hp))
    kc = jnp.swapaxes(kern, 2, 3)
    kc = kc.at[:, 0].add(d_skip.astype(F32).reshape(g, 1, c) * jnp.eye(c, dtype=F32)[None])
    rev_re, rev_im = pw_re[:, t - 1::-1][:, :t], pw_im[:, t - 1::-1][:, :t]
    wst_re = rev_re[:, :, None, :] * jnp.swapaxes(sp["bb_re"], 1, 2)[:, None] \
        - rev_im[:, :, None, :] * jnp.swapaxes(sp["bb_im"], 1, 2)[:, None]
    wst_im = rev_re[:, :, None, :] * jnp.swapaxes(sp["bb_im"], 1, 2)[:, None] \
        + rev_im[:, :, None, :] * jnp.swapaxes(sp["bb_re"], 1, 2)[:, None]
    wo_re = jnp.transpose(ca_re[:, 1:t + 1], (0, 3, 1, 2))
    wo_im = -jnp.transpose(ca_im[:, 1:t + 1], (0, 3, 1, 2))
    nv, gl = g // S5_LANE_GROUPS, S5_LANE_GROUPS
    kc, wst_re, wst_im, wo_re, wo_im = lax.optimization_barrier((kc, wst_re, wst_im, wo_re, wo_im))
    kc5 =jnp.transpose(kc.reshape(nv, gl, t, c, c), (0, 2, 1, 3, 4))
    ws6 = jnp.transpose(jnp.stack([wst_re, wst_im], axis=3).reshape(nv, gl, t, c, 2, p),
                        (0, 2, 1, 3, 4, 5))
    wo6 = jnp.transpose(jnp.stack([wo_re, wo_im], axis=0).reshape(2, nv, gl, p, t, c),
                        (1, 0, 2, 3, 4, 5))
    kc5, ws6, wo6 = lax.optimization_barrier((kc5.astype(BF16), ws6.astype(BF16), wo6.astype(BF16)))
    spread_b = np.zeros((c, LANES), np.float32)
    spread_s = np.zeros((2 * p, 2 * gl * p), np.float32)
    spread_o = np.zeros((t * c, t * LANES), np.float32)
    for h in range(gl):
        spread_b[np.arange(c), h * c + np.arange(c)] = 1.0
        for ri in range(2):
            spread_s[ri * p + np.arange(p), ri * gl * p + h * p + np.arange(p)] = 1.0
        for tt in range(t):
            spread_o[tt * c + np.arange(c), tt * LANES + h * c + np.arange(c)] = 1.0
    at_re = pw_re[:, t].reshape(1, g * p)
    at_im = pw_im[:, t].reshape(1, g * p)
    return dict(kc=kc5.reshape(nv, t, LANES, c), ws=ws6.reshape(nv, t * LANES, 2 * p),
                wo=wo6.reshape(nv, 2 * gl * p, t * c), spread_b=jnp.asarray(spread_b, BF16),
                spread_s=jnp.asarray(spread_s, BF16), spread_o=jnp.asarray(spread_o, BF16),
                at_re=at_re, at_im=at_im)


def _s5_chunk_rows(u_ref, nchunk):
    return jnp.concatenate(
        [u_ref[pl.ds(s, nchunk, stride=S5_CHUNK), :] for s in range(S5_CHUNK)], axis=1).astype(BF16)


S5_SLABS = S5_LANE_GROUPS * SSM_STATE // LANES


S5_EXPAND_ROWS = 256
S5_C_SHIFT = SSM_GROUP.bit_length() - 1
S5_P_SHIFT = SSM_STATE.bit_length() - 1


def _s5_expand(dst_ref, compact_ref, spread_ref, row_shift, col_shift):
    n_rows, n_cols = dst_ref.shape
    col_g = lax.shift_right_logical(lax.broadcasted_iota(jnp.int32, (S5_EXPAND_ROWS, n_cols), 1), col_shift)
    for r0 in range(0, n_rows, S5_EXPAND_ROWS):
        row_g = lax.shift_right_logical(r0 + lax.broadcasted_iota(jnp.int32, (S5_EXPAND_ROWS, n_cols), 0), row_shift)
        same = ((row_g ^ col_g) & (S5_LANE_GROUPS - 1)) == 0
        blk = jnp.dot(compact_ref[r0:r0 + S5_EXPAND_ROWS, :], spread_ref[...], preferred_element_type=F32)
        dst_ref[r0:r0 + S5_EXPAND_ROWS, :] = jnp.where(same, blk, 0.0).astype(dst_ref.dtype)


def _s5_state_kernel(u_ref, ws_ref, spread_ref, sre_ref, sim_ref, wst_sc):
    nchunk = sre_ref.shape[1]

    @pl.when(pl.program_id(1) == 0)
    def _():
        _s5_expand(wst_sc, ws_ref.at[0], spread_ref, S5_C_SHIFT, S5_P_SHIFT)

    s = jnp.dot(_s5_chunk_rows(u_ref, nchunk), wst_sc[...], preferred_element_type=F32)
    for k in range(S5_SLABS):
        sre_ref[k] = s[:, k * LANES:(k + 1) * LANES]
        sim_ref[k] = s[:, (S5_SLABS + k) * LANES:(S5_SLABS + k + 1) * LANES]


def _s5_scan_kernel(sre_ref, sim_ref, are_ref, aim_ref, hre_ref, him_ref, fre_ref, fim_ref, *, bsz):
    nchunk = sre_ref.shape[1] // bsz
    are = [jnp.broadcast_to(are_ref[:, k * LANES:(k + 1) * LANES], (bsz, LANES)) for k in range(S5_SLABS)]
    aim = [jnp.broadcast_to(aim_ref[:, k * LANES:(k + 1) * LANES], (bsz, LANES)) for k in range(S5_SLABS)]

    def body(j, carry):
        rows = pl.ds(j, bsz, stride=nchunk)
        out = []
        for k in range(S5_SLABS):
            cre, cim = carry[2 * k], carry[2 * k + 1]
            hre_ref[k, rows, :] = cre
            him_ref[k, rows, :] = cim
            sr = sre_ref[k, rows, :]
            si = sim_ref[k, rows, :]
            out += [are[k] * cre - aim[k] * cim + sr, are[k] * cim + aim[k] * cre + si]
        return tuple(out)

    zero = jnp.zeros((bsz, LANES), F32)
    fin = lax.fori_loop(0, nchunk, body, (zero,) * (2 * S5_SLABS))
    fre_ref[...] = jnp.concatenate(fin[0::2], axis=1)
    fim_ref[...] = jnp.concatenate(fin[1::2], axis=1)


def _s5_out_kernel(u_ref, kc_ref, spread_b_ref, hre_ref, him_ref, wo_ref, spread_o_ref, y_ref, m_sc, wout_sc):
    nchunk = hre_ref.shape[1]

    @pl.when(pl.program_id(1) == 0)
    def _():
        rg = lax.shift_right_logical(lax.broadcasted_iota(jnp.int32, (LANES, LANES), 0), S5_C_SHIFT)
        cg = lax.shift_right_logical(lax.broadcasted_iota(jnp.int32, (LANES, LANES), 1), S5_C_SHIFT)
        zero_blk = jnp.zeros((LANES, LANES), BF16)
        lag_blk = [jnp.where(rg == cg, jnp.dot(kc_ref[0, tau], spread_b_ref[...], preferred_element_type=F32),
                             0.0).astype(BF16) for tau in range(S5_CHUNK)]
        for s in range(S5_CHUNK):
            for t in range(S5_CHUNK):
                m_sc[s * LANES:(s + 1) * LANES, t * LANES:(t + 1) * LANES] = lag_blk[t - s] if t >= s else zero_blk
        _s5_expand(wout_sc, wo_ref.at[0], spread_o_ref, S5_P_SHIFT, S5_C_SHIFT)

    hcat = jnp.concatenate([hre_ref[k] for k in range(S5_SLABS)] + [him_ref[k] for k in range(S5_SLABS)],
                           axis=1).astype(BF16)
    lhs = _s5_chunk_rows(u_ref, nchunk)
    y = jnp.concatenate(
        [jnp.dot(lhs[:, :j + MXU_DIM], m_sc[:j + MXU_DIM, j:j + MXU_DIM], preferred_element_type=F32)
         for j in range(0, S5_CHUNK * LANES, MXU_DIM)], axis=1)
    y = y + jnp.dot(hcat, wout_sc[...], preferred_element_type=F32)
    for s in range(S5_CHUNK):
        y_ref[pl.ds(s, nchunk, stride=S5_CHUNK), :] = y[:, s * LANES:(s + 1) * LANES]


def _s5_prompt(u, bsz, seq, mats):
    at_re, at_im = mats["at_re"], mats["at_im"]
    g, t, p, c = N_SSM_GROUPS, S5_CHUNK, SSM_STATE, SSM_GROUP
    nchunk = seq // t
    n = nchunk * bsz
    nv = g // S5_LANE_GROUPS
    half = S5_LANE_GROUPS * p
    s_re, s_im = pl.pallas_call(
        _s5_state_kernel,
        grid=(nv, bsz),
        in_specs=[pl.BlockSpec((seq, LANES), lambda v, b: (b, v)),
                  pl.BlockSpec((1, t * LANES, 2 * p), lambda v, b: (v, 0, 0)),
                  pl.BlockSpec((2 * p, 2 * half), lambda v, b: (0, 0))],
        out_specs=[pl.BlockSpec((S5_SLABS, nchunk, LANES), lambda v, b: (v, b, 0)),
                   pl.BlockSpec((S5_SLABS, nchunk, LANES), lambda v, b: (v, b, 0))],
        out_shape=[jax.ShapeDtypeStruct((nv * S5_SLABS, n, LANES), F32)] * 2,
        scratch_shapes=[pltpu.VMEM((t * LANES, 2 * half), BF16)],
        compiler_params=_cparams(("parallel", "arbitrary"), VMEM_LIMIT),
        name="s5_state",
    )(u, mats["ws"], mats["spread_s"])
    h_re, h_im, f_re, f_im = pl.pallas_call(
        functools.partial(_s5_scan_kernel, bsz=bsz),
        grid=(nv,),
        in_specs=[pl.BlockSpec((S5_SLABS, n, LANES), lambda i: (i, 0, 0)),
                  pl.BlockSpec((S5_SLABS, n, LANES), lambda i: (i, 0, 0)),
                  pl.BlockSpec((1, half), lambda i: (0, i)),
                  pl.BlockSpec((1, half), lambda i: (0, i))],
        out_specs=[pl.BlockSpec((S5_SLABS, n, LANES), lambda i: (i, 0, 0)),
                   pl.BlockSpec((S5_SLABS, n, LANES), lambda i: (i, 0, 0)),
                   pl.BlockSpec((bsz, half), lambda i: (0, i)),
                   pl.BlockSpec((bsz, half), lambda i: (0, i))],
        out_shape=[jax.ShapeDtypeStruct((nv * S5_SLABS, n, LANES), F32)] * 2
        + [jax.ShapeDtypeStruct((bsz, g * p), F32)] * 2,
        compiler_params=_cparams(("parallel",)),
        name="s5_scan",
    )(s_re, s_im, at_re, at_im)
    y = pl.pallas_call(
        _s5_out_kernel,
        grid=(nv, bsz),
        in_specs=[pl.BlockSpec((seq, LANES), lambda v, b: (b, v)),
                  pl.BlockSpec((1, t, LANES, c), lambda v, b: (v, 0, 0, 0)),
                  pl.BlockSpec((c, LANES), lambda v, b: (0, 0)),
                  pl.BlockSpec((S5_SLABS, nchunk, LANES), lambda v, b: (v, b, 0)),
                  pl.BlockSpec((S5_SLABS, nchunk, LANES), lambda v, b: (v, b, 0)),
                  pl.BlockSpec((1, 2 * half, t * c), lambda v, b: (v, 0, 0)),
                  pl.BlockSpec((t * c, t * LANES), lambda v, b: (0, 0))],
        out_specs=pl.BlockSpec((seq, LANES), lambda v, b: (b, v)),
        out_shape=jax.ShapeDtypeStruct((bsz * seq, D_SSM), F32),
        scratch_shapes=[pltpu.VMEM((t * LANES, t * LANES), BF16), pltpu.VMEM((2 * half, t * LANES), BF16)],
        compiler_params=_cparams(("parallel", "arbitrary"), VMEM_LIMIT),
        name="s5_out",
    )(u, mats["kc"], mats["spread_b"], h_re, h_im, mats["wo"], mats["spread_o"])
    return y, f_re, f_im


S5S_GROUPS = LANES // SSM_GROUP


def _s5_sample_mats(sp, d_skip):
    go, gl, c, p = N_SSM_GROUPS // S5S_GROUPS, S5S_GROUPS, SSM_GROUP, SSM_STATE
    eye = jnp.eye(gl, dtype=F32)

    def bdiag_in(b):
        b4 = b.reshape(go, gl, p, c)
        return jnp.einsum("ogpc,gh->ogchp", b4, eye).reshape(go, gl * c, gl * p)

    def bdiag_out(cm):
        c4 = cm.reshape(go, gl, c, p)
        return jnp.einsum("ogcp,gh->ogphc", c4, eye).reshape(go, gl * p, gl * c)

    b8 = jnp.concatenate([bdiag_in(sp["bb_re"]), bdiag_in(sp["bb_im"])], axis=2)
    c8 = jnp.concatenate([bdiag_out(sp["c_re"]), -bdiag_out(sp["c_im"])], axis=1)
    a_re = sp["ab_re"].reshape(1, N_SSM_GROUPS * p)
    a_im = sp["ab_im"].reshape(1, N_SSM_GROUPS * p)
    return b8, c8, a_re, a_im, d_skip.astype(F32).reshape(1, D_SSM)


def _s5_sample_kernel(u_ref, hre_ref, him_ref, b8_ref, c8_ref, are_ref, aim_ref, d_ref,
                      y_ref, ore_ref, oim_ref):
    hp = lax.Precision.HIGHEST
    u = u_ref[...]
    half = S5S_GROUPS * SSM_STATE
    bu = jnp.dot(u, b8_ref[0], preferred_element_type=F32, precision=hp)
    are, aim = are_ref[...], aim_ref[...]
    h0r, h0i = hre_ref[...], him_ref[...]
    hr = are * h0r - aim * h0i + bu[:, :half]
    hi = are * h0i + aim * h0r + bu[:, half:]
    ore_ref[...] = hr
    oim_ref[...] = hi
    y = jnp.dot(jnp.concatenate([hr, hi], axis=1), c8_ref[0], preferred_element_type=F32, precision=hp)
    y_ref[...] = (y + d_ref[...] * u).astype(y_ref.dtype)


def _s5_sample(u, h0_re, h0_im, mats):
    b8, c8, a_re, a_im, d = mats
    n = u.shape[0]
    half = S5S_GROUPS * SSM_STATE
    return pl.pallas_call(
        _s5_sample_kernel,
        grid=(N_SSM_GROUPS // S5S_GROUPS,),
        in_specs=[pl.BlockSpec((n, LANES), lambda i: (0, i)),
                  pl.BlockSpec((n, half), lambda i: (0, i)),
                  pl.BlockSpec((n, half), lambda i: (0, i)),
                  pl.BlockSpec((1, LANES, 2 * half), lambda i: (i, 0, 0)),
                  pl.BlockSpec((1, 2 * half, LANES), lambda i: (i, 0, 0)),
                  pl.BlockSpec((1, half), lambda i: (0, i)),
                  pl.BlockSpec((1, half), lambda i: (0, i)),
                  pl.BlockSpec((1, LANES), lambda i: (0, i))],
        out_specs=[pl.BlockSpec((n, LANES), lambda i: (0, i)),
                   pl.BlockSpec((n, half), lambda i: (0, i)),
                   pl.BlockSpec((n, half), lambda i: (0, i))],
        out_shape=[jax.ShapeDtypeStruct((n, D_SSM), BF16),
                   jax.ShapeDtypeStruct((n, N_SSM_GROUPS * SSM_STATE), F32),
                   jax.ShapeDtypeStruct((n, N_SSM_GROUPS * SSM_STATE), F32)],
        compiler_params=_cparams(("parallel",)),
        name="s5_sample",
    )(u, h0_re, h0_im, b8, c8, a_re, a_im, d)


def _layer_norm(x, g, b):
    mu = jnp.mean(x, axis=-1, keepdims=True)
    xc = x - mu
    var = jnp.mean(xc * xc, axis=-1, keepdims=True)
    return xc * lax.rsqrt(var + LN_EPS) * g + b


RUN_ROWS = SUBLANES
TAB_ROWS = 3


def _merge_kernel(x_ref, oa_ref, ys_ref, carry_in_ref, wao_ref, wso_ref, wg_ref, bg_ref, wo_ref,
                  g1_ref, b1_ref, wrt_ref, brt_ref,
                  x1_ref, lpos_ref, cols_ref, tab_ref, carry_out_ref, carry_sc):
    step = pl.program_id(0)

    @pl.when(step == 0)
    def _():
        carry_sc[...] = carry_in_ref[...]

    tm = x_ref.shape[0]
    x = x_ref[...]
    branch_a = jnp.dot(oa_ref[...].astype(BF16), wao_ref[...], preferred_element_type=F32)
    z = jnp.dot(jax.nn.gelu(ys_ref[...].astype(F32)).astype(BF16), wso_ref[...], preferred_element_type=F32)
    branch_b = z[:, :D_MODEL] * jax.nn.sigmoid(z[:, D_MODEL:])
    gates = jax.nn.sigmoid(jnp.dot(x.astype(BF16), wg_ref[...], preferred_element_type=F32) + bg_ref[...])
    mixed = gates[:, :D_MODEL] * branch_a + gates[:, D_MODEL:] * branch_b
    mix = jnp.dot(mixed.astype(BF16), wo_ref[...], preferred_element_type=F32)
    x1 = _layer_norm(DEEPNORM_ALPHA * x + mix, g1_ref[...], b1_ref[...])
    x1_ref[...] = x1

    def split2(v):
        hi = v.astype(BF16)
        return hi, (v - hi.astype(F32)).astype(BF16)

    def dot_nt(a, b):
        return lax.dot_general(a, b, (((1,), (1,)), ((), ())), preferred_element_type=F32)

    w_hi, w_lo = split2(wrt_ref[...])
    rt = tm // tab_ref.shape[0]
    sub = lax.broadcasted_iota(jnp.int32, (N_EXPERTS, rt), 0)
    r = lax.broadcasted_iota(jnp.int32, (rt, rt), 0)
    c = lax.broadcasted_iota(jnp.int32, (rt, rt), 1)
    er = lax.broadcasted_iota(jnp.int32, (N_EXPERTS, N_EXPERTS), 0)
    ec = lax.broadcasted_iota(jnp.int32, (N_EXPERTS, N_EXPERTS), 1)
    rid = lax.broadcasted_iota(jnp.int32, (SUBLANES, LANES), 0)
    lane_pad = jnp.zeros((SUBLANES, LANES - N_EXPERTS), F32)
    for h in range(tab_ref.shape[0]):
        x_hi, x_lo = split2(x1[h * rt:(h + 1) * rt])
        logits = dot_nt(w_hi, x_hi) + dot_nt(w_hi, x_lo) + dot_nt(w_lo, x_hi) + brt_ref[...]
        work = logits
        vals, sels = [], []
        for _ in range(TOP_K):
            mx = jnp.max(work, axis=0, keepdims=True)
            idx = jnp.min(jnp.where(work == mx, sub, N_EXPERTS), axis=0, keepdims=True)
            sel = sub == idx
            vals.append(mx)
            sels.append(sel)
            work = jnp.where(sel, -jnp.inf, work)
        ex = [jnp.exp(v - vals[0]) for v in vals]
        tot = ex[0] + ex[1] + ex[2] + ex[3]
        gate_rows = jnp.concatenate([e / tot for e in ex], axis=0)

        multi = jnp.zeros((N_EXPERTS, rt), F32)
        for sel in sels:
            multi = multi + jnp.where(sel, 1.0, 0.0)
        multi_b = multi.astype(BF16)
        earlier = jnp.dot(multi_b, jnp.where(r < c, 1.0, 0.0).astype(BF16), preferred_element_type=F32)
        cnt_col = jnp.sum(multi, axis=1, keepdims=True)
        nb_col = jnp.floor((cnt_col + (RUN_ROWS - 1.0)) * (1.0 / RUN_ROWS))
        loff_col = jnp.dot(jnp.where(ec < er, 1.0, 0.0).astype(BF16),
                           jnp.broadcast_to(nb_col, (N_EXPERTS, rt)).astype(BF16), preferred_element_type=F32)
        base = RUN_ROWS * loff_col + earlier
        lpos = jnp.concatenate([jnp.sum(jnp.where(sel, base, 0.0), axis=0, keepdims=True) for sel in sels],
                               axis=0)
        lpos_ref[:, h * rt:(h + 1) * rt] = lpos.astype(jnp.int32)
        rows_hi, rows_lo = split2(jnp.concatenate([lpos, gate_rows], axis=0))
        eye = jnp.where(r == c, 1.0, 0.0).astype(BF16)
        cols_ref[h * rt:(h + 1) * rt, :] = dot_nt(eye, rows_hi) + dot_nt(eye, rows_lo)

        cnt_row = dot_nt(jnp.ones((SUBLANES, rt), BF16), multi_b)
        nb_row = jnp.floor((cnt_row + (RUN_ROWS - 1.0)) * (1.0 / RUN_ROWS))
        loff_row = jnp.dot(nb_row.astype(BF16), jnp.where(er < ec, 1.0, 0.0).astype(BF16),
                           preferred_element_type=F32)
        nb_p = jnp.concatenate([nb_row, lane_pad], axis=1)
        loff_p = jnp.concatenate([loff_row, lane_pad], axis=1)
        goff_p = carry_sc[...]
        tab = jnp.where(rid == 0, nb_p, jnp.where(rid == 1, loff_p, jnp.where(rid == 2, goff_p, 0.0)))
        tab_ref[h] = tab.astype(jnp.int32)
        carry_sc[...] = goff_p + nb_p
    carry_out_ref[...] = carry_sc[...]


def _merge(x, o_attn, y_ssm, carry_in, w, *, tile, route_tile):
    n = x.shape[0]
    nt = n // tile
    per_step = tile // route_tile
    full = lambda shape: pl.BlockSpec(shape, lambda i: (0,) * len(shape))
    return pl.pallas_call(
        _merge_kernel,
        grid=(nt,),
        in_specs=[pl.BlockSpec((tile, D_MODEL), lambda i: (i, 0)),
                  pl.BlockSpec((tile, D_ATTN), lambda i: (i, 0)),
                  pl.BlockSpec((tile, D_SSM), lambda i: (i, 0)),
                  full((SUBLANES, LANES)),
                  full((D_ATTN, D_MODEL)), full((D_SSM, 2 * D_MODEL)), full((D_MODEL, 2 * D_MODEL)),
                  full((1, 2 * D_MODEL)), full((D_MODEL, D_MODEL)),
                  full((1, D_MODEL)), full((1, D_MODEL)),
                  full((N_EXPERTS, D_MODEL)), full((N_EXPERTS, 1))],
        out_specs=[pl.BlockSpec((tile, D_MODEL), lambda i: (i, 0)),
                   pl.BlockSpec((TOP_K, tile), lambda i: (0, i)),
                   pl.BlockSpec((tile, 2 * TOP_K), lambda i: (i, 0)),
                   pl.BlockSpec((per_step, SUBLANES, LANES), lambda i: (i, 0, 0)),
                   full((SUBLANES, LANES))],
        out_shape=[jax.ShapeDtypeStruct((n, D_MODEL), F32),
                   jax.ShapeDtypeStruct((TOP_K, n), jnp.int32),
                   jax.ShapeDtypeStruct((n, 2 * TOP_K), F32),
                   jax.ShapeDtypeStruct((nt * per_step, SUBLANES, LANES), jnp.int32),
                   jax.ShapeDtypeStruct((SUBLANES, LANES), F32)],
        scratch_shapes=[pltpu.VMEM((SUBLANES, LANES), F32)],
        compiler_params=_cparams(("arbitrary",), VMEM_LIMIT_MERGE),
        name="merge",
    )(x, o_attn, y_ssm, carry_in, w["wao"], w["wso"], w["wg"], w["bg"], w["wo"], w["g1"], w["b1"],
      w["wrt"], w["brt"])


def _tab(tab_ref, tile, row, e):
    return tab_ref[(tile * TAB_ROWS + row) * N_EXPERTS + e]


BIG_PIECE = 4 * RUN_ROWS
MAX_UNITS_LOG2 = 8


def _for_each_run_piece(tab_ref, tile, fn):
    def per_expert(e, carry):
        loff = RUN_ROWS * _tab(tab_ref, tile, 1, e)
        goff = RUN_ROWS * _tab(tab_ref, tile, 2, e)
        units = _tab(tab_ref, tile, 0, e)
        n_big = lax.shift_right_logical(units, 2)

        def big(j, c2):
            fn(pl.multiple_of(loff + j * BIG_PIECE, RUN_ROWS), goff + j * BIG_PIECE, e, BIG_PIECE)
            return c2

        lax.fori_loop(0, n_big, big, 0)
        done = n_big * BIG_PIECE

        def small(j, c2):
            fn(pl.multiple_of(loff + done + j * RUN_ROWS, RUN_ROWS), goff + done + j * RUN_ROWS, e, RUN_ROWS)
            return c2

        lax.fori_loop(0, units & 3, small, 0)
        return carry

    lax.fori_loop(0, N_EXPERTS, per_expert, 0)


def _drain_units(units, wait_copy, buffer_rows):
    assert buffer_rows < (RUN_ROWS << MAX_UNITS_LOG2)
    for b in range(MAX_UNITS_LOG2):
        if (RUN_ROWS << b) > buffer_rows:
            break

        @pl.when((lax.shift_right_logical(units, b) & 1) == 1)
        def _():
            wait_copy(RUN_ROWS << b).wait()


def _dispatch_kernel(tab_ref, seg_ref, tot_ref, tail_ref, lpos_p_ref, xp_ref, lpos_s_ref, xs_in_ref, xs_ref,
                     loc_sc, zero_sc, sem, zsem):
    i = pl.program_id(0)
    last = pl.num_programs(0) - 1
    tile = i
    slot = i % 2
    loc = loc_sc.shape[1]

    @pl.when(i == 0)
    def _():
        zero_sc[...] = jnp.zeros_like(zero_sc)

        def tail_copy(e, j):
            row = pl.multiple_of(RUN_ROWS * (tail_ref[e] + j), RUN_ROWS)
            return pltpu.make_async_copy(zero_sc.at[pl.ds(0, RUN_ROWS)], xs_ref.at[pl.ds(row, RUN_ROWS)], zsem)

        def per_expert(e, carry):
            n = tail_ref[N_EXPERTS + e]
            lax.fori_loop(0, n, lambda j, c2: (tail_copy(e, j).start(), c2)[1], 0)
            lax.fori_loop(0, n, lambda j, c2: (tail_copy(e, j).wait(), c2)[1], 0)
            return carry

        lax.fori_loop(0, N_EXPERTS, per_expert, 0)

        def block_copy(b):
            row = pl.multiple_of(b * MOE_ROWS, MOE_ROWS)
            return pltpu.make_async_copy(zero_sc, xs_ref.at[pl.ds(row, MOE_ROWS)], zsem)

        first_unused, n_blocks = tail_ref[2 * N_EXPERTS], xs_ref.shape[0] // MOE_ROWS
        lax.fori_loop(first_unused, n_blocks, lambda b, c2: (block_copy(b).start(), c2)[1], 0)
        lax.fori_loop(first_unused, n_blocks, lambda b, c2: (block_copy(b).wait(), c2)[1], 0)

    def sort_tile(lpos_ref, x_ref):
        tm = x_ref.shape[0]
        rows = lax.broadcasted_iota(jnp.int32, (loc, tm), 0)
        lp = lpos_ref[...]
        onehot = jnp.zeros((loc, tm), F32)
        for k in range(TOP_K):
            onehot = jnp.where(rows == lp[k:k + 1], 1.0, onehot)
        loc_sc[slot] = jnp.dot(onehot.astype(BF16), x_ref[...].astype(BF16), preferred_element_type=F32)

    @pl.when(i < last)
    def _():
        sort_tile(lpos_p_ref, xp_ref)

    @pl.when(i == last)
    def _():
        sort_tile(lpos_s_ref, xs_in_ref)

    def piece_copy(sl, lrow, grow, e, n):
        dst = pl.multiple_of(seg_ref[e] + grow, RUN_ROWS)
        return pltpu.make_async_copy(loc_sc.at[sl, pl.ds(lrow, n)], xs_ref.at[pl.ds(dst, n)], sem.at[sl])

    _for_each_run_piece(tab_ref, tile, lambda lrow, grow, e, n: piece_copy(slot, lrow, grow, e, n).start())

    def drain(tl, sl):
        _drain_units(tot_ref[tl], lambda n: piece_copy(sl, 0, 0, 0, n), loc)

    @pl.when(i > 0)
    def _():
        drain(tile - 1, 1 - slot)

    @pl.when(i == last)
    def _():
        drain(tile, slot)


def _dispatch(tab, seg_start, tot, tails, lpos_p, x1_p, lpos_s, x1_s, *, tile, nrows):
    nt_p = x1_p.shape[0] // tile
    ns = x1_s.shape[0]
    loc = tile * TOP_K + N_EXPERTS * RUN_ROWS
    prompt_blk = lambda i, *_: jnp.minimum(i, nt_p - 1)
    return pl.pallas_call(
        _dispatch_kernel,
        grid_spec=pltpu.PrefetchScalarGridSpec(
            num_scalar_prefetch=4,
            grid=(nt_p + 1,),
            in_specs=[pl.BlockSpec((TOP_K, tile), lambda i, *_: (0, prompt_blk(i))),
                      pl.BlockSpec((tile, D_MODEL), lambda i, *_: (prompt_blk(i), 0)),
                      pl.BlockSpec((TOP_K, ns), lambda i, *_: (0, 0)),
                      pl.BlockSpec((ns, D_MODEL), lambda i, *_: (0, 0))],
            out_specs=pl.BlockSpec(memory_space=pl.ANY),
            scratch_shapes=[pltpu.VMEM((2, loc, D_MODEL), F32), pltpu.VMEM((MOE_ROWS, D_MODEL), F32),
                            pltpu.SemaphoreType.DMA((2,)), pltpu.SemaphoreType.DMA(())]),
        out_shape=jax.ShapeDtypeStruct((nrows, D_MODEL), F32),
        compiler_params=_cparams(("arbitrary",), VMEM_LIMIT),
        name="dispatch",
    )(tab, seg_start, tot, tails, lpos_p, x1_p, lpos_s, x1_s)


def _deinterleave_matrix():
    pm = np.zeros((MXU_DIM, MXU_DIM), np.float32)
    half = MXU_DIM // 2
    for c in range(half):
        pm[2 * c, c] = 1.0
        pm[2 * c + 1, half + c] = 1.0
    return pm


def _expert_kernel(be_ref, nu_ref, nv_ref, ord_ref, nxt_ref, xs_ref, w1_hbm, b1_ref, w2_hbm, b2_ref, pm_ref, y_ref,
                   w1f_sc, w2f_sc, w1p_sc, w2b_sc, sem):
    del nu_ref
    i = pl.program_id(0)
    e = be_ref[i]
    prev = be_ref[jnp.maximum(i - 1, 0)]
    nblk = 2 * D_FF // MXU_DIM

    def weight_copies(expert, slot):
        return (pltpu.make_async_copy(w1_hbm.at[expert], w1f_sc.at[slot], sem.at[0, slot]),
                pltpu.make_async_copy(w2_hbm.at[expert], w2f_sc.at[slot], sem.at[1, slot]))

    @pl.when(i == 0)
    def _():
        for cp in weight_copies(e, 0):
            cp.start()

    @pl.when((i == 0) | (e != prev))
    def _():
        slot = ord_ref[i] % 2
        for cp in weight_copies(e, slot):
            cp.wait()
        nxt = nxt_ref[i]

        @pl.when(nxt >= 0)
        def _():
            for cp in weight_copies(nxt, 1 - slot):
                cp.start()

        for cb in range(nblk):
            blk = w1f_sc[slot, :, cb * MXU_DIM:(cb + 1) * MXU_DIM].astype(BF16)
            w1p_sc[:, cb * MXU_DIM:(cb + 1) * MXU_DIM] = jnp.dot(
                blk, pm_ref[...], preferred_element_type=F32).astype(BF16)
        w2b_sc[...] = w2f_sc[slot].astype(BF16)

    for blk in range(MOE_STEP_BLOCKS):
        rows = slice(blk * MOE_ROWS, (blk + 1) * MOE_ROWS)

        @pl.when(blk < nv_ref[i])
        def _():
            x = xs_ref[rows, :].astype(BF16)
            h = jnp.dot(x, w1p_sc[...], preferred_element_type=F32) + b1_ref[0]
            half = MXU_DIM // 2
            acts = []
            for cb in range(nblk):
                x_glu = jnp.minimum(h[:, cb * MXU_DIM:cb * MXU_DIM + half], SWIGLU_LIMIT)
                x_lin = jnp.clip(h[:, cb * MXU_DIM + half:(cb + 1) * MXU_DIM], -SWIGLU_LIMIT, SWIGLU_LIMIT)
                acts.append((x_glu * jax.nn.sigmoid(SWIGLU_ALPHA * x_glu) * (x_lin + 1.0)).astype(BF16))
            act = jnp.concatenate(acts, axis=1)
            y_ref[rows, :] = jnp.dot(act, w2b_sc[...], preferred_element_type=F32) + b2_ref[0]

        @pl.when(blk >= nv_ref[i])
        def _():
            y_ref[rows, :] = jnp.zeros((MOE_ROWS, D_MODEL), F32)


def _experts(block_e, n_used, n_valid, run_ord, run_next, xs, w1, b1p, w2, b2, pm):
    nrows = xs.shape[0]
    step_rows = MOE_STEP_BLOCKS * MOE_ROWS
    nb = nrows // step_rows
    return pl.pallas_call(
        _expert_kernel,
        grid_spec=pltpu.PrefetchScalarGridSpec(
            num_scalar_prefetch=5,
            grid=(nb,),
            in_specs=[pl.BlockSpec((step_rows, D_MODEL), lambda i, be, nu, *_: (jnp.minimum(i, nu[0] - 1), 0)),
                      pl.BlockSpec(memory_space=pl.ANY),
                      pl.BlockSpec((1, 1, 2 * D_FF), lambda i, be, *_: (be[i], 0, 0)),
                      pl.BlockSpec(memory_space=pl.ANY),
                      pl.BlockSpec((1, 1, D_MODEL), lambda i, be, *_: (be[i], 0, 0)),
                      pl.BlockSpec((MXU_DIM, MXU_DIM), lambda i, *_: (0, 0))],
            out_specs=pl.BlockSpec((step_rows, D_MODEL), lambda i, *_: (i, 0)),
            scratch_shapes=[pltpu.VMEM((2, D_MODEL, 2 * D_FF), F32), pltpu.VMEM((2, D_FF, D_MODEL), F32),
                            pltpu.VMEM((D_MODEL, 2 * D_FF), BF16), pltpu.VMEM((D_FF, D_MODEL), BF16),
                            pltpu.SemaphoreType.DMA((2, 2))]),
        out_shape=jax.ShapeDtypeStruct((nrows, D_MODEL), F32),
        compiler_params=_cparams(("arbitrary",), VMEM_LIMIT_MERGE),
        name="experts",
    )(block_e, n_used, n_valid, run_ord, run_next, xs, w1, b1p, w2, b2, pm)


def _combine_kernel(tab_ref, seg_ref, tot_ref, cols_ref, x1_ref, g2_ref, b2_ref, ys_ref, y_ref, loc_sc, sem,
                    *, tile_base):
    i = pl.program_id(0)
    last = pl.num_programs(0) - 1
    tile = i + tile_base
    slot = i % 2
    loc, tm = loc_sc.shape[1], x1_ref.shape[0]

    def piece_copy(sl, lrow, grow, e, n):
        src = pl.multiple_of(seg_ref[e] + grow, RUN_ROWS)
        return pltpu.make_async_copy(ys_ref.at[pl.ds(src, n)], loc_sc.at[sl, pl.ds(lrow, n)], sem.at[sl])

    def gather(tl, sl):
        _for_each_run_piece(tab_ref, tl, lambda lrow, grow, e, n: piece_copy(sl, lrow, grow, e, n).start())

    @pl.when(i == 0)
    def _():
        loc_sc[...] = jnp.zeros_like(loc_sc)
        gather(tile, slot)

    @pl.when(i < last)
    def _():
        gather(tile + 1, 1 - slot)

    _drain_units(tot_ref[tile], lambda n: piece_copy(slot, 0, 0, 0, n), loc)

    cols = cols_ref[...]
    lane = lax.broadcasted_iota(jnp.int32, (tm, loc), 1)
    weights = jnp.zeros((tm, loc), F32)
    for k in range(TOP_K):
        weights = jnp.where(lane == cols[:, k:k + 1].astype(jnp.int32), cols[:, TOP_K + k:TOP_K + k + 1], weights)
    ffn = jnp.dot(weights.astype(BF16), loc_sc[slot].astype(BF16), preferred_element_type=F32)
    y_ref[...] = _layer_norm(DEEPNORM_ALPHA * x1_ref[...] + ffn, g2_ref[...], b2_ref[...])


def _combine(tab, seg_start, tot, cols, x1, g2, b2, ys, *, tile, tile_base):
    n = x1.shape[0]
    loc = tile * TOP_K + N_EXPERTS * RUN_ROWS
    return pl.pallas_call(
        functools.partial(_combine_kernel, tile_base=tile_base),
        grid_spec=pltpu.PrefetchScalarGridSpec(
            num_scalar_prefetch=3,
            grid=(n // tile,),
            in_specs=[pl.BlockSpec((tile, 2 * TOP_K), lambda i, *_: (i, 0)),
                      pl.BlockSpec((tile, D_MODEL), lambda i, *_: (i, 0)),
                      pl.BlockSpec((1, D_MODEL), lambda i, *_: (0, 0)),
                      pl.BlockSpec((1, D_MODEL), lambda i, *_: (0, 0)),
                      pl.BlockSpec(memory_space=pl.ANY)],
            out_specs=pl.BlockSpec((tile, D_MODEL), lambda i, *_: (i, 0)),
            scratch_shapes=[pltpu.VMEM((2, loc, D_MODEL), F32), pltpu.SemaphoreType.DMA((2,))]),
        out_shape=jax.ShapeDtypeStruct((n, D_MODEL), F32),
        compiler_params=_cparams(("arbitrary",), VMEM_LIMIT),
        name="combine",
    )(tab, seg_start, tot, cols, x1, g2, b2, ys)


def kernel(x_prompt, x_sample, cache_k_win, cache_v_win, state_ssm_re, state_ssm_im, w_in, b_in, attn_sinks,
           w_attn_out, ssm_a_re, ssm_a_im, ssm_log_dt, ssm_b_re, ssm_b_im, ssm_c_re, ssm_c_im, ssm_d, w_ssm_out,
           w_gate, b_gate, w_out, ln1_g, ln1_b, w_router, b_router, w_exp1, b_exp1, w_exp2, b_exp2, ln2_g, ln2_b):
    assert w_in.shape[0] == DEPTH == 1
    bsz, seq, _ = x_prompt.shape
    nsamp = x_sample.shape[0]
    assert x_sample.shape[1] == 1
    n_p = bsz * seq
    n_tok = n_p + nsamp

    xp = x_prompt.reshape(n_p, D_MODEL)
    xsm = x_sample.reshape(nsamp, D_MODEL)
    b_in2 = b_in[0].reshape(1, D_IN)
    sinks = attn_sinks[0].astype(F32)

    q_p, k_p, v_p, u_p = _proj(xp, w_in[0].astype(BF16), b_in2, tile=512, exact_f32=False, q_dtype=BF16)
    q_s, k_s, v_s, u_s = _proj(xsm, w_in[0], b_in2, tile=nsamp, exact_f32=True, q_dtype=F32)

    o_p = _attn_prompt(sinks, q_p.reshape(bsz, seq, D_ATTN), k_p.reshape(bsz, seq, D_KV),
                       v_p.reshape(bsz, seq, D_KV)).reshape(n_p, D_ATTN)
    k_buf = cache_k_win[0].reshape(nsamp, WINDOW, D_KV)
    v_buf = cache_v_win[0].reshape(nsamp, WINDOW, D_KV)
    o_s = _attn_sample(sinks, q_s, k_s, v_s, k_buf, v_buf)

    sp = _s5_params(ssm_a_re[0], ssm_a_im[0], ssm_log_dt[0], ssm_b_re[0], ssm_b_im[0], ssm_c_re[0], ssm_c_im[0])
    y_p, hp_re, hp_im = _s5_prompt(u_p, bsz, seq, _s5_chunk_mats(sp, ssm_d[0]))
    y_s, hs_re, hs_im = _s5_sample(u_s, state_ssm_re[0].reshape(nsamp, -1), state_ssm_im[0].reshape(nsamp, -1),
                                   _s5_sample_mats(sp, ssm_d[0]))

    wm = dict(wao=w_attn_out[0].astype(BF16), wso=w_ssm_out[0].astype(BF16), wg=w_gate[0].astype(BF16),
              bg=b_gate[0].reshape(1, -1), wo=w_out[0].astype(BF16), g1=ln1_g[0].reshape(1, -1),
              b1=ln1_b[0].reshape(1, -1), wrt=w_router[0].T, brt=b_router[0].reshape(-1, 1))
    carry0 = jnp.zeros((SUBLANES, LANES), F32)
    x1_p, lpos_p, cols_p, tab_p, carry1 = _merge(xp, o_p, y_p, carry0, wm, tile=MERGE_TILE, route_tile=TOK_TILE)
    x1_s, lpos_s, cols_s, tab_s, carry2 = _merge(xsm, o_s, y_s, carry1, wm, tile=nsamp, route_tile=nsamp)

    nt_p = n_p // TOK_TILE
    tab = jnp.concatenate([tab_p[:, :TAB_ROWS, :N_EXPERTS], tab_s[:, :TAB_ROWS, :N_EXPERTS]], axis=0)
    tot = jnp.sum(tab[:, 0, :], axis=1).astype(jnp.int32)
    tab = tab.reshape(-1)
    seg_rows = carry2[0, :N_EXPERTS].astype(jnp.int32) * RUN_ROWS
    step_rows = MOE_STEP_BLOCKS * MOE_ROWS
    padded = ((seg_rows + step_rows - 1) // step_rows) * step_rows
    pad_end = jnp.cumsum(padded)
    pad_start = (pad_end - padded).astype(jnp.int32)
    seg_end = pad_start + seg_rows
    n_runs = (nt_p + 1) * N_EXPERTS
    nb_max = (n_tok * TOP_K + n_runs * (RUN_ROWS - 1) + N_EXPERTS * (step_rows - 1) + step_rows - 1) // step_rows
    n_used = (pad_end[-1] // step_rows).astype(jnp.int32)
    tails = jnp.concatenate([seg_end // RUN_ROWS, (padded - seg_rows) // RUN_ROWS,
                             (pad_end[-1:] // MOE_ROWS)]).astype(jnp.int32)
    blk_start = jnp.arange(nb_max, dtype=jnp.int32) * step_rows
    blk_e = jnp.minimum(jnp.sum(blk_start[:, None] >= pad_end[None, :], axis=1), N_EXPERTS - 1).astype(jnp.int32)
    used = jnp.arange(nb_max) < n_used
    blk_e = jnp.where(used, blk_e, jnp.max(jnp.where(used, blk_e, 0)))
    ids = jnp.arange(N_EXPERTS, dtype=jnp.int32)
    of_blk = blk_e[:, None] == ids[None, :]
    n_valid = jnp.clip((jnp.sum(jnp.where(of_blk, seg_end[None, :], 0), axis=1) - blk_start + MOE_ROWS - 1)
                       // MOE_ROWS, 0, MOE_STEP_BLOCKS)
    n_valid = jnp.where(used, n_valid, 0).astype(jnp.int32)
    new_run = jnp.concatenate([jnp.ones((1,), jnp.int32), (blk_e[1:] != blk_e[:-1]).astype(jnp.int32)])
    run_ord = (jnp.cumsum(new_run) - 1).astype(jnp.int32)
    later = (ids[None, :] > ids[:, None]) & (padded > 0)[None, :]
    next_e = jnp.min(jnp.where(later, ids[None, :], N_EXPERTS), axis=1)
    next_e = jnp.where(next_e < N_EXPERTS, next_e, -1).astype(jnp.int32)
    run_next = jnp.sum(jnp.where(of_blk, next_e[None, :], 0), axis=1).astype(jnp.int32)

    nrows = nb_max * step_rows
    xs = _dispatch(tab, pad_start, tot, tails, lpos_p, x1_p, lpos_s, x1_s, tile=TOK_TILE, nrows=nrows)

    b1p = b_exp1[0].reshape(N_EXPERTS, 2 * D_FF // MXU_DIM, MXU_DIM // 2, 2)
    b1p = jnp.swapaxes(b1p, 2, 3).reshape(N_EXPERTS, 1, 2 * D_FF)
    ys = _experts(blk_e, n_used.reshape(1), n_valid, run_ord, run_next, xs, w_exp1[0], b1p, w_exp2[0],
                  b_exp2[0].reshape(N_EXPERTS, 1, D_MODEL),
                  jnp.asarray(_deinterleave_matrix(), BF16))

    g2, b2 = ln2_g[0].reshape(1, -1), ln2_b[0].reshape(1, -1)
    y_prompt = _combine(tab, pad_start, tot, cols_p, x1_p, g2, b2, ys, tile=TOK_TILE, tile_base=0)
    y_sample = _combine(tab, pad_start, tot, cols_s, x1_s, g2, b2, ys, tile=nsamp, tile_base=nt_p)

    k_p4 = k_p.reshape(bsz, seq, D_KV)[:, -WINDOW:].reshape(bsz, WINDOW, N_KV_HEADS, HEAD_DIM)
    v_p4 = v_p.reshape(bsz, seq, D_KV)[:, -WINDOW:].reshape(bsz, WINDOW, N_KV_HEADS, HEAD_DIM)
    k_s4 = jnp.concatenate([cache_k_win[0][:, 1:], k_s.reshape(nsamp, 1, N_KV_HEADS, HEAD_DIM)], axis=1)
    v_s4 = jnp.concatenate([cache_v_win[0][:, 1:], v_s.reshape(nsamp, 1, N_KV_HEADS, HEAD_DIM)], axis=1)
    st = lambda a, n: a.reshape(1, n, N_SSM_GROUPS, SSM_STATE)
    return (y_prompt.reshape(bsz, seq, D_MODEL), y_sample.reshape(nsamp, 1, D_MODEL),
            k_p4[None], v_p4[None], st(hp_re, bsz), st(hp_im, bsz),
            k_s4[None], v_s4[None], st(hs_re, nsamp), st(hs_im, nsamp))
```

```python
import functools
import math

import numpy as np
import jax
import jax.numpy as jnp
from jax import lax
from jax.experimental import pallas as pl
from jax.experimental.pallas import tpu as pltpu

F32 = jnp.float32
BF16 = jnp.bfloat16

D_MODEL = 1024
HEAD_DIM = 64
N_Q_HEADS = 8
N_KV_HEADS = 2
Q_PER_KV = N_Q_HEADS // N_KV_HEADS
D_ATTN = N_Q_HEADS * HEAD_DIM
D_KV = N_KV_HEADS * HEAD_DIM
WINDOW = 128
ATTN_SCALE = HEAD_DIM ** -0.5
SSM_GROUP = 16
D_SSM = D_MODEL // 2
N_SSM_GROUPS = D_SSM // SSM_GROUP
SSM_STATE = 64
D_IN = D_ATTN + 2 * D_KV + D_SSM
N_EXPERTS = 32
TOP_K = 4
D_FF = D_MODEL
SWIGLU_LIMIT = 7.0
SWIGLU_ALPHA = 1.702
LN_EPS = 1e-5
DEPTH = 1
DEEPNORM_ALPHA = (2 * DEPTH) ** 0.25

LANES = 128
SUBLANES = 8
MXU_DIM = 256

S5_CHUNK = MXU_DIM // SSM_GROUP
S5_LANE_GROUPS = LANES // SSM_GROUP
MOE_ROWS = 256
MOE_STEP_BLOCKS = 2
TOK_TILE = 256
MERGE_TILE = 512
VMEM_LIMIT = 48 * 1024 * 1024
VMEM_LIMIT_MERGE = 56 * 1024 * 1024


def _cparams(sem, vmem=None):
    return pltpu.CompilerParams(dimension_semantics=sem, vmem_limit_bytes=vmem)


def _proj_kernel(x_ref, w_ref, b_ref, q_ref, k_ref, v_ref, u_ref, *, exact_f32):
    if exact_f32:
        h = jnp.dot(x_ref[...], w_ref[...], preferred_element_type=F32, precision=lax.Precision.HIGHEST)
    else:
        h = jnp.dot(x_ref[...].astype(BF16), w_ref[...], preferred_element_type=F32)
    h = h + b_ref[...]
    q_ref[...] = (h[:, :D_ATTN] * ATTN_SCALE).astype(q_ref.dtype)
    k_ref[...] = h[:, D_ATTN:D_ATTN + D_KV]
    v_ref[...] = h[:, D_ATTN + D_KV:D_ATTN + 2 * D_KV]
    u_ref[...] = h[:, D_ATTN + 2 * D_KV:].astype(u_ref.dtype)


def _proj(x, w, b, *, tile, exact_f32, q_dtype):
    n = x.shape[0]
    return pl.pallas_call(
        functools.partial(_proj_kernel, exact_f32=exact_f32),
        grid=(n // tile,),
        in_specs=[pl.BlockSpec((tile, D_MODEL), lambda i: (i, 0)),
                  pl.BlockSpec((D_MODEL, D_IN), lambda i: (0, 0)),
                  pl.BlockSpec((1, D_IN), lambda i: (0, 0))],
        out_specs=[pl.BlockSpec((tile, D_ATTN), lambda i: (i, 0)),
                   pl.BlockSpec((tile, D_KV), lambda i: (i, 0)),
                   pl.BlockSpec((tile, D_KV), lambda i: (i, 0)),
                   pl.BlockSpec((tile, D_SSM), lambda i: (i, 0))],
        out_shape=[jax.ShapeDtypeStruct((n, D_ATTN), q_dtype),
                   jax.ShapeDtypeStruct((n, D_KV), F32),
                   jax.ShapeDtypeStruct((n, D_KV), F32),
                   jax.ShapeDtypeStruct((n, D_SSM), F32)],
        compiler_params=_cparams(("parallel",)),
        name="proj",
    )(x, w, b)


ATT_Q_TILE = 512


def _attn_prompt_kernel(sink_ref, q_ref, k_ref, v_ref, o_ref):
    i = pl.program_id(1)
    nk, nq = 2 * WINDOW, 2 * WINDOW
    lo = lax.broadcasted_iota(jnp.int32, (nk, LANES), 1) < HEAD_DIM
    top = lax.broadcasted_iota(jnp.int32, (nq, 1), 0) < WINDOW
    for blk in range(ATT_Q_TILE // WINDOW):
        q0 = i * ATT_Q_TILE + blk * WINDOW
        k0 = pl.multiple_of(jnp.maximum(q0 - WINDOW, 0), WINDOW)
        kk = k_ref[0, pl.ds(k0, nk), :]
        vv = v_ref[0, pl.ds(k0, nk), :]
        kk_sw = pltpu.roll(kk, HEAD_DIM, axis=1)
        vv_sw = pltpu.roll(vv, HEAD_DIM, axis=1)
        k_var = [[jnp.where(lo, kk, 0.0).astype(BF16), jnp.where(lo, 0.0, kk_sw).astype(BF16)],
                 [jnp.where(lo, kk_sw, 0.0).astype(BF16), jnp.where(lo, 0.0, kk).astype(BF16)]]
        v_var = [[jnp.where(lo, vv, 1.0).astype(BF16), jnp.where(lo, 1.0, vv_sw).astype(BF16)],
                 [jnp.where(lo, vv_sw, 1.0).astype(BF16), jnp.where(lo, 1.0, vv).astype(BF16)]]
        qpos = q0 + lax.broadcasted_iota(jnp.int32, (nq, nk), 0) % WINDOW
        kpos = k0 + lax.broadcasted_iota(jnp.int32, (nq, nk), 1)
        valid = (kpos <= qpos) & (qpos - kpos <= WINDOW)
        rows = slice(blk * WINDOW, (blk + 1) * WINDOW)
        for kv in range(N_KV_HEADS):
            pairs = (2 * kv, 2 * kv + 1)
            qs = jnp.concatenate([q_ref[0, rows, pr * LANES:(pr + 1) * LANES] for pr in pairs], axis=0)
            outs = []
            for parity in range(2):
                sink = jnp.where(top, sink_ref[2 * pairs[0] + parity], sink_ref[2 * pairs[1] + parity])
                s = lax.dot_general(qs, k_var[kv][parity], (((1,), (1,)), ((), ())), preferred_element_type=F32)
                s = jnp.where(valid, s, -jnp.inf)
                m = jnp.maximum(jnp.max(s, axis=-1, keepdims=True), sink)
                p = jnp.exp(s - m).astype(BF16)
                acc = jnp.dot(p, v_var[kv][parity], preferred_element_type=F32)
                outs.append(acc / (pltpu.roll(acc, HEAD_DIM, axis=1) + jnp.exp(sink - m)))
            o = jnp.where(lo, outs[0], outs[1]).astype(o_ref.dtype)
            for j, pr in enumerate(pairs):
                o_ref[0, rows, pr * LANES:(pr + 1) * LANES] = o[j * WINDOW:(j + 1) * WINDOW]


def _attn_prompt(sinks, q, k, v):
    bsz, seq = q.shape[0], q.shape[1]
    return pl.pallas_call(
        _attn_prompt_kernel,
        grid=(bsz, seq // ATT_Q_TILE),
        in_specs=[pl.BlockSpec(memory_space=pltpu.SMEM),
                  pl.BlockSpec((1, ATT_Q_TILE, D_ATTN), lambda b, i: (b, i, 0)),
                  pl.BlockSpec((1, seq, D_KV), lambda b, i: (b, 0, 0)),
                  pl.BlockSpec((1, seq, D_KV), lambda b, i: (b, 0, 0))],
        out_specs=pl.BlockSpec((1, ATT_Q_TILE, D_ATTN), lambda b, i: (b, i, 0)),
        out_shape=jax.ShapeDtypeStruct((bsz, seq, D_ATTN), BF16),
        compiler_params=_cparams(("parallel", "parallel")),
        name="attn_prompt",
    )(sinks, q, k, v)


ATT_S_GROUP = 16


def _attn_sample_kernel(sink_ref, q_ref, kn_ref, vn_ref, kb_ref, vb_ref, o_ref):
    g = ATT_S_GROUP
    rows = Q_PER_KV * g
    ncol = g * WINDOW
    kb = kb_ref[...].reshape(ncol, D_KV).astype(BF16)
    vb = vb_ref[...].reshape(ncol, D_KV).astype(BF16)
    rseq = lax.broadcasted_iota(jnp.int32, (rows, ncol), 0) % g
    cseq = lax.broadcasted_iota(jnp.int32, (rows, ncol), 1) // WINDOW
    own = rseq == cseq
    rhead = lax.broadcasted_iota(jnp.int32, (rows, 1), 0) // g
    for kv in range(N_KV_HEADS):
        lo = kv * HEAD_DIM
        qs = jnp.concatenate(
            [q_ref[:, (kv * Q_PER_KV + h) * HEAD_DIM:(kv * Q_PER_KV + h + 1) * HEAD_DIM] for h in range(Q_PER_KV)],
            axis=0)
        kn = jnp.concatenate([kn_ref[:, lo:lo + HEAD_DIM]] * Q_PER_KV, axis=0)
        vn = jnp.concatenate([vn_ref[:, lo:lo + HEAD_DIM]] * Q_PER_KV, axis=0)
        sink = jnp.zeros((rows, 1), F32)
        for h in range(Q_PER_KV):
            sink = jnp.where(rhead == h, sink_ref[kv * Q_PER_KV + h], sink)
        qs = qs.astype(BF16)
        s = lax.dot_general(qs, kb[:, lo:lo + HEAD_DIM], (((1,), (1,)), ((), ())), preferred_element_type=F32)
        s = jnp.where(own, s, -jnp.inf)
        s_new = jnp.sum(qs.astype(F32) * kn.astype(BF16).astype(F32), axis=-1, keepdims=True)
        m = jnp.maximum(jnp.maximum(jnp.max(s, axis=-1, keepdims=True), s_new), sink)
        p = jnp.exp(s - m)
        p_new = jnp.exp(s_new - m)
        denom = jnp.sum(p, axis=-1, keepdims=True) + p_new + jnp.exp(sink - m)
        o = jnp.dot(p.astype(BF16), vb[:, lo:lo + HEAD_DIM], preferred_element_type=F32)
        o = (o + p_new.astype(BF16).astype(F32) * vn.astype(BF16).astype(F32)) / denom
        for h in range(Q_PER_KV):
            c0 = (kv * Q_PER_KV + h) * HEAD_DIM
            o_ref[:, c0:c0 + HEAD_DIM] = o[h * g:(h + 1) * g].astype(o_ref.dtype)


def _attn_sample(sinks, q, k_new, v_new, k_buf, v_buf):
    n = q.shape[0]
    g = ATT_S_GROUP
    return pl.pallas_call(
        _attn_sample_kernel,
        grid=(n // g,),
        in_specs=[pl.BlockSpec(memory_space=pltpu.SMEM),
                  pl.BlockSpec((g, D_ATTN), lambda i: (i, 0)),
                  pl.BlockSpec((g, D_KV), lambda i: (i, 0)),
                  pl.BlockSpec((g, D_KV), lambda i: (i, 0)),
                  pl.BlockSpec((g, WINDOW, D_KV), lambda i: (i, 0, 0)),
                  pl.BlockSpec((g, WINDOW, D_KV), lambda i: (i, 0, 0))],
        out_specs=pl.BlockSpec((g, D_ATTN), lambda i: (i, 0)),
        out_shape=jax.ShapeDtypeStruct((n, D_ATTN), F32),
        compiler_params=_cparams(("parallel",)),
        name="attn_sample",
    )(sinks, q, k_new, v_new, k_buf, v_buf)


def _s5_params(a_re, a_im, log_dt, b_re, b_im, c_re, c_im):
    hp = lax.Precision.HIGHEST
    dt = jnp.exp(log_dt.astype(F32))[:, None]
    are, aim = a_re.astype(F32), a_im.astype(F32)
    tau = jnp.arange(S5_CHUNK + 1, dtype=F32)[None, :, None]
    mag = jnp.exp(tau * (dt * are)[:, None, :])
    ang = tau * (dt * aim)[:, None, :]
    pw_re, pw_im = mag * jnp.cos(ang), mag * jnp.sin(ang)
    ab_re, ab_im = pw_re[:, 1], pw_im[:, 1]
    den = are * are + aim * aim
    f_re = ((ab_re - 1.0) * are + ab_im * aim) / den
    f_im = (ab_im * are - (ab_re - 1.0) * aim) / den
    bre, bim = b_re.astype(F32), b_im.astype(F32)
    bb_re = f_re[..., None] * bre - f_im[..., None] * bim
    bb_im = f_re[..., None] * bim + f_im[..., None] * bre
    cre, cim = c_re.astype(F32), c_im.astype(F32)
    return dict(pw_re=pw_re, pw_im=pw_im, ab_re=ab_re, ab_im=ab_im, bb_re=bb_re, bb_im=bb_im,
                c_re=cre, c_im=cim, hp=hp)


def _s5_chunk_mats(sp, d_skip):
    hp = sp["hp"]
    g, t, c, p = N_SSM_GROUPS, S5_CHUNK, SSM_GROUP, SSM_STATE
    pw_re, pw_im = sp["pw_re"], sp["pw_im"]
    ca_re = sp["c_re"][:, None] * pw_re[:, :, None, :] - sp["c_im"][:, None] * pw_im[:, :, None, :]
    ca_im = sp["c_re"][:, None] * pw_im[:, :, None, :] + sp["c_im"][:, None] * pw_re[:, :, None, :]
    kern = (jnp.einsum("gtcp,gpd->gtcd", ca_re[:, :t], sp["bb_re"], precision=hp)
            - jnp.einsum("gtcp,gpd->gtcd", ca_im[:, :t], sp["bb_im"], precision=hp))
    kc = jnp.swapaxes(kern, 2, 3)
    kc = kc.at[:, 0].add(d_skip.astype(F32).reshape(g, 1, c) * jnp.eye(c, dtype=F32)[None])
    rev_re, rev_im = pw_re[:, t - 1::-1][:, :t], pw_im[:, t - 1::-1][:, :t]
    wst_re = rev_re[:, :, None, :] * jnp.swapaxes(sp["bb_re"], 1, 2)[:, None] \
        - rev_im[:, :, None, :] * jnp.swapaxes(sp["bb_im"], 1, 2)[:, None]
    wst_im = rev_re[:, :, None, :] * jnp.swapaxes(sp["bb_im"], 1, 2)[:, None] \
        + rev_im[:, :, None, :] * jnp.swapaxes(sp["bb_re"], 1, 2)[:, None]
    wo_re = jnp.transpose(ca_re[:, 1:t + 1], (0, 3, 1, 2))
    wo_im = -jnp.transpose(ca_im[:, 1:t + 1], (0, 3, 1, 2))
    nv, gl = g // S5_LANE_GROUPS, S5_LANE_GROUPS
    kc, wst_re, wst_im, wo_re, wo_im = lax.optimization_barrier((kc, wst_re, wst_im, wo_re, wo_im))
    kc5 =jnp.transpose(kc.reshape(nv, gl, t, c, c), (0, 2, 1, 3, 4))
    ws6 = jnp.transpose(jnp.stack([wst_re, wst_im], axis=3).reshape(nv, gl, t, c, 2, p),
                        (0, 2, 1, 3, 4, 5))
    wo6 = jnp.transpose(jnp.stack([wo_re, wo_im], axis=0).reshape(2, nv, gl, p, t, c),
                        (1, 0, 2, 3, 4, 5))
    kc5, ws6, wo6 = lax.optimization_barrier((kc5.astype(BF16), ws6.astype(BF16), wo6.astype(BF16)))
    spread_b = np.zeros((c, LANES), np.float32)
    spread_s = np.zeros((2 * p, 2 * gl * p), np.float32)
    spread_o = np.zeros((t * c, t * LANES), np.float32)
    for h in range(gl):
        spread_b[np.arange(c), h * c + np.arange(c)] = 1.0
        for ri in range(2):
            spread_s[ri * p + np.arange(p), ri * gl * p + h * p + np.arange(p)] = 1.0
        for tt in range(t):
            spread_o[tt * c + np.arange(c), tt * LANES + h * c + np.arange(c)] = 1.0
    at_re = pw_re[:, t].reshape(1, g * p)
    at_im = pw_im[:, t].reshape(1, g * p)
    return dict(kc=kc5.reshape(nv, t, LANES, c), ws=ws6.reshape(nv, t * LANES, 2 * p),
                wo=wo6.reshape(nv, 2 * gl * p, t * c), spread_b=jnp.asarray(spread_b, BF16),
                spread_s=jnp.asarray(spread_s, BF16), spread_o=jnp.asarray(spread_o, BF16),
                at_re=at_re, at_im=at_im)


def _s5_chunk_rows(u_ref, nchunk):
    return jnp.concatenate(
        [u_ref[pl.ds(s, nchunk, stride=S5_CHUNK), :] for s in range(S5_CHUNK)], axis=1).astype(BF16)


S5_SLABS = S5_LANE_GROUPS * SSM_STATE // LANES


S5_EXPAND_ROWS = 256
S5_C_SHIFT = SSM_GROUP.bit_length() - 1
S5_P_SHIFT = SSM_STATE.bit_length() - 1


def _s5_expand(dst_ref, compact_ref, spread_ref, row_shift, col_shift):
    n_rows, n_cols = dst_ref.shape
    col_g = lax.shift_right_logical(lax.broadcasted_iota(jnp.int32, (S5_EXPAND_ROWS, n_cols), 1), col_shift)
    for r0 in range(0, n_rows, S5_EXPAND_ROWS):
        row_g = lax.shift_right_logical(r0 + lax.broadcasted_iota(jnp.int32, (S5_EXPAND_ROWS, n_cols), 0), row_shift)
        same = ((row_g ^ col_g) & (S5_LANE_GROUPS - 1)) == 0
        blk = jnp.dot(compact_ref[r0:r0 + S5_EXPAND_ROWS, :], spread_ref[...], preferred_element_type=F32)
        dst_ref[r0:r0 + S5_EXPAND_ROWS, :] = jnp.where(same, blk, 0.0).astype(dst_ref.dtype)


def _s5_state_kernel(u_ref, ws_ref, spread_ref, sre_ref, sim_ref, wst_sc):
    nchunk = sre_ref.shape[1]

    @pl.when(pl.program_id(1) == 0)
    def _():
        _s5_expand(wst_sc, ws_ref.at[0], spread_ref, S5_C_SHIFT, S5_P_SHIFT)

    s = jnp.dot(_s5_chunk_rows(u_ref, nchunk), wst_sc[...], preferred_element_type=F32)
    for k in range(S5_SLABS):
        sre_ref[k] = s[:, k * LANES:(k + 1) * LANES]
        sim_ref[k] = s[:, (S5_SLABS + k) * LANES:(S5_SLABS + k + 1) * LANES]


def _s5_scan_kernel(sre_ref, sim_ref, are_ref, aim_ref, hre_ref, him_ref, fre_ref, fim_ref, *, bsz):
    nchunk = sre_ref.shape[1] // bsz
    are = [jnp.broadcast_to(are_ref[:, k * LANES:(k + 1) * LANES], (bsz, LANES)) for k in range(S5_SLABS)]
    aim = [jnp.broadcast_to(aim_ref[:, k * LANES:(k + 1) * LANES], (bsz, LANES)) for k in range(S5_SLABS)]

    def body(j, carry):
        rows = pl.ds(j, bsz, stride=nchunk)
        out = []
        for k in range(S5_SLABS):
            cre, cim = carry[2 * k], carry[2 * k + 1]
            hre_ref[k, rows, :] = cre
            him_ref[k, rows, :] = cim
            sr = sre_ref[k, rows, :]
            si = sim_ref[k, rows, :]
            out += [are[k] * cre - aim[k] * cim + sr, are[k] * cim + aim[k] * cre + si]
        return tuple(out)

    zero = jnp.zeros((bsz, LANES), F32)
    fin = lax.fori_loop(0, nchunk, body, (zero,) * (2 * S5_SLABS), unroll=4)
    fre_ref[...] = jnp.concatenate(fin[0::2], axis=1)
    fim_ref[...] = jnp.concatenate(fin[1::2], axis=1)


def _s5_out_kernel(u_ref, kc_ref, spread_b_ref, hre_ref, him_ref, wo_ref, spread_o_ref, y_ref, m_sc, wout_sc):
    nchunk = hre_ref.shape[1]

    @pl.when(pl.program_id(1) == 0)
    def _():
        rg = lax.shift_right_logical(lax.broadcasted_iota(jnp.int32, (LANES, LANES), 0), S5_C_SHIFT)
        cg = lax.shift_right_logical(lax.broadcasted_iota(jnp.int32, (LANES, LANES), 1), S5_C_SHIFT)
        zero_blk = jnp.zeros((LANES, LANES), BF16)
        lag_blk = [jnp.where(rg == cg, jnp.dot(kc_ref[0, tau], spread_b_ref[...], preferred_element_type=F32),
                             0.0).astype(BF16) for tau in range(S5_CHUNK)]
        for s in range(S5_CHUNK):
            for t in range(S5_CHUNK):
                m_sc[s * LANES:(s + 1) * LANES, t * LANES:(t + 1) * LANES] = lag_blk[t - s] if t >= s else zero_blk
        _s5_expand(wout_sc, wo_ref.at[0], spread_o_ref, S5_P_SHIFT, S5_C_SHIFT)

    hcat = jnp.concatenate([hre_ref[k] for k in range(S5_SLABS)] + [him_ref[k] for k in range(S5_SLABS)],
                           axis=1).astype(BF16)
    lhs = _s5_chunk_rows(u_ref, nchunk)
    y = jnp.concatenate(
        [jnp.dot(lhs[:, :j + MXU_DIM], m_sc[:j + MXU_DIM, j:j + MXU_DIM], preferred_element_type=F32)
         for j in range(0, S5_CHUNK * LANES, MXU_DIM)], axis=1)
    y = y + jnp.dot(hcat, wout_sc[...], preferred_element_type=F32)
    for s in range(S5_CHUNK):
        y_ref[pl.ds(s, nchunk, stride=S5_CHUNK), :] = y[:, s * LANES:(s + 1) * LANES]


def _s5_prompt(u, bsz, seq, mats):
    at_re, at_im = mats["at_re"], mats["at_im"]
    g, t, p, c = N_SSM_GROUPS, S5_CHUNK, SSM_STATE, SSM_GROUP
    nchunk = seq // t
    n = nchunk * bsz
    nv = g // S5_LANE_GROUPS
    half = S5_LANE_GROUPS * p
    s_re, s_im = pl.pallas_call(
        _s5_state_kernel,
        grid=(nv, bsz),
        in_specs=[pl.BlockSpec((seq, LANES), lambda v, b: (b, v)),
                  pl.BlockSpec((1, t * LANES, 2 * p), lambda v, b: (v, 0, 0)),
                  pl.BlockSpec((2 * p, 2 * half), lambda v, b: (0, 0))],
        out_specs=[pl.BlockSpec((S5_SLABS, nchunk, LANES), lambda v, b: (v, b, 0)),
                   pl.BlockSpec((S5_SLABS, nchunk, LANES), lambda v, b: (v, b, 0))],
        out_shape=[jax.ShapeDtypeStruct((nv * S5_SLABS, n, LANES), F32)] * 2,
        scratch_shapes=[pltpu.VMEM((t * LANES, 2 * half), BF16)],
        compiler_params=_cparams(("parallel", "arbitrary"), VMEM_LIMIT),
        name="s5_state",
    )(u, mats["ws"], mats["spread_s"])
    h_re, h_im, f_re, f_im = pl.pallas_call(
        functools.partial(_s5_scan_kernel, bsz=bsz),
        grid=(nv,),
        in_specs=[pl.BlockSpec((S5_SLABS, n, LANES), lambda i: (i, 0, 0)),
                  pl.BlockSpec((S5_SLABS, n, LANES), lambda i: (i, 0, 0)),
                  pl.BlockSpec((1, half), lambda i: (0, i)),
                  pl.BlockSpec((1, half), lambda i: (0, i))],
        out_specs=[pl.BlockSpec((S5_SLABS, n, LANES), lambda i: (i, 0, 0)),
                   pl.BlockSpec((S5_SLABS, n, LANES), lambda i: (i, 0, 0)),
                   pl.BlockSpec((bsz, half), lambda i: (0, i)),
                   pl.BlockSpec((bsz, half), lambda i: (0, i))],
        out_shape=[jax.ShapeDtypeStruct((nv * S5_SLABS, n, LANES), F32)] * 2
        + [jax.ShapeDtypeStruct((bsz, g * p), F32)] * 2,
        compiler_params=_cparams(("parallel",)),
        name="s5_scan",
    )(s_re, s_im, at_re, at_im)
    y = pl.pallas_call(
        _s5_out_kernel,
        grid=(nv, bsz),
        in_specs=[pl.BlockSpec((seq, LANES), lambda v, b: (b, v)),
                  pl.BlockSpec((1, t, LANES, c), lambda v, b: (v, 0, 0, 0)),
                  pl.BlockSpec((c, LANES), lambda v, b: (0, 0)),
                  pl.BlockSpec((S5_SLABS, nchunk, LANES), lambda v, b: (v, b, 0)),
                  pl.BlockSpec((S5_SLABS, nchunk, LANES), lambda v, b: (v, b, 0)),
                  pl.BlockSpec((1, 2 * half, t * c), lambda v, b: (v, 0, 0)),
                  pl.BlockSpec((t * c, t * LANES), lambda v, b: (0, 0))],
        out_specs=pl.BlockSpec((seq, LANES), lambda v, b: (b, v)),
        out_shape=jax.ShapeDtypeStruct((bsz * seq, D_SSM), F32),
        scratch_shapes=[pltpu.VMEM((t * LANES, t * LANES), BF16), pltpu.VMEM((2 * half, t * LANES), BF16)],
        compiler_params=_cparams(("parallel", "arbitrary"), VMEM_LIMIT),
        name="s5_out",
    )(u, mats["kc"], mats["spread_b"], h_re, h_im, mats["wo"], mats["spread_o"])
    return y, f_re, f_im


S5S_GROUPS = LANES // SSM_GROUP


def _s5_sample_mats(sp, d_skip):
    go, gl, c, p = N_SSM_GROUPS // S5S_GROUPS, S5S_GROUPS, SSM_GROUP, SSM_STATE
    eye = jnp.eye(gl, dtype=F32)

    def bdiag_in(b):
        b4 = b.reshape(go, gl, p, c)
        return jnp.einsum("ogpc,gh->ogchp", b4, eye).reshape(go, gl * c, gl * p)

    def bdiag_out(cm):
        c4 = cm.reshape(go, gl, c, p)
        return jnp.einsum("ogcp,gh->ogphc", c4, eye).reshape(go, gl * p, gl * c)

    b8 = jnp.concatenate([bdiag_in(sp["bb_re"]), bdiag_in(sp["bb_im"])], axis=2)
    c8 = jnp.concatenate([bdiag_out(sp["c_re"]), -bdiag_out(sp["c_im"])], axis=1)
    a_re = sp["ab_re"].reshape(1, N_SSM_GROUPS * p)
    a_im = sp["ab_im"].reshape(1, N_SSM_GROUPS * p)
    return b8, c8, a_re, a_im, d_skip.astype(F32).reshape(1, D_SSM)


def _s5_sample_kernel(u_ref, hre_ref, him_ref, b8_ref, c8_ref, are_ref, aim_ref, d_ref,
                      y_ref, ore_ref, oim_ref):
    hp = lax.Precision.HIGHEST
    u = u_ref[...]
    half = S5S_GROUPS * SSM_STATE
    bu = jnp.dot(u, b8_ref[0], preferred_element_type=F32, precision=hp)
    are, aim = are_ref[...], aim_ref[...]
    h0r, h0i = hre_ref[...], him_ref[...]
    hr = are * h0r - aim * h0i + bu[:, :half]
    hi = are * h0i + aim * h0r + bu[:, half:]
    ore_ref[...] = hr
    oim_ref[...] = hi
    y = jnp.dot(jnp.concatenate([hr, hi], axis=1), c8_ref[0], preferred_element_type=F32, precision=hp)
    y_ref[...] = (y + d_ref[...] * u).astype(y_ref.dtype)


def _s5_sample(u, h0_re, h0_im, mats):
    b8, c8, a_re, a_im, d = mats
    n = u.shape[0]
    half = S5S_GROUPS * SSM_STATE
    return pl.pallas_call(
        _s5_sample_kernel,
        grid=(N_SSM_GROUPS // S5S_GROUPS,),
        in_specs=[pl.BlockSpec((n, LANES), lambda i: (0, i)),
                  pl.BlockSpec((n, half), lambda i: (0, i)),
                  pl.BlockSpec((n, half), lambda i: (0, i)),
                  pl.BlockSpec((1, LANES, 2 * half), lambda i: (i, 0, 0)),
                  pl.BlockSpec((1, 2 * half, LANES), lambda i: (i, 0, 0)),
                  pl.BlockSpec((1, half), lambda i: (0, i)),
                  pl.BlockSpec((1, half), lambda i: (0, i)),
                  pl.BlockSpec((1, LANES), lambda i: (0, i))],
        out_specs=[pl.BlockSpec((n, LANES), lambda i: (0, i)),
                   pl.BlockSpec((n, half), lambda i: (0, i)),
                   pl.BlockSpec((n, half), lambda i: (0, i))],
        out_shape=[jax.ShapeDtypeStruct((n, D_SSM), BF16),
                   jax.ShapeDtypeStruct((n, N_SSM_GROUPS * SSM_STATE), F32),
                   jax.ShapeDtypeStruct((n, N_SSM_GROUPS * SSM_STATE), F32)],
        compiler_params=_cparams(("parallel",)),
        name="s5_sample",
    )(u, h0_re, h0_im, b8, c8, a_re, a_im, d)


def _layer_norm(x, g, b):
    mu = jnp.mean(x, axis=-1, keepdims=True)
    xc = x - mu
    var = jnp.mean(xc * xc, axis=-1, keepdims=True)
    return xc * lax.rsqrt(var + LN_EPS) * g + b


def _sigmoid(x):
    return 0.5 * jnp.tanh(0.5 * x) + 0.5


RUN_ROWS = SUBLANES
TAB_ROWS = 3


def _merge_kernel(x_ref, oa_ref, ys_ref, carry_in_ref, wao_ref, wso_ref, wg_ref, bg_ref, wo_ref,
                  g1_ref, b1_ref, wrt_ref, brt_ref,
                  x1_ref, lpos_ref, cols_ref, tab_ref, carry_out_ref, carry_sc):
    step = pl.program_id(0)

    @pl.when(step == 0)
    def _():
        carry_sc[...] = carry_in_ref[...]

    tm = x_ref.shape[0]
    x = x_ref[...]
    branch_a = jnp.dot(oa_ref[...].astype(BF16), wao_ref[...], preferred_element_type=F32)
    z = jnp.dot(jax.nn.gelu(ys_ref[...].astype(F32)).astype(BF16), wso_ref[...], preferred_element_type=F32)
    branch_b = z[:, :D_MODEL] * _sigmoid(z[:, D_MODEL:])
    gates = _sigmoid(jnp.dot(x.astype(BF16), wg_ref[...], preferred_element_type=F32) + bg_ref[...])
    mixed = gates[:, :D_MODEL] * branch_a + gates[:, D_MODEL:] * branch_b
    mix = jnp.dot(mixed.astype(BF16), wo_ref[...], preferred_element_type=F32)
    x1 = _layer_norm(DEEPNORM_ALPHA * x + mix, g1_ref[...], b1_ref[...])
    x1_ref[...] = x1

    def split2(v):
        hi = v.astype(BF16)
        return hi, (v - hi.astype(F32)).astype(BF16)

    def dot_nt(a, b):
        return lax.dot_general(a, b, (((1,), (1,)), ((), ())), preferred_element_type=F32)

    w_hi, w_lo = split2(wrt_ref[...])
    rt = tm // tab_ref.shape[0]
    sub = lax.broadcasted_iota(jnp.int32, (N_EXPERTS, rt), 0)
    r = lax.broadcasted_iota(jnp.int32, (rt, rt), 0)
    c = lax.broadcasted_iota(jnp.int32, (rt, rt), 1)
    er = lax.broadcasted_iota(jnp.int32, (N_EXPERTS, N_EXPERTS), 0)
    ec = lax.broadcasted_iota(jnp.int32, (N_EXPERTS, N_EXPERTS), 1)
    rid = lax.broadcasted_iota(jnp.int32, (SUBLANES, LANES), 0)
    lane_pad = jnp.zeros((SUBLANES, LANES - N_EXPERTS), F32)
    for h in range(tab_ref.shape[0]):
        x_hi, x_lo = split2(x1[h * rt:(h + 1) * rt])
        logits = dot_nt(w_hi, x_hi) + dot_nt(w_hi, x_lo) + dot_nt(w_lo, x_hi) + brt_ref[...]
        work = logits
        vals, sels = [], []
        for _ in range(TOP_K):
            mx = jnp.max(work, axis=0, keepdims=True)
            idx = jnp.min(jnp.where(work == mx, sub, N_EXPERTS), axis=0, keepdims=True)
            sel = sub == idx
            vals.append(mx)
            sels.append(sel)
            work = jnp.where(sel, -jnp.inf, work)
        ex = [jnp.exp(v - vals[0]) for v in vals]
        tot = ex[0] + ex[1] + ex[2] + ex[3]
        gate_rows = jnp.concatenate([e / tot for e in ex], axis=0)

        multi = jnp.zeros((N_EXPERTS, rt), F32)
        for sel in sels:
            multi = multi + jnp.where(sel, 1.0, 0.0)
        multi_b = multi.astype(BF16)
        earlier = jnp.dot(multi_b, jnp.where(r < c, 1.0, 0.0).astype(BF16), preferred_element_type=F32)
        cnt_col = jnp.sum(multi, axis=1, keepdims=True)
        nb_col = jnp.floor((cnt_col + (RUN_ROWS - 1.0)) * (1.0 / RUN_ROWS))
        loff_col = jnp.dot(jnp.where(ec < er, 1.0, 0.0).astype(BF16),
                           jnp.broadcast_to(nb_col, (N_EXPERTS, rt)).astype(BF16), preferred_element_type=F32)
        base = RUN_ROWS * loff_col + earlier
        lpos = jnp.concatenate([jnp.sum(jnp.where(sel, base, 0.0), axis=0, keepdims=True) for sel in sels],
                               axis=0)
        lpos_ref[:, h * rt:(h + 1) * rt] = lpos.astype(jnp.int32)
        rows_hi, rows_lo = split2(jnp.concatenate([lpos, gate_rows], axis=0))
        eye = jnp.where(r == c, 1.0, 0.0).astype(BF16)
        cols_ref[h * rt:(h + 1) * rt, :] = dot_nt(eye, rows_hi) + dot_nt(eye, rows_lo)

        cnt_row = dot_nt(jnp.ones((SUBLANES, rt), BF16), multi_b)
        nb_row = jnp.floor((cnt_row + (RUN_ROWS - 1.0)) * (1.0 / RUN_ROWS))
        loff_row = jnp.dot(nb_row.astype(BF16), jnp.where(er < ec, 1.0, 0.0).astype(BF16),
                           preferred_element_type=F32)
        nb_p = jnp.concatenate([nb_row, lane_pad], axis=1)
        loff_p = jnp.concatenate([loff_row, lane_pad], axis=1)
        goff_p = carry_sc[...]
        tab = jnp.where(rid == 0, nb_p, jnp.where(rid == 1, loff_p, jnp.where(rid == 2, goff_p, 0.0)))
        tab_ref[h] = tab.astype(jnp.int32)
        carry_sc[...] = goff_p + nb_p
    carry_out_ref[...] = carry_sc[...]


def _merge(x, o_attn, y_ssm, carry_in, w, *, tile, route_tile):
    n = x.shape[0]
    nt = n // tile
    per_step = tile // route_tile
    full = lambda shape: pl.BlockSpec(shape, lambda i: (0,) * len(shape))
    return pl.pallas_call(
        _merge_kernel,
        grid=(nt,),
        in_specs=[pl.BlockSpec((tile, D_MODEL), lambda i: (i, 0)),
                  pl.BlockSpec((tile, D_ATTN), lambda i: (i, 0)),
                  pl.BlockSpec((tile, D_SSM), lambda i: (i, 0)),
                  full((SUBLANES, LANES)),
                  full((D_ATTN, D_MODEL)), full((D_SSM, 2 * D_MODEL)), full((D_MODEL, 2 * D_MODEL)),
                  full((1, 2 * D_MODEL)), full((D_MODEL, D_MODEL)),
                  full((1, D_MODEL)), full((1, D_MODEL)),
                  full((N_EXPERTS, D_MODEL)), full((N_EXPERTS, 1))],
        out_specs=[pl.BlockSpec((tile, D_MODEL), lambda i: (i, 0)),
                   pl.BlockSpec((TOP_K, tile), lambda i: (0, i)),
                   pl.BlockSpec((tile, 2 * TOP_K), lambda i: (i, 0)),
                   pl.BlockSpec((per_step, SUBLANES, LANES), lambda i: (i, 0, 0)),
                   full((SUBLANES, LANES))],
        out_shape=[jax.ShapeDtypeStruct((n, D_MODEL), F32),
                   jax.ShapeDtypeStruct((TOP_K, n), jnp.int32),
                   jax.ShapeDtypeStruct((n, 2 * TOP_K), F32),
                   jax.ShapeDtypeStruct((nt * per_step, SUBLANES, LANES), jnp.int32),
                   jax.ShapeDtypeStruct((SUBLANES, LANES), F32)],
        scratch_shapes=[pltpu.VMEM((SUBLANES, LANES), F32)],
        compiler_params=_cparams(("arbitrary",), VMEM_LIMIT_MERGE),
        name="merge",
    )(x, o_attn, y_ssm, carry_in, w["wao"], w["wso"], w["wg"], w["bg"], w["wo"], w["g1"], w["b1"],
      w["wrt"], w["brt"])


def _tab(tab_ref, tile, row, e):
    return tab_ref[(tile * TAB_ROWS + row) * N_EXPERTS + e]


BIG_PIECE = 4 * RUN_ROWS
MAX_UNITS_LOG2 = 8


def _for_each_run_piece(tab_ref, tile, fn):
    def per_expert(e, carry):
        loff = RUN_ROWS * _tab(tab_ref, tile, 1, e)
        goff = RUN_ROWS * _tab(tab_ref, tile, 2, e)
        units = _tab(tab_ref, tile, 0, e)
        n_big = lax.shift_right_logical(units, 2)

        def big(j, c2):
            fn(pl.multiple_of(loff + j * BIG_PIECE, RUN_ROWS), goff + j * BIG_PIECE, e, BIG_PIECE)
            return c2

        lax.fori_loop(0, n_big, big, 0)
        done = n_big * BIG_PIECE

        def small(j, c2):
            fn(pl.multiple_of(loff + done + j * RUN_ROWS, RUN_ROWS), goff + done + j * RUN_ROWS, e, RUN_ROWS)
            return c2

        lax.fori_loop(0, units & 3, small, 0)
        return carry

    lax.fori_loop(0, N_EXPERTS, per_expert, 0)


def _drain_units(units, wait_copy, buffer_rows):
    assert buffer_rows < (RUN_ROWS << MAX_UNITS_LOG2)
    for b in range(MAX_UNITS_LOG2):
        if (RUN_ROWS << b) > buffer_rows:
            break

        @pl.when((lax.shift_right_logical(units, b) & 1) == 1)
        def _():
            wait_copy(RUN_ROWS << b).wait()


def _dispatch_kernel(tab_ref, seg_ref, tot_ref, tail_ref, lpos_p_ref, xp_ref, lpos_s_ref, xs_in_ref, xs_ref,
                     loc_sc, zero_sc, sem, zsem):
    i = pl.program_id(0)
    last = pl.num_programs(0) - 1
    tile = i
    slot = i % 2
    loc = loc_sc.shape[1]

    @pl.when(i == 0)
    def _():
        zero_sc[...] = jnp.zeros_like(zero_sc)

        def tail_copy(e, j):
            row = pl.multiple_of(RUN_ROWS * (tail_ref[e] + j), RUN_ROWS)
            return pltpu.make_async_copy(zero_sc.at[pl.ds(0, RUN_ROWS)], xs_ref.at[pl.ds(row, RUN_ROWS)], zsem)

        def per_expert(e, carry):
            n = tail_ref[N_EXPERTS + e]
            lax.fori_loop(0, n, lambda j, c2: (tail_copy(e, j).start(), c2)[1], 0)
            lax.fori_loop(0, n, lambda j, c2: (tail_copy(e, j).wait(), c2)[1], 0)
            return carry

        lax.fori_loop(0, N_EXPERTS, per_expert, 0)

        def block_copy(b):
            row = pl.multiple_of(b * MOE_ROWS, MOE_ROWS)
            return pltpu.make_async_copy(zero_sc, xs_ref.at[pl.ds(row, MOE_ROWS)], zsem)

        first_unused, n_blocks = tail_ref[2 * N_EXPERTS], xs_ref.shape[0] // MOE_ROWS
        lax.fori_loop(first_unused, n_blocks, lambda b, c2: (block_copy(b).start(), c2)[1], 0)
        lax.fori_loop(first_unused, n_blocks, lambda b, c2: (block_copy(b).wait(), c2)[1], 0)

    def sort_tile(lpos_ref, x_ref):
        tm = x_ref.shape[0]
        rows = lax.broadcasted_iota(jnp.int32, (loc, tm), 0)
        lp = lpos_ref[...]
        onehot = jnp.zeros((loc, tm), F32)
        for k in range(TOP_K):
            onehot = jnp.where(rows == lp[k:k + 1], 1.0, onehot)
        loc_sc[slot] = jnp.dot(onehot.astype(BF16), x_ref[...].astype(BF16), preferred_element_type=F32)

    @pl.when(i < last)
    def _():
        sort_tile(lpos_p_ref, xp_ref)

    @pl.when(i == last)
    def _():
        sort_tile(lpos_s_ref, xs_in_ref)

    def piece_copy(sl, lrow, grow, e, n):
        dst = pl.multiple_of(seg_ref[e] + grow, RUN_ROWS)
        return pltpu.make_async_copy(loc_sc.at[sl, pl.ds(lrow, n)], xs_ref.at[pl.ds(dst, n)], sem.at[sl])

    _for_each_run_piece(tab_ref, tile, lambda lrow, grow, e, n: piece_copy(slot, lrow, grow, e, n).start())

    def drain(tl, sl):
        _drain_units(tot_ref[tl], lambda n: piece_copy(sl, 0, 0, 0, n), loc)

    @pl.when(i > 0)
    def _():
        drain(tile - 1, 1 - slot)

    @pl.when(i == last)
    def _():
        drain(tile, slot)


def _dispatch(tab, seg_start, tot, tails, lpos_p, x1_p, lpos_s, x1_s, *, tile, nrows):
    nt_p = x1_p.shape[0] // tile
    ns = x1_s.shape[0]
    loc = tile * TOP_K + N_EXPERTS * RUN_ROWS
    prompt_blk = lambda i, *_: jnp.minimum(i, nt_p - 1)
    return pl.pallas_call(
        _dispatch_kernel,
        grid_spec=pltpu.PrefetchScalarGridSpec(
            num_scalar_prefetch=4,
            grid=(nt_p + 1,),
            in_specs=[pl.BlockSpec((TOP_K, tile), lambda i, *_: (0, prompt_blk(i))),
                      pl.BlockSpec((tile, D_MODEL), lambda i, *_: (prompt_blk(i), 0)),
                      pl.BlockSpec((TOP_K, ns), lambda i, *_: (0, 0)),
                      pl.BlockSpec((ns, D_MODEL), lambda i, *_: (0, 0))],
            out_specs=pl.BlockSpec(memory_space=pl.ANY),
            scratch_shapes=[pltpu.VMEM((2, loc, D_MODEL), F32), pltpu.VMEM((MOE_ROWS, D_MODEL), F32),
                            pltpu.SemaphoreType.DMA((2,)), pltpu.SemaphoreType.DMA(())]),
        out_shape=jax.ShapeDtypeStruct((nrows, D_MODEL), F32),
        compiler_params=_cparams(("arbitrary",), VMEM_LIMIT),
        name="dispatch",
    )(tab, seg_start, tot, tails, lpos_p, x1_p, lpos_s, x1_s)


def _deinterleave_matrix():
    pm = np.zeros((MXU_DIM, MXU_DIM), np.float32)
    half = MXU_DIM // 2
    for c in range(half):
        pm[2 * c, c] = 1.0
        pm[2 * c + 1, half + c] = 1.0
    return pm


def _expert_kernel(be_ref, nu_ref, nv_ref, ord_ref, nxt_ref, xs_ref, w1_hbm, b1_ref, w2_hbm, b2_ref, pm_ref, y_ref,
                   w1f_sc, w2f_sc, w1p_sc, w2b_sc, sem):
    del nu_ref
    i = pl.program_id(0)
    e = be_ref[i]
    prev = be_ref[jnp.maximum(i - 1, 0)]
    nblk = 2 * D_FF // MXU_DIM

    def weight_copies(expert, slot):
        return (pltpu.make_async_copy(w1_hbm.at[expert], w1f_sc.at[slot], sem.at[0, slot]),
                pltpu.make_async_copy(w2_hbm.at[expert], w2f_sc.at[slot], sem.at[1, slot]))

    @pl.when(i == 0)
    def _():
        for cp in weight_copies(e, 0):
            cp.start()

    @pl.when((i == 0) | (e != prev))
    def _():
        slot = ord_ref[i] % 2
        for cp in weight_copies(e, slot):
            cp.wait()
        nxt = nxt_ref[i]

        @pl.when(nxt >= 0)
        def _():
            for cp in weight_copies(nxt, 1 - slot):
                cp.start()

        for cb in range(nblk):
            blk = w1f_sc[slot, :, cb * MXU_DIM:(cb + 1) * MXU_DIM].astype(BF16)
            w1p_sc[:, cb * MXU_DIM:(cb + 1) * MXU_DIM] = jnp.dot(
                blk, pm_ref[...], preferred_element_type=F32).astype(BF16)
        w2b_sc[...] = w2f_sc[slot].astype(BF16)

    for blk in range(MOE_STEP_BLOCKS):
        rows = slice(blk * MOE_ROWS, (blk + 1) * MOE_ROWS)

        @pl.when(blk < nv_ref[i])
        def _():
            x = xs_ref[rows, :].astype(BF16)
            h = jnp.dot(x, w1p_sc[...], preferred_element_type=F32) + b1_ref[0]
            half = MXU_DIM // 2
            acts = []
            for cb in range(nblk):
                x_glu = jnp.minimum(h[:, cb * MXU_DIM:cb * MXU_DIM + half], SWIGLU_LIMIT)
                x_lin = jnp.clip(h[:, cb * MXU_DIM + half:(cb + 1) * MXU_DIM], -SWIGLU_LIMIT, SWIGLU_LIMIT)
                acts.append((x_glu * jax.nn.sigmoid(SWIGLU_ALPHA * x_glu) * (x_lin + 1.0)).astype(BF16))
            act = jnp.concatenate(acts, axis=1)
            y_ref[rows, :] = jnp.dot(act, w2b_sc[...], preferred_element_type=F32) + b2_ref[0]

        @pl.when(blk >= nv_ref[i])
        def _():
            y_ref[rows, :] = jnp.zeros((MOE_ROWS, D_MODEL), F32)


def _experts(block_e, n_used, n_valid, run_ord, run_next, xs, w1, b1p, w2, b2, pm):
    nrows = xs.shape[0]
    step_rows = MOE_STEP_BLOCKS * MOE_ROWS
    nb = nrows // step_rows
    return pl.pallas_call(
        _expert_kernel,
        grid_spec=pltpu.PrefetchScalarGridSpec(
            num_scalar_prefetch=5,
            grid=(nb,),
            in_specs=[pl.BlockSpec((step_rows, D_MODEL), lambda i, be, nu, *_: (jnp.minimum(i, nu[0] - 1), 0)),
                      pl.BlockSpec(memory_space=pl.ANY),
                      pl.BlockSpec((1, 1, 2 * D_FF), lambda i, be, *_: (be[i], 0, 0)),
                      pl.BlockSpec(memory_space=pl.ANY),
                      pl.BlockSpec((1, 1, D_MODEL), lambda i, be, *_: (be[i], 0, 0)),
                      pl.BlockSpec((MXU_DIM, MXU_DIM), lambda i, *_: (0, 0))],
            out_specs=pl.BlockSpec((step_rows, D_MODEL), lambda i, *_: (i, 0)),
            scratch_shapes=[pltpu.VMEM((2, D_MODEL, 2 * D_FF), F32), pltpu.VMEM((2, D_FF, D_MODEL), F32),
                            pltpu.VMEM((D_MODEL, 2 * D_FF), BF16), pltpu.VMEM((D_FF, D_MODEL), BF16),
                            pltpu.SemaphoreType.DMA((2, 2))]),
        out_shape=jax.ShapeDtypeStruct((nrows, D_MODEL), F32),
        compiler_params=_cparams(("arbitrary",), VMEM_LIMIT_MERGE),
        name="experts",
    )(block_e, n_used, n_valid, run_ord, run_next, xs, w1, b1p, w2, b2, pm)


def _combine_kernel(tab_ref, seg_ref, tot_ref, cols_ref, x1_ref, g2_ref, b2_ref, ys_ref, y_ref, loc_sc, sem,
                    *, tile_base):
    i = pl.program_id(0)
    last = pl.num_programs(0) - 1
    tile = i + tile_base
    slot = i % 2
    loc, tm = loc_sc.shape[1], x1_ref.shape[0]

    def piece_copy(sl, lrow, grow, e, n):
        src = pl.multiple_of(seg_ref[e] + grow, RUN_ROWS)
        return pltpu.make_async_copy(ys_ref.at[pl.ds(src, n)], loc_sc.at[sl, pl.ds(lrow, n)], sem.at[sl])

    def gather(tl, sl):
        _for_each_run_piece(tab_ref, tl, lambda lrow, grow, e, n: piece_copy(sl, lrow, grow, e, n).start())

    @pl.when(i == 0)
    def _():
        loc_sc[...] = jnp.zeros_like(loc_sc)
        gather(tile, slot)

    @pl.when(i < last)
    def _():
        gather(tile + 1, 1 - slot)

    _drain_units(tot_ref[tile], lambda n: piece_copy(slot, 0, 0, 0, n), loc)

    cols = cols_ref[...]
    lane = lax.broadcasted_iota(jnp.int32, (tm, loc), 1)
    weights = jnp.zeros((tm, loc), F32)
    for k in range(TOP_K):
        weights = jnp.where(lane == cols[:, k:k + 1].astype(jnp.int32), cols[:, TOP_K + k:TOP_K + k + 1], weights)
    ffn = jnp.dot(weights.astype(BF16), loc_sc[slot].astype(BF16), preferred_element_type=F32)
    y_ref[...] = _layer_norm(DEEPNORM_ALPHA * x1_ref[...] + ffn, g2_ref[...], b2_ref[...])


def _combine(tab, seg_start, tot, cols, x1, g2, b2, ys, *, tile, tile_base):
    n = x1.shape[0]
    loc = tile * TOP_K + N_EXPERTS * RUN_ROWS
    return pl.pallas_call(
        functools.partial(_combine_kernel, tile_base=tile_base),
        grid_spec=pltpu.PrefetchScalarGridSpec(
            num_scalar_prefetch=3,
            grid=(n // tile,),
            in_specs=[pl.BlockSpec((tile, 2 * TOP_K), lambda i, *_: (i, 0)),
                      pl.BlockSpec((tile, D_MODEL), lambda i, *_: (i, 0)),
                      pl.BlockSpec((1, D_MODEL), lambda i, *_: (0, 0)),
                      pl.BlockSpec((1, D_MODEL), lambda i, *_: (0, 0)),
                      pl.BlockSpec(memory_space=pl.ANY)],
            out_specs=pl.BlockSpec((tile, D_MODEL), lambda i, *_: (i, 0)),
            scratch_shapes=[pltpu.VMEM((2, loc, D_MODEL), F32), pltpu.SemaphoreType.DMA((2,))]),
        out_shape=jax.ShapeDtypeStruct((n, D_MODEL), F32),
        compiler_params=_cparams(("arbitrary",), VMEM_LIMIT),
        name="combine",
    )(tab, seg_start, tot, cols, x1, g2, b2, ys)


def kernel(x_prompt, x_sample, cache_k_win, cache_v_win, state_ssm_re, state_ssm_im, w_in, b_in, attn_sinks,
           w_attn_out, ssm_a_re, ssm_a_im, ssm_log_dt, ssm_b_re, ssm_b_im, ssm_c_re, ssm_c_im, ssm_d, w_ssm_out,
           w_gate, b_gate, w_out, ln1_g, ln1_b, w_router, b_router, w_exp1, b_exp1, w_exp2, b_exp2, ln2_g, ln2_b):
    assert w_in.shape[0] == DEPTH == 1
    bsz, seq, _ = x_prompt.shape
    nsamp = x_sample.shape[0]
    assert x_sample.shape[1] == 1
    n_p = bsz * seq
    n_tok = n_p + nsamp

    xp = x_prompt.reshape(n_p, D_MODEL)
    xsm = x_sample.reshape(nsamp, D_MODEL)
    b_in2 = b_in[0].reshape(1, D_IN)
    sinks = attn_sinks[0].astype(F32)

    q_p, k_p, v_p, u_p = _proj(xp, w_in[0].astype(BF16), b_in2, tile=512, exact_f32=False, q_dtype=BF16)
    q_s, k_s, v_s, u_s = _proj(xsm, w_in[0], b_in2, tile=nsamp, exact_f32=True, q_dtype=F32)

    o_p = _attn_prompt(sinks, q_p.reshape(bsz, seq, D_ATTN), k_p.reshape(bsz, seq, D_KV),
                       v_p.reshape(bsz, seq, D_KV)).reshape(n_p, D_ATTN)
    k_buf = cache_k_win[0].reshape(nsamp, WINDOW, D_KV)
    v_buf = cache_v_win[0].reshape(nsamp, WINDOW, D_KV)
    o_s = _attn_sample(sinks, q_s, k_s, v_s, k_buf, v_buf)

    sp = _s5_params(ssm_a_re[0], ssm_a_im[0], ssm_log_dt[0], ssm_b_re[0], ssm_b_im[0], ssm_c_re[0], ssm_c_im[0])
    y_p, hp_re, hp_im = _s5_prompt(u_p, bsz, seq, _s5_chunk_mats(sp, ssm_d[0]))
    y_s, hs_re, hs_im = _s5_sample(u_s, state_ssm_re[0].reshape(nsamp, -1), state_ssm_im[0].reshape(nsamp, -1),
                                   _s5_sample_mats(sp, ssm_d[0]))

    wm = dict(wao=w_attn_out[0].astype(BF16), wso=w_ssm_out[0].astype(BF16), wg=w_gate[0].astype(BF16),
              bg=b_gate[0].reshape(1, -1), wo=w_out[0].astype(BF16), g1=ln1_g[0].reshape(1, -1),
              b1=ln1_b[0].reshape(1, -1), wrt=w_router[0].T, brt=b_router[0].reshape(-1, 1))
    carry0 = jnp.zeros((SUBLANES, LANES), F32)
    x1_p, lpos_p, cols_p, tab_p, carry1 = _merge(xp, o_p, y_p, carry0, wm, tile=MERGE_TILE, route_tile=TOK_TILE)
    x1_s, lpos_s, cols_s, tab_s, carry2 = _merge(xsm, o_s, y_s, carry1, wm, tile=nsamp, route_tile=nsamp)

    nt_p = n_p // TOK_TILE
    tab = jnp.concatenate([tab_p[:, :TAB_ROWS, :N_EXPERTS], tab_s[:, :TAB_ROWS, :N_EXPERTS]], axis=0)
    tot = jnp.sum(tab[:, 0, :], axis=1).astype(jnp.int32)
    tab = tab.reshape(-1)
    seg_rows = carry2[0, :N_EXPERTS].astype(jnp.int32) * RUN_ROWS
    step_rows = MOE_STEP_BLOCKS * MOE_ROWS
    padded = ((seg_rows + step_rows - 1) // step_rows) * step_rows
    pad_end = jnp.cumsum(padded)
    pad_start = (pad_end - padded).astype(jnp.int32)
    seg_end = pad_start + seg_rows
    n_runs = (nt_p + 1) * N_EXPERTS
    nb_max = (n_tok * TOP_K + n_runs * (RUN_ROWS - 1) + N_EXPERTS * (step_rows - 1) + step_rows - 1) // step_rows
    n_used = (pad_end[-1] // step_rows).astype(jnp.int32)
    tails = jnp.concatenate([seg_end // RUN_ROWS, (padded - seg_rows) // RUN_ROWS,
                             (pad_end[-1:] // MOE_ROWS)]).astype(jnp.int32)
    blk_start = jnp.arange(nb_max, dtype=jnp.int32) * step_rows
    blk_e = jnp.minimum(jnp.sum(blk_start[:, None] >= pad_end[None, :], axis=1), N_EXPERTS - 1).astype(jnp.int32)
    used = jnp.arange(nb_max) < n_used
    blk_e = jnp.where(used, blk_e, jnp.max(jnp.where(used, blk_e, 0)))
    ids = jnp.arange(N_EXPERTS, dtype=jnp.int32)
    of_blk = blk_e[:, None] == ids[None, :]
    n_valid = jnp.clip((jnp.sum(jnp.where(of_blk, seg_end[None, :], 0), axis=1) - blk_start + MOE_ROWS - 1)
                       // MOE_ROWS, 0, MOE_STEP_BLOCKS)
    n_valid = jnp.where(used, n_valid, 0).astype(jnp.int32)
    new_run = jnp.concatenate([jnp.ones((1,), jnp.int32), (blk_e[1:] != blk_e[:-1]).astype(jnp.int32)])
    run_ord = (jnp.cumsum(new_run) - 1).astype(jnp.int32)
    later = (ids[None, :] > ids[:, None]) & (padded > 0)[None, :]
    next_e = jnp.min(jnp.where(later, ids[None, :], N_EXPERTS), axis=1)
    next_e = jnp.where(next_e < N_EXPERTS, next_e, -1).astype(jnp.int32)
    run_next = jnp.sum(jnp.where(of_blk, next_e[None, :], 0), axis=1).astype(jnp.int32)

    nrows = nb_max * step_rows
    xs = _dispatch(tab, pad_start, tot, tails, lpos_p, x1_p, lpos_s, x1_s, tile=TOK_TILE, nrows=nrows)

    b1p = b_exp1[0].reshape(N_EXPERTS, 2 * D_FF // MXU_DIM, MXU_DIM // 2, 2)
    b1p = jnp.swapaxes(b1p, 2, 3).reshape(N_EXPERTS, 1, 2 * D_FF)
    ys = _experts(blk_e, n_used.reshape(1), n_valid, run_ord, run_next, xs, w_exp1[0], b1p, w_exp2[0],
                  b_exp2[0].reshape(N_EXPERTS, 1, D_MODEL),
                  jnp.asarray(_deinterleave_matrix(), BF16))

    g2, b2 = ln2_g[0].reshape(1, -1), ln2_b[0].reshape(1, -1)
    y_prompt = _combine(tab, pad_start, tot, cols_p, x1_p, g2, b2, ys, tile=TOK_TILE, tile_base=0)
    y_sample = _combine(tab, pad_start, tot, cols_s, x1_s, g2, b2, ys, tile=nsamp, tile_base=nt_p)

    k_p4 = k_p.reshape(bsz, seq, D_KV)[:, -WINDOW:].reshape(bsz, WINDOW, N_KV_HEADS, HEAD_DIM)
    v_p4 = v_p.reshape(bsz, seq, D_KV)[:, -WINDOW:].reshape(bsz, WINDOW, N_KV_HEADS, HEAD_DIM)
    k_s4 = jnp.concatenate([cache_k_win[0][:, 1:], k_s.reshape(nsamp, 1, N_KV_HEADS, HEAD_DIM)], axis=1)
    v_s4 = jnp.concatenate([cache_v_win[0][:, 1:], v_s.reshape(nsamp, 1, N_KV_HEADS, HEAD_DIM)], axis=1)
    st = lambda a, n: a.reshape(1, n, N_SSM_GROUPS, SSM_STATE)
    return (y_prompt.reshape(bsz, seq, D_MODEL), y_sample.reshape(nsamp, 1, D_MODEL),
            k_p4[None], v_p4[None], st(hp_re, bsz), st(hp_im, bsz),
            k_s4[None], v_s4[None], st(hs_re, nsamp), st(hs_im, nsamp))
```

```python
import functools
import math

import numpy as np
import jax
import jax.numpy as jnp
from jax import lax
from jax.experimental import pallas as pl
from jax.experimental.pallas import tpu as pltpu

F32 = jnp.float32
BF16 = jnp.bfloat16

D_MODEL = 1024
HEAD_DIM = 64
N_Q_HEADS = 8
N_KV_HEADS = 2
Q_PER_KV = N_Q_HEADS // N_KV_HEADS
D_ATTN = N_Q_HEADS * HEAD_DIM
D_KV = N_KV_HEADS * HEAD_DIM
WINDOW = 128
ATTN_SCALE = HEAD_DIM ** -0.5
SSM_GROUP = 16
D_SSM = D_MODEL // 2
N_SSM_GROUPS = D_SSM // SSM_GROUP
SSM_STATE = 64
D_IN = D_ATTN + 2 * D_KV + D_SSM
N_EXPERTS = 32
TOP_K = 4
D_FF = D_MODEL
SWIGLU_LIMIT = 7.0
SWIGLU_ALPHA = 1.702
LN_EPS = 1e-5
DEPTH = 1
DEEPNORM_ALPHA = (2 * DEPTH) ** 0.25

LANES = 128
SUBLANES = 8
MXU_DIM = 256

S5_CHUNK = MXU_DIM // SSM_GROUP
S5_LANE_GROUPS = LANES // SSM_GROUP
MOE_ROWS = 256
MOE_STEP_BLOCKS = 2
TOK_TILE = 256
MERGE_TILE = 512
VMEM_LIMIT = 48 * 1024 * 1024
VMEM_LIMIT_MERGE = 56 * 1024 * 1024


def _cparams(sem, vmem=None):
    return pltpu.CompilerParams(dimension_semantics=sem, vmem_limit_bytes=vmem)


def _proj_kernel(x_ref, w_ref, b_ref, q_ref, k_ref, v_ref, u_ref, *, exact_f32):
    if exact_f32:
        h = jnp.dot(x_ref[...], w_ref[...], preferred_element_type=F32, precision=lax.Precision.HIGHEST)
    else:
        h = jnp.dot(x_ref[...].astype(BF16), w_ref[...], preferred_element_type=F32)
    h = h + b_ref[...]
    q_ref[...] = (h[:, :D_ATTN] * ATTN_SCALE).astype(q_ref.dtype)
    k_ref[...] = h[:, D_ATTN:D_ATTN + D_KV]
    v_ref[...] = h[:, D_ATTN + D_KV:D_ATTN + 2 * D_KV]
    u_ref[...] = h[:, D_ATTN + 2 * D_KV:].astype(u_ref.dtype)


def _proj(x, w, b, *, tile, exact_f32, q_dtype):
    n = x.shape[0]
    return pl.pallas_call(
        functools.partial(_proj_kernel, exact_f32=exact_f32),
        grid=(n // tile,),
        in_specs=[pl.BlockSpec((tile, D_MODEL), lambda i: (i, 0)),
                  pl.BlockSpec((D_MODEL, D_IN), lambda i: (0, 0)),
                  pl.BlockSpec((1, D_IN), lambda i: (0, 0))],
        out_specs=[pl.BlockSpec((tile, D_ATTN), lambda i: (i, 0)),
                   pl.BlockSpec((tile, D_KV), lambda i: (i, 0)),
                   pl.BlockSpec((tile, D_KV), lambda i: (i, 0)),
                   pl.BlockSpec((tile, D_SSM), lambda i: (i, 0))],
        out_shape=[jax.ShapeDtypeStruct((n, D_ATTN), q_dtype),
                   jax.ShapeDtypeStruct((n, D_KV), F32),
                   jax.ShapeDtypeStruct((n, D_KV), F32),
                   jax.ShapeDtypeStruct((n, D_SSM), F32)],
        compiler_params=_cparams(("parallel",)),
        name="proj",
    )(x, w, b)


ATT_Q_TILE = 512


def _attn_prompt_kernel(sink_ref, q_ref, k_ref, v_ref, o_ref):
    i = pl.program_id(1)
    nk, nq = 2 * WINDOW, 2 * WINDOW
    lo = lax.broadcasted_iota(jnp.int32, (nk, LANES), 1) < HEAD_DIM
    top = lax.broadcasted_iota(jnp.int32, (nq, 1), 0) < WINDOW
    for blk in range(ATT_Q_TILE // WINDOW):
        q0 = i * ATT_Q_TILE + blk * WINDOW
        k0 = pl.multiple_of(jnp.maximum(q0 - WINDOW, 0), WINDOW)
        kk = k_ref[0, pl.ds(k0, nk), :]
        vv = v_ref[0, pl.ds(k0, nk), :]
        kk_sw = pltpu.roll(kk, HEAD_DIM, axis=1)
        vv_sw = pltpu.roll(vv, HEAD_DIM, axis=1)
        k_var = [[jnp.where(lo, kk, 0.0).astype(BF16), jnp.where(lo, 0.0, kk_sw).astype(BF16)],
                 [jnp.where(lo, kk_sw, 0.0).astype(BF16), jnp.where(lo, 0.0, kk).astype(BF16)]]
        v_var = [[jnp.where(lo, vv, 1.0).astype(BF16), jnp.where(lo, 1.0, vv_sw).astype(BF16)],
                 [jnp.where(lo, vv_sw, 1.0).astype(BF16), jnp.where(lo, 1.0, vv).astype(BF16)]]
        qpos = q0 + lax.broadcasted_iota(jnp.int32, (nq, nk), 0) % WINDOW
        kpos = k0 + lax.broadcasted_iota(jnp.int32, (nq, nk), 1)
        valid = (kpos <= qpos) & (qpos - kpos <= WINDOW)
        rows = slice(blk * WINDOW, (blk + 1) * WINDOW)
        for kv in range(N_KV_HEADS):
            pairs = (2 * kv, 2 * kv + 1)
            qs = jnp.concatenate([q_ref[0, rows, pr * LANES:(pr + 1) * LANES] for pr in pairs], axis=0)
            outs = []
            for parity in range(2):
                sink = jnp.where(top, sink_ref[2 * pairs[0] + parity], sink_ref[2 * pairs[1] + parity])
                s = lax.dot_general(qs, k_var[kv][parity], (((1,), (1,)), ((), ())), preferred_element_type=F32)
                s = jnp.where(valid, s, -jnp.inf)
                m = jnp.maximum(jnp.max(s, axis=-1, keepdims=True), sink)
                p = jnp.exp(s - m).astype(BF16)
                acc = jnp.dot(p, v_var[kv][parity], preferred_element_type=F32)
                outs.append(acc / (pltpu.roll(acc, HEAD_DIM, axis=1) + jnp.exp(sink - m)))
            o = jnp.where(lo, outs[0], outs[1]).astype(o_ref.dtype)
            for j, pr in enumerate(pairs):
                o_ref[0, rows, pr * LANES:(pr + 1) * LANES] = o[j * WINDOW:(j + 1) * WINDOW]


def _attn_prompt(sinks, q, k, v):
    bsz, seq = q.shape[0], q.shape[1]
    return pl.pallas_call(
        _attn_prompt_kernel,
        grid=(bsz, seq // ATT_Q_TILE),
        in_specs=[pl.BlockSpec(memory_space=pltpu.SMEM),
                  pl.BlockSpec((1, ATT_Q_TILE, D_ATTN), lambda b, i: (b, i, 0)),
                  pl.BlockSpec((1, seq, D_KV), lambda b, i: (b, 0, 0)),
                  pl.BlockSpec((1, seq, D_KV), lambda b, i: (b, 0, 0))],
        out_specs=pl.BlockSpec((1, ATT_Q_TILE, D_ATTN), lambda b, i: (b, i, 0)),
        out_shape=jax.ShapeDtypeStruct((bsz, seq, D_ATTN), BF16),
        compiler_params=_cparams(("parallel", "parallel")),
        name="attn_prompt",
    )(sinks, q, k, v)


ATT_S_GROUP = 16


def _attn_sample_kernel(sink_ref, q_ref, kn_ref, vn_ref, kb_ref, vb_ref, o_ref):
    g = ATT_S_GROUP
    rows = Q_PER_KV * g
    ncol = g * WINDOW
    kb = kb_ref[...].reshape(ncol, D_KV).astype(BF16)
    vb = vb_ref[...].reshape(ncol, D_KV).astype(BF16)
    rseq = lax.broadcasted_iota(jnp.int32, (rows, ncol), 0) % g
    cseq = lax.broadcasted_iota(jnp.int32, (rows, ncol), 1) // WINDOW
    own = rseq == cseq
    rhead = lax.broadcasted_iota(jnp.int32, (rows, 1), 0) // g
    for kv in range(N_KV_HEADS):
        lo = kv * HEAD_DIM
        qs = jnp.concatenate(
            [q_ref[:, (kv * Q_PER_KV + h) * HEAD_DIM:(kv * Q_PER_KV + h + 1) * HEAD_DIM] for h in range(Q_PER_KV)],
            axis=0)
        kn = jnp.concatenate([kn_ref[:, lo:lo + HEAD_DIM]] * Q_PER_KV, axis=0)
        vn = jnp.concatenate([vn_ref[:, lo:lo + HEAD_DIM]] * Q_PER_KV, axis=0)
        sink = jnp.zeros((rows, 1), F32)
        for h in range(Q_PER_KV):
            sink = jnp.where(rhead == h, sink_ref[kv * Q_PER_KV + h], sink)
        qs = qs.astype(BF16)
        s = lax.dot_general(qs, kb[:, lo:lo + HEAD_DIM], (((1,), (1,)), ((), ())), preferred_element_type=F32)
        s = jnp.where(own, s, -jnp.inf)
        s_new = jnp.sum(qs.astype(F32) * kn.astype(BF16).astype(F32), axis=-1, keepdims=True)
        m = jnp.maximum(jnp.maximum(jnp.max(s, axis=-1, keepdims=True), s_new), sink)
        p = jnp.exp(s - m)
        p_new = jnp.exp(s_new - m)
        denom = jnp.sum(p, axis=-1, keepdims=True) + p_new + jnp.exp(sink - m)
        o = jnp.dot(p.astype(BF16), vb[:, lo:lo + HEAD_DIM], preferred_element_type=F32)
        o = (o + p_new.astype(BF16).astype(F32) * vn.astype(BF16).astype(F32)) / denom
        for h in range(Q_PER_KV):
            c0 = (kv * Q_PER_KV + h) * HEAD_DIM
            o_ref[:, c0:c0 + HEAD_DIM] = o[h * g:(h + 1) * g].astype(o_ref.dtype)


def _attn_sample(sinks, q, k_new, v_new, k_buf, v_buf):
    n = q.shape[0]
    g = ATT_S_GROUP
    return pl.pallas_call(
        _attn_sample_kernel,
        grid=(n // g,),
        in_specs=[pl.BlockSpec(memory_space=pltpu.SMEM),
                  pl.BlockSpec((g, D_ATTN), lambda i: (i, 0)),
                  pl.BlockSpec((g, D_KV), lambda i: (i, 0)),
                  pl.BlockSpec((g, D_KV), lambda i: (i, 0)),
                  pl.BlockSpec((g, WINDOW, D_KV), lambda i: (i, 0, 0)),
                  pl.BlockSpec((g, WINDOW, D_KV), lambda i: (i, 0, 0))],
        out_specs=pl.BlockSpec((g, D_ATTN), lambda i: (i, 0)),
        out_shape=jax.ShapeDtypeStruct((n, D_ATTN), F32),
        compiler_params=_cparams(("parallel",)),
        name="attn_sample",
    )(sinks, q, k_new, v_new, k_buf, v_buf)


def _s5_params(a_re, a_im, log_dt, b_re, b_im, c_re, c_im):
    hp = lax.Precision.HIGHEST
    dt = jnp.exp(log_dt.astype(F32))[:, None]
    are, aim = a_re.astype(F32), a_im.astype(F32)
    tau = jnp.arange(S5_CHUNK + 1, dtype=F32)[None, :, None]
    mag = jnp.exp(tau * (dt * are)[:, None, :])
    ang = tau * (dt * aim)[:, None, :]
    pw_re, pw_im = mag * jnp.cos(ang), mag * jnp.sin(ang)
    ab_re, ab_im = pw_re[:, 1], pw_im[:, 1]
    den = are * are + aim * aim
    f_re = ((ab_re - 1.0) * are + ab_im * aim) / den
    f_im = (ab_im * are - (ab_re - 1.0) * aim) / den
    bre, bim = b_re.astype(F32), b_im.astype(F32)
    bb_re = f_re[..., None] * bre - f_im[..., None] * bim
    bb_im = f_re[..., None] * bim + f_im[..., None] * bre
    cre, cim = c_re.astype(F32), c_im.astype(F32)
    return dict(pw_re=pw_re, pw_im=pw_im, ab_re=ab_re, ab_im=ab_im, bb_re=bb_re, bb_im=bb_im,
                c_re=cre, c_im=cim, hp=hp)


def _s5_chunk_mats(sp, d_skip):
    hp = sp["hp"]
    g, t, c, p = N_SSM_GROUPS, S5_CHUNK, SSM_GROUP, SSM_STATE
    pw_re, pw_im = sp["pw_re"], sp["pw_im"]
    ca_re = sp["c_re"][:, None] * pw_re[:, :, None, :] - sp["c_im"][:, None] * pw_im[:, :, None, :]
    ca_im = sp["c_re"][:, None] * pw_im[:, :, None, :] + sp["c_im"][:, None] * pw_re[:, :, None, :]
    kern = (jnp.einsum("gtcp,gpd->gtcd", ca_re[:, :t], sp["bb_re"], precision=hp)
            - jnp.einsum("gtcp,gpd->gtcd", ca_im[:, :t], sp["bb_im"], precision=hp))
    kc = jnp.swapaxes(kern, 2, 3)
    kc = kc.at[:, 0].add(d_skip.astype(F32).reshape(g, 1, c) * jnp.eye(c, dtype=F32)[None])
    rev_re, rev_im = pw_re[:, t - 1::-1][:, :t], pw_im[:, t - 1::-1][:, :t]
    wst_re = rev_re[:, :, None, :] * jnp.swapaxes(sp["bb_re"], 1, 2)[:, None] \
        - rev_im[:, :, None, :] * jnp.swapaxes(sp["bb_im"], 1, 2)[:, None]
    wst_im = rev_re[:, :, None, :] * jnp.swapaxes(sp["bb_im"], 1, 2)[:, None] \
        + rev_im[:, :, None, :] * jnp.swapaxes(sp["bb_re"], 1, 2)[:, None]
    wo_re = jnp.transpose(ca_re[:, 1:t + 1], (0, 3, 1, 2))
    wo_im = -jnp.transpose(ca_im[:, 1:t + 1], (0, 3, 1, 2))
    nv, gl = g // S5_LANE_GROUPS, S5_LANE_GROUPS
    kc, wst_re, wst_im, wo_re, wo_im = lax.optimization_barrier((kc, wst_re, wst_im, wo_re, wo_im))
    kc5 =jnp.transpose(kc.reshape(nv, gl, t, c, c), (0, 2, 1, 3, 4))
    ws6 = jnp.transpose(jnp.stack([wst_re, wst_im], axis=3).reshape(nv, gl, t, c, 2, p),
                        (0, 2, 1, 3, 4, 5))
    wo6 = jnp.transpose(jnp.stack([wo_re, wo_im], axis=0).reshape(2, nv, gl, p, t, c),
                        (1, 0, 2, 3, 4, 5))
    kc5, ws6, wo6 = lax.optimization_barrier((kc5.astype(BF16), ws6.astype(BF16), wo6.astype(BF16)))
    spread_b = np.zeros((c, LANES), np.float32)
    spread_s = np.zeros((2 * p, 2 * gl * p), np.float32)
    spread_o = np.zeros((t * c, t * LANES), np.float32)
    for h in range(gl):
        spread_b[np.arange(c), h * c + np.arange(c)] = 1.0
        for ri in range(2):
            spread_s[ri * p + np.arange(p), ri * gl * p + h * p + np.arange(p)] = 1.0
        for tt in range(t):
            spread_o[tt * c + np.arange(c), tt * LANES + h * c + np.arange(c)] = 1.0
    at_re = pw_re[:, t].reshape(1, g * p)
    at_im = pw_im[:, t].reshape(1, g * p)
    return dict(kc=kc5.reshape(nv, t, LANES, c), ws=ws6.reshape(nv, t * LANES, 2 * p),
                wo=wo6.reshape(nv, 2 * gl * p, t * c), spread_b=jnp.asarray(spread_b, BF16),
                spread_s=jnp.asarray(spread_s, BF16), spread_o=jnp.asarray(spread_o, BF16),
                at_re=at_re, at_im=at_im)


def _s5_chunk_rows(u_ref, nchunk):
    return jnp.concatenate(
        [u_ref[pl.ds(s, nchunk, stride=S5_CHUNK), :] for s in range(S5_CHUNK)], axis=1).astype(BF16)


S5_SLABS = S5_LANE_GROUPS * SSM_STATE // LANES


S5_EXPAND_ROWS = 256
S5_C_SHIFT = SSM_GROUP.bit_length() - 1
S5_P_SHIFT = SSM_STATE.bit_length() - 1


def _s5_expand(dst_ref, compact_ref, spread_ref, row_shift, col_shift):
    n_rows, n_cols = dst_ref.shape
    col_g = lax.shift_right_logical(lax.broadcasted_iota(jnp.int32, (S5_EXPAND_ROWS, n_cols), 1), col_shift)
    for r0 in range(0, n_rows, S5_EXPAND_ROWS):
        row_g = lax.shift_right_logical(r0 + lax.broadcasted_iota(jnp.int32, (S5_EXPAND_ROWS, n_cols), 0), row_shift)
        same = ((row_g ^ col_g) & (S5_LANE_GROUPS - 1)) == 0
        blk = jnp.dot(compact_ref[r0:r0 + S5_EXPAND_ROWS, :], spread_ref[...], preferred_element_type=F32)
        dst_ref[r0:r0 + S5_EXPAND_ROWS, :] = jnp.where(same, blk, 0.0).astype(dst_ref.dtype)


def _s5_state_kernel(u_ref, ws_ref, spread_ref, sre_ref, sim_ref, wst_sc):
    nchunk = sre_ref.shape[1]

    @pl.when(pl.program_id(1) == 0)
    def _():
        _s5_expand(wst_sc, ws_ref.at[0], spread_ref, S5_C_SHIFT, S5_P_SHIFT)

    s = jnp.dot(_s5_chunk_rows(u_ref, nchunk), wst_sc[...], preferred_element_type=F32)
    for k in range(S5_SLABS):
        sre_ref[k] = s[:, k * LANES:(k + 1) * LANES]
        sim_ref[k] = s[:, (S5_SLABS + k) * LANES:(S5_SLABS + k + 1) * LANES]


def _s5_scan_kernel(sre_ref, sim_ref, are_ref, aim_ref, hre_ref, him_ref, fre_ref, fim_ref, *, bsz):
    nchunk = sre_ref.shape[1] // bsz
    are = [jnp.broadcast_to(are_ref[:, k * LANES:(k + 1) * LANES], (bsz, LANES)) for k in range(S5_SLABS)]
    aim = [jnp.broadcast_to(aim_ref[:, k * LANES:(k + 1) * LANES], (bsz, LANES)) for k in range(S5_SLABS)]

    def body(j, carry):
        rows = pl.ds(j, bsz, stride=nchunk)
        out = []
        for k in range(S5_SLABS):
            cre, cim = carry[2 * k], carry[2 * k + 1]
            hre_ref[k, rows, :] = cre
            him_ref[k, rows, :] = cim
            sr = sre_ref[k, rows, :]
            si = sim_ref[k, rows, :]
            out += [are[k] * cre - aim[k] * cim + sr, are[k] * cim + aim[k] * cre + si]
        return tuple(out)

    zero = jnp.zeros((bsz, LANES), F32)
    fin = lax.fori_loop(0, nchunk, body, (zero,) * (2 * S5_SLABS), unroll=4)
    fre_ref[...] = jnp.concatenate(fin[0::2], axis=1)
    fim_ref[...] = jnp.concatenate(fin[1::2], axis=1)


def _s5_out_kernel(u_ref, kc_ref, spread_b_ref, hre_ref, him_ref, wo_ref, spread_o_ref, y_ref, m_sc, wout_sc):
    nchunk = hre_ref.shape[1]

    @pl.when(pl.program_id(1) == 0)
    def _():
        rg = lax.shift_right_logical(lax.broadcasted_iota(jnp.int32, (LANES, LANES), 0), S5_C_SHIFT)
        cg = lax.shift_right_logical(lax.broadcasted_iota(jnp.int32, (LANES, LANES), 1), S5_C_SHIFT)
        zero_blk = jnp.zeros((LANES, LANES), BF16)
        lag_blk = [jnp.where(rg == cg, jnp.dot(kc_ref[0, tau], spread_b_ref[...], preferred_element_type=F32),
                             0.0).astype(BF16) for tau in range(S5_CHUNK)]
        for s in range(S5_CHUNK):
            for t in range(S5_CHUNK):
                m_sc[s * LANES:(s + 1) * LANES, t * LANES:(t + 1) * LANES] = lag_blk[t - s] if t >= s else zero_blk
        _s5_expand(wout_sc, wo_ref.at[0], spread_o_ref, S5_P_SHIFT, S5_C_SHIFT)

    hcat = jnp.concatenate([hre_ref[k] for k in range(S5_SLABS)] + [him_ref[k] for k in range(S5_SLABS)],
                           axis=1).astype(BF16)
    lhs = _s5_chunk_rows(u_ref, nchunk)
    y = jnp.concatenate(
        [jnp.dot(lhs[:, :j + MXU_DIM], m_sc[:j + MXU_DIM, j:j + MXU_DIM], preferred_element_type=F32)
         for j in range(0, S5_CHUNK * LANES, MXU_DIM)], axis=1)
    y = y + jnp.dot(hcat, wout_sc[...], preferred_element_type=F32)
    for s in range(S5_CHUNK):
        y_ref[pl.ds(s, nchunk, stride=S5_CHUNK), :] = y[:, s * LANES:(s + 1) * LANES]


def _s5_prompt(u, bsz, seq, mats):
    at_re, at_im = mats["at_re"], mats["at_im"]
    g, t, p, c = N_SSM_GROUPS, S5_CHUNK, SSM_STATE, SSM_GROUP
    nchunk = seq // t
    n = nchunk * bsz
    nv = g // S5_LANE_GROUPS
    half = S5_LANE_GROUPS * p
    s_re, s_im = pl.pallas_call(
        _s5_state_kernel,
        grid=(nv, bsz),
        in_specs=[pl.BlockSpec((seq, LANES), lambda v, b: (b, v)),
                  pl.BlockSpec((1, t * LANES, 2 * p), lambda v, b: (v, 0, 0)),
                  pl.BlockSpec((2 * p, 2 * half), lambda v, b: (0, 0))],
        out_specs=[pl.BlockSpec((S5_SLABS, nchunk, LANES), lambda v, b: (v, b, 0)),
                   pl.BlockSpec((S5_SLABS, nchunk, LANES), lambda v, b: (v, b, 0))],
        out_shape=[jax.ShapeDtypeStruct((nv * S5_SLABS, n, LANES), F32)] * 2,
        scratch_shapes=[pltpu.VMEM((t * LANES, 2 * half), BF16)],
        compiler_params=_cparams(("parallel", "arbitrary"), VMEM_LIMIT),
        name="s5_state",
    )(u, mats["ws"], mats["spread_s"])
    h_re, h_im, f_re, f_im = pl.pallas_call(
        functools.partial(_s5_scan_kernel, bsz=bsz),
        grid=(nv,),
        in_specs=[pl.BlockSpec((S5_SLABS, n, LANES), lambda i: (i, 0, 0)),
                  pl.BlockSpec((S5_SLABS, n, LANES), lambda i: (i, 0, 0)),
                  pl.BlockSpec((1, half), lambda i: (0, i)),
                  pl.BlockSpec((1, half), lambda i: (0, i))],
        out_specs=[pl.BlockSpec((S5_SLABS, n, LANES), lambda i: (i, 0, 0)),
                   pl.BlockSpec((S5_SLABS, n, LANES), lambda i: (i, 0, 0)),
                   pl.BlockSpec((bsz, half), lambda i: (0, i)),
                   pl.BlockSpec((bsz, half), lambda i: (0, i))],
        out_shape=[jax.ShapeDtypeStruct((nv * S5_SLABS, n, LANES), F32)] * 2
        + [jax.ShapeDtypeStruct((bsz, g * p), F32)] * 2,
        compiler_params=_cparams(("parallel",)),
        name="s5_scan",
    )(s_re, s_im, at_re, at_im)
    y = pl.pallas_call(
        _s5_out_kernel,
        grid=(nv, bsz),
        in_specs=[pl.BlockSpec((seq, LANES), lambda v, b: (b, v)),
                  pl.BlockSpec((1, t, LANES, c), lambda v, b: (v, 0, 0, 0)),
                  pl.BlockSpec((c, LANES), lambda v, b: (0, 0)),
                  pl.BlockSpec((S5_SLABS, nchunk, LANES), lambda v, b: (v, b, 0)),
                  pl.BlockSpec((S5_SLABS, nchunk, LANES), lambda v, b: (v, b, 0)),
                  pl.BlockSpec((1, 2 * half, t * c), lambda v, b: (v, 0, 0)),
                  pl.BlockSpec((t * c, t * LANES), lambda v, b: (0, 0))],
        out_specs=pl.BlockSpec((seq, LANES), lambda v, b: (b, v)),
        out_shape=jax.ShapeDtypeStruct((bsz * seq, D_SSM), F32),
        scratch_shapes=[pltpu.VMEM((t * LANES, t * LANES), BF16), pltpu.VMEM((2 * half, t * LANES), BF16)],
        compiler_params=_cparams(("parallel", "arbitrary"), VMEM_LIMIT),
        name="s5_out",
    )(u, mats["kc"], mats["spread_b"], h_re, h_im, mats["wo"], mats["spread_o"])
    return y, f_re, f_im


S5S_GROUPS = LANES // SSM_GROUP


def _s5_sample_mats(sp, d_skip):
    go, gl, c, p = N_SSM_GROUPS // S5S_GROUPS, S5S_GROUPS, SSM_GROUP, SSM_STATE
    eye = jnp.eye(gl, dtype=F32)

    def bdiag_in(b):
        b4 = b.reshape(go, gl, p, c)
        return jnp.einsum("ogpc,gh->ogchp", b4, eye).reshape(go, gl * c, gl * p)

    def bdiag_out(cm):
        c4 = cm.reshape(go, gl, c, p)
        return jnp.einsum("ogcp,gh->ogphc", c4, eye).reshape(go, gl * p, gl * c)

    b8 = jnp.concatenate([bdiag_in(sp["bb_re"]), bdiag_in(sp["bb_im"])], axis=2)
    c8 = jnp.concatenate([bdiag_out(sp["c_re"]), -bdiag_out(sp["c_im"])], axis=1)
    a_re = sp["ab_re"].reshape(1, N_SSM_GROUPS * p)
    a_im = sp["ab_im"].reshape(1, N_SSM_GROUPS * p)
    return b8, c8, a_re, a_im, d_skip.astype(F32).reshape(1, D_SSM)


def _s5_sample_kernel(u_ref, hre_ref, him_ref, b8_ref, c8_ref, are_ref, aim_ref, d_ref,
                      y_ref, ore_ref, oim_ref):
    hp = lax.Precision.HIGHEST
    u = u_ref[...]
    half = S5S_GROUPS * SSM_STATE
    bu = jnp.dot(u, b8_ref[0], preferred_element_type=F32, precision=hp)
    are, aim = are_ref[...], aim_ref[...]
    h0r, h0i = hre_ref[...], him_ref[...]
    hr = are * h0r - aim * h0i + bu[:, :half]
    hi = are * h0i + aim * h0r + bu[:, half:]
    ore_ref[...] = hr
    oim_ref[...] = hi
    y = jnp.dot(jnp.concatenate([hr, hi], axis=1), c8_ref[0], preferred_element_type=F32, precision=hp)
    y_ref[...] = (y + d_ref[...] * u).astype(y_ref.dtype)


def _s5_sample(u, h0_re, h0_im, mats):
    b8, c8, a_re, a_im, d = mats
    n = u.shape[0]
    half = S5S_GROUPS * SSM_STATE
    return pl.pallas_call(
        _s5_sample_kernel,
        grid=(N_SSM_GROUPS // S5S_GROUPS,),
        in_specs=[pl.BlockSpec((n, LANES), lambda i: (0, i)),
                  pl.BlockSpec((n, half), lambda i: (0, i)),
                  pl.BlockSpec((n, half), lambda i: (0, i)),
                  pl.BlockSpec((1, LANES, 2 * half), lambda i: (i, 0, 0)),
                  pl.BlockSpec((1, 2 * half, LANES), lambda i: (i, 0, 0)),
                  pl.BlockSpec((1, half), lambda i: (0, i)),
                  pl.BlockSpec((1, half), lambda i: (0, i)),
                  pl.BlockSpec((1, LANES), lambda i: (0, i))],
        out_specs=[pl.BlockSpec((n, LANES), lambda i: (0, i)),
                   pl.BlockSpec((n, half), lambda i: (0, i)),
                   pl.BlockSpec((n, half), lambda i: (0, i))],
        out_shape=[jax.ShapeDtypeStruct((n, D_SSM), BF16),
                   jax.ShapeDtypeStruct((n, N_SSM_GROUPS * SSM_STATE), F32),
                   jax.ShapeDtypeStruct((n, N_SSM_GROUPS * SSM_STATE), F32)],
        compiler_params=_cparams(("parallel",)),
        name="s5_sample",
    )(u, h0_re, h0_im, b8, c8, a_re, a_im, d)


def _layer_norm(x, g, b):
    mu = jnp.mean(x, axis=-1, keepdims=True)
    xc = x - mu
    var = jnp.mean(xc * xc, axis=-1, keepdims=True)
    return xc * lax.rsqrt(var + LN_EPS) * g + b


def _sigmoid(x):
    return 0.5 * jnp.tanh(0.5 * x) + 0.5


RUN_ROWS = SUBLANES
TAB_ROWS = 3


def _merge_kernel(x_ref, oa_ref, ys_ref, carry_in_ref, wao_ref, wso_ref, wg_ref, bg_ref, wo_ref,
                  g1_ref, b1_ref, wrt_ref, brt_ref,
                  x1_ref, lpos_ref, cols_ref, tab_ref, carry_out_ref, carry_sc):
    step = pl.program_id(0)

    @pl.when(step == 0)
    def _():
        carry_sc[...] = carry_in_ref[...]

    tm = x_ref.shape[0]
    x = x_ref[...]
    branch_a = jnp.dot(oa_ref[...].astype(BF16), wao_ref[...], preferred_element_type=F32)
    z = jnp.dot(jax.nn.gelu(ys_ref[...].astype(F32)).astype(BF16), wso_ref[...], preferred_element_type=F32)
    branch_b = z[:, :D_MODEL] * _sigmoid(z[:, D_MODEL:])
    gates = _sigmoid(jnp.dot(x.astype(BF16), wg_ref[...], preferred_element_type=F32) + bg_ref[...])
    mixed = gates[:, :D_MODEL] * branch_a + gates[:, D_MODEL:] * branch_b
    mix = jnp.dot(mixed.astype(BF16), wo_ref[...], preferred_element_type=F32)
    x1 = _layer_norm(DEEPNORM_ALPHA * x + mix, g1_ref[...], b1_ref[...])
    x1_ref[...] = x1

    def split2(v):
        hi = v.astype(BF16)
        return hi, (v - hi.astype(F32)).astype(BF16)

    def dot_nt(a, b):
        return lax.dot_general(a, b, (((1,), (1,)), ((), ())), preferred_element_type=F32)

    w_hi, w_lo = split2(wrt_ref[...])
    rt = tm // tab_ref.shape[0]
    sub = lax.broadcasted_iota(jnp.int32, (N_EXPERTS, rt), 0)
    r = lax.broadcasted_iota(jnp.int32, (rt, rt), 0)
    c = lax.broadcasted_iota(jnp.int32, (rt, rt), 1)
    er = lax.broadcasted_iota(jnp.int32, (N_EXPERTS, N_EXPERTS), 0)
    ec = lax.broadcasted_iota(jnp.int32, (N_EXPERTS, N_EXPERTS), 1)
    rid = lax.broadcasted_iota(jnp.int32, (SUBLANES, LANES), 0)
    lane_pad = jnp.zeros((SUBLANES, LANES - N_EXPERTS), F32)
    for h in range(tab_ref.shape[0]):
        x_hi, x_lo = split2(x1[h * rt:(h + 1) * rt])
        logits = dot_nt(w_hi, x_hi) + dot_nt(w_hi, x_lo) + dot_nt(w_lo, x_hi) + brt_ref[...]
        work = logits
        vals, sels = [], []
        for _ in range(TOP_K):
            mx = jnp.max(work, axis=0, keepdims=True)
            idx = jnp.min(jnp.where(work == mx, sub, N_EXPERTS), axis=0, keepdims=True)
            sel = sub == idx
            vals.append(mx)
            sels.append(sel)
            work = jnp.where(sel, -jnp.inf, work)
        ex = [jnp.exp(v - vals[0]) for v in vals]
        tot = ex[0] + ex[1] + ex[2] + ex[3]
        gate_rows = jnp.concatenate([e / tot for e in ex], axis=0)

        multi = jnp.zeros((N_EXPERTS, rt), F32)
        for sel in sels:
            multi = multi + jnp.where(sel, 1.0, 0.0)
        multi_b = multi.astype(BF16)
        earlier = jnp.dot(multi_b, jnp.where(r < c, 1.0, 0.0).astype(BF16), preferred_element_type=F32)
        cnt_col = jnp.sum(multi, axis=1, keepdims=True)
        nb_col = jnp.floor((cnt_col + (RUN_ROWS - 1.0)) * (1.0 / RUN_ROWS))
        loff_col = jnp.dot(jnp.where(ec < er, 1.0, 0.0).astype(BF16),
                           jnp.broadcast_to(nb_col, (N_EXPERTS, rt)).astype(BF16), preferred_element_type=F32)
        base = RUN_ROWS * loff_col + earlier
        lpos = jnp.concatenate([jnp.sum(jnp.where(sel, base, 0.0), axis=0, keepdims=True) for sel in sels],
                               axis=0)
        lpos_ref[:, h * rt:(h + 1) * rt] = lpos.astype(jnp.int32)
        rows_hi, rows_lo = split2(jnp.concatenate([lpos, gate_rows], axis=0))
        eye = jnp.where(r == c, 1.0, 0.0).astype(BF16)
        cols_ref[h * rt:(h + 1) * rt, :] = dot_nt(eye, rows_hi) + dot_nt(eye, rows_lo)

        cnt_row = dot_nt(jnp.ones((SUBLANES, rt), BF16), multi_b)
        nb_row = jnp.floor((cnt_row + (RUN_ROWS - 1.0)) * (1.0 / RUN_ROWS))
        loff_row = jnp.dot(nb_row.astype(BF16), jnp.where(er < ec, 1.0, 0.0).astype(BF16),
                           preferred_element_type=F32)
        nb_p = jnp.concatenate([nb_row, lane_pad], axis=1)
        loff_p = jnp.concatenate([loff_row, lane_pad], axis=1)
        goff_p = carry_sc[...]
        tab = jnp.where(rid == 0, nb_p, jnp.where(rid == 1, loff_p, jnp.where(rid == 2, goff_p, 0.0)))
        tab_ref[h] = tab.astype(jnp.int32)
        carry_sc[...] = goff_p + nb_p
    carry_out_ref[...] = carry_sc[...]


def _merge(x, o_attn, y_ssm, carry_in, w, *, tile, route_tile):
    n = x.shape[0]
    nt = n // tile
    per_step = tile // route_tile
    full = lambda shape: pl.BlockSpec(shape, lambda i: (0,) * len(shape))
    return pl.pallas_call(
        _merge_kernel,
        grid=(nt,),
        in_specs=[pl.BlockSpec((tile, D_MODEL), lambda i: (i, 0)),
                  pl.BlockSpec((tile, D_ATTN), lambda i: (i, 0)),
                  pl.BlockSpec((tile, D_SSM), lambda i: (i, 0)),
                  full((SUBLANES, LANES)),
                  full((D_ATTN, D_MODEL)), full((D_SSM, 2 * D_MODEL)), full((D_MODEL, 2 * D_MODEL)),
                  full((1, 2 * D_MODEL)), full((D_MODEL, D_MODEL)),
                  full((1, D_MODEL)), full((1, D_MODEL)),
                  full((N_EXPERTS, D_MODEL)), full((N_EXPERTS, 1))],
        out_specs=[pl.BlockSpec((tile, D_MODEL), lambda i: (i, 0)),
                   pl.BlockSpec((TOP_K, tile), lambda i: (0, i)),
                   pl.BlockSpec((tile, 2 * TOP_K), lambda i: (i, 0)),
                   pl.BlockSpec((per_step, SUBLANES, LANES), lambda i: (i, 0, 0)),
                   full((SUBLANES, LANES))],
        out_shape=[jax.ShapeDtypeStruct((n, D_MODEL), F32),
                   jax.ShapeDtypeStruct((TOP_K, n), jnp.int32),
                   jax.ShapeDtypeStruct((n, 2 * TOP_K), F32),
                   jax.ShapeDtypeStruct((nt * per_step, SUBLANES, LANES), jnp.int32),
                   jax.ShapeDtypeStruct((SUBLANES, LANES), F32)],
        scratch_shapes=[pltpu.VMEM((SUBLANES, LANES), F32)],
        compiler_params=_cparams(("arbitrary",), VMEM_LIMIT_MERGE),
        name="merge",
    )(x, o_attn, y_ssm, carry_in, w["wao"], w["wso"], w["wg"], w["bg"], w["wo"], w["g1"], w["b1"],
      w["wrt"], w["brt"])


def _tab(tab_ref, tile, row, e):
    return tab_ref[(tile * TAB_ROWS + row) * N_EXPERTS + e]


BIG_PIECE = 4 * RUN_ROWS
MAX_UNITS_LOG2 = 8


def _for_each_run_piece(tab_ref, tile, fn):
    def per_expert(e, carry):
        loff = RUN_ROWS * _tab(tab_ref, tile, 1, e)
        goff = RUN_ROWS * _tab(tab_ref, tile, 2, e)
        units = _tab(tab_ref, tile, 0, e)
        n_big = lax.shift_right_logical(units, 2)

        def big(j, c2):
            fn(pl.multiple_of(loff + j * BIG_PIECE, RUN_ROWS), goff + j * BIG_PIECE, e, BIG_PIECE)
            return c2

        lax.fori_loop(0, n_big, big, 0)
        done = n_big * BIG_PIECE

        def small(j, c2):
            fn(pl.multiple_of(loff + done + j * RUN_ROWS, RUN_ROWS), goff + done + j * RUN_ROWS, e, RUN_ROWS)
            return c2

        lax.fori_loop(0, units & 3, small, 0)
        return carry

    lax.fori_loop(0, N_EXPERTS, per_expert, 0)


def _drain_units(units, wait_copy, buffer_rows):
    assert buffer_rows < (RUN_ROWS << MAX_UNITS_LOG2)
    for b in range(MAX_UNITS_LOG2):
        if (RUN_ROWS << b) > buffer_rows:
            break

        @pl.when((lax.shift_right_logical(units, b) & 1) == 1)
        def _():
            wait_copy(RUN_ROWS << b).wait()


def _dispatch_kernel(tab_ref, seg_ref, tot_ref, tail_ref, lpos_p_ref, xp_ref, lpos_s_ref, xs_in_ref, xs_ref,
                     loc_sc, zero_sc, sem, zsem):
    i = pl.program_id(0)
    last = pl.num_programs(0) - 1
    tile = i
    slot = i % 2
    loc = loc_sc.shape[1]

    @pl.when(i == 0)
    def _():
        zero_sc[...] = jnp.zeros_like(zero_sc)

        def tail_copy(e, j):
            row = pl.multiple_of(RUN_ROWS * (tail_ref[e] + j), RUN_ROWS)
            return pltpu.make_async_copy(zero_sc.at[pl.ds(0, RUN_ROWS)], xs_ref.at[pl.ds(row, RUN_ROWS)], zsem)

        def per_expert(e, carry):
            n = tail_ref[N_EXPERTS + e]
            lax.fori_loop(0, n, lambda j, c2: (tail_copy(e, j).start(), c2)[1], 0)
            lax.fori_loop(0, n, lambda j, c2: (tail_copy(e, j).wait(), c2)[1], 0)
            return carry

        lax.fori_loop(0, N_EXPERTS, per_expert, 0)

        def block_copy(b):
            row = pl.multiple_of(b * MOE_ROWS, MOE_ROWS)
            return pltpu.make_async_copy(zero_sc, xs_ref.at[pl.ds(row, MOE_ROWS)], zsem)

        first_unused, n_blocks = tail_ref[2 * N_EXPERTS], xs_ref.shape[0] // MOE_ROWS
        lax.fori_loop(first_unused, n_blocks, lambda b, c2: (block_copy(b).start(), c2)[1], 0)
        lax.fori_loop(first_unused, n_blocks, lambda b, c2: (block_copy(b).wait(), c2)[1], 0)

    def sort_tile(lpos_ref, x_ref):
        tm = x_ref.shape[0]
        rows = lax.broadcasted_iota(jnp.int32, (loc, tm), 0)
        lp = lpos_ref[...]
        onehot = jnp.zeros((loc, tm), F32)
        for k in range(TOP_K):
            onehot = jnp.where(rows == lp[k:k + 1], 1.0, onehot)
        loc_sc[slot] = jnp.dot(onehot.astype(BF16), x_ref[...].astype(BF16), preferred_element_type=F32)

    @pl.when(i < last)
    def _():
        sort_tile(lpos_p_ref, xp_ref)

    @pl.when(i == last)
    def _():
        sort_tile(lpos_s_ref, xs_in_ref)

    def piece_copy(sl, lrow, grow, e, n):
        dst = pl.multiple_of(seg_ref[e] + grow, RUN_ROWS)
        return pltpu.make_async_copy(loc_sc.at[sl, pl.ds(lrow, n)], xs_ref.at[pl.ds(dst, n)], sem.at[sl])

    _for_each_run_piece(tab_ref, tile, lambda lrow, grow, e, n: piece_copy(slot, lrow, grow, e, n).start())

    def drain(tl, sl):
        _drain_units(tot_ref[tl], lambda n: piece_copy(sl, 0, 0, 0, n), loc)

    @pl.when(i > 0)
    def _():
        drain(tile - 1, 1 - slot)

    @pl.when(i == last)
    def _():
        drain(tile, slot)


def _dispatch(tab, seg_start, tot, tails, lpos_p, x1_p, lpos_s, x1_s, *, tile, nrows):
    nt_p = x1_p.shape[0] // tile
    ns = x1_s.shape[0]
    loc = tile * TOP_K + N_EXPERTS * RUN_ROWS
    prompt_blk = lambda i, *_: jnp.minimum(i, nt_p - 1)
    return pl.pallas_call(
        _dispatch_kernel,
        grid_spec=pltpu.PrefetchScalarGridSpec(
            num_scalar_prefetch=4,
            grid=(nt_p + 1,),
            in_specs=[pl.BlockSpec((TOP_K, tile), lambda i, *_: (0, prompt_blk(i))),
                      pl.BlockSpec((tile, D_MODEL), lambda i, *_: (prompt_blk(i), 0)),
                      pl.BlockSpec((TOP_K, ns), lambda i, *_: (0, 0)),
                      pl.BlockSpec((ns, D_MODEL), lambda i, *_: (0, 0))],
            out_specs=pl.BlockSpec(memory_space=pl.ANY),
            scratch_shapes=[pltpu.VMEM((2, loc, D_MODEL), F32), pltpu.VMEM((MOE_ROWS, D_MODEL), F32),
                            pltpu.SemaphoreType.DMA((2,)), pltpu.SemaphoreType.DMA(())]),
        out_shape=jax.ShapeDtypeStruct((nrows, D_MODEL), F32),
        compiler_params=_cparams(("arbitrary",), VMEM_LIMIT),
        name="dispatch",
    )(tab, seg_start, tot, tails, lpos_p, x1_p, lpos_s, x1_s)


def _deinterleave_matrix():
    pm = np.zeros((MXU_DIM, MXU_DIM), np.float32)
    half = MXU_DIM // 2
    for c in range(half):
        pm[2 * c, c] = 1.0
        pm[2 * c + 1, half + c] = 1.0
    return pm


def _expert_kernel(be_ref, nu_ref, nv_ref, ord_ref, nxt_ref, xs_ref, w1_hbm, b1_ref, w2_hbm, b2_ref, pm_ref, y_ref,
                   w1f_sc, w2f_sc, w1p_sc, w2b_sc, sem):
    del nu_ref
    i = pl.program_id(0)
    e = be_ref[i]
    prev = be_ref[jnp.maximum(i - 1, 0)]
    nblk = 2 * D_FF // MXU_DIM

    def weight_copies(expert, slot):
        return (pltpu.make_async_copy(w1_hbm.at[expert], w1f_sc.at[slot], sem.at[0, slot]),
                pltpu.make_async_copy(w2_hbm.at[expert], w2f_sc.at[slot], sem.at[1, slot]))

    @pl.when(i == 0)
    def _():
        for cp in weight_copies(e, 0):
            cp.start()

    @pl.when((i == 0) | (e != prev))
    def _():
        slot = ord_ref[i] % 2
        for cp in weight_copies(e, slot):
            cp.wait()

        for cb in range(nblk):
            blk = w1f_sc[slot, :, cb * MXU_DIM:(cb + 1) * MXU_DIM].astype(BF16)
            w1p_sc[:, cb * MXU_DIM:(cb + 1) * MXU_DIM] = jnp.dot(
                blk, pm_ref[...], preferred_element_type=F32).astype(BF16)
        w2b_sc[...] = w2f_sc[slot].astype(BF16)

        nxt = nxt_ref[i]

        @pl.when(nxt >= 0)
        def _():
            for cp in weight_copies(nxt, 1 - slot):
                cp.start()

    for blk in range(MOE_STEP_BLOCKS):
        rows = slice(blk * MOE_ROWS, (blk + 1) * MOE_ROWS)

        @pl.when(blk < nv_ref[i])
        def _():
            x = xs_ref[rows, :].astype(BF16)
            h = jnp.dot(x, w1p_sc[...], preferred_element_type=F32) + b1_ref[0]
            half = MXU_DIM // 2
            acts = []
            for cb in range(nblk):
                x_glu = jnp.minimum(h[:, cb * MXU_DIM:cb * MXU_DIM + half], SWIGLU_LIMIT)
                x_lin = jnp.clip(h[:, cb * MXU_DIM + half:(cb + 1) * MXU_DIM], -SWIGLU_LIMIT, SWIGLU_LIMIT)
                acts.append((x_glu * jax.nn.sigmoid(SWIGLU_ALPHA * x_glu) * (x_lin + 1.0)).astype(BF16))
            act = jnp.concatenate(acts, axis=1)
            y_ref[rows, :] = jnp.dot(act, w2b_sc[...], preferred_element_type=F32) + b2_ref[0]

        @pl.when(blk >= nv_ref[i])
        def _():
            y_ref[rows, :] = jnp.zeros((MOE_ROWS, D_MODEL), F32)


def _experts(block_e, n_used, n_valid, run_ord, run_next, xs, w1, b1p, w2, b2, pm):
    nrows = xs.shape[0]
    step_rows = MOE_STEP_BLOCKS * MOE_ROWS
    nb = nrows // step_rows
    return pl.pallas_call(
        _expert_kernel,
        grid_spec=pltpu.PrefetchScalarGridSpec(
            num_scalar_prefetch=5,
            grid=(nb,),
            in_specs=[pl.BlockSpec((step_rows, D_MODEL), lambda i, be, nu, *_: (jnp.minimum(i, nu[0] - 1), 0)),
                      pl.BlockSpec(memory_space=pl.ANY),
                      pl.BlockSpec((1, 1, 2 * D_FF), lambda i, be, *_: (be[i], 0, 0)),
                      pl.BlockSpec(memory_space=pl.ANY),
                      pl.BlockSpec((1, 1, D_MODEL), lambda i, be, *_: (be[i], 0, 0)),
                      pl.BlockSpec((MXU_DIM, MXU_DIM), lambda i, *_: (0, 0))],
            out_specs=pl.BlockSpec((step_rows, D_MODEL), lambda i, *_: (i, 0)),
            scratch_shapes=[pltpu.VMEM((2, D_MODEL, 2 * D_FF), F32), pltpu.VMEM((2, D_FF, D_MODEL), F32),
                            pltpu.VMEM((D_MODEL, 2 * D_FF), BF16), pltpu.VMEM((D_FF, D_MODEL), BF16),
                            pltpu.SemaphoreType.DMA((2, 2))]),
        out_shape=jax.ShapeDtypeStruct((nrows, D_MODEL), F32),
        compiler_params=_cparams(("arbitrary",), VMEM_LIMIT_MERGE),
        name="experts",
    )(block_e, n_used, n_valid, run_ord, run_next, xs, w1, b1p, w2, b2, pm)


def _combine_kernel(tab_ref, seg_ref, tot_ref, cols_ref, x1_ref, g2_ref, b2_ref, ys_ref, y_ref, loc_sc, sem,
                    *, tile_base):
    i = pl.program_id(0)
    last = pl.num_programs(0) - 1
    tile = i + tile_base
    slot = i % 2
    loc, tm = loc_sc.shape[1], x1_ref.shape[0]

    def piece_copy(sl, lrow, grow, e, n):
        src = pl.multiple_of(seg_ref[e] + grow, RUN_ROWS)
        return pltpu.make_async_copy(ys_ref.at[pl.ds(src, n)], loc_sc.at[sl, pl.ds(lrow, n)], sem.at[sl])

    def gather(tl, sl):
        _for_each_run_piece(tab_ref, tl, lambda lrow, grow, e, n: piece_copy(sl, lrow, grow, e, n).start())

    @pl.when(i == 0)
    def _():
        loc_sc[...] = jnp.zeros_like(loc_sc)
        gather(tile, slot)

    @pl.when(i < last)
    def _():
        gather(tile + 1, 1 - slot)

    _drain_units(tot_ref[tile], lambda n: piece_copy(slot, 0, 0, 0, n), loc)

    cols = cols_ref[...]
    lane = lax.broadcasted_iota(jnp.int32, (tm, loc), 1)
    weights = jnp.zeros((tm, loc), F32)
    for k in range(TOP_K):
        weights = jnp.where(lane == cols[:, k:k + 1].astype(jnp.int32), cols[:, TOP_K + k:TOP_K + k + 1], weights)
    ffn = jnp.dot(weights.astype(BF16), loc_sc[slot].astype(BF16), preferred_element_type=F32)
    y_ref[...] = _layer_norm(DEEPNORM_ALPHA * x1_ref[...] + ffn, g2_ref[...], b2_ref[...])


def _combine(tab, seg_start, tot, cols, x1, g2, b2, ys, *, tile, tile_base):
    n = x1.shape[0]
    loc = tile * TOP_K + N_EXPERTS * RUN_ROWS
    return pl.pallas_call(
        functools.partial(_combine_kernel, tile_base=tile_base),
        grid_spec=pltpu.PrefetchScalarGridSpec(
            num_scalar_prefetch=3,
            grid=(n // tile,),
            in_specs=[pl.BlockSpec((tile, 2 * TOP_K), lambda i, *_: (i, 0)),
                      pl.BlockSpec((tile, D_MODEL), lambda i, *_: (i, 0)),
                      pl.BlockSpec((1, D_MODEL), lambda i, *_: (0, 0)),
                      pl.BlockSpec((1, D_MODEL), lambda i, *_: (0, 0)),
                      pl.BlockSpec(memory_space=pl.ANY)],
            out_specs=pl.BlockSpec((tile, D_MODEL), lambda i, *_: (i, 0)),
            scratch_shapes=[pltpu.VMEM((2, loc, D_MODEL), F32), pltpu.SemaphoreType.DMA((2,))]),
        out_shape=jax.ShapeDtypeStruct((n, D_MODEL), F32),
        compiler_params=_cparams(("arbitrary",), VMEM_LIMIT),
        name="combine",
    )(tab, seg_start, tot, cols, x1, g2, b2, ys)


def kernel(x_prompt, x_sample, cache_k_win, cache_v_win, state_ssm_re, state_ssm_im, w_in, b_in, attn_sinks,
           w_attn_out, ssm_a_re, ssm_a_im, ssm_log_dt, ssm_b_re, ssm_b_im, ssm_c_re, ssm_c_im, ssm_d, w_ssm_out,
           w_gate, b_gate, w_out, ln1_g, ln1_b, w_router, b_router, w_exp1, b_exp1, w_exp2, b_exp2, ln2_g, ln2_b):
    assert w_in.shape[0] == DEPTH == 1
    bsz, seq, _ = x_prompt.shape
    nsamp = x_sample.shape[0]
    assert x_sample.shape[1] == 1
    n_p = bsz * seq
    n_tok = n_p + nsamp

    xp = x_prompt.reshape(n_p, D_MODEL)
    xsm = x_sample.reshape(nsamp, D_MODEL)
    b_in2 = b_in[0].reshape(1, D_IN)
    sinks = attn_sinks[0].astype(F32)

    q_p, k_p, v_p, u_p = _proj(xp, w_in[0].astype(BF16), b_in2, tile=512, exact_f32=False, q_dtype=BF16)
    q_s, k_s, v_s, u_s = _proj(xsm, w_in[0], b_in2, tile=nsamp, exact_f32=True, q_dtype=F32)

    o_p = _attn_prompt(sinks, q_p.reshape(bsz, seq, D_ATTN), k_p.reshape(bsz, seq, D_KV),
                       v_p.reshape(bsz, seq, D_KV)).reshape(n_p, D_ATTN)
    k_buf = cache_k_win[0].reshape(nsamp, WINDOW, D_KV)
    v_buf = cache_v_win[0].reshape(nsamp, WINDOW, D_KV)
    o_s = _attn_sample(sinks, q_s, k_s, v_s, k_buf, v_buf)

    sp = _s5_params(ssm_a_re[0], ssm_a_im[0], ssm_log_dt[0], ssm_b_re[0], ssm_b_im[0], ssm_c_re[0], ssm_c_im[0])
    y_p, hp_re, hp_im = _s5_prompt(u_p, bsz, seq, _s5_chunk_mats(sp, ssm_d[0]))
    y_s, hs_re, hs_im = _s5_sample(u_s, state_ssm_re[0].reshape(nsamp, -1), state_ssm_im[0].reshape(nsamp, -1),
                                   _s5_sample_mats(sp, ssm_d[0]))

    wm = dict(wao=w_attn_out[0].astype(BF16), wso=w_ssm_out[0].astype(BF16), wg=w_gate[0].astype(BF16),
              bg=b_gate[0].reshape(1, -1), wo=w_out[0].astype(BF16), g1=ln1_g[0].reshape(1, -1),
              b1=ln1_b[0].reshape(1, -1), wrt=w_router[0].T, brt=b_router[0].reshape(-1, 1))
    carry0 = jnp.zeros((SUBLANES, LANES), F32)
    x1_p, lpos_p, cols_p, tab_p, carry1 = _merge(xp, o_p, y_p, carry0, wm, tile=MERGE_TILE, route_tile=TOK_TILE)
    x1_s, lpos_s, cols_s, tab_s, carry2 = _merge(xsm, o_s, y_s, carry1, wm, tile=nsamp, route_tile=nsamp)

    nt_p = n_p // TOK_TILE
    tab = jnp.concatenate([tab_p[:, :TAB_ROWS, :N_EXPERTS], tab_s[:, :TAB_ROWS, :N_EXPERTS]], axis=0)
    tot = jnp.sum(tab[:, 0, :], axis=1).astype(jnp.int32)
    tab = tab.reshape(-1)
    seg_rows = carry2[0, :N_EXPERTS].astype(jnp.int32) * RUN_ROWS
    step_rows = MOE_STEP_BLOCKS * MOE_ROWS
    padded = ((seg_rows + step_rows - 1) // step_rows) * step_rows
    pad_end = jnp.cumsum(padded)
    pad_start = (pad_end - padded).astype(jnp.int32)
    seg_end = pad_start + seg_rows
    n_runs = (nt_p + 1) * N_EXPERTS
    nb_max = (n_tok * TOP_K + n_runs * (RUN_ROWS - 1) + N_EXPERTS * (step_rows - 1) + step_rows - 1) // step_rows
    n_used = (pad_end[-1] // step_rows).astype(jnp.int32)
    tails = jnp.concatenate([seg_end // RUN_ROWS, (padded - seg_rows) // RUN_ROWS,
                             (pad_end[-1:] // MOE_ROWS)]).astype(jnp.int32)
    blk_start = jnp.arange(nb_max, dtype=jnp.int32) * step_rows
    blk_e = jnp.minimum(jnp.sum(blk_start[:, None] >= pad_end[None, :], axis=1), N_EXPERTS - 1).astype(jnp.int32)
    used = jnp.arange(nb_max) < n_used
    blk_e = jnp.where(used, blk_e, jnp.max(jnp.where(used, blk_e, 0)))
    ids = jnp.arange(N_EXPERTS, dtype=jnp.int32)
    of_blk = blk_e[:, None] == ids[None, :]
    n_valid = jnp.clip((jnp.sum(jnp.where(of_blk, seg_end[None, :], 0), axis=1) - blk_start + MOE_ROWS - 1)
                       // MOE_ROWS, 0, MOE_STEP_BLOCKS)
    n_valid = jnp.where(used, n_valid, 0).astype(jnp.int32)
    new_run = jnp.concatenate([jnp.ones((1,), jnp.int32), (blk_e[1:] != blk_e[:-1]).astype(jnp.int32)])
    run_ord = (jnp.cumsum(new_run) - 1).astype(jnp.int32)
    later = (ids[None, :] > ids[:, None]) & (padded > 0)[None, :]
    next_e = jnp.min(jnp.where(later, ids[None, :], N_EXPERTS), axis=1)
    next_e = jnp.where(next_e < N_EXPERTS, next_e, -1).astype(jnp.int32)
    run_next = jnp.sum(jnp.where(of_blk, next_e[None, :], 0), axis=1).astype(jnp.int32)

    nrows = nb_max * step_rows
    xs = _dispatch(tab, pad_start, tot, tails, lpos_p, x1_p, lpos_s, x1_s, tile=TOK_TILE, nrows=nrows)

    b1p = b_exp1[0].reshape(N_EXPERTS, 2 * D_FF // MXU_DIM, MXU_DIM // 2, 2)
    b1p = jnp.swapaxes(b1p, 2, 3).reshape(N_EXPERTS, 1, 2 * D_FF)
    ys = _experts(blk_e, n_used.reshape(1), n_valid, run_ord, run_next, xs, w_exp1[0], b1p, w_exp2[0],
                  b_exp2[0].reshape(N_EXPERTS, 1, D_MODEL),
                  jnp.asarray(_deinterleave_matrix(), BF16))

    g2, b2 = ln2_g[0].reshape(1, -1), ln2_b[0].reshape(1, -1)
    y_prompt = _combine(tab, pad_start, tot, cols_p, x1_p, g2, b2, ys, tile=TOK_TILE, tile_base=0)
    y_sample = _combine(tab, pad_start, tot, cols_s, x1_s, g2, b2, ys, tile=nsamp, tile_base=nt_p)

    k_p4 = k_p.reshape(bsz, seq, D_KV)[:, -WINDOW:].reshape(bsz, WINDOW, N_KV_HEADS, HEAD_DIM)
    v_p4 = v_p.reshape(bsz, seq, D_KV)[:, -WINDOW:].reshape(bsz, WINDOW, N_KV_HEADS, HEAD_DIM)
    k_s4 = jnp.concatenate([cache_k_win[0][:, 1:], k_s.reshape(nsamp, 1, N_KV_HEADS, HEAD_DIM)], axis=1)
    v_s4 = jnp.concatenate([cache_v_win[0][:, 1:], v_s.reshape(nsamp, 1, N_KV_HEADS, HEAD_DIM)], axis=1)
    st = lambda a, n: a.reshape(1, n, N_SSM_GROUPS, SSM_STATE)
    return (y_prompt.reshape(bsz, seq, D_MODEL), y_sample.reshape(nsamp, 1, D_MODEL),
            k_p4[None], v_p4[None], st(hp_re, bsz), st(hp_im, bsz),
            k_s4[None], v_s4[None], st(hs_re, nsamp), st(hs_im, nsamp))
```

```python
import functools
import math

import numpy as np
import jax
import jax.numpy as jnp
from jax import lax
from jax.experimental import pallas as pl
from jax.experimental.pallas import tpu as pltpu

F32 = jnp.float32
BF16 = jnp.bfloat16

D_MODEL = 1024
HEAD_DIM = 64
N_Q_HEADS = 8
N_KV_HEADS = 2
Q_PER_KV = N_Q_HEADS // N_KV_HEADS
D_ATTN = N_Q_HEADS * HEAD_DIM
D_KV = N_KV_HEADS * HEAD_DIM
WINDOW = 128
ATTN_SCALE = HEAD_DIM ** -0.5
SSM_GROUP = 16
D_SSM = D_MODEL // 2
N_SSM_GROUPS = D_SSM // SSM_GROUP
SSM_STATE = 64
D_IN = D_ATTN + 2 * D_KV + D_SSM
N_EXPERTS = 32
TOP_K = 4
D_FF = D_MODEL
SWIGLU_LIMIT = 7.0
SWIGLU_ALPHA = 1.702
LN_EPS = 1e-5
DEPTH = 1
DEEPNORM_ALPHA = (2 * DEPTH) ** 0.25

LANES = 128
SUBLANES = 8
MXU_DIM = 256

S5_CHUNK = MXU_DIM // SSM_GROUP
S5_LANE_GROUPS = LANES // SSM_GROUP
MOE_ROWS = 256
MOE_STEP_BLOCKS = 2
TOK_TILE = 256
MERGE_TILE = 512
VMEM_LIMIT = 48 * 1024 * 1024
VMEM_LIMIT_MERGE = 56 * 1024 * 1024


def _cparams(sem, vmem=None):
    return pltpu.CompilerParams(dimension_semantics=sem, vmem_limit_bytes=vmem)


def _proj_kernel(x_ref, w_ref, b_ref, q_ref, k_ref, v_ref, u_ref, *, exact_f32):
    if exact_f32:
        h = jnp.dot(x_ref[...], w_ref[...], preferred_element_type=F32, precision=lax.Precision.HIGHEST)
    else:
        h = jnp.dot(x_ref[...].astype(BF16), w_ref[...], preferred_element_type=F32)
    h = h + b_ref[...]
    q_ref[...] = (h[:, :D_ATTN] * ATTN_SCALE).astype(q_ref.dtype)
    k_ref[...] = h[:, D_ATTN:D_ATTN + D_KV]
    v_ref[...] = h[:, D_ATTN + D_KV:D_ATTN + 2 * D_KV]
    u_ref[...] = h[:, D_ATTN + 2 * D_KV:].astype(u_ref.dtype)


def _proj(x, w, b, *, tile, exact_f32, q_dtype):
    n = x.shape[0]
    return pl.pallas_call(
        functools.partial(_proj_kernel, exact_f32=exact_f32),
        grid=(n // tile,),
        in_specs=[pl.BlockSpec((tile, D_MODEL), lambda i: (i, 0)),
                  pl.BlockSpec((D_MODEL, D_IN), lambda i: (0, 0)),
                  pl.BlockSpec((1, D_IN), lambda i: (0, 0))],
        out_specs=[pl.BlockSpec((tile, D_ATTN), lambda i: (i, 0)),
                   pl.BlockSpec((tile, D_KV), lambda i: (i, 0)),
                   pl.BlockSpec((tile, D_KV), lambda i: (i, 0)),
                   pl.BlockSpec((tile, D_SSM), lambda i: (i, 0))],
        out_shape=[jax.ShapeDtypeStruct((n, D_ATTN), q_dtype),
                   jax.ShapeDtypeStruct((n, D_KV), F32),
                   jax.ShapeDtypeStruct((n, D_KV), F32),
                   jax.ShapeDtypeStruct((n, D_SSM), F32)],
        compiler_params=_cparams(("parallel",)),
        name="proj",
    )(x, w, b)


ATT_Q_TILE = 512


def _attn_prompt_kernel(sink_ref, q_ref, k_ref, v_ref, o_ref):
    i = pl.program_id(1)
    nk, nq = 2 * WINDOW, 2 * WINDOW
    lo = lax.broadcasted_iota(jnp.int32, (nk, LANES), 1) < HEAD_DIM
    top = lax.broadcasted_iota(jnp.int32, (nq, 1), 0) < WINDOW
    for blk in range(ATT_Q_TILE // WINDOW):
        q0 = i * ATT_Q_TILE + blk * WINDOW
        k0 = pl.multiple_of(jnp.maximum(q0 - WINDOW, 0), WINDOW)
        kk = k_ref[0, pl.ds(k0, nk), :]
        vv = v_ref[0, pl.ds(k0, nk), :]
        kk_sw = pltpu.roll(kk, HEAD_DIM, axis=1)
        vv_sw = pltpu.roll(vv, HEAD_DIM, axis=1)
        k_var = [[jnp.where(lo, kk, 0.0).astype(BF16), jnp.where(lo, 0.0, kk_sw).astype(BF16)],
                 [jnp.where(lo, kk_sw, 0.0).astype(BF16), jnp.where(lo, 0.0, kk).astype(BF16)]]
        v_var = [[jnp.where(lo, vv, 1.0).astype(BF16), jnp.where(lo, 1.0, vv_sw).astype(BF16)],
                 [jnp.where(lo, vv_sw, 1.0).astype(BF16), jnp.where(lo, 1.0, vv).astype(BF16)]]
        qpos = q0 + lax.broadcasted_iota(jnp.int32, (nq, nk), 0) % WINDOW
        kpos = k0 + lax.broadcasted_iota(jnp.int32, (nq, nk), 1)
        valid = (kpos <= qpos) & (qpos - kpos <= WINDOW)
        rows = slice(blk * WINDOW, (blk + 1) * WINDOW)
        for kv in range(N_KV_HEADS):
            pairs = (2 * kv, 2 * kv + 1)
            qs = jnp.concatenate([q_ref[0, rows, pr * LANES:(pr + 1) * LANES] for pr in pairs], axis=0)
            outs = []
            for parity in range(2):
                sink = jnp.where(top, sink_ref[2 * pairs[0] + parity], sink_ref[2 * pairs[1] + parity])
                s = lax.dot_general(qs, k_var[kv][parity], (((1,), (1,)), ((), ())), preferred_element_type=F32)
                s = jnp.where(valid, s, -jnp.inf)
                m = jnp.maximum(jnp.max(s, axis=-1, keepdims=True), sink)
                p = jnp.exp(s - m).astype(BF16)
                acc = jnp.dot(p, v_var[kv][parity], preferred_element_type=F32)
                outs.append(acc / (pltpu.roll(acc, HEAD_DIM, axis=1) + jnp.exp(sink - m)))
            o = jnp.where(lo, outs[0], outs[1]).astype(o_ref.dtype)
            for j, pr in enumerate(pairs):
                o_ref[0, rows, pr * LANES:(pr + 1) * LANES] = o[j * WINDOW:(j + 1) * WINDOW]


def _attn_prompt(sinks, q, k, v):
    bsz, seq = q.shape[0], q.shape[1]
    return pl.pallas_call(
        _attn_prompt_kernel,
        grid=(bsz, seq // ATT_Q_TILE),
        in_specs=[pl.BlockSpec(memory_space=pltpu.SMEM),
                  pl.BlockSpec((1, ATT_Q_TILE, D_ATTN), lambda b, i: (b, i, 0)),
                  pl.BlockSpec((1, seq, D_KV), lambda b, i: (b, 0, 0)),
                  pl.BlockSpec((1, seq, D_KV), lambda b, i: (b, 0, 0))],
        out_specs=pl.BlockSpec((1, ATT_Q_TILE, D_ATTN), lambda b, i: (b, i, 0)),
        out_shape=jax.ShapeDtypeStruct((bsz, seq, D_ATTN), BF16),
        compiler_params=_cparams(("parallel", "parallel")),
        name="attn_prompt",
    )(sinks, q, k, v)


ATT_S_GROUP = 16


def _attn_sample_kernel(sink_ref, q_ref, kn_ref, vn_ref, kb_ref, vb_ref, o_ref):
    g = ATT_S_GROUP
    rows = Q_PER_KV * g
    ncol = g * WINDOW
    hp = lax.Precision.HIGHEST
    kb = kb_ref[...].reshape(ncol, D_KV)
    vb = vb_ref[...].reshape(ncol, D_KV)
    rseq = lax.broadcasted_iota(jnp.int32, (rows, ncol), 0) % g
    cseq = lax.broadcasted_iota(jnp.int32, (rows, ncol), 1) // WINDOW
    own = rseq == cseq
    rhead = lax.broadcasted_iota(jnp.int32, (rows, 1), 0) // g
    for kv in range(N_KV_HEADS):
        lo = kv * HEAD_DIM
        qs = jnp.concatenate(
            [q_ref[:, (kv * Q_PER_KV + h) * HEAD_DIM:(kv * Q_PER_KV + h + 1) * HEAD_DIM] for h in range(Q_PER_KV)],
            axis=0)
        kn = jnp.concatenate([kn_ref[:, lo:lo + HEAD_DIM]] * Q_PER_KV, axis=0)
        vn = jnp.concatenate([vn_ref[:, lo:lo + HEAD_DIM]] * Q_PER_KV, axis=0)
        sink = jnp.zeros((rows, 1), F32)
        for h in range(Q_PER_KV):
            sink = jnp.where(rhead == h, sink_ref[kv * Q_PER_KV + h], sink)
        qs = qs.astype(F32)
        s = lax.dot_general(qs, kb[:, lo:lo + HEAD_DIM], (((1,), (1,)), ((), ())), preferred_element_type=F32,
                            precision=hp)
        s = jnp.where(own, s, -jnp.inf)
        s_new = jnp.sum(qs * kn, axis=-1, keepdims=True)
        m = jnp.maximum(jnp.maximum(jnp.max(s, axis=-1, keepdims=True), s_new), sink)
        p = jnp.exp(s - m)
        p_new = jnp.exp(s_new - m)
        denom = jnp.sum(p, axis=-1, keepdims=True) + p_new + jnp.exp(sink - m)
        o = jnp.dot(p, vb[:, lo:lo + HEAD_DIM], preferred_element_type=F32, precision=hp)
        o = (o + p_new * vn) / denom
        for h in range(Q_PER_KV):
            c0 = (kv * Q_PER_KV + h) * HEAD_DIM
            o_ref[:, c0:c0 + HEAD_DIM] = o[h * g:(h + 1) * g].astype(o_ref.dtype)


def _attn_sample(sinks, q, k_new, v_new, k_buf, v_buf):
    n = q.shape[0]
    g = ATT_S_GROUP
    return pl.pallas_call(
        _attn_sample_kernel,
        grid=(n // g,),
        in_specs=[pl.BlockSpec(memory_space=pltpu.SMEM),
                  pl.BlockSpec((g, D_ATTN), lambda i: (i, 0)),
                  pl.BlockSpec((g, D_KV), lambda i: (i, 0)),
                  pl.BlockSpec((g, D_KV), lambda i: (i, 0)),
                  pl.BlockSpec((g, WINDOW, D_KV), lambda i: (i, 0, 0)),
                  pl.BlockSpec((g, WINDOW, D_KV), lambda i: (i, 0, 0))],
        out_specs=pl.BlockSpec((g, D_ATTN), lambda i: (i, 0)),
        out_shape=jax.ShapeDtypeStruct((n, D_ATTN), F32),
        compiler_params=_cparams(("parallel",)),
        name="attn_sample",
    )(sinks, q, k_new, v_new, k_buf, v_buf)


def _s5_params(a_re, a_im, log_dt, b_re, b_im, c_re, c_im):
    hp = lax.Precision.HIGHEST
    dt = jnp.exp(log_dt.astype(F32))[:, None]
    are, aim = a_re.astype(F32), a_im.astype(F32)
    tau = jnp.arange(S5_CHUNK + 1, dtype=F32)[None, :, None]
    mag = jnp.exp(tau * (dt * are)[:, None, :])
    ang = tau * (dt * aim)[:, None, :]
    pw_re, pw_im = mag * jnp.cos(ang), mag * jnp.sin(ang)
    ab_re, ab_im = pw_re[:, 1], pw_im[:, 1]
    den = are * are + aim * aim
    f_re = ((ab_re - 1.0) * are + ab_im * aim) / den
    f_im = (ab_im * are - (ab_re - 1.0) * aim) / den
    bre, bim = b_re.astype(F32), b_im.astype(F32)
    bb_re = f_re[..., None] * bre - f_im[..., None] * bim
    bb_im = f_re[..., None] * bim + f_im[..., None] * bre
    cre, cim = c_re.astype(F32), c_im.astype(F32)
    return dict(pw_re=pw_re, pw_im=pw_im, ab_re=ab_re, ab_im=ab_im, bb_re=bb_re, bb_im=bb_im,
                c_re=cre, c_im=cim, hp=hp)


def _s5_chunk_mats(sp, d_skip):
    hp = sp["hp"]
    g, t, c, p = N_SSM_GROUPS, S5_CHUNK, SSM_GROUP, SSM_STATE
    pw_re, pw_im = sp["pw_re"], sp["pw_im"]
    ca_re = sp["c_re"][:, None] * pw_re[:, :, None, :] - sp["c_im"][:, None] * pw_im[:, :, None, :]
    ca_im = sp["c_re"][:, None] * pw_im[:, :, None, :] + sp["c_im"][:, None] * pw_re[:, :, None, :]
    kern = (jnp.einsum("gtcp,gpd->gtcd", ca_re[:, :t], sp["bb_re"], precision=hp)
            - jnp.einsum("gtcp,gpd->gtcd", ca_im[:, :t], sp["bb_im"], precision=hp))
    kc = jnp.swapaxes(kern, 2, 3)
    kc = kc.at[:, 0].add(d_skip.astype(F32).reshape(g, 1, c) * jnp.eye(c, dtype=F32)[None])
    rev_re, rev_im = pw_re[:, t - 1::-1][:, :t], pw_im[:, t - 1::-1][:, :t]
    wst_re = rev_re[:, :, None, :] * jnp.swapaxes(sp["bb_re"], 1, 2)[:, None] \
        - rev_im[:, :, None, :] * jnp.swapaxes(sp["bb_im"], 1, 2)[:, None]
    wst_im = rev_re[:, :, None, :] * jnp.swapaxes(sp["bb_im"], 1, 2)[:, None] \
        + rev_im[:, :, None, :] * jnp.swapaxes(sp["bb_re"], 1, 2)[:, None]
    wo_re = jnp.transpose(ca_re[:, 1:t + 1], (0, 3, 1, 2))
    wo_im = -jnp.transpose(ca_im[:, 1:t + 1], (0, 3, 1, 2))
    nv, gl = g // S5_LANE_GROUPS, S5_LANE_GROUPS
    kc, wst_re, wst_im, wo_re, wo_im = lax.optimization_barrier((kc, wst_re, wst_im, wo_re, wo_im))
    kc5 =jnp.transpose(kc.reshape(nv, gl, t, c, c), (0, 2, 1, 3, 4))
    ws6 = jnp.transpose(jnp.stack([wst_re, wst_im], axis=3).reshape(nv, gl, t, c, 2, p),
                        (0, 2, 1, 3, 4, 5))
    wo6 = jnp.transpose(jnp.stack([wo_re, wo_im], axis=0).reshape(2, nv, gl, p, t, c),
                        (1, 0, 2, 3, 4, 5))
    kc5, ws6, wo6 = lax.optimization_barrier((kc5.astype(BF16), ws6.astype(BF16), wo6.astype(BF16)))
    spread_b = np.zeros((c, LANES), np.float32)
    spread_s = np.zeros((2 * p, 2 * gl * p), np.float32)
    spread_o = np.zeros((t * c, t * LANES), np.float32)
    for h in range(gl):
        spread_b[np.arange(c), h * c + np.arange(c)] = 1.0
        for ri in range(2):
            spread_s[ri * p + np.arange(p), ri * gl * p + h * p + np.arange(p)] = 1.0
        for tt in range(t):
            spread_o[tt * c + np.arange(c), tt * LANES + h * c + np.arange(c)] = 1.0
    at_re = pw_re[:, t].reshape(1, g * p)
    at_im = pw_im[:, t].reshape(1, g * p)
    return dict(kc=kc5.reshape(nv, t, LANES, c), ws=ws6.reshape(nv, t * LANES, 2 * p),
                wo=wo6.reshape(nv, 2 * gl * p, t * c), spread_b=jnp.asarray(spread_b, BF16),
                spread_s=jnp.asarray(spread_s, BF16), spread_o=jnp.asarray(spread_o, BF16),
                at_re=at_re, at_im=at_im)


def _s5_chunk_rows(u_ref, nchunk):
    return jnp.concatenate(
        [u_ref[pl.ds(s, nchunk, stride=S5_CHUNK), :] for s in range(S5_CHUNK)], axis=1).astype(BF16)


S5_SLABS = S5_LANE_GROUPS * SSM_STATE // LANES


S5_EXPAND_ROWS = 256
S5_C_SHIFT = SSM_GROUP.bit_length() - 1
S5_P_SHIFT = SSM_STATE.bit_length() - 1


def _s5_expand(dst_ref, compact_ref, spread_ref, row_shift, col_shift):
    n_rows, n_cols = dst_ref.shape
    col_g = lax.shift_right_logical(lax.broadcasted_iota(jnp.int32, (S5_EXPAND_ROWS, n_cols), 1), col_shift)
    for r0 in range(0, n_rows, S5_EXPAND_ROWS):
        row_g = lax.shift_right_logical(r0 + lax.broadcasted_iota(jnp.int32, (S5_EXPAND_ROWS, n_cols), 0), row_shift)
        same = ((row_g ^ col_g) & (S5_LANE_GROUPS - 1)) == 0
        blk = jnp.dot(compact_ref[r0:r0 + S5_EXPAND_ROWS, :], spread_ref[...], preferred_element_type=F32)
        dst_ref[r0:r0 + S5_EXPAND_ROWS, :] = jnp.where(same, blk, 0.0).astype(dst_ref.dtype)


def _s5_state_kernel(u_ref, ws_ref, spread_ref, sre_ref, sim_ref, wst_sc):
    nchunk = sre_ref.shape[1]

    @pl.when(pl.program_id(1) == 0)
    def _():
        _s5_expand(wst_sc, ws_ref.at[0], spread_ref, S5_C_SHIFT, S5_P_SHIFT)

    s = jnp.dot(_s5_chunk_rows(u_ref, nchunk), wst_sc[...], preferred_element_type=F32)
    for k in range(S5_SLABS):
        sre_ref[k] = s[:, k * LANES:(k + 1) * LANES]
        sim_ref[k] = s[:, (S5_SLABS + k) * LANES:(S5_SLABS + k + 1) * LANES]


def _s5_scan_kernel(sre_ref, sim_ref, are_ref, aim_ref, hre_ref, him_ref, fre_ref, fim_ref, *, bsz):
    nchunk = sre_ref.shape[1] // bsz
    are = [jnp.broadcast_to(are_ref[:, k * LANES:(k + 1) * LANES], (bsz, LANES)) for k in range(S5_SLABS)]
    aim = [jnp.broadcast_to(aim_ref[:, k * LANES:(k + 1) * LANES], (bsz, LANES)) for k in range(S5_SLABS)]

    def body(j, carry):
        rows = pl.ds(j, bsz, stride=nchunk)
        out = []
        for k in range(S5_SLABS):
            cre, cim = carry[2 * k], carry[2 * k + 1]
            hre_ref[k, rows, :] = cre
            him_ref[k, rows, :] = cim
            sr = sre_ref[k, rows, :]
            si = sim_ref[k, rows, :]
            out += [are[k] * cre - aim[k] * cim + sr, are[k] * cim + aim[k] * cre + si]
        return tuple(out)

    zero = jnp.zeros((bsz, LANES), F32)
    fin = lax.fori_loop(0, nchunk, body, (zero,) * (2 * S5_SLABS), unroll=4)
    fre_ref[...] = jnp.concatenate(fin[0::2], axis=1)
    fim_ref[...] = jnp.concatenate(fin[1::2], axis=1)


def _s5_out_kernel(u_ref, kc_ref, spread_b_ref, hre_ref, him_ref, wo_ref, spread_o_ref, y_ref, m_sc, wout_sc):
    nchunk = hre_ref.shape[1]

    @pl.when(pl.program_id(1) == 0)
    def _():
        rg = lax.shift_right_logical(lax.broadcasted_iota(jnp.int32, (LANES, LANES), 0), S5_C_SHIFT)
        cg = lax.shift_right_logical(lax.broadcasted_iota(jnp.int32, (LANES, LANES), 1), S5_C_SHIFT)
        zero_blk = jnp.zeros((LANES, LANES), BF16)
        lag_blk = [jnp.where(rg == cg, jnp.dot(kc_ref[0, tau], spread_b_ref[...], preferred_element_type=F32),
                             0.0).astype(BF16) for tau in range(S5_CHUNK)]
        for s in range(S5_CHUNK):
            for t in range(S5_CHUNK):
                m_sc[s * LANES:(s + 1) * LANES, t * LANES:(t + 1) * LANES] = lag_blk[t - s] if t >= s else zero_blk
        _s5_expand(wout_sc, wo_ref.at[0], spread_o_ref, S5_P_SHIFT, S5_C_SHIFT)

    hcat = jnp.concatenate([hre_ref[k] for k in range(S5_SLABS)] + [him_ref[k] for k in range(S5_SLABS)],
                           axis=1).astype(BF16)
    lhs = _s5_chunk_rows(u_ref, nchunk)
    y = jnp.concatenate(
        [jnp.dot(lhs[:, :j + MXU_DIM], m_sc[:j + MXU_DIM, j:j + MXU_DIM], preferred_element_type=F32)
         for j in range(0, S5_CHUNK * LANES, MXU_DIM)], axis=1)
    y = y + jnp.dot(hcat, wout_sc[...], preferred_element_type=F32)
    for s in range(S5_CHUNK):
        y_ref[pl.ds(s, nchunk, stride=S5_CHUNK), :] = y[:, s * LANES:(s + 1) * LANES]


def _s5_prompt(u, bsz, seq, mats):
    at_re, at_im = mats["at_re"], mats["at_im"]
    g, t, p, c = N_SSM_GROUPS, S5_CHUNK, SSM_STATE, SSM_GROUP
    nchunk = seq // t
    n = nchunk * bsz
    nv = g // S5_LANE_GROUPS
    half = S5_LANE_GROUPS * p
    s_re, s_im = pl.pallas_call(
        _s5_state_kernel,
        grid=(nv, bsz),
        in_specs=[pl.BlockSpec((seq, LANES), lambda v, b: (b, v)),
                  pl.BlockSpec((1, t * LANES, 2 * p), lambda v, b: (v, 0, 0)),
                  pl.BlockSpec((2 * p, 2 * half), lambda v, b: (0, 0))],
        out_specs=[pl.BlockSpec((S5_SLABS, nchunk, LANES), lambda v, b: (v, b, 0)),
                   pl.BlockSpec((S5_SLABS, nchunk, LANES), lambda v, b: (v, b, 0))],
        out_shape=[jax.ShapeDtypeStruct((nv * S5_SLABS, n, LANES), F32)] * 2,
        scratch_shapes=[pltpu.VMEM((t * LANES, 2 * half), BF16)],
        compiler_params=_cparams(("parallel", "arbitrary"), VMEM_LIMIT),
        name="s5_state",
    )(u, mats["ws"], mats["spread_s"])
    h_re, h_im, f_re, f_im = pl.pallas_call(
        functools.partial(_s5_scan_kernel, bsz=bsz),
        grid=(nv,),
        in_specs=[pl.BlockSpec((S5_SLABS, n, LANES), lambda i: (i, 0, 0)),
                  pl.BlockSpec((S5_SLABS, n, LANES), lambda i: (i, 0, 0)),
                  pl.BlockSpec((1, half), lambda i: (0, i)),
                  pl.BlockSpec((1, half), lambda i: (0, i))],
        out_specs=[pl.BlockSpec((S5_SLABS, n, LANES), lambda i: (i, 0, 0)),
                   pl.BlockSpec((S5_SLABS, n, LANES), lambda i: (i, 0, 0)),
                   pl.BlockSpec((bsz, half), lambda i: (0, i)),
                   pl.BlockSpec((bsz, half), lambda i: (0, i))],
        out_shape=[jax.ShapeDtypeStruct((nv * S5_SLABS, n, LANES), F32)] * 2
        + [jax.ShapeDtypeStruct((bsz, g * p), F32)] * 2,
        compiler_params=_cparams(("parallel",)),
        name="s5_scan",
    )(s_re, s_im, at_re, at_im)
    y = pl.pallas_call(
        _s5_out_kernel,
        grid=(nv, bsz),
        in_specs=[pl.BlockSpec((seq, LANES), lambda v, b: (b, v)),
                  pl.BlockSpec((1, t, LANES, c), lambda v, b: (v, 0, 0, 0)),
                  pl.BlockSpec((c, LANES), lambda v, b: (0, 0)),
                  pl.BlockSpec((S5_SLABS, nchunk, LANES), lambda v, b: (v, b, 0)),
                  pl.BlockSpec((S5_SLABS, nchunk, LANES), lambda v, b: (v, b, 0)),
                  pl.BlockSpec((1, 2 * half, t * c), lambda v, b: (v, 0, 0)),
                  pl.BlockSpec((t * c, t * LANES), lambda v, b: (0, 0))],
        out_specs=pl.BlockSpec((seq, LANES), lambda v, b: (b, v)),
        out_shape=jax.ShapeDtypeStruct((bsz * seq, D_SSM), F32),
        scratch_shapes=[pltpu.VMEM((t * LANES, t * LANES), BF16), pltpu.VMEM((2 * half, t * LANES), BF16)],
        compiler_params=_cparams(("parallel", "arbitrary"), VMEM_LIMIT),
        name="s5_out",
    )(u, mats["kc"], mats["spread_b"], h_re, h_im, mats["wo"], mats["spread_o"])
    return y, f_re, f_im


S5S_GROUPS = LANES // SSM_GROUP


def _s5_sample_mats(sp, d_skip):
    go, gl, c, p = N_SSM_GROUPS // S5S_GROUPS, S5S_GROUPS, SSM_GROUP, SSM_STATE
    eye = jnp.eye(gl, dtype=F32)

    def bdiag_in(b):
        b4 = b.reshape(go, gl, p, c)
        return jnp.einsum("ogpc,gh->ogchp", b4, eye).reshape(go, gl * c, gl * p)

    def bdiag_out(cm):
        c4 = cm.reshape(go, gl, c, p)
        return jnp.einsum("ogcp,gh->ogphc", c4, eye).reshape(go, gl * p, gl * c)

    b8 = jnp.concatenate([bdiag_in(sp["bb_re"]), bdiag_in(sp["bb_im"])], axis=2)
    c8 = jnp.concatenate([bdiag_out(sp["c_re"]), -bdiag_out(sp["c_im"])], axis=1)
    a_re = sp["ab_re"].reshape(1, N_SSM_GROUPS * p)
    a_im = sp["ab_im"].reshape(1, N_SSM_GROUPS * p)
    return b8, c8, a_re, a_im, d_skip.astype(F32).reshape(1, D_SSM)


def _s5_sample_kernel(u_ref, hre_ref, him_ref, b8_ref, c8_ref, are_ref, aim_ref, d_ref,
                      y_ref, ore_ref, oim_ref):
    hp = lax.Precision.HIGHEST
    u = u_ref[...]
    half = S5S_GROUPS * SSM_STATE
    bu = jnp.dot(u, b8_ref[0], preferred_element_type=F32, precision=hp)
    are, aim = are_ref[...], aim_ref[...]
    h0r, h0i = hre_ref[...], him_ref[...]
    hr = are * h0r - aim * h0i + bu[:, :half]
    hi = are * h0i + aim * h0r + bu[:, half:]
    ore_ref[...] = hr
    oim_ref[...] = hi
    y = jnp.dot(jnp.concatenate([hr, hi], axis=1), c8_ref[0], preferred_element_type=F32, precision=hp)
    y_ref[...] = (y + d_ref[...] * u).astype(y_ref.dtype)


def _s5_sample(u, h0_re, h0_im, mats):
    b8, c8, a_re, a_im, d = mats
    n = u.shape[0]
    half = S5S_GROUPS * SSM_STATE
    return pl.pallas_call(
        _s5_sample_kernel,
        grid=(N_SSM_GROUPS // S5S_GROUPS,),
        in_specs=[pl.BlockSpec((n, LANES), lambda i: (0, i)),
                  pl.BlockSpec((n, half), lambda i: (0, i)),
                  pl.BlockSpec((n, half), lambda i: (0, i)),
                  pl.BlockSpec((1, LANES, 2 * half), lambda i: (i, 0, 0)),
                  pl.BlockSpec((1, 2 * half, LANES), lambda i: (i, 0, 0)),
                  pl.BlockSpec((1, half), lambda i: (0, i)),
                  pl.BlockSpec((1, half), lambda i: (0, i)),
                  pl.BlockSpec((1, LANES), lambda i: (0, i))],
        out_specs=[pl.BlockSpec((n, LANES), lambda i: (0, i)),
                   pl.BlockSpec((n, half), lambda i: (0, i)),
                   pl.BlockSpec((n, half), lambda i: (0, i))],
        out_shape=[jax.ShapeDtypeStruct((n, D_SSM), F32),
                   jax.ShapeDtypeStruct((n, N_SSM_GROUPS * SSM_STATE), F32),
                   jax.ShapeDtypeStruct((n, N_SSM_GROUPS * SSM_STATE), F32)],
        compiler_params=_cparams(("parallel",)),
        name="s5_sample",
    )(u, h0_re, h0_im, b8, c8, a_re, a_im, d)


def _layer_norm(x, g, b):
    mu = jnp.mean(x, axis=-1, keepdims=True)
    xc = x - mu
    var = jnp.mean(xc * xc, axis=-1, keepdims=True)
    return xc * lax.rsqrt(var + LN_EPS) * g + b


def _sigmoid(x):
    return 0.5 * jnp.tanh(0.5 * x) + 0.5


RUN_ROWS = SUBLANES
TAB_ROWS = 3


def _merge_kernel(x_ref, oa_ref, ys_ref, carry_in_ref, wao_ref, wso_ref, wg_ref, bg_ref, wo_ref,
                  g1_ref, b1_ref, wrt_ref, brt_ref,
                  x1_ref, lpos_ref, cols_ref, tab_ref, carry_out_ref, carry_sc, *, f32_matmuls):
    step = pl.program_id(0)

    @pl.when(step == 0)
    def _():
        carry_sc[...] = carry_in_ref[...]

    def mm(a, w_ref):
        if f32_matmuls:
            return jnp.dot(a.astype(F32), w_ref[...], preferred_element_type=F32, precision=lax.Precision.HIGHEST)
        return jnp.dot(a.astype(BF16), w_ref[...], preferred_element_type=F32)

    tm = x_ref.shape[0]
    x = x_ref[...]
    branch_a = mm(oa_ref[...], wao_ref)
    z = mm(jax.nn.gelu(ys_ref[...].astype(F32)), wso_ref)
    branch_b = z[:, :D_MODEL] * _sigmoid(z[:, D_MODEL:])
    gates = _sigmoid(mm(x, wg_ref) + bg_ref[...])
    mixed = gates[:, :D_MODEL] * branch_a + gates[:, D_MODEL:] * branch_b
    mix = mm(mixed, wo_ref)
    x1 = _layer_norm(DEEPNORM_ALPHA * x + mix, g1_ref[...], b1_ref[...])
    x1_ref[...] = x1

    def split2(v):
        hi = v.astype(BF16)
        return hi, (v - hi.astype(F32)).astype(BF16)

    def dot_nt(a, b):
        return lax.dot_general(a, b, (((1,), (1,)), ((), ())), preferred_element_type=F32)

    w_hi, w_lo = split2(wrt_ref[...])
    rt = tm // tab_ref.shape[0]
    sub = lax.broadcasted_iota(jnp.int32, (N_EXPERTS, rt), 0)
    r = lax.broadcasted_iota(jnp.int32, (rt, rt), 0)
    c = lax.broadcasted_iota(jnp.int32, (rt, rt), 1)
    er = lax.broadcasted_iota(jnp.int32, (N_EXPERTS, N_EXPERTS), 0)
    ec = lax.broadcasted_iota(jnp.int32, (N_EXPERTS, N_EXPERTS), 1)
    rid = lax.broadcasted_iota(jnp.int32, (SUBLANES, LANES), 0)
    lane_pad = jnp.zeros((SUBLANES, LANES - N_EXPERTS), F32)
    for h in range(tab_ref.shape[0]):
        x_hi, x_lo = split2(x1[h * rt:(h + 1) * rt])
        logits = dot_nt(w_hi, x_hi) + dot_nt(w_hi, x_lo) + dot_nt(w_lo, x_hi) + brt_ref[...]
        work = logits
        vals, sels = [], []
        for _ in range(TOP_K):
            mx = jnp.max(work, axis=0, keepdims=True)
            idx = jnp.min(jnp.where(work == mx, sub, N_EXPERTS), axis=0, keepdims=True)
            sel = sub == idx
            vals.append(mx)
            sels.append(sel)
            work = jnp.where(sel, -jnp.inf, work)
        ex = [jnp.exp(v - vals[0]) for v in vals]
        tot = ex[0] + ex[1] + ex[2] + ex[3]
        gate_rows = jnp.concatenate([e / tot for e in ex], axis=0)

        multi = jnp.zeros((N_EXPERTS, rt), F32)
        for sel in sels:
            multi = multi + jnp.where(sel, 1.0, 0.0)
        multi_b = multi.astype(BF16)
        earlier = jnp.dot(multi_b, jnp.where(r < c, 1.0, 0.0).astype(BF16), preferred_element_type=F32)
        cnt_col = jnp.sum(multi, axis=1, keepdims=True)
        nb_col = jnp.floor((cnt_col + (RUN_ROWS - 1.0)) * (1.0 / RUN_ROWS))
        loff_col = jnp.dot(jnp.where(ec < er, 1.0, 0.0).astype(BF16),
                           jnp.broadcast_to(nb_col, (N_EXPERTS, rt)).astype(BF16), preferred_element_type=F32)
        base = RUN_ROWS * loff_col + earlier
        lpos = jnp.concatenate([jnp.sum(jnp.where(sel, base, 0.0), axis=0, keepdims=True) for sel in sels],
                               axis=0)
        lpos_ref[:, h * rt:(h + 1) * rt] = lpos.astype(jnp.int32)
        rows_hi, rows_lo = split2(jnp.concatenate([lpos, gate_rows], axis=0))
        eye = jnp.where(r == c, 1.0, 0.0).astype(BF16)
        cols_ref[h * rt:(h + 1) * rt, :] = dot_nt(eye, rows_hi) + dot_nt(eye, rows_lo)

        cnt_row = dot_nt(jnp.ones((SUBLANES, rt), BF16), multi_b)
        nb_row = jnp.floor((cnt_row + (RUN_ROWS - 1.0)) * (1.0 / RUN_ROWS))
        loff_row = jnp.dot(nb_row.astype(BF16), jnp.where(er < ec, 1.0, 0.0).astype(BF16),
                           preferred_element_type=F32)
        nb_p = jnp.concatenate([nb_row, lane_pad], axis=1)
        loff_p = jnp.concatenate([loff_row, lane_pad], axis=1)
        goff_p = carry_sc[...]
        tab = jnp.where(rid == 0, nb_p, jnp.where(rid == 1, loff_p, jnp.where(rid == 2, goff_p, 0.0)))
        tab_ref[h] = tab.astype(jnp.int32)
        carry_sc[...] = goff_p + nb_p
    carry_out_ref[...] = carry_sc[...]


def _merge(x, o_attn, y_ssm, carry_in, w, *, tile, route_tile, f32_matmuls):
    n = x.shape[0]
    nt = n // tile
    per_step = tile // route_tile
    full = lambda shape: pl.BlockSpec(shape, lambda i: (0,) * len(shape))
    return pl.pallas_call(
        functools.partial(_merge_kernel, f32_matmuls=f32_matmuls),
        grid=(nt,),
        in_specs=[pl.BlockSpec((tile, D_MODEL), lambda i: (i, 0)),
                  pl.BlockSpec((tile, D_ATTN), lambda i: (i, 0)),
                  pl.BlockSpec((tile, D_SSM), lambda i: (i, 0)),
                  full((SUBLANES, LANES)),
                  full((D_ATTN, D_MODEL)), full((D_SSM, 2 * D_MODEL)), full((D_MODEL, 2 * D_MODEL)),
                  full((1, 2 * D_MODEL)), full((D_MODEL, D_MODEL)),
                  full((1, D_MODEL)), full((1, D_MODEL)),
                  full((N_EXPERTS, D_MODEL)), full((N_EXPERTS, 1))],
        out_specs=[pl.BlockSpec((tile, D_MODEL), lambda i: (i, 0)),
                   pl.BlockSpec((TOP_K, tile), lambda i: (0, i)),
                   pl.BlockSpec((tile, 2 * TOP_K), lambda i: (i, 0)),
                   pl.BlockSpec((per_step, SUBLANES, LANES), lambda i: (i, 0, 0)),
                   full((SUBLANES, LANES))],
        out_shape=[jax.ShapeDtypeStruct((n, D_MODEL), F32),
                   jax.ShapeDtypeStruct((TOP_K, n), jnp.int32),
                   jax.ShapeDtypeStruct((n, 2 * TOP_K), F32),
                   jax.ShapeDtypeStruct((nt * per_step, SUBLANES, LANES), jnp.int32),
                   jax.ShapeDtypeStruct((SUBLANES, LANES), F32)],
        scratch_shapes=[pltpu.VMEM((SUBLANES, LANES), F32)],
        compiler_params=_cparams(("arbitrary",), VMEM_LIMIT_MERGE),
        name="merge",
    )(x, o_attn, y_ssm, carry_in, w["wao"], w["wso"], w["wg"], w["bg"], w["wo"], w["g1"], w["b1"],
      w["wrt"], w["brt"])


def _tab(tab_ref, tile, row, e):
    return tab_ref[(tile * TAB_ROWS + row) * N_EXPERTS + e]


BIG_PIECE = 4 * RUN_ROWS
MAX_UNITS_LOG2 = 8


def _for_each_run_piece(tab_ref, tile, fn):
    def per_expert(e, carry):
        loff = RUN_ROWS * _tab(tab_ref, tile, 1, e)
        goff = RUN_ROWS * _tab(tab_ref, tile, 2, e)
        units = _tab(tab_ref, tile, 0, e)
        n_big = lax.shift_right_logical(units, 2)

        def big(j, c2):
            fn(pl.multiple_of(loff + j * BIG_PIECE, RUN_ROWS), goff + j * BIG_PIECE, e, BIG_PIECE)
            return c2

        lax.fori_loop(0, n_big, big, 0)
        done = n_big * BIG_PIECE

        def small(j, c2):
            fn(pl.multiple_of(loff + done + j * RUN_ROWS, RUN_ROWS), goff + done + j * RUN_ROWS, e, RUN_ROWS)
            return c2

        lax.fori_loop(0, units & 3, small, 0)
        return carry

    lax.fori_loop(0, N_EXPERTS, per_expert, 0)


def _drain_units(units, wait_copy, buffer_rows):
    assert buffer_rows < (RUN_ROWS << MAX_UNITS_LOG2)
    for b in range(MAX_UNITS_LOG2):
        if (RUN_ROWS << b) > buffer_rows:
            break

        @pl.when((lax.shift_right_logical(units, b) & 1) == 1)
        def _():
            wait_copy(RUN_ROWS << b).wait()


def _dispatch_kernel(tab_ref, seg_ref, tot_ref, tail_ref, lpos_p_ref, xp_ref, lpos_s_ref, xs_in_ref, xs_ref,
                     loc_sc, zero_sc, sem, zsem):
    i = pl.program_id(0)
    last = pl.num_programs(0) - 1
    tile = i
    slot = i % 2
    loc = loc_sc.shape[1]

    @pl.when(i == 0)
    def _():
        zero_sc[...] = jnp.zeros_like(zero_sc)

        def tail_copy(e, j):
            row = pl.multiple_of(RUN_ROWS * (tail_ref[e] + j), RUN_ROWS)
            return pltpu.make_async_copy(zero_sc.at[pl.ds(0, RUN_ROWS)], xs_ref.at[pl.ds(row, RUN_ROWS)], zsem)

        def per_expert(e, carry):
            n = tail_ref[N_EXPERTS + e]
            lax.fori_loop(0, n, lambda j, c2: (tail_copy(e, j).start(), c2)[1], 0)
            lax.fori_loop(0, n, lambda j, c2: (tail_copy(e, j).wait(), c2)[1], 0)
            return carry

        lax.fori_loop(0, N_EXPERTS, per_expert, 0)

        def block_copy(b):
            row = pl.multiple_of(b * MOE_ROWS, MOE_ROWS)
            return pltpu.make_async_copy(zero_sc, xs_ref.at[pl.ds(row, MOE_ROWS)], zsem)

        first_unused, n_blocks = tail_ref[2 * N_EXPERTS], xs_ref.shape[0] // MOE_ROWS
        lax.fori_loop(first_unused, n_blocks, lambda b, c2: (block_copy(b).start(), c2)[1], 0)
        lax.fori_loop(first_unused, n_blocks, lambda b, c2: (block_copy(b).wait(), c2)[1], 0)

    def sort_tile(lpos_ref, x_ref):
        tm = x_ref.shape[0]
        rows = lax.broadcasted_iota(jnp.int32, (loc, tm), 0)
        lp = lpos_ref[...]
        onehot = jnp.zeros((loc, tm), F32)
        for k in range(TOP_K):
            onehot = jnp.where(rows == lp[k:k + 1], 1.0, onehot)
        loc_sc[slot] = jnp.dot(onehot.astype(BF16), x_ref[...].astype(BF16), preferred_element_type=F32)

    @pl.when(i < last)
    def _():
        sort_tile(lpos_p_ref, xp_ref)

    @pl.when(i == last)
    def _():
        sort_tile(lpos_s_ref, xs_in_ref)

    def piece_copy(sl, lrow, grow, e, n):
        dst = pl.multiple_of(seg_ref[e] + grow, RUN_ROWS)
        return pltpu.make_async_copy(loc_sc.at[sl, pl.ds(lrow, n)], xs_ref.at[pl.ds(dst, n)], sem.at[sl])

    _for_each_run_piece(tab_ref, tile, lambda lrow, grow, e, n: piece_copy(slot, lrow, grow, e, n).start())

    def drain(tl, sl):
        _drain_units(tot_ref[tl], lambda n: piece_copy(sl, 0, 0, 0, n), loc)

    @pl.when(i > 0)
    def _():
        drain(tile - 1, 1 - slot)

    @pl.when(i == last)
    def _():
        drain(tile, slot)


def _dispatch(tab, seg_start, tot, tails, lpos_p, x1_p, lpos_s, x1_s, *, tile, nrows):
    nt_p = x1_p.shape[0] // tile
    ns = x1_s.shape[0]
    loc = tile * TOP_K + N_EXPERTS * RUN_ROWS
    prompt_blk = lambda i, *_: jnp.minimum(i, nt_p - 1)
    return pl.pallas_call(
        _dispatch_kernel,
        grid_spec=pltpu.PrefetchScalarGridSpec(
            num_scalar_prefetch=4,
            grid=(nt_p + 1,),
            in_specs=[pl.BlockSpec((TOP_K, tile), lambda i, *_: (0, prompt_blk(i))),
                      pl.BlockSpec((tile, D_MODEL), lambda i, *_: (prompt_blk(i), 0)),
                      pl.BlockSpec((TOP_K, ns), lambda i, *_: (0, 0)),
                      pl.BlockSpec((ns, D_MODEL), lambda i, *_: (0, 0))],
            out_specs=pl.BlockSpec(memory_space=pl.ANY),
            scratch_shapes=[pltpu.VMEM((2, loc, D_MODEL), F32), pltpu.VMEM((MOE_ROWS, D_MODEL), F32),
                            pltpu.SemaphoreType.DMA((2,)), pltpu.SemaphoreType.DMA(())]),
        out_shape=jax.ShapeDtypeStruct((nrows, D_MODEL), F32),
        compiler_params=_cparams(("arbitrary",), VMEM_LIMIT),
        name="dispatch",
    )(tab, seg_start, tot, tails, lpos_p, x1_p, lpos_s, x1_s)


def _deinterleave_matrix():
    pm = np.zeros((MXU_DIM, MXU_DIM), np.float32)
    half = MXU_DIM // 2
    for c in range(half):
        pm[2 * c, c] = 1.0
        pm[2 * c + 1, half + c] = 1.0
    return pm


def _expert_kernel(be_ref, nu_ref, nv_ref, ord_ref, nxt_ref, xs_ref, w1_hbm, b1_ref, w2_hbm, b2_ref, pm_ref, y_ref,
                   w1f_sc, w2f_sc, w1p_sc, w2b_sc, sem):
    del nu_ref
    i = pl.program_id(0)
    e = be_ref[i]
    prev = be_ref[jnp.maximum(i - 1, 0)]
    nblk = 2 * D_FF // MXU_DIM

    def weight_copies(expert, slot):
        return (pltpu.make_async_copy(w1_hbm.at[expert], w1f_sc.at[slot], sem.at[0, slot]),
                pltpu.make_async_copy(w2_hbm.at[expert], w2f_sc.at[slot], sem.at[1, slot]))

    @pl.when(i == 0)
    def _():
        for cp in weight_copies(e, 0):
            cp.start()

    @pl.when((i == 0) | (e != prev))
    def _():
        slot = ord_ref[i] % 2
        for cp in weight_copies(e, slot):
            cp.wait()

        for cb in range(nblk):
            blk = w1f_sc[slot, :, cb * MXU_DIM:(cb + 1) * MXU_DIM].astype(BF16)
            w1p_sc[:, cb * MXU_DIM:(cb + 1) * MXU_DIM] = jnp.dot(
                blk, pm_ref[...], preferred_element_type=F32).astype(BF16)
        w2b_sc[...] = w2f_sc[slot].astype(BF16)

        nxt = nxt_ref[i]

        @pl.when(nxt >= 0)
        def _():
            for cp in weight_copies(nxt, 1 - slot):
                cp.start()

    for blk in range(MOE_STEP_BLOCKS):
        rows = slice(blk * MOE_ROWS, (blk + 1) * MOE_ROWS)

        @pl.when(blk < nv_ref[i])
        def _():
            x = xs_ref[rows, :].astype(BF16)
            h = jnp.dot(x, w1p_sc[...], preferred_element_type=F32) + b1_ref[0]
            half = MXU_DIM // 2
            acts = []
            for cb in range(nblk):
                x_glu = jnp.minimum(h[:, cb * MXU_DIM:cb * MXU_DIM + half], SWIGLU_LIMIT)
                x_lin = jnp.clip(h[:, cb * MXU_DIM + half:(cb + 1) * MXU_DIM], -SWIGLU_LIMIT, SWIGLU_LIMIT)
                acts.append((x_glu * jax.nn.sigmoid(SWIGLU_ALPHA * x_glu) * (x_lin + 1.0)).astype(BF16))
            act = jnp.concatenate(acts, axis=1)
            y_ref[rows, :] = jnp.dot(act, w2b_sc[...], preferred_element_type=F32) + b2_ref[0]

        @pl.when(blk >= nv_ref[i])
        def _():
            y_ref[rows, :] = jnp.zeros((MOE_ROWS, D_MODEL), F32)


def _experts(block_e, n_used, n_valid, run_ord, run_next, xs, w1, b1p, w2, b2, pm):
    nrows = xs.shape[0]
    step_rows = MOE_STEP_BLOCKS * MOE_ROWS
    nb = nrows // step_rows
    return pl.pallas_call(
        _expert_kernel,
        grid_spec=pltpu.PrefetchScalarGridSpec(
            num_scalar_prefetch=5,
            grid=(nb,),
            in_specs=[pl.BlockSpec((step_rows, D_MODEL), lambda i, be, nu, *_: (jnp.minimum(i, nu[0] - 1), 0)),
                      pl.BlockSpec(memory_space=pl.ANY),
                      pl.BlockSpec((1, 1, 2 * D_FF), lambda i, be, *_: (be[i], 0, 0)),
                      pl.BlockSpec(memory_space=pl.ANY),
                      pl.BlockSpec((1, 1, D_MODEL), lambda i, be, *_: (be[i], 0, 0)),
                      pl.BlockSpec((MXU_DIM, MXU_DIM), lambda i, *_: (0, 0))],
            out_specs=pl.BlockSpec((step_rows, D_MODEL), lambda i, *_: (i, 0)),
            scratch_shapes=[pltpu.VMEM((2, D_MODEL, 2 * D_FF), F32), pltpu.VMEM((2, D_FF, D_MODEL), F32),
                            pltpu.VMEM((D_MODEL, 2 * D_FF), BF16), pltpu.VMEM((D_FF, D_MODEL), BF16),
                            pltpu.SemaphoreType.DMA((2, 2))]),
        out_shape=jax.ShapeDtypeStruct((nrows, D_MODEL), F32),
        compiler_params=_cparams(("arbitrary",), VMEM_LIMIT_MERGE),
        name="experts",
    )(block_e, n_used, n_valid, run_ord, run_next, xs, w1, b1p, w2, b2, pm)


def _combine_kernel(tab_ref, seg_ref, tot_ref, cols_ref, x1_ref, g2_ref, b2_ref, ys_ref, y_ref, loc_sc, sem,
                    *, tile_base):
    i = pl.program_id(0)
    last = pl.num_programs(0) - 1
    tile = i + tile_base
    slot = i % 2
    loc, tm = loc_sc.shape[1], x1_ref.shape[0]

    def piece_copy(sl, lrow, grow, e, n):
        src = pl.multiple_of(seg_ref[e] + grow, RUN_ROWS)
        return pltpu.make_async_copy(ys_ref.at[pl.ds(src, n)], loc_sc.at[sl, pl.ds(lrow, n)], sem.at[sl])

    def gather(tl, sl):
        _for_each_run_piece(tab_ref, tl, lambda lrow, grow, e, n: piece_copy(sl, lrow, grow, e, n).start())

    @pl.when(i == 0)
    def _():
        loc_sc[...] = jnp.zeros_like(loc_sc)
        gather(tile, slot)

    @pl.when(i < last)
    def _():
        gather(tile + 1, 1 - slot)

    _drain_units(tot_ref[tile], lambda n: piece_copy(slot, 0, 0, 0, n), loc)

    cols = cols_ref[...]
    lane = lax.broadcasted_iota(jnp.int32, (tm, loc), 1)
    weights = jnp.zeros((tm, loc), F32)
    for k in range(TOP_K):
        weights = jnp.where(lane == cols[:, k:k + 1].astype(jnp.int32), cols[:, TOP_K + k:TOP_K + k + 1], weights)
    ffn = jnp.dot(weights.astype(BF16), loc_sc[slot].astype(BF16), preferred_element_type=F32)
    y_ref[...] = _layer_norm(DEEPNORM_ALPHA * x1_ref[...] + ffn, g2_ref[...], b2_ref[...])


def _combine(tab, seg_start, tot, cols, x1, g2, b2, ys, *, tile, tile_base):
    n = x1.shape[0]
    loc = tile * TOP_K + N_EXPERTS * RUN_ROWS
    return pl.pallas_call(
        functools.partial(_combine_kernel, tile_base=tile_base),
        grid_spec=pltpu.PrefetchScalarGridSpec(
            num_scalar_prefetch=3,
            grid=(n // tile,),
            in_specs=[pl.BlockSpec((tile, 2 * TOP_K), lambda i, *_: (i, 0)),
                      pl.BlockSpec((tile, D_MODEL), lambda i, *_: (i, 0)),
                      pl.BlockSpec((1, D_MODEL), lambda i, *_: (0, 0)),
                      pl.BlockSpec((1, D_MODEL), lambda i, *_: (0, 0)),
                      pl.BlockSpec(memory_space=pl.ANY)],
            out_specs=pl.BlockSpec((tile, D_MODEL), lambda i, *_: (i, 0)),
            scratch_shapes=[pltpu.VMEM((2, loc, D_MODEL), F32), pltpu.SemaphoreType.DMA((2,))]),
        out_shape=jax.ShapeDtypeStruct((n, D_MODEL), F32),
        compiler_params=_cparams(("arbitrary",), VMEM_LIMIT),
        name="combine",
    )(tab, seg_start, tot, cols, x1, g2, b2, ys)


def kernel(x_prompt, x_sample, cache_k_win, cache_v_win, state_ssm_re, state_ssm_im, w_in, b_in, attn_sinks,
           w_attn_out, ssm_a_re, ssm_a_im, ssm_log_dt, ssm_b_re, ssm_b_im, ssm_c_re, ssm_c_im, ssm_d, w_ssm_out,
           w_gate, b_gate, w_out, ln1_g, ln1_b, w_router, b_router, w_exp1, b_exp1, w_exp2, b_exp2, ln2_g, ln2_b):
    assert w_in.shape[0] == DEPTH == 1
    bsz, seq, _ = x_prompt.shape
    nsamp = x_sample.shape[0]
    assert x_sample.shape[1] == 1
    n_p = bsz * seq
    n_tok = n_p + nsamp

    xp = x_prompt.reshape(n_p, D_MODEL)
    xsm = x_sample.reshape(nsamp, D_MODEL)
    b_in2 = b_in[0].reshape(1, D_IN)
    sinks = attn_sinks[0].astype(F32)

    q_p, k_p, v_p, u_p = _proj(xp, w_in[0].astype(BF16), b_in2, tile=512, exact_f32=False, q_dtype=BF16)
    q_s, k_s, v_s, u_s = _proj(xsm, w_in[0], b_in2, tile=nsamp, exact_f32=True, q_dtype=F32)

    o_p = _attn_prompt(sinks, q_p.reshape(bsz, seq, D_ATTN), k_p.reshape(bsz, seq, D_KV),
                       v_p.reshape(bsz, seq, D_KV)).reshape(n_p, D_ATTN)
    k_buf = cache_k_win[0].reshape(nsamp, WINDOW, D_KV)
    v_buf = cache_v_win[0].reshape(nsamp, WINDOW, D_KV)
    o_s = _attn_sample(sinks, q_s, k_s, v_s, k_buf, v_buf)

    sp = _s5_params(ssm_a_re[0], ssm_a_im[0], ssm_log_dt[0], ssm_b_re[0], ssm_b_im[0], ssm_c_re[0], ssm_c_im[0])
    y_p, hp_re, hp_im = _s5_prompt(u_p, bsz, seq, _s5_chunk_mats(sp, ssm_d[0]))
    y_s, hs_re, hs_im = _s5_sample(u_s, state_ssm_re[0].reshape(nsamp, -1), state_ssm_im[0].reshape(nsamp, -1),
                                   _s5_sample_mats(sp, ssm_d[0]))

    wm = dict(wao=w_attn_out[0].astype(BF16), wso=w_ssm_out[0].astype(BF16), wg=w_gate[0].astype(BF16),
              bg=b_gate[0].reshape(1, -1), wo=w_out[0].astype(BF16), g1=ln1_g[0].reshape(1, -1),
              b1=ln1_b[0].reshape(1, -1), wrt=w_router[0].T, brt=b_router[0].reshape(-1, 1))
    wm_f32 = dict(wm, wao=w_attn_out[0], wso=w_ssm_out[0], wg=w_gate[0], wo=w_out[0])
    carry0 = jnp.zeros((SUBLANES, LANES), F32)
    x1_p, lpos_p, cols_p, tab_p, carry1 = _merge(xp, o_p, y_p, carry0, wm, tile=MERGE_TILE, route_tile=TOK_TILE,
                                                 f32_matmuls=False)
    x1_s, lpos_s, cols_s, tab_s, carry2 = _merge(xsm, o_s, y_s, carry1, wm_f32, tile=nsamp, route_tile=nsamp,
                                                 f32_matmuls=True)

    nt_p = n_p // TOK_TILE
    tab = jnp.concatenate([tab_p[:, :TAB_ROWS, :N_EXPERTS], tab_s[:, :TAB_ROWS, :N_EXPERTS]], axis=0)
    tot = jnp.sum(tab[:, 0, :], axis=1).astype(jnp.int32)
    tab = tab.reshape(-1)
    seg_rows = carry2[0, :N_EXPERTS].astype(jnp.int32) * RUN_ROWS
    step_rows = MOE_STEP_BLOCKS * MOE_ROWS
    padded = ((seg_rows + step_rows - 1) // step_rows) * step_rows
    pad_end = jnp.cumsum(padded)
    pad_start = (pad_end - padded).astype(jnp.int32)
    seg_end = pad_start + seg_rows
    n_runs = (nt_p + 1) * N_EXPERTS
    nb_max = (n_tok * TOP_K + n_runs * (RUN_ROWS - 1) + N_EXPERTS * (step_rows - 1) + step_rows - 1) // step_rows
    n_used = (pad_end[-1] // step_rows).astype(jnp.int32)
    tails = jnp.concatenate([seg_end // RUN_ROWS, (padded - seg_rows) // RUN_ROWS,
                             (pad_end[-1:] // MOE_ROWS)]).astype(jnp.int32)
    blk_start = jnp.arange(nb_max, dtype=jnp.int32) * step_rows
    blk_e = jnp.minimum(jnp.sum(blk_start[:, None] >= pad_end[None, :], axis=1), N_EXPERTS - 1).astype(jnp.int32)
    used = jnp.arange(nb_max) < n_used
    blk_e = jnp.where(used, blk_e, jnp.max(jnp.where(used, blk_e, 0)))
    ids = jnp.arange(N_EXPERTS, dtype=jnp.int32)
    of_blk = blk_e[:, None] == ids[None, :]
    n_valid = jnp.clip((jnp.sum(jnp.where(of_blk, seg_end[None, :], 0), axis=1) - blk_start + MOE_ROWS - 1)
                       // MOE_ROWS, 0, MOE_STEP_BLOCKS)
    n_valid = jnp.where(used, n_valid, 0).astype(jnp.int32)
    new_run = jnp.concatenate([jnp.ones((1,), jnp.int32), (blk_e[1:] != blk_e[:-1]).astype(jnp.int32)])
    run_ord = (jnp.cumsum(new_run) - 1).astype(jnp.int32)
    later = (ids[None, :] > ids[:, None]) & (padded > 0)[None, :]
    next_e = jnp.min(jnp.where(later, ids[None, :], N_EXPERTS), axis=1)
    next_e = jnp.where(next_e < N_EXPERTS, next_e, -1).astype(jnp.int32)
    run_next = jnp.sum(jnp.where(of_blk, next_e[None, :], 0), axis=1).astype(jnp.int32)

    nrows = nb_max * step_rows
    xs = _dispatch(tab, pad_start, tot, tails, lpos_p, x1_p, lpos_s, x1_s, tile=TOK_TILE, nrows=nrows)

    b1p = b_exp1[0].reshape(N_EXPERTS, 2 * D_FF // MXU_DIM, MXU_DIM // 2, 2)
    b1p = jnp.swapaxes(b1p, 2, 3).reshape(N_EXPERTS, 1, 2 * D_FF)
    ys = _experts(blk_e, n_used.reshape(1), n_valid, run_ord, run_next, xs, w_exp1[0], b1p, w_exp2[0],
                  b_exp2[0].reshape(N_EXPERTS, 1, D_MODEL),
                  jnp.asarray(_deinterleave_matrix(), BF16))

    g2, b2 = ln2_g[0].reshape(1, -1), ln2_b[0].reshape(1, -1)
    y_prompt = _combine(tab, pad_start, tot, cols_p, x1_p, g2, b2, ys, tile=TOK_TILE, tile_base=0)
    y_sample = _combine(tab, pad_start, tot, cols_s, x1_s, g2, b2, ys, tile=nsamp, tile_base=nt_p)

    k_p4 = k_p.reshape(bsz, seq, D_KV)[:, -WINDOW:].reshape(bsz, WINDOW, N_KV_HEADS, HEAD_DIM)
    v_p4 = v_p.reshape(bsz, seq, D_KV)[:, -WINDOW:].reshape(bsz, WINDOW, N_KV_HEADS, HEAD_DIM)
    k_s4 = jnp.concatenate([cache_k_win[0][:, 1:], k_s.reshape(nsamp, 1, N_KV_HEADS, HEAD_DIM)], axis=1)
    v_s4 = jnp.concatenate([cache_v_win[0][:, 1:], v_s.reshape(nsamp, 1, N_KV_HEADS, HEAD_DIM)], axis=1)
    st = lambda a, n: a.reshape(1, n, N_SSM_GROUPS, SSM_STATE)
    return (y_prompt.reshape(bsz, seq, D_MODEL), y_sample.reshape(nsamp, 1, D_MODEL),
            k_p4[None], v_p4[None], st(hp_re, bsz), st(hp_im, bsz),
            k_s4[None], v_s4[None], st(hs_re, nsamp), st(hs_im, nsamp))
```

```python
import functools

import numpy as np
import jax
import jax.numpy as jnp
from jax import lax
from jax.experimental import pallas as pl
from jax.experimental.pallas import tpu as pltpu

F32 = jnp.float32
BF16 = jnp.bfloat16

D_MODEL = 1024
HEAD_DIM = 64
N_Q_HEADS = 8
N_KV_HEADS = 2
Q_PER_KV = N_Q_HEADS // N_KV_HEADS
D_ATTN = N_Q_HEADS * HEAD_DIM
D_KV = N_KV_HEADS * HEAD_DIM
WINDOW = 128
ATTN_SCALE = HEAD_DIM ** -0.5
SSM_GROUP = 16
D_SSM = D_MODEL // 2
N_SSM_GROUPS = D_SSM // SSM_GROUP
SSM_STATE = 64
D_IN = D_ATTN + 2 * D_KV + D_SSM
N_EXPERTS = 32
TOP_K = 4
D_FF = D_MODEL
SWIGLU_LIMIT = 7.0
SWIGLU_ALPHA = 1.702
LN_EPS = 1e-5
DEPTH = 1
DEEPNORM_ALPHA = (2 * DEPTH) ** 0.25

LANES = 128
SUBLANES = 8
MXU_DIM = 256

S5_CHUNK = MXU_DIM // SSM_GROUP
S5_LANE_GROUPS = LANES // SSM_GROUP
MOE_ROWS = 256
MOE_STEP_BLOCKS = 2
TOK_TILE = 256
MERGE_TILE = 512
VMEM_LIMIT = 48 * 1024 * 1024
VMEM_LIMIT_LARGE = 56 * 1024 * 1024


def _cparams(sem, vmem=None):
    return pltpu.CompilerParams(dimension_semantics=sem, vmem_limit_bytes=vmem)


def _split_bf16(v):
    hi = v.astype(BF16)
    return hi, (v - hi.astype(F32)).astype(BF16)


def _dot_split(a, b, dims=(((1,), (0,)), ((), ()))):
    a_hi, a_lo = _split_bf16(a)
    b_hi, b_lo = _split_bf16(b)
    dot = lambda p, q: lax.dot_general(p, q, dims, preferred_element_type=F32)
    return dot(a_hi, b_hi) + dot(a_hi, b_lo) + dot(a_lo, b_hi)


def _proj_kernel(x_ref, w_ref, b_ref, q_ref, k_ref, v_ref, u_ref, *, exact_f32):
    if exact_f32:
        h = jnp.dot(x_ref[...], w_ref[...], preferred_element_type=F32, precision=lax.Precision.HIGHEST)
    else:
        h = jnp.dot(x_ref[...].astype(BF16), w_ref[...], preferred_element_type=F32)
    h = h + b_ref[...]
    q_ref[...] = (h[:, :D_ATTN] * ATTN_SCALE).astype(q_ref.dtype)
    k_ref[...] = h[:, D_ATTN:D_ATTN + D_KV]
    v_ref[...] = h[:, D_ATTN + D_KV:D_ATTN + 2 * D_KV]
    u_ref[...] = h[:, D_ATTN + 2 * D_KV:].astype(u_ref.dtype)


def _proj(x, w, b, *, tile, exact_f32, q_dtype):
    n = x.shape[0]
    return pl.pallas_call(
        functools.partial(_proj_kernel, exact_f32=exact_f32),
        grid=(n // tile,),
        in_specs=[pl.BlockSpec((tile, D_MODEL), lambda i: (i, 0)),
                  pl.BlockSpec((D_MODEL, D_IN), lambda i: (0, 0)),
                  pl.BlockSpec((1, D_IN), lambda i: (0, 0))],
        out_specs=[pl.BlockSpec((tile, D_ATTN), lambda i: (i, 0)),
                   pl.BlockSpec((tile, D_KV), lambda i: (i, 0)),
                   pl.BlockSpec((tile, D_KV), lambda i: (i, 0)),
                   pl.BlockSpec((tile, D_SSM), lambda i: (i, 0))],
        out_shape=[jax.ShapeDtypeStruct((n, D_ATTN), q_dtype),
                   jax.ShapeDtypeStruct((n, D_KV), F32),
                   jax.ShapeDtypeStruct((n, D_KV), F32),
                   jax.ShapeDtypeStruct((n, D_SSM), F32)],
        compiler_params=_cparams(("parallel",)),
        name="proj",
    )(x, w, b)


ATT_Q_TILE = 512


def _attn_prompt_kernel(sink_ref, q_ref, k_ref, v_ref, o_ref):
    i = pl.program_id(1)
    nk, nq = 2 * WINDOW, 2 * WINDOW
    lo = lax.broadcasted_iota(jnp.int32, (nk, LANES), 1) < HEAD_DIM
    top = lax.broadcasted_iota(jnp.int32, (nq, 1), 0) < WINDOW
    for blk in range(ATT_Q_TILE // WINDOW):
        q0 = i * ATT_Q_TILE + blk * WINDOW
        k0 = pl.multiple_of(jnp.maximum(q0 - WINDOW, 0), WINDOW)
        kk = k_ref[0, pl.ds(k0, nk), :]
        vv = v_ref[0, pl.ds(k0, nk), :]
        kk_sw = pltpu.roll(kk, HEAD_DIM, axis=1)
        vv_sw = pltpu.roll(vv, HEAD_DIM, axis=1)
        k_var = [[jnp.where(lo, kk, 0.0).astype(BF16), jnp.where(lo, 0.0, kk_sw).astype(BF16)],
                 [jnp.where(lo, kk_sw, 0.0).astype(BF16), jnp.where(lo, 0.0, kk).astype(BF16)]]
        v_var = [[jnp.where(lo, vv, 1.0).astype(BF16), jnp.where(lo, 1.0, vv_sw).astype(BF16)],
                 [jnp.where(lo, vv_sw, 1.0).astype(BF16), jnp.where(lo, 1.0, vv).astype(BF16)]]
        qpos = q0 + lax.broadcasted_iota(jnp.int32, (nq, nk), 0) % WINDOW
        kpos = k0 + lax.broadcasted_iota(jnp.int32, (nq, nk), 1)
        valid = (kpos <= qpos) & (qpos - kpos <= WINDOW)
        rows = slice(blk * WINDOW, (blk + 1) * WINDOW)
        for kv in range(N_KV_HEADS):
            pairs = (2 * kv, 2 * kv + 1)
            qs = jnp.concatenate([q_ref[0, rows, pr * LANES:(pr + 1) * LANES] for pr in pairs], axis=0)
            outs = []
            for parity in range(2):
                sink = jnp.where(top, sink_ref[2 * pairs[0] + parity], sink_ref[2 * pairs[1] + parity])
                s = lax.dot_general(qs, k_var[kv][parity], (((1,), (1,)), ((), ())), preferred_element_type=F32)
                s = jnp.where(valid, s, -jnp.inf)
                m = jnp.maximum(jnp.max(s, axis=-1, keepdims=True), sink)
                p = jnp.exp(s - m).astype(BF16)
                acc = jnp.dot(p, v_var[kv][parity], preferred_element_type=F32)
                outs.append(acc / (pltpu.roll(acc, HEAD_DIM, axis=1) + jnp.exp(sink - m)))
            o = jnp.where(lo, outs[0], outs[1]).astype(o_ref.dtype)
            for j, pr in enumerate(pairs):
                o_ref[0, rows, pr * LANES:(pr + 1) * LANES] = o[j * WINDOW:(j + 1) * WINDOW]


def _attn_prompt(sinks, q, k, v):
    bsz, seq = q.shape[0], q.shape[1]
    return pl.pallas_call(
        _attn_prompt_kernel,
        grid=(bsz, seq // ATT_Q_TILE),
        in_specs=[pl.BlockSpec(memory_space=pltpu.SMEM),
                  pl.BlockSpec((1, ATT_Q_TILE, D_ATTN), lambda b, i: (b, i, 0)),
                  pl.BlockSpec((1, seq, D_KV), lambda b, i: (b, 0, 0)),
                  pl.BlockSpec((1, seq, D_KV), lambda b, i: (b, 0, 0))],
        out_specs=pl.BlockSpec((1, ATT_Q_TILE, D_ATTN), lambda b, i: (b, i, 0)),
        out_shape=jax.ShapeDtypeStruct((bsz, seq, D_ATTN), BF16),
        compiler_params=_cparams(("parallel", "parallel")),
        name="attn_prompt",
    )(sinks, q, k, v)


ATT_S_GROUP = 16


def _attn_sample_kernel(sink_ref, q_ref, kn_ref, vn_ref, kb_ref, vb_ref, o_ref):
    g = ATT_S_GROUP
    rows = Q_PER_KV * g
    ncol = g * WINDOW
    kb = kb_ref[...].reshape(ncol, D_KV)
    vb = vb_ref[...].reshape(ncol, D_KV)
    rseq = lax.broadcasted_iota(jnp.int32, (rows, ncol), 0) % g
    cseq = lax.broadcasted_iota(jnp.int32, (rows, ncol), 1) // WINDOW
    own = rseq == cseq
    rhead = lax.broadcasted_iota(jnp.int32, (rows, 1), 0) // g
    for kv in range(N_KV_HEADS):
        lo = kv * HEAD_DIM
        qs = jnp.concatenate(
            [q_ref[:, (kv * Q_PER_KV + h) * HEAD_DIM:(kv * Q_PER_KV + h + 1) * HEAD_DIM] for h in range(Q_PER_KV)],
            axis=0)
        kn = jnp.concatenate([kn_ref[:, lo:lo + HEAD_DIM]] * Q_PER_KV, axis=0)
        vn = jnp.concatenate([vn_ref[:, lo:lo + HEAD_DIM]] * Q_PER_KV, axis=0)
        sink = jnp.zeros((rows, 1), F32)
        for h in range(Q_PER_KV):
            sink = jnp.where(rhead == h, sink_ref[kv * Q_PER_KV + h], sink)
        qs = qs.astype(F32)
        s = _dot_split(qs, kb[:, lo:lo + HEAD_DIM], (((1,), (1,)), ((), ())))
        s = jnp.where(own, s, -jnp.inf)
        s_new = jnp.sum(qs * kn, axis=-1, keepdims=True)
        m = jnp.maximum(jnp.maximum(jnp.max(s, axis=-1, keepdims=True), s_new), sink)
        p = jnp.exp(s - m)
        p_new = jnp.exp(s_new - m)
        denom = jnp.sum(p, axis=-1, keepdims=True) + p_new + jnp.exp(sink - m)
        o = (_dot_split(p, vb[:, lo:lo + HEAD_DIM]) + p_new * vn) / denom
        for h in range(Q_PER_KV):
            c0 = (kv * Q_PER_KV + h) * HEAD_DIM
            o_ref[:, c0:c0 + HEAD_DIM] = o[h * g:(h + 1) * g].astype(o_ref.dtype)


def _attn_sample(sinks, q, k_new, v_new, k_buf, v_buf):
    n = q.shape[0]
    g = ATT_S_GROUP
    return pl.pallas_call(
        _attn_sample_kernel,
        grid=(n // g,),
        in_specs=[pl.BlockSpec(memory_space=pltpu.SMEM),
                  pl.BlockSpec((g, D_ATTN), lambda i: (i, 0)),
                  pl.BlockSpec((g, D_KV), lambda i: (i, 0)),
                  pl.BlockSpec((g, D_KV), lambda i: (i, 0)),
                  pl.BlockSpec((g, WINDOW, D_KV), lambda i: (i, 0, 0)),
                  pl.BlockSpec((g, WINDOW, D_KV), lambda i: (i, 0, 0))],
        out_specs=pl.BlockSpec((g, D_ATTN), lambda i: (i, 0)),
        out_shape=jax.ShapeDtypeStruct((n, D_ATTN), F32),
        compiler_params=_cparams(("parallel",)),
        name="attn_sample",
    )(sinks, q, k_new, v_new, k_buf, v_buf)


def _s5_params(a_re, a_im, log_dt, b_re, b_im, c_re, c_im):
    hp = lax.Precision.HIGHEST
    dt = jnp.exp(log_dt.astype(F32))[:, None]
    are, aim = a_re.astype(F32), a_im.astype(F32)
    tau = jnp.arange(S5_CHUNK + 1, dtype=F32)[None, :, None]
    mag = jnp.exp(tau * (dt * are)[:, None, :])
    ang = tau * (dt * aim)[:, None, :]
    pw_re, pw_im = mag * jnp.cos(ang), mag * jnp.sin(ang)
    ab_re, ab_im = pw_re[:, 1], pw_im[:, 1]
    den = are * are + aim * aim
    f_re = ((ab_re - 1.0) * are + ab_im * aim) / den
    f_im = (ab_im * are - (ab_re - 1.0) * aim) / den
    bre, bim = b_re.astype(F32), b_im.astype(F32)
    bb_re = f_re[..., None] * bre - f_im[..., None] * bim
    bb_im = f_re[..., None] * bim + f_im[..., None] * bre
    cre, cim = c_re.astype(F32), c_im.astype(F32)
    return dict(pw_re=pw_re, pw_im=pw_im, ab_re=ab_re, ab_im=ab_im, bb_re=bb_re, bb_im=bb_im,
                c_re=cre, c_im=cim, hp=hp)


def _s5_chunk_mats(sp, d_skip):
    hp = sp["hp"]
    g, t, c, p = N_SSM_GROUPS, S5_CHUNK, SSM_GROUP, SSM_STATE
    pw_re, pw_im = sp["pw_re"], sp["pw_im"]
    ca_re = sp["c_re"][:, None] * pw_re[:, :, None, :] - sp["c_im"][:, None] * pw_im[:, :, None, :]
    ca_im = sp["c_re"][:, None] * pw_im[:, :, None, :] + sp["c_im"][:, None] * pw_re[:, :, None, :]
    kern = (jnp.einsum("gtcp,gpd->gtcd", ca_re[:, :t], sp["bb_re"], precision=hp)
            - jnp.einsum("gtcp,gpd->gtcd", ca_im[:, :t], sp["bb_im"], precision=hp))
    kc = jnp.swapaxes(kern, 2, 3)
    kc = kc.at[:, 0].add(d_skip.astype(F32).reshape(g, 1, c) * jnp.eye(c, dtype=F32)[None])
    rev_re, rev_im = pw_re[:, t - 1::-1][:, :t], pw_im[:, t - 1::-1][:, :t]
    wst_re = rev_re[:, :, None, :] * jnp.swapaxes(sp["bb_re"], 1, 2)[:, None] \
        - rev_im[:, :, None, :] * jnp.swapaxes(sp["bb_im"], 1, 2)[:, None]
    wst_im = rev_re[:, :, None, :] * jnp.swapaxes(sp["bb_im"], 1, 2)[:, None] \
        + rev_im[:, :, None, :] * jnp.swapaxes(sp["bb_re"], 1, 2)[:, None]
    wo_re = jnp.transpose(ca_re[:, 1:t + 1], (0, 3, 1, 2))
    wo_im = -jnp.transpose(ca_im[:, 1:t + 1], (0, 3, 1, 2))
    nv, gl = g // S5_LANE_GROUPS, S5_LANE_GROUPS
    kc, wst_re, wst_im, wo_re, wo_im = lax.optimization_barrier((kc, wst_re, wst_im, wo_re, wo_im))
    kc5 =jnp.transpose(kc.reshape(nv, gl, t, c, c), (0, 2, 1, 3, 4))
    ws6 = jnp.transpose(jnp.stack([wst_re, wst_im], axis=3).reshape(nv, gl, t, c, 2, p),
                        (0, 2, 1, 3, 4, 5))
    wo6 = jnp.transpose(jnp.stack([wo_re, wo_im], axis=0).reshape(2, nv, gl, p, t, c),
                        (1, 0, 2, 3, 4, 5))
    kc5, ws6, wo6 = lax.optimization_barrier((kc5.astype(BF16), ws6.astype(BF16), wo6.astype(BF16)))
    spread_b = np.zeros((c, LANES), np.float32)
    spread_s = np.zeros((2 * p, 2 * gl * p), np.float32)
    spread_o = np.zeros((t * c, t * LANES), np.float32)
    for h in range(gl):
        spread_b[np.arange(c), h * c + np.arange(c)] = 1.0
        for ri in range(2):
            spread_s[ri * p + np.arange(p), ri * gl * p + h * p + np.arange(p)] = 1.0
        for tt in range(t):
            spread_o[tt * c + np.arange(c), tt * LANES + h * c + np.arange(c)] = 1.0
    at_re = pw_re[:, t].reshape(1, g * p)
    at_im = pw_im[:, t].reshape(1, g * p)
    return dict(kc=kc5.reshape(nv, t, LANES, c), ws=ws6.reshape(nv, t * LANES, 2 * p),
                wo=wo6.reshape(nv, 2 * gl * p, t * c), spread_b=jnp.asarray(spread_b, BF16),
                spread_s=jnp.asarray(spread_s, BF16), spread_o=jnp.asarray(spread_o, BF16),
                at_re=at_re, at_im=at_im)


def _s5_chunk_rows(u_ref, nchunk):
    return jnp.concatenate(
        [u_ref[pl.ds(s, nchunk, stride=S5_CHUNK), :] for s in range(S5_CHUNK)], axis=1).astype(BF16)


S5_SLABS = S5_LANE_GROUPS * SSM_STATE // LANES


S5_EXPAND_ROWS = 256
S5_C_SHIFT = SSM_GROUP.bit_length() - 1
S5_P_SHIFT = SSM_STATE.bit_length() - 1


def _s5_expand(dst_ref, compact_ref, spread_ref, row_shift, col_shift):
    n_rows, n_cols = dst_ref.shape
    col_g = lax.shift_right_logical(lax.broadcasted_iota(jnp.int32, (S5_EXPAND_ROWS, n_cols), 1), col_shift)
    for r0 in range(0, n_rows, S5_EXPAND_ROWS):
        row_g = lax.shift_right_logical(r0 + lax.broadcasted_iota(jnp.int32, (S5_EXPAND_ROWS, n_cols), 0), row_shift)
        same = ((row_g ^ col_g) & (S5_LANE_GROUPS - 1)) == 0
        blk = jnp.dot(compact_ref[r0:r0 + S5_EXPAND_ROWS, :], spread_ref[...], preferred_element_type=F32)
        dst_ref[r0:r0 + S5_EXPAND_ROWS, :] = jnp.where(same, blk, 0.0).astype(dst_ref.dtype)


def _s5_state_kernel(u_ref, ws_ref, spread_ref, sre_ref, sim_ref, wst_sc):
    nchunk = sre_ref.shape[1]

    @pl.when(pl.program_id(1) == 0)
    def _():
        _s5_expand(wst_sc, ws_ref.at[0], spread_ref, S5_C_SHIFT, S5_P_SHIFT)

    s = jnp.dot(_s5_chunk_rows(u_ref, nchunk), wst_sc[...], preferred_element_type=F32)
    for k in range(S5_SLABS):
        sre_ref[k] = s[:, k * LANES:(k + 1) * LANES]
        sim_ref[k] = s[:, (S5_SLABS + k) * LANES:(S5_SLABS + k + 1) * LANES]


def _s5_scan_kernel(sre_ref, sim_ref, are_ref, aim_ref, hre_ref, him_ref, fre_ref, fim_ref, *, bsz):
    nchunk = sre_ref.shape[1] // bsz
    are = [jnp.broadcast_to(are_ref[:, k * LANES:(k + 1) * LANES], (bsz, LANES)) for k in range(S5_SLABS)]
    aim = [jnp.broadcast_to(aim_ref[:, k * LANES:(k + 1) * LANES], (bsz, LANES)) for k in range(S5_SLABS)]

    def body(j, carry):
        rows = pl.ds(j, bsz, stride=nchunk)
        out = []
        for k in range(S5_SLABS):
            cre, cim = carry[2 * k], carry[2 * k + 1]
            hre_ref[k, rows, :] = cre
            him_ref[k, rows, :] = cim
            sr = sre_ref[k, rows, :]
            si = sim_ref[k, rows, :]
            out += [are[k] * cre - aim[k] * cim + sr, are[k] * cim + aim[k] * cre + si]
        return tuple(out)

    zero = jnp.zeros((bsz, LANES), F32)
    fin = lax.fori_loop(0, nchunk, body, (zero,) * (2 * S5_SLABS), unroll=4)
    fre_ref[...] = jnp.concatenate(fin[0::2], axis=1)
    fim_ref[...] = jnp.concatenate(fin[1::2], axis=1)


def _s5_out_kernel(u_ref, kc_ref, spread_b_ref, hre_ref, him_ref, wo_ref, spread_o_ref, y_ref, m_sc, wout_sc):
    nchunk = hre_ref.shape[1]

    @pl.when(pl.program_id(1) == 0)
    def _():
        rg = lax.shift_right_logical(lax.broadcasted_iota(jnp.int32, (LANES, LANES), 0), S5_C_SHIFT)
        cg = lax.shift_right_logical(lax.broadcasted_iota(jnp.int32, (LANES, LANES), 1), S5_C_SHIFT)
        zero_blk = jnp.zeros((LANES, LANES), BF16)
        lag_blk = [jnp.where(rg == cg, jnp.dot(kc_ref[0, tau], spread_b_ref[...], preferred_element_type=F32),
                             0.0).astype(BF16) for tau in range(S5_CHUNK)]
        for s in range(S5_CHUNK):
            for t in range(S5_CHUNK):
                m_sc[s * LANES:(s + 1) * LANES, t * LANES:(t + 1) * LANES] = lag_blk[t - s] if t >= s else zero_blk
        _s5_expand(wout_sc, wo_ref.at[0], spread_o_ref, S5_P_SHIFT, S5_C_SHIFT)

    hcat = jnp.concatenate([hre_ref[k] for k in range(S5_SLABS)] + [him_ref[k] for k in range(S5_SLABS)],
                           axis=1).astype(BF16)
    lhs = _s5_chunk_rows(u_ref, nchunk)
    y = jnp.concatenate(
        [jnp.dot(lhs[:, :j + MXU_DIM], m_sc[:j + MXU_DIM, j:j + MXU_DIM], preferred_element_type=F32)
         for j in range(0, S5_CHUNK * LANES, MXU_DIM)], axis=1)
    y = y + jnp.dot(hcat, wout_sc[...], preferred_element_type=F32)
    for s in range(S5_CHUNK):
        y_ref[pl.ds(s, nchunk, stride=S5_CHUNK), :] = y[:, s * LANES:(s + 1) * LANES]


def _s5_prompt(u, bsz, seq, mats):
    at_re, at_im = mats["at_re"], mats["at_im"]
    g, t, p, c = N_SSM_GROUPS, S5_CHUNK, SSM_STATE, SSM_GROUP
    nchunk = seq // t
    n = nchunk * bsz
    nv = g // S5_LANE_GROUPS
    half = S5_LANE_GROUPS * p
    s_re, s_im = pl.pallas_call(
        _s5_state_kernel,
        grid=(nv, bsz),
        in_specs=[pl.BlockSpec((seq, LANES), lambda v, b: (b, v)),
                  pl.BlockSpec((1, t * LANES, 2 * p), lambda v, b: (v, 0, 0)),
                  pl.BlockSpec((2 * p, 2 * half), lambda v, b: (0, 0))],
        out_specs=[pl.BlockSpec((S5_SLABS, nchunk, LANES), lambda v, b: (v, b, 0)),
                   pl.BlockSpec((S5_SLABS, nchunk, LANES), lambda v, b: (v, b, 0))],
        out_shape=[jax.ShapeDtypeStruct((nv * S5_SLABS, n, LANES), F32)] * 2,
        scratch_shapes=[pltpu.VMEM((t * LANES, 2 * half), BF16)],
        compiler_params=_cparams(("parallel", "arbitrary"), VMEM_LIMIT),
        name="s5_state",
    )(u, mats["ws"], mats["spread_s"])
    h_re, h_im, f_re, f_im = pl.pallas_call(
        functools.partial(_s5_scan_kernel, bsz=bsz),
        grid=(nv,),
        in_specs=[pl.BlockSpec((S5_SLABS, n, LANES), lambda i: (i, 0, 0)),
                  pl.BlockSpec((S5_SLABS, n, LANES), lambda i: (i, 0, 0)),
                  pl.BlockSpec((1, half), lambda i: (0, i)),
                  pl.BlockSpec((1, half), lambda i: (0, i))],
        out_specs=[pl.BlockSpec((S5_SLABS, n, LANES), lambda i: (i, 0, 0)),
                   pl.BlockSpec((S5_SLABS, n, LANES), lambda i: (i, 0, 0)),
                   pl.BlockSpec((bsz, half), lambda i: (0, i)),
                   pl.BlockSpec((bsz, half), lambda i: (0, i))],
        out_shape=[jax.ShapeDtypeStruct((nv * S5_SLABS, n, LANES), F32)] * 2
        + [jax.ShapeDtypeStruct((bsz, g * p), F32)] * 2,
        compiler_params=_cparams(("parallel",)),
        name="s5_scan",
    )(s_re, s_im, at_re, at_im)
    y = pl.pallas_call(
        _s5_out_kernel,
        grid=(nv, bsz),
        in_specs=[pl.BlockSpec((seq, LANES), lambda v, b: (b, v)),
                  pl.BlockSpec((1, t, LANES, c), lambda v, b: (v, 0, 0, 0)),
                  pl.BlockSpec((c, LANES), lambda v, b: (0, 0)),
                  pl.BlockSpec((S5_SLABS, nchunk, LANES), lambda v, b: (v, b, 0)),
                  pl.BlockSpec((S5_SLABS, nchunk, LANES), lambda v, b: (v, b, 0)),
                  pl.BlockSpec((1, 2 * half, t * c), lambda v, b: (v, 0, 0)),
                  pl.BlockSpec((t * c, t * LANES), lambda v, b: (0, 0))],
        out_specs=pl.BlockSpec((seq, LANES), lambda v, b: (b, v)),
        out_shape=jax.ShapeDtypeStruct((bsz * seq, D_SSM), F32),
        scratch_shapes=[pltpu.VMEM((t * LANES, t * LANES), BF16), pltpu.VMEM((2 * half, t * LANES), BF16)],
        compiler_params=_cparams(("parallel", "arbitrary"), VMEM_LIMIT),
        name="s5_out",
    )(u, mats["kc"], mats["spread_b"], h_re, h_im, mats["wo"], mats["spread_o"])
    return y, f_re, f_im


S5S_GROUPS = LANES // SSM_GROUP


def _s5_sample_mats(sp, d_skip):
    go, gl, c, p = N_SSM_GROUPS // S5S_GROUPS, S5S_GROUPS, SSM_GROUP, SSM_STATE
    eye = jnp.eye(gl, dtype=F32)

    def bdiag_in(b):
        b4 = b.reshape(go, gl, p, c)
        return jnp.einsum("ogpc,gh->ogchp", b4, eye).reshape(go, gl * c, gl * p)

    def bdiag_out(cm):
        c4 = cm.reshape(go, gl, c, p)
        return jnp.einsum("ogcp,gh->ogphc", c4, eye).reshape(go, gl * p, gl * c)

    b8 = jnp.concatenate([bdiag_in(sp["bb_re"]), bdiag_in(sp["bb_im"])], axis=2)
    c8 = jnp.concatenate([bdiag_out(sp["c_re"]), -bdiag_out(sp["c_im"])], axis=1)
    a_re = sp["ab_re"].reshape(1, N_SSM_GROUPS * p)
    a_im = sp["ab_im"].reshape(1, N_SSM_GROUPS * p)
    return b8, c8, a_re, a_im, d_skip.astype(F32).reshape(1, D_SSM)


def _s5_sample_kernel(u_ref, hre_ref, him_ref, b8_ref, c8_ref, are_ref, aim_ref, d_ref,
                      y_ref, ore_ref, oim_ref):
    hp = lax.Precision.HIGHEST
    u = u_ref[...]
    half = S5S_GROUPS * SSM_STATE
    bu = jnp.dot(u, b8_ref[0], preferred_element_type=F32, precision=hp)
    are, aim = are_ref[...], aim_ref[...]
    h0r, h0i = hre_ref[...], him_ref[...]
    hr = are * h0r - aim * h0i + bu[:, :half]
    hi = are * h0i + aim * h0r + bu[:, half:]
    ore_ref[...] = hr
    oim_ref[...] = hi
    y = jnp.dot(jnp.concatenate([hr, hi], axis=1), c8_ref[0], preferred_element_type=F32, precision=hp)
    y_ref[...] = (y + d_ref[...] * u).astype(y_ref.dtype)


def _s5_sample(u, h0_re, h0_im, mats):
    b8, c8, a_re, a_im, d = mats
    n = u.shape[0]
    half = S5S_GROUPS * SSM_STATE
    return pl.pallas_call(
        _s5_sample_kernel,
        grid=(N_SSM_GROUPS // S5S_GROUPS,),
        in_specs=[pl.BlockSpec((n, LANES), lambda i: (0, i)),
                  pl.BlockSpec((n, half), lambda i: (0, i)),
                  pl.BlockSpec((n, half), lambda i: (0, i)),
                  pl.BlockSpec((1, LANES, 2 * half), lambda i: (i, 0, 0)),
                  pl.BlockSpec((1, 2 * half, LANES), lambda i: (i, 0, 0)),
                  pl.BlockSpec((1, half), lambda i: (0, i)),
                  pl.BlockSpec((1, half), lambda i: (0, i)),
                  pl.BlockSpec((1, LANES), lambda i: (0, i))],
        out_specs=[pl.BlockSpec((n, LANES), lambda i: (0, i)),
                   pl.BlockSpec((n, half), lambda i: (0, i)),
                   pl.BlockSpec((n, half), lambda i: (0, i))],
        out_shape=[jax.ShapeDtypeStruct((n, D_SSM), F32),
                   jax.ShapeDtypeStruct((n, N_SSM_GROUPS * SSM_STATE), F32),
                   jax.ShapeDtypeStruct((n, N_SSM_GROUPS * SSM_STATE), F32)],
        compiler_params=_cparams(("parallel",)),
        name="s5_sample",
    )(u, h0_re, h0_im, b8, c8, a_re, a_im, d)


def _layer_norm(x, g, b):
    mu = jnp.mean(x, axis=-1, keepdims=True)
    xc = x - mu
    var = jnp.mean(xc * xc, axis=-1, keepdims=True)
    return xc * lax.rsqrt(var + LN_EPS) * g + b


def _sigmoid(x):
    return 0.5 * jnp.tanh(0.5 * x) + 0.5


RUN_ROWS = SUBLANES
TAB_ROWS = 3


def _merge_kernel(x_ref, oa_ref, ys_ref, carry_in_ref, wao_ref, wso_ref, wg_ref, bg_ref, wo_ref,
                  g1_ref, b1_ref, wrt_ref, brt_ref,
                  x1_ref, lpos_ref, cols_ref, tab_ref, carry_out_ref, carry_sc, *, f32_matmuls):
    step = pl.program_id(0)

    @pl.when(step == 0)
    def _():
        carry_sc[...] = carry_in_ref[...]

    def mm(a, w_ref):
        if f32_matmuls:
            return _dot_split(a.astype(F32), w_ref[...])
        return jnp.dot(a.astype(BF16), w_ref[...], preferred_element_type=F32)

    tm = x_ref.shape[0]
    x = x_ref[...]
    branch_a = mm(oa_ref[...], wao_ref)
    z = mm(jax.nn.gelu(ys_ref[...].astype(F32)), wso_ref)
    branch_b = z[:, :D_MODEL] * _sigmoid(z[:, D_MODEL:])
    gates = _sigmoid(mm(x, wg_ref) + bg_ref[...])
    mixed = gates[:, :D_MODEL] * branch_a + gates[:, D_MODEL:] * branch_b
    mix = mm(mixed, wo_ref)
    x1 = _layer_norm(DEEPNORM_ALPHA * x + mix, g1_ref[...], b1_ref[...])
    x1_ref[...] = x1

    split2 = _split_bf16

    def dot_nt(a, b):
        return lax.dot_general(a, b, (((1,), (1,)), ((), ())), preferred_element_type=F32)

    w_hi, w_lo = split2(wrt_ref[...])
    rt = tm // tab_ref.shape[0]
    sub = lax.broadcasted_iota(jnp.int32, (N_EXPERTS, rt), 0)
    r = lax.broadcasted_iota(jnp.int32, (rt, rt), 0)
    c = lax.broadcasted_iota(jnp.int32, (rt, rt), 1)
    er = lax.broadcasted_iota(jnp.int32, (N_EXPERTS, N_EXPERTS), 0)
    ec = lax.broadcasted_iota(jnp.int32, (N_EXPERTS, N_EXPERTS), 1)
    rid = lax.broadcasted_iota(jnp.int32, (SUBLANES, LANES), 0)
    lane_pad = jnp.zeros((SUBLANES, LANES - N_EXPERTS), F32)
    for h in range(tab_ref.shape[0]):
        x_hi, x_lo = split2(x1[h * rt:(h + 1) * rt])
        logits = dot_nt(w_hi, x_hi) + dot_nt(w_hi, x_lo) + dot_nt(w_lo, x_hi) + brt_ref[...]
        work = logits
        vals, sels = [], []
        for _ in range(TOP_K):
            mx = jnp.max(work, axis=0, keepdims=True)
            idx = jnp.min(jnp.where(work == mx, sub, N_EXPERTS), axis=0, keepdims=True)
            sel = sub == idx
            vals.append(mx)
            sels.append(sel)
            work = jnp.where(sel, -jnp.inf, work)
        ex = [jnp.exp(v - vals[0]) for v in vals]
        tot = ex[0] + ex[1] + ex[2] + ex[3]
        gate_rows = jnp.concatenate([e / tot for e in ex], axis=0)

        multi = jnp.zeros((N_EXPERTS, rt), F32)
        for sel in sels:
            multi = multi + jnp.where(sel, 1.0, 0.0)
        multi_b = multi.astype(BF16)
        earlier = jnp.dot(multi_b, jnp.where(r < c, 1.0, 0.0).astype(BF16), preferred_element_type=F32)
        cnt_col = jnp.sum(multi, axis=1, keepdims=True)
        nb_col = jnp.floor((cnt_col + (RUN_ROWS - 1.0)) * (1.0 / RUN_ROWS))
        loff_col = jnp.dot(jnp.where(ec < er, 1.0, 0.0).astype(BF16),
                           jnp.broadcast_to(nb_col, (N_EXPERTS, rt)).astype(BF16), preferred_element_type=F32)
        base = RUN_ROWS * loff_col + earlier
        lpos = jnp.concatenate([jnp.sum(jnp.where(sel, base, 0.0), axis=0, keepdims=True) for sel in sels],
                               axis=0)
        lpos_ref[:, h * rt:(h + 1) * rt] = lpos.astype(jnp.int32)
        rows_hi, rows_lo = split2(jnp.concatenate([lpos, gate_rows], axis=0))
        eye = jnp.where(r == c, 1.0, 0.0).astype(BF16)
        cols_ref[h * rt:(h + 1) * rt, :] = dot_nt(eye, rows_hi) + dot_nt(eye, rows_lo)

        cnt_row = dot_nt(jnp.ones((SUBLANES, rt), BF16), multi_b)
        nb_row = jnp.floor((cnt_row + (RUN_ROWS - 1.0)) * (1.0 / RUN_ROWS))
        loff_row = jnp.dot(nb_row.astype(BF16), jnp.where(er < ec, 1.0, 0.0).astype(BF16),
                           preferred_element_type=F32)
        nb_p = jnp.concatenate([nb_row, lane_pad], axis=1)
        loff_p = jnp.concatenate([loff_row, lane_pad], axis=1)
        goff_p = carry_sc[...]
        tab = jnp.where(rid == 0, nb_p, jnp.where(rid == 1, loff_p, jnp.where(rid == 2, goff_p, 0.0)))
        tab_ref[h] = tab.astype(jnp.int32)
        carry_sc[...] = goff_p + nb_p
    carry_out_ref[...] = carry_sc[...]


def _merge(x, o_attn, y_ssm, carry_in, w, *, tile, route_tile, f32_matmuls):
    n = x.shape[0]
    nt = n // tile
    per_step = tile // route_tile
    full = lambda shape: pl.BlockSpec(shape, lambda i: (0,) * len(shape))
    return pl.pallas_call(
        functools.partial(_merge_kernel, f32_matmuls=f32_matmuls),
        grid=(nt,),
        in_specs=[pl.BlockSpec((tile, D_MODEL), lambda i: (i, 0)),
                  pl.BlockSpec((tile, D_ATTN), lambda i: (i, 0)),
                  pl.BlockSpec((tile, D_SSM), lambda i: (i, 0)),
                  full((SUBLANES, LANES)),
                  full((D_ATTN, D_MODEL)), full((D_SSM, 2 * D_MODEL)), full((D_MODEL, 2 * D_MODEL)),
                  full((1, 2 * D_MODEL)), full((D_MODEL, D_MODEL)),
                  full((1, D_MODEL)), full((1, D_MODEL)),
                  full((N_EXPERTS, D_MODEL)), full((N_EXPERTS, 1))],
        out_specs=[pl.BlockSpec((tile, D_MODEL), lambda i: (i, 0)),
                   pl.BlockSpec((TOP_K, tile), lambda i: (0, i)),
                   pl.BlockSpec((tile, 2 * TOP_K), lambda i: (i, 0)),
                   pl.BlockSpec((per_step, SUBLANES, LANES), lambda i: (i, 0, 0)),
                   full((SUBLANES, LANES))],
        out_shape=[jax.ShapeDtypeStruct((n, D_MODEL), F32),
                   jax.ShapeDtypeStruct((TOP_K, n), jnp.int32),
                   jax.ShapeDtypeStruct((n, 2 * TOP_K), F32),
                   jax.ShapeDtypeStruct((nt * per_step, SUBLANES, LANES), jnp.int32),
                   jax.ShapeDtypeStruct((SUBLANES, LANES), F32)],
        scratch_shapes=[pltpu.VMEM((SUBLANES, LANES), F32)],
        compiler_params=_cparams(("arbitrary",), VMEM_LIMIT_LARGE),
        name="merge",
    )(x, o_attn, y_ssm, carry_in, w["wao"], w["wso"], w["wg"], w["bg"], w["wo"], w["g1"], w["b1"],
      w["wrt"], w["brt"])


def _tab(tab_ref, tile, row, e):
    return tab_ref[(tile * TAB_ROWS + row) * N_EXPERTS + e]


BIG_PIECE = 4 * RUN_ROWS
MAX_UNITS_LOG2 = 8


def _for_each_run_piece(tab_ref, tile, fn):
    def per_expert(e, carry):
        loff = RUN_ROWS * _tab(tab_ref, tile, 1, e)
        goff = RUN_ROWS * _tab(tab_ref, tile, 2, e)
        units = _tab(tab_ref, tile, 0, e)
        n_big = lax.shift_right_logical(units, 2)

        def big(j, c2):
            fn(pl.multiple_of(loff + j * BIG_PIECE, RUN_ROWS), goff + j * BIG_PIECE, e, BIG_PIECE)
            return c2

        lax.fori_loop(0, n_big, big, 0)
        done = n_big * BIG_PIECE

        def small(j, c2):
            fn(pl.multiple_of(loff + done + j * RUN_ROWS, RUN_ROWS), goff + done + j * RUN_ROWS, e, RUN_ROWS)
            return c2

        lax.fori_loop(0, units & 3, small, 0)
        return carry

    lax.fori_loop(0, N_EXPERTS, per_expert, 0)


def _drain_units(units, wait_copy, buffer_rows):
    assert buffer_rows < (RUN_ROWS << MAX_UNITS_LOG2)
    for b in range(MAX_UNITS_LOG2):
        if (RUN_ROWS << b) > buffer_rows:
            break

        @pl.when((lax.shift_right_logical(units, b) & 1) == 1)
        def _():
            wait_copy(RUN_ROWS << b).wait()


def _dispatch_kernel(tab_ref, seg_ref, tot_ref, tail_ref, lpos_p_ref, xp_ref, lpos_s_ref, xs_in_ref, xs_ref,
                     loc_sc, zero_sc, sem, zsem):
    i = pl.program_id(0)
    last = pl.num_programs(0) - 1
    tile = i
    slot = i % 2
    loc = loc_sc.shape[1]

    @pl.when(i == 0)
    def _():
        zero_sc[...] = jnp.zeros_like(zero_sc)

        def tail_copy(e, j):
            row = pl.multiple_of(RUN_ROWS * (tail_ref[e] + j), RUN_ROWS)
            return pltpu.make_async_copy(zero_sc.at[pl.ds(0, RUN_ROWS)], xs_ref.at[pl.ds(row, RUN_ROWS)], zsem)

        def per_expert(e, carry):
            n = tail_ref[N_EXPERTS + e]
            lax.fori_loop(0, n, lambda j, c2: (tail_copy(e, j).start(), c2)[1], 0)
            lax.fori_loop(0, n, lambda j, c2: (tail_copy(e, j).wait(), c2)[1], 0)
            return carry

        lax.fori_loop(0, N_EXPERTS, per_expert, 0)

        def block_copy(b):
            row = pl.multiple_of(b * MOE_ROWS, MOE_ROWS)
            return pltpu.make_async_copy(zero_sc, xs_ref.at[pl.ds(row, MOE_ROWS)], zsem)

        first_unused, n_blocks = tail_ref[2 * N_EXPERTS], xs_ref.shape[0] // MOE_ROWS
        lax.fori_loop(first_unused, n_blocks, lambda b, c2: (block_copy(b).start(), c2)[1], 0)
        lax.fori_loop(first_unused, n_blocks, lambda b, c2: (block_copy(b).wait(), c2)[1], 0)

    def sort_tile(lpos_ref, x_ref):
        tm = x_ref.shape[0]
        rows = lax.broadcasted_iota(jnp.int32, (loc, tm), 0)
        lp = lpos_ref[...]
        onehot = jnp.zeros((loc, tm), F32)
        for k in range(TOP_K):
            onehot = jnp.where(rows == lp[k:k + 1], 1.0, onehot)
        loc_sc[slot] = jnp.dot(onehot.astype(BF16), x_ref[...].astype(BF16), preferred_element_type=F32)

    @pl.when(i < last)
    def _():
        sort_tile(lpos_p_ref, xp_ref)

    @pl.when(i == last)
    def _():
        sort_tile(lpos_s_ref, xs_in_ref)

    def piece_copy(sl, lrow, grow, e, n):
        dst = pl.multiple_of(seg_ref[e] + grow, RUN_ROWS)
        return pltpu.make_async_copy(loc_sc.at[sl, pl.ds(lrow, n)], xs_ref.at[pl.ds(dst, n)], sem.at[sl])

    _for_each_run_piece(tab_ref, tile, lambda lrow, grow, e, n: piece_copy(slot, lrow, grow, e, n).start())

    def drain(tl, sl):
        _drain_units(tot_ref[tl], lambda n: piece_copy(sl, 0, 0, 0, n), loc)

    @pl.when(i > 0)
    def _():
        drain(tile - 1, 1 - slot)

    @pl.when(i == last)
    def _():
        drain(tile, slot)


def _dispatch(tab, seg_start, tot, tails, lpos_p, x1_p, lpos_s, x1_s, *, tile, nrows):
    nt_p = x1_p.shape[0] // tile
    ns = x1_s.shape[0]
    loc = tile * TOP_K + N_EXPERTS * RUN_ROWS
    prompt_blk = lambda i, *_: jnp.minimum(i, nt_p - 1)
    return pl.pallas_call(
        _dispatch_kernel,
        grid_spec=pltpu.PrefetchScalarGridSpec(
            num_scalar_prefetch=4,
            grid=(nt_p + 1,),
            in_specs=[pl.BlockSpec((TOP_K, tile), lambda i, *_: (0, prompt_blk(i))),
                      pl.BlockSpec((tile, D_MODEL), lambda i, *_: (prompt_blk(i), 0)),
                      pl.BlockSpec((TOP_K, ns), lambda i, *_: (0, 0)),
                      pl.BlockSpec((ns, D_MODEL), lambda i, *_: (0, 0))],
            out_specs=pl.BlockSpec(memory_space=pl.ANY),
            scratch_shapes=[pltpu.VMEM((2, loc, D_MODEL), F32), pltpu.VMEM((MOE_ROWS, D_MODEL), F32),
                            pltpu.SemaphoreType.DMA((2,)), pltpu.SemaphoreType.DMA(())]),
        out_shape=jax.ShapeDtypeStruct((nrows, D_MODEL), F32),
        compiler_params=_cparams(("arbitrary",), VMEM_LIMIT),
        name="dispatch",
    )(tab, seg_start, tot, tails, lpos_p, x1_p, lpos_s, x1_s)


def _deinterleave_matrix():
    pm = np.zeros((MXU_DIM, MXU_DIM), np.float32)
    half = MXU_DIM // 2
    for c in range(half):
        pm[2 * c, c] = 1.0
        pm[2 * c + 1, half + c] = 1.0
    return pm


def _expert_kernel(be_ref, nu_ref, nv_ref, ord_ref, nxt_ref, xs_ref, w1_hbm, b1_ref, w2_hbm, b2_ref, pm_ref, y_ref,
                   w1f_sc, w2f_sc, w1p_sc, w2b_sc, sem):
    del nu_ref
    i = pl.program_id(0)
    e = be_ref[i]
    prev = be_ref[jnp.maximum(i - 1, 0)]
    nblk = 2 * D_FF // MXU_DIM

    def weight_copies(expert, slot):
        return (pltpu.make_async_copy(w1_hbm.at[expert], w1f_sc.at[slot], sem.at[0, slot]),
                pltpu.make_async_copy(w2_hbm.at[expert], w2f_sc.at[slot], sem.at[1, slot]))

    @pl.when(i == 0)
    def _():
        for cp in weight_copies(e, 0):
            cp.start()

    @pl.when((i == 0) | (e != prev))
    def _():
        slot = ord_ref[i] % 2
        for cp in weight_copies(e, slot):
            cp.wait()

        for cb in range(nblk):
            blk = w1f_sc[slot, :, cb * MXU_DIM:(cb + 1) * MXU_DIM].astype(BF16)
            w1p_sc[:, cb * MXU_DIM:(cb + 1) * MXU_DIM] = jnp.dot(
                blk, pm_ref[...], preferred_element_type=F32).astype(BF16)
        w2b_sc[...] = w2f_sc[slot].astype(BF16)

        nxt = nxt_ref[i]

        @pl.when(nxt >= 0)
        def _():
            for cp in weight_copies(nxt, 1 - slot):
                cp.start()

    for blk in range(MOE_STEP_BLOCKS):
        rows = slice(blk * MOE_ROWS, (blk + 1) * MOE_ROWS)

        @pl.when(blk < nv_ref[i])
        def _():
            x = xs_ref[rows, :].astype(BF16)
            h = jnp.dot(x, w1p_sc[...], preferred_element_type=F32) + b1_ref[0]
            half = MXU_DIM // 2
            acts = []
            for cb in range(nblk):
                x_glu = jnp.minimum(h[:, cb * MXU_DIM:cb * MXU_DIM + half], SWIGLU_LIMIT)
                x_lin = jnp.clip(h[:, cb * MXU_DIM + half:(cb + 1) * MXU_DIM], -SWIGLU_LIMIT, SWIGLU_LIMIT)
                acts.append((x_glu * jax.nn.sigmoid(SWIGLU_ALPHA * x_glu) * (x_lin + 1.0)).astype(BF16))
            act = jnp.concatenate(acts, axis=1)
            y_ref[rows, :] = jnp.dot(act, w2b_sc[...], preferred_element_type=F32) + b2_ref[0]

        @pl.when(blk >= nv_ref[i])
        def _():
            y_ref[rows, :] = jnp.zeros((MOE_ROWS, D_MODEL), F32)


def _experts(block_e, n_used, n_valid, run_ord, run_next, xs, w1, b1p, w2, b2, pm):
    nrows = xs.shape[0]
    step_rows = MOE_STEP_BLOCKS * MOE_ROWS
    nb = nrows // step_rows
    return pl.pallas_call(
        _expert_kernel,
        grid_spec=pltpu.PrefetchScalarGridSpec(
            num_scalar_prefetch=5,
            grid=(nb,),
            in_specs=[pl.BlockSpec((step_rows, D_MODEL), lambda i, be, nu, *_: (jnp.minimum(i, nu[0] - 1), 0)),
                      pl.BlockSpec(memory_space=pl.ANY),
                      pl.BlockSpec((1, 1, 2 * D_FF), lambda i, be, *_: (be[i], 0, 0)),
                      pl.BlockSpec(memory_space=pl.ANY),
                      pl.BlockSpec((1, 1, D_MODEL), lambda i, be, *_: (be[i], 0, 0)),
                      pl.BlockSpec((MXU_DIM, MXU_DIM), lambda i, *_: (0, 0))],
            out_specs=pl.BlockSpec((step_rows, D_MODEL), lambda i, *_: (i, 0)),
            scratch_shapes=[pltpu.VMEM((2, D_MODEL, 2 * D_FF), F32), pltpu.VMEM((2, D_FF, D_MODEL), F32),
                            pltpu.VMEM((D_MODEL, 2 * D_FF), BF16), pltpu.VMEM((D_FF, D_MODEL), BF16),
                            pltpu.SemaphoreType.DMA((2, 2))]),
        out_shape=jax.ShapeDtypeStruct((nrows, D_MODEL), F32),
        compiler_params=_cparams(("arbitrary",), VMEM_LIMIT_LARGE),
        name="experts",
    )(block_e, n_used, n_valid, run_ord, run_next, xs, w1, b1p, w2, b2, pm)


def _combine_kernel(tab_ref, seg_ref, tot_ref, cols_ref, x1_ref, g2_ref, b2_ref, ys_ref, y_ref, loc_sc, sem,
                    *, tile_base):
    i = pl.program_id(0)
    last = pl.num_programs(0) - 1
    tile = i + tile_base
    slot = i % 2
    loc, tm = loc_sc.shape[1], x1_ref.shape[0]

    def piece_copy(sl, lrow, grow, e, n):
        src = pl.multiple_of(seg_ref[e] + grow, RUN_ROWS)
        return pltpu.make_async_copy(ys_ref.at[pl.ds(src, n)], loc_sc.at[sl, pl.ds(lrow, n)], sem.at[sl])

    def gather(tl, sl):
        _for_each_run_piece(tab_ref, tl, lambda lrow, grow, e, n: piece_copy(sl, lrow, grow, e, n).start())

    @pl.when(i == 0)
    def _():
        loc_sc[...] = jnp.zeros_like(loc_sc)
        gather(tile, slot)

    @pl.when(i < last)
    def _():
        gather(tile + 1, 1 - slot)

    _drain_units(tot_ref[tile], lambda n: piece_copy(slot, 0, 0, 0, n), loc)

    cols = cols_ref[...]
    lane = lax.broadcasted_iota(jnp.int32, (tm, loc), 1)
    weights = jnp.zeros((tm, loc), F32)
    for k in range(TOP_K):
        weights = jnp.where(lane == cols[:, k:k + 1].astype(jnp.int32), cols[:, TOP_K + k:TOP_K + k + 1], weights)
    ffn = jnp.dot(weights.astype(BF16), loc_sc[slot].astype(BF16), preferred_element_type=F32)
    y_ref[...] = _layer_norm(DEEPNORM_ALPHA * x1_ref[...] + ffn, g2_ref[...], b2_ref[...])


def _combine(tab, seg_start, tot, cols, x1, g2, b2, ys, *, tile, tile_base):
    n = x1.shape[0]
    loc = tile * TOP_K + N_EXPERTS * RUN_ROWS
    return pl.pallas_call(
        functools.partial(_combine_kernel, tile_base=tile_base),
        grid_spec=pltpu.PrefetchScalarGridSpec(
            num_scalar_prefetch=3,
            grid=(n // tile,),
            in_specs=[pl.BlockSpec((tile, 2 * TOP_K), lambda i, *_: (i, 0)),
                      pl.BlockSpec((tile, D_MODEL), lambda i, *_: (i, 0)),
                      pl.BlockSpec((1, D_MODEL), lambda i, *_: (0, 0)),
                      pl.BlockSpec((1, D_MODEL), lambda i, *_: (0, 0)),
                      pl.BlockSpec(memory_space=pl.ANY)],
            out_specs=pl.BlockSpec((tile, D_MODEL), lambda i, *_: (i, 0)),
            scratch_shapes=[pltpu.VMEM((2, loc, D_MODEL), F32), pltpu.SemaphoreType.DMA((2,))]),
        out_shape=jax.ShapeDtypeStruct((n, D_MODEL), F32),
        compiler_params=_cparams(("arbitrary",), VMEM_LIMIT),
        name="combine",
    )(tab, seg_start, tot, cols, x1, g2, b2, ys)


def kernel(x_prompt, x_sample, cache_k_win, cache_v_win, state_ssm_re, state_ssm_im, w_in, b_in, attn_sinks,
           w_attn_out, ssm_a_re, ssm_a_im, ssm_log_dt, ssm_b_re, ssm_b_im, ssm_c_re, ssm_c_im, ssm_d, w_ssm_out,
           w_gate, b_gate, w_out, ln1_g, ln1_b, w_router, b_router, w_exp1, b_exp1, w_exp2, b_exp2, ln2_g, ln2_b):
    assert w_in.shape[0] == DEPTH == 1
    bsz, seq, _ = x_prompt.shape
    nsamp = x_sample.shape[0]
    assert x_sample.shape[1] == 1
    n_p = bsz * seq
    n_tok = n_p + nsamp

    xp = x_prompt.reshape(n_p, D_MODEL)
    xsm = x_sample.reshape(nsamp, D_MODEL)
    b_in2 = b_in[0].reshape(1, D_IN)
    sinks = attn_sinks[0].astype(F32)

    q_p, k_p, v_p, u_p = _proj(xp, w_in[0].astype(BF16), b_in2, tile=512, exact_f32=False, q_dtype=BF16)
    q_s, k_s, v_s, u_s = _proj(xsm, w_in[0], b_in2, tile=nsamp, exact_f32=True, q_dtype=F32)

    o_p = _attn_prompt(sinks, q_p.reshape(bsz, seq, D_ATTN), k_p.reshape(bsz, seq, D_KV),
                       v_p.reshape(bsz, seq, D_KV)).reshape(n_p, D_ATTN)
    k_buf = cache_k_win[0].reshape(nsamp, WINDOW, D_KV)
    v_buf = cache_v_win[0].reshape(nsamp, WINDOW, D_KV)
    o_s = _attn_sample(sinks, q_s, k_s, v_s, k_buf, v_buf)

    sp = _s5_params(ssm_a_re[0], ssm_a_im[0], ssm_log_dt[0], ssm_b_re[0], ssm_b_im[0], ssm_c_re[0], ssm_c_im[0])
    y_p, hp_re, hp_im = _s5_prompt(u_p, bsz, seq, _s5_chunk_mats(sp, ssm_d[0]))
    y_s, hs_re, hs_im = _s5_sample(u_s, state_ssm_re[0].reshape(nsamp, -1), state_ssm_im[0].reshape(nsamp, -1),
                                   _s5_sample_mats(sp, ssm_d[0]))

    wm = dict(wao=w_attn_out[0].astype(BF16), wso=w_ssm_out[0].astype(BF16), wg=w_gate[0].astype(BF16),
              bg=b_gate[0].reshape(1, -1), wo=w_out[0].astype(BF16), g1=ln1_g[0].reshape(1, -1),
              b1=ln1_b[0].reshape(1, -1), wrt=w_router[0].T, brt=b_router[0].reshape(-1, 1))
    wm_f32 = dict(wm, wao=w_attn_out[0], wso=w_ssm_out[0], wg=w_gate[0], wo=w_out[0])
    carry0 = jnp.zeros((SUBLANES, LANES), F32)
    x1_p, lpos_p, cols_p, tab_p, carry1 = _merge(xp, o_p, y_p, carry0, wm, tile=MERGE_TILE, route_tile=TOK_TILE,
                                                 f32_matmuls=False)
    x1_s, lpos_s, cols_s, tab_s, carry2 = _merge(xsm, o_s, y_s, carry1, wm_f32, tile=nsamp, route_tile=nsamp,
                                                 f32_matmuls=True)

    nt_p = n_p // TOK_TILE
    tab = jnp.concatenate([tab_p[:, :TAB_ROWS, :N_EXPERTS], tab_s[:, :TAB_ROWS, :N_EXPERTS]], axis=0)
    tot = jnp.sum(tab[:, 0, :], axis=1).astype(jnp.int32)
    tab = tab.reshape(-1)
    seg_rows = carry2[0, :N_EXPERTS].astype(jnp.int32) * RUN_ROWS
    step_rows = MOE_STEP_BLOCKS * MOE_ROWS
    padded = ((seg_rows + step_rows - 1) // step_rows) * step_rows
    pad_end = jnp.cumsum(padded)
    pad_start = (pad_end - padded).astype(jnp.int32)
    seg_end = pad_start + seg_rows
    n_runs = (nt_p + 1) * N_EXPERTS
    nb_max = (n_tok * TOP_K + n_runs * (RUN_ROWS - 1) + N_EXPERTS * (step_rows - 1) + step_rows - 1) // step_rows
    n_used = (pad_end[-1] // step_rows).astype(jnp.int32)
    tails = jnp.concatenate([seg_end // RUN_ROWS, (padded - seg_rows) // RUN_ROWS,
                             (pad_end[-1:] // MOE_ROWS)]).astype(jnp.int32)
    blk_start = jnp.arange(nb_max, dtype=jnp.int32) * step_rows
    blk_e = jnp.minimum(jnp.sum(blk_start[:, None] >= pad_end[None, :], axis=1), N_EXPERTS - 1).astype(jnp.int32)
    used = jnp.arange(nb_max) < n_used
    blk_e = jnp.where(used, blk_e, jnp.max(jnp.where(used, blk_e, 0)))
    ids = jnp.arange(N_EXPERTS, dtype=jnp.int32)
    of_blk = blk_e[:, None] == ids[None, :]
    n_valid = jnp.clip((jnp.sum(jnp.where(of_blk, seg_end[None, :], 0), axis=1) - blk_start + MOE_ROWS - 1)
                       // MOE_ROWS, 0, MOE_STEP_BLOCKS)
    n_valid = jnp.where(used, n_valid, 0).astype(jnp.int32)
    new_run = jnp.concatenate([jnp.ones((1,), jnp.int32), (blk_e[1:] != blk_e[:-1]).astype(jnp.int32)])
    run_ord = (jnp.cumsum(new_run) - 1).astype(jnp.int32)
    later = (ids[None, :] > ids[:, None]) & (padded > 0)[None, :]
    next_e = jnp.min(jnp.where(later, ids[None, :], N_EXPERTS), axis=1)
    next_e = jnp.where(next_e < N_EXPERTS, next_e, -1).astype(jnp.int32)
    run_next = jnp.sum(jnp.where(of_blk, next_e[None, :], 0), axis=1).astype(jnp.int32)

    nrows = nb_max * step_rows
    xs = _dispatch(tab, pad_start, tot, tails, lpos_p, x1_p, lpos_s, x1_s, tile=TOK_TILE, nrows=nrows)

    b1p = b_exp1[0].reshape(N_EXPERTS, 2 * D_FF // MXU_DIM, MXU_DIM // 2, 2)
    b1p = jnp.swapaxes(b1p, 2, 3).reshape(N_EXPERTS, 1, 2 * D_FF)
    ys = _experts(blk_e, n_used.reshape(1), n_valid, run_ord, run_next, xs, w_exp1[0], b1p, w_exp2[0],
                  b_exp2[0].reshape(N_EXPERTS, 1, D_MODEL),
                  jnp.asarray(_deinterleave_matrix(), BF16))

    g2, b2 = ln2_g[0].reshape(1, -1), ln2_b[0].reshape(1, -1)
    y_prompt = _combine(tab, pad_start, tot, cols_p, x1_p, g2, b2, ys, tile=TOK_TILE, tile_base=0)
    y_sample = _combine(tab, pad_start, tot, cols_s, x1_s, g2, b2, ys, tile=nsamp, tile_base=nt_p)

    k_p4 = k_p.reshape(bsz, seq, D_KV)[:, -WINDOW:].reshape(bsz, WINDOW, N_KV_HEADS, HEAD_DIM)
    v_p4 = v_p.reshape(bsz, seq, D_KV)[:, -WINDOW:].reshape(bsz, WINDOW, N_KV_HEADS, HEAD_DIM)
    k_s4 = jnp.concatenate([cache_k_win[0][:, 1:], k_s.reshape(nsamp, 1, N_KV_HEADS, HEAD_DIM)], axis=1)
    v_s4 = jnp.concatenate([cache_v_win[0][:, 1:], v_s.reshape(nsamp, 1, N_KV_HEADS, HEAD_DIM)], axis=1)
    st = lambda a, n: a.reshape(1, n, N_SSM_GROUPS, SSM_STATE)
    return (y_prompt.reshape(bsz, seq, D_MODEL), y_sample.reshape(nsamp, 1, D_MODEL),
            k_p4[None], v_p4[None], st(hp_re, bsz), st(hp_im, bsz),
            k_s4[None], v_s4[None], st(hs_re, nsamp), st(hs_im, nsamp))
```

```python
import functools

import numpy as np
import jax
import jax.numpy as jnp
from jax import lax
from jax.experimental import pallas as pl
from jax.experimental.pallas import tpu as pltpu

F32 = jnp.float32
BF16 = jnp.bfloat16

D_MODEL = 1024
HEAD_DIM = 64
N_Q_HEADS = 8
N_KV_HEADS = 2
Q_PER_KV = N_Q_HEADS // N_KV_HEADS
D_ATTN = N_Q_HEADS * HEAD_DIM
D_KV = N_KV_HEADS * HEAD_DIM
WINDOW = 128
ATTN_SCALE = HEAD_DIM ** -0.5
SSM_GROUP = 16
D_SSM = D_MODEL // 2
N_SSM_GROUPS = D_SSM // SSM_GROUP
SSM_STATE = 64
D_IN = D_ATTN + 2 * D_KV + D_SSM
N_EXPERTS = 32
TOP_K = 4
D_FF = D_MODEL
SWIGLU_LIMIT = 7.0
SWIGLU_ALPHA = 1.702
LN_EPS = 1e-5
DEPTH = 1
DEEPNORM_ALPHA = (2 * DEPTH) ** 0.25

LANES = 128
SUBLANES = 8
MXU_DIM = 256

S5_CHUNK = MXU_DIM // SSM_GROUP
S5_LANE_GROUPS = LANES // SSM_GROUP
MOE_ROWS = 256
MOE_STEP_BLOCKS = 2
TOK_TILE = 256
MERGE_TILE = 512
VMEM_LIMIT = 48 * 1024 * 1024
VMEM_LIMIT_LARGE = 56 * 1024 * 1024


def _cparams(sem, vmem=None):
    return pltpu.CompilerParams(dimension_semantics=sem, vmem_limit_bytes=vmem)


def _split_bf16(v):
    hi = v.astype(BF16)
    return hi, (v - hi.astype(F32)).astype(BF16)


def _dot_split(a, b, dims=(((1,), (0,)), ((), ()))):
    a_hi, a_lo = _split_bf16(a)
    b_hi, b_lo = _split_bf16(b)
    dot = lambda p, q: lax.dot_general(p, q, dims, preferred_element_type=F32)
    return dot(a_hi, b_hi) + dot(a_hi, b_lo) + dot(a_lo, b_hi)


def _proj_kernel(x_ref, w_ref, b_ref, q_ref, k_ref, v_ref, u_ref, *, exact_f32):
    if exact_f32:
        h = jnp.dot(x_ref[...], w_ref[...], preferred_element_type=F32, precision=lax.Precision.HIGHEST)
    else:
        h = jnp.dot(x_ref[...].astype(BF16), w_ref[...], preferred_element_type=F32)
    h = h + b_ref[...]
    q_ref[...] = (h[:, :D_ATTN] * ATTN_SCALE).astype(q_ref.dtype)
    k_ref[...] = h[:, D_ATTN:D_ATTN + D_KV]
    v_ref[...] = h[:, D_ATTN + D_KV:D_ATTN + 2 * D_KV]
    u_ref[...] = h[:, D_ATTN + 2 * D_KV:].astype(u_ref.dtype)


def _proj(x, w, b, *, tile, exact_f32, q_dtype):
    n = x.shape[0]
    return pl.pallas_call(
        functools.partial(_proj_kernel, exact_f32=exact_f32),
        grid=(n // tile,),
        in_specs=[pl.BlockSpec((tile, D_MODEL), lambda i: (i, 0)),
                  pl.BlockSpec((D_MODEL, D_IN), lambda i: (0, 0)),
                  pl.BlockSpec((1, D_IN), lambda i: (0, 0))],
        out_specs=[pl.BlockSpec((tile, D_ATTN), lambda i: (i, 0)),
                   pl.BlockSpec((tile, D_KV), lambda i: (i, 0)),
                   pl.BlockSpec((tile, D_KV), lambda i: (i, 0)),
                   pl.BlockSpec((tile, D_SSM), lambda i: (i, 0))],
        out_shape=[jax.ShapeDtypeStruct((n, D_ATTN), q_dtype),
                   jax.ShapeDtypeStruct((n, D_KV), F32),
                   jax.ShapeDtypeStruct((n, D_KV), F32),
                   jax.ShapeDtypeStruct((n, D_SSM), F32)],
        compiler_params=_cparams(("parallel",)),
        name="proj",
    )(x, w, b)


ATT_Q_TILE = 512


def _attn_prompt_kernel(sink_ref, q_ref, k_ref, v_ref, o_ref):
    i = pl.program_id(1)
    nk, nq = 2 * WINDOW, 2 * WINDOW
    lo = lax.broadcasted_iota(jnp.int32, (nk, LANES), 1) < HEAD_DIM
    top = lax.broadcasted_iota(jnp.int32, (nq, 1), 0) < WINDOW
    for blk in range(ATT_Q_TILE // WINDOW):
        q0 = i * ATT_Q_TILE + blk * WINDOW
        k0 = pl.multiple_of(jnp.maximum(q0 - WINDOW, 0), WINDOW)
        kk = k_ref[0, pl.ds(k0, nk), :]
        vv = v_ref[0, pl.ds(k0, nk), :]
        kk_sw = pltpu.roll(kk, HEAD_DIM, axis=1)
        vv_sw = pltpu.roll(vv, HEAD_DIM, axis=1)
        k_var = [[jnp.where(lo, kk, 0.0).astype(BF16), jnp.where(lo, 0.0, kk_sw).astype(BF16)],
                 [jnp.where(lo, kk_sw, 0.0).astype(BF16), jnp.where(lo, 0.0, kk).astype(BF16)]]
        v_var = [[jnp.where(lo, vv, 1.0).astype(BF16), jnp.where(lo, 1.0, vv_sw).astype(BF16)],
                 [jnp.where(lo, vv_sw, 1.0).astype(BF16), jnp.where(lo, 1.0, vv).astype(BF16)]]
        qpos = q0 + lax.broadcasted_iota(jnp.int32, (nq, nk), 0) % WINDOW
        kpos = k0 + lax.broadcasted_iota(jnp.int32, (nq, nk), 1)
        valid = (kpos <= qpos) & (qpos - kpos <= WINDOW)
        rows = slice(blk * WINDOW, (blk + 1) * WINDOW)
        for kv in range(N_KV_HEADS):
            pairs = (2 * kv, 2 * kv + 1)
            qs = jnp.concatenate([q_ref[0, rows, pr * LANES:(pr + 1) * LANES] for pr in pairs], axis=0)
            outs = []
            for parity in range(2):
                sink = jnp.where(top, sink_ref[2 * pairs[0] + parity], sink_ref[2 * pairs[1] + parity])
                s = lax.dot_general(qs, k_var[kv][parity], (((1,), (1,)), ((), ())), preferred_element_type=F32)
                s = jnp.where(valid, s, -jnp.inf)
                m = jnp.maximum(jnp.max(s, axis=-1, keepdims=True), sink)
                p = jnp.exp(s - m).astype(BF16)
                acc = jnp.dot(p, v_var[kv][parity], preferred_element_type=F32)
                outs.append(acc / (pltpu.roll(acc, HEAD_DIM, axis=1) + jnp.exp(sink - m)))
            o = jnp.where(lo, outs[0], outs[1]).astype(o_ref.dtype)
            for j, pr in enumerate(pairs):
                o_ref[0, rows, pr * LANES:(pr + 1) * LANES] = o[j * WINDOW:(j + 1) * WINDOW]


def _attn_prompt(sinks, q, k, v):
    bsz, seq = q.shape[0], q.shape[1]
    return pl.pallas_call(
        _attn_prompt_kernel,
        grid=(bsz, seq // ATT_Q_TILE),
        in_specs=[pl.BlockSpec(memory_space=pltpu.SMEM),
                  pl.BlockSpec((1, ATT_Q_TILE, D_ATTN), lambda b, i: (b, i, 0)),
                  pl.BlockSpec((1, seq, D_KV), lambda b, i: (b, 0, 0)),
                  pl.BlockSpec((1, seq, D_KV), lambda b, i: (b, 0, 0))],
        out_specs=pl.BlockSpec((1, ATT_Q_TILE, D_ATTN), lambda b, i: (b, i, 0)),
        out_shape=jax.ShapeDtypeStruct((bsz, seq, D_ATTN), BF16),
        compiler_params=_cparams(("parallel", "parallel")),
        name="attn_prompt",
    )(sinks, q, k, v)


ATT_S_GROUP = 16


def _attn_sample_kernel(sink_ref, q_ref, kn_ref, vn_ref, kb_ref, vb_ref, o_ref, knext_ref, vnext_ref):
    g = ATT_S_GROUP
    rows = Q_PER_KV * g
    ncol = g * WINDOW
    for buf_ref, new_ref, next_ref in ((kb_ref, kn_ref, knext_ref), (vb_ref, vn_ref, vnext_ref)):
        next_ref[:, 0:WINDOW - 1, :] = buf_ref[:, 1:WINDOW, :]
        next_ref[:, WINDOW - 1, :] = new_ref[...]
    kb = kb_ref[...].reshape(ncol, D_KV)
    vb = vb_ref[...].reshape(ncol, D_KV)
    rseq = lax.broadcasted_iota(jnp.int32, (rows, ncol), 0) % g
    cseq = lax.broadcasted_iota(jnp.int32, (rows, ncol), 1) // WINDOW
    own = rseq == cseq
    rhead = lax.broadcasted_iota(jnp.int32, (rows, 1), 0) // g
    for kv in range(N_KV_HEADS):
        lo = kv * HEAD_DIM
        qs = jnp.concatenate(
            [q_ref[:, (kv * Q_PER_KV + h) * HEAD_DIM:(kv * Q_PER_KV + h + 1) * HEAD_DIM] for h in range(Q_PER_KV)],
            axis=0)
        kn = jnp.concatenate([kn_ref[:, lo:lo + HEAD_DIM]] * Q_PER_KV, axis=0)
        vn = jnp.concatenate([vn_ref[:, lo:lo + HEAD_DIM]] * Q_PER_KV, axis=0)
        sink = jnp.zeros((rows, 1), F32)
        for h in range(Q_PER_KV):
            sink = jnp.where(rhead == h, sink_ref[kv * Q_PER_KV + h], sink)
        qs = qs.astype(F32)
        s = _dot_split(qs, kb[:, lo:lo + HEAD_DIM], (((1,), (1,)), ((), ())))
        s = jnp.where(own, s, -jnp.inf)
        s_new = jnp.sum(qs * kn, axis=-1, keepdims=True)
        m = jnp.maximum(jnp.maximum(jnp.max(s, axis=-1, keepdims=True), s_new), sink)
        p = jnp.exp(s - m)
        p_new = jnp.exp(s_new - m)
        denom = jnp.sum(p, axis=-1, keepdims=True) + p_new + jnp.exp(sink - m)
        o = (_dot_split(p, vb[:, lo:lo + HEAD_DIM]) + p_new * vn) / denom
        for h in range(Q_PER_KV):
            c0 = (kv * Q_PER_KV + h) * HEAD_DIM
            o_ref[:, c0:c0 + HEAD_DIM] = o[h * g:(h + 1) * g].astype(o_ref.dtype)


def _attn_sample(sinks, q, k_new, v_new, k_buf, v_buf):
    n = q.shape[0]
    g = ATT_S_GROUP
    return pl.pallas_call(
        _attn_sample_kernel,
        grid=(n // g,),
        in_specs=[pl.BlockSpec(memory_space=pltpu.SMEM),
                  pl.BlockSpec((g, D_ATTN), lambda i: (i, 0)),
                  pl.BlockSpec((g, D_KV), lambda i: (i, 0)),
                  pl.BlockSpec((g, D_KV), lambda i: (i, 0)),
                  pl.BlockSpec((g, WINDOW, D_KV), lambda i: (i, 0, 0)),
                  pl.BlockSpec((g, WINDOW, D_KV), lambda i: (i, 0, 0))],
        out_specs=[pl.BlockSpec((g, D_ATTN), lambda i: (i, 0)),
                   pl.BlockSpec((g, WINDOW, D_KV), lambda i: (i, 0, 0)),
                   pl.BlockSpec((g, WINDOW, D_KV), lambda i: (i, 0, 0))],
        out_shape=[jax.ShapeDtypeStruct((n, D_ATTN), F32),
                   jax.ShapeDtypeStruct((n, WINDOW, D_KV), F32),
                   jax.ShapeDtypeStruct((n, WINDOW, D_KV), F32)],
        compiler_params=_cparams(("parallel",)),
        name="attn_sample",
    )(sinks, q, k_new, v_new, k_buf, v_buf)


def _s5_params(a_re, a_im, log_dt, b_re, b_im, c_re, c_im):
    hp = lax.Precision.HIGHEST
    dt = jnp.exp(log_dt.astype(F32))[:, None]
    are, aim = a_re.astype(F32), a_im.astype(F32)
    tau = jnp.arange(S5_CHUNK + 1, dtype=F32)[None, :, None]
    mag = jnp.exp(tau * (dt * are)[:, None, :])
    ang = tau * (dt * aim)[:, None, :]
    pw_re, pw_im = mag * jnp.cos(ang), mag * jnp.sin(ang)
    ab_re, ab_im = pw_re[:, 1], pw_im[:, 1]
    den = are * are + aim * aim
    f_re = ((ab_re - 1.0) * are + ab_im * aim) / den
    f_im = (ab_im * are - (ab_re - 1.0) * aim) / den
    bre, bim = b_re.astype(F32), b_im.astype(F32)
    bb_re = f_re[..., None] * bre - f_im[..., None] * bim
    bb_im = f_re[..., None] * bim + f_im[..., None] * bre
    cre, cim = c_re.astype(F32), c_im.astype(F32)
    return dict(pw_re=pw_re, pw_im=pw_im, ab_re=ab_re, ab_im=ab_im, bb_re=bb_re, bb_im=bb_im,
                c_re=cre, c_im=cim, hp=hp)


def _s5_chunk_mats(sp, d_skip):
    hp = sp["hp"]
    g, t, c, p = N_SSM_GROUPS, S5_CHUNK, SSM_GROUP, SSM_STATE
    pw_re, pw_im = sp["pw_re"], sp["pw_im"]
    ca_re = sp["c_re"][:, None] * pw_re[:, :, None, :] - sp["c_im"][:, None] * pw_im[:, :, None, :]
    ca_im = sp["c_re"][:, None] * pw_im[:, :, None, :] + sp["c_im"][:, None] * pw_re[:, :, None, :]
    kern = (jnp.einsum("gtcp,gpd->gtcd", ca_re[:, :t], sp["bb_re"], precision=hp)
            - jnp.einsum("gtcp,gpd->gtcd", ca_im[:, :t], sp["bb_im"], precision=hp))
    kc = jnp.swapaxes(kern, 2, 3)
    kc = kc.at[:, 0].add(d_skip.astype(F32).reshape(g, 1, c) * jnp.eye(c, dtype=F32)[None])
    rev_re, rev_im = pw_re[:, t - 1::-1][:, :t], pw_im[:, t - 1::-1][:, :t]
    wst_re = rev_re[:, :, None, :] * jnp.swapaxes(sp["bb_re"], 1, 2)[:, None] \
        - rev_im[:, :, None, :] * jnp.swapaxes(sp["bb_im"], 1, 2)[:, None]
    wst_im = rev_re[:, :, None, :] * jnp.swapaxes(sp["bb_im"], 1, 2)[:, None] \
        + rev_im[:, :, None, :] * jnp.swapaxes(sp["bb_re"], 1, 2)[:, None]
    wo_re = jnp.transpose(ca_re[:, 1:t + 1], (0, 3, 1, 2))
    wo_im = -jnp.transpose(ca_im[:, 1:t + 1], (0, 3, 1, 2))
    nv, gl = g // S5_LANE_GROUPS, S5_LANE_GROUPS
    kc, wst_re, wst_im, wo_re, wo_im = lax.optimization_barrier((kc, wst_re, wst_im, wo_re, wo_im))
    kc5 =jnp.transpose(kc.reshape(nv, gl, t, c, c), (0, 2, 1, 3, 4))
    ws6 = jnp.transpose(jnp.stack([wst_re, wst_im], axis=3).reshape(nv, gl, t, c, 2, p),
                        (0, 2, 1, 3, 4, 5))
    wo6 = jnp.transpose(jnp.stack([wo_re, wo_im], axis=0).reshape(2, nv, gl, p, t, c),
                        (1, 0, 2, 3, 4, 5))
    kc5, ws6, wo6 = lax.optimization_barrier((kc5.astype(BF16), ws6.astype(BF16), wo6.astype(BF16)))
    spread_b = np.zeros((c, LANES), np.float32)
    spread_s = np.zeros((2 * p, 2 * gl * p), np.float32)
    spread_o = np.zeros((t * c, t * LANES), np.float32)
    for h in range(gl):
        spread_b[np.arange(c), h * c + np.arange(c)] = 1.0
        for ri in range(2):
            spread_s[ri * p + np.arange(p), ri * gl * p + h * p + np.arange(p)] = 1.0
        for tt in range(t):
            spread_o[tt * c + np.arange(c), tt * LANES + h * c + np.arange(c)] = 1.0
    at_re = pw_re[:, t].reshape(1, g * p)
    at_im = pw_im[:, t].reshape(1, g * p)
    return dict(kc=kc5.reshape(nv, t, LANES, c), ws=ws6.reshape(nv, t * LANES, 2 * p),
                wo=wo6.reshape(nv, 2 * gl * p, t * c), spread_b=jnp.asarray(spread_b, BF16),
                spread_s=jnp.asarray(spread_s, BF16), spread_o=jnp.asarray(spread_o, BF16),
                at_re=at_re, at_im=at_im)


def _s5_chunk_rows(u_ref, nchunk):
    return jnp.concatenate(
        [u_ref[pl.ds(s, nchunk, stride=S5_CHUNK), :] for s in range(S5_CHUNK)], axis=1).astype(BF16)


S5_SLABS = S5_LANE_GROUPS * SSM_STATE // LANES


S5_EXPAND_ROWS = 256
S5_C_SHIFT = SSM_GROUP.bit_length() - 1
S5_P_SHIFT = SSM_STATE.bit_length() - 1


def _s5_expand(dst_ref, compact_ref, spread_ref, row_shift, col_shift):
    n_rows, n_cols = dst_ref.shape
    col_g = lax.shift_right_logical(lax.broadcasted_iota(jnp.int32, (S5_EXPAND_ROWS, n_cols), 1), col_shift)
    for r0 in range(0, n_rows, S5_EXPAND_ROWS):
        row_g = lax.shift_right_logical(r0 + lax.broadcasted_iota(jnp.int32, (S5_EXPAND_ROWS, n_cols), 0), row_shift)
        same = ((row_g ^ col_g) & (S5_LANE_GROUPS - 1)) == 0
        blk = jnp.dot(compact_ref[r0:r0 + S5_EXPAND_ROWS, :], spread_ref[...], preferred_element_type=F32)
        dst_ref[r0:r0 + S5_EXPAND_ROWS, :] = jnp.where(same, blk, 0.0).astype(dst_ref.dtype)


def _s5_state_kernel(u_ref, ws_ref, spread_ref, sre_ref, sim_ref, wst_sc):
    nchunk = sre_ref.shape[1]

    @pl.when(pl.program_id(1) == 0)
    def _():
        _s5_expand(wst_sc, ws_ref.at[0], spread_ref, S5_C_SHIFT, S5_P_SHIFT)

    s = jnp.dot(_s5_chunk_rows(u_ref, nchunk), wst_sc[...], preferred_element_type=F32)
    for k in range(S5_SLABS):
        sre_ref[k] = s[:, k * LANES:(k + 1) * LANES]
        sim_ref[k] = s[:, (S5_SLABS + k) * LANES:(S5_SLABS + k + 1) * LANES]


def _s5_scan_kernel(sre_ref, sim_ref, are_ref, aim_ref, hre_ref, him_ref, fre_ref, fim_ref, *, bsz):
    nchunk = sre_ref.shape[1] // bsz
    are = [jnp.broadcast_to(are_ref[:, k * LANES:(k + 1) * LANES], (bsz, LANES)) for k in range(S5_SLABS)]
    aim = [jnp.broadcast_to(aim_ref[:, k * LANES:(k + 1) * LANES], (bsz, LANES)) for k in range(S5_SLABS)]

    def body(j, carry):
        rows = pl.ds(j, bsz, stride=nchunk)
        out = []
        for k in range(S5_SLABS):
            cre, cim = carry[2 * k], carry[2 * k + 1]
            hre_ref[k, rows, :] = cre
            him_ref[k, rows, :] = cim
            sr = sre_ref[k, rows, :]
            si = sim_ref[k, rows, :]
            out += [are[k] * cre - aim[k] * cim + sr, are[k] * cim + aim[k] * cre + si]
        return tuple(out)

    zero = jnp.zeros((bsz, LANES), F32)
    fin = lax.fori_loop(0, nchunk, body, (zero,) * (2 * S5_SLABS), unroll=4)
    fre_ref[...] = jnp.concatenate(fin[0::2], axis=1)
    fim_ref[...] = jnp.concatenate(fin[1::2], axis=1)


def _s5_out_kernel(u_ref, kc_ref, spread_b_ref, hre_ref, him_ref, wo_ref, spread_o_ref, y_ref, m_sc, wout_sc):
    nchunk = hre_ref.shape[1]

    @pl.when(pl.program_id(1) == 0)
    def _():
        rg = lax.shift_right_logical(lax.broadcasted_iota(jnp.int32, (LANES, LANES), 0), S5_C_SHIFT)
        cg = lax.shift_right_logical(lax.broadcasted_iota(jnp.int32, (LANES, LANES), 1), S5_C_SHIFT)
        zero_blk = jnp.zeros((LANES, LANES), BF16)
        lag_blk = [jnp.where(rg == cg, jnp.dot(kc_ref[0, tau], spread_b_ref[...], preferred_element_type=F32),
                             0.0).astype(BF16) for tau in range(S5_CHUNK)]
        for s in range(S5_CHUNK):
            for t in range(S5_CHUNK):
                m_sc[s * LANES:(s + 1) * LANES, t * LANES:(t + 1) * LANES] = lag_blk[t - s] if t >= s else zero_blk
        _s5_expand(wout_sc, wo_ref.at[0], spread_o_ref, S5_P_SHIFT, S5_C_SHIFT)

    hcat = jnp.concatenate([hre_ref[k] for k in range(S5_SLABS)] + [him_ref[k] for k in range(S5_SLABS)],
                           axis=1).astype(BF16)
    lhs = _s5_chunk_rows(u_ref, nchunk)
    y = jnp.concatenate(
        [jnp.dot(lhs[:, :j + MXU_DIM], m_sc[:j + MXU_DIM, j:j + MXU_DIM], preferred_element_type=F32)
         for j in range(0, S5_CHUNK * LANES, MXU_DIM)], axis=1)
    y = y + jnp.dot(hcat, wout_sc[...], preferred_element_type=F32)
    for s in range(S5_CHUNK):
        y_ref[pl.ds(s, nchunk, stride=S5_CHUNK), :] = y[:, s * LANES:(s + 1) * LANES]


def _s5_prompt(u, bsz, seq, mats):
    at_re, at_im = mats["at_re"], mats["at_im"]
    g, t, p, c = N_SSM_GROUPS, S5_CHUNK, SSM_STATE, SSM_GROUP
    nchunk = seq // t
    n = nchunk * bsz
    nv = g // S5_LANE_GROUPS
    half = S5_LANE_GROUPS * p
    s_re, s_im = pl.pallas_call(
        _s5_state_kernel,
        grid=(nv, bsz),
        in_specs=[pl.BlockSpec((seq, LANES), lambda v, b: (b, v)),
                  pl.BlockSpec((1, t * LANES, 2 * p), lambda v, b: (v, 0, 0)),
                  pl.BlockSpec((2 * p, 2 * half), lambda v, b: (0, 0))],
        out_specs=[pl.BlockSpec((S5_SLABS, nchunk, LANES), lambda v, b: (v, b, 0)),
                   pl.BlockSpec((S5_SLABS, nchunk, LANES), lambda v, b: (v, b, 0))],
        out_shape=[jax.ShapeDtypeStruct((nv * S5_SLABS, n, LANES), F32)] * 2,
        scratch_shapes=[pltpu.VMEM((t * LANES, 2 * half), BF16)],
        compiler_params=_cparams(("parallel", "arbitrary"), VMEM_LIMIT),
        name="s5_state",
    )(u, mats["ws"], mats["spread_s"])
    h_re, h_im, f_re, f_im = pl.pallas_call(
        functools.partial(_s5_scan_kernel, bsz=bsz),
        grid=(nv,),
        in_specs=[pl.BlockSpec((S5_SLABS, n, LANES), lambda i: (i, 0, 0)),
                  pl.BlockSpec((S5_SLABS, n, LANES), lambda i: (i, 0, 0)),
                  pl.BlockSpec((1, half), lambda i: (0, i)),
                  pl.BlockSpec((1, half), lambda i: (0, i))],
        out_specs=[pl.BlockSpec((S5_SLABS, n, LANES), lambda i: (i, 0, 0)),
                   pl.BlockSpec((S5_SLABS, n, LANES), lambda i: (i, 0, 0)),
                   pl.BlockSpec((bsz, half), lambda i: (0, i)),
                   pl.BlockSpec((bsz, half), lambda i: (0, i))],
        out_shape=[jax.ShapeDtypeStruct((nv * S5_SLABS, n, LANES), F32)] * 2
        + [jax.ShapeDtypeStruct((bsz, g * p), F32)] * 2,
        compiler_params=_cparams(("parallel",)),
        name="s5_scan",
    )(s_re, s_im, at_re, at_im)
    y = pl.pallas_call(
        _s5_out_kernel,
        grid=(nv, bsz),
        in_specs=[pl.BlockSpec((seq, LANES), lambda v, b: (b, v)),
                  pl.BlockSpec((1, t, LANES, c), lambda v, b: (v, 0, 0, 0)),
                  pl.BlockSpec((c, LANES), lambda v, b: (0, 0)),
                  pl.BlockSpec((S5_SLABS, nchunk, LANES), lambda v, b: (v, b, 0)),
                  pl.BlockSpec((S5_SLABS, nchunk, LANES), lambda v, b: (v, b, 0)),
                  pl.BlockSpec((1, 2 * half, t * c), lambda v, b: (v, 0, 0)),
                  pl.BlockSpec((t * c, t * LANES), lambda v, b: (0, 0))],
        out_specs=pl.BlockSpec((seq, LANES), lambda v, b: (b, v)),
        out_shape=jax.ShapeDtypeStruct((bsz * seq, D_SSM), F32),
        scratch_shapes=[pltpu.VMEM((t * LANES, t * LANES), BF16), pltpu.VMEM((2 * half, t * LANES), BF16)],
        compiler_params=_cparams(("parallel", "arbitrary"), VMEM_LIMIT),
        name="s5_out",
    )(u, mats["kc"], mats["spread_b"], h_re, h_im, mats["wo"], mats["spread_o"])
    return y, f_re, f_im


S5S_GROUPS = LANES // SSM_GROUP


def _s5_sample_mats(sp, d_skip):
    go, gl, c, p = N_SSM_GROUPS // S5S_GROUPS, S5S_GROUPS, SSM_GROUP, SSM_STATE
    eye = jnp.eye(gl, dtype=F32)

    def bdiag_in(b):
        b4 = b.reshape(go, gl, p, c)
        return jnp.einsum("ogpc,gh->ogchp", b4, eye).reshape(go, gl * c, gl * p)

    def bdiag_out(cm):
        c4 = cm.reshape(go, gl, c, p)
        return jnp.einsum("ogcp,gh->ogphc", c4, eye).reshape(go, gl * p, gl * c)

    b8 = jnp.concatenate([bdiag_in(sp["bb_re"]), bdiag_in(sp["bb_im"])], axis=2)
    c8 = jnp.concatenate([bdiag_out(sp["c_re"]), -bdiag_out(sp["c_im"])], axis=1)
    a_re = sp["ab_re"].reshape(1, N_SSM_GROUPS * p)
    a_im = sp["ab_im"].reshape(1, N_SSM_GROUPS * p)
    return b8, c8, a_re, a_im, d_skip.astype(F32).reshape(1, D_SSM)


def _s5_sample_kernel(u_ref, hre_ref, him_ref, b8_ref, c8_ref, are_ref, aim_ref, d_ref,
                      y_ref, ore_ref, oim_ref):
    hp = lax.Precision.HIGHEST
    u = u_ref[...]
    half = S5S_GROUPS * SSM_STATE
    bu = jnp.dot(u, b8_ref[0], preferred_element_type=F32, precision=hp)
    are, aim = are_ref[...], aim_ref[...]
    h0r, h0i = hre_ref[...], him_ref[...]
    hr = are * h0r - aim * h0i + bu[:, :half]
    hi = are * h0i + aim * h0r + bu[:, half:]
    ore_ref[...] = hr
    oim_ref[...] = hi
    y = jnp.dot(jnp.concatenate([hr, hi], axis=1), c8_ref[0], preferred_element_type=F32, precision=hp)
    y_ref[...] = (y + d_ref[...] * u).astype(y_ref.dtype)


def _s5_sample(u, h0_re, h0_im, mats):
    b8, c8, a_re, a_im, d = mats
    n = u.shape[0]
    half = S5S_GROUPS * SSM_STATE
    return pl.pallas_call(
        _s5_sample_kernel,
        grid=(N_SSM_GROUPS // S5S_GROUPS,),
        in_specs=[pl.BlockSpec((n, LANES), lambda i: (0, i)),
                  pl.BlockSpec((n, half), lambda i: (0, i)),
                  pl.BlockSpec((n, half), lambda i: (0, i)),
                  pl.BlockSpec((1, LANES, 2 * half), lambda i: (i, 0, 0)),
                  pl.BlockSpec((1, 2 * half, LANES), lambda i: (i, 0, 0)),
                  pl.BlockSpec((1, half), lambda i: (0, i)),
                  pl.BlockSpec((1, half), lambda i: (0, i)),
                  pl.BlockSpec((1, LANES), lambda i: (0, i))],
        out_specs=[pl.BlockSpec((n, LANES), lambda i: (0, i)),
                   pl.BlockSpec((n, half), lambda i: (0, i)),
                   pl.BlockSpec((n, half), lambda i: (0, i))],
        out_shape=[jax.ShapeDtypeStruct((n, D_SSM), F32),
                   jax.ShapeDtypeStruct((n, N_SSM_GROUPS * SSM_STATE), F32),
                   jax.ShapeDtypeStruct((n, N_SSM_GROUPS * SSM_STATE), F32)],
        compiler_params=_cparams(("parallel",)),
        name="s5_sample",
    )(u, h0_re, h0_im, b8, c8, a_re, a_im, d)


def _layer_norm(x, g, b):
    mu = jnp.mean(x, axis=-1, keepdims=True)
    xc = x - mu
    var = jnp.mean(xc * xc, axis=-1, keepdims=True)
    return xc * lax.rsqrt(var + LN_EPS) * g + b


def _sigmoid(x):
    return 0.5 * jnp.tanh(0.5 * x) + 0.5


RUN_ROWS = SUBLANES
TAB_ROWS = 3


def _merge_kernel(x_ref, oa_ref, ys_ref, carry_in_ref, wao_ref, wso_ref, wg_ref, bg_ref, wo_ref,
                  g1_ref, b1_ref, wrt_ref, brt_ref,
                  x1_ref, lpos_ref, cols_ref, tab_ref, carry_out_ref, carry_sc, *, f32_matmuls):
    step = pl.program_id(0)

    @pl.when(step == 0)
    def _():
        carry_sc[...] = carry_in_ref[...]

    def mm(a, w_ref):
        if f32_matmuls:
            return _dot_split(a.astype(F32), w_ref[...])
        return jnp.dot(a.astype(BF16), w_ref[...], preferred_element_type=F32)

    tm = x_ref.shape[0]
    x = x_ref[...]
    branch_a = mm(oa_ref[...], wao_ref)
    z = mm(jax.nn.gelu(ys_ref[...].astype(F32)), wso_ref)
    branch_b = z[:, :D_MODEL] * _sigmoid(z[:, D_MODEL:])
    gates = _sigmoid(mm(x, wg_ref) + bg_ref[...])
    mixed = gates[:, :D_MODEL] * branch_a + gates[:, D_MODEL:] * branch_b
    mix = mm(mixed, wo_ref)
    x1 = _layer_norm(DEEPNORM_ALPHA * x + mix, g1_ref[...], b1_ref[...])
    x1_ref[...] = x1

    split2 = _split_bf16

    def dot_nt(a, b):
        return lax.dot_general(a, b, (((1,), (1,)), ((), ())), preferred_element_type=F32)

    w_hi, w_lo = split2(wrt_ref[...])
    rt = tm // tab_ref.shape[0]
    sub = lax.broadcasted_iota(jnp.int32, (N_EXPERTS, rt), 0)
    r = lax.broadcasted_iota(jnp.int32, (rt, rt), 0)
    c = lax.broadcasted_iota(jnp.int32, (rt, rt), 1)
    er = lax.broadcasted_iota(jnp.int32, (N_EXPERTS, N_EXPERTS), 0)
    ec = lax.broadcasted_iota(jnp.int32, (N_EXPERTS, N_EXPERTS), 1)
    rid = lax.broadcasted_iota(jnp.int32, (SUBLANES, LANES), 0)
    lane_pad = jnp.zeros((SUBLANES, LANES - N_EXPERTS), F32)
    for h in range(tab_ref.shape[0]):
        x_hi, x_lo = split2(x1[h * rt:(h + 1) * rt])
        logits = dot_nt(w_hi, x_hi) + dot_nt(w_hi, x_lo) + dot_nt(w_lo, x_hi) + brt_ref[...]
        work = logits
        vals, sels = [], []
        for _ in range(TOP_K):
            mx = jnp.max(work, axis=0, keepdims=True)
            idx = jnp.min(jnp.where(work == mx, sub, N_EXPERTS), axis=0, keepdims=True)
            sel = sub == idx
            vals.append(mx)
            sels.append(sel)
            work = jnp.where(sel, -jnp.inf, work)
        ex = [jnp.exp(v - vals[0]) for v in vals]
        tot = ex[0] + ex[1] + ex[2] + ex[3]
        gate_rows = jnp.concatenate([e / tot for e in ex], axis=0)

        multi = jnp.zeros((N_EXPERTS, rt), F32)
        for sel in sels:
            multi = multi + jnp.where(sel, 1.0, 0.0)
        multi_b = multi.astype(BF16)
        earlier = jnp.dot(multi_b, jnp.where(r < c, 1.0, 0.0).astype(BF16), preferred_element_type=F32)
        cnt_col = jnp.sum(multi, axis=1, keepdims=True)
        nb_col = jnp.floor((cnt_col + (RUN_ROWS - 1.0)) * (1.0 / RUN_ROWS))
        loff_col = jnp.dot(jnp.where(ec < er, 1.0, 0.0).astype(BF16),
                           jnp.broadcast_to(nb_col, (N_EXPERTS, rt)).astype(BF16), preferred_element_type=F32)
        base = RUN_ROWS * loff_col + earlier
        lpos = jnp.concatenate([jnp.sum(jnp.where(sel, base, 0.0), axis=0, keepdims=True) for sel in sels],
                               axis=0)
        lpos_ref[:, h * rt:(h + 1) * rt] = lpos.astype(jnp.int32)
        rows_hi, rows_lo = split2(jnp.concatenate([lpos, gate_rows], axis=0))
        eye = jnp.where(r == c, 1.0, 0.0).astype(BF16)
        cols_ref[h * rt:(h + 1) * rt, :] = dot_nt(eye, rows_hi) + dot_nt(eye, rows_lo)

        cnt_row = dot_nt(jnp.ones((SUBLANES, rt), BF16), multi_b)
        nb_row = jnp.floor((cnt_row + (RUN_ROWS - 1.0)) * (1.0 / RUN_ROWS))
        loff_row = jnp.dot(nb_row.astype(BF16), jnp.where(er < ec, 1.0, 0.0).astype(BF16),
                           preferred_element_type=F32)
        nb_p = jnp.concatenate([nb_row, lane_pad], axis=1)
        loff_p = jnp.concatenate([loff_row, lane_pad], axis=1)
        goff_p = carry_sc[...]
        tab = jnp.where(rid == 0, nb_p, jnp.where(rid == 1, loff_p, jnp.where(rid == 2, goff_p, 0.0)))
        tab_ref[h] = tab.astype(jnp.int32)
        carry_sc[...] = goff_p + nb_p
    carry_out_ref[...] = carry_sc[...]


def _merge(x, o_attn, y_ssm, carry_in, w, *, tile, route_tile, f32_matmuls):
    n = x.shape[0]
    nt = n // tile
    per_step = tile // route_tile
    full = lambda shape: pl.BlockSpec(shape, lambda i: (0,) * len(shape))
    return pl.pallas_call(
        functools.partial(_merge_kernel, f32_matmuls=f32_matmuls),
        grid=(nt,),
        in_specs=[pl.BlockSpec((tile, D_MODEL), lambda i: (i, 0)),
                  pl.BlockSpec((tile, D_ATTN), lambda i: (i, 0)),
                  pl.BlockSpec((tile, D_SSM), lambda i: (i, 0)),
                  full((SUBLANES, LANES)),
                  full((D_ATTN, D_MODEL)), full((D_SSM, 2 * D_MODEL)), full((D_MODEL, 2 * D_MODEL)),
                  full((1, 2 * D_MODEL)), full((D_MODEL, D_MODEL)),
                  full((1, D_MODEL)), full((1, D_MODEL)),
                  full((N_EXPERTS, D_MODEL)), full((N_EXPERTS, 1))],
        out_specs=[pl.BlockSpec((tile, D_MODEL), lambda i: (i, 0)),
                   pl.BlockSpec((TOP_K, tile), lambda i: (0, i)),
                   pl.BlockSpec((tile, 2 * TOP_K), lambda i: (i, 0)),
                   pl.BlockSpec((per_step, SUBLANES, LANES), lambda i: (i, 0, 0)),
                   full((SUBLANES, LANES))],
        out_shape=[jax.ShapeDtypeStruct((n, D_MODEL), F32),
                   jax.ShapeDtypeStruct((TOP_K, n), jnp.int32),
                   jax.ShapeDtypeStruct((n, 2 * TOP_K), F32),
                   jax.ShapeDtypeStruct((nt * per_step, SUBLANES, LANES), jnp.int32),
                   jax.ShapeDtypeStruct((SUBLANES, LANES), F32)],
        scratch_shapes=[pltpu.VMEM((SUBLANES, LANES), F32)],
        compiler_params=_cparams(("arbitrary",), VMEM_LIMIT_LARGE),
        name="merge",
    )(x, o_attn, y_ssm, carry_in, w["wao"], w["wso"], w["wg"], w["bg"], w["wo"], w["g1"], w["b1"],
      w["wrt"], w["brt"])


def _tab(tab_ref, tile, row, e):
    return tab_ref[(tile * TAB_ROWS + row) * N_EXPERTS + e]


BIG_PIECE = 4 * RUN_ROWS
MAX_UNITS_LOG2 = 8


def _for_each_run_piece(tab_ref, tile, fn):
    def per_expert(e, carry):
        loff = RUN_ROWS * _tab(tab_ref, tile, 1, e)
        goff = RUN_ROWS * _tab(tab_ref, tile, 2, e)
        units = _tab(tab_ref, tile, 0, e)
        n_big = lax.shift_right_logical(units, 2)

        def big(j, c2):
            fn(pl.multiple_of(loff + j * BIG_PIECE, RUN_ROWS), goff + j * BIG_PIECE, e, BIG_PIECE)
            return c2

        lax.fori_loop(0, n_big, big, 0)
        done = n_big * BIG_PIECE

        def small(j, c2):
            fn(pl.multiple_of(loff + done + j * RUN_ROWS, RUN_ROWS), goff + done + j * RUN_ROWS, e, RUN_ROWS)
            return c2

        lax.fori_loop(0, units & 3, small, 0)
        return carry

    lax.fori_loop(0, N_EXPERTS, per_expert, 0)


def _drain_units(units, wait_copy, buffer_rows):
    assert buffer_rows < (RUN_ROWS << MAX_UNITS_LOG2)
    for b in range(MAX_UNITS_LOG2):
        if (RUN_ROWS << b) > buffer_rows:
            break

        @pl.when((lax.shift_right_logical(units, b) & 1) == 1)
        def _():
            wait_copy(RUN_ROWS << b).wait()


def _dispatch_kernel(tab_ref, seg_ref, tot_ref, tail_ref, lpos_p_ref, xp_ref, lpos_s_ref, xs_in_ref, xs_ref,
                     loc_sc, zero_sc, sem, zsem):
    i = pl.program_id(0)
    last = pl.num_programs(0) - 1
    tile = i
    slot = i % 2
    loc = loc_sc.shape[1]

    @pl.when(i == 0)
    def _():
        zero_sc[...] = jnp.zeros_like(zero_sc)

        def tail_copy(e, j):
            row = pl.multiple_of(RUN_ROWS * (tail_ref[e] + j), RUN_ROWS)
            return pltpu.make_async_copy(zero_sc.at[pl.ds(0, RUN_ROWS)], xs_ref.at[pl.ds(row, RUN_ROWS)], zsem)

        def per_expert(e, carry):
            n = tail_ref[N_EXPERTS + e]
            lax.fori_loop(0, n, lambda j, c2: (tail_copy(e, j).start(), c2)[1], 0)
            lax.fori_loop(0, n, lambda j, c2: (tail_copy(e, j).wait(), c2)[1], 0)
            return carry

        lax.fori_loop(0, N_EXPERTS, per_expert, 0)

        def block_copy(b):
            row = pl.multiple_of(b * MOE_ROWS, MOE_ROWS)
            return pltpu.make_async_copy(zero_sc, xs_ref.at[pl.ds(row, MOE_ROWS)], zsem)

        first_unused, n_blocks = tail_ref[2 * N_EXPERTS], xs_ref.shape[0] // MOE_ROWS
        lax.fori_loop(first_unused, n_blocks, lambda b, c2: (block_copy(b).start(), c2)[1], 0)
        lax.fori_loop(first_unused, n_blocks, lambda b, c2: (block_copy(b).wait(), c2)[1], 0)

    def sort_tile(lpos_ref, x_ref):
        tm = x_ref.shape[0]
        rows = lax.broadcasted_iota(jnp.int32, (loc, tm), 0)
        lp = lpos_ref[...]
        onehot = jnp.zeros((loc, tm), F32)
        for k in range(TOP_K):
            onehot = jnp.where(rows == lp[k:k + 1], 1.0, onehot)
        loc_sc[slot] = jnp.dot(onehot.astype(BF16), x_ref[...].astype(BF16), preferred_element_type=F32)

    @pl.when(i < last)
    def _():
        sort_tile(lpos_p_ref, xp_ref)

    @pl.when(i == last)
    def _():
        sort_tile(lpos_s_ref, xs_in_ref)

    def piece_copy(sl, lrow, grow, e, n):
        dst = pl.multiple_of(seg_ref[e] + grow, RUN_ROWS)
        return pltpu.make_async_copy(loc_sc.at[sl, pl.ds(lrow, n)], xs_ref.at[pl.ds(dst, n)], sem.at[sl])

    _for_each_run_piece(tab_ref, tile, lambda lrow, grow, e, n: piece_copy(slot, lrow, grow, e, n).start())

    def drain(tl, sl):
        _drain_units(tot_ref[tl], lambda n: piece_copy(sl, 0, 0, 0, n), loc)

    @pl.when(i > 0)
    def _():
        drain(tile - 1, 1 - slot)

    @pl.when(i == last)
    def _():
        drain(tile, slot)


def _dispatch(tab, seg_start, tot, tails, lpos_p, x1_p, lpos_s, x1_s, *, tile, nrows):
    nt_p = x1_p.shape[0] // tile
    ns = x1_s.shape[0]
    loc = tile * TOP_K + N_EXPERTS * RUN_ROWS
    prompt_blk = lambda i, *_: jnp.minimum(i, nt_p - 1)
    return pl.pallas_call(
        _dispatch_kernel,
        grid_spec=pltpu.PrefetchScalarGridSpec(
            num_scalar_prefetch=4,
            grid=(nt_p + 1,),
            in_specs=[pl.BlockSpec((TOP_K, tile), lambda i, *_: (0, prompt_blk(i))),
                      pl.BlockSpec((tile, D_MODEL), lambda i, *_: (prompt_blk(i), 0)),
                      pl.BlockSpec((TOP_K, ns), lambda i, *_: (0, 0)),
                      pl.BlockSpec((ns, D_MODEL), lambda i, *_: (0, 0))],
            out_specs=pl.BlockSpec(memory_space=pl.ANY),
            scratch_shapes=[pltpu.VMEM((2, loc, D_MODEL), F32), pltpu.VMEM((MOE_ROWS, D_MODEL), F32),
                            pltpu.SemaphoreType.DMA((2,)), pltpu.SemaphoreType.DMA(())]),
        out_shape=jax.ShapeDtypeStruct((nrows, D_MODEL), F32),
        compiler_params=_cparams(("arbitrary",), VMEM_LIMIT),
        name="dispatch",
    )(tab, seg_start, tot, tails, lpos_p, x1_p, lpos_s, x1_s)


def _deinterleave_matrix():
    pm = np.zeros((MXU_DIM, MXU_DIM), np.float32)
    half = MXU_DIM // 2
    for c in range(half):
        pm[2 * c, c] = 1.0
        pm[2 * c + 1, half + c] = 1.0
    return pm


def _expert_kernel(be_ref, nu_ref, nv_ref, ord_ref, nxt_ref, xs_ref, w1_hbm, b1_ref, w2_hbm, b2_ref, pm_ref, y_ref,
                   w1f_sc, w2f_sc, w1p_sc, w2b_sc, sem):
    del nu_ref
    i = pl.program_id(0)
    e = be_ref[i]
    prev = be_ref[jnp.maximum(i - 1, 0)]
    nblk = 2 * D_FF // MXU_DIM

    def weight_copies(expert, slot):
        return (pltpu.make_async_copy(w1_hbm.at[expert], w1f_sc.at[slot], sem.at[0, slot]),
                pltpu.make_async_copy(w2_hbm.at[expert], w2f_sc.at[slot], sem.at[1, slot]))

    @pl.when(i == 0)
    def _():
        for cp in weight_copies(e, 0):
            cp.start()

    @pl.when((i == 0) | (e != prev))
    def _():
        slot = ord_ref[i] % 2
        for cp in weight_copies(e, slot):
            cp.wait()

        for cb in range(nblk):
            blk = w1f_sc[slot, :, cb * MXU_DIM:(cb + 1) * MXU_DIM].astype(BF16)
            w1p_sc[:, cb * MXU_DIM:(cb + 1) * MXU_DIM] = jnp.dot(
                blk, pm_ref[...], preferred_element_type=F32).astype(BF16)
        w2b_sc[...] = w2f_sc[slot].astype(BF16)

        nxt = nxt_ref[i]

        @pl.when(nxt >= 0)
        def _():
            for cp in weight_copies(nxt, 1 - slot):
                cp.start()

    for blk in range(MOE_STEP_BLOCKS):
        rows = slice(blk * MOE_ROWS, (blk + 1) * MOE_ROWS)

        @pl.when(blk < nv_ref[i])
        def _():
            x = xs_ref[rows, :].astype(BF16)
            h = jnp.dot(x, w1p_sc[...], preferred_element_type=F32) + b1_ref[0]
            half = MXU_DIM // 2
            acts = []
            for cb in range(nblk):
                x_glu = jnp.minimum(h[:, cb * MXU_DIM:cb * MXU_DIM + half], SWIGLU_LIMIT)
                x_lin = jnp.clip(h[:, cb * MXU_DIM + half:(cb + 1) * MXU_DIM], -SWIGLU_LIMIT, SWIGLU_LIMIT)
                acts.append((x_glu * jax.nn.sigmoid(SWIGLU_ALPHA * x_glu) * (x_lin + 1.0)).astype(BF16))
            act = jnp.concatenate(acts, axis=1)
            y_ref[rows, :] = jnp.dot(act, w2b_sc[...], preferred_element_type=F32) + b2_ref[0]

        @pl.when(blk >= nv_ref[i])
        def _():
            y_ref[rows, :] = jnp.zeros((MOE_ROWS, D_MODEL), F32)


def _experts(block_e, n_used, n_valid, run_ord, run_next, xs, w1, b1p, w2, b2, pm):
    nrows = xs.shape[0]
    step_rows = MOE_STEP_BLOCKS * MOE_ROWS
    nb = nrows // step_rows
    return pl.pallas_call(
        _expert_kernel,
        grid_spec=pltpu.PrefetchScalarGridSpec(
            num_scalar_prefetch=5,
            grid=(nb,),
            in_specs=[pl.BlockSpec((step_rows, D_MODEL), lambda i, be, nu, *_: (jnp.minimum(i, nu[0] - 1), 0)),
                      pl.BlockSpec(memory_space=pl.ANY),
                      pl.BlockSpec((1, 1, 2 * D_FF), lambda i, be, *_: (be[i], 0, 0)),
                      pl.BlockSpec(memory_space=pl.ANY),
                      pl.BlockSpec((1, 1, D_MODEL), lambda i, be, *_: (be[i], 0, 0)),
                      pl.BlockSpec((MXU_DIM, MXU_DIM), lambda i, *_: (0, 0))],
            out_specs=pl.BlockSpec((step_rows, D_MODEL), lambda i, *_: (i, 0)),
            scratch_shapes=[pltpu.VMEM((2, D_MODEL, 2 * D_FF), F32), pltpu.VMEM((2, D_FF, D_MODEL), F32),
                            pltpu.VMEM((D_MODEL, 2 * D_FF), BF16), pltpu.VMEM((D_FF, D_MODEL), BF16),
                            pltpu.SemaphoreType.DMA((2, 2))]),
        out_shape=jax.ShapeDtypeStruct((nrows, D_MODEL), F32),
        compiler_params=_cparams(("arbitrary",), VMEM_LIMIT_LARGE),
        name="experts",
    )(block_e, n_used, n_valid, run_ord, run_next, xs, w1, b1p, w2, b2, pm)


def _combine_kernel(tab_ref, seg_ref, tot_ref, cols_ref, x1_ref, g2_ref, b2_ref, ys_ref, y_ref, loc_sc, sem,
                    *, tile_base):
    i = pl.program_id(0)
    last = pl.num_programs(0) - 1
    tile = i + tile_base
    slot = i % 2
    loc, tm = loc_sc.shape[1], x1_ref.shape[0]

    def piece_copy(sl, lrow, grow, e, n):
        src = pl.multiple_of(seg_ref[e] + grow, RUN_ROWS)
        return pltpu.make_async_copy(ys_ref.at[pl.ds(src, n)], loc_sc.at[sl, pl.ds(lrow, n)], sem.at[sl])

    def gather(tl, sl):
        _for_each_run_piece(tab_ref, tl, lambda lrow, grow, e, n: piece_copy(sl, lrow, grow, e, n).start())

    @pl.when(i == 0)
    def _():
        loc_sc[...] = jnp.zeros_like(loc_sc)
        gather(tile, slot)

    @pl.when(i < last)
    def _():
        gather(tile + 1, 1 - slot)

    _drain_units(tot_ref[tile], lambda n: piece_copy(slot, 0, 0, 0, n), loc)

    cols = cols_ref[...]
    lane = lax.broadcasted_iota(jnp.int32, (tm, loc), 1)
    weights = jnp.zeros((tm, loc), F32)
    for k in range(TOP_K):
        weights = jnp.where(lane == cols[:, k:k + 1].astype(jnp.int32), cols[:, TOP_K + k:TOP_K + k + 1], weights)
    ffn = jnp.dot(weights.astype(BF16), loc_sc[slot].astype(BF16), preferred_element_type=F32)
    y_ref[...] = _layer_norm(DEEPNORM_ALPHA * x1_ref[...] + ffn, g2_ref[...], b2_ref[...])


def _combine(tab, seg_start, tot, cols, x1, g2, b2, ys, *, tile, tile_base):
    n = x1.shape[0]
    loc = tile * TOP_K + N_EXPERTS * RUN_ROWS
    return pl.pallas_call(
        functools.partial(_combine_kernel, tile_base=tile_base),
        grid_spec=pltpu.PrefetchScalarGridSpec(
            num_scalar_prefetch=3,
            grid=(n // tile,),
            in_specs=[pl.BlockSpec((tile, 2 * TOP_K), lambda i, *_: (i, 0)),
                      pl.BlockSpec((tile, D_MODEL), lambda i, *_: (i, 0)),
                      pl.BlockSpec((1, D_MODEL), lambda i, *_: (0, 0)),
                      pl.BlockSpec((1, D_MODEL), lambda i, *_: (0, 0)),
                      pl.BlockSpec(memory_space=pl.ANY)],
            out_specs=pl.BlockSpec((tile, D_MODEL), lambda i, *_: (i, 0)),
            scratch_shapes=[pltpu.VMEM((2, loc, D_MODEL), F32), pltpu.SemaphoreType.DMA((2,))]),
        out_shape=jax.ShapeDtypeStruct((n, D_MODEL), F32),
        compiler_params=_cparams(("arbitrary",), VMEM_LIMIT),
        name="combine",
    )(tab, seg_start, tot, cols, x1, g2, b2, ys)


def kernel(x_prompt, x_sample, cache_k_win, cache_v_win, state_ssm_re, state_ssm_im, w_in, b_in, attn_sinks,
           w_attn_out, ssm_a_re, ssm_a_im, ssm_log_dt, ssm_b_re, ssm_b_im, ssm_c_re, ssm_c_im, ssm_d, w_ssm_out,
           w_gate, b_gate, w_out, ln1_g, ln1_b, w_router, b_router, w_exp1, b_exp1, w_exp2, b_exp2, ln2_g, ln2_b):
    assert w_in.shape[0] == DEPTH == 1
    bsz, seq, _ = x_prompt.shape
    nsamp = x_sample.shape[0]
    assert x_sample.shape[1] == 1
    n_p = bsz * seq
    n_tok = n_p + nsamp

    xp = x_prompt.reshape(n_p, D_MODEL)
    xsm = x_sample.reshape(nsamp, D_MODEL)
    b_in2 = b_in[0].reshape(1, D_IN)
    sinks = attn_sinks[0].astype(F32)

    q_p, k_p, v_p, u_p = _proj(xp, w_in[0].astype(BF16), b_in2, tile=512, exact_f32=False, q_dtype=BF16)
    q_s, k_s, v_s, u_s = _proj(xsm, w_in[0], b_in2, tile=nsamp, exact_f32=True, q_dtype=F32)

    o_p = _attn_prompt(sinks, q_p.reshape(bsz, seq, D_ATTN), k_p.reshape(bsz, seq, D_KV),
                       v_p.reshape(bsz, seq, D_KV)).reshape(n_p, D_ATTN)
    k_buf = cache_k_win[0].reshape(nsamp, WINDOW, D_KV)
    v_buf = cache_v_win[0].reshape(nsamp, WINDOW, D_KV)
    o_s, k_next, v_next = _attn_sample(sinks, q_s, k_s, v_s, k_buf, v_buf)

    sp = _s5_params(ssm_a_re[0], ssm_a_im[0], ssm_log_dt[0], ssm_b_re[0], ssm_b_im[0], ssm_c_re[0], ssm_c_im[0])
    y_p, hp_re, hp_im = _s5_prompt(u_p, bsz, seq, _s5_chunk_mats(sp, ssm_d[0]))
    y_s, hs_re, hs_im = _s5_sample(u_s, state_ssm_re[0].reshape(nsamp, -1), state_ssm_im[0].reshape(nsamp, -1),
                                   _s5_sample_mats(sp, ssm_d[0]))

    wm = dict(wao=w_attn_out[0].astype(BF16), wso=w_ssm_out[0].astype(BF16), wg=w_gate[0].astype(BF16),
              bg=b_gate[0].reshape(1, -1), wo=w_out[0].astype(BF16), g1=ln1_g[0].reshape(1, -1),
              b1=ln1_b[0].reshape(1, -1), wrt=w_router[0].T, brt=b_router[0].reshape(-1, 1))
    wm_f32 = dict(wm, wao=w_attn_out[0], wso=w_ssm_out[0], wg=w_gate[0], wo=w_out[0])
    carry0 = jnp.zeros((SUBLANES, LANES), F32)
    x1_p, lpos_p, cols_p, tab_p, carry1 = _merge(xp, o_p, y_p, carry0, wm, tile=MERGE_TILE, route_tile=TOK_TILE,
                                                 f32_matmuls=False)
    x1_s, lpos_s, cols_s, tab_s, carry2 = _merge(xsm, o_s, y_s, carry1, wm_f32, tile=nsamp, route_tile=nsamp,
                                                 f32_matmuls=True)

    nt_p = n_p // TOK_TILE
    tab = jnp.concatenate([tab_p[:, :TAB_ROWS, :N_EXPERTS], tab_s[:, :TAB_ROWS, :N_EXPERTS]], axis=0)
    tot = jnp.sum(tab[:, 0, :], axis=1).astype(jnp.int32)
    tab = tab.reshape(-1)
    seg_rows = carry2[0, :N_EXPERTS].astype(jnp.int32) * RUN_ROWS
    step_rows = MOE_STEP_BLOCKS * MOE_ROWS
    padded = ((seg_rows + step_rows - 1) // step_rows) * step_rows
    pad_end = jnp.cumsum(padded)
    pad_start = (pad_end - padded).astype(jnp.int32)
    seg_end = pad_start + seg_rows
    n_runs = (nt_p + 1) * N_EXPERTS
    nb_max = (n_tok * TOP_K + n_runs * (RUN_ROWS - 1) + N_EXPERTS * (step_rows - 1) + step_rows - 1) // step_rows
    n_used = (pad_end[-1] // step_rows).astype(jnp.int32)
    tails = jnp.concatenate([seg_end // RUN_ROWS, (padded - seg_rows) // RUN_ROWS,
                             (pad_end[-1:] // MOE_ROWS)]).astype(jnp.int32)
    blk_start = jnp.arange(nb_max, dtype=jnp.int32) * step_rows
    blk_e = jnp.minimum(jnp.sum(blk_start[:, None] >= pad_end[None, :], axis=1), N_EXPERTS - 1).astype(jnp.int32)
    used = jnp.arange(nb_max) < n_used
    blk_e = jnp.where(used, blk_e, jnp.max(jnp.where(used, blk_e, 0)))
    ids = jnp.arange(N_EXPERTS, dtype=jnp.int32)
    of_blk = blk_e[:, None] == ids[None, :]
    n_valid = jnp.clip((jnp.sum(jnp.where(of_blk, seg_end[None, :], 0), axis=1) - blk_start + MOE_ROWS - 1)
                       // MOE_ROWS, 0, MOE_STEP_BLOCKS)
    n_valid = jnp.where(used, n_valid, 0).astype(jnp.int32)
    new_run = jnp.concatenate([jnp.ones((1,), jnp.int32), (blk_e[1:] != blk_e[:-1]).astype(jnp.int32)])
    run_ord = (jnp.cumsum(new_run) - 1).astype(jnp.int32)
    later = (ids[None, :] > ids[:, None]) & (padded > 0)[None, :]
    next_e = jnp.min(jnp.where(later, ids[None, :], N_EXPERTS), axis=1)
    next_e = jnp.where(next_e < N_EXPERTS, next_e, -1).astype(jnp.int32)
    run_next = jnp.sum(jnp.where(of_blk, next_e[None, :], 0), axis=1).astype(jnp.int32)

    nrows = nb_max * step_rows
    xs = _dispatch(tab, pad_start, tot, tails, lpos_p, x1_p, lpos_s, x1_s, tile=TOK_TILE, nrows=nrows)

    b1p = b_exp1[0].reshape(N_EXPERTS, 2 * D_FF // MXU_DIM, MXU_DIM // 2, 2)
    b1p = jnp.swapaxes(b1p, 2, 3).reshape(N_EXPERTS, 1, 2 * D_FF)
    ys = _experts(blk_e, n_used.reshape(1), n_valid, run_ord, run_next, xs, w_exp1[0], b1p, w_exp2[0],
                  b_exp2[0].reshape(N_EXPERTS, 1, D_MODEL),
                  jnp.asarray(_deinterleave_matrix(), BF16))

    g2, b2 = ln2_g[0].reshape(1, -1), ln2_b[0].reshape(1, -1)
    y_prompt = _combine(tab, pad_start, tot, cols_p, x1_p, g2, b2, ys, tile=TOK_TILE, tile_base=0)
    y_sample = _combine(tab, pad_start, tot, cols_s, x1_s, g2, b2, ys, tile=nsamp, tile_base=nt_p)

    k_p4 = k_p.reshape(bsz, seq, D_KV)[:, -WINDOW:].reshape(bsz, WINDOW, N_KV_HEADS, HEAD_DIM)
    v_p4 = v_p.reshape(bsz, seq, D_KV)[:, -WINDOW:].reshape(bsz, WINDOW, N_KV_HEADS, HEAD_DIM)
    k_s4 = k_next.reshape(nsamp, WINDOW, N_KV_HEADS, HEAD_DIM)
    v_s4 = v_next.reshape(nsamp, WINDOW, N_KV_HEADS, HEAD_DIM)
    st = lambda a, n: a.reshape(1, n, N_SSM_GROUPS, SSM_STATE)
    return (y_prompt.reshape(bsz, seq, D_MODEL), y_sample.reshape(nsamp, 1, D_MODEL),
            k_p4[None], v_p4[None], st(hp_re, bsz), st(hp_im, bsz),
            k_s4[None], v_s4[None], st(hs_re, nsamp), st(hs_im, nsamp))
```

```python
import functools

import numpy as np
import jax
import jax.numpy as jnp
from jax import lax
from jax.experimental import pallas as pl
from jax.experimental.pallas import tpu as pltpu

F32 = jnp.float32
BF16 = jnp.bfloat16

D_MODEL = 1024
HEAD_DIM = 64
N_Q_HEADS = 8
N_KV_HEADS = 2
Q_PER_KV = N_Q_HEADS // N_KV_HEADS
D_ATTN = N_Q_HEADS * HEAD_DIM
D_KV = N_KV_HEADS * HEAD_DIM
WINDOW = 128
ATTN_SCALE = HEAD_DIM ** -0.5
SSM_GROUP = 16
D_SSM = D_MODEL // 2
N_SSM_GROUPS = D_SSM // SSM_GROUP
SSM_STATE = 64
D_IN = D_ATTN + 2 * D_KV + D_SSM
N_EXPERTS = 32
TOP_K = 4
D_FF = D_MODEL
SWIGLU_LIMIT = 7.0
SWIGLU_ALPHA = 1.702
LN_EPS = 1e-5
DEPTH = 1
DEEPNORM_ALPHA = (2 * DEPTH) ** 0.25

LANES = 128
SUBLANES = 8
MXU_DIM = 256

S5_CHUNK = MXU_DIM // SSM_GROUP
S5_LANE_GROUPS = LANES // SSM_GROUP
MOE_ROWS = 256
MOE_STEP_BLOCKS = 2
TOK_TILE = 256
MERGE_TILE = 512
VMEM_LIMIT = 48 * 1024 * 1024
VMEM_LIMIT_LARGE = 56 * 1024 * 1024


def _cparams(sem, vmem=None):
    return pltpu.CompilerParams(dimension_semantics=sem, vmem_limit_bytes=vmem)


def _split_bf16(v):
    hi = v.astype(BF16)
    return hi, (v - hi.astype(F32)).astype(BF16)


def _dot_split(a, b, dims=(((1,), (0,)), ((), ()))):
    a_hi, a_lo = _split_bf16(a)
    b_hi, b_lo = _split_bf16(b)
    dot = lambda p, q: lax.dot_general(p, q, dims, preferred_element_type=F32)
    return dot(a_hi, b_hi) + dot(a_hi, b_lo) + dot(a_lo, b_hi)


def _proj_kernel(x_ref, w_ref, b_ref, q_ref, k_ref, v_ref, u_ref, *, exact_f32):
    if exact_f32:
        h = jnp.dot(x_ref[...], w_ref[...], preferred_element_type=F32, precision=lax.Precision.HIGHEST)
    else:
        h = jnp.dot(x_ref[...].astype(BF16), w_ref[...], preferred_element_type=F32)
    h = h + b_ref[...]
    q_ref[...] = (h[:, :D_ATTN] * ATTN_SCALE).astype(q_ref.dtype)
    k_ref[...] = h[:, D_ATTN:D_ATTN + D_KV]
    v_ref[...] = h[:, D_ATTN + D_KV:D_ATTN + 2 * D_KV]
    u_ref[...] = h[:, D_ATTN + 2 * D_KV:].astype(u_ref.dtype)


def _proj(x, w, b, *, tile, exact_f32, q_dtype):
    n = x.shape[0]
    return pl.pallas_call(
        functools.partial(_proj_kernel, exact_f32=exact_f32),
        grid=(n // tile,),
        in_specs=[pl.BlockSpec((tile, D_MODEL), lambda i: (i, 0)),
                  pl.BlockSpec((D_MODEL, D_IN), lambda i: (0, 0)),
                  pl.BlockSpec((1, D_IN), lambda i: (0, 0))],
        out_specs=[pl.BlockSpec((tile, D_ATTN), lambda i: (i, 0)),
                   pl.BlockSpec((tile, D_KV), lambda i: (i, 0)),
                   pl.BlockSpec((tile, D_KV), lambda i: (i, 0)),
                   pl.BlockSpec((tile, D_SSM), lambda i: (i, 0))],
        out_shape=[jax.ShapeDtypeStruct((n, D_ATTN), q_dtype),
                   jax.ShapeDtypeStruct((n, D_KV), F32),
                   jax.ShapeDtypeStruct((n, D_KV), F32),
                   jax.ShapeDtypeStruct((n, D_SSM), F32)],
        compiler_params=_cparams(("parallel",)),
        name="proj",
    )(x, w, b)


ATT_Q_TILE = 512


def _attn_prompt_kernel(sink_ref, q_ref, k_ref, v_ref, o_ref):
    i = pl.program_id(1)
    nk, nq = 2 * WINDOW, 2 * WINDOW
    lo = lax.broadcasted_iota(jnp.int32, (nk, LANES), 1) < HEAD_DIM
    top = lax.broadcasted_iota(jnp.int32, (nq, 1), 0) < WINDOW
    for blk in range(ATT_Q_TILE // WINDOW):
        q0 = i * ATT_Q_TILE + blk * WINDOW
        k0 = pl.multiple_of(jnp.maximum(q0 - WINDOW, 0), WINDOW)
        kk = k_ref[0, pl.ds(k0, nk), :]
        vv = v_ref[0, pl.ds(k0, nk), :]
        kk_sw = pltpu.roll(kk, HEAD_DIM, axis=1)
        vv_sw = pltpu.roll(vv, HEAD_DIM, axis=1)
        k_var = [[jnp.where(lo, kk, 0.0).astype(BF16), jnp.where(lo, 0.0, kk_sw).astype(BF16)],
                 [jnp.where(lo, kk_sw, 0.0).astype(BF16), jnp.where(lo, 0.0, kk).astype(BF16)]]
        v_var = [[jnp.where(lo, vv, 1.0).astype(BF16), jnp.where(lo, 1.0, vv_sw).astype(BF16)],
                 [jnp.where(lo, vv_sw, 1.0).astype(BF16), jnp.where(lo, 1.0, vv).astype(BF16)]]
        qpos = q0 + lax.broadcasted_iota(jnp.int32, (nq, nk), 0) % WINDOW
        kpos = k0 + lax.broadcasted_iota(jnp.int32, (nq, nk), 1)
        valid = (kpos <= qpos) & (qpos - kpos <= WINDOW)
        rows = slice(blk * WINDOW, (blk + 1) * WINDOW)
        for kv in range(N_KV_HEADS):
            pairs = (2 * kv, 2 * kv + 1)
            qs = jnp.concatenate([q_ref[0, rows, pr * LANES:(pr + 1) * LANES] for pr in pairs], axis=0)
            outs = []
            for parity in range(2):
                sink = jnp.where(top, sink_ref[2 * pairs[0] + parity], sink_ref[2 * pairs[1] + parity])
                s = lax.dot_general(qs, k_var[kv][parity], (((1,), (1,)), ((), ())), preferred_element_type=F32)
                s = jnp.where(valid, s, -jnp.inf)
                m = jnp.maximum(jnp.max(s, axis=-1, keepdims=True), sink)
                p = jnp.exp(s - m).astype(BF16)
                acc = jnp.dot(p, v_var[kv][parity], preferred_element_type=F32)
                outs.append(acc / (pltpu.roll(acc, HEAD_DIM, axis=1) + jnp.exp(sink - m)))
            o = jnp.where(lo, outs[0], outs[1]).astype(o_ref.dtype)
            for j, pr in enumerate(pairs):
                o_ref[0, rows, pr * LANES:(pr + 1) * LANES] = o[j * WINDOW:(j + 1) * WINDOW]


def _attn_prompt(sinks, q, k, v):
    bsz, seq = q.shape[0], q.shape[1]
    return pl.pallas_call(
        _attn_prompt_kernel,
        grid=(bsz, seq // ATT_Q_TILE),
        in_specs=[pl.BlockSpec(memory_space=pltpu.SMEM),
                  pl.BlockSpec((1, ATT_Q_TILE, D_ATTN), lambda b, i: (b, i, 0)),
                  pl.BlockSpec((1, seq, D_KV), lambda b, i: (b, 0, 0)),
                  pl.BlockSpec((1, seq, D_KV), lambda b, i: (b, 0, 0))],
        out_specs=pl.BlockSpec((1, ATT_Q_TILE, D_ATTN), lambda b, i: (b, i, 0)),
        out_shape=jax.ShapeDtypeStruct((bsz, seq, D_ATTN), BF16),
        compiler_params=_cparams(("parallel", "parallel")),
        name="attn_prompt",
    )(sinks, q, k, v)


ATT_S_GROUP = 16


def _attn_sample_kernel(sink_ref, q_ref, kn_ref, vn_ref, kb_ref, vb_ref, o_ref, knext_ref, vnext_ref):
    g = ATT_S_GROUP
    rows = Q_PER_KV * g
    ncol = g * WINDOW
    for buf_ref, new_ref, next_ref in ((kb_ref, kn_ref, knext_ref), (vb_ref, vn_ref, vnext_ref)):
        next_ref[:, 0:WINDOW - 1, :] = buf_ref[:, 1:WINDOW, :]
        next_ref[:, WINDOW - 1, :] = new_ref[...]
    kb = kb_ref[...].reshape(ncol, D_KV)
    vb = vb_ref[...].reshape(ncol, D_KV)
    rseq = lax.broadcasted_iota(jnp.int32, (rows, ncol), 0) % g
    cseq = lax.broadcasted_iota(jnp.int32, (rows, ncol), 1) // WINDOW
    own = rseq == cseq
    rhead = lax.broadcasted_iota(jnp.int32, (rows, 1), 0) // g
    for kv in range(N_KV_HEADS):
        lo = kv * HEAD_DIM
        qs = jnp.concatenate(
            [q_ref[:, (kv * Q_PER_KV + h) * HEAD_DIM:(kv * Q_PER_KV + h + 1) * HEAD_DIM] for h in range(Q_PER_KV)],
            axis=0)
        kn = jnp.concatenate([kn_ref[:, lo:lo + HEAD_DIM]] * Q_PER_KV, axis=0)
        vn = jnp.concatenate([vn_ref[:, lo:lo + HEAD_DIM]] * Q_PER_KV, axis=0)
        sink = jnp.zeros((rows, 1), F32)
        for h in range(Q_PER_KV):
            sink = jnp.where(rhead == h, sink_ref[kv * Q_PER_KV + h], sink)
        qs = qs.astype(F32)
        s = _dot_split(qs, kb[:, lo:lo + HEAD_DIM], (((1,), (1,)), ((), ())))
        s = jnp.where(own, s, -jnp.inf)
        s_new = jnp.sum(qs * kn, axis=-1, keepdims=True)
        m = jnp.maximum(jnp.maximum(jnp.max(s, axis=-1, keepdims=True), s_new), sink)
        p = jnp.exp(s - m)
        p_new = jnp.exp(s_new - m)
        denom = jnp.sum(p, axis=-1, keepdims=True) + p_new + jnp.exp(sink - m)
        o = (_dot_split(p, vb[:, lo:lo + HEAD_DIM]) + p_new * vn) / denom
        for h in range(Q_PER_KV):
            c0 = (kv * Q_PER_KV + h) * HEAD_DIM
            o_ref[:, c0:c0 + HEAD_DIM] = o[h * g:(h + 1) * g].astype(o_ref.dtype)


def _attn_sample(sinks, q, k_new, v_new, k_buf, v_buf):
    n = q.shape[0]
    g = ATT_S_GROUP
    return pl.pallas_call(
        _attn_sample_kernel,
        grid=(n // g,),
        in_specs=[pl.BlockSpec(memory_space=pltpu.SMEM),
                  pl.BlockSpec((g, D_ATTN), lambda i: (i, 0)),
                  pl.BlockSpec((g, D_KV), lambda i: (i, 0)),
                  pl.BlockSpec((g, D_KV), lambda i: (i, 0)),
                  pl.BlockSpec((g, WINDOW, D_KV), lambda i: (i, 0, 0)),
                  pl.BlockSpec((g, WINDOW, D_KV), lambda i: (i, 0, 0))],
        out_specs=[pl.BlockSpec((g, D_ATTN), lambda i: (i, 0)),
                   pl.BlockSpec((g, WINDOW, D_KV), lambda i: (i, 0, 0)),
                   pl.BlockSpec((g, WINDOW, D_KV), lambda i: (i, 0, 0))],
        out_shape=[jax.ShapeDtypeStruct((n, D_ATTN), F32),
                   jax.ShapeDtypeStruct((n, WINDOW, D_KV), F32),
                   jax.ShapeDtypeStruct((n, WINDOW, D_KV), F32)],
        compiler_params=_cparams(("parallel",)),
        name="attn_sample",
    )(sinks, q, k_new, v_new, k_buf, v_buf)


def _s5_params(a_re, a_im, log_dt, b_re, b_im, c_re, c_im):
    hp = lax.Precision.HIGHEST
    dt = jnp.exp(log_dt.astype(F32))[:, None]
    are, aim = a_re.astype(F32), a_im.astype(F32)
    tau = jnp.arange(S5_CHUNK + 1, dtype=F32)[None, :, None]
    mag = jnp.exp(tau * (dt * are)[:, None, :])
    ang = tau * (dt * aim)[:, None, :]
    pw_re, pw_im = mag * jnp.cos(ang), mag * jnp.sin(ang)
    ab_re, ab_im = pw_re[:, 1], pw_im[:, 1]
    den = are * are + aim * aim
    f_re = ((ab_re - 1.0) * are + ab_im * aim) / den
    f_im = (ab_im * are - (ab_re - 1.0) * aim) / den
    bre, bim = b_re.astype(F32), b_im.astype(F32)
    bb_re = f_re[..., None] * bre - f_im[..., None] * bim
    bb_im = f_re[..., None] * bim + f_im[..., None] * bre
    cre, cim = c_re.astype(F32), c_im.astype(F32)
    return dict(pw_re=pw_re, pw_im=pw_im, ab_re=ab_re, ab_im=ab_im, bb_re=bb_re, bb_im=bb_im,
                c_re=cre, c_im=cim, hp=hp)


def _s5_chunk_mats(sp, d_skip):
    hp = sp["hp"]
    g, t, c, p = N_SSM_GROUPS, S5_CHUNK, SSM_GROUP, SSM_STATE
    pw_re, pw_im = sp["pw_re"], sp["pw_im"]
    ca_re = sp["c_re"][:, None] * pw_re[:, :, None, :] - sp["c_im"][:, None] * pw_im[:, :, None, :]
    ca_im = sp["c_re"][:, None] * pw_im[:, :, None, :] + sp["c_im"][:, None] * pw_re[:, :, None, :]
    kern = (jnp.einsum("gtcp,gpd->gtcd", ca_re[:, :t], sp["bb_re"], precision=hp)
            - jnp.einsum("gtcp,gpd->gtcd", ca_im[:, :t], sp["bb_im"], precision=hp))
    kc = jnp.swapaxes(kern, 2, 3)
    kc = kc.at[:, 0].add(d_skip.astype(F32).reshape(g, 1, c) * jnp.eye(c, dtype=F32)[None])
    rev_re, rev_im = pw_re[:, t - 1::-1][:, :t], pw_im[:, t - 1::-1][:, :t]
    wst_re = rev_re[:, :, None, :] * jnp.swapaxes(sp["bb_re"], 1, 2)[:, None] \
        - rev_im[:, :, None, :] * jnp.swapaxes(sp["bb_im"], 1, 2)[:, None]
    wst_im = rev_re[:, :, None, :] * jnp.swapaxes(sp["bb_im"], 1, 2)[:, None] \
        + rev_im[:, :, None, :] * jnp.swapaxes(sp["bb_re"], 1, 2)[:, None]
    wo_re = jnp.transpose(ca_re[:, 1:t + 1], (0, 3, 1, 2))
    wo_im = -jnp.transpose(ca_im[:, 1:t + 1], (0, 3, 1, 2))
    nv, gl = g // S5_LANE_GROUPS, S5_LANE_GROUPS
    kc, wst_re, wst_im, wo_re, wo_im = lax.optimization_barrier((kc, wst_re, wst_im, wo_re, wo_im))
    kc_c = kc.astype(BF16).reshape(nv, gl * t * c, c)
    ws_c = jnp.concatenate([wst_re, wst_im], axis=-1).astype(BF16).reshape(nv, gl * t * c, 2 * p)
    wo_re_c = wo_re.astype(BF16).reshape(nv, gl * p, t * c)
    wo_im_c = wo_im.astype(BF16).reshape(nv, gl * p, t * c)
    spread_b = np.zeros((c, LANES), np.float32)
    spread_s = np.zeros((2 * p, 2 * gl * p), np.float32)
    spread_o = np.zeros((t * c, t * LANES), np.float32)
    for h in range(gl):
        spread_b[np.arange(c), h * c + np.arange(c)] = 1.0
        for ri in range(2):
            spread_s[ri * p + np.arange(p), ri * gl * p + h * p + np.arange(p)] = 1.0
        for tt in range(t):
            spread_o[tt * c + np.arange(c), tt * LANES + h * c + np.arange(c)] = 1.0
    at_re = pw_re[:, t].reshape(1, g * p)
    at_im = pw_im[:, t].reshape(1, g * p)
    return dict(kc=kc_c, ws=ws_c, wo_re=wo_re_c, wo_im=wo_im_c, spread_b=jnp.asarray(spread_b, BF16),
                spread_s=jnp.asarray(spread_s, BF16), spread_o=jnp.asarray(spread_o, BF16),
                at_re=at_re, at_im=at_im)


def _s5_chunk_rows(u_ref, nchunk):
    return jnp.concatenate(
        [u_ref[pl.ds(s, nchunk, stride=S5_CHUNK), :] for s in range(S5_CHUNK)], axis=1).astype(BF16)


S5_SLABS = S5_LANE_GROUPS * SSM_STATE // LANES


S5_EXPAND_ROWS = 256
S5_C_SHIFT = SSM_GROUP.bit_length() - 1
S5_P_SHIFT = SSM_STATE.bit_length() - 1


def _s5_expand(dst_ref, compact_rows, spread_ref, row_shift, col_shift):
    n_rows, n_cols = dst_ref.shape
    col_g = lax.shift_right_logical(lax.broadcasted_iota(jnp.int32, (S5_EXPAND_ROWS, n_cols), 1), col_shift)
    for r0 in range(0, n_rows, S5_EXPAND_ROWS):
        row_g = lax.shift_right_logical(r0 + lax.broadcasted_iota(jnp.int32, (S5_EXPAND_ROWS, n_cols), 0), row_shift)
        same = ((row_g ^ col_g) & (S5_LANE_GROUPS - 1)) == 0
        blk = jnp.dot(compact_rows(r0), spread_ref[...], preferred_element_type=F32)
        dst_ref[r0:r0 + S5_EXPAND_ROWS, :] = jnp.where(same, blk, 0.0).astype(dst_ref.dtype)


def _s5_group_rows(ref, index):
    per_group = S5_CHUNK * SSM_GROUP
    return jnp.concatenate([ref[0, g * per_group + index * SSM_GROUP:g * per_group + (index + 1) * SSM_GROUP, :]
                            for g in range(S5_LANE_GROUPS)], axis=0)


def _s5_state_kernel(u_ref, ws_ref, spread_ref, sre_ref, sim_ref, wst_sc):
    nchunk = sre_ref.shape[1]

    @pl.when(pl.program_id(1) == 0)
    def _():
        per_call = S5_EXPAND_ROWS // LANES
        _s5_expand(wst_sc,
                   lambda r0: jnp.concatenate([_s5_group_rows(ws_ref, r0 // LANES + j) for j in range(per_call)], axis=0),
                   spread_ref, S5_C_SHIFT, S5_P_SHIFT)

    s = jnp.dot(_s5_chunk_rows(u_ref, nchunk), wst_sc[...], preferred_element_type=F32)
    for k in range(S5_SLABS):
        sre_ref[k] = s[:, k * LANES:(k + 1) * LANES]
        sim_ref[k] = s[:, (S5_SLABS + k) * LANES:(S5_SLABS + k + 1) * LANES]


def _s5_scan_kernel(sre_ref, sim_ref, are_ref, aim_ref, hre_ref, him_ref, fre_ref, fim_ref, *, bsz):
    nchunk = sre_ref.shape[1] // bsz
    are = [jnp.broadcast_to(are_ref[:, k * LANES:(k + 1) * LANES], (bsz, LANES)) for k in range(S5_SLABS)]
    aim = [jnp.broadcast_to(aim_ref[:, k * LANES:(k + 1) * LANES], (bsz, LANES)) for k in range(S5_SLABS)]

    def body(j, carry):
        rows = pl.ds(j, bsz, stride=nchunk)
        out = []
        for k in range(S5_SLABS):
            cre, cim = carry[2 * k], carry[2 * k + 1]
            hre_ref[k, rows, :] = cre
            him_ref[k, rows, :] = cim
            sr = sre_ref[k, rows, :]
            si = sim_ref[k, rows, :]
            out += [are[k] * cre - aim[k] * cim + sr, are[k] * cim + aim[k] * cre + si]
        return tuple(out)

    zero = jnp.zeros((bsz, LANES), F32)
    fin = lax.fori_loop(0, nchunk, body, (zero,) * (2 * S5_SLABS), unroll=4)
    fre_ref[...] = jnp.concatenate(fin[0::2], axis=1)
    fim_ref[...] = jnp.concatenate(fin[1::2], axis=1)


def _s5_out_kernel(u_ref, kc_ref, spread_b_ref, hre_ref, him_ref, wo_re_ref, wo_im_ref, spread_o_ref, y_ref,
                   m_sc, wout_sc):
    nchunk = hre_ref.shape[1]

    @pl.when(pl.program_id(1) == 0)
    def _():
        rg = lax.shift_right_logical(lax.broadcasted_iota(jnp.int32, (LANES, LANES), 0), S5_C_SHIFT)
        cg = lax.shift_right_logical(lax.broadcasted_iota(jnp.int32, (LANES, LANES), 1), S5_C_SHIFT)
        zero_blk = jnp.zeros((LANES, LANES), BF16)
        lag_blk = [jnp.where(rg == cg, jnp.dot(_s5_group_rows(kc_ref, tau), spread_b_ref[...],
                                               preferred_element_type=F32), 0.0).astype(BF16)
                   for tau in range(S5_CHUNK)]
        for s in range(S5_CHUNK):
            for t in range(S5_CHUNK):
                m_sc[s * LANES:(s + 1) * LANES, t * LANES:(t + 1) * LANES] = lag_blk[t - s] if t >= s else zero_blk
        half = wo_re_ref.shape[1]
        _s5_expand(wout_sc,
                   lambda r0: (wo_re_ref[0, r0:r0 + S5_EXPAND_ROWS, :] if r0 < half
                               else wo_im_ref[0, r0 - half:r0 - half + S5_EXPAND_ROWS, :]),
                   spread_o_ref, S5_P_SHIFT, S5_C_SHIFT)

    hcat = jnp.concatenate([hre_ref[k] for k in range(S5_SLABS)] + [him_ref[k] for k in range(S5_SLABS)],
                           axis=1).astype(BF16)
    lhs = _s5_chunk_rows(u_ref, nchunk)
    y = jnp.concatenate(
        [jnp.dot(lhs[:, :j + MXU_DIM], m_sc[:j + MXU_DIM, j:j + MXU_DIM], preferred_element_type=F32)
         for j in range(0, S5_CHUNK * LANES, MXU_DIM)], axis=1)
    y = y + jnp.dot(hcat, wout_sc[...], preferred_element_type=F32)
    for s in range(S5_CHUNK):
        y_ref[pl.ds(s, nchunk, stride=S5_CHUNK), :] = y[:, s * LANES:(s + 1) * LANES]


def _s5_prompt(u, bsz, seq, mats):
    at_re, at_im = mats["at_re"], mats["at_im"]
    g, t, p, c = N_SSM_GROUPS, S5_CHUNK, SSM_STATE, SSM_GROUP
    nchunk = seq // t
    n = nchunk * bsz
    nv = g // S5_LANE_GROUPS
    half = S5_LANE_GROUPS * p
    s_re, s_im = pl.pallas_call(
        _s5_state_kernel,
        grid=(nv, bsz),
        in_specs=[pl.BlockSpec((seq, LANES), lambda v, b: (b, v)),
                  pl.BlockSpec((1, t * LANES, 2 * p), lambda v, b: (v, 0, 0)),
                  pl.BlockSpec((2 * p, 2 * half), lambda v, b: (0, 0))],
        out_specs=[pl.BlockSpec((S5_SLABS, nchunk, LANES), lambda v, b: (v, b, 0)),
                   pl.BlockSpec((S5_SLABS, nchunk, LANES), lambda v, b: (v, b, 0))],
        out_shape=[jax.ShapeDtypeStruct((nv * S5_SLABS, n, LANES), F32)] * 2,
        scratch_shapes=[pltpu.VMEM((t * LANES, 2 * half), BF16)],
        compiler_params=_cparams(("parallel", "arbitrary"), VMEM_LIMIT),
        name="s5_state",
    )(u, mats["ws"], mats["spread_s"])
    h_re, h_im, f_re, f_im = pl.pallas_call(
        functools.partial(_s5_scan_kernel, bsz=bsz),
        grid=(nv,),
        in_specs=[pl.BlockSpec((S5_SLABS, n, LANES), lambda i: (i, 0, 0)),
                  pl.BlockSpec((S5_SLABS, n, LANES), lambda i: (i, 0, 0)),
                  pl.BlockSpec((1, half), lambda i: (0, i)),
                  pl.BlockSpec((1, half), lambda i: (0, i))],
        out_specs=[pl.BlockSpec((S5_SLABS, n, LANES), lambda i: (i, 0, 0)),
                   pl.BlockSpec((S5_SLABS, n, LANES), lambda i: (i, 0, 0)),
                   pl.BlockSpec((bsz, half), lambda i: (0, i)),
                   pl.BlockSpec((bsz, half), lambda i: (0, i))],
        out_shape=[jax.ShapeDtypeStruct((nv * S5_SLABS, n, LANES), F32)] * 2
        + [jax.ShapeDtypeStruct((bsz, g * p), F32)] * 2,
        compiler_params=_cparams(("parallel",)),
        name="s5_scan",
    )(s_re, s_im, at_re, at_im)
    y = pl.pallas_call(
        _s5_out_kernel,
        grid=(nv, bsz),
        in_specs=[pl.BlockSpec((seq, LANES), lambda v, b: (b, v)),
                  pl.BlockSpec((1, t * LANES, c), lambda v, b: (v, 0, 0)),
                  pl.BlockSpec((c, LANES), lambda v, b: (0, 0)),
                  pl.BlockSpec((S5_SLABS, nchunk, LANES), lambda v, b: (v, b, 0)),
                  pl.BlockSpec((S5_SLABS, nchunk, LANES), lambda v, b: (v, b, 0)),
                  pl.BlockSpec((1, half, t * c), lambda v, b: (v, 0, 0)),
                  pl.BlockSpec((1, half, t * c), lambda v, b: (v, 0, 0)),
                  pl.BlockSpec((t * c, t * LANES), lambda v, b: (0, 0))],
        out_specs=pl.BlockSpec((seq, LANES), lambda v, b: (b, v)),
        out_shape=jax.ShapeDtypeStruct((bsz * seq, D_SSM), F32),
        scratch_shapes=[pltpu.VMEM((t * LANES, t * LANES), BF16), pltpu.VMEM((2 * half, t * LANES), BF16)],
        compiler_params=_cparams(("parallel", "arbitrary"), VMEM_LIMIT),
        name="s5_out",
    )(u, mats["kc"], mats["spread_b"], h_re, h_im, mats["wo_re"], mats["wo_im"], mats["spread_o"])
    return y, f_re, f_im


S5S_GROUPS = LANES // SSM_GROUP


def _s5_sample_mats(sp, d_skip):
    go, gl, c, p = N_SSM_GROUPS // S5S_GROUPS, S5S_GROUPS, SSM_GROUP, SSM_STATE
    eye = jnp.eye(gl, dtype=F32)

    def bdiag_in(b):
        b4 = b.reshape(go, gl, p, c)
        return jnp.einsum("ogpc,gh->ogchp", b4, eye).reshape(go, gl * c, gl * p)

    def bdiag_out(cm):
        c4 = cm.reshape(go, gl, c, p)
        return jnp.einsum("ogcp,gh->ogphc", c4, eye).reshape(go, gl * p, gl * c)

    b8 = jnp.concatenate([bdiag_in(sp["bb_re"]), bdiag_in(sp["bb_im"])], axis=2)
    c8 = jnp.concatenate([bdiag_out(sp["c_re"]), -bdiag_out(sp["c_im"])], axis=1)
    a_re = sp["ab_re"].reshape(1, N_SSM_GROUPS * p)
    a_im = sp["ab_im"].reshape(1, N_SSM_GROUPS * p)
    return b8, c8, a_re, a_im, d_skip.astype(F32).reshape(1, D_SSM)


def _s5_sample_kernel(u_ref, hre_ref, him_ref, b8_ref, c8_ref, are_ref, aim_ref, d_ref,
                      y_ref, ore_ref, oim_ref):
    hp = lax.Precision.HIGHEST
    u = u_ref[...]
    half = S5S_GROUPS * SSM_STATE
    bu = jnp.dot(u, b8_ref[0], preferred_element_type=F32, precision=hp)
    are, aim = are_ref[...], aim_ref[...]
    h0r, h0i = hre_ref[...], him_ref[...]
    hr = are * h0r - aim * h0i + bu[:, :half]
    hi = are * h0i + aim * h0r + bu[:, half:]
    ore_ref[...] = hr
    oim_ref[...] = hi
    y = jnp.dot(jnp.concatenate([hr, hi], axis=1), c8_ref[0], preferred_element_type=F32, precision=hp)
    y_ref[...] = (y + d_ref[...] * u).astype(y_ref.dtype)


def _s5_sample(u, h0_re, h0_im, mats):
    b8, c8, a_re, a_im, d = mats
    n = u.shape[0]
    half = S5S_GROUPS * SSM_STATE
    return pl.pallas_call(
        _s5_sample_kernel,
        grid=(N_SSM_GROUPS // S5S_GROUPS,),
        in_specs=[pl.BlockSpec((n, LANES), lambda i: (0, i)),
                  pl.BlockSpec((n, half), lambda i: (0, i)),
                  pl.BlockSpec((n, half), lambda i: (0, i)),
                  pl.BlockSpec((1, LANES, 2 * half), lambda i: (i, 0, 0)),
                  pl.BlockSpec((1, 2 * half, LANES), lambda i: (i, 0, 0)),
                  pl.BlockSpec((1, half), lambda i: (0, i)),
                  pl.BlockSpec((1, half), lambda i: (0, i)),
                  pl.BlockSpec((1, LANES), lambda i: (0, i))],
        out_specs=[pl.BlockSpec((n, LANES), lambda i: (0, i)),
                   pl.BlockSpec((n, half), lambda i: (0, i)),
                   pl.BlockSpec((n, half), lambda i: (0, i))],
        out_shape=[jax.ShapeDtypeStruct((n, D_SSM), F32),
                   jax.ShapeDtypeStruct((n, N_SSM_GROUPS * SSM_STATE), F32),
                   jax.ShapeDtypeStruct((n, N_SSM_GROUPS * SSM_STATE), F32)],
        compiler_params=_cparams(("parallel",)),
        name="s5_sample",
    )(u, h0_re, h0_im, b8, c8, a_re, a_im, d)


def _layer_norm(x, g, b):
    mu = jnp.mean(x, axis=-1, keepdims=True)
    xc = x - mu
    var = jnp.mean(xc * xc, axis=-1, keepdims=True)
    return xc * lax.rsqrt(var + LN_EPS) * g + b


def _sigmoid(x):
    return 0.5 * jnp.tanh(0.5 * x) + 0.5


RUN_ROWS = SUBLANES
TAB_ROWS = 3


def _merge_kernel(x_ref, oa_ref, ys_ref, carry_in_ref, wao_ref, wso_ref, wg_ref, bg_ref, wo_ref,
                  g1_ref, b1_ref, wrt_ref, brt_ref,
                  x1_ref, lpos_ref, cols_ref, tab_ref, carry_out_ref, carry_sc, *, f32_matmuls):
    step = pl.program_id(0)

    @pl.when(step == 0)
    def _():
        carry_sc[...] = carry_in_ref[...]

    def mm(a, w_ref):
        if f32_matmuls:
            return _dot_split(a.astype(F32), w_ref[...])
        return jnp.dot(a.astype(BF16), w_ref[...], preferred_element_type=F32)

    tm = x_ref.shape[0]
    x = x_ref[...]
    branch_a = mm(oa_ref[...], wao_ref)
    z = mm(jax.nn.gelu(ys_ref[...].astype(F32)), wso_ref)
    branch_b = z[:, :D_MODEL] * _sigmoid(z[:, D_MODEL:])
    gates = _sigmoid(mm(x, wg_ref) + bg_ref[...])
    mixed = gates[:, :D_MODEL] * branch_a + gates[:, D_MODEL:] * branch_b
    mix = mm(mixed, wo_ref)
    x1 = _layer_norm(DEEPNORM_ALPHA * x + mix, g1_ref[...], b1_ref[...])
    x1_ref[...] = x1

    split2 = _split_bf16

    def dot_nt(a, b):
        return lax.dot_general(a, b, (((1,), (1,)), ((), ())), preferred_element_type=F32)

    w_hi, w_lo = split2(wrt_ref[...])
    rt = tm // tab_ref.shape[0]
    sub = lax.broadcasted_iota(jnp.int32, (N_EXPERTS, rt), 0)
    r = lax.broadcasted_iota(jnp.int32, (rt, rt), 0)
    c = lax.broadcasted_iota(jnp.int32, (rt, rt), 1)
    er = lax.broadcasted_iota(jnp.int32, (N_EXPERTS, N_EXPERTS), 0)
    ec = lax.broadcasted_iota(jnp.int32, (N_EXPERTS, N_EXPERTS), 1)
    rid = lax.broadcasted_iota(jnp.int32, (SUBLANES, LANES), 0)
    lane_pad = jnp.zeros((SUBLANES, LANES - N_EXPERTS), F32)
    for h in range(tab_ref.shape[0]):
        x_hi, x_lo = split2(x1[h * rt:(h + 1) * rt])
        logits = dot_nt(w_hi, x_hi) + dot_nt(w_hi, x_lo) + dot_nt(w_lo, x_hi) + brt_ref[...]
        work = logits
        vals, sels = [], []
        for _ in range(TOP_K):
            mx = jnp.max(work, axis=0, keepdims=True)
            idx = jnp.min(jnp.where(work == mx, sub, N_EXPERTS), axis=0, keepdims=True)
            sel = sub == idx
            vals.append(mx)
            sels.append(sel)
            work = jnp.where(sel, -jnp.inf, work)
        ex = [jnp.exp(v - vals[0]) for v in vals]
        tot = ex[0] + ex[1] + ex[2] + ex[3]
        gate_rows = jnp.concatenate([e / tot for e in ex], axis=0)

        multi = jnp.zeros((N_EXPERTS, rt), F32)
        for sel in sels:
            multi = multi + jnp.where(sel, 1.0, 0.0)
        multi_b = multi.astype(BF16)
        earlier = jnp.dot(multi_b, jnp.where(r < c, 1.0, 0.0).astype(BF16), preferred_element_type=F32)
        cnt_col = jnp.sum(multi, axis=1, keepdims=True)
        nb_col = jnp.floor((cnt_col + (RUN_ROWS - 1.0)) * (1.0 / RUN_ROWS))
        loff_col = jnp.dot(jnp.where(ec < er, 1.0, 0.0).astype(BF16),
                           jnp.broadcast_to(nb_col, (N_EXPERTS, rt)).astype(BF16), preferred_element_type=F32)
        base = RUN_ROWS * loff_col + earlier
        lpos = jnp.concatenate([jnp.sum(jnp.where(sel, base, 0.0), axis=0, keepdims=True) for sel in sels],
                               axis=0)
        lpos_ref[:, h * rt:(h + 1) * rt] = lpos.astype(jnp.int32)
        rows_hi, rows_lo = split2(jnp.concatenate([lpos, gate_rows], axis=0))
        eye = jnp.where(r == c, 1.0, 0.0).astype(BF16)
        cols_ref[h * rt:(h + 1) * rt, :] = dot_nt(eye, rows_hi) + dot_nt(eye, rows_lo)

        cnt_row = dot_nt(jnp.ones((SUBLANES, rt), BF16), multi_b)
        nb_row = jnp.floor((cnt_row + (RUN_ROWS - 1.0)) * (1.0 / RUN_ROWS))
        loff_row = jnp.dot(nb_row.astype(BF16), jnp.where(er < ec, 1.0, 0.0).astype(BF16),
                           preferred_element_type=F32)
        nb_p = jnp.concatenate([nb_row, lane_pad], axis=1)
        loff_p = jnp.concatenate([loff_row, lane_pad], axis=1)
        goff_p = carry_sc[...]
        tab = jnp.where(rid == 0, nb_p, jnp.where(rid == 1, loff_p, jnp.where(rid == 2, goff_p, 0.0)))
        tab_ref[h] = tab.astype(jnp.int32)
        carry_sc[...] = goff_p + nb_p
    carry_out_ref[...] = carry_sc[...]


def _merge(x, o_attn, y_ssm, carry_in, w, *, tile, route_tile, f32_matmuls):
    n = x.shape[0]
    nt = n // tile
    per_step = tile // route_tile
    full = lambda shape: pl.BlockSpec(shape, lambda i: (0,) * len(shape))
    return pl.pallas_call(
        functools.partial(_merge_kernel, f32_matmuls=f32_matmuls),
        grid=(nt,),
        in_specs=[pl.BlockSpec((tile, D_MODEL), lambda i: (i, 0)),
                  pl.BlockSpec((tile, D_ATTN), lambda i: (i, 0)),
                  pl.BlockSpec((tile, D_SSM), lambda i: (i, 0)),
                  full((SUBLANES, LANES)),
                  full((D_ATTN, D_MODEL)), full((D_SSM, 2 * D_MODEL)), full((D_MODEL, 2 * D_MODEL)),
                  full((1, 2 * D_MODEL)), full((D_MODEL, D_MODEL)),
                  full((1, D_MODEL)), full((1, D_MODEL)),
                  full((N_EXPERTS, D_MODEL)), full((N_EXPERTS, 1))],
        out_specs=[pl.BlockSpec((tile, D_MODEL), lambda i: (i, 0)),
                   pl.BlockSpec((TOP_K, tile), lambda i: (0, i)),
                   pl.BlockSpec((tile, 2 * TOP_K), lambda i: (i, 0)),
                   pl.BlockSpec((per_step, SUBLANES, LANES), lambda i: (i, 0, 0)),
                   full((SUBLANES, LANES))],
        out_shape=[jax.ShapeDtypeStruct((n, D_MODEL), F32),
                   jax.ShapeDtypeStruct((TOP_K, n), jnp.int32),
                   jax.ShapeDtypeStruct((n, 2 * TOP_K), F32),
                   jax.ShapeDtypeStruct((nt * per_step, SUBLANES, LANES), jnp.int32),
                   jax.ShapeDtypeStruct((SUBLANES, LANES), F32)],
        scratch_shapes=[pltpu.VMEM((SUBLANES, LANES), F32)],
        compiler_params=_cparams(("arbitrary",), VMEM_LIMIT_LARGE),
        name="merge",
    )(x, o_attn, y_ssm, carry_in, w["wao"], w["wso"], w["wg"], w["bg"], w["wo"], w["g1"], w["b1"],
      w["wrt"], w["brt"])


def _tab(tab_ref, tile, row, e):
    return tab_ref[(tile * TAB_ROWS + row) * N_EXPERTS + e]


BIG_PIECE = 4 * RUN_ROWS
MAX_UNITS_LOG2 = 8


def _for_each_run_piece(tab_ref, tile, fn):
    def per_expert(e, carry):
        loff = RUN_ROWS * _tab(tab_ref, tile, 1, e)
        goff = RUN_ROWS * _tab(tab_ref, tile, 2, e)
        units = _tab(tab_ref, tile, 0, e)
        n_big = lax.shift_right_logical(units, 2)

        def big(j, c2):
            fn(pl.multiple_of(loff + j * BIG_PIECE, RUN_ROWS), goff + j * BIG_PIECE, e, BIG_PIECE)
            return c2

        lax.fori_loop(0, n_big, big, 0)
        done = n_big * BIG_PIECE

        def small(j, c2):
            fn(pl.multiple_of(loff + done + j * RUN_ROWS, RUN_ROWS), goff + done + j * RUN_ROWS, e, RUN_ROWS)
            return c2

        lax.fori_loop(0, units & 3, small, 0)
        return carry

    lax.fori_loop(0, N_EXPERTS, per_expert, 0)


def _drain_units(units, wait_copy, buffer_rows):
    assert buffer_rows < (RUN_ROWS << MAX_UNITS_LOG2)
    for b in range(MAX_UNITS_LOG2):
        if (RUN_ROWS << b) > buffer_rows:
            break

        @pl.when((lax.shift_right_logical(units, b) & 1) == 1)
        def _():
            wait_copy(RUN_ROWS << b).wait()


def _dispatch_kernel(tab_ref, seg_ref, tot_ref, tail_ref, lpos_p_ref, xp_ref, lpos_s_ref, xs_in_ref, xs_ref,
                     loc_sc, zero_sc, sem, zsem):
    i = pl.program_id(0)
    last = pl.num_programs(0) - 1
    tile = i
    slot = i % 2
    loc = loc_sc.shape[1]

    @pl.when(i == 0)
    def _():
        zero_sc[...] = jnp.zeros_like(zero_sc)

        def tail_copy(e, j):
            row = pl.multiple_of(RUN_ROWS * (tail_ref[e] + j), RUN_ROWS)
            return pltpu.make_async_copy(zero_sc.at[pl.ds(0, RUN_ROWS)], xs_ref.at[pl.ds(row, RUN_ROWS)], zsem)

        def per_expert(e, carry):
            n = tail_ref[N_EXPERTS + e]
            lax.fori_loop(0, n, lambda j, c2: (tail_copy(e, j).start(), c2)[1], 0)
            lax.fori_loop(0, n, lambda j, c2: (tail_copy(e, j).wait(), c2)[1], 0)
            return carry

        lax.fori_loop(0, N_EXPERTS, per_expert, 0)

        def block_copy(b):
            row = pl.multiple_of(b * MOE_ROWS, MOE_ROWS)
            return pltpu.make_async_copy(zero_sc, xs_ref.at[pl.ds(row, MOE_ROWS)], zsem)

        first_unused, n_blocks = tail_ref[2 * N_EXPERTS], xs_ref.shape[0] // MOE_ROWS
        lax.fori_loop(first_unused, n_blocks, lambda b, c2: (block_copy(b).start(), c2)[1], 0)
        lax.fori_loop(first_unused, n_blocks, lambda b, c2: (block_copy(b).wait(), c2)[1], 0)

    def sort_tile(lpos_ref, x_ref):
        tm = x_ref.shape[0]
        rows = lax.broadcasted_iota(jnp.int32, (loc, tm), 0)
        lp = lpos_ref[...]
        onehot = jnp.zeros((loc, tm), F32)
        for k in range(TOP_K):
            onehot = jnp.where(rows == lp[k:k + 1], 1.0, onehot)
        loc_sc[slot] = jnp.dot(onehot.astype(BF16), x_ref[...].astype(BF16), preferred_element_type=F32)

    @pl.when(i < last)
    def _():
        sort_tile(lpos_p_ref, xp_ref)

    @pl.when(i == last)
    def _():
        sort_tile(lpos_s_ref, xs_in_ref)

    def piece_copy(sl, lrow, grow, e, n):
        dst = pl.multiple_of(seg_ref[e] + grow, RUN_ROWS)
        return pltpu.make_async_copy(loc_sc.at[sl, pl.ds(lrow, n)], xs_ref.at[pl.ds(dst, n)], sem.at[sl])

    _for_each_run_piece(tab_ref, tile, lambda lrow, grow, e, n: piece_copy(slot, lrow, grow, e, n).start())

    def drain(tl, sl):
        _drain_units(tot_ref[tl], lambda n: piece_copy(sl, 0, 0, 0, n), loc)

    @pl.when(i > 0)
    def _():
        drain(tile - 1, 1 - slot)

    @pl.when(i == last)
    def _():
        drain(tile, slot)


def _dispatch(tab, seg_start, tot, tails, lpos_p, x1_p, lpos_s, x1_s, *, tile, nrows):
    nt_p = x1_p.shape[0] // tile
    ns = x1_s.shape[0]
    loc = tile * TOP_K + N_EXPERTS * RUN_ROWS
    prompt_blk = lambda i, *_: jnp.minimum(i, nt_p - 1)
    return pl.pallas_call(
        _dispatch_kernel,
        grid_spec=pltpu.PrefetchScalarGridSpec(
            num_scalar_prefetch=4,
            grid=(nt_p + 1,),
            in_specs=[pl.BlockSpec((TOP_K, tile), lambda i, *_: (0, prompt_blk(i))),
                      pl.BlockSpec((tile, D_MODEL), lambda i, *_: (prompt_blk(i), 0)),
                      pl.BlockSpec((TOP_K, ns), lambda i, *_: (0, 0)),
                      pl.BlockSpec((ns, D_MODEL), lambda i, *_: (0, 0))],
            out_specs=pl.BlockSpec(memory_space=pl.ANY),
            scratch_shapes=[pltpu.VMEM((2, loc, D_MODEL), F32), pltpu.VMEM((MOE_ROWS, D_MODEL), F32),
                            pltpu.SemaphoreType.DMA((2,)), pltpu.SemaphoreType.DMA(())]),
        out_shape=jax.ShapeDtypeStruct((nrows, D_MODEL), F32),
        compiler_params=_cparams(("arbitrary",), VMEM_LIMIT),
        name="dispatch",
    )(tab, seg_start, tot, tails, lpos_p, x1_p, lpos_s, x1_s)


def _deinterleave_matrix():
    pm = np.zeros((MXU_DIM, MXU_DIM), np.float32)
    half = MXU_DIM // 2
    for c in range(half):
        pm[2 * c, c] = 1.0
        pm[2 * c + 1, half + c] = 1.0
    return pm


def _expert_kernel(be_ref, nu_ref, nv_ref, ord_ref, nxt_ref, xs_ref, w1_hbm, b1_ref, w2_hbm, b2_ref, pm_ref, y_ref,
                   w1f_sc, w2f_sc, w1p_sc, w2b_sc, sem):
    del nu_ref
    i = pl.program_id(0)
    e = be_ref[i]
    prev = be_ref[jnp.maximum(i - 1, 0)]
    nblk = 2 * D_FF // MXU_DIM

    def weight_copies(expert, slot):
        return (pltpu.make_async_copy(w1_hbm.at[expert], w1f_sc.at[slot], sem.at[0, slot]),
                pltpu.make_async_copy(w2_hbm.at[expert], w2f_sc.at[slot], sem.at[1, slot]))

    @pl.when(i == 0)
    def _():
        for cp in weight_copies(e, 0):
            cp.start()

    @pl.when((i == 0) | (e != prev))
    def _():
        slot = ord_ref[i] % 2
        for cp in weight_copies(e, slot):
            cp.wait()

        for cb in range(nblk):
            blk = w1f_sc[slot, :, cb * MXU_DIM:(cb + 1) * MXU_DIM].astype(BF16)
            w1p_sc[:, cb * MXU_DIM:(cb + 1) * MXU_DIM] = jnp.dot(
                blk, pm_ref[...], preferred_element_type=F32).astype(BF16)
        w2b_sc[...] = w2f_sc[slot].astype(BF16)

        nxt = nxt_ref[i]

        @pl.when(nxt >= 0)
        def _():
            for cp in weight_copies(nxt, 1 - slot):
                cp.start()

    for blk in range(MOE_STEP_BLOCKS):
        rows = slice(blk * MOE_ROWS, (blk + 1) * MOE_ROWS)

        @pl.when(blk < nv_ref[i])
        def _():
            x = xs_ref[rows, :].astype(BF16)
            h = jnp.dot(x, w1p_sc[...], preferred_element_type=F32) + b1_ref[0]
            half = MXU_DIM // 2
            acts = []
            for cb in range(nblk):
                x_glu = jnp.minimum(h[:, cb * MXU_DIM:cb * MXU_DIM + half], SWIGLU_LIMIT)
                x_lin = jnp.clip(h[:, cb * MXU_DIM + half:(cb + 1) * MXU_DIM], -SWIGLU_LIMIT, SWIGLU_LIMIT)
                acts.append((x_glu * jax.nn.sigmoid(SWIGLU_ALPHA * x_glu) * (x_lin + 1.0)).astype(BF16))
            act = jnp.concatenate(acts, axis=1)
            y_ref[rows, :] = jnp.dot(act, w2b_sc[...], preferred_element_type=F32) + b2_ref[0]

        @pl.when(blk >= nv_ref[i])
        def _():
            y_ref[rows, :] = jnp.zeros((MOE_ROWS, D_MODEL), F32)


def _experts(block_e, n_used, n_valid, run_ord, run_next, xs, w1, b1p, w2, b2, pm):
    nrows = xs.shape[0]
    step_rows = MOE_STEP_BLOCKS * MOE_ROWS
    nb = nrows // step_rows
    return pl.pallas_call(
        _expert_kernel,
        grid_spec=pltpu.PrefetchScalarGridSpec(
            num_scalar_prefetch=5,
            grid=(nb,),
            in_specs=[pl.BlockSpec((step_rows, D_MODEL), lambda i, be, nu, *_: (jnp.minimum(i, nu[0] - 1), 0)),
                      pl.BlockSpec(memory_space=pl.ANY),
                      pl.BlockSpec((1, 1, 2 * D_FF), lambda i, be, *_: (be[i], 0, 0)),
                      pl.BlockSpec(memory_space=pl.ANY),
                      pl.BlockSpec((1, 1, D_MODEL), lambda i, be, *_: (be[i], 0, 0)),
                      pl.BlockSpec((MXU_DIM, MXU_DIM), lambda i, *_: (0, 0))],
            out_specs=pl.BlockSpec((step_rows, D_MODEL), lambda i, *_: (i, 0)),
            scratch_shapes=[pltpu.VMEM((2, D_MODEL, 2 * D_FF), F32), pltpu.VMEM((2, D_FF, D_MODEL), F32),
                            pltpu.VMEM((D_MODEL, 2 * D_FF), BF16), pltpu.VMEM((D_FF, D_MODEL), BF16),
                            pltpu.SemaphoreType.DMA((2, 2))]),
        out_shape=jax.ShapeDtypeStruct((nrows, D_MODEL), F32),
        compiler_params=_cparams(("arbitrary",), VMEM_LIMIT_LARGE),
        name="experts",
    )(block_e, n_used, n_valid, run_ord, run_next, xs, w1, b1p, w2, b2, pm)


def _combine_kernel(tab_ref, seg_ref, tot_ref, cols_ref, x1_ref, g2_ref, b2_ref, ys_ref, y_ref, loc_sc, sem,
                    *, tile_base):
    i = pl.program_id(0)
    last = pl.num_programs(0) - 1
    tile = i + tile_base
    slot = i % 2
    loc, tm = loc_sc.shape[1], x1_ref.shape[0]

    def piece_copy(sl, lrow, grow, e, n):
        src = pl.multiple_of(seg_ref[e] + grow, RUN_ROWS)
        return pltpu.make_async_copy(ys_ref.at[pl.ds(src, n)], loc_sc.at[sl, pl.ds(lrow, n)], sem.at[sl])

    def gather(tl, sl):
        _for_each_run_piece(tab_ref, tl, lambda lrow, grow, e, n: piece_copy(sl, lrow, grow, e, n).start())

    @pl.when(i == 0)
    def _():
        loc_sc[...] = jnp.zeros_like(loc_sc)
        gather(tile, slot)

    @pl.when(i < last)
    def _():
        gather(tile + 1, 1 - slot)

    _drain_units(tot_ref[tile], lambda n: piece_copy(slot, 0, 0, 0, n), loc)

    cols = cols_ref[...]
    lane = lax.broadcasted_iota(jnp.int32, (tm, loc), 1)
    weights = jnp.zeros((tm, loc), F32)
    for k in range(TOP_K):
        weights = jnp.where(lane == cols[:, k:k + 1].astype(jnp.int32), cols[:, TOP_K + k:TOP_K + k + 1], weights)
    ffn = jnp.dot(weights.astype(BF16), loc_sc[slot].astype(BF16), preferred_element_type=F32)
    y_ref[...] = _layer_norm(DEEPNORM_ALPHA * x1_ref[...] + ffn, g2_ref[...], b2_ref[...])


def _combine(tab, seg_start, tot, cols, x1, g2, b2, ys, *, tile, tile_base):
    n = x1.shape[0]
    loc = tile * TOP_K + N_EXPERTS * RUN_ROWS
    return pl.pallas_call(
        functools.partial(_combine_kernel, tile_base=tile_base),
        grid_spec=pltpu.PrefetchScalarGridSpec(
            num_scalar_prefetch=3,
            grid=(n // tile,),
            in_specs=[pl.BlockSpec((tile, 2 * TOP_K), lambda i, *_: (i, 0)),
                      pl.BlockSpec((tile, D_MODEL), lambda i, *_: (i, 0)),
                      pl.BlockSpec((1, D_MODEL), lambda i, *_: (0, 0)),
                      pl.BlockSpec((1, D_MODEL), lambda i, *_: (0, 0)),
                      pl.BlockSpec(memory_space=pl.ANY)],
            out_specs=pl.BlockSpec((tile, D_MODEL), lambda i, *_: (i, 0)),
            scratch_shapes=[pltpu.VMEM((2, loc, D_MODEL), F32), pltpu.SemaphoreType.DMA((2,))]),
        out_shape=jax.ShapeDtypeStruct((n, D_MODEL), F32),
        compiler_params=_cparams(("arbitrary",), VMEM_LIMIT),
        name="combine",
    )(tab, seg_start, tot, cols, x1, g2, b2, ys)


def kernel(x_prompt, x_sample, cache_k_win, cache_v_win, state_ssm_re, state_ssm_im, w_in, b_in, attn_sinks,
           w_attn_out, ssm_a_re, ssm_a_im, ssm_log_dt, ssm_b_re, ssm_b_im, ssm_c_re, ssm_c_im, ssm_d, w_ssm_out,
           w_gate, b_gate, w_out, ln1_g, ln1_b, w_router, b_router, w_exp1, b_exp1, w_exp2, b_exp2, ln2_g, ln2_b):
    assert w_in.shape[0] == DEPTH == 1
    bsz, seq, _ = x_prompt.shape
    nsamp = x_sample.shape[0]
    assert x_sample.shape[1] == 1
    n_p = bsz * seq
    n_tok = n_p + nsamp

    xp = x_prompt.reshape(n_p, D_MODEL)
    xsm = x_sample.reshape(nsamp, D_MODEL)
    b_in2 = b_in[0].reshape(1, D_IN)
    sinks = attn_sinks[0].astype(F32)

    q_p, k_p, v_p, u_p = _proj(xp, w_in[0].astype(BF16), b_in2, tile=512, exact_f32=False, q_dtype=BF16)
    q_s, k_s, v_s, u_s = _proj(xsm, w_in[0], b_in2, tile=nsamp, exact_f32=True, q_dtype=F32)

    o_p = _attn_prompt(sinks, q_p.reshape(bsz, seq, D_ATTN), k_p.reshape(bsz, seq, D_KV),
                       v_p.reshape(bsz, seq, D_KV)).reshape(n_p, D_ATTN)
    k_buf = cache_k_win[0].reshape(nsamp, WINDOW, D_KV)
    v_buf = cache_v_win[0].reshape(nsamp, WINDOW, D_KV)
    o_s, k_next, v_next = _attn_sample(sinks, q_s, k_s, v_s, k_buf, v_buf)

    sp = _s5_params(ssm_a_re[0], ssm_a_im[0], ssm_log_dt[0], ssm_b_re[0], ssm_b_im[0], ssm_c_re[0], ssm_c_im[0])
    y_p, hp_re, hp_im = _s5_prompt(u_p, bsz, seq, _s5_chunk_mats(sp, ssm_d[0]))
    y_s, hs_re, hs_im = _s5_sample(u_s, state_ssm_re[0].reshape(nsamp, -1), state_ssm_im[0].reshape(nsamp, -1),
                                   _s5_sample_mats(sp, ssm_d[0]))

    wm = dict(wao=w_attn_out[0].astype(BF16), wso=w_ssm_out[0].astype(BF16), wg=w_gate[0].astype(BF16),
              bg=b_gate[0].reshape(1, -1), wo=w_out[0].astype(BF16), g1=ln1_g[0].reshape(1, -1),
              b1=ln1_b[0].reshape(1, -1), wrt=w_router[0].T, brt=b_router[0].reshape(-1, 1))
    wm_f32 = dict(wm, wao=w_attn_out[0], wso=w_ssm_out[0], wg=w_gate[0], wo=w_out[0])
    carry0 = jnp.zeros((SUBLANES, LANES), F32)
    x1_p, lpos_p, cols_p, tab_p, carry1 = _merge(xp, o_p, y_p, carry0, wm, tile=MERGE_TILE, route_tile=TOK_TILE,
                                                 f32_matmuls=False)
    x1_s, lpos_s, cols_s, tab_s, carry2 = _merge(xsm, o_s, y_s, carry1, wm_f32, tile=nsamp, route_tile=nsamp,
                                                 f32_matmuls=True)

    nt_p = n_p // TOK_TILE
    tab = jnp.concatenate([tab_p[:, :TAB_ROWS, :N_EXPERTS], tab_s[:, :TAB_ROWS, :N_EXPERTS]], axis=0)
    tot = jnp.sum(tab[:, 0, :], axis=1).astype(jnp.int32)
    tab = tab.reshape(-1)
    seg_rows = carry2[0, :N_EXPERTS].astype(jnp.int32) * RUN_ROWS
    step_rows = MOE_STEP_BLOCKS * MOE_ROWS
    padded = ((seg_rows + step_rows - 1) // step_rows) * step_rows
    pad_end = jnp.cumsum(padded)
    pad_start = (pad_end - padded).astype(jnp.int32)
    seg_end = pad_start + seg_rows
    n_runs = (nt_p + 1) * N_EXPERTS
    nb_max = (n_tok * TOP_K + n_runs * (RUN_ROWS - 1) + N_EXPERTS * (step_rows - 1) + step_rows - 1) // step_rows
    n_used = (pad_end[-1] // step_rows).astype(jnp.int32)
    tails = jnp.concatenate([seg_end // RUN_ROWS, (padded - seg_rows) // RUN_ROWS,
                             (pad_end[-1:] // MOE_ROWS)]).astype(jnp.int32)
    blk_start = jnp.arange(nb_max, dtype=jnp.int32) * step_rows
    blk_e = jnp.minimum(jnp.sum(blk_start[:, None] >= pad_end[None, :], axis=1), N_EXPERTS - 1).astype(jnp.int32)
    used = jnp.arange(nb_max) < n_used
    blk_e = jnp.where(used, blk_e, jnp.max(jnp.where(used, blk_e, 0)))
    ids = jnp.arange(N_EXPERTS, dtype=jnp.int32)
    of_blk = blk_e[:, None] == ids[None, :]
    n_valid = jnp.clip((jnp.sum(jnp.where(of_blk, seg_end[None, :], 0), axis=1) - blk_start + MOE_ROWS - 1)
                       // MOE_ROWS, 0, MOE_STEP_BLOCKS)
    n_valid = jnp.where(used, n_valid, 0).astype(jnp.int32)
    new_run = jnp.concatenate([jnp.ones((1,), jnp.int32), (blk_e[1:] != blk_e[:-1]).astype(jnp.int32)])
    run_ord = (jnp.cumsum(new_run) - 1).astype(jnp.int32)
    later = (ids[None, :] > ids[:, None]) & (padded > 0)[None, :]
    next_e = jnp.min(jnp.where(later, ids[None, :], N_EXPERTS), axis=1)
    next_e = jnp.where(next_e < N_EXPERTS, next_e, -1).astype(jnp.int32)
    run_next = jnp.sum(jnp.where(of_blk, next_e[None, :], 0), axis=1).astype(jnp.int32)

    nrows = nb_max * step_rows
    xs = _dispatch(tab, pad_start, tot, tails, lpos_p, x1_p, lpos_s, x1_s, tile=TOK_TILE, nrows=nrows)

    b1p = b_exp1[0].reshape(N_EXPERTS, 2 * D_FF // MXU_DIM, MXU_DIM // 2, 2)
    b1p = jnp.swapaxes(b1p, 2, 3).reshape(N_EXPERTS, 1, 2 * D_FF)
    ys = _experts(blk_e, n_used.reshape(1), n_valid, run_ord, run_next, xs, w_exp1[0], b1p, w_exp2[0],
                  b_exp2[0].reshape(N_EXPERTS, 1, D_MODEL),
                  jnp.asarray(_deinterleave_matrix(), BF16))

    g2, b2 = ln2_g[0].reshape(1, -1), ln2_b[0].reshape(1, -1)
    y_prompt = _combine(tab, pad_start, tot, cols_p, x1_p, g2, b2, ys, tile=TOK_TILE, tile_base=0)
    y_sample = _combine(tab, pad_start, tot, cols_s, x1_s, g2, b2, ys, tile=nsamp, tile_base=nt_p)

    k_p4 = k_p.reshape(bsz, seq, D_KV)[:, -WINDOW:].reshape(bsz, WINDOW, N_KV_HEADS, HEAD_DIM)
    v_p4 = v_p.reshape(bsz, seq, D_KV)[:, -WINDOW:].reshape(bsz, WINDOW, N_KV_HEADS, HEAD_DIM)
    k_s4 = k_next.reshape(nsamp, WINDOW, N_KV_HEADS, HEAD_DIM)
    v_s4 = v_next.reshape(nsamp, WINDOW, N_KV_HEADS, HEAD_DIM)
    st = lambda a, n: a.reshape(1, n, N_SSM_GROUPS, SSM_STATE)
    return (y_prompt.reshape(bsz, seq, D_MODEL), y_sample.reshape(nsamp, 1, D_MODEL),
            k_p4[None], v_p4[None], st(hp_re, bsz), st(hp_im, bsz),
            k_s4[None], v_s4[None], st(hs_re, nsamp), st(hs_im, nsamp))
```

```python
import functools

import numpy as np
import jax
import jax.numpy as jnp
from jax import lax
from jax.experimental import pallas as pl
from jax.experimental.pallas import tpu as pltpu

F32 = jnp.float32
BF16 = jnp.bfloat16

D_MODEL = 1024
HEAD_DIM = 64
N_Q_HEADS = 8
N_KV_HEADS = 2
Q_PER_KV = N_Q_HEADS // N_KV_HEADS
D_ATTN = N_Q_HEADS * HEAD_DIM
D_KV = N_KV_HEADS * HEAD_DIM
WINDOW = 128
ATTN_SCALE = HEAD_DIM ** -0.5
SSM_GROUP = 16
D_SSM = D_MODEL // 2
N_SSM_GROUPS = D_SSM // SSM_GROUP
SSM_STATE = 64
D_IN = D_ATTN + 2 * D_KV + D_SSM
N_EXPERTS = 32
TOP_K = 4
D_FF = D_MODEL
SWIGLU_LIMIT = 7.0
SWIGLU_ALPHA = 1.702
LN_EPS = 1e-5
DEPTH = 1
DEEPNORM_ALPHA = (2 * DEPTH) ** 0.25

LANES = 128
SUBLANES = 8
MXU_DIM = 256

S5_CHUNK = MXU_DIM // SSM_GROUP
S5_LANE_GROUPS = LANES // SSM_GROUP
MOE_ROWS = 256
MOE_STEP_BLOCKS = 2
TOK_TILE = 256
MERGE_TILE = 512
VMEM_LIMIT = 48 * 1024 * 1024
VMEM_LIMIT_LARGE = 56 * 1024 * 1024


def _cparams(sem, vmem=None):
    return pltpu.CompilerParams(dimension_semantics=sem, vmem_limit_bytes=vmem)


def _split_bf16(v):
    hi = v.astype(BF16)
    return hi, (v - hi.astype(F32)).astype(BF16)


def _dot_split(a, b, dims=(((1,), (0,)), ((), ()))):
    a_hi, a_lo = _split_bf16(a)
    b_hi, b_lo = _split_bf16(b)
    dot = lambda p, q: lax.dot_general(p, q, dims, preferred_element_type=F32)
    return dot(a_hi, b_hi) + dot(a_hi, b_lo) + dot(a_lo, b_hi)


def _proj_kernel(x_ref, w_ref, b_ref, q_ref, k_ref, v_ref, u_ref, *, exact_f32):
    if exact_f32:
        h = jnp.dot(x_ref[...], w_ref[...], preferred_element_type=F32, precision=lax.Precision.HIGHEST)
    else:
        h = jnp.dot(x_ref[...].astype(BF16), w_ref[...], preferred_element_type=F32)
    h = h + b_ref[...]
    q_ref[...] = (h[:, :D_ATTN] * ATTN_SCALE).astype(q_ref.dtype)
    k_ref[...] = h[:, D_ATTN:D_ATTN + D_KV]
    v_ref[...] = h[:, D_ATTN + D_KV:D_ATTN + 2 * D_KV]
    u_ref[...] = h[:, D_ATTN + 2 * D_KV:].astype(u_ref.dtype)


def _proj(x, w, b, *, tile, exact_f32, q_dtype):
    n = x.shape[0]
    return pl.pallas_call(
        functools.partial(_proj_kernel, exact_f32=exact_f32),
        grid=(n // tile,),
        in_specs=[pl.BlockSpec((tile, D_MODEL), lambda i: (i, 0)),
                  pl.BlockSpec((D_MODEL, D_IN), lambda i: (0, 0)),
                  pl.BlockSpec((1, D_IN), lambda i: (0, 0))],
        out_specs=[pl.BlockSpec((tile, D_ATTN), lambda i: (i, 0)),
                   pl.BlockSpec((tile, D_KV), lambda i: (i, 0)),
                   pl.BlockSpec((tile, D_KV), lambda i: (i, 0)),
                   pl.BlockSpec((tile, D_SSM), lambda i: (i, 0))],
        out_shape=[jax.ShapeDtypeStruct((n, D_ATTN), q_dtype),
                   jax.ShapeDtypeStruct((n, D_KV), F32),
                   jax.ShapeDtypeStruct((n, D_KV), F32),
                   jax.ShapeDtypeStruct((n, D_SSM), F32)],
        compiler_params=_cparams(("parallel",), VMEM_LIMIT),
        name="proj",
    )(x, w, b)


ATT_Q_TILE = 512


def _attn_prompt_kernel(sink_ref, q_ref, k_ref, v_ref, o_ref):
    i = pl.program_id(1)
    nk, nq = 2 * WINDOW, 2 * WINDOW
    lo = lax.broadcasted_iota(jnp.int32, (nk, LANES), 1) < HEAD_DIM
    top = lax.broadcasted_iota(jnp.int32, (nq, 1), 0) < WINDOW
    for blk in range(ATT_Q_TILE // WINDOW):
        q0 = i * ATT_Q_TILE + blk * WINDOW
        k0 = pl.multiple_of(jnp.maximum(q0 - WINDOW, 0), WINDOW)
        kk = k_ref[0, pl.ds(k0, nk), :]
        vv = v_ref[0, pl.ds(k0, nk), :]
        kk_sw = pltpu.roll(kk, HEAD_DIM, axis=1)
        vv_sw = pltpu.roll(vv, HEAD_DIM, axis=1)
        k_var = [[jnp.where(lo, kk, 0.0).astype(BF16), jnp.where(lo, 0.0, kk_sw).astype(BF16)],
                 [jnp.where(lo, kk_sw, 0.0).astype(BF16), jnp.where(lo, 0.0, kk).astype(BF16)]]
        v_var = [[jnp.where(lo, vv, 1.0).astype(BF16), jnp.where(lo, 1.0, vv_sw).astype(BF16)],
                 [jnp.where(lo, vv_sw, 1.0).astype(BF16), jnp.where(lo, 1.0, vv).astype(BF16)]]
        qpos = q0 + lax.broadcasted_iota(jnp.int32, (nq, nk), 0) % WINDOW
        kpos = k0 + lax.broadcasted_iota(jnp.int32, (nq, nk), 1)
        valid = (kpos <= qpos) & (qpos - kpos <= WINDOW)
        rows = slice(blk * WINDOW, (blk + 1) * WINDOW)
        for kv in range(N_KV_HEADS):
            pairs = (2 * kv, 2 * kv + 1)
            qs = jnp.concatenate([q_ref[0, rows, pr * LANES:(pr + 1) * LANES] for pr in pairs], axis=0)
            outs = []
            for parity in range(2):
                sink = jnp.where(top, sink_ref[2 * pairs[0] + parity], sink_ref[2 * pairs[1] + parity])
                s = lax.dot_general(qs, k_var[kv][parity], (((1,), (1,)), ((), ())), preferred_element_type=F32)
                s = jnp.where(valid, s, -jnp.inf)
                m = jnp.maximum(jnp.max(s, axis=-1, keepdims=True), sink)
                p = jnp.exp(s - m).astype(BF16)
                acc = jnp.dot(p, v_var[kv][parity], preferred_element_type=F32)
                outs.append(acc / (pltpu.roll(acc, HEAD_DIM, axis=1) + jnp.exp(sink - m)))
            o = jnp.where(lo, outs[0], outs[1]).astype(o_ref.dtype)
            for j, pr in enumerate(pairs):
                o_ref[0, rows, pr * LANES:(pr + 1) * LANES] = o[j * WINDOW:(j + 1) * WINDOW]


def _attn_prompt(sinks, q, k, v):
    bsz, seq = q.shape[0], q.shape[1]
    return pl.pallas_call(
        _attn_prompt_kernel,
        grid=(bsz, seq // ATT_Q_TILE),
        in_specs=[pl.BlockSpec(memory_space=pltpu.SMEM),
                  pl.BlockSpec((1, ATT_Q_TILE, D_ATTN), lambda b, i: (b, i, 0)),
                  pl.BlockSpec((1, seq, D_KV), lambda b, i: (b, 0, 0)),
                  pl.BlockSpec((1, seq, D_KV), lambda b, i: (b, 0, 0))],
        out_specs=pl.BlockSpec((1, ATT_Q_TILE, D_ATTN), lambda b, i: (b, i, 0)),
        out_shape=jax.ShapeDtypeStruct((bsz, seq, D_ATTN), BF16),
        compiler_params=_cparams(("parallel", "parallel")),
        name="attn_prompt",
    )(sinks, q, k, v)


ATT_S_GROUP = 16


def _attn_sample_kernel(sink_ref, q_ref, kn_ref, vn_ref, kb_ref, vb_ref, o_ref, knext_ref, vnext_ref):
    g = ATT_S_GROUP
    rows = Q_PER_KV * g
    ncol = g * WINDOW
    for buf_ref, new_ref, next_ref in ((kb_ref, kn_ref, knext_ref), (vb_ref, vn_ref, vnext_ref)):
        next_ref[:, 0:WINDOW - 1, :] = buf_ref[:, 1:WINDOW, :]
        next_ref[:, WINDOW - 1, :] = new_ref[...]
    kb = kb_ref[...].reshape(ncol, D_KV)
    vb = vb_ref[...].reshape(ncol, D_KV)
    rseq = lax.broadcasted_iota(jnp.int32, (rows, ncol), 0) % g
    cseq = lax.broadcasted_iota(jnp.int32, (rows, ncol), 1) // WINDOW
    own = rseq == cseq
    rhead = lax.broadcasted_iota(jnp.int32, (rows, 1), 0) // g
    for kv in range(N_KV_HEADS):
        lo = kv * HEAD_DIM
        qs = jnp.concatenate(
            [q_ref[:, (kv * Q_PER_KV + h) * HEAD_DIM:(kv * Q_PER_KV + h + 1) * HEAD_DIM] for h in range(Q_PER_KV)],
            axis=0)
        kn = jnp.concatenate([kn_ref[:, lo:lo + HEAD_DIM]] * Q_PER_KV, axis=0)
        vn = jnp.concatenate([vn_ref[:, lo:lo + HEAD_DIM]] * Q_PER_KV, axis=0)
        sink = jnp.zeros((rows, 1), F32)
        for h in range(Q_PER_KV):
            sink = jnp.where(rhead == h, sink_ref[kv * Q_PER_KV + h], sink)
        qs = qs.astype(F32)
        s = _dot_split(qs, kb[:, lo:lo + HEAD_DIM], (((1,), (1,)), ((), ())))
        s = jnp.where(own, s, -jnp.inf)
        s_new = jnp.sum(qs * kn, axis=-1, keepdims=True)
        m = jnp.maximum(jnp.maximum(jnp.max(s, axis=-1, keepdims=True), s_new), sink)
        p = jnp.exp(s - m)
        p_new = jnp.exp(s_new - m)
        denom = jnp.sum(p, axis=-1, keepdims=True) + p_new + jnp.exp(sink - m)
        o = (_dot_split(p, vb[:, lo:lo + HEAD_DIM]) + p_new * vn) / denom
        for h in range(Q_PER_KV):
            c0 = (kv * Q_PER_KV + h) * HEAD_DIM
            o_ref[:, c0:c0 + HEAD_DIM] = o[h * g:(h + 1) * g].astype(o_ref.dtype)


def _attn_sample(sinks, q, k_new, v_new, k_buf, v_buf):
    n = q.shape[0]
    g = ATT_S_GROUP
    return pl.pallas_call(
        _attn_sample_kernel,
        grid=(n // g,),
        in_specs=[pl.BlockSpec(memory_space=pltpu.SMEM),
                  pl.BlockSpec((g, D_ATTN), lambda i: (i, 0)),
                  pl.BlockSpec((g, D_KV), lambda i: (i, 0)),
                  pl.BlockSpec((g, D_KV), lambda i: (i, 0)),
                  pl.BlockSpec((g, WINDOW, D_KV), lambda i: (i, 0, 0)),
                  pl.BlockSpec((g, WINDOW, D_KV), lambda i: (i, 0, 0))],
        out_specs=[pl.BlockSpec((g, D_ATTN), lambda i: (i, 0)),
                   pl.BlockSpec((g, WINDOW, D_KV), lambda i: (i, 0, 0)),
                   pl.BlockSpec((g, WINDOW, D_KV), lambda i: (i, 0, 0))],
        out_shape=[jax.ShapeDtypeStruct((n, D_ATTN), F32),
                   jax.ShapeDtypeStruct((n, WINDOW, D_KV), F32),
                   jax.ShapeDtypeStruct((n, WINDOW, D_KV), F32)],
        compiler_params=_cparams(("parallel",)),
        name="attn_sample",
    )(sinks, q, k_new, v_new, k_buf, v_buf)


def _s5_params(a_re, a_im, log_dt, b_re, b_im, c_re, c_im):
    hp = lax.Precision.HIGHEST
    dt = jnp.exp(log_dt.astype(F32))[:, None]
    are, aim = a_re.astype(F32), a_im.astype(F32)
    tau = jnp.arange(S5_CHUNK + 1, dtype=F32)[None, :, None]
    mag = jnp.exp(tau * (dt * are)[:, None, :])
    ang = tau * (dt * aim)[:, None, :]
    pw_re, pw_im = mag * jnp.cos(ang), mag * jnp.sin(ang)
    ab_re, ab_im = pw_re[:, 1], pw_im[:, 1]
    den = are * are + aim * aim
    f_re = ((ab_re - 1.0) * are + ab_im * aim) / den
    f_im = (ab_im * are - (ab_re - 1.0) * aim) / den
    bre, bim = b_re.astype(F32), b_im.astype(F32)
    bb_re = f_re[..., None] * bre - f_im[..., None] * bim
    bb_im = f_re[..., None] * bim + f_im[..., None] * bre
    cre, cim = c_re.astype(F32), c_im.astype(F32)
    return dict(pw_re=pw_re, pw_im=pw_im, ab_re=ab_re, ab_im=ab_im, bb_re=bb_re, bb_im=bb_im,
                c_re=cre, c_im=cim, hp=hp)


def _s5_chunk_mats(sp, d_skip):
    hp = sp["hp"]
    g, t, c, p = N_SSM_GROUPS, S5_CHUNK, SSM_GROUP, SSM_STATE
    pw_re, pw_im = sp["pw_re"], sp["pw_im"]
    ca_re = sp["c_re"][:, None] * pw_re[:, :, None, :] - sp["c_im"][:, None] * pw_im[:, :, None, :]
    ca_im = sp["c_re"][:, None] * pw_im[:, :, None, :] + sp["c_im"][:, None] * pw_re[:, :, None, :]
    kern = (jnp.einsum("gtcp,gpd->gtcd", ca_re[:, :t], sp["bb_re"], precision=hp)
            - jnp.einsum("gtcp,gpd->gtcd", ca_im[:, :t], sp["bb_im"], precision=hp))
    kc = jnp.swapaxes(kern, 2, 3)
    kc = kc.at[:, 0].add(d_skip.astype(F32).reshape(g, 1, c) * jnp.eye(c, dtype=F32)[None])
    rev_re, rev_im = pw_re[:, t - 1::-1][:, :t], pw_im[:, t - 1::-1][:, :t]
    wst_re = rev_re[:, :, None, :] * jnp.swapaxes(sp["bb_re"], 1, 2)[:, None] \
        - rev_im[:, :, None, :] * jnp.swapaxes(sp["bb_im"], 1, 2)[:, None]
    wst_im = rev_re[:, :, None, :] * jnp.swapaxes(sp["bb_im"], 1, 2)[:, None] \
        + rev_im[:, :, None, :] * jnp.swapaxes(sp["bb_re"], 1, 2)[:, None]
    wo_re = jnp.transpose(ca_re[:, 1:t + 1], (0, 3, 1, 2))
    wo_im = -jnp.transpose(ca_im[:, 1:t + 1], (0, 3, 1, 2))
    nv, gl = g // S5_LANE_GROUPS, S5_LANE_GROUPS
    kc, wst_re, wst_im, wo_re, wo_im = lax.optimization_barrier((kc, wst_re, wst_im, wo_re, wo_im))
    kc_c = kc.astype(BF16).reshape(nv, gl * t * c, c)
    ws_c = jnp.concatenate([wst_re, wst_im], axis=-1).astype(BF16).reshape(nv, gl * t * c, 2 * p)
    wo_re_c = wo_re.astype(BF16).reshape(nv, gl * p, t * c)
    wo_im_c = wo_im.astype(BF16).reshape(nv, gl * p, t * c)
    spread_b = np.zeros((c, LANES), np.float32)
    spread_s = np.zeros((2 * p, 2 * gl * p), np.float32)
    spread_o = np.zeros((t * c, t * LANES), np.float32)
    for h in range(gl):
        spread_b[np.arange(c), h * c + np.arange(c)] = 1.0
        for ri in range(2):
            spread_s[ri * p + np.arange(p), ri * gl * p + h * p + np.arange(p)] = 1.0
        for tt in range(t):
            spread_o[tt * c + np.arange(c), tt * LANES + h * c + np.arange(c)] = 1.0
    at_re = pw_re[:, t].reshape(1, g * p)
    at_im = pw_im[:, t].reshape(1, g * p)
    return dict(kc=kc_c, ws=ws_c, wo_re=wo_re_c, wo_im=wo_im_c, spread_b=jnp.asarray(spread_b, BF16),
                spread_s=jnp.asarray(spread_s, BF16), spread_o=jnp.asarray(spread_o, BF16),
                at_re=at_re, at_im=at_im)


def _s5_chunk_rows(u_ref, nchunk):
    return jnp.concatenate(
        [u_ref[pl.ds(s, nchunk, stride=S5_CHUNK), :] for s in range(S5_CHUNK)], axis=1).astype(BF16)


S5_SLABS = S5_LANE_GROUPS * SSM_STATE // LANES


S5_EXPAND_ROWS = 256
S5_C_SHIFT = SSM_GROUP.bit_length() - 1
S5_P_SHIFT = SSM_STATE.bit_length() - 1


def _s5_expand(dst_ref, compact_rows, spread_ref, row_shift, col_shift):
    n_rows, n_cols = dst_ref.shape
    col_g = lax.shift_right_logical(lax.broadcasted_iota(jnp.int32, (S5_EXPAND_ROWS, n_cols), 1), col_shift)
    for r0 in range(0, n_rows, S5_EXPAND_ROWS):
        row_g = lax.shift_right_logical(r0 + lax.broadcasted_iota(jnp.int32, (S5_EXPAND_ROWS, n_cols), 0), row_shift)
        same = ((row_g ^ col_g) & (S5_LANE_GROUPS - 1)) == 0
        blk = jnp.dot(compact_rows(r0), spread_ref[...], preferred_element_type=F32)
        dst_ref[r0:r0 + S5_EXPAND_ROWS, :] = jnp.where(same, blk, 0.0).astype(dst_ref.dtype)


def _s5_group_rows(ref, index):
    per_group = S5_CHUNK * SSM_GROUP
    return jnp.concatenate([ref[0, g * per_group + index * SSM_GROUP:g * per_group + (index + 1) * SSM_GROUP, :]
                            for g in range(S5_LANE_GROUPS)], axis=0)


def _s5_state_kernel(u_ref, ws_ref, spread_ref, sre_ref, sim_ref, wst_sc):
    nchunk = sre_ref.shape[1]

    @pl.when(pl.program_id(1) == 0)
    def _():
        per_call = S5_EXPAND_ROWS // LANES
        _s5_expand(wst_sc,
                   lambda r0: jnp.concatenate([_s5_group_rows(ws_ref, r0 // LANES + j) for j in range(per_call)], axis=0),
                   spread_ref, S5_C_SHIFT, S5_P_SHIFT)

    s = jnp.dot(_s5_chunk_rows(u_ref, nchunk), wst_sc[...], preferred_element_type=F32)
    for k in range(S5_SLABS):
        sre_ref[k] = s[:, k * LANES:(k + 1) * LANES]
        sim_ref[k] = s[:, (S5_SLABS + k) * LANES:(S5_SLABS + k + 1) * LANES]


def _s5_scan_kernel(sre_ref, sim_ref, are_ref, aim_ref, hre_ref, him_ref, fre_ref, fim_ref, *, bsz):
    nchunk = sre_ref.shape[1] // bsz
    are = [jnp.broadcast_to(are_ref[:, k * LANES:(k + 1) * LANES], (bsz, LANES)) for k in range(S5_SLABS)]
    aim = [jnp.broadcast_to(aim_ref[:, k * LANES:(k + 1) * LANES], (bsz, LANES)) for k in range(S5_SLABS)]

    def body(j, carry):
        rows = pl.ds(j, bsz, stride=nchunk)
        out = []
        for k in range(S5_SLABS):
            cre, cim = carry[2 * k], carry[2 * k + 1]
            hre_ref[k, rows, :] = cre
            him_ref[k, rows, :] = cim
            sr = sre_ref[k, rows, :]
            si = sim_ref[k, rows, :]
            out += [are[k] * cre - aim[k] * cim + sr, are[k] * cim + aim[k] * cre + si]
        return tuple(out)

    zero = jnp.zeros((bsz, LANES), F32)
    fin = lax.fori_loop(0, nchunk, body, (zero,) * (2 * S5_SLABS), unroll=4)
    fre_ref[...] = jnp.concatenate(fin[0::2], axis=1)
    fim_ref[...] = jnp.concatenate(fin[1::2], axis=1)


def _s5_out_kernel(u_ref, kc_ref, spread_b_ref, hre_ref, him_ref, wo_re_ref, wo_im_ref, spread_o_ref, y_ref,
                   m_sc, wout_sc):
    nchunk = hre_ref.shape[1]

    @pl.when(pl.program_id(1) == 0)
    def _():
        rg = lax.shift_right_logical(lax.broadcasted_iota(jnp.int32, (LANES, LANES), 0), S5_C_SHIFT)
        cg = lax.shift_right_logical(lax.broadcasted_iota(jnp.int32, (LANES, LANES), 1), S5_C_SHIFT)
        zero_blk = jnp.zeros((LANES, LANES), BF16)
        lag_blk = [jnp.where(rg == cg, jnp.dot(_s5_group_rows(kc_ref, tau), spread_b_ref[...],
                                               preferred_element_type=F32), 0.0).astype(BF16)
                   for tau in range(S5_CHUNK)]
        for s in range(S5_CHUNK):
            for t in range(S5_CHUNK):
                m_sc[s * LANES:(s + 1) * LANES, t * LANES:(t + 1) * LANES] = lag_blk[t - s] if t >= s else zero_blk
        half = wo_re_ref.shape[1]
        _s5_expand(wout_sc,
                   lambda r0: (wo_re_ref[0, r0:r0 + S5_EXPAND_ROWS, :] if r0 < half
                               else wo_im_ref[0, r0 - half:r0 - half + S5_EXPAND_ROWS, :]),
                   spread_o_ref, S5_P_SHIFT, S5_C_SHIFT)

    hcat = jnp.concatenate([hre_ref[k] for k in range(S5_SLABS)] + [him_ref[k] for k in range(S5_SLABS)],
                           axis=1).astype(BF16)
    lhs = _s5_chunk_rows(u_ref, nchunk)
    y = jnp.concatenate(
        [jnp.dot(lhs[:, :j + MXU_DIM], m_sc[:j + MXU_DIM, j:j + MXU_DIM], preferred_element_type=F32)
         for j in range(0, S5_CHUNK * LANES, MXU_DIM)], axis=1)
    y = y + jnp.dot(hcat, wout_sc[...], preferred_element_type=F32)
    for s in range(S5_CHUNK):
        y_ref[pl.ds(s, nchunk, stride=S5_CHUNK), :] = y[:, s * LANES:(s + 1) * LANES]


def _s5_prompt(u, bsz, seq, mats):
    at_re, at_im = mats["at_re"], mats["at_im"]
    g, t, p, c = N_SSM_GROUPS, S5_CHUNK, SSM_STATE, SSM_GROUP
    nchunk = seq // t
    n = nchunk * bsz
    nv = g // S5_LANE_GROUPS
    half = S5_LANE_GROUPS * p
    s_re, s_im = pl.pallas_call(
        _s5_state_kernel,
        grid=(nv, bsz),
        in_specs=[pl.BlockSpec((seq, LANES), lambda v, b: (b, v)),
                  pl.BlockSpec((1, t * LANES, 2 * p), lambda v, b: (v, 0, 0)),
                  pl.BlockSpec((2 * p, 2 * half), lambda v, b: (0, 0))],
        out_specs=[pl.BlockSpec((S5_SLABS, nchunk, LANES), lambda v, b: (v, b, 0)),
                   pl.BlockSpec((S5_SLABS, nchunk, LANES), lambda v, b: (v, b, 0))],
        out_shape=[jax.ShapeDtypeStruct((nv * S5_SLABS, n, LANES), F32)] * 2,
        scratch_shapes=[pltpu.VMEM((t * LANES, 2 * half), BF16)],
        compiler_params=_cparams(("parallel", "arbitrary"), VMEM_LIMIT),
        name="s5_state",
    )(u, mats["ws"], mats["spread_s"])
    h_re, h_im, f_re, f_im = pl.pallas_call(
        functools.partial(_s5_scan_kernel, bsz=bsz),
        grid=(nv,),
        in_specs=[pl.BlockSpec((S5_SLABS, n, LANES), lambda i: (i, 0, 0)),
                  pl.BlockSpec((S5_SLABS, n, LANES), lambda i: (i, 0, 0)),
                  pl.BlockSpec((1, half), lambda i: (0, i)),
                  pl.BlockSpec((1, half), lambda i: (0, i))],
        out_specs=[pl.BlockSpec((S5_SLABS, n, LANES), lambda i: (i, 0, 0)),
                   pl.BlockSpec((S5_SLABS, n, LANES), lambda i: (i, 0, 0)),
                   pl.BlockSpec((bsz, half), lambda i: (0, i)),
                   pl.BlockSpec((bsz, half), lambda i: (0, i))],
        out_shape=[jax.ShapeDtypeStruct((nv * S5_SLABS, n, LANES), F32)] * 2
        + [jax.ShapeDtypeStruct((bsz, g * p), F32)] * 2,
        compiler_params=_cparams(("parallel",)),
        name="s5_scan",
    )(s_re, s_im, at_re, at_im)
    y = pl.pallas_call(
        _s5_out_kernel,
        grid=(nv, bsz),
        in_specs=[pl.BlockSpec((seq, LANES), lambda v, b: (b, v)),
                  pl.BlockSpec((1, t * LANES, c), lambda v, b: (v, 0, 0)),
                  pl.BlockSpec((c, LANES), lambda v, b: (0, 0)),
                  pl.BlockSpec((S5_SLABS, nchunk, LANES), lambda v, b: (v, b, 0)),
                  pl.BlockSpec((S5_SLABS, nchunk, LANES), lambda v, b: (v, b, 0)),
                  pl.BlockSpec((1, half, t * c), lambda v, b: (v, 0, 0)),
                  pl.BlockSpec((1, half, t * c), lambda v, b: (v, 0, 0)),
                  pl.BlockSpec((t * c, t * LANES), lambda v, b: (0, 0))],
        out_specs=pl.BlockSpec((seq, LANES), lambda v, b: (b, v)),
        out_shape=jax.ShapeDtypeStruct((bsz * seq, D_SSM), F32),
        scratch_shapes=[pltpu.VMEM((t * LANES, t * LANES), BF16), pltpu.VMEM((2 * half, t * LANES), BF16)],
        compiler_params=_cparams(("parallel", "arbitrary"), VMEM_LIMIT),
        name="s5_out",
    )(u, mats["kc"], mats["spread_b"], h_re, h_im, mats["wo_re"], mats["wo_im"], mats["spread_o"])
    return y, f_re, f_im


S5S_GROUPS = LANES // SSM_GROUP


def _s5_sample_mats(sp, d_skip):
    go, gl, c, p = N_SSM_GROUPS // S5S_GROUPS, S5S_GROUPS, SSM_GROUP, SSM_STATE
    eye = jnp.eye(gl, dtype=F32)

    def bdiag_in(b):
        b4 = b.reshape(go, gl, p, c)
        return jnp.einsum("ogpc,gh->ogchp", b4, eye).reshape(go, gl * c, gl * p)

    def bdiag_out(cm):
        c4 = cm.reshape(go, gl, c, p)
        return jnp.einsum("ogcp,gh->ogphc", c4, eye).reshape(go, gl * p, gl * c)

    b8 = jnp.concatenate([bdiag_in(sp["bb_re"]), bdiag_in(sp["bb_im"])], axis=2)
    c8 = jnp.concatenate([bdiag_out(sp["c_re"]), -bdiag_out(sp["c_im"])], axis=1)
    a_re = sp["ab_re"].reshape(1, N_SSM_GROUPS * p)
    a_im = sp["ab_im"].reshape(1, N_SSM_GROUPS * p)
    return b8, c8, a_re, a_im, d_skip.astype(F32).reshape(1, D_SSM)


def _s5_sample_kernel(u_ref, hre_ref, him_ref, b8_ref, c8_ref, are_ref, aim_ref, d_ref,
                      y_ref, ore_ref, oim_ref):
    hp = lax.Precision.HIGHEST
    u = u_ref[...]
    half = S5S_GROUPS * SSM_STATE
    bu = jnp.dot(u, b8_ref[0], preferred_element_type=F32, precision=hp)
    are, aim = are_ref[...], aim_ref[...]
    h0r, h0i = hre_ref[...], him_ref[...]
    hr = are * h0r - aim * h0i + bu[:, :half]
    hi = are * h0i + aim * h0r + bu[:, half:]
    ore_ref[...] = hr
    oim_ref[...] = hi
    y = jnp.dot(jnp.concatenate([hr, hi], axis=1), c8_ref[0], preferred_element_type=F32, precision=hp)
    y_ref[...] = (y + d_ref[...] * u).astype(y_ref.dtype)


def _s5_sample(u, h0_re, h0_im, mats):
    b8, c8, a_re, a_im, d = mats
    n = u.shape[0]
    half = S5S_GROUPS * SSM_STATE
    return pl.pallas_call(
        _s5_sample_kernel,
        grid=(N_SSM_GROUPS // S5S_GROUPS,),
        in_specs=[pl.BlockSpec((n, LANES), lambda i: (0, i)),
                  pl.BlockSpec((n, half), lambda i: (0, i)),
                  pl.BlockSpec((n, half), lambda i: (0, i)),
                  pl.BlockSpec((1, LANES, 2 * half), lambda i: (i, 0, 0)),
                  pl.BlockSpec((1, 2 * half, LANES), lambda i: (i, 0, 0)),
                  pl.BlockSpec((1, half), lambda i: (0, i)),
                  pl.BlockSpec((1, half), lambda i: (0, i)),
                  pl.BlockSpec((1, LANES), lambda i: (0, i))],
        out_specs=[pl.BlockSpec((n, LANES), lambda i: (0, i)),
                   pl.BlockSpec((n, half), lambda i: (0, i)),
                   pl.BlockSpec((n, half), lambda i: (0, i))],
        out_shape=[jax.ShapeDtypeStruct((n, D_SSM), F32),
                   jax.ShapeDtypeStruct((n, N_SSM_GROUPS * SSM_STATE), F32),
                   jax.ShapeDtypeStruct((n, N_SSM_GROUPS * SSM_STATE), F32)],
        compiler_params=_cparams(("parallel",)),
        name="s5_sample",
    )(u, h0_re, h0_im, b8, c8, a_re, a_im, d)


def _layer_norm(x, g, b):
    mu = jnp.mean(x, axis=-1, keepdims=True)
    xc = x - mu
    var = jnp.mean(xc * xc, axis=-1, keepdims=True)
    return xc * lax.rsqrt(var + LN_EPS) * g + b


def _sigmoid(x):
    return 0.5 * jnp.tanh(0.5 * x) + 0.5


RUN_ROWS = SUBLANES
TAB_ROWS = 3


def _merge_kernel(x_ref, oa_ref, ys_ref, carry_in_ref, wao_ref, wso_ref, wg_ref, bg_ref, wo_ref,
                  g1_ref, b1_ref, wrt_ref, brt_ref,
                  x1_ref, lpos_ref, cols_ref, tab_ref, carry_out_ref, carry_sc, *, f32_matmuls):
    step = pl.program_id(0)

    @pl.when(step == 0)
    def _():
        carry_sc[...] = carry_in_ref[...]

    def mm(a, w_ref):
        if f32_matmuls:
            return _dot_split(a.astype(F32), w_ref[...])
        return jnp.dot(a.astype(BF16), w_ref[...], preferred_element_type=F32)

    tm = x_ref.shape[0]
    x = x_ref[...]
    branch_a = mm(oa_ref[...], wao_ref)
    z = mm(jax.nn.gelu(ys_ref[...].astype(F32)), wso_ref)
    branch_b = z[:, :D_MODEL] * _sigmoid(z[:, D_MODEL:])
    gates = _sigmoid(mm(x, wg_ref) + bg_ref[...])
    mixed = gates[:, :D_MODEL] * branch_a + gates[:, D_MODEL:] * branch_b
    mix = mm(mixed, wo_ref)
    x1 = _layer_norm(DEEPNORM_ALPHA * x + mix, g1_ref[...], b1_ref[...])
    x1_ref[...] = x1

    split2 = _split_bf16

    def dot_nt(a, b):
        return lax.dot_general(a, b, (((1,), (1,)), ((), ())), preferred_element_type=F32)

    w_hi, w_lo = split2(wrt_ref[...])
    rt = tm // tab_ref.shape[0]
    sub = lax.broadcasted_iota(jnp.int32, (N_EXPERTS, rt), 0)
    r = lax.broadcasted_iota(jnp.int32, (rt, rt), 0)
    c = lax.broadcasted_iota(jnp.int32, (rt, rt), 1)
    er = lax.broadcasted_iota(jnp.int32, (N_EXPERTS, N_EXPERTS), 0)
    ec = lax.broadcasted_iota(jnp.int32, (N_EXPERTS, N_EXPERTS), 1)
    rid = lax.broadcasted_iota(jnp.int32, (SUBLANES, LANES), 0)
    lane_pad = jnp.zeros((SUBLANES, LANES - N_EXPERTS), F32)
    for h in range(tab_ref.shape[0]):
        x_hi, x_lo = split2(x1[h * rt:(h + 1) * rt])
        logits = dot_nt(w_hi, x_hi) + dot_nt(w_hi, x_lo) + dot_nt(w_lo, x_hi) + brt_ref[...]
        work = logits
        vals, sels = [], []
        for _ in range(TOP_K):
            mx = jnp.max(work, axis=0, keepdims=True)
            idx = jnp.min(jnp.where(work == mx, sub, N_EXPERTS), axis=0, keepdims=True)
            sel = sub == idx
            vals.append(mx)
            sels.append(sel)
            work = jnp.where(sel, -jnp.inf, work)
        ex = [jnp.exp(v - vals[0]) for v in vals]
        tot = ex[0] + ex[1] + ex[2] + ex[3]
        gate_rows = jnp.concatenate([e / tot for e in ex], axis=0)

        multi = jnp.zeros((N_EXPERTS, rt), F32)
        for sel in sels:
            multi = multi + jnp.where(sel, 1.0, 0.0)
        multi_b = multi.astype(BF16)
        earlier = jnp.dot(multi_b, jnp.where(r < c, 1.0, 0.0).astype(BF16), preferred_element_type=F32)
        cnt_col = jnp.sum(multi, axis=1, keepdims=True)
        nb_col = jnp.floor((cnt_col + (RUN_ROWS - 1.0)) * (1.0 / RUN_ROWS))
        loff_col = jnp.dot(jnp.where(ec < er, 1.0, 0.0).astype(BF16),
                           jnp.broadcast_to(nb_col, (N_EXPERTS, rt)).astype(BF16), preferred_element_type=F32)
        base = RUN_ROWS * loff_col + earlier
        lpos = jnp.concatenate([jnp.sum(jnp.where(sel, base, 0.0), axis=0, keepdims=True) for sel in sels],
                               axis=0)
        lpos_ref[:, h * rt:(h + 1) * rt] = lpos.astype(jnp.int32)
        rows_hi, rows_lo = split2(jnp.concatenate([lpos, gate_rows], axis=0))
        eye = jnp.where(r == c, 1.0, 0.0).astype(BF16)
        cols_ref[h * rt:(h + 1) * rt, :] = dot_nt(eye, rows_hi) + dot_nt(eye, rows_lo)

        cnt_row = dot_nt(jnp.ones((SUBLANES, rt), BF16), multi_b)
        nb_row = jnp.floor((cnt_row + (RUN_ROWS - 1.0)) * (1.0 / RUN_ROWS))
        loff_row = jnp.dot(nb_row.astype(BF16), jnp.where(er < ec, 1.0, 0.0).astype(BF16),
                           preferred_element_type=F32)
        nb_p = jnp.concatenate([nb_row, lane_pad], axis=1)
        loff_p = jnp.concatenate([loff_row, lane_pad], axis=1)
        goff_p = carry_sc[...]
        tab = jnp.where(rid == 0, nb_p, jnp.where(rid == 1, loff_p, jnp.where(rid == 2, goff_p, 0.0)))
        tab_ref[h] = tab.astype(jnp.int32)
        carry_sc[...] = goff_p + nb_p
    carry_out_ref[...] = carry_sc[...]


def _merge(x, o_attn, y_ssm, carry_in, w, *, tile, route_tile, f32_matmuls):
    n = x.shape[0]
    nt = n // tile
    per_step = tile // route_tile
    full = lambda shape: pl.BlockSpec(shape, lambda i: (0,) * len(shape))
    return pl.pallas_call(
        functools.partial(_merge_kernel, f32_matmuls=f32_matmuls),
        grid=(nt,),
        in_specs=[pl.BlockSpec((tile, D_MODEL), lambda i: (i, 0)),
                  pl.BlockSpec((tile, D_ATTN), lambda i: (i, 0)),
                  pl.BlockSpec((tile, D_SSM), lambda i: (i, 0)),
                  full((SUBLANES, LANES)),
                  full((D_ATTN, D_MODEL)), full((D_SSM, 2 * D_MODEL)), full((D_MODEL, 2 * D_MODEL)),
                  full((1, 2 * D_MODEL)), full((D_MODEL, D_MODEL)),
                  full((1, D_MODEL)), full((1, D_MODEL)),
                  full((N_EXPERTS, D_MODEL)), full((N_EXPERTS, 1))],
        out_specs=[pl.BlockSpec((tile, D_MODEL), lambda i: (i, 0)),
                   pl.BlockSpec((TOP_K, tile), lambda i: (0, i)),
                   pl.BlockSpec((tile, 2 * TOP_K), lambda i: (i, 0)),
                   pl.BlockSpec((per_step, SUBLANES, LANES), lambda i: (i, 0, 0)),
                   full((SUBLANES, LANES))],
        out_shape=[jax.ShapeDtypeStruct((n, D_MODEL), F32),
                   jax.ShapeDtypeStruct((TOP_K, n), jnp.int32),
                   jax.ShapeDtypeStruct((n, 2 * TOP_K), F32),
                   jax.ShapeDtypeStruct((nt * per_step, SUBLANES, LANES), jnp.int32),
                   jax.ShapeDtypeStruct((SUBLANES, LANES), F32)],
        scratch_shapes=[pltpu.VMEM((SUBLANES, LANES), F32)],
        compiler_params=_cparams(("arbitrary",), VMEM_LIMIT_LARGE),
        name="merge",
    )(x, o_attn, y_ssm, carry_in, w["wao"], w["wso"], w["wg"], w["bg"], w["wo"], w["g1"], w["b1"],
      w["wrt"], w["brt"])


def _tab(tab_ref, tile, row, e):
    return tab_ref[(tile * TAB_ROWS + row) * N_EXPERTS + e]


BIG_PIECE = 4 * RUN_ROWS
MAX_UNITS_LOG2 = 8


def _for_each_run_piece(tab_ref, tile, fn):
    def per_expert(e, carry):
        loff = RUN_ROWS * _tab(tab_ref, tile, 1, e)
        goff = RUN_ROWS * _tab(tab_ref, tile, 2, e)
        units = _tab(tab_ref, tile, 0, e)
        n_big = lax.shift_right_logical(units, 2)

        def big(j, c2):
            fn(pl.multiple_of(loff + j * BIG_PIECE, RUN_ROWS), goff + j * BIG_PIECE, e, BIG_PIECE)
            return c2

        lax.fori_loop(0, n_big, big, 0)
        done = n_big * BIG_PIECE

        def small(j, c2):
            fn(pl.multiple_of(loff + done + j * RUN_ROWS, RUN_ROWS), goff + done + j * RUN_ROWS, e, RUN_ROWS)
            return c2

        lax.fori_loop(0, units & 3, small, 0)
        return carry

    lax.fori_loop(0, N_EXPERTS, per_expert, 0)


def _drain_units(units, wait_copy, buffer_rows):
    assert buffer_rows < (RUN_ROWS << MAX_UNITS_LOG2)
    for b in range(MAX_UNITS_LOG2):
        if (RUN_ROWS << b) > buffer_rows:
            break

        @pl.when((lax.shift_right_logical(units, b) & 1) == 1)
        def _():
            wait_copy(RUN_ROWS << b).wait()


def _dispatch_kernel(tab_ref, seg_ref, tot_ref, tail_ref, lpos_p_ref, xp_ref, lpos_s_ref, xs_in_ref, xs_ref,
                     loc_sc, zero_sc, sem, zsem):
    i = pl.program_id(0)
    last = pl.num_programs(0) - 1
    tile = i
    slot = i % 2
    loc = loc_sc.shape[1]

    @pl.when(i == 0)
    def _():
        zero_sc[...] = jnp.zeros_like(zero_sc)

        def tail_copy(e, j):
            row = pl.multiple_of(RUN_ROWS * (tail_ref[e] + j), RUN_ROWS)
            return pltpu.make_async_copy(zero_sc.at[pl.ds(0, RUN_ROWS)], xs_ref.at[pl.ds(row, RUN_ROWS)], zsem)

        def per_expert(e, carry):
            n = tail_ref[N_EXPERTS + e]
            lax.fori_loop(0, n, lambda j, c2: (tail_copy(e, j).start(), c2)[1], 0)
            lax.fori_loop(0, n, lambda j, c2: (tail_copy(e, j).wait(), c2)[1], 0)
            return carry

        lax.fori_loop(0, N_EXPERTS, per_expert, 0)

        def block_copy(b):
            row = pl.multiple_of(b * MOE_ROWS, MOE_ROWS)
            return pltpu.make_async_copy(zero_sc, xs_ref.at[pl.ds(row, MOE_ROWS)], zsem)

        first_unused, n_blocks = tail_ref[2 * N_EXPERTS], xs_ref.shape[0] // MOE_ROWS
        lax.fori_loop(first_unused, n_blocks, lambda b, c2: (block_copy(b).start(), c2)[1], 0)
        lax.fori_loop(first_unused, n_blocks, lambda b, c2: (block_copy(b).wait(), c2)[1], 0)

    def sort_tile(lpos_ref, x_ref):
        tm = x_ref.shape[0]
        rows = lax.broadcasted_iota(jnp.int32, (loc, tm), 0)
        lp = lpos_ref[...]
        onehot = jnp.zeros((loc, tm), F32)
        for k in range(TOP_K):
            onehot = jnp.where(rows == lp[k:k + 1], 1.0, onehot)
        loc_sc[slot] = jnp.dot(onehot.astype(BF16), x_ref[...].astype(BF16), preferred_element_type=F32)

    @pl.when(i < last)
    def _():
        sort_tile(lpos_p_ref, xp_ref)

    @pl.when(i == last)
    def _():
        sort_tile(lpos_s_ref, xs_in_ref)

    def piece_copy(sl, lrow, grow, e, n):
        dst = pl.multiple_of(seg_ref[e] + grow, RUN_ROWS)
        return pltpu.make_async_copy(loc_sc.at[sl, pl.ds(lrow, n)], xs_ref.at[pl.ds(dst, n)], sem.at[sl])

    _for_each_run_piece(tab_ref, tile, lambda lrow, grow, e, n: piece_copy(slot, lrow, grow, e, n).start())

    def drain(tl, sl):
        _drain_units(tot_ref[tl], lambda n: piece_copy(sl, 0, 0, 0, n), loc)

    @pl.when(i > 0)
    def _():
        drain(tile - 1, 1 - slot)

    @pl.when(i == last)
    def _():
        drain(tile, slot)


def _dispatch(tab, seg_start, tot, tails, lpos_p, x1_p, lpos_s, x1_s, *, tile, nrows):
    nt_p = x1_p.shape[0] // tile
    ns = x1_s.shape[0]
    loc = tile * TOP_K + N_EXPERTS * RUN_ROWS
    prompt_blk = lambda i, *_: jnp.minimum(i, nt_p - 1)
    return pl.pallas_call(
        _dispatch_kernel,
        grid_spec=pltpu.PrefetchScalarGridSpec(
            num_scalar_prefetch=4,
            grid=(nt_p + 1,),
            in_specs=[pl.BlockSpec((TOP_K, tile), lambda i, *_: (0, prompt_blk(i))),
                      pl.BlockSpec((tile, D_MODEL), lambda i, *_: (prompt_blk(i), 0)),
                      pl.BlockSpec((TOP_K, ns), lambda i, *_: (0, 0)),
                      pl.BlockSpec((ns, D_MODEL), lambda i, *_: (0, 0))],
            out_specs=pl.BlockSpec(memory_space=pl.ANY),
            scratch_shapes=[pltpu.VMEM((2, loc, D_MODEL), F32), pltpu.VMEM((MOE_ROWS, D_MODEL), F32),
                            pltpu.SemaphoreType.DMA((2,)), pltpu.SemaphoreType.DMA(())]),
        out_shape=jax.ShapeDtypeStruct((nrows, D_MODEL), F32),
        compiler_params=_cparams(("arbitrary",), VMEM_LIMIT),
        name="dispatch",
    )(tab, seg_start, tot, tails, lpos_p, x1_p, lpos_s, x1_s)


def _deinterleave_matrix():
    pm = np.zeros((MXU_DIM, MXU_DIM), np.float32)
    half = MXU_DIM // 2
    for c in range(half):
        pm[2 * c, c] = 1.0
        pm[2 * c + 1, half + c] = 1.0
    return pm


def _expert_kernel(be_ref, nu_ref, nv_ref, ord_ref, nxt_ref, xs_ref, w1_hbm, b1_ref, w2_hbm, b2_ref, pm_ref, y_ref,
                   w1f_sc, w2f_sc, w1p_sc, w2b_sc, sem):
    del nu_ref
    i = pl.program_id(0)
    e = be_ref[i]
    prev = be_ref[jnp.maximum(i - 1, 0)]
    nblk = 2 * D_FF // MXU_DIM

    def weight_copies(expert, slot):
        return (pltpu.make_async_copy(w1_hbm.at[expert], w1f_sc.at[slot], sem.at[0, slot]),
                pltpu.make_async_copy(w2_hbm.at[expert], w2f_sc.at[slot], sem.at[1, slot]))

    @pl.when(i == 0)
    def _():
        for cp in weight_copies(e, 0):
            cp.start()

    @pl.when((i == 0) | (e != prev))
    def _():
        slot = ord_ref[i] % 2
        for cp in weight_copies(e, slot):
            cp.wait()

        for cb in range(nblk):
            blk = w1f_sc[slot, :, cb * MXU_DIM:(cb + 1) * MXU_DIM].astype(BF16)
            w1p_sc[:, cb * MXU_DIM:(cb + 1) * MXU_DIM] = jnp.dot(
                blk, pm_ref[...], preferred_element_type=F32).astype(BF16)
        w2b_sc[...] = w2f_sc[slot].astype(BF16)

        nxt = nxt_ref[i]

        @pl.when(nxt >= 0)
        def _():
            for cp in weight_copies(nxt, 1 - slot):
                cp.start()

    for blk in range(MOE_STEP_BLOCKS):
        rows = slice(blk * MOE_ROWS, (blk + 1) * MOE_ROWS)

        @pl.when(blk < nv_ref[i])
        def _():
            x = xs_ref[rows, :].astype(BF16)
            h = jnp.dot(x, w1p_sc[...], preferred_element_type=F32) + b1_ref[0]
            half = MXU_DIM // 2
            acts = []
            for cb in range(nblk):
                x_glu = jnp.minimum(h[:, cb * MXU_DIM:cb * MXU_DIM + half], SWIGLU_LIMIT)
                x_lin = jnp.clip(h[:, cb * MXU_DIM + half:(cb + 1) * MXU_DIM], -SWIGLU_LIMIT, SWIGLU_LIMIT)
                acts.append((x_glu * jax.nn.sigmoid(SWIGLU_ALPHA * x_glu) * (x_lin + 1.0)).astype(BF16))
            act = jnp.concatenate(acts, axis=1)
            y_ref[rows, :] = jnp.dot(act, w2b_sc[...], preferred_element_type=F32) + b2_ref[0]

        @pl.when(blk >= nv_ref[i])
        def _():
            y_ref[rows, :] = jnp.zeros((MOE_ROWS, D_MODEL), F32)


def _experts(block_e, n_used, n_valid, run_ord, run_next, xs, w1, b1p, w2, b2, pm):
    nrows = xs.shape[0]
    step_rows = MOE_STEP_BLOCKS * MOE_ROWS
    nb = nrows // step_rows
    return pl.pallas_call(
        _expert_kernel,
        grid_spec=pltpu.PrefetchScalarGridSpec(
            num_scalar_prefetch=5,
            grid=(nb,),
            in_specs=[pl.BlockSpec((step_rows, D_MODEL), lambda i, be, nu, *_: (jnp.minimum(i, nu[0] - 1), 0)),
                      pl.BlockSpec(memory_space=pl.ANY),
                      pl.BlockSpec((1, 1, 2 * D_FF), lambda i, be, *_: (be[i], 0, 0)),
                      pl.BlockSpec(memory_space=pl.ANY),
                      pl.BlockSpec((1, 1, D_MODEL), lambda i, be, *_: (be[i], 0, 0)),
                      pl.BlockSpec((MXU_DIM, MXU_DIM), lambda i, *_: (0, 0))],
            out_specs=pl.BlockSpec((step_rows, D_MODEL), lambda i, *_: (i, 0)),
            scratch_shapes=[pltpu.VMEM((2, D_MODEL, 2 * D_FF), F32), pltpu.VMEM((2, D_FF, D_MODEL), F32),
                            pltpu.VMEM((D_MODEL, 2 * D_FF), BF16), pltpu.VMEM((D_FF, D_MODEL), BF16),
                            pltpu.SemaphoreType.DMA((2, 2))]),
        out_shape=jax.ShapeDtypeStruct((nrows, D_MODEL), F32),
        compiler_params=_cparams(("arbitrary",), VMEM_LIMIT_LARGE),
        name="experts",
    )(block_e, n_used, n_valid, run_ord, run_next, xs, w1, b1p, w2, b2, pm)


def _combine_kernel(tab_ref, seg_ref, tot_ref, cols_ref, x1_ref, g2_ref, b2_ref, ys_ref, y_ref, loc_sc, sem,
                    *, tile_base):
    i = pl.program_id(0)
    last = pl.num_programs(0) - 1
    tile = i + tile_base
    slot = i % 2
    loc, tm = loc_sc.shape[1], x1_ref.shape[0]

    def piece_copy(sl, lrow, grow, e, n):
        src = pl.multiple_of(seg_ref[e] + grow, RUN_ROWS)
        return pltpu.make_async_copy(ys_ref.at[pl.ds(src, n)], loc_sc.at[sl, pl.ds(lrow, n)], sem.at[sl])

    def gather(tl, sl):
        _for_each_run_piece(tab_ref, tl, lambda lrow, grow, e, n: piece_copy(sl, lrow, grow, e, n).start())

    @pl.when(i == 0)
    def _():
        loc_sc[...] = jnp.zeros_like(loc_sc)
        gather(tile, slot)

    @pl.when(i < last)
    def _():
        gather(tile + 1, 1 - slot)

    _drain_units(tot_ref[tile], lambda n: piece_copy(slot, 0, 0, 0, n), loc)

    cols = cols_ref[...]
    lane = lax.broadcasted_iota(jnp.int32, (tm, loc), 1)
    weights = jnp.zeros((tm, loc), F32)
    for k in range(TOP_K):
        weights = jnp.where(lane == cols[:, k:k + 1].astype(jnp.int32), cols[:, TOP_K + k:TOP_K + k + 1], weights)
    ffn = jnp.dot(weights.astype(BF16), loc_sc[slot].astype(BF16), preferred_element_type=F32)
    y_ref[...] = _layer_norm(DEEPNORM_ALPHA * x1_ref[...] + ffn, g2_ref[...], b2_ref[...])


def _combine(tab, seg_start, tot, cols, x1, g2, b2, ys, *, tile, tile_base):
    n = x1.shape[0]
    loc = tile * TOP_K + N_EXPERTS * RUN_ROWS
    return pl.pallas_call(
        functools.partial(_combine_kernel, tile_base=tile_base),
        grid_spec=pltpu.PrefetchScalarGridSpec(
            num_scalar_prefetch=3,
            grid=(n // tile,),
            in_specs=[pl.BlockSpec((tile, 2 * TOP_K), lambda i, *_: (i, 0)),
                      pl.BlockSpec((tile, D_MODEL), lambda i, *_: (i, 0)),
                      pl.BlockSpec((1, D_MODEL), lambda i, *_: (0, 0)),
                      pl.BlockSpec((1, D_MODEL), lambda i, *_: (0, 0)),
                      pl.BlockSpec(memory_space=pl.ANY)],
            out_specs=pl.BlockSpec((tile, D_MODEL), lambda i, *_: (i, 0)),
            scratch_shapes=[pltpu.VMEM((2, loc, D_MODEL), F32), pltpu.SemaphoreType.DMA((2,))]),
        out_shape=jax.ShapeDtypeStruct((n, D_MODEL), F32),
        compiler_params=_cparams(("arbitrary",), VMEM_LIMIT),
        name="combine",
    )(tab, seg_start, tot, cols, x1, g2, b2, ys)


def kernel(x_prompt, x_sample, cache_k_win, cache_v_win, state_ssm_re, state_ssm_im, w_in, b_in, attn_sinks,
           w_attn_out, ssm_a_re, ssm_a_im, ssm_log_dt, ssm_b_re, ssm_b_im, ssm_c_re, ssm_c_im, ssm_d, w_ssm_out,
           w_gate, b_gate, w_out, ln1_g, ln1_b, w_router, b_router, w_exp1, b_exp1, w_exp2, b_exp2, ln2_g, ln2_b):
    assert w_in.shape[0] == DEPTH == 1
    bsz, seq, _ = x_prompt.shape
    nsamp = x_sample.shape[0]
    assert x_sample.shape[1] == 1
    n_p = bsz * seq
    n_tok = n_p + nsamp

    xp = x_prompt.reshape(n_p, D_MODEL)
    xsm = x_sample.reshape(nsamp, D_MODEL)
    b_in2 = b_in[0].reshape(1, D_IN)
    sinks = attn_sinks[0].astype(F32)

    q_p, k_p, v_p, u_p = _proj(xp, w_in[0].astype(BF16), b_in2, tile=1024, exact_f32=False, q_dtype=BF16)
    q_s, k_s, v_s, u_s = _proj(xsm, w_in[0], b_in2, tile=nsamp, exact_f32=True, q_dtype=F32)

    o_p = _attn_prompt(sinks, q_p.reshape(bsz, seq, D_ATTN), k_p.reshape(bsz, seq, D_KV),
                       v_p.reshape(bsz, seq, D_KV)).reshape(n_p, D_ATTN)
    k_buf = cache_k_win[0].reshape(nsamp, WINDOW, D_KV)
    v_buf = cache_v_win[0].reshape(nsamp, WINDOW, D_KV)
    o_s, k_next, v_next = _attn_sample(sinks, q_s, k_s, v_s, k_buf, v_buf)

    sp = _s5_params(ssm_a_re[0], ssm_a_im[0], ssm_log_dt[0], ssm_b_re[0], ssm_b_im[0], ssm_c_re[0], ssm_c_im[0])
    y_p, hp_re, hp_im = _s5_prompt(u_p, bsz, seq, _s5_chunk_mats(sp, ssm_d[0]))
    y_s, hs_re, hs_im = _s5_sample(u_s, state_ssm_re[0].reshape(nsamp, -1), state_ssm_im[0].reshape(nsamp, -1),
                                   _s5_sample_mats(sp, ssm_d[0]))

    wm = dict(wao=w_attn_out[0].astype(BF16), wso=w_ssm_out[0].astype(BF16), wg=w_gate[0].astype(BF16),
              bg=b_gate[0].reshape(1, -1), wo=w_out[0].astype(BF16), g1=ln1_g[0].reshape(1, -1),
              b1=ln1_b[0].reshape(1, -1), wrt=w_router[0].T, brt=b_router[0].reshape(-1, 1))
    wm_f32 = dict(wm, wao=w_attn_out[0], wso=w_ssm_out[0], wg=w_gate[0], wo=w_out[0])
    carry0 = jnp.zeros((SUBLANES, LANES), F32)
    x1_p, lpos_p, cols_p, tab_p, carry1 = _merge(xp, o_p, y_p, carry0, wm, tile=MERGE_TILE, route_tile=TOK_TILE,
                                                 f32_matmuls=False)
    x1_s, lpos_s, cols_s, tab_s, carry2 = _merge(xsm, o_s, y_s, carry1, wm_f32, tile=nsamp, route_tile=nsamp,
                                                 f32_matmuls=True)

    nt_p = n_p // TOK_TILE
    tab = jnp.concatenate([tab_p[:, :TAB_ROWS, :N_EXPERTS], tab_s[:, :TAB_ROWS, :N_EXPERTS]], axis=0)
    tot = jnp.sum(tab[:, 0, :], axis=1).astype(jnp.int32)
    tab = tab.reshape(-1)
    seg_rows = carry2[0, :N_EXPERTS].astype(jnp.int32) * RUN_ROWS
    step_rows = MOE_STEP_BLOCKS * MOE_ROWS
    padded = ((seg_rows + step_rows - 1) // step_rows) * step_rows
    pad_end = jnp.cumsum(padded)
    pad_start = (pad_end - padded).astype(jnp.int32)
    seg_end = pad_start + seg_rows
    n_runs = (nt_p + 1) * N_EXPERTS
    nb_max = (n_tok * TOP_K + n_runs * (RUN_ROWS - 1) + N_EXPERTS * (step_rows - 1) + step_rows - 1) // step_rows
    n_used = (pad_end[-1] // step_rows).astype(jnp.int32)
    tails = jnp.concatenate([seg_end // RUN_ROWS, (padded - seg_rows) // RUN_ROWS,
                             (pad_end[-1:] // MOE_ROWS)]).astype(jnp.int32)
    blk_start = jnp.arange(nb_max, dtype=jnp.int32) * step_rows
    blk_e = jnp.minimum(jnp.sum(blk_start[:, None] >= pad_end[None, :], axis=1), N_EXPERTS - 1).astype(jnp.int32)
    used = jnp.arange(nb_max) < n_used
    blk_e = jnp.where(used, blk_e, jnp.max(jnp.where(used, blk_e, 0)))
    ids = jnp.arange(N_EXPERTS, dtype=jnp.int32)
    of_blk = blk_e[:, None] == ids[None, :]
    n_valid = jnp.clip((jnp.sum(jnp.where(of_blk, seg_end[None, :], 0), axis=1) - blk_start + MOE_ROWS - 1)
                       // MOE_ROWS, 0, MOE_STEP_BLOCKS)
    n_valid = jnp.where(used, n_valid, 0).astype(jnp.int32)
    new_run = jnp.concatenate([jnp.ones((1,), jnp.int32), (blk_e[1:] != blk_e[:-1]).astype(jnp.int32)])
    run_ord = (jnp.cumsum(new_run) - 1).astype(jnp.int32)
    later = (ids[None, :] > ids[:, None]) & (padded > 0)[None, :]
    next_e = jnp.min(jnp.where(later, ids[None, :], N_EXPERTS), axis=1)
    next_e = jnp.where(next_e < N_EXPERTS, next_e, -1).astype(jnp.int32)
    run_next = jnp.sum(jnp.where(of_blk, next_e[None, :], 0), axis=1).astype(jnp.int32)

    nrows = nb_max * step_rows
    xs = _dispatch(tab, pad_start, tot, tails, lpos_p, x1_p, lpos_s, x1_s, tile=TOK_TILE, nrows=nrows)

    b1p = b_exp1[0].reshape(N_EXPERTS, 2 * D_FF // MXU_DIM, MXU_DIM // 2, 2)
    b1p = jnp.swapaxes(b1p, 2, 3).reshape(N_EXPERTS, 1, 2 * D_FF)
    ys = _experts(blk_e, n_used.reshape(1), n_valid, run_ord, run_next, xs, w_exp1[0], b1p, w_exp2[0],
                  b_exp2[0].reshape(N_EXPERTS, 1, D_MODEL),
                  jnp.asarray(_deinterleave_matrix(), BF16))

    g2, b2 = ln2_g[0].reshape(1, -1), ln2_b[0].reshape(1, -1)
    y_prompt = _combine(tab, pad_start, tot, cols_p, x1_p, g2, b2, ys, tile=TOK_TILE, tile_base=0)
    y_sample = _combine(tab, pad_start, tot, cols_s, x1_s, g2, b2, ys, tile=nsamp, tile_base=nt_p)

    k_p4 = k_p.reshape(bsz, seq, D_KV)[:, -WINDOW:].reshape(bsz, WINDOW, N_KV_HEADS, HEAD_DIM)
    v_p4 = v_p.reshape(bsz, seq, D_KV)[:, -WINDOW:].reshape(bsz, WINDOW, N_KV_HEADS, HEAD_DIM)
    k_s4 = k_next.reshape(nsamp, WINDOW, N_KV_HEADS, HEAD_DIM)
    v_s4 = v_next.reshape(nsamp, WINDOW, N_KV_HEADS, HEAD_DIM)
    st = lambda a, n: a.reshape(1, n, N_SSM_GROUPS, SSM_STATE)
    return (y_prompt.reshape(bsz, seq, D_MODEL), y_sample.reshape(nsamp, 1, D_MODEL),
            k_p4[None], v_p4[None], st(hp_re, bsz), st(hp_im, bsz),
            k_s4[None], v_s4[None], st(hs_re, nsamp), st(hs_im, nsamp))
```

```python
import functools

import numpy as np
import jax
import jax.numpy as jnp
from jax import lax
from jax.experimental import pallas as pl
from jax.experimental.pallas import tpu as pltpu

F32 = jnp.float32
BF16 = jnp.bfloat16

D_MODEL = 1024
HEAD_DIM = 64
N_Q_HEADS = 8
N_KV_HEADS = 2
Q_PER_KV = N_Q_HEADS // N_KV_HEADS
D_ATTN = N_Q_HEADS * HEAD_DIM
D_KV = N_KV_HEADS * HEAD_DIM
WINDOW = 128
ATTN_SCALE = HEAD_DIM ** -0.5
SSM_GROUP = 16
D_SSM = D_MODEL // 2
N_SSM_GROUPS = D_SSM // SSM_GROUP
SSM_STATE = 64
D_IN = D_ATTN + 2 * D_KV + D_SSM
N_EXPERTS = 32
TOP_K = 4
D_FF = D_MODEL
SWIGLU_LIMIT = 7.0
SWIGLU_ALPHA = 1.702
LN_EPS = 1e-5
DEPTH = 1
DEEPNORM_ALPHA = (2 * DEPTH) ** 0.25

LANES = 128
SUBLANES = 8
MXU_DIM = 256

S5_CHUNK = MXU_DIM // SSM_GROUP
S5_LANE_GROUPS = LANES // SSM_GROUP
MOE_ROWS = 256
MOE_STEP_BLOCKS = 2
TOK_TILE = 256
MERGE_TILE = 512
VMEM_LIMIT = 48 * 1024 * 1024
VMEM_LIMIT_LARGE = 56 * 1024 * 1024


def _cparams(sem, vmem=None):
    return pltpu.CompilerParams(dimension_semantics=sem, vmem_limit_bytes=vmem)


def _split_bf16(v):
    hi = v.astype(BF16)
    return hi, (v - hi.astype(F32)).astype(BF16)


def _dot_split(a, b, dims=(((1,), (0,)), ((), ()))):
    a_hi, a_lo = _split_bf16(a)
    b_hi, b_lo = _split_bf16(b)
    dot = lambda p, q: lax.dot_general(p, q, dims, preferred_element_type=F32)
    return dot(a_hi, b_hi) + dot(a_hi, b_lo) + dot(a_lo, b_hi)


def _proj_kernel(x_ref, w_ref, b_ref, q_ref, k_ref, v_ref, u_ref, *, exact_f32):
    if exact_f32:
        h = jnp.dot(x_ref[...], w_ref[...], preferred_element_type=F32, precision=lax.Precision.HIGHEST)
    else:
        h = jnp.dot(x_ref[...].astype(BF16), w_ref[...], preferred_element_type=F32)
    h = h + b_ref[...]
    q_ref[...] = (h[:, :D_ATTN] * ATTN_SCALE).astype(q_ref.dtype)
    k_ref[...] = h[:, D_ATTN:D_ATTN + D_KV]
    v_ref[...] = h[:, D_ATTN + D_KV:D_ATTN + 2 * D_KV]
    u_ref[...] = h[:, D_ATTN + 2 * D_KV:].astype(u_ref.dtype)


def _proj(x, w, b, *, tile, exact_f32, q_dtype):
    n = x.shape[0]
    return pl.pallas_call(
        functools.partial(_proj_kernel, exact_f32=exact_f32),
        grid=(n // tile,),
        in_specs=[pl.BlockSpec((tile, D_MODEL), lambda i: (i, 0)),
                  pl.BlockSpec((D_MODEL, D_IN), lambda i: (0, 0)),
                  pl.BlockSpec((1, D_IN), lambda i: (0, 0))],
        out_specs=[pl.BlockSpec((tile, D_ATTN), lambda i: (i, 0)),
                   pl.BlockSpec((tile, D_KV), lambda i: (i, 0)),
                   pl.BlockSpec((tile, D_KV), lambda i: (i, 0)),
                   pl.BlockSpec((tile, D_SSM), lambda i: (i, 0))],
        out_shape=[jax.ShapeDtypeStruct((n, D_ATTN), q_dtype),
                   jax.ShapeDtypeStruct((n, D_KV), F32),
                   jax.ShapeDtypeStruct((n, D_KV), F32),
                   jax.ShapeDtypeStruct((n, D_SSM), F32)],
        compiler_params=_cparams(("parallel",), VMEM_LIMIT),
        name="proj",
    )(x, w, b)


ATT_Q_TILE = 512


def _attn_prompt_kernel(sink_ref, q_ref, k_ref, v_ref, o_ref):
    i = pl.program_id(1)
    nk, nq = 2 * WINDOW, 2 * WINDOW
    lo = lax.broadcasted_iota(jnp.int32, (nk, LANES), 1) < HEAD_DIM
    top = lax.broadcasted_iota(jnp.int32, (nq, 1), 0) < WINDOW
    for blk in range(ATT_Q_TILE // WINDOW):
        q0 = i * ATT_Q_TILE + blk * WINDOW
        k0 = pl.multiple_of(jnp.maximum(q0 - WINDOW, 0), WINDOW)
        kk = k_ref[0, pl.ds(k0, nk), :]
        vv = v_ref[0, pl.ds(k0, nk), :]
        kk_sw = pltpu.roll(kk, HEAD_DIM, axis=1)
        vv_sw = pltpu.roll(vv, HEAD_DIM, axis=1)
        k_var = [[jnp.where(lo, kk, 0.0).astype(BF16), jnp.where(lo, 0.0, kk_sw).astype(BF16)],
                 [jnp.where(lo, kk_sw, 0.0).astype(BF16), jnp.where(lo, 0.0, kk).astype(BF16)]]
        v_var = [[jnp.where(lo, vv, 1.0).astype(BF16), jnp.where(lo, 1.0, vv_sw).astype(BF16)],
                 [jnp.where(lo, vv_sw, 1.0).astype(BF16), jnp.where(lo, 1.0, vv).astype(BF16)]]
        qpos = q0 + lax.broadcasted_iota(jnp.int32, (nq, nk), 0) % WINDOW
        kpos = k0 + lax.broadcasted_iota(jnp.int32, (nq, nk), 1)
        valid = (kpos <= qpos) & (qpos - kpos <= WINDOW)
        rows = slice(blk * WINDOW, (blk + 1) * WINDOW)
        for kv in range(N_KV_HEADS):
            pairs = (2 * kv, 2 * kv + 1)
            qs = jnp.concatenate([q_ref[0, rows, pr * LANES:(pr + 1) * LANES] for pr in pairs], axis=0)
            outs = []
            for parity in range(2):
                sink = jnp.where(top, sink_ref[2 * pairs[0] + parity], sink_ref[2 * pairs[1] + parity])
                s = lax.dot_general(qs, k_var[kv][parity], (((1,), (1,)), ((), ())), preferred_element_type=F32)
                s = jnp.where(valid, s, -jnp.inf)
                m = jnp.maximum(jnp.max(s, axis=-1, keepdims=True), sink)
                p = jnp.exp(s - m).astype(BF16)
                acc = jnp.dot(p, v_var[kv][parity], preferred_element_type=F32)
                outs.append(acc / (pltpu.roll(acc, HEAD_DIM, axis=1) + jnp.exp(sink - m)))
            o = jnp.where(lo, outs[0], outs[1]).astype(o_ref.dtype)
            for j, pr in enumerate(pairs):
                o_ref[0, rows, pr * LANES:(pr + 1) * LANES] = o[j * WINDOW:(j + 1) * WINDOW]


def _attn_prompt(sinks, q, k, v):
    bsz, seq = q.shape[0], q.shape[1]
    return pl.pallas_call(
        _attn_prompt_kernel,
        grid=(bsz, seq // ATT_Q_TILE),
        in_specs=[pl.BlockSpec(memory_space=pltpu.SMEM),
                  pl.BlockSpec((1, ATT_Q_TILE, D_ATTN), lambda b, i: (b, i, 0)),
                  pl.BlockSpec((1, seq, D_KV), lambda b, i: (b, 0, 0)),
                  pl.BlockSpec((1, seq, D_KV), lambda b, i: (b, 0, 0))],
        out_specs=pl.BlockSpec((1, ATT_Q_TILE, D_ATTN), lambda b, i: (b, i, 0)),
        out_shape=jax.ShapeDtypeStruct((bsz, seq, D_ATTN), BF16),
        compiler_params=_cparams(("parallel", "parallel")),
        name="attn_prompt",
    )(sinks, q, k, v)


ATT_S_GROUP = 16


def _attn_sample_kernel(sink_ref, q_ref, kn_ref, vn_ref, kb_ref, vb_ref, o_ref, knext_ref, vnext_ref):
    g = ATT_S_GROUP
    rows = Q_PER_KV * g
    ncol = g * WINDOW
    for buf_ref, new_ref, next_ref in ((kb_ref, kn_ref, knext_ref), (vb_ref, vn_ref, vnext_ref)):
        next_ref[:, 0:WINDOW - 1, :] = buf_ref[:, 1:WINDOW, :]
        next_ref[:, WINDOW - 1, :] = new_ref[...]
    kb = kb_ref[...].reshape(ncol, D_KV)
    vb = vb_ref[...].reshape(ncol, D_KV)
    rseq = lax.broadcasted_iota(jnp.int32, (rows, ncol), 0) % g
    cseq = lax.broadcasted_iota(jnp.int32, (rows, ncol), 1) // WINDOW
    own = rseq == cseq
    rhead = lax.broadcasted_iota(jnp.int32, (rows, 1), 0) // g
    for kv in range(N_KV_HEADS):
        lo = kv * HEAD_DIM
        qs = jnp.concatenate(
            [q_ref[:, (kv * Q_PER_KV + h) * HEAD_DIM:(kv * Q_PER_KV + h + 1) * HEAD_DIM] for h in range(Q_PER_KV)],
            axis=0)
        kn = jnp.concatenate([kn_ref[:, lo:lo + HEAD_DIM]] * Q_PER_KV, axis=0)
        vn = jnp.concatenate([vn_ref[:, lo:lo + HEAD_DIM]] * Q_PER_KV, axis=0)
        sink = jnp.zeros((rows, 1), F32)
        for h in range(Q_PER_KV):
            sink = jnp.where(rhead == h, sink_ref[kv * Q_PER_KV + h], sink)
        qs = qs.astype(F32)
        s = _dot_split(qs, kb[:, lo:lo + HEAD_DIM], (((1,), (1,)), ((), ())))
        s = jnp.where(own, s, -jnp.inf)
        s_new = jnp.sum(qs * kn, axis=-1, keepdims=True)
        m = jnp.maximum(jnp.maximum(jnp.max(s, axis=-1, keepdims=True), s_new), sink)
        p = jnp.exp(s - m)
        p_new = jnp.exp(s_new - m)
        denom = jnp.sum(p, axis=-1, keepdims=True) + p_new + jnp.exp(sink - m)
        o = (_dot_split(p, vb[:, lo:lo + HEAD_DIM]) + p_new * vn) / denom
        for h in range(Q_PER_KV):
            c0 = (kv * Q_PER_KV + h) * HEAD_DIM
            o_ref[:, c0:c0 + HEAD_DIM] = o[h * g:(h + 1) * g].astype(o_ref.dtype)


def _attn_sample(sinks, q, k_new, v_new, k_buf, v_buf):
    n = q.shape[0]
    g = ATT_S_GROUP
    return pl.pallas_call(
        _attn_sample_kernel,
        grid=(n // g,),
        in_specs=[pl.BlockSpec(memory_space=pltpu.SMEM),
                  pl.BlockSpec((g, D_ATTN), lambda i: (i, 0)),
                  pl.BlockSpec((g, D_KV), lambda i: (i, 0)),
                  pl.BlockSpec((g, D_KV), lambda i: (i, 0)),
                  pl.BlockSpec((g, WINDOW, D_KV), lambda i: (i, 0, 0)),
                  pl.BlockSpec((g, WINDOW, D_KV), lambda i: (i, 0, 0))],
        out_specs=[pl.BlockSpec((g, D_ATTN), lambda i: (i, 0)),
                   pl.BlockSpec((g, WINDOW, D_KV), lambda i: (i, 0, 0)),
                   pl.BlockSpec((g, WINDOW, D_KV), lambda i: (i, 0, 0))],
        out_shape=[jax.ShapeDtypeStruct((n, D_ATTN), F32),
                   jax.ShapeDtypeStruct((n, WINDOW, D_KV), F32),
                   jax.ShapeDtypeStruct((n, WINDOW, D_KV), F32)],
        compiler_params=_cparams(("parallel",)),
        name="attn_sample",
    )(sinks, q, k_new, v_new, k_buf, v_buf)


def _s5_params(a_re, a_im, log_dt, b_re, b_im, c_re, c_im):
    hp = lax.Precision.HIGHEST
    dt = jnp.exp(log_dt.astype(F32))[:, None]
    are, aim = a_re.astype(F32), a_im.astype(F32)
    tau = jnp.arange(S5_CHUNK + 1, dtype=F32)[None, :, None]
    mag = jnp.exp(tau * (dt * are)[:, None, :])
    ang = tau * (dt * aim)[:, None, :]
    pw_re, pw_im = mag * jnp.cos(ang), mag * jnp.sin(ang)
    ab_re, ab_im = pw_re[:, 1], pw_im[:, 1]
    den = are * are + aim * aim
    f_re = ((ab_re - 1.0) * are + ab_im * aim) / den
    f_im = (ab_im * are - (ab_re - 1.0) * aim) / den
    bre, bim = b_re.astype(F32), b_im.astype(F32)
    bb_re = f_re[..., None] * bre - f_im[..., None] * bim
    bb_im = f_re[..., None] * bim + f_im[..., None] * bre
    cre, cim = c_re.astype(F32), c_im.astype(F32)
    return dict(pw_re=pw_re, pw_im=pw_im, ab_re=ab_re, ab_im=ab_im, bb_re=bb_re, bb_im=bb_im,
                c_re=cre, c_im=cim, hp=hp)


def _s5_chunk_mats(sp, d_skip):
    hp = sp["hp"]
    g, t, c, p = N_SSM_GROUPS, S5_CHUNK, SSM_GROUP, SSM_STATE
    pw_re, pw_im = sp["pw_re"], sp["pw_im"]
    ca_re = sp["c_re"][:, None] * pw_re[:, :, None, :] - sp["c_im"][:, None] * pw_im[:, :, None, :]
    ca_im = sp["c_re"][:, None] * pw_im[:, :, None, :] + sp["c_im"][:, None] * pw_re[:, :, None, :]
    kern = (jnp.einsum("gtcp,gpd->gtcd", ca_re[:, :t], sp["bb_re"], precision=hp)
            - jnp.einsum("gtcp,gpd->gtcd", ca_im[:, :t], sp["bb_im"], precision=hp))
    kc = jnp.swapaxes(kern, 2, 3)
    kc = kc.at[:, 0].add(d_skip.astype(F32).reshape(g, 1, c) * jnp.eye(c, dtype=F32)[None])
    rev_re, rev_im = pw_re[:, t - 1::-1][:, :t], pw_im[:, t - 1::-1][:, :t]
    wst_re = rev_re[:, :, None, :] * jnp.swapaxes(sp["bb_re"], 1, 2)[:, None] \
        - rev_im[:, :, None, :] * jnp.swapaxes(sp["bb_im"], 1, 2)[:, None]
    wst_im = rev_re[:, :, None, :] * jnp.swapaxes(sp["bb_im"], 1, 2)[:, None] \
        + rev_im[:, :, None, :] * jnp.swapaxes(sp["bb_re"], 1, 2)[:, None]
    wo_re = jnp.transpose(ca_re[:, 1:t + 1], (0, 3, 1, 2))
    wo_im = -jnp.transpose(ca_im[:, 1:t + 1], (0, 3, 1, 2))
    nv, gl = g // S5_LANE_GROUPS, S5_LANE_GROUPS
    kc, wst_re, wst_im, wo_re, wo_im = lax.optimization_barrier((kc, wst_re, wst_im, wo_re, wo_im))
    kc_c = kc.astype(BF16).reshape(nv, gl * t * c, c)
    ws_c = jnp.concatenate([wst_re, wst_im], axis=-1).astype(BF16).reshape(nv, gl * t * c, 2 * p)
    wo_re_c = wo_re.astype(BF16).reshape(nv, gl * p, t * c)
    wo_im_c = wo_im.astype(BF16).reshape(nv, gl * p, t * c)
    spread_b = np.zeros((c, LANES), np.float32)
    spread_s = np.zeros((2 * p, 2 * gl * p), np.float32)
    spread_o = np.zeros((t * c, t * LANES), np.float32)
    for h in range(gl):
        spread_b[np.arange(c), h * c + np.arange(c)] = 1.0
        for ri in range(2):
            spread_s[ri * p + np.arange(p), ri * gl * p + h * p + np.arange(p)] = 1.0
        for tt in range(t):
            spread_o[tt * c + np.arange(c), tt * LANES + h * c + np.arange(c)] = 1.0
    at_re = pw_re[:, t].reshape(1, g * p)
    at_im = pw_im[:, t].reshape(1, g * p)
    return dict(kc=kc_c, ws=ws_c, wo_re=wo_re_c, wo_im=wo_im_c, spread_b=jnp.asarray(spread_b, BF16),
                spread_s=jnp.asarray(spread_s, BF16), spread_o=jnp.asarray(spread_o, BF16),
                at_re=at_re, at_im=at_im)


def _s5_chunk_rows(u_ref, nchunk):
    return jnp.concatenate(
        [u_ref[pl.ds(s, nchunk, stride=S5_CHUNK), :] for s in range(S5_CHUNK)], axis=1).astype(BF16)


S5_SLABS = S5_LANE_GROUPS * SSM_STATE // LANES


S5_SEQS_PER_STEP = 2
S5_EXPAND_ROWS = 256
S5_C_SHIFT = SSM_GROUP.bit_length() - 1
S5_P_SHIFT = SSM_STATE.bit_length() - 1


def _s5_expand(dst_ref, compact_rows, spread_ref, row_shift, col_shift):
    n_rows, n_cols = dst_ref.shape
    col_g = lax.shift_right_logical(lax.broadcasted_iota(jnp.int32, (S5_EXPAND_ROWS, n_cols), 1), col_shift)
    for r0 in range(0, n_rows, S5_EXPAND_ROWS):
        row_g = lax.shift_right_logical(r0 + lax.broadcasted_iota(jnp.int32, (S5_EXPAND_ROWS, n_cols), 0), row_shift)
        same = ((row_g ^ col_g) & (S5_LANE_GROUPS - 1)) == 0
        blk = jnp.dot(compact_rows(r0), spread_ref[...], preferred_element_type=F32)
        dst_ref[r0:r0 + S5_EXPAND_ROWS, :] = jnp.where(same, blk, 0.0).astype(dst_ref.dtype)


def _s5_group_rows(ref, index):
    per_group = S5_CHUNK * SSM_GROUP
    return jnp.concatenate([ref[0, g * per_group + index * SSM_GROUP:g * per_group + (index + 1) * SSM_GROUP, :]
                            for g in range(S5_LANE_GROUPS)], axis=0)


def _s5_state_kernel(u_ref, ws_ref, spread_ref, sre_ref, sim_ref, wst_sc):
    nchunk = sre_ref.shape[1]

    @pl.when(pl.program_id(1) == 0)
    def _():
        per_call = S5_EXPAND_ROWS // LANES
        _s5_expand(wst_sc,
                   lambda r0: jnp.concatenate([_s5_group_rows(ws_ref, r0 // LANES + j) for j in range(per_call)], axis=0),
                   spread_ref, S5_C_SHIFT, S5_P_SHIFT)

    s = jnp.dot(_s5_chunk_rows(u_ref, nchunk), wst_sc[...], preferred_element_type=F32)
    for k in range(S5_SLABS):
        sre_ref[k] = s[:, k * LANES:(k + 1) * LANES]
        sim_ref[k] = s[:, (S5_SLABS + k) * LANES:(S5_SLABS + k + 1) * LANES]


def _s5_scan_kernel(sre_ref, sim_ref, are_ref, aim_ref, hre_ref, him_ref, fre_ref, fim_ref, *, bsz):
    nchunk = sre_ref.shape[1] // bsz
    are = [jnp.broadcast_to(are_ref[:, k * LANES:(k + 1) * LANES], (bsz, LANES)) for k in range(S5_SLABS)]
    aim = [jnp.broadcast_to(aim_ref[:, k * LANES:(k + 1) * LANES], (bsz, LANES)) for k in range(S5_SLABS)]

    def body(j, carry):
        rows = pl.ds(j, bsz, stride=nchunk)
        out = []
        for k in range(S5_SLABS):
            cre, cim = carry[2 * k], carry[2 * k + 1]
            hre_ref[k, rows, :] = cre
            him_ref[k, rows, :] = cim
            sr = sre_ref[k, rows, :]
            si = sim_ref[k, rows, :]
            out += [are[k] * cre - aim[k] * cim + sr, are[k] * cim + aim[k] * cre + si]
        return tuple(out)

    zero = jnp.zeros((bsz, LANES), F32)
    fin = lax.fori_loop(0, nchunk, body, (zero,) * (2 * S5_SLABS), unroll=4)
    fre_ref[...] = jnp.concatenate(fin[0::2], axis=1)
    fim_ref[...] = jnp.concatenate(fin[1::2], axis=1)


def _s5_out_kernel(u_ref, kc_ref, spread_b_ref, hre_ref, him_ref, wo_re_ref, wo_im_ref, spread_o_ref, y_ref,
                   m_sc, wout_sc):
    nchunk = hre_ref.shape[1]

    @pl.when(pl.program_id(1) == 0)
    def _():
        rg = lax.shift_right_logical(lax.broadcasted_iota(jnp.int32, (LANES, LANES), 0), S5_C_SHIFT)
        cg = lax.shift_right_logical(lax.broadcasted_iota(jnp.int32, (LANES, LANES), 1), S5_C_SHIFT)
        zero_blk = jnp.zeros((LANES, LANES), BF16)
        lag_blk = [jnp.where(rg == cg, jnp.dot(_s5_group_rows(kc_ref, tau), spread_b_ref[...],
                                               preferred_element_type=F32), 0.0).astype(BF16)
                   for tau in range(S5_CHUNK)]
        for s in range(S5_CHUNK):
            for t in range(S5_CHUNK):
                m_sc[s * LANES:(s + 1) * LANES, t * LANES:(t + 1) * LANES] = lag_blk[t - s] if t >= s else zero_blk
        half = wo_re_ref.shape[1]
        _s5_expand(wout_sc,
                   lambda r0: (wo_re_ref[0, r0:r0 + S5_EXPAND_ROWS, :] if r0 < half
                               else wo_im_ref[0, r0 - half:r0 - half + S5_EXPAND_ROWS, :]),
                   spread_o_ref, S5_P_SHIFT, S5_C_SHIFT)

    hcat = jnp.concatenate([hre_ref[k] for k in range(S5_SLABS)] + [him_ref[k] for k in range(S5_SLABS)],
                           axis=1).astype(BF16)
    lhs = _s5_chunk_rows(u_ref, nchunk)
    y = jnp.concatenate(
        [jnp.dot(lhs[:, :j + MXU_DIM], m_sc[:j + MXU_DIM, j:j + MXU_DIM], preferred_element_type=F32)
         for j in range(0, S5_CHUNK * LANES, MXU_DIM)], axis=1)
    y = y + jnp.dot(hcat, wout_sc[...], preferred_element_type=F32)
    for s in range(S5_CHUNK):
        y_ref[pl.ds(s, nchunk, stride=S5_CHUNK), :] = y[:, s * LANES:(s + 1) * LANES]


def _s5_prompt(u, bsz, seq, mats):
    at_re, at_im = mats["at_re"], mats["at_im"]
    g, t, p, c = N_SSM_GROUPS, S5_CHUNK, SSM_STATE, SSM_GROUP
    nchunk = seq // t
    n = nchunk * bsz
    nv = g // S5_LANE_GROUPS
    half = S5_LANE_GROUPS * p
    per_step = S5_SEQS_PER_STEP if bsz % S5_SEQS_PER_STEP == 0 else 1
    s_re, s_im = pl.pallas_call(
        _s5_state_kernel,
        grid=(nv, bsz // per_step),
        in_specs=[pl.BlockSpec((per_step * seq, LANES), lambda v, b: (b, v)),
                  pl.BlockSpec((1, t * LANES, 2 * p), lambda v, b: (v, 0, 0)),
                  pl.BlockSpec((2 * p, 2 * half), lambda v, b: (0, 0))],
        out_specs=[pl.BlockSpec((S5_SLABS, per_step * nchunk, LANES), lambda v, b: (v, b, 0)),
                   pl.BlockSpec((S5_SLABS, per_step * nchunk, LANES), lambda v, b: (v, b, 0))],
        out_shape=[jax.ShapeDtypeStruct((nv * S5_SLABS, n, LANES), F32)] * 2,
        scratch_shapes=[pltpu.VMEM((t * LANES, 2 * half), BF16)],
        compiler_params=_cparams(("parallel", "arbitrary"), VMEM_LIMIT),
        name="s5_state",
    )(u, mats["ws"], mats["spread_s"])
    h_re, h_im, f_re, f_im = pl.pallas_call(
        functools.partial(_s5_scan_kernel, bsz=bsz),
        grid=(nv,),
        in_specs=[pl.BlockSpec((S5_SLABS, n, LANES), lambda i: (i, 0, 0)),
                  pl.BlockSpec((S5_SLABS, n, LANES), lambda i: (i, 0, 0)),
                  pl.BlockSpec((1, half), lambda i: (0, i)),
                  pl.BlockSpec((1, half), lambda i: (0, i))],
        out_specs=[pl.BlockSpec((S5_SLABS, n, LANES), lambda i: (i, 0, 0)),
                   pl.BlockSpec((S5_SLABS, n, LANES), lambda i: (i, 0, 0)),
                   pl.BlockSpec((bsz, half), lambda i: (0, i)),
                   pl.BlockSpec((bsz, half), lambda i: (0, i))],
        out_shape=[jax.ShapeDtypeStruct((nv * S5_SLABS, n, LANES), F32)] * 2
        + [jax.ShapeDtypeStruct((bsz, g * p), F32)] * 2,
        compiler_params=_cparams(("parallel",)),
        name="s5_scan",
    )(s_re, s_im, at_re, at_im)
    y = pl.pallas_call(
        _s5_out_kernel,
        grid=(nv, bsz // per_step),
        in_specs=[pl.BlockSpec((per_step * seq, LANES), lambda v, b: (b, v)),
                  pl.BlockSpec((1, t * LANES, c), lambda v, b: (v, 0, 0)),
                  pl.BlockSpec((c, LANES), lambda v, b: (0, 0)),
                  pl.BlockSpec((S5_SLABS, per_step * nchunk, LANES), lambda v, b: (v, b, 0)),
                  pl.BlockSpec((S5_SLABS, per_step * nchunk, LANES), lambda v, b: (v, b, 0)),
                  pl.BlockSpec((1, half, t * c), lambda v, b: (v, 0, 0)),
                  pl.BlockSpec((1, half, t * c), lambda v, b: (v, 0, 0)),
                  pl.BlockSpec((t * c, t * LANES), lambda v, b: (0, 0))],
        out_specs=pl.BlockSpec((per_step * seq, LANES), lambda v, b: (b, v)),
        out_shape=jax.ShapeDtypeStruct((bsz * seq, D_SSM), F32),
        scratch_shapes=[pltpu.VMEM((t * LANES, t * LANES), BF16), pltpu.VMEM((2 * half, t * LANES), BF16)],
        compiler_params=_cparams(("parallel", "arbitrary"), VMEM_LIMIT),
        name="s5_out",
    )(u, mats["kc"], mats["spread_b"], h_re, h_im, mats["wo_re"], mats["wo_im"], mats["spread_o"])
    return y, f_re, f_im


S5S_GROUPS = LANES // SSM_GROUP


def _s5_sample_mats(sp, d_skip):
    go, gl, c, p = N_SSM_GROUPS // S5S_GROUPS, S5S_GROUPS, SSM_GROUP, SSM_STATE
    eye = jnp.eye(gl, dtype=F32)

    def bdiag_in(b):
        b4 = b.reshape(go, gl, p, c)
        return jnp.einsum("ogpc,gh->ogchp", b4, eye).reshape(go, gl * c, gl * p)

    def bdiag_out(cm):
        c4 = cm.reshape(go, gl, c, p)
        return jnp.einsum("ogcp,gh->ogphc", c4, eye).reshape(go, gl * p, gl * c)

    b8 = jnp.concatenate([bdiag_in(sp["bb_re"]), bdiag_in(sp["bb_im"])], axis=2)
    c8 = jnp.concatenate([bdiag_out(sp["c_re"]), -bdiag_out(sp["c_im"])], axis=1)
    a_re = sp["ab_re"].reshape(1, N_SSM_GROUPS * p)
    a_im = sp["ab_im"].reshape(1, N_SSM_GROUPS * p)
    return b8, c8, a_re, a_im, d_skip.astype(F32).reshape(1, D_SSM)


def _s5_sample_kernel(u_ref, hre_ref, him_ref, b8_ref, c8_ref, are_ref, aim_ref, d_ref,
                      y_ref, ore_ref, oim_ref):
    hp = lax.Precision.HIGHEST
    u = u_ref[...]
    half = S5S_GROUPS * SSM_STATE
    bu = jnp.dot(u, b8_ref[0], preferred_element_type=F32, precision=hp)
    are, aim = are_ref[...], aim_ref[...]
    h0r, h0i = hre_ref[...], him_ref[...]
    hr = are * h0r - aim * h0i + bu[:, :half]
    hi = are * h0i + aim * h0r + bu[:, half:]
    ore_ref[...] = hr
    oim_ref[...] = hi
    y = jnp.dot(jnp.concatenate([hr, hi], axis=1), c8_ref[0], preferred_element_type=F32, precision=hp)
    y_ref[...] = (y + d_ref[...] * u).astype(y_ref.dtype)


def _s5_sample(u, h0_re, h0_im, mats):
    b8, c8, a_re, a_im, d = mats
    n = u.shape[0]
    half = S5S_GROUPS * SSM_STATE
    return pl.pallas_call(
        _s5_sample_kernel,
        grid=(N_SSM_GROUPS // S5S_GROUPS,),
        in_specs=[pl.BlockSpec((n, LANES), lambda i: (0, i)),
                  pl.BlockSpec((n, half), lambda i: (0, i)),
                  pl.BlockSpec((n, half), lambda i: (0, i)),
                  pl.BlockSpec((1, LANES, 2 * half), lambda i: (i, 0, 0)),
                  pl.BlockSpec((1, 2 * half, LANES), lambda i: (i, 0, 0)),
                  pl.BlockSpec((1, half), lambda i: (0, i)),
                  pl.BlockSpec((1, half), lambda i: (0, i)),
                  pl.BlockSpec((1, LANES), lambda i: (0, i))],
        out_specs=[pl.BlockSpec((n, LANES), lambda i: (0, i)),
                   pl.BlockSpec((n, half), lambda i: (0, i)),
                   pl.BlockSpec((n, half), lambda i: (0, i))],
        out_shape=[jax.ShapeDtypeStruct((n, D_SSM), F32),
                   jax.ShapeDtypeStruct((n, N_SSM_GROUPS * SSM_STATE), F32),
                   jax.ShapeDtypeStruct((n, N_SSM_GROUPS * SSM_STATE), F32)],
        compiler_params=_cparams(("parallel",)),
        name="s5_sample",
    )(u, h0_re, h0_im, b8, c8, a_re, a_im, d)


def _layer_norm(x, g, b):
    mu = jnp.mean(x, axis=-1, keepdims=True)
    xc = x - mu
    var = jnp.mean(xc * xc, axis=-1, keepdims=True)
    return xc * lax.rsqrt(var + LN_EPS) * g + b


def _sigmoid(x):
    return 0.5 * jnp.tanh(0.5 * x) + 0.5


RUN_ROWS = SUBLANES
TAB_ROWS = 3


def _merge_kernel(x_ref, oa_ref, ys_ref, carry_in_ref, wao_ref, wso_ref, wg_ref, bg_ref, wo_ref,
                  g1_ref, b1_ref, wrt_ref, brt_ref,
                  x1_ref, lpos_ref, cols_ref, tab_ref, carry_out_ref, carry_sc, *, f32_matmuls):
    step = pl.program_id(0)

    @pl.when(step == 0)
    def _():
        carry_sc[...] = carry_in_ref[...]

    def mm(a, w_ref):
        if f32_matmuls:
            return _dot_split(a.astype(F32), w_ref[...])
        return jnp.dot(a.astype(BF16), w_ref[...], preferred_element_type=F32)

    tm = x_ref.shape[0]
    x = x_ref[...]
    branch_a = mm(oa_ref[...], wao_ref)
    z = mm(jax.nn.gelu(ys_ref[...].astype(F32)), wso_ref)
    branch_b = z[:, :D_MODEL] * _sigmoid(z[:, D_MODEL:])
    gates = _sigmoid(mm(x, wg_ref) + bg_ref[...])
    mixed = gates[:, :D_MODEL] * branch_a + gates[:, D_MODEL:] * branch_b
    mix = mm(mixed, wo_ref)
    x1 = _layer_norm(DEEPNORM_ALPHA * x + mix, g1_ref[...], b1_ref[...])
    x1_ref[...] = x1

    split2 = _split_bf16

    def dot_nt(a, b):
        return lax.dot_general(a, b, (((1,), (1,)), ((), ())), preferred_element_type=F32)

    w_hi, w_lo = split2(wrt_ref[...])
    rt = tm // tab_ref.shape[0]
    sub = lax.broadcasted_iota(jnp.int32, (N_EXPERTS, rt), 0)
    r = lax.broadcasted_iota(jnp.int32, (rt, rt), 0)
    c = lax.broadcasted_iota(jnp.int32, (rt, rt), 1)
    er = lax.broadcasted_iota(jnp.int32, (N_EXPERTS, N_EXPERTS), 0)
    ec = lax.broadcasted_iota(jnp.int32, (N_EXPERTS, N_EXPERTS), 1)
    rid = lax.broadcasted_iota(jnp.int32, (SUBLANES, LANES), 0)
    lane_pad = jnp.zeros((SUBLANES, LANES - N_EXPERTS), F32)
    for h in range(tab_ref.shape[0]):
        x_hi, x_lo = split2(x1[h * rt:(h + 1) * rt])
        logits = dot_nt(w_hi, x_hi) + dot_nt(w_hi, x_lo) + dot_nt(w_lo, x_hi) + brt_ref[...]
        work = logits
        vals, sels = [], []
        for _ in range(TOP_K):
            mx = jnp.max(work, axis=0, keepdims=True)
            idx = jnp.min(jnp.where(work == mx, sub, N_EXPERTS), axis=0, keepdims=True)
            sel = sub == idx
            vals.append(mx)
            sels.append(sel)
            work = jnp.where(sel, -jnp.inf, work)
        ex = [jnp.exp(v - vals[0]) for v in vals]
        tot = ex[0] + ex[1] + ex[2] + ex[3]
        gate_rows = jnp.concatenate([e / tot for e in ex], axis=0)

        multi = jnp.zeros((N_EXPERTS, rt), F32)
        for sel in sels:
            multi = multi + jnp.where(sel, 1.0, 0.0)
        multi_b = multi.astype(BF16)
        earlier = jnp.dot(multi_b, jnp.where(r < c, 1.0, 0.0).astype(BF16), preferred_element_type=F32)
        cnt_col = jnp.sum(multi, axis=1, keepdims=True)
        nb_col = jnp.floor((cnt_col + (RUN_ROWS - 1.0)) * (1.0 / RUN_ROWS))
        loff_col = jnp.dot(jnp.where(ec < er, 1.0, 0.0).astype(BF16),
                           jnp.broadcast_to(nb_col, (N_EXPERTS, rt)).astype(BF16), preferred_element_type=F32)
        base = RUN_ROWS * loff_col + earlier
        lpos = jnp.concatenate([jnp.sum(jnp.where(sel, base, 0.0), axis=0, keepdims=True) for sel in sels],
                               axis=0)
        lpos_ref[:, h * rt:(h + 1) * rt] = lpos.astype(jnp.int32)
        rows_hi, rows_lo = split2(jnp.concatenate([lpos, gate_rows], axis=0))
        eye = jnp.where(r == c, 1.0, 0.0).astype(BF16)
        cols_ref[h * rt:(h + 1) * rt, :] = dot_nt(eye, rows_hi) + dot_nt(eye, rows_lo)

        cnt_row = dot_nt(jnp.ones((SUBLANES, rt), BF16), multi_b)
        nb_row = jnp.floor((cnt_row + (RUN_ROWS - 1.0)) * (1.0 / RUN_ROWS))
        loff_row = jnp.dot(nb_row.astype(BF16), jnp.where(er < ec, 1.0, 0.0).astype(BF16),
                           preferred_element_type=F32)
        nb_p = jnp.concatenate([nb_row, lane_pad], axis=1)
        loff_p = jnp.concatenate([loff_row, lane_pad], axis=1)
        goff_p = carry_sc[...]
        tab = jnp.where(rid == 0, nb_p, jnp.where(rid == 1, loff_p, jnp.where(rid == 2, goff_p, 0.0)))
        tab_ref[h] = tab.astype(jnp.int32)
        carry_sc[...] = goff_p + nb_p
    carry_out_ref[...] = carry_sc[...]


def _merge(x, o_attn, y_ssm, carry_in, w, *, tile, route_tile, f32_matmuls):
    n = x.shape[0]
    nt = n // tile
    per_step = tile // route_tile
    full = lambda shape: pl.BlockSpec(shape, lambda i: (0,) * len(shape))
    return pl.pallas_call(
        functools.partial(_merge_kernel, f32_matmuls=f32_matmuls),
        grid=(nt,),
        in_specs=[pl.BlockSpec((tile, D_MODEL), lambda i: (i, 0)),
                  pl.BlockSpec((tile, D_ATTN), lambda i: (i, 0)),
                  pl.BlockSpec((tile, D_SSM), lambda i: (i, 0)),
                  full((SUBLANES, LANES)),
                  full((D_ATTN, D_MODEL)), full((D_SSM, 2 * D_MODEL)), full((D_MODEL, 2 * D_MODEL)),
                  full((1, 2 * D_MODEL)), full((D_MODEL, D_MODEL)),
                  full((1, D_MODEL)), full((1, D_MODEL)),
                  full((N_EXPERTS, D_MODEL)), full((N_EXPERTS, 1))],
        out_specs=[pl.BlockSpec((tile, D_MODEL), lambda i: (i, 0)),
                   pl.BlockSpec((TOP_K, tile), lambda i: (0, i)),
                   pl.BlockSpec((tile, 2 * TOP_K), lambda i: (i, 0)),
                   pl.BlockSpec((per_step, SUBLANES, LANES), lambda i: (i, 0, 0)),
                   full((SUBLANES, LANES))],
        out_shape=[jax.ShapeDtypeStruct((n, D_MODEL), F32),
                   jax.ShapeDtypeStruct((TOP_K, n), jnp.int32),
                   jax.ShapeDtypeStruct((n, 2 * TOP_K), F32),
                   jax.ShapeDtypeStruct((nt * per_step, SUBLANES, LANES), jnp.int32),
                   jax.ShapeDtypeStruct((SUBLANES, LANES), F32)],
        scratch_shapes=[pltpu.VMEM((SUBLANES, LANES), F32)],
        compiler_params=_cparams(("arbitrary",), VMEM_LIMIT_LARGE),
        name="merge",
    )(x, o_attn, y_ssm, carry_in, w["wao"], w["wso"], w["wg"], w["bg"], w["wo"], w["g1"], w["b1"],
      w["wrt"], w["brt"])


def _tab(tab_ref, tile, row, e):
    return tab_ref[(tile * TAB_ROWS + row) * N_EXPERTS + e]


BIG_PIECE = 4 * RUN_ROWS
MAX_UNITS_LOG2 = 8


def _for_each_run_piece(tab_ref, tile, fn):
    def per_expert(e, carry):
        loff = RUN_ROWS * _tab(tab_ref, tile, 1, e)
        goff = RUN_ROWS * _tab(tab_ref, tile, 2, e)
        units = _tab(tab_ref, tile, 0, e)
        n_big = lax.shift_right_logical(units, 2)

        def big(j, c2):
            fn(pl.multiple_of(loff + j * BIG_PIECE, RUN_ROWS), goff + j * BIG_PIECE, e, BIG_PIECE)
            return c2

        lax.fori_loop(0, n_big, big, 0)
        done = n_big * BIG_PIECE

        def small(j, c2):
            fn(pl.multiple_of(loff + done + j * RUN_ROWS, RUN_ROWS), goff + done + j * RUN_ROWS, e, RUN_ROWS)
            return c2

        lax.fori_loop(0, units & 3, small, 0)
        return carry

    lax.fori_loop(0, N_EXPERTS, per_expert, 0)


def _drain_units(units, wait_copy, buffer_rows):
    assert buffer_rows < (RUN_ROWS << MAX_UNITS_LOG2)
    for b in range(MAX_UNITS_LOG2):
        if (RUN_ROWS << b) > buffer_rows:
            break

        @pl.when((lax.shift_right_logical(units, b) & 1) == 1)
        def _():
            wait_copy(RUN_ROWS << b).wait()


def _dispatch_kernel(tab_ref, seg_ref, tot_ref, tail_ref, lpos_p_ref, xp_ref, lpos_s_ref, xs_in_ref, xs_ref,
                     loc_sc, zero_sc, sem, zsem):
    i = pl.program_id(0)
    last = pl.num_programs(0) - 1
    tile = i
    slot = i % 2
    loc = loc_sc.shape[1]

    @pl.when(i == 0)
    def _():
        zero_sc[...] = jnp.zeros_like(zero_sc)

        def tail_copy(e, j):
            row = pl.multiple_of(RUN_ROWS * (tail_ref[e] + j), RUN_ROWS)
            return pltpu.make_async_copy(zero_sc.at[pl.ds(0, RUN_ROWS)], xs_ref.at[pl.ds(row, RUN_ROWS)], zsem)

        def per_expert(e, carry):
            n = tail_ref[N_EXPERTS + e]
            lax.fori_loop(0, n, lambda j, c2: (tail_copy(e, j).start(), c2)[1], 0)
            lax.fori_loop(0, n, lambda j, c2: (tail_copy(e, j).wait(), c2)[1], 0)
            return carry

        lax.fori_loop(0, N_EXPERTS, per_expert, 0)

        def block_copy(b):
            row = pl.multiple_of(b * MOE_ROWS, MOE_ROWS)
            return pltpu.make_async_copy(zero_sc, xs_ref.at[pl.ds(row, MOE_ROWS)], zsem)

        first_unused, n_blocks = tail_ref[2 * N_EXPERTS], xs_ref.shape[0] // MOE_ROWS
        lax.fori_loop(first_unused, n_blocks, lambda b, c2: (block_copy(b).start(), c2)[1], 0)
        lax.fori_loop(first_unused, n_blocks, lambda b, c2: (block_copy(b).wait(), c2)[1], 0)

    def sort_tile(lpos_ref, x_ref):
        tm = x_ref.shape[0]
        rows = lax.broadcasted_iota(jnp.int32, (loc, tm), 0)
        lp = lpos_ref[...]
        onehot = jnp.zeros((loc, tm), F32)
        for k in range(TOP_K):
            onehot = jnp.where(rows == lp[k:k + 1], 1.0, onehot)
        loc_sc[slot] = jnp.dot(onehot.astype(BF16), x_ref[...].astype(BF16), preferred_element_type=F32)

    @pl.when(i < last)
    def _():
        sort_tile(lpos_p_ref, xp_ref)

    @pl.when(i == last)
    def _():
        sort_tile(lpos_s_ref, xs_in_ref)

    def piece_copy(sl, lrow, grow, e, n):
        dst = pl.multiple_of(seg_ref[e] + grow, RUN_ROWS)
        return pltpu.make_async_copy(loc_sc.at[sl, pl.ds(lrow, n)], xs_ref.at[pl.ds(dst, n)], sem.at[sl])

    _for_each_run_piece(tab_ref, tile, lambda lrow, grow, e, n: piece_copy(slot, lrow, grow, e, n).start())

    def drain(tl, sl):
        _drain_units(tot_ref[tl], lambda n: piece_copy(sl, 0, 0, 0, n), loc)

    @pl.when(i > 0)
    def _():
        drain(tile - 1, 1 - slot)

    @pl.when(i == last)
    def _():
        drain(tile, slot)


def _dispatch(tab, seg_start, tot, tails, lpos_p, x1_p, lpos_s, x1_s, *, tile, nrows):
    nt_p = x1_p.shape[0] // tile
    ns = x1_s.shape[0]
    loc = tile * TOP_K + N_EXPERTS * RUN_ROWS
    prompt_blk = lambda i, *_: jnp.minimum(i, nt_p - 1)
    return pl.pallas_call(
        _dispatch_kernel,
        grid_spec=pltpu.PrefetchScalarGridSpec(
            num_scalar_prefetch=4,
            grid=(nt_p + 1,),
            in_specs=[pl.BlockSpec((TOP_K, tile), lambda i, *_: (0, prompt_blk(i))),
                      pl.BlockSpec((tile, D_MODEL), lambda i, *_: (prompt_blk(i), 0)),
                      pl.BlockSpec((TOP_K, ns), lambda i, *_: (0, 0)),
                      pl.BlockSpec((ns, D_MODEL), lambda i, *_: (0, 0))],
            out_specs=pl.BlockSpec(memory_space=pl.ANY),
            scratch_shapes=[pltpu.VMEM((2, loc, D_MODEL), F32), pltpu.VMEM((MOE_ROWS, D_MODEL), F32),
                            pltpu.SemaphoreType.DMA((2,)), pltpu.SemaphoreType.DMA(())]),
        out_shape=jax.ShapeDtypeStruct((nrows, D_MODEL), F32),
        compiler_params=_cparams(("arbitrary",), VMEM_LIMIT),
        name="dispatch",
    )(tab, seg_start, tot, tails, lpos_p, x1_p, lpos_s, x1_s)


def _deinterleave_matrix():
    pm = np.zeros((MXU_DIM, MXU_DIM), np.float32)
    half = MXU_DIM // 2
    for c in range(half):
        pm[2 * c, c] = 1.0
        pm[2 * c + 1, half + c] = 1.0
    return pm


def _expert_kernel(be_ref, nu_ref, nv_ref, ord_ref, nxt_ref, xs_ref, w1_hbm, b1_ref, w2_hbm, b2_ref, pm_ref, y_ref,
                   w1f_sc, w2f_sc, w1p_sc, w2b_sc, sem):
    del nu_ref
    i = pl.program_id(0)
    e = be_ref[i]
    prev = be_ref[jnp.maximum(i - 1, 0)]
    nblk = 2 * D_FF // MXU_DIM

    def weight_copies(expert, slot):
        return (pltpu.make_async_copy(w1_hbm.at[expert], w1f_sc.at[slot], sem.at[0, slot]),
                pltpu.make_async_copy(w2_hbm.at[expert], w2f_sc.at[slot], sem.at[1, slot]))

    @pl.when(i == 0)
    def _():
        for cp in weight_copies(e, 0):
            cp.start()

    @pl.when((i == 0) | (e != prev))
    def _():
        slot = ord_ref[i] % 2
        for cp in weight_copies(e, slot):
            cp.wait()

        for cb in range(nblk):
            blk = w1f_sc[slot, :, cb * MXU_DIM:(cb + 1) * MXU_DIM].astype(BF16)
            w1p_sc[:, cb * MXU_DIM:(cb + 1) * MXU_DIM] = jnp.dot(
                blk, pm_ref[...], preferred_element_type=F32).astype(BF16)
        w2b_sc[...] = w2f_sc[slot].astype(BF16)

        nxt = nxt_ref[i]

        @pl.when(nxt >= 0)
        def _():
            for cp in weight_copies(nxt, 1 - slot):
                cp.start()

    for blk in range(MOE_STEP_BLOCKS):
        rows = slice(blk * MOE_ROWS, (blk + 1) * MOE_ROWS)

        @pl.when(blk < nv_ref[i])
        def _():
            x = xs_ref[rows, :].astype(BF16)
            h = jnp.dot(x, w1p_sc[...], preferred_element_type=F32) + b1_ref[0]
            half = MXU_DIM // 2
            acts = []
            for cb in range(nblk):
                x_glu = jnp.minimum(h[:, cb * MXU_DIM:cb * MXU_DIM + half], SWIGLU_LIMIT)
                x_lin = jnp.clip(h[:, cb * MXU_DIM + half:(cb + 1) * MXU_DIM], -SWIGLU_LIMIT, SWIGLU_LIMIT)
                acts.append((x_glu * jax.nn.sigmoid(SWIGLU_ALPHA * x_glu) * (x_lin + 1.0)).astype(BF16))
            act = jnp.concatenate(acts, axis=1)
            y_ref[rows, :] = jnp.dot(act, w2b_sc[...], preferred_element_type=F32) + b2_ref[0]

        @pl.when(blk >= nv_ref[i])
        def _():
            y_ref[rows, :] = jnp.zeros((MOE_ROWS, D_MODEL), F32)


def _experts(block_e, n_used, n_valid, run_ord, run_next, xs, w1, b1p, w2, b2, pm):
    nrows = xs.shape[0]
    step_rows = MOE_STEP_BLOCKS * MOE_ROWS
    nb = nrows // step_rows
    return pl.pallas_call(
        _expert_kernel,
        grid_spec=pltpu.PrefetchScalarGridSpec(
            num_scalar_prefetch=5,
            grid=(nb,),
            in_specs=[pl.BlockSpec((step_rows, D_MODEL), lambda i, be, nu, *_: (jnp.minimum(i, nu[0] - 1), 0)),
                      pl.BlockSpec(memory_space=pl.ANY),
                      pl.BlockSpec((1, 1, 2 * D_FF), lambda i, be, *_: (be[i], 0, 0)),
                      pl.BlockSpec(memory_space=pl.ANY),
                      pl.BlockSpec((1, 1, D_MODEL), lambda i, be, *_: (be[i], 0, 0)),
                      pl.BlockSpec((MXU_DIM, MXU_DIM), lambda i, *_: (0, 0))],
            out_specs=pl.BlockSpec((step_rows, D_MODEL), lambda i, *_: (i, 0)),
            scratch_shapes=[pltpu.VMEM((2, D_MODEL, 2 * D_FF), F32), pltpu.VMEM((2, D_FF, D_MODEL), F32),
                            pltpu.VMEM((D_MODEL, 2 * D_FF), BF16), pltpu.VMEM((D_FF, D_MODEL), BF16),
                            pltpu.SemaphoreType.DMA((2, 2))]),
        out_shape=jax.ShapeDtypeStruct((nrows, D_MODEL), F32),
        compiler_params=_cparams(("arbitrary",), VMEM_LIMIT_LARGE),
        name="experts",
    )(block_e, n_used, n_valid, run_ord, run_next, xs, w1, b1p, w2, b2, pm)


def _combine_kernel(tab_ref, seg_ref, tot_ref, cols_ref, x1_ref, g2_ref, b2_ref, ys_ref, y_ref, loc_sc, sem,
                    *, tile_base):
    i = pl.program_id(0)
    last = pl.num_programs(0) - 1
    tile = i + tile_base
    slot = i % 2
    loc, tm = loc_sc.shape[1], x1_ref.shape[0]

    def piece_copy(sl, lrow, grow, e, n):
        src = pl.multiple_of(seg_ref[e] + grow, RUN_ROWS)
        return pltpu.make_async_copy(ys_ref.at[pl.ds(src, n)], loc_sc.at[sl, pl.ds(lrow, n)], sem.at[sl])

    def gather(tl, sl):
        _for_each_run_piece(tab_ref, tl, lambda lrow, grow, e, n: piece_copy(sl, lrow, grow, e, n).start())

    @pl.when(i == 0)
    def _():
        loc_sc[...] = jnp.zeros_like(loc_sc)
        gather(tile, slot)

    @pl.when(i < last)
    def _():
        gather(tile + 1, 1 - slot)

    _drain_units(tot_ref[tile], lambda n: piece_copy(slot, 0, 0, 0, n), loc)

    cols = cols_ref[...]
    lane = lax.broadcasted_iota(jnp.int32, (tm, loc), 1)
    weights = jnp.zeros((tm, loc), F32)
    for k in range(TOP_K):
        weights = jnp.where(lane == cols[:, k:k + 1].astype(jnp.int32), cols[:, TOP_K + k:TOP_K + k + 1], weights)
    ffn = jnp.dot(weights.astype(BF16), loc_sc[slot].astype(BF16), preferred_element_type=F32)
    y_ref[...] = _layer_norm(DEEPNORM_ALPHA * x1_ref[...] + ffn, g2_ref[...], b2_ref[...])


def _combine(tab, seg_start, tot, cols, x1, g2, b2, ys, *, tile, tile_base):
    n = x1.shape[0]
    loc = tile * TOP_K + N_EXPERTS * RUN_ROWS
    return pl.pallas_call(
        functools.partial(_combine_kernel, tile_base=tile_base),
        grid_spec=pltpu.PrefetchScalarGridSpec(
            num_scalar_prefetch=3,
            grid=(n // tile,),
            in_specs=[pl.BlockSpec((tile, 2 * TOP_K), lambda i, *_: (i, 0)),
                      pl.BlockSpec((tile, D_MODEL), lambda i, *_: (i, 0)),
                      pl.BlockSpec((1, D_MODEL), lambda i, *_: (0, 0)),
                      pl.BlockSpec((1, D_MODEL), lambda i, *_: (0, 0)),
                      pl.BlockSpec(memory_space=pl.ANY)],
            out_specs=pl.BlockSpec((tile, D_MODEL), lambda i, *_: (i, 0)),
            scratch_shapes=[pltpu.VMEM((2, loc, D_MODEL), F32), pltpu.SemaphoreType.DMA((2,))]),
        out_shape=jax.ShapeDtypeStruct((n, D_MODEL), F32),
        compiler_params=_cparams(("arbitrary",), VMEM_LIMIT),
        name="combine",
    )(tab, seg_start, tot, cols, x1, g2, b2, ys)


def kernel(x_prompt, x_sample, cache_k_win, cache_v_win, state_ssm_re, state_ssm_im, w_in, b_in, attn_sinks,
           w_attn_out, ssm_a_re, ssm_a_im, ssm_log_dt, ssm_b_re, ssm_b_im, ssm_c_re, ssm_c_im, ssm_d, w_ssm_out,
           w_gate, b_gate, w_out, ln1_g, ln1_b, w_router, b_router, w_exp1, b_exp1, w_exp2, b_exp2, ln2_g, ln2_b):
    assert w_in.shape[0] == DEPTH == 1
    bsz, seq, _ = x_prompt.shape
    nsamp = x_sample.shape[0]
    assert x_sample.shape[1] == 1
    n_p = bsz * seq
    n_tok = n_p + nsamp

    xp = x_prompt.reshape(n_p, D_MODEL)
    xsm = x_sample.reshape(nsamp, D_MODEL)
    b_in2 = b_in[0].reshape(1, D_IN)
    sinks = attn_sinks[0].astype(F32)

    q_p, k_p, v_p, u_p = _proj(xp, w_in[0].astype(BF16), b_in2, tile=2048, exact_f32=False, q_dtype=BF16)
    q_s, k_s, v_s, u_s = _proj(xsm, w_in[0], b_in2, tile=nsamp, exact_f32=True, q_dtype=F32)

    o_p = _attn_prompt(sinks, q_p.reshape(bsz, seq, D_ATTN), k_p.reshape(bsz, seq, D_KV),
                       v_p.reshape(bsz, seq, D_KV)).reshape(n_p, D_ATTN)
    k_buf = cache_k_win[0].reshape(nsamp, WINDOW, D_KV)
    v_buf = cache_v_win[0].reshape(nsamp, WINDOW, D_KV)
    o_s, k_next, v_next = _attn_sample(sinks, q_s, k_s, v_s, k_buf, v_buf)

    sp = _s5_params(ssm_a_re[0], ssm_a_im[0], ssm_log_dt[0], ssm_b_re[0], ssm_b_im[0], ssm_c_re[0], ssm_c_im[0])
    y_p, hp_re, hp_im = _s5_prompt(u_p, bsz, seq, _s5_chunk_mats(sp, ssm_d[0]))
    y_s, hs_re, hs_im = _s5_sample(u_s, state_ssm_re[0].reshape(nsamp, -1), state_ssm_im[0].reshape(nsamp, -1),
                                   _s5_sample_mats(sp, ssm_d[0]))

    wm = dict(wao=w_attn_out[0].astype(BF16), wso=w_ssm_out[0].astype(BF16), wg=w_gate[0].astype(BF16),
              bg=b_gate[0].reshape(1, -1), wo=w_out[0].astype(BF16), g1=ln1_g[0].reshape(1, -1),
              b1=ln1_b[0].reshape(1, -1), wrt=w_router[0].T, brt=b_router[0].reshape(-1, 1))
    wm_f32 = dict(wm, wao=w_attn_out[0], wso=w_ssm_out[0], wg=w_gate[0], wo=w_out[0])
    carry0 = jnp.zeros((SUBLANES, LANES), F32)
    x1_p, lpos_p, cols_p, tab_p, carry1 = _merge(xp, o_p, y_p, carry0, wm, tile=MERGE_TILE, route_tile=TOK_TILE,
                                                 f32_matmuls=False)
    x1_s, lpos_s, cols_s, tab_s, carry2 = _merge(xsm, o_s, y_s, carry1, wm_f32, tile=nsamp, route_tile=nsamp,
                                                 f32_matmuls=True)

    nt_p = n_p // TOK_TILE
    tab = jnp.concatenate([tab_p[:, :TAB_ROWS, :N_EXPERTS], tab_s[:, :TAB_ROWS, :N_EXPERTS]], axis=0)
    tot = jnp.sum(tab[:, 0, :], axis=1).astype(jnp.int32)
    tab = tab.reshape(-1)
    seg_rows = carry2[0, :N_EXPERTS].astype(jnp.int32) * RUN_ROWS
    step_rows = MOE_STEP_BLOCKS * MOE_ROWS
    padded = ((seg_rows + step_rows - 1) // step_rows) * step_rows
    pad_end = jnp.cumsum(padded)
    pad_start = (pad_end - padded).astype(jnp.int32)
    seg_end = pad_start + seg_rows
    n_runs = (nt_p + 1) * N_EXPERTS
    nb_max = (n_tok * TOP_K + n_runs * (RUN_ROWS - 1) + N_EXPERTS * (step_rows - 1) + step_rows - 1) // step_rows
    n_used = (pad_end[-1] // step_rows).astype(jnp.int32)
    tails = jnp.concatenate([seg_end // RUN_ROWS, (padded - seg_rows) // RUN_ROWS,
                             (pad_end[-1:] // MOE_ROWS)]).astype(jnp.int32)
    blk_start = jnp.arange(nb_max, dtype=jnp.int32) * step_rows
    blk_e = jnp.minimum(jnp.sum(blk_start[:, None] >= pad_end[None, :], axis=1), N_EXPERTS - 1).astype(jnp.int32)
    used = jnp.arange(nb_max) < n_used
    blk_e = jnp.where(used, blk_e, jnp.max(jnp.where(used, blk_e, 0)))
    ids = jnp.arange(N_EXPERTS, dtype=jnp.int32)
    of_blk = blk_e[:, None] == ids[None, :]
    n_valid = jnp.clip((jnp.sum(jnp.where(of_blk, seg_end[None, :], 0), axis=1) - blk_start + MOE_ROWS - 1)
                       // MOE_ROWS, 0, MOE_STEP_BLOCKS)
    n_valid = jnp.where(used, n_valid, 0).astype(jnp.int32)
    new_run = jnp.concatenate([jnp.ones((1,), jnp.int32), (blk_e[1:] != blk_e[:-1]).astype(jnp.int32)])
    run_ord = (jnp.cumsum(new_run) - 1).astype(jnp.int32)
    later = (ids[None, :] > ids[:, None]) & (padded > 0)[None, :]
    next_e = jnp.min(jnp.where(later, ids[None, :], N_EXPERTS), axis=1)
    next_e = jnp.where(next_e < N_EXPERTS, next_e, -1).astype(jnp.int32)
    run_next = jnp.sum(jnp.where(of_blk, next_e[None, :], 0), axis=1).astype(jnp.int32)

    nrows = nb_max * step_rows
    xs = _dispatch(tab, pad_start, tot, tails, lpos_p, x1_p, lpos_s, x1_s, tile=TOK_TILE, nrows=nrows)

    b1p = b_exp1[0].reshape(N_EXPERTS, 2 * D_FF // MXU_DIM, MXU_DIM // 2, 2)
    b1p = jnp.swapaxes(b1p, 2, 3).reshape(N_EXPERTS, 1, 2 * D_FF)
    ys = _experts(blk_e, n_used.reshape(1), n_valid, run_ord, run_next, xs, w_exp1[0], b1p, w_exp2[0],
                  b_exp2[0].reshape(N_EXPERTS, 1, D_MODEL),
                  jnp.asarray(_deinterleave_matrix(), BF16))

    g2, b2 = ln2_g[0].reshape(1, -1), ln2_b[0].reshape(1, -1)
    y_prompt = _combine(tab, pad_start, tot, cols_p, x1_p, g2, b2, ys, tile=TOK_TILE, tile_base=0)
    y_sample = _combine(tab, pad_start, tot, cols_s, x1_s, g2, b2, ys, tile=nsamp, tile_base=nt_p)

    k_p4 = k_p.reshape(bsz, seq, D_KV)[:, -WINDOW:].reshape(bsz, WINDOW, N_KV_HEADS, HEAD_DIM)
    v_p4 = v_p.reshape(bsz, seq, D_KV)[:, -WINDOW:].reshape(bsz, WINDOW, N_KV_HEADS, HEAD_DIM)
    k_s4 = k_next.reshape(nsamp, WINDOW, N_KV_HEADS, HEAD_DIM)
    v_s4 = v_next.reshape(nsamp, WINDOW, N_KV_HEADS, HEAD_DIM)
    st = lambda a, n: a.reshape(1, n, N_SSM_GROUPS, SSM_STATE)
    return (y_prompt.reshape(bsz, seq, D_MODEL), y_sample.reshape(nsamp, 1, D_MODEL),
            k_p4[None], v_p4[None], st(hp_re, bsz), st(hp_im, bsz),
            k_s4[None], v_s4[None], st(hs_re, nsamp), st(hs_im, nsamp))
```

```python
import functools

import numpy as np
import jax
import jax.numpy as jnp
from jax import lax
from jax.experimental import pallas as pl
from jax.experimental.pallas import tpu as pltpu

F32 = jnp.float32
BF16 = jnp.bfloat16

D_MODEL = 1024
HEAD_DIM = 64
N_Q_HEADS = 8
N_KV_HEADS = 2
Q_PER_KV = N_Q_HEADS // N_KV_HEADS
D_ATTN = N_Q_HEADS * HEAD_DIM
D_KV = N_KV_HEADS * HEAD_DIM
WINDOW = 128
ATTN_SCALE = HEAD_DIM ** -0.5
SSM_GROUP = 16
D_SSM = D_MODEL // 2
N_SSM_GROUPS = D_SSM // SSM_GROUP
SSM_STATE = 64
D_IN = D_ATTN + 2 * D_KV + D_SSM
N_EXPERTS = 32
TOP_K = 4
D_FF = D_MODEL
SWIGLU_LIMIT = 7.0
SWIGLU_ALPHA = 1.702
LN_EPS = 1e-5
DEPTH = 1
DEEPNORM_ALPHA = (2 * DEPTH) ** 0.25

LANES = 128
SUBLANES = 8
MXU_DIM = 256

S5_CHUNK = MXU_DIM // SSM_GROUP
S5_LANE_GROUPS = LANES // SSM_GROUP
MOE_ROWS = 256
MOE_STEP_BLOCKS = 2
TOK_TILE = 256
MERGE_TILE = 512
VMEM_LIMIT = 48 * 1024 * 1024
VMEM_LIMIT_LARGE = 56 * 1024 * 1024


def _cparams(sem, vmem=None):
    return pltpu.CompilerParams(dimension_semantics=sem, vmem_limit_bytes=vmem)


def _split_bf16(v):
    hi = v.astype(BF16)
    return hi, (v - hi.astype(F32)).astype(BF16)


def _dot_split(a, b, dims=(((1,), (0,)), ((), ()))):
    a_hi, a_lo = _split_bf16(a)
    b_hi, b_lo = _split_bf16(b)
    dot = lambda p, q: lax.dot_general(p, q, dims, preferred_element_type=F32)
    return dot(a_hi, b_hi) + dot(a_hi, b_lo) + dot(a_lo, b_hi)


def _proj_kernel(x_ref, w_ref, b_ref, q_ref, k_ref, v_ref, u_ref, *, exact_f32):
    if exact_f32:
        h = jnp.dot(x_ref[...], w_ref[...], preferred_element_type=F32, precision=lax.Precision.HIGHEST)
    else:
        h = jnp.dot(x_ref[...].astype(BF16), w_ref[...], preferred_element_type=F32)
    h = h + b_ref[...]
    q_ref[...] = (h[:, :D_ATTN] * ATTN_SCALE).astype(q_ref.dtype)
    k_ref[...] = h[:, D_ATTN:D_ATTN + D_KV]
    v_ref[...] = h[:, D_ATTN + D_KV:D_ATTN + 2 * D_KV]
    u_ref[...] = h[:, D_ATTN + 2 * D_KV:].astype(u_ref.dtype)


def _proj(x, w, b, *, tile, exact_f32, q_dtype):
    n = x.shape[0]
    return pl.pallas_call(
        functools.partial(_proj_kernel, exact_f32=exact_f32),
        grid=(n // tile,),
        in_specs=[pl.BlockSpec((tile, D_MODEL), lambda i: (i, 0)),
                  pl.BlockSpec((D_MODEL, D_IN), lambda i: (0, 0)),
                  pl.BlockSpec((1, D_IN), lambda i: (0, 0))],
        out_specs=[pl.BlockSpec((tile, D_ATTN), lambda i: (i, 0)),
                   pl.BlockSpec((tile, D_KV), lambda i: (i, 0)),
                   pl.BlockSpec((tile, D_KV), lambda i: (i, 0)),
                   pl.BlockSpec((tile, D_SSM), lambda i: (i, 0))],
        out_shape=[jax.ShapeDtypeStruct((n, D_ATTN), q_dtype),
                   jax.ShapeDtypeStruct((n, D_KV), F32),
                   jax.ShapeDtypeStruct((n, D_KV), F32),
                   jax.ShapeDtypeStruct((n, D_SSM), F32)],
        compiler_params=_cparams(("parallel",), VMEM_LIMIT),
        name="proj",
    )(x, w, b)


ATT_Q_TILE = 512


def _attn_prompt_kernel(sink_ref, q_ref, k_ref, v_ref, o_ref):
    i = pl.program_id(1)
    nk, nq = 2 * WINDOW, 2 * WINDOW
    lo = lax.broadcasted_iota(jnp.int32, (nk, LANES), 1) < HEAD_DIM
    top = lax.broadcasted_iota(jnp.int32, (nq, 1), 0) < WINDOW
    for blk in range(ATT_Q_TILE // WINDOW):
        q0 = i * ATT_Q_TILE + blk * WINDOW
        k0 = pl.multiple_of(jnp.maximum(q0 - WINDOW, 0), WINDOW)
        kk = k_ref[0, pl.ds(k0, nk), :]
        vv = v_ref[0, pl.ds(k0, nk), :]
        kk_sw = pltpu.roll(kk, HEAD_DIM, axis=1)
        vv_sw = pltpu.roll(vv, HEAD_DIM, axis=1)
        k_var = [[jnp.where(lo, kk, 0.0).astype(BF16), jnp.where(lo, 0.0, kk_sw).astype(BF16)],
                 [jnp.where(lo, kk_sw, 0.0).astype(BF16), jnp.where(lo, 0.0, kk).astype(BF16)]]
        v_var = [[jnp.where(lo, vv, 1.0).astype(BF16), jnp.where(lo, 1.0, vv_sw).astype(BF16)],
                 [jnp.where(lo, vv_sw, 1.0).astype(BF16), jnp.where(lo, 1.0, vv).astype(BF16)]]
        qpos = q0 + lax.broadcasted_iota(jnp.int32, (nq, nk), 0) % WINDOW
        kpos = k0 + lax.broadcasted_iota(jnp.int32, (nq, nk), 1)
        valid = (kpos <= qpos) & (qpos - kpos <= WINDOW)
        rows = slice(blk * WINDOW, (blk + 1) * WINDOW)
        for kv in range(N_KV_HEADS):
            pairs = (2 * kv, 2 * kv + 1)
            qs = jnp.concatenate([q_ref[0, rows, pr * LANES:(pr + 1) * LANES] for pr in pairs], axis=0)
            outs = []
            for parity in range(2):
                sink = jnp.where(top, sink_ref[2 * pairs[0] + parity], sink_ref[2 * pairs[1] + parity])
                s = lax.dot_general(qs, k_var[kv][parity], (((1,), (1,)), ((), ())), preferred_element_type=F32)
                s = jnp.where(valid, s, -jnp.inf)
                m = jnp.maximum(jnp.max(s, axis=-1, keepdims=True), sink)
                p = jnp.exp(s - m).astype(BF16)
                acc = jnp.dot(p, v_var[kv][parity], preferred_element_type=F32)
                outs.append(acc / (pltpu.roll(acc, HEAD_DIM, axis=1) + jnp.exp(sink - m)))
            o = jnp.where(lo, outs[0], outs[1]).astype(o_ref.dtype)
            for j, pr in enumerate(pairs):
                o_ref[0, rows, pr * LANES:(pr + 1) * LANES] = o[j * WINDOW:(j + 1) * WINDOW]


def _attn_prompt(sinks, q, k, v):
    bsz, seq = q.shape[0], q.shape[1]
    return pl.pallas_call(
        _attn_prompt_kernel,
        grid=(bsz, seq // ATT_Q_TILE),
        in_specs=[pl.BlockSpec(memory_space=pltpu.SMEM),
                  pl.BlockSpec((1, ATT_Q_TILE, D_ATTN), lambda b, i: (b, i, 0)),
                  pl.BlockSpec((1, seq, D_KV), lambda b, i: (b, 0, 0)),
                  pl.BlockSpec((1, seq, D_KV), lambda b, i: (b, 0, 0))],
        out_specs=pl.BlockSpec((1, ATT_Q_TILE, D_ATTN), lambda b, i: (b, i, 0)),
        out_shape=jax.ShapeDtypeStruct((bsz, seq, D_ATTN), BF16),
        compiler_params=_cparams(("parallel", "parallel")),
        name="attn_prompt",
    )(sinks, q, k, v)


ATT_S_GROUP = 16


def _attn_sample_kernel(sink_ref, q_ref, kn_ref, vn_ref, kb_ref, vb_ref, o_ref, knext_ref, vnext_ref):
    g = ATT_S_GROUP
    rows = Q_PER_KV * g
    ncol = g * WINDOW
    for buf_ref, new_ref, next_ref in ((kb_ref, kn_ref, knext_ref), (vb_ref, vn_ref, vnext_ref)):
        next_ref[:, 0:WINDOW - 1, :] = buf_ref[:, 1:WINDOW, :]
        next_ref[:, WINDOW - 1, :] = new_ref[...]
    kb = kb_ref[...].reshape(ncol, D_KV)
    vb = vb_ref[...].reshape(ncol, D_KV)
    rseq = lax.broadcasted_iota(jnp.int32, (rows, ncol), 0) % g
    cseq = lax.broadcasted_iota(jnp.int32, (rows, ncol), 1) // WINDOW
    own = rseq == cseq
    rhead = lax.broadcasted_iota(jnp.int32, (rows, 1), 0) // g
    for kv in range(N_KV_HEADS):
        lo = kv * HEAD_DIM
        qs = jnp.concatenate(
            [q_ref[:, (kv * Q_PER_KV + h) * HEAD_DIM:(kv * Q_PER_KV + h + 1) * HEAD_DIM] for h in range(Q_PER_KV)],
            axis=0)
        kn = jnp.concatenate([kn_ref[:, lo:lo + HEAD_DIM]] * Q_PER_KV, axis=0)
        vn = jnp.concatenate([vn_ref[:, lo:lo + HEAD_DIM]] * Q_PER_KV, axis=0)
        sink = jnp.zeros((rows, 1), F32)
        for h in range(Q_PER_KV):
            sink = jnp.where(rhead == h, sink_ref[kv * Q_PER_KV + h], sink)
        qs = qs.astype(F32)
        s = _dot_split(qs, kb[:, lo:lo + HEAD_DIM], (((1,), (1,)), ((), ())))
        s = jnp.where(own, s, -jnp.inf)
        s_new = jnp.sum(qs * kn, axis=-1, keepdims=True)
        m = jnp.maximum(jnp.maximum(jnp.max(s, axis=-1, keepdims=True), s_new), sink)
        p = jnp.exp(s - m)
        p_new = jnp.exp(s_new - m)
        denom = jnp.sum(p, axis=-1, keepdims=True) + p_new + jnp.exp(sink - m)
        o = (_dot_split(p, vb[:, lo:lo + HEAD_DIM]) + p_new * vn) / denom
        for h in range(Q_PER_KV):
            c0 = (kv * Q_PER_KV + h) * HEAD_DIM
            o_ref[:, c0:c0 + HEAD_DIM] = o[h * g:(h + 1) * g].astype(o_ref.dtype)


def _attn_sample(sinks, q, k_new, v_new, k_buf, v_buf):
    n = q.shape[0]
    g = ATT_S_GROUP
    return pl.pallas_call(
        _attn_sample_kernel,
        grid=(n // g,),
        in_specs=[pl.BlockSpec(memory_space=pltpu.SMEM),
                  pl.BlockSpec((g, D_ATTN), lambda i: (i, 0)),
                  pl.BlockSpec((g, D_KV), lambda i: (i, 0)),
                  pl.BlockSpec((g, D_KV), lambda i: (i, 0)),
                  pl.BlockSpec((g, WINDOW, D_KV), lambda i: (i, 0, 0)),
                  pl.BlockSpec((g, WINDOW, D_KV), lambda i: (i, 0, 0))],
        out_specs=[pl.BlockSpec((g, D_ATTN), lambda i: (i, 0)),
                   pl.BlockSpec((g, WINDOW, D_KV), lambda i: (i, 0, 0)),
                   pl.BlockSpec((g, WINDOW, D_KV), lambda i: (i, 0, 0))],
        out_shape=[jax.ShapeDtypeStruct((n, D_ATTN), F32),
                   jax.ShapeDtypeStruct((n, WINDOW, D_KV), F32),
                   jax.ShapeDtypeStruct((n, WINDOW, D_KV), F32)],
        compiler_params=_cparams(("parallel",)),
        name="attn_sample",
    )(sinks, q, k_new, v_new, k_buf, v_buf)


def _s5_params(a_re, a_im, log_dt, b_re, b_im, c_re, c_im):
    hp = lax.Precision.HIGHEST
    dt = jnp.exp(log_dt.astype(F32))[:, None]
    are, aim = a_re.astype(F32), a_im.astype(F32)
    tau = jnp.arange(S5_CHUNK + 1, dtype=F32)[None, :, None]
    mag = jnp.exp(tau * (dt * are)[:, None, :])
    ang = tau * (dt * aim)[:, None, :]
    pw_re, pw_im = mag * jnp.cos(ang), mag * jnp.sin(ang)
    ab_re, ab_im = pw_re[:, 1], pw_im[:, 1]
    den = are * are + aim * aim
    f_re = ((ab_re - 1.0) * are + ab_im * aim) / den
    f_im = (ab_im * are - (ab_re - 1.0) * aim) / den
    bre, bim = b_re.astype(F32), b_im.astype(F32)
    bb_re = f_re[..., None] * bre - f_im[..., None] * bim
    bb_im = f_re[..., None] * bim + f_im[..., None] * bre
    cre, cim = c_re.astype(F32), c_im.astype(F32)
    return dict(pw_re=pw_re, pw_im=pw_im, ab_re=ab_re, ab_im=ab_im, bb_re=bb_re, bb_im=bb_im,
                c_re=cre, c_im=cim, hp=hp)


def _s5_chunk_mats(sp, d_skip):
    hp = sp["hp"]
    g, t, c, p = N_SSM_GROUPS, S5_CHUNK, SSM_GROUP, SSM_STATE
    pw_re, pw_im = sp["pw_re"], sp["pw_im"]
    ca_re = sp["c_re"][:, None] * pw_re[:, :, None, :] - sp["c_im"][:, None] * pw_im[:, :, None, :]
    ca_im = sp["c_re"][:, None] * pw_im[:, :, None, :] + sp["c_im"][:, None] * pw_re[:, :, None, :]
    kern = (jnp.einsum("gtcp,gpd->gtcd", ca_re[:, :t], sp["bb_re"], precision=hp)
            - jnp.einsum("gtcp,gpd->gtcd", ca_im[:, :t], sp["bb_im"], precision=hp))
    kc = jnp.swapaxes(kern, 2, 3)
    kc = kc.at[:, 0].add(d_skip.astype(F32).reshape(g, 1, c) * jnp.eye(c, dtype=F32)[None])
    rev_re, rev_im = pw_re[:, t - 1::-1][:, :t], pw_im[:, t - 1::-1][:, :t]
    wst_re = rev_re[:, :, None, :] * jnp.swapaxes(sp["bb_re"], 1, 2)[:, None] \
        - rev_im[:, :, None, :] * jnp.swapaxes(sp["bb_im"], 1, 2)[:, None]
    wst_im = rev_re[:, :, None, :] * jnp.swapaxes(sp["bb_im"], 1, 2)[:, None] \
        + rev_im[:, :, None, :] * jnp.swapaxes(sp["bb_re"], 1, 2)[:, None]
    wo_re = jnp.transpose(ca_re[:, 1:t + 1], (0, 3, 1, 2))
    wo_im = -jnp.transpose(ca_im[:, 1:t + 1], (0, 3, 1, 2))
    nv, gl = g // S5_LANE_GROUPS, S5_LANE_GROUPS
    kc, wst_re, wst_im, wo_re, wo_im = lax.optimization_barrier((kc, wst_re, wst_im, wo_re, wo_im))
    kc_c = kc.astype(BF16).reshape(nv, gl * t * c, c)
    ws_c = jnp.concatenate([wst_re, wst_im], axis=-1).astype(BF16).reshape(nv, gl * t * c, 2 * p)
    wo_re_c = wo_re.astype(BF16).reshape(nv, gl * p, t * c)
    wo_im_c = wo_im.astype(BF16).reshape(nv, gl * p, t * c)
    spread_b = np.zeros((c, LANES), np.float32)
    spread_s = np.zeros((2 * p, 2 * gl * p), np.float32)
    spread_o = np.zeros((t * c, t * LANES), np.float32)
    for h in range(gl):
        spread_b[np.arange(c), h * c + np.arange(c)] = 1.0
        for ri in range(2):
            spread_s[ri * p + np.arange(p), ri * gl * p + h * p + np.arange(p)] = 1.0
        for tt in range(t):
            spread_o[tt * c + np.arange(c), tt * LANES + h * c + np.arange(c)] = 1.0
    at_re = pw_re[:, t].reshape(1, g * p)
    at_im = pw_im[:, t].reshape(1, g * p)
    return dict(kc=kc_c, ws=ws_c, wo_re=wo_re_c, wo_im=wo_im_c, spread_b=jnp.asarray(spread_b, BF16),
                spread_s=jnp.asarray(spread_s, BF16), spread_o=jnp.asarray(spread_o, BF16),
                at_re=at_re, at_im=at_im)


def _s5_chunk_rows(u_ref, nchunk):
    return jnp.concatenate(
        [u_ref[pl.ds(s, nchunk, stride=S5_CHUNK), :] for s in range(S5_CHUNK)], axis=1).astype(BF16)


S5_SLABS = S5_LANE_GROUPS * SSM_STATE // LANES


S5_SEQS_PER_STEP = 2
S5_EXPAND_ROWS = 256
S5_C_SHIFT = SSM_GROUP.bit_length() - 1
S5_P_SHIFT = SSM_STATE.bit_length() - 1


def _s5_expand(dst_ref, compact_rows, spread_ref, row_shift, col_shift):
    n_rows, n_cols = dst_ref.shape
    col_g = lax.shift_right_logical(lax.broadcasted_iota(jnp.int32, (S5_EXPAND_ROWS, n_cols), 1), col_shift)
    for r0 in range(0, n_rows, S5_EXPAND_ROWS):
        row_g = lax.shift_right_logical(r0 + lax.broadcasted_iota(jnp.int32, (S5_EXPAND_ROWS, n_cols), 0), row_shift)
        same = ((row_g ^ col_g) & (S5_LANE_GROUPS - 1)) == 0
        blk = jnp.dot(compact_rows(r0), spread_ref[...], preferred_element_type=F32)
        dst_ref[r0:r0 + S5_EXPAND_ROWS, :] = jnp.where(same, blk, 0.0).astype(dst_ref.dtype)


def _s5_group_rows(ref, index):
    per_group = S5_CHUNK * SSM_GROUP
    return jnp.concatenate([ref[0, g * per_group + index * SSM_GROUP:g * per_group + (index + 1) * SSM_GROUP, :]
                            for g in range(S5_LANE_GROUPS)], axis=0)


def _s5_state_kernel(u_ref, ws_ref, spread_ref, sre_ref, sim_ref, wst_sc):
    nchunk = sre_ref.shape[1]

    @pl.when(pl.program_id(1) == 0)
    def _():
        per_call = S5_EXPAND_ROWS // LANES
        _s5_expand(wst_sc,
                   lambda r0: jnp.concatenate([_s5_group_rows(ws_ref, r0 // LANES + j) for j in range(per_call)], axis=0),
                   spread_ref, S5_C_SHIFT, S5_P_SHIFT)

    s = jnp.dot(_s5_chunk_rows(u_ref, nchunk), wst_sc[...], preferred_element_type=F32)
    for k in range(S5_SLABS):
        sre_ref[k] = s[:, k * LANES:(k + 1) * LANES]
        sim_ref[k] = s[:, (S5_SLABS + k) * LANES:(S5_SLABS + k + 1) * LANES]


def _s5_scan_kernel(sre_ref, sim_ref, are_ref, aim_ref, hre_ref, him_ref, fre_ref, fim_ref, *, bsz):
    nchunk = sre_ref.shape[1] // bsz
    are = [jnp.broadcast_to(are_ref[:, k * LANES:(k + 1) * LANES], (bsz, LANES)) for k in range(S5_SLABS)]
    aim = [jnp.broadcast_to(aim_ref[:, k * LANES:(k + 1) * LANES], (bsz, LANES)) for k in range(S5_SLABS)]

    def body(j, carry):
        rows = pl.ds(j, bsz, stride=nchunk)
        out = []
        for k in range(S5_SLABS):
            cre, cim = carry[2 * k], carry[2 * k + 1]
            hre_ref[k, rows, :] = cre
            him_ref[k, rows, :] = cim
            sr = sre_ref[k, rows, :]
            si = sim_ref[k, rows, :]
            out += [are[k] * cre - aim[k] * cim + sr, are[k] * cim + aim[k] * cre + si]
        return tuple(out)

    zero = jnp.zeros((bsz, LANES), F32)
    fin = lax.fori_loop(0, nchunk, body, (zero,) * (2 * S5_SLABS), unroll=4)
    fre_ref[...] = jnp.concatenate(fin[0::2], axis=1)
    fim_ref[...] = jnp.concatenate(fin[1::2], axis=1)


def _s5_out_kernel(u_ref, kc_ref, spread_b_ref, hre_ref, him_ref, wo_re_ref, wo_im_ref, spread_o_ref, y_ref,
                   m_sc, wout_sc):
    nchunk = hre_ref.shape[1]

    @pl.when(pl.program_id(1) == 0)
    def _():
        rg = lax.shift_right_logical(lax.broadcasted_iota(jnp.int32, (LANES, LANES), 0), S5_C_SHIFT)
        cg = lax.shift_right_logical(lax.broadcasted_iota(jnp.int32, (LANES, LANES), 1), S5_C_SHIFT)
        zero_blk = jnp.zeros((LANES, LANES), BF16)
        lag_blk = [jnp.where(rg == cg, jnp.dot(_s5_group_rows(kc_ref, tau), spread_b_ref[...],
                                               preferred_element_type=F32), 0.0).astype(BF16)
                   for tau in range(S5_CHUNK)]
        for s in range(S5_CHUNK):
            for t in range(S5_CHUNK):
                m_sc[s * LANES:(s + 1) * LANES, t * LANES:(t + 1) * LANES] = lag_blk[t - s] if t >= s else zero_blk
        half = wo_re_ref.shape[1]
        _s5_expand(wout_sc,
                   lambda r0: (wo_re_ref[0, r0:r0 + S5_EXPAND_ROWS, :] if r0 < half
                               else wo_im_ref[0, r0 - half:r0 - half + S5_EXPAND_ROWS, :]),
                   spread_o_ref, S5_P_SHIFT, S5_C_SHIFT)

    hcat = jnp.concatenate([hre_ref[k] for k in range(S5_SLABS)] + [him_ref[k] for k in range(S5_SLABS)],
                           axis=1).astype(BF16)
    lhs = _s5_chunk_rows(u_ref, nchunk)
    y = jnp.concatenate(
        [jnp.dot(lhs[:, :j + MXU_DIM], m_sc[:j + MXU_DIM, j:j + MXU_DIM], preferred_element_type=F32)
         for j in range(0, S5_CHUNK * LANES, MXU_DIM)], axis=1)
    y = y + jnp.dot(hcat, wout_sc[...], preferred_element_type=F32)
    for s in range(S5_CHUNK):
        y_ref[pl.ds(s, nchunk, stride=S5_CHUNK), :] = y[:, s * LANES:(s + 1) * LANES]


def _s5_prompt(u, bsz, seq, mats):
    at_re, at_im = mats["at_re"], mats["at_im"]
    g, t, p, c = N_SSM_GROUPS, S5_CHUNK, SSM_STATE, SSM_GROUP
    nchunk = seq // t
    n = nchunk * bsz
    nv = g // S5_LANE_GROUPS
    half = S5_LANE_GROUPS * p
    per_step = S5_SEQS_PER_STEP if bsz % S5_SEQS_PER_STEP == 0 else 1
    s_re, s_im = pl.pallas_call(
        _s5_state_kernel,
        grid=(nv, bsz // per_step),
        in_specs=[pl.BlockSpec((per_step * seq, LANES), lambda v, b: (b, v)),
                  pl.BlockSpec((1, t * LANES, 2 * p), lambda v, b: (v, 0, 0)),
                  pl.BlockSpec((2 * p, 2 * half), lambda v, b: (0, 0))],
        out_specs=[pl.BlockSpec((S5_SLABS, per_step * nchunk, LANES), lambda v, b: (v, b, 0)),
                   pl.BlockSpec((S5_SLABS, per_step * nchunk, LANES), lambda v, b: (v, b, 0))],
        out_shape=[jax.ShapeDtypeStruct((nv * S5_SLABS, n, LANES), F32)] * 2,
        scratch_shapes=[pltpu.VMEM((t * LANES, 2 * half), BF16)],
        compiler_params=_cparams(("parallel", "arbitrary"), VMEM_LIMIT),
        name="s5_state",
    )(u, mats["ws"], mats["spread_s"])
    h_re, h_im, f_re, f_im = pl.pallas_call(
        functools.partial(_s5_scan_kernel, bsz=bsz),
        grid=(nv,),
        in_specs=[pl.BlockSpec((S5_SLABS, n, LANES), lambda i: (i, 0, 0)),
                  pl.BlockSpec((S5_SLABS, n, LANES), lambda i: (i, 0, 0)),
                  pl.BlockSpec((1, half), lambda i: (0, i)),
                  pl.BlockSpec((1, half), lambda i: (0, i))],
        out_specs=[pl.BlockSpec((S5_SLABS, n, LANES), lambda i: (i, 0, 0)),
                   pl.BlockSpec((S5_SLABS, n, LANES), lambda i: (i, 0, 0)),
                   pl.BlockSpec((bsz, half), lambda i: (0, i)),
                   pl.BlockSpec((bsz, half), lambda i: (0, i))],
        out_shape=[jax.ShapeDtypeStruct((nv * S5_SLABS, n, LANES), F32)] * 2
        + [jax.ShapeDtypeStruct((bsz, g * p), F32)] * 2,
        compiler_params=_cparams(("parallel",)),
        name="s5_scan",
    )(s_re, s_im, at_re, at_im)
    y = pl.pallas_call(
        _s5_out_kernel,
        grid=(nv, bsz // per_step),
        in_specs=[pl.BlockSpec((per_step * seq, LANES), lambda v, b: (b, v)),
                  pl.BlockSpec((1, t * LANES, c), lambda v, b: (v, 0, 0)),
                  pl.BlockSpec((c, LANES), lambda v, b: (0, 0)),
                  pl.BlockSpec((S5_SLABS, per_step * nchunk, LANES), lambda v, b: (v, b, 0)),
                  pl.BlockSpec((S5_SLABS, per_step * nchunk, LANES), lambda v, b: (v, b, 0)),
                  pl.BlockSpec((1, half, t * c), lambda v, b: (v, 0, 0)),
                  pl.BlockSpec((1, half, t * c), lambda v, b: (v, 0, 0)),
                  pl.BlockSpec((t * c, t * LANES), lambda v, b: (0, 0))],
        out_specs=pl.BlockSpec((per_step * seq, LANES), lambda v, b: (b, v)),
        out_shape=jax.ShapeDtypeStruct((bsz * seq, D_SSM), F32),
        scratch_shapes=[pltpu.VMEM((t * LANES, t * LANES), BF16), pltpu.VMEM((2 * half, t * LANES), BF16)],
        compiler_params=_cparams(("parallel", "arbitrary"), VMEM_LIMIT),
        name="s5_out",
    )(u, mats["kc"], mats["spread_b"], h_re, h_im, mats["wo_re"], mats["wo_im"], mats["spread_o"])
    return y, f_re, f_im


S5S_GROUPS = LANES // SSM_GROUP


def _s5_sample_mats(sp, d_skip):
    go, gl, c, p = N_SSM_GROUPS // S5S_GROUPS, S5S_GROUPS, SSM_GROUP, SSM_STATE
    eye = jnp.eye(gl, dtype=F32)

    def bdiag_in(b):
        b4 = b.reshape(go, gl, p, c)
        return jnp.einsum("ogpc,gh->ogchp", b4, eye).reshape(go, gl * c, gl * p)

    def bdiag_out(cm):
        c4 = cm.reshape(go, gl, c, p)
        return jnp.einsum("ogcp,gh->ogphc", c4, eye).reshape(go, gl * p, gl * c)

    b8 = jnp.concatenate([bdiag_in(sp["bb_re"]), bdiag_in(sp["bb_im"])], axis=2)
    c8 = jnp.concatenate([bdiag_out(sp["c_re"]), -bdiag_out(sp["c_im"])], axis=1)
    a_re = sp["ab_re"].reshape(1, N_SSM_GROUPS * p)
    a_im = sp["ab_im"].reshape(1, N_SSM_GROUPS * p)
    return b8, c8, a_re, a_im, d_skip.astype(F32).reshape(1, D_SSM)


def _s5_sample_kernel(u_ref, hre_ref, him_ref, b8_ref, c8_ref, are_ref, aim_ref, d_ref,
                      y_ref, ore_ref, oim_ref):
    hp = lax.Precision.HIGHEST
    u = u_ref[...]
    half = S5S_GROUPS * SSM_STATE
    bu = jnp.dot(u, b8_ref[0], preferred_element_type=F32, precision=hp)
    are, aim = are_ref[...], aim_ref[...]
    h0r, h0i = hre_ref[...], him_ref[...]
    hr = are * h0r - aim * h0i + bu[:, :half]
    hi = are * h0i + aim * h0r + bu[:, half:]
    ore_ref[...] = hr
    oim_ref[...] = hi
    y = jnp.dot(jnp.concatenate([hr, hi], axis=1), c8_ref[0], preferred_element_type=F32, precision=hp)
    y_ref[...] = (y + d_ref[...] * u).astype(y_ref.dtype)


def _s5_sample(u, h0_re, h0_im, mats):
    b8, c8, a_re, a_im, d = mats
    n = u.shape[0]
    half = S5S_GROUPS * SSM_STATE
    return pl.pallas_call(
        _s5_sample_kernel,
        grid=(N_SSM_GROUPS // S5S_GROUPS,),
        in_specs=[pl.BlockSpec((n, LANES), lambda i: (0, i)),
                  pl.BlockSpec((n, half), lambda i: (0, i)),
                  pl.BlockSpec((n, half), lambda i: (0, i)),
                  pl.BlockSpec((1, LANES, 2 * half), lambda i: (i, 0, 0)),
                  pl.BlockSpec((1, 2 * half, LANES), lambda i: (i, 0, 0)),
                  pl.BlockSpec((1, half), lambda i: (0, i)),
                  pl.BlockSpec((1, half), lambda i: (0, i)),
                  pl.BlockSpec((1, LANES), lambda i: (0, i))],
        out_specs=[pl.BlockSpec((n, LANES), lambda i: (0, i)),
                   pl.BlockSpec((n, half), lambda i: (0, i)),
                   pl.BlockSpec((n, half), lambda i: (0, i))],
        out_shape=[jax.ShapeDtypeStruct((n, D_SSM), F32),
                   jax.ShapeDtypeStruct((n, N_SSM_GROUPS * SSM_STATE), F32),
                   jax.ShapeDtypeStruct((n, N_SSM_GROUPS * SSM_STATE), F32)],
        compiler_params=_cparams(("parallel",)),
        name="s5_sample",
    )(u, h0_re, h0_im, b8, c8, a_re, a_im, d)


def _layer_norm(x, g, b):
    mu = jnp.mean(x, axis=-1, keepdims=True)
    xc = x - mu
    var = jnp.mean(xc * xc, axis=-1, keepdims=True)
    return xc * lax.rsqrt(var + LN_EPS) * g + b


def _sigmoid(x):
    return 0.5 * jnp.tanh(0.5 * x) + 0.5


RUN_ROWS = SUBLANES
TAB_ROWS = 3


def _merge_kernel(x_ref, oa_ref, ys_ref, carry_in_ref, wao_ref, wso_ref, wg_ref, bg_ref, wo_ref,
                  g1_ref, b1_ref, wrt_ref, brt_ref,
                  x1_ref, lpos_ref, cols_ref, tab_ref, carry_out_ref, carry_sc, *, f32_matmuls):
    step = pl.program_id(0)

    @pl.when(step == 0)
    def _():
        carry_sc[...] = carry_in_ref[...]

    def mm(a, w_ref):
        if f32_matmuls:
            return _dot_split(a.astype(F32), w_ref[...])
        return jnp.dot(a.astype(BF16), w_ref[...], preferred_element_type=F32)

    tm = x_ref.shape[0]
    x = x_ref[...]
    branch_a = mm(oa_ref[...], wao_ref)
    z = mm(jax.nn.gelu(ys_ref[...].astype(F32)), wso_ref)
    branch_b = z[:, :D_MODEL] * _sigmoid(z[:, D_MODEL:])
    gates = _sigmoid(mm(x, wg_ref) + bg_ref[...])
    mixed = gates[:, :D_MODEL] * branch_a + gates[:, D_MODEL:] * branch_b
    mix = mm(mixed, wo_ref)
    x1 = _layer_norm(DEEPNORM_ALPHA * x + mix, g1_ref[...], b1_ref[...])
    x1_ref[...] = x1

    split2 = _split_bf16

    def dot_nt(a, b):
        return lax.dot_general(a, b, (((1,), (1,)), ((), ())), preferred_element_type=F32)

    w_hi, w_lo = split2(wrt_ref[...])
    rt = tm // tab_ref.shape[0]
    sub = lax.broadcasted_iota(jnp.int32, (N_EXPERTS, rt), 0)
    r = lax.broadcasted_iota(jnp.int32, (rt, rt), 0)
    c = lax.broadcasted_iota(jnp.int32, (rt, rt), 1)
    er = lax.broadcasted_iota(jnp.int32, (N_EXPERTS, N_EXPERTS), 0)
    ec = lax.broadcasted_iota(jnp.int32, (N_EXPERTS, N_EXPERTS), 1)
    rid = lax.broadcasted_iota(jnp.int32, (SUBLANES, LANES), 0)
    lane_pad = jnp.zeros((SUBLANES, LANES - N_EXPERTS), F32)
    for h in range(tab_ref.shape[0]):
        x_hi, x_lo = split2(x1[h * rt:(h + 1) * rt])
        logits = dot_nt(w_hi, x_hi) + dot_nt(w_hi, x_lo) + dot_nt(w_lo, x_hi) + brt_ref[...]
        work = logits
        vals, sels = [], []
        for _ in range(TOP_K):
            mx = jnp.max(work, axis=0, keepdims=True)
            idx = jnp.min(jnp.where(work == mx, sub, N_EXPERTS), axis=0, keepdims=True)
            sel = sub == idx
            vals.append(mx)
            sels.append(sel)
            work = jnp.where(sel, -jnp.inf, work)
        ex = [jnp.exp(v - vals[0]) for v in vals]
        tot = ex[0] + ex[1] + ex[2] + ex[3]
        gate_rows = jnp.concatenate([e / tot for e in ex], axis=0)

        multi = jnp.zeros((N_EXPERTS, rt), F32)
        for sel in sels:
            multi = multi + jnp.where(sel, 1.0, 0.0)
        multi_b = multi.astype(BF16)
        earlier = jnp.dot(multi_b, jnp.where(r < c, 1.0, 0.0).astype(BF16), preferred_element_type=F32)
        cnt_col = jnp.sum(multi, axis=1, keepdims=True)
        nb_col = jnp.floor((cnt_col + (RUN_ROWS - 1.0)) * (1.0 / RUN_ROWS))
        loff_col = jnp.dot(jnp.where(ec < er, 1.0, 0.0).astype(BF16),
                           jnp.broadcast_to(nb_col, (N_EXPERTS, rt)).astype(BF16), preferred_element_type=F32)
        base = RUN_ROWS * loff_col + earlier
        lpos = jnp.concatenate([jnp.sum(jnp.where(sel, base, 0.0), axis=0, keepdims=True) for sel in sels],
                               axis=0)
        lpos_ref[:, h * rt:(h + 1) * rt] = lpos.astype(jnp.int32)
        rows_hi, rows_lo = split2(jnp.concatenate([lpos, gate_rows], axis=0))
        eye = jnp.where(r == c, 1.0, 0.0).astype(BF16)
        cols_ref[h * rt:(h + 1) * rt, :] = dot_nt(eye, rows_hi) + dot_nt(eye, rows_lo)

        cnt_row = dot_nt(jnp.ones((SUBLANES, rt), BF16), multi_b)
        nb_row = jnp.floor((cnt_row + (RUN_ROWS - 1.0)) * (1.0 / RUN_ROWS))
        loff_row = jnp.dot(nb_row.astype(BF16), jnp.where(er < ec, 1.0, 0.0).astype(BF16),
                           preferred_element_type=F32)
        nb_p = jnp.concatenate([nb_row, lane_pad], axis=1)
        loff_p = jnp.concatenate([loff_row, lane_pad], axis=1)
        goff_p = carry_sc[...]
        tab = jnp.where(rid == 0, nb_p, jnp.where(rid == 1, loff_p, jnp.where(rid == 2, goff_p, 0.0)))
        tab_ref[h] = tab.astype(jnp.int32)
        carry_sc[...] = goff_p + nb_p
    carry_out_ref[...] = carry_sc[...]


def _merge(x, o_attn, y_ssm, carry_in, w, *, tile, route_tile, f32_matmuls):
    n = x.shape[0]
    nt = n // tile
    per_step = tile // route_tile
    full = lambda shape: pl.BlockSpec(shape, lambda i: (0,) * len(shape))
    return pl.pallas_call(
        functools.partial(_merge_kernel, f32_matmuls=f32_matmuls),
        grid=(nt,),
        in_specs=[pl.BlockSpec((tile, D_MODEL), lambda i: (i, 0)),
                  pl.BlockSpec((tile, D_ATTN), lambda i: (i, 0)),
                  pl.BlockSpec((tile, D_SSM), lambda i: (i, 0)),
                  full((SUBLANES, LANES)),
                  full((D_ATTN, D_MODEL)), full((D_SSM, 2 * D_MODEL)), full((D_MODEL, 2 * D_MODEL)),
                  full((1, 2 * D_MODEL)), full((D_MODEL, D_MODEL)),
                  full((1, D_MODEL)), full((1, D_MODEL)),
                  full((N_EXPERTS, D_MODEL)), full((N_EXPERTS, 1))],
        out_specs=[pl.BlockSpec((tile, D_MODEL), lambda i: (i, 0)),
                   pl.BlockSpec((TOP_K, tile), lambda i: (0, i)),
                   pl.BlockSpec((tile, 2 * TOP_K), lambda i: (i, 0)),
                   pl.BlockSpec((per_step, SUBLANES, LANES), lambda i: (i, 0, 0)),
                   full((SUBLANES, LANES))],
        out_shape=[jax.ShapeDtypeStruct((n, D_MODEL), F32),
                   jax.ShapeDtypeStruct((TOP_K, n), jnp.int32),
                   jax.ShapeDtypeStruct((n, 2 * TOP_K), F32),
                   jax.ShapeDtypeStruct((nt * per_step, SUBLANES, LANES), jnp.int32),
                   jax.ShapeDtypeStruct((SUBLANES, LANES), F32)],
        scratch_shapes=[pltpu.VMEM((SUBLANES, LANES), F32)],
        compiler_params=_cparams(("arbitrary",), VMEM_LIMIT_LARGE),
        name="merge",
    )(x, o_attn, y_ssm, carry_in, w["wao"], w["wso"], w["wg"], w["bg"], w["wo"], w["g1"], w["b1"],
      w["wrt"], w["brt"])


def _tab(tab_ref, tile, row, e):
    return tab_ref[(tile * TAB_ROWS + row) * N_EXPERTS + e]


BIG_PIECE = 4 * RUN_ROWS
MAX_UNITS_LOG2 = 8


def _for_each_run_piece(tab_ref, tile, fn):
    def per_expert(e, carry):
        loff = RUN_ROWS * _tab(tab_ref, tile, 1, e)
        goff = RUN_ROWS * _tab(tab_ref, tile, 2, e)
        units = _tab(tab_ref, tile, 0, e)
        n_big = lax.shift_right_logical(units, 2)

        def big(j, c2):
            fn(pl.multiple_of(loff + j * BIG_PIECE, RUN_ROWS), goff + j * BIG_PIECE, e, BIG_PIECE)
            return c2

        lax.fori_loop(0, n_big, big, 0)
        done = n_big * BIG_PIECE

        def small(j, c2):
            fn(pl.multiple_of(loff + done + j * RUN_ROWS, RUN_ROWS), goff + done + j * RUN_ROWS, e, RUN_ROWS)
            return c2

        lax.fori_loop(0, units & 3, small, 0)
        return carry

    lax.fori_loop(0, N_EXPERTS, per_expert, 0)


def _drain_units(units, wait_copy, buffer_rows):
    assert buffer_rows < (RUN_ROWS << MAX_UNITS_LOG2)
    for b in range(MAX_UNITS_LOG2):
        if (RUN_ROWS << b) > buffer_rows:
            break

        @pl.when((lax.shift_right_logical(units, b) & 1) == 1)
        def _():
            wait_copy(RUN_ROWS << b).wait()


def _dispatch_kernel(tab_ref, seg_ref, tot_ref, tail_ref, lpos_p_ref, xp_ref, lpos_s_ref, xs_in_ref, xs_ref,
                     loc_sc, zero_sc, sem, zsem):
    i = pl.program_id(0)
    last = pl.num_programs(0) - 1
    tile = i
    slot = i % 2
    loc = loc_sc.shape[1]

    def tail_copy(e, j):
        row = pl.multiple_of(RUN_ROWS * (tail_ref[e] + j), RUN_ROWS)
        return pltpu.make_async_copy(zero_sc.at[pl.ds(0, RUN_ROWS)], xs_ref.at[pl.ds(row, RUN_ROWS)], zsem)

    def block_copy(b):
        row = pl.multiple_of(b * MOE_ROWS, MOE_ROWS)
        return pltpu.make_async_copy(zero_sc, xs_ref.at[pl.ds(row, MOE_ROWS)], zsem)

    def for_each_zero_copy(fn):
        def per_expert(e, carry):
            lax.fori_loop(0, tail_ref[N_EXPERTS + e], lambda j, c2: (fn(tail_copy(e, j)), c2)[1], 0)
            return carry

        lax.fori_loop(0, N_EXPERTS, per_expert, 0)
        lax.fori_loop(tail_ref[2 * N_EXPERTS], xs_ref.shape[0] // MOE_ROWS,
                      lambda b, c2: (fn(block_copy(b)), c2)[1], 0)

    @pl.when(i == 0)
    def _():
        zero_sc[...] = jnp.zeros_like(zero_sc)
        for_each_zero_copy(lambda cp: cp.start())

    def sort_tile(lpos_ref, x_ref):
        tm = x_ref.shape[0]
        rows = lax.broadcasted_iota(jnp.int32, (loc, tm), 0)
        lp = lpos_ref[...]
        onehot = jnp.zeros((loc, tm), F32)
        for k in range(TOP_K):
            onehot = jnp.where(rows == lp[k:k + 1], 1.0, onehot)
        loc_sc[slot] = jnp.dot(onehot.astype(BF16), x_ref[...].astype(BF16), preferred_element_type=F32)

    @pl.when(i < last)
    def _():
        sort_tile(lpos_p_ref, xp_ref)

    @pl.when(i == last)
    def _():
        sort_tile(lpos_s_ref, xs_in_ref)

    def piece_copy(sl, lrow, grow, e, n):
        dst = pl.multiple_of(seg_ref[e] + grow, RUN_ROWS)
        return pltpu.make_async_copy(loc_sc.at[sl, pl.ds(lrow, n)], xs_ref.at[pl.ds(dst, n)], sem.at[sl])

    _for_each_run_piece(tab_ref, tile, lambda lrow, grow, e, n: piece_copy(slot, lrow, grow, e, n).start())

    def drain(tl, sl):
        _drain_units(tot_ref[tl], lambda n: piece_copy(sl, 0, 0, 0, n), loc)

    @pl.when(i > 0)
    def _():
        drain(tile - 1, 1 - slot)

    @pl.when(i == last)
    def _():
        drain(tile, slot)
        for_each_zero_copy(lambda cp: cp.wait())


def _dispatch(tab, seg_start, tot, tails, lpos_p, x1_p, lpos_s, x1_s, *, tile, nrows):
    nt_p = x1_p.shape[0] // tile
    ns = x1_s.shape[0]
    loc = tile * TOP_K + N_EXPERTS * RUN_ROWS
    prompt_blk = lambda i, *_: jnp.minimum(i, nt_p - 1)
    return pl.pallas_call(
        _dispatch_kernel,
        grid_spec=pltpu.PrefetchScalarGridSpec(
            num_scalar_prefetch=4,
            grid=(nt_p + 1,),
            in_specs=[pl.BlockSpec((TOP_K, tile), lambda i, *_: (0, prompt_blk(i))),
                      pl.BlockSpec((tile, D_MODEL), lambda i, *_: (prompt_blk(i), 0)),
                      pl.BlockSpec((TOP_K, ns), lambda i, *_: (0, 0)),
                      pl.BlockSpec((ns, D_MODEL), lambda i, *_: (0, 0))],
            out_specs=pl.BlockSpec(memory_space=pl.ANY),
            scratch_shapes=[pltpu.VMEM((2, loc, D_MODEL), F32), pltpu.VMEM((MOE_ROWS, D_MODEL), F32),
                            pltpu.SemaphoreType.DMA((2,)), pltpu.SemaphoreType.DMA(())]),
        out_shape=jax.ShapeDtypeStruct((nrows, D_MODEL), F32),
        compiler_params=_cparams(("arbitrary",), VMEM_LIMIT),
        name="dispatch",
    )(tab, seg_start, tot, tails, lpos_p, x1_p, lpos_s, x1_s)


def _deinterleave_matrix():
    pm = np.zeros((MXU_DIM, MXU_DIM), np.float32)
    half = MXU_DIM // 2
    for c in range(half):
        pm[2 * c, c] = 1.0
        pm[2 * c + 1, half + c] = 1.0
    return pm


def _expert_kernel(be_ref, nu_ref, nv_ref, ord_ref, nxt_ref, xs_ref, w1_hbm, b1_ref, w2_hbm, b2_ref, pm_ref, y_ref,
                   w1f_sc, w2f_sc, w1p_sc, w2b_sc, sem):
    del nu_ref
    i = pl.program_id(0)
    e = be_ref[i]
    prev = be_ref[jnp.maximum(i - 1, 0)]
    nblk = 2 * D_FF // MXU_DIM

    def weight_copies(expert, slot):
        return (pltpu.make_async_copy(w1_hbm.at[expert], w1f_sc.at[slot], sem.at[0, slot]),
                pltpu.make_async_copy(w2_hbm.at[expert], w2f_sc.at[slot], sem.at[1, slot]))

    @pl.when(i == 0)
    def _():
        for cp in weight_copies(e, 0):
            cp.start()

    @pl.when((i == 0) | (e != prev))
    def _():
        slot = ord_ref[i] % 2
        for cp in weight_copies(e, slot):
            cp.wait()

        for cb in range(nblk):
            blk = w1f_sc[slot, :, cb * MXU_DIM:(cb + 1) * MXU_DIM].astype(BF16)
            w1p_sc[:, cb * MXU_DIM:(cb + 1) * MXU_DIM] = jnp.dot(
                blk, pm_ref[...], preferred_element_type=F32).astype(BF16)
        w2b_sc[...] = w2f_sc[slot].astype(BF16)

        nxt = nxt_ref[i]

        @pl.when(nxt >= 0)
        def _():
            for cp in weight_copies(nxt, 1 - slot):
                cp.start()

    for blk in range(MOE_STEP_BLOCKS):
        rows = slice(blk * MOE_ROWS, (blk + 1) * MOE_ROWS)

        @pl.when(blk < nv_ref[i])
        def _():
            x = xs_ref[rows, :].astype(BF16)
            h = jnp.dot(x, w1p_sc[...], preferred_element_type=F32) + b1_ref[0]
            half = MXU_DIM // 2
            acts = []
            for cb in range(nblk):
                x_glu = jnp.minimum(h[:, cb * MXU_DIM:cb * MXU_DIM + half], SWIGLU_LIMIT)
                x_lin = jnp.clip(h[:, cb * MXU_DIM + half:(cb + 1) * MXU_DIM], -SWIGLU_LIMIT, SWIGLU_LIMIT)
                acts.append((x_glu * jax.nn.sigmoid(SWIGLU_ALPHA * x_glu) * (x_lin + 1.0)).astype(BF16))
            act = jnp.concatenate(acts, axis=1)
            y_ref[rows, :] = jnp.dot(act, w2b_sc[...], preferred_element_type=F32) + b2_ref[0]

        @pl.when(blk >= nv_ref[i])
        def _():
            y_ref[rows, :] = jnp.zeros((MOE_ROWS, D_MODEL), F32)


def _experts(block_e, n_used, n_valid, run_ord, run_next, xs, w1, b1p, w2, b2, pm):
    nrows = xs.shape[0]
    step_rows = MOE_STEP_BLOCKS * MOE_ROWS
    nb = nrows // step_rows
    return pl.pallas_call(
        _expert_kernel,
        grid_spec=pltpu.PrefetchScalarGridSpec(
            num_scalar_prefetch=5,
            grid=(nb,),
            in_specs=[pl.BlockSpec((step_rows, D_MODEL), lambda i, be, nu, *_: (jnp.minimum(i, nu[0] - 1), 0)),
                      pl.BlockSpec(memory_space=pl.ANY),
                      pl.BlockSpec((1, 1, 2 * D_FF), lambda i, be, *_: (be[i], 0, 0)),
                      pl.BlockSpec(memory_space=pl.ANY),
                      pl.BlockSpec((1, 1, D_MODEL), lambda i, be, *_: (be[i], 0, 0)),
                      pl.BlockSpec((MXU_DIM, MXU_DIM), lambda i, *_: (0, 0))],
            out_specs=pl.BlockSpec((step_rows, D_MODEL), lambda i, *_: (i, 0)),
            scratch_shapes=[pltpu.VMEM((2, D_MODEL, 2 * D_FF), F32), pltpu.VMEM((2, D_FF, D_MODEL), F32),
                            pltpu.VMEM((D_MODEL, 2 * D_FF), BF16), pltpu.VMEM((D_FF, D_MODEL), BF16),
                            pltpu.SemaphoreType.DMA((2, 2))]),
        out_shape=jax.ShapeDtypeStruct((nrows, D_MODEL), F32),
        compiler_params=_cparams(("arbitrary",), VMEM_LIMIT_LARGE),
        name="experts",
    )(block_e, n_used, n_valid, run_ord, run_next, xs, w1, b1p, w2, b2, pm)


def _combine_kernel(tab_ref, seg_ref, tot_ref, cols_ref, x1_ref, g2_ref, b2_ref, ys_ref, y_ref, loc_sc, sem,
                    *, tile_base):
    i = pl.program_id(0)
    last = pl.num_programs(0) - 1
    tile = i + tile_base
    slot = i % 2
    loc, tm = loc_sc.shape[1], x1_ref.shape[0]

    def piece_copy(sl, lrow, grow, e, n):
        src = pl.multiple_of(seg_ref[e] + grow, RUN_ROWS)
        return pltpu.make_async_copy(ys_ref.at[pl.ds(src, n)], loc_sc.at[sl, pl.ds(lrow, n)], sem.at[sl])

    def gather(tl, sl):
        _for_each_run_piece(tab_ref, tl, lambda lrow, grow, e, n: piece_copy(sl, lrow, grow, e, n).start())

    @pl.when(i == 0)
    def _():
        loc_sc[...] = jnp.zeros_like(loc_sc)
        gather(tile, slot)

    @pl.when(i < last)
    def _():
        gather(tile + 1, 1 - slot)

    _drain_units(tot_ref[tile], lambda n: piece_copy(slot, 0, 0, 0, n), loc)

    cols = cols_ref[...]
    lane = lax.broadcasted_iota(jnp.int32, (tm, loc), 1)
    weights = jnp.zeros((tm, loc), F32)
    for k in range(TOP_K):
        weights = jnp.where(lane == cols[:, k:k + 1].astype(jnp.int32), cols[:, TOP_K + k:TOP_K + k + 1], weights)
    ffn = jnp.dot(weights.astype(BF16), loc_sc[slot].astype(BF16), preferred_element_type=F32)
    y_ref[...] = _layer_norm(DEEPNORM_ALPHA * x1_ref[...] + ffn, g2_ref[...], b2_ref[...])


def _combine(tab, seg_start, tot, cols, x1, g2, b2, ys, *, tile, tile_base):
    n = x1.shape[0]
    loc = tile * TOP_K + N_EXPERTS * RUN_ROWS
    return pl.pallas_call(
        functools.partial(_combine_kernel, tile_base=tile_base),
        grid_spec=pltpu.PrefetchScalarGridSpec(
            num_scalar_prefetch=3,
            grid=(n // tile,),
            in_specs=[pl.BlockSpec((tile, 2 * TOP_K), lambda i, *_: (i, 0)),
                      pl.BlockSpec((tile, D_MODEL), lambda i, *_: (i, 0)),
                      pl.BlockSpec((1, D_MODEL), lambda i, *_: (0, 0)),
                      pl.BlockSpec((1, D_MODEL), lambda i, *_: (0, 0)),
                      pl.BlockSpec(memory_space=pl.ANY)],
            out_specs=pl.BlockSpec((tile, D_MODEL), lambda i, *_: (i, 0)),
            scratch_shapes=[pltpu.VMEM((2, loc, D_MODEL), F32), pltpu.SemaphoreType.DMA((2,))]),
        out_shape=jax.ShapeDtypeStruct((n, D_MODEL), F32),
        compiler_params=_cparams(("arbitrary",), VMEM_LIMIT),
        name="combine",
    )(tab, seg_start, tot, cols, x1, g2, b2, ys)


def kernel(x_prompt, x_sample, cache_k_win, cache_v_win, state_ssm_re, state_ssm_im, w_in, b_in, attn_sinks,
           w_attn_out, ssm_a_re, ssm_a_im, ssm_log_dt, ssm_b_re, ssm_b_im, ssm_c_re, ssm_c_im, ssm_d, w_ssm_out,
           w_gate, b_gate, w_out, ln1_g, ln1_b, w_router, b_router, w_exp1, b_exp1, w_exp2, b_exp2, ln2_g, ln2_b):
    assert w_in.shape[0] == DEPTH == 1
    bsz, seq, _ = x_prompt.shape
    nsamp = x_sample.shape[0]
    assert x_sample.shape[1] == 1
    n_p = bsz * seq
    n_tok = n_p + nsamp

    xp = x_prompt.reshape(n_p, D_MODEL)
    xsm = x_sample.reshape(nsamp, D_MODEL)
    b_in2 = b_in[0].reshape(1, D_IN)
    sinks = attn_sinks[0].astype(F32)

    q_p, k_p, v_p, u_p = _proj(xp, w_in[0].astype(BF16), b_in2, tile=2048, exact_f32=False, q_dtype=BF16)
    q_s, k_s, v_s, u_s = _proj(xsm, w_in[0], b_in2, tile=nsamp, exact_f32=True, q_dtype=F32)

    o_p = _attn_prompt(sinks, q_p.reshape(bsz, seq, D_ATTN), k_p.reshape(bsz, seq, D_KV),
                       v_p.reshape(bsz, seq, D_KV)).reshape(n_p, D_ATTN)
    k_buf = cache_k_win[0].reshape(nsamp, WINDOW, D_KV)
    v_buf = cache_v_win[0].reshape(nsamp, WINDOW, D_KV)
    o_s, k_next, v_next = _attn_sample(sinks, q_s, k_s, v_s, k_buf, v_buf)

    sp = _s5_params(ssm_a_re[0], ssm_a_im[0], ssm_log_dt[0], ssm_b_re[0], ssm_b_im[0], ssm_c_re[0], ssm_c_im[0])
    y_p, hp_re, hp_im = _s5_prompt(u_p, bsz, seq, _s5_chunk_mats(sp, ssm_d[0]))
    y_s, hs_re, hs_im = _s5_sample(u_s, state_ssm_re[0].reshape(nsamp, -1), state_ssm_im[0].reshape(nsamp, -1),
                                   _s5_sample_mats(sp, ssm_d[0]))

    wm = dict(wao=w_attn_out[0].astype(BF16), wso=w_ssm_out[0].astype(BF16), wg=w_gate[0].astype(BF16),
              bg=b_gate[0].reshape(1, -1), wo=w_out[0].astype(BF16), g1=ln1_g[0].reshape(1, -1),
              b1=ln1_b[0].reshape(1, -1), wrt=w_router[0].T, brt=b_router[0].reshape(-1, 1))
    wm_f32 = dict(wm, wao=w_attn_out[0], wso=w_ssm_out[0], wg=w_gate[0], wo=w_out[0])
    carry0 = jnp.zeros((SUBLANES, LANES), F32)
    x1_p, lpos_p, cols_p, tab_p, carry1 = _merge(xp, o_p, y_p, carry0, wm, tile=MERGE_TILE, route_tile=TOK_TILE,
                                                 f32_matmuls=False)
    x1_s, lpos_s, cols_s, tab_s, carry2 = _merge(xsm, o_s, y_s, carry1, wm_f32, tile=nsamp, route_tile=nsamp,
                                                 f32_matmuls=True)

    nt_p = n_p // TOK_TILE
    tab = jnp.concatenate([tab_p[:, :TAB_ROWS, :N_EXPERTS], tab_s[:, :TAB_ROWS, :N_EXPERTS]], axis=0)
    tot = jnp.sum(tab[:, 0, :], axis=1).astype(jnp.int32)
    tab = tab.reshape(-1)
    seg_rows = carry2[0, :N_EXPERTS].astype(jnp.int32) * RUN_ROWS
    step_rows = MOE_STEP_BLOCKS * MOE_ROWS
    padded = ((seg_rows + step_rows - 1) // step_rows) * step_rows
    pad_end = jnp.cumsum(padded)
    pad_start = (pad_end - padded).astype(jnp.int32)
    seg_end = pad_start + seg_rows
    n_runs = (nt_p + 1) * N_EXPERTS
    nb_max = (n_tok * TOP_K + n_runs * (RUN_ROWS - 1) + N_EXPERTS * (step_rows - 1) + step_rows - 1) // step_rows
    n_used = (pad_end[-1] // step_rows).astype(jnp.int32)
    tails = jnp.concatenate([seg_end // RUN_ROWS, (padded - seg_rows) // RUN_ROWS,
                             (pad_end[-1:] // MOE_ROWS)]).astype(jnp.int32)
    blk_start = jnp.arange(nb_max, dtype=jnp.int32) * step_rows
    blk_e = jnp.minimum(jnp.sum(blk_start[:, None] >= pad_end[None, :], axis=1), N_EXPERTS - 1).astype(jnp.int32)
    used = jnp.arange(nb_max) < n_used
    blk_e = jnp.where(used, blk_e, jnp.max(jnp.where(used, blk_e, 0)))
    ids = jnp.arange(N_EXPERTS, dtype=jnp.int32)
    of_blk = blk_e[:, None] == ids[None, :]
    n_valid = jnp.clip((jnp.sum(jnp.where(of_blk, seg_end[None, :], 0), axis=1) - blk_start + MOE_ROWS - 1)
                       // MOE_ROWS, 0, MOE_STEP_BLOCKS)
    n_valid = jnp.where(used, n_valid, 0).astype(jnp.int32)
    new_run = jnp.concatenate([jnp.ones((1,), jnp.int32), (blk_e[1:] != blk_e[:-1]).astype(jnp.int32)])
    run_ord = (jnp.cumsum(new_run) - 1).astype(jnp.int32)
    later = (ids[None, :] > ids[:, None]) & (padded > 0)[None, :]
    next_e = jnp.min(jnp.where(later, ids[None, :], N_EXPERTS), axis=1)
    next_e = jnp.where(next_e < N_EXPERTS, next_e, -1).astype(jnp.int32)
    run_next = jnp.sum(jnp.where(of_blk, next_e[None, :], 0), axis=1).astype(jnp.int32)

    nrows = nb_max * step_rows
    xs = _dispatch(tab, pad_start, tot, tails, lpos_p, x1_p, lpos_s, x1_s, tile=TOK_TILE, nrows=nrows)

    b1p = b_exp1[0].reshape(N_EXPERTS, 2 * D_FF // MXU_DIM, MXU_DIM // 2, 2)
    b1p = jnp.swapaxes(b1p, 2, 3).reshape(N_EXPERTS, 1, 2 * D_FF)
    ys = _experts(blk_e, n_used.reshape(1), n_valid, run_ord, run_next, xs, w_exp1[0], b1p, w_exp2[0],
                  b_exp2[0].reshape(N_EXPERTS, 1, D_MODEL),
                  jnp.asarray(_deinterleave_matrix(), BF16))

    g2, b2 = ln2_g[0].reshape(1, -1), ln2_b[0].reshape(1, -1)
    y_prompt = _combine(tab, pad_start, tot, cols_p, x1_p, g2, b2, ys, tile=TOK_TILE, tile_base=0)
    y_sample = _combine(tab, pad_start, tot, cols_s, x1_s, g2, b2, ys, tile=nsamp, tile_base=nt_p)

    k_p4 = k_p.reshape(bsz, seq, D_KV)[:, -WINDOW:].reshape(bsz, WINDOW, N_KV_HEADS, HEAD_DIM)
    v_p4 = v_p.reshape(bsz, seq, D_KV)[:, -WINDOW:].reshape(bsz, WINDOW, N_KV_HEADS, HEAD_DIM)
    k_s4 = k_next.reshape(nsamp, WINDOW, N_KV_HEADS, HEAD_DIM)
    v_s4 = v_next.reshape(nsamp, WINDOW, N_KV_HEADS, HEAD_DIM)
    st = lambda a, n: a.reshape(1, n, N_SSM_GROUPS, SSM_STATE)
    return (y_prompt.reshape(bsz, seq, D_MODEL), y_sample.reshape(nsamp, 1, D_MODEL),
            k_p4[None], v_p4[None], st(hp_re, bsz), st(hp_im, bsz),
            k_s4[None], v_s4[None], st(hs_re, nsamp), st(hs_im, nsamp))
```
